```python
import jax, jax.numpy as jnp
from jax import lax
import numpy as np

D_MODEL = 1024
BATCH = 8
SEQ = 4096
DEPTH = 2

CHUNK = 64
CONV_WIDTH = 3
D_CONV = 1024
D_GMLP = 1024
GMLP_BLOCK = 128
N_GROUPS_GMLP = 8
D_POOL = 1024
POOL_WINDOWS = (2, 4, 8, 16)
POOL_GROUP = D_POOL // len(POOL_WINDOWS)
N_BRANCHES = 3
D_FF = 2816
D_IN = 3 * D_CONV + 2 * D_GMLP + D_POOL + N_BRANCHES * D_MODEL
ALPHA = (2 * DEPTH) ** 0.25
BETA = (8 * DEPTH) ** -0.25
LN_EPS = 1e-5

kernel_name = "hybrid_conv_gmlp_pool_deepnorm_adaln"


def layer_norm(x, g, b):
    xf = x.astype(jnp.float32)
    mu = jnp.mean(xf, axis=-1, keepdims=True)
    var = jnp.mean(jnp.square(xf - mu), axis=-1, keepdims=True)
    y = (xf - mu) * lax.rsqrt(var + LN_EPS)
    return (y * g.astype(jnp.float32) + b.astype(jnp.float32)).astype(x.dtype)


def causal_dwconv(x, w):
    k, ch = w.shape
    return lax.conv_general_dilated(
        x, w[:, None, :].astype(x.dtype), window_strides=(1,), padding=[(k - 1, 0)],
        dimension_numbers=("NWC", "WIO", "NWC"), feature_group_count=ch)


def spatial_gating(u, v, ln_g, ln_b, w_s, b_s):
    bn, s, _ = v.shape
    v = layer_norm(v, ln_g, ln_b)
    vb = v.reshape(bn, s // GMLP_BLOCK, GMLP_BLOCK, N_GROUPS_GMLP, D_GMLP // N_GROUPS_GMLP)
    pos = jnp.arange(GMLP_BLOCK)
    allowed = (pos[None, :] // CHUNK) <= (pos[:, None] // CHUNK)
    w = jnp.where(allowed[None], w_s, jnp.zeros_like(w_s))
    mixed = jnp.einsum("gij,bnjgc->bnigc", w, vb) + b_s.T[None, None, :, :, None]
    return u * mixed.reshape(bn, s, D_GMLP)


def multiscale_pool(p, w_pool, scale):
    s = p.shape[1]
    pf = p.astype(jnp.float32)
    cs = jnp.cumsum(pf, axis=1)
    t = jnp.arange(1, s + 1, dtype=jnp.float32)
    outs = []
    for k, win in enumerate(POOL_WINDOWS):
        lo, hi = k * POOL_GROUP, (k + 1) * POOL_GROUP
        csk = cs[..., lo:hi]
        prev = jnp.pad(csk, ((0, 0), (win, 0), (0, 0)))[:, :s]
        mean = (csk - prev) / jnp.minimum(t, float(win))[None, :, None]
        d = (mean - pf[..., lo:hi]).astype(p.dtype)
        outs.append(d @ w_pool[k])
    return jnp.concatenate(outs, axis=-1) * scale


def _fwd_setup_inputs(seed: int = 0) -> dict:
    key = jax.random.key(seed)
    ks = jax.random.split(key, 26)
    f32 = jnp.float32

    def nrm(k, shape, s):
        return jax.random.normal(k, shape, f32) * s

    L = DEPTH
    w_in_scale = D_MODEL ** -0.5
    return {
        "x": nrm(ks[0], (BATCH, SEQ, D_MODEL), 1.0),
        "c": nrm(ks[1], (BATCH, D_MODEL), 1.0),
        "w_ada": nrm(ks[2], (L, D_MODEL, 6 * D_MODEL), 0.5 * D_MODEL ** -0.5),
        "b_ada": nrm(ks[3], (L, 6 * D_MODEL), 0.02),
        "w_in": nrm(ks[4], (L, D_MODEL, D_IN), w_in_scale),
        "b_in": nrm(ks[5], (L, D_IN), 0.01),
        "conv_a": nrm(ks[6], (L, CONV_WIDTH, D_CONV), 0.5),
        "w_a_out": nrm(ks[7], (L, D_CONV, D_MODEL), D_CONV ** -0.5),
        "ln_v_g": 1.0 + nrm(ks[8], (L, D_GMLP), 0.02),
        "ln_v_b": nrm(ks[9], (L, D_GMLP), 0.02),
        "w_spatial": nrm(ks[10], (L, N_GROUPS_GMLP, GMLP_BLOCK, GMLP_BLOCK), 0.5 * GMLP_BLOCK ** -0.5),
        "b_spatial": 1.0 + nrm(ks[11], (L, N_GROUPS_GMLP, GMLP_BLOCK), 0.02),
        "w_b_out": nrm(ks[12], (L, D_GMLP, D_MODEL), D_GMLP ** -0.5),
        "w_pool": nrm(ks[13], (L, len(POOL_WINDOWS), POOL_GROUP, POOL_GROUP), POOL_GROUP ** -0.5),
        "pool_scale": 1.0 + nrm(ks[14], (L, D_POOL), 0.02),
        "w_o": nrm(ks[15], (L, D_MODEL, D_MODEL), BETA * D_MODEL ** -0.5),
        "ln1_g": 1.0 + nrm(ks[16], (L, D_MODEL), 0.02),
        "ln1_b": nrm(ks[17], (L, D_MODEL), 0.02),
        "w_up": nrm(ks[18], (L, D_MODEL, 2 * D_FF), w_in_scale),
        "b_up": nrm(ks[19], (L, 2 * D_FF), 0.01),
        "conv_ffn": nrm(ks[20], (L, CONV_WIDTH, D_FF), 0.5),
        "conv_ffn_b": nrm(ks[21], (L, D_FF), 0.01),
        "w_down": nrm(ks[22], (L, D_FF, D_MODEL), BETA * D_FF ** -0.5),
        "ln2_g": 1.0 + nrm(ks[23], (L, D_MODEL), 0.02),
        "ln2_b": nrm(ks[24], (L, D_MODEL), 0.02),
    }


def _fwd_reference(x, c, w_ada, b_ada, w_in, b_in, conv_a, w_a_out, ln_v_g, ln_v_b,
              w_spatial, b_spatial, w_b_out, w_pool, pool_scale, w_o, ln1_g, ln1_b,
              w_up, b_up, conv_ffn, conv_ffn_b, w_down, ln2_g, ln2_b):
    split_points = [D_CONV, 2 * D_CONV, 3 * D_CONV, 3 * D_CONV + D_GMLP,
                    3 * D_CONV + 2 * D_GMLP, 3 * D_CONV + 2 * D_GMLP + D_POOL]
    c_act = jax.nn.silu(c)
    for l in range(DEPTH):
        ada = (c_act @ w_ada[l] + b_ada[l])[:, None, :]
        sh1, sc1, gt1, sh2, sc2, gt2 = jnp.split(ada, 6, axis=-1)

        h = x * (1.0 + sc1) + sh1
        z = h @ w_in[l] + b_in[l]
        zb, zc, zx, zu, zv, zp, zg = jnp.split(z, split_points, axis=-1)
        y_a = (zb * causal_dwconv(zc * zx, conv_a[l])) @ w_a_out[l]
        y_b = spatial_gating(jax.nn.gelu(zu), jax.nn.gelu(zv), ln_v_g[l], ln_v_b[l],
                             w_spatial[l], b_spatial[l]) @ w_b_out[l]
        y_c = multiscale_pool(zp, w_pool[l], pool_scale[l])
        g_a, g_b, g_c = jnp.split(jax.nn.sigmoid(zg), 3, axis=-1)
        merged = g_a * y_a + g_b * y_b + g_c * y_c
        x = layer_norm(ALPHA * x + gt1 * (merged @ w_o[l]), ln1_g[l], ln1_b[l])

        h = x * (1.0 + sc2) + sh2
        up_a, up_g = jnp.split(h @ w_up[l] + b_up[l], 2, axis=-1)
        f = jax.nn.gelu(causal_dwconv(up_a, conv_ffn[l]) + conv_ffn_b[l]) * up_g
        x = layer_norm(ALPHA * x + gt2 * (f @ w_down[l]), ln2_g[l], ln2_b[l])
    return x


import jax as _jax
import jax.numpy as _jnp

TWIN_FORMAT = 'train_step'
FWD_PARAMS = ['x', 'c', 'w_ada', 'b_ada', 'w_in', 'b_in', 'conv_a', 'w_a_out', 'ln_v_g', 'ln_v_b', 'w_spatial', 'b_spatial', 'w_b_out', 'w_pool', 'pool_scale', 'w_o', 'ln1_g', 'ln1_b', 'w_up', 'b_up', 'conv_ffn', 'conv_ffn_b', 'w_down', 'ln2_g', 'ln2_b']
TWIN_WEIGHTS = ['w_ada', 'b_ada', 'w_in', 'b_in', 'conv_a', 'w_a_out', 'ln_v_g', 'ln_v_b', 'w_spatial', 'b_spatial', 'w_b_out', 'w_pool', 'pool_scale', 'w_o', 'ln1_g', 'ln1_b', 'w_up', 'b_up', 'conv_ffn', 'conv_ffn_b', 'w_down', 'ln2_g', 'ln2_b']
TWIN_DIFF_INPUT = 'x'
TWIN_INPUTS = ['x', 'c', 'w_ada', 'b_ada', 'w_in', 'b_in', 'conv_a', 'w_a_out', 'ln_v_g', 'ln_v_b', 'w_spatial', 'b_spatial', 'w_b_out', 'w_pool', 'pool_scale', 'w_o', 'ln1_g', 'ln1_b', 'w_up', 'b_up', 'conv_ffn', 'conv_ffn_b', 'w_down', 'ln2_g', 'ln2_b', 'loss_target', 'm_w_ada', 'm_b_ada', 'm_w_in', 'm_b_in', 'm_conv_a', 'm_w_a_out', 'm_ln_v_g', 'm_ln_v_b', 'm_w_spatial', 'm_b_spatial', 'm_w_b_out', 'm_w_pool', 'm_pool_scale', 'm_w_o', 'm_ln1_g', 'm_ln1_b', 'm_w_up', 'm_b_up', 'm_conv_ffn', 'm_conv_ffn_b', 'm_w_down', 'm_ln2_g', 'm_ln2_b', 'v_w_ada', 'v_b_ada', 'v_w_in', 'v_b_in', 'v_conv_a', 'v_w_a_out', 'v_ln_v_g', 'v_ln_v_b', 'v_w_spatial', 'v_b_spatial', 'v_w_b_out', 'v_w_pool', 'v_pool_scale', 'v_w_o', 'v_ln1_g', 'v_ln1_b', 'v_w_up', 'v_b_up', 'v_conv_ffn', 'v_conv_ffn_b', 'v_w_down', 'v_ln2_g', 'v_ln2_b']
TWIN_OUTPUTS = ['loss', 'grad_x', 'grad_w_ada', 'grad_b_ada', 'grad_w_in', 'grad_b_in', 'grad_conv_a', 'grad_w_a_out', 'grad_ln_v_g', 'grad_ln_v_b', 'grad_w_spatial', 'grad_b_spatial', 'grad_w_b_out', 'grad_w_pool', 'grad_pool_scale', 'grad_w_o', 'grad_ln1_g', 'grad_ln1_b', 'grad_w_up', 'grad_b_up', 'grad_conv_ffn', 'grad_conv_ffn_b', 'grad_w_down', 'grad_ln2_g', 'grad_ln2_b', 'delta_w_ada', 'delta_b_ada', 'delta_w_in', 'delta_b_in', 'delta_conv_a', 'delta_w_a_out', 'delta_ln_v_g', 'delta_ln_v_b', 'delta_w_spatial', 'delta_b_spatial', 'delta_w_b_out', 'delta_w_pool', 'delta_pool_scale', 'delta_w_o', 'delta_ln1_g', 'delta_ln1_b', 'delta_w_up', 'delta_b_up', 'delta_conv_ffn', 'delta_conv_ffn_b', 'delta_w_down', 'delta_ln2_g', 'delta_ln2_b', 'new_m_w_ada', 'new_m_b_ada', 'new_m_w_in', 'new_m_b_in', 'new_m_conv_a', 'new_m_w_a_out', 'new_m_ln_v_g', 'new_m_ln_v_b', 'new_m_w_spatial', 'new_m_b_spatial', 'new_m_w_b_out', 'new_m_w_pool', 'new_m_pool_scale', 'new_m_w_o', 'new_m_ln1_g', 'new_m_ln1_b', 'new_m_w_up', 'new_m_b_up', 'new_m_conv_ffn', 'new_m_conv_ffn_b', 'new_m_w_down', 'new_m_ln2_g', 'new_m_ln2_b', 'new_v_w_ada', 'new_v_b_ada', 'new_v_w_in', 'new_v_b_in', 'new_v_conv_a', 'new_v_w_a_out', 'new_v_ln_v_g', 'new_v_ln_v_b', 'new_v_w_spatial', 'new_v_b_spatial', 'new_v_w_b_out', 'new_v_w_pool', 'new_v_pool_scale', 'new_v_w_o', 'new_v_ln1_g', 'new_v_ln1_b', 'new_v_w_up', 'new_v_b_up', 'new_v_conv_ffn', 'new_v_conv_ffn_b', 'new_v_w_down', 'new_v_ln2_g', 'new_v_ln2_b']
TWIN_LEAF_KINDS = {'loss': 'loss', 'grad_x': 'grad_x', 'grad_w_ada': 'grad_w', 'grad_b_ada': 'grad_w', 'grad_w_in': 'grad_w', 'grad_b_in': 'grad_w', 'grad_conv_a': 'grad_w', 'grad_w_a_out': 'grad_w', 'grad_ln_v_g': 'grad_w', 'grad_ln_v_b': 'grad_w', 'grad_w_spatial': 'grad_w', 'grad_b_spatial': 'grad_w', 'grad_w_b_out': 'grad_w', 'grad_w_pool': 'grad_w', 'grad_pool_scale': 'grad_w', 'grad_w_o': 'grad_w', 'grad_ln1_g': 'grad_w', 'grad_ln1_b': 'grad_w', 'grad_w_up': 'grad_w', 'grad_b_up': 'grad_w', 'grad_conv_ffn': 'grad_w', 'grad_conv_ffn_b': 'grad_w', 'grad_w_down': 'grad_w', 'grad_ln2_g': 'grad_w', 'grad_ln2_b': 'grad_w', 'delta_w_ada': 'delta_w', 'delta_b_ada': 'delta_w', 'delta_w_in': 'delta_w', 'delta_b_in': 'delta_w', 'delta_conv_a': 'delta_w', 'delta_w_a_out': 'delta_w', 'delta_ln_v_g': 'delta_w', 'delta_ln_v_b': 'delta_w', 'delta_w_spatial': 'delta_w', 'delta_b_spatial': 'delta_w', 'delta_w_b_out': 'delta_w', 'delta_w_pool': 'delta_w', 'delta_pool_scale': 'delta_w', 'delta_w_o': 'delta_w', 'delta_ln1_g': 'delta_w', 'delta_ln1_b': 'delta_w', 'delta_w_up': 'delta_w', 'delta_b_up': 'delta_w', 'delta_conv_ffn': 'delta_w', 'delta_conv_ffn_b': 'delta_w', 'delta_w_down': 'delta_w', 'delta_ln2_g': 'delta_w', 'delta_ln2_b': 'delta_w', 'new_m_w_ada': 'new_m', 'new_m_b_ada': 'new_m', 'new_m_w_in': 'new_m', 'new_m_b_in': 'new_m', 'new_m_conv_a': 'new_m', 'new_m_w_a_out': 'new_m', 'new_m_ln_v_g': 'new_m', 'new_m_ln_v_b': 'new_m', 'new_m_w_spatial': 'new_m', 'new_m_b_spatial': 'new_m', 'new_m_w_b_out': 'new_m', 'new_m_w_pool': 'new_m', 'new_m_pool_scale': 'new_m', 'new_m_w_o': 'new_m', 'new_m_ln1_g': 'new_m', 'new_m_ln1_b': 'new_m', 'new_m_w_up': 'new_m', 'new_m_b_up': 'new_m', 'new_m_conv_ffn': 'new_m', 'new_m_conv_ffn_b': 'new_m', 'new_m_w_down': 'new_m', 'new_m_ln2_g': 'new_m', 'new_m_ln2_b': 'new_m', 'new_v_w_ada': 'new_v', 'new_v_b_ada': 'new_v', 'new_v_w_in': 'new_v', 'new_v_b_in': 'new_v', 'new_v_conv_a': 'new_v', 'new_v_w_a_out': 'new_v', 'new_v_ln_v_g': 'new_v', 'new_v_ln_v_b': 'new_v', 'new_v_w_spatial': 'new_v', 'new_v_b_spatial': 'new_v', 'new_v_w_b_out': 'new_v', 'new_v_w_pool': 'new_v', 'new_v_pool_scale': 'new_v', 'new_v_w_o': 'new_v', 'new_v_ln1_g': 'new_v', 'new_v_ln1_b': 'new_v', 'new_v_w_up': 'new_v', 'new_v_b_up': 'new_v', 'new_v_conv_ffn': 'new_v', 'new_v_conv_ffn_b': 'new_v', 'new_v_w_down': 'new_v', 'new_v_ln2_g': 'new_v', 'new_v_ln2_b': 'new_v'}


def _forward(args):
    return _fwd_reference(*[args[k] for k in FWD_PARAMS])


def _output_shape():
    def fwd():
        inp = _fwd_setup_inputs(0)
        return _fwd_reference(*[inp[k] for k in FWD_PARAMS])
    out = _jax.eval_shape(fwd)
    return out.shape, out.dtype

N_MICROBATCH = 1
ADAM_LR = 0.001
ADAM_B1 = 0.9
ADAM_B2 = 0.999
ADAM_EPS = 1e-08
ADAM_WD = 0.01
ADAM_STEP = 10
PER_EXAMPLE_BATCH_AXIS = {'x': 0, 'c': 0, 'loss_target': 0}
SHARED_INPUTS = []
_WEIGHT_DTYPES = {'w_ada': _jnp.float32, 'b_ada': _jnp.float32, 'w_in': _jnp.float32, 'b_in': _jnp.float32, 'conv_a': _jnp.float32, 'w_a_out': _jnp.float32, 'ln_v_g': _jnp.float32, 'ln_v_b': _jnp.float32, 'w_spatial': _jnp.float32, 'b_spatial': _jnp.float32, 'w_b_out': _jnp.float32, 'w_pool': _jnp.float32, 'pool_scale': _jnp.float32, 'w_o': _jnp.float32, 'ln1_g': _jnp.float32, 'ln1_b': _jnp.float32, 'w_up': _jnp.float32, 'b_up': _jnp.float32, 'conv_ffn': _jnp.float32, 'conv_ffn_b': _jnp.float32, 'w_down': _jnp.float32, 'ln2_g': _jnp.float32, 'ln2_b': _jnp.float32}
MOMENT_SCALE = {'w_ada': 2.039412e-02, 'b_ada': 3.516200e-02, 'w_in': 8.453433e-03, 'b_in': 7.404279e-03, 'conv_a': 1.351756e-02, 'w_a_out': 1.171017e-02, 'ln_v_g': 3.312780e-03, 'ln_v_b': 3.191338e-03, 'w_spatial': 6.477592e-03, 'b_spatial': 7.562147e-03, 'w_b_out': 9.091347e-03, 'w_pool': 9.586785e-03, 'pool_scale': 9.756852e-03, 'w_o': 3.529475e-02, 'ln1_g': 1.178191e+00, 'ln1_b': 5.794743e-01, 'w_up': 7.904818e-03, 'b_up': 7.934066e-03, 'conv_ffn': 9.305587e-03, 'conv_ffn_b': 8.286220e-03, 'w_down': 2.588947e-02, 'ln2_g': 2.267394e+01, 'ln2_b': 9.879609e-01}


def _to_microbatches(a, axis):
    t = _jnp.moveaxis(a, axis, 0)
    t = t.reshape((N_MICROBATCH, t.shape[0] // N_MICROBATCH) + t.shape[1:])
    return _jnp.moveaxis(t, 1, axis + 1)


def setup_inputs(seed: int = 0) -> dict:
    inp = _fwd_setup_inputs(seed)
    key = _jax.random.fold_in(_jax.random.key(seed), 7919)
    shape, _ = _output_shape()
    out = dict(inp)
    out["loss_target"] = _jax.random.normal(_jax.random.fold_in(key, 0), shape, _jnp.float32)
    for i, name in enumerate(TWIN_WEIGHTS):
        w = inp[name].astype(_jnp.float32)
        if MOMENT_SCALE is None:
            s = _jnp.sqrt(_jnp.mean(_jnp.square(w)) + 1e-30)
        else:
            s = MOMENT_SCALE[name]
        km, kv = _jax.random.split(_jax.random.fold_in(key, i + 1))
        out[name] = w
        out["m_" + name] = s * _jax.random.normal(km, w.shape, _jnp.float32)
        out["v_" + name] = (s * s) * _jax.random.uniform(kv, w.shape, _jnp.float32, 0.5, 1.5)
    if N_MICROBATCH > 1:
        for name, axis in PER_EXAMPLE_BATCH_AXIS.items():
            out[name] = _to_microbatches(out[name], axis)
    return {'x': out['x'], 'c': out['c'], 'w_ada': out['w_ada'], 'b_ada': out['b_ada'], 'w_in': out['w_in'], 'b_in': out['b_in'], 'conv_a': out['conv_a'], 'w_a_out': out['w_a_out'], 'ln_v_g': out['ln_v_g'], 'ln_v_b': out['ln_v_b'], 'w_spatial': out['w_spatial'], 'b_spatial': out['b_spatial'], 'w_b_out': out['w_b_out'], 'w_pool': out['w_pool'], 'pool_scale': out['pool_scale'], 'w_o': out['w_o'], 'ln1_g': out['ln1_g'], 'ln1_b': out['ln1_b'], 'w_up': out['w_up'], 'b_up': out['b_up'], 'conv_ffn': out['conv_ffn'], 'conv_ffn_b': out['conv_ffn_b'], 'w_down': out['w_down'], 'ln2_g': out['ln2_g'], 'ln2_b': out['ln2_b'], 'loss_target': out['loss_target'], 'm_w_ada': out['m_w_ada'], 'm_b_ada': out['m_b_ada'], 'm_w_in': out['m_w_in'], 'm_b_in': out['m_b_in'], 'm_conv_a': out['m_conv_a'], 'm_w_a_out': out['m_w_a_out'], 'm_ln_v_g': out['m_ln_v_g'], 'm_ln_v_b': out['m_ln_v_b'], 'm_w_spatial': out['m_w_spatial'], 'm_b_spatial': out['m_b_spatial'], 'm_w_b_out': out['m_w_b_out'], 'm_w_pool': out['m_w_pool'], 'm_pool_scale': out['m_pool_scale'], 'm_w_o': out['m_w_o'], 'm_ln1_g': out['m_ln1_g'], 'm_ln1_b': out['m_ln1_b'], 'm_w_up': out['m_w_up'], 'm_b_up': out['m_b_up'], 'm_conv_ffn': out['m_conv_ffn'], 'm_conv_ffn_b': out['m_conv_ffn_b'], 'm_w_down': out['m_w_down'], 'm_ln2_g': out['m_ln2_g'], 'm_ln2_b': out['m_ln2_b'], 'v_w_ada': out['v_w_ada'], 'v_b_ada': out['v_b_ada'], 'v_w_in': out['v_w_in'], 'v_b_in': out['v_b_in'], 'v_conv_a': out['v_conv_a'], 'v_w_a_out': out['v_w_a_out'], 'v_ln_v_g': out['v_ln_v_g'], 'v_ln_v_b': out['v_ln_v_b'], 'v_w_spatial': out['v_w_spatial'], 'v_b_spatial': out['v_b_spatial'], 'v_w_b_out': out['v_w_b_out'], 'v_w_pool': out['v_w_pool'], 'v_pool_scale': out['v_pool_scale'], 'v_w_o': out['v_w_o'], 'v_ln1_g': out['v_ln1_g'], 'v_ln1_b': out['v_ln1_b'], 'v_w_up': out['v_w_up'], 'v_b_up': out['v_b_up'], 'v_conv_ffn': out['v_conv_ffn'], 'v_conv_ffn_b': out['v_conv_ffn_b'], 'v_w_down': out['v_w_down'], 'v_ln2_g': out['v_ln2_g'], 'v_ln2_b': out['v_ln2_b']}


def _loss(weights, diff, rest, loss_target):
    with _jax.named_scope("forward"):
        args = {**rest, TWIN_DIFF_INPUT: diff, **{k: w.astype(_WEIGHT_DTYPES[k]) for k, w in weights.items()}}
        y = _forward(args)
    with _jax.named_scope("loss_head"):
        err = _jnp.square(y.astype(_jnp.float32) - loss_target)
        return 0.5 * _jnp.sum(_jnp.mean(err, axis=-1)) if err.ndim else 0.5 * err


def _adamw(w, g, m, v):
    m = ADAM_B1 * m + (1.0 - ADAM_B1) * g
    v = ADAM_B2 * v + (1.0 - ADAM_B2) * _jnp.square(g)
    m_hat = m / (1.0 - ADAM_B1 ** ADAM_STEP)
    v_hat = v / (1.0 - ADAM_B2 ** ADAM_STEP)
    delta = -ADAM_LR * (m_hat / (_jnp.sqrt(v_hat) + ADAM_EPS) + ADAM_WD * w)
    return delta, m, v


def reference(x, c, w_ada, b_ada, w_in, b_in, conv_a, w_a_out, ln_v_g, ln_v_b, w_spatial, b_spatial, w_b_out, w_pool, pool_scale, w_o, ln1_g, ln1_b, w_up, b_up, conv_ffn, conv_ffn_b, w_down, ln2_g, ln2_b, loss_target, m_w_ada, m_b_ada, m_w_in, m_b_in, m_conv_a, m_w_a_out, m_ln_v_g, m_ln_v_b, m_w_spatial, m_b_spatial, m_w_b_out, m_w_pool, m_pool_scale, m_w_o, m_ln1_g, m_ln1_b, m_w_up, m_b_up, m_conv_ffn, m_conv_ffn_b, m_w_down, m_ln2_g, m_ln2_b, v_w_ada, v_b_ada, v_w_in, v_b_in, v_conv_a, v_w_a_out, v_ln_v_g, v_ln_v_b, v_w_spatial, v_b_spatial, v_w_b_out, v_w_pool, v_pool_scale, v_w_o, v_ln1_g, v_ln1_b, v_w_up, v_b_up, v_conv_ffn, v_conv_ffn_b, v_w_down, v_ln2_g, v_ln2_b):
    given = dict(x=x, c=c, w_ada=w_ada, b_ada=b_ada, w_in=w_in, b_in=b_in, conv_a=conv_a, w_a_out=w_a_out, ln_v_g=ln_v_g, ln_v_b=ln_v_b, w_spatial=w_spatial, b_spatial=b_spatial, w_b_out=w_b_out, w_pool=w_pool, pool_scale=pool_scale, w_o=w_o, ln1_g=ln1_g, ln1_b=ln1_b, w_up=w_up, b_up=b_up, conv_ffn=conv_ffn, conv_ffn_b=conv_ffn_b, w_down=w_down, ln2_g=ln2_g, ln2_b=ln2_b, loss_target=loss_target, m_w_ada=m_w_ada, m_b_ada=m_b_ada, m_w_in=m_w_in, m_b_in=m_b_in, m_conv_a=m_conv_a, m_w_a_out=m_w_a_out, m_ln_v_g=m_ln_v_g, m_ln_v_b=m_ln_v_b, m_w_spatial=m_w_spatial, m_b_spatial=m_b_spatial, m_w_b_out=m_w_b_out, m_w_pool=m_w_pool, m_pool_scale=m_pool_scale, m_w_o=m_w_o, m_ln1_g=m_ln1_g, m_ln1_b=m_ln1_b, m_w_up=m_w_up, m_b_up=m_b_up, m_conv_ffn=m_conv_ffn, m_conv_ffn_b=m_conv_ffn_b, m_w_down=m_w_down, m_ln2_g=m_ln2_g, m_ln2_b=m_ln2_b, v_w_ada=v_w_ada, v_b_ada=v_b_ada, v_w_in=v_w_in, v_b_in=v_b_in, v_conv_a=v_conv_a, v_w_a_out=v_w_a_out, v_ln_v_g=v_ln_v_g, v_ln_v_b=v_ln_v_b, v_w_spatial=v_w_spatial, v_b_spatial=v_b_spatial, v_w_b_out=v_w_b_out, v_w_pool=v_w_pool, v_pool_scale=v_pool_scale, v_w_o=v_w_o, v_ln1_g=v_ln1_g, v_ln1_b=v_ln1_b, v_w_up=v_w_up, v_b_up=v_b_up, v_conv_ffn=v_conv_ffn, v_conv_ffn_b=v_conv_ffn_b, v_w_down=v_w_down, v_ln2_g=v_ln2_g, v_ln2_b=v_ln2_b)
    weights = {n: given[n] for n in TWIN_WEIGHTS}
    shared = {n: given[n] for n in SHARED_INPUTS}
    per_example = {n: given[n] for n in ['x', 'c']}
    grad_fn = _jax.value_and_grad(_loss, argnums=(0, 1))

    def one_microbatch(ex, loss_target):
        ex = dict(ex)
        diff = ex.pop(TWIN_DIFF_INPUT)
        return grad_fn(weights, diff, {**shared, **ex}, loss_target)

    if N_MICROBATCH == 1:
        loss, (grad_w, grad_x) = one_microbatch(per_example, given["loss_target"])
    else:
        def body(carry, xs):
            loss_sum, grad_sum = carry
            l_k, (gw_k, gx_k) = one_microbatch(xs[0], xs[1])
            with _jax.named_scope("update"):
                return (loss_sum + l_k, _jax.tree.map(_jnp.add, grad_sum, gw_k)), gx_k

        init = (_jnp.zeros((), _jnp.float32), _jax.tree.map(_jnp.zeros_like, weights))
        (loss, grad_w), grad_x = _jax.lax.scan(body, init, (per_example, given["loss_target"]))
    with _jax.named_scope("update"):
        delta_w, new_m, new_v = {}, {}, {}
        for n in TWIN_WEIGHTS:
            delta_w[n], new_m[n], new_v[n] = _adamw(weights[n], grad_w[n], given["m_" + n], given["v_" + n])
    return (loss, grad_x, *[grad_w[n] for n in TWIN_WEIGHTS], *[delta_w[n] for n in TWIN_WEIGHTS],
            *[new_m[n] for n in TWIN_WEIGHTS], *[new_v[n] for n in TWIN_WEIGHTS])
```

```python
import functools

import jax
import jax.numpy as jnp
from jax import lax
from jax.experimental import pallas as pl
from jax.experimental.pallas import tpu as pltpu

F32 = jnp.float32
BF = jnp.bfloat16
MESH = pl.DeviceIdType.MESH

LN_EPS = 1e-5
POOL_WINDOWS = (2, 4, 8, 16)
GMLP_BLOCK = 128
CHUNK = 64
HALO = 16
ADAM_LR, ADAM_B1, ADAM_B2, ADAM_EPS, ADAM_WD, ADAM_STEP = 0.001, 0.9, 0.999, 1e-08, 0.01, 10
N_DEV = 8
VMEM_LIMIT = 56 * 1024 * 1024

NN = ((1,), (0,))
NT = ((1,), (1,))
TN = ((0,), (0,))


def _params(sem=None, vmem=VMEM_LIMIT, **kw):
    if sem is not None:
        kw["dimension_semantics"] = sem
    return pltpu.CompilerParams(vmem_limit_bytes=vmem, **kw)


def _gelu_parts(x):
    k = 0.7978845608028654
    x2 = x * x
    t = jnp.tanh(k * (x + 0.044715 * (x2 * x)))
    cdf = 0.5 * (1.0 + t)
    dcdf = 0.5 * (1.0 - t * t) * (k * (1.0 + 3.0 * 0.044715 * x2))
    return x * cdf, cdf + x * dcdf


def _gelu(x):
    t = jnp.tanh(0.7978845608028654 * (x + 0.044715 * (x * x * x)))
    return x * (0.5 * (1.0 + t))


def _rowsum(v):
    return jnp.sum(v, axis=0, keepdims=True)


def _ln_stats(r):
    mu = jnp.mean(r, axis=-1, keepdims=True)
    xc = r - mu
    var = jnp.mean(xc * xc, axis=-1, keepdims=True)
    rstd = lax.rsqrt(var + LN_EPS)
    return xc * rstd, rstd


def _ln_bwd(dy, xhat, rstd, gain):
    dxh = dy * gain
    m1 = jnp.mean(dxh, axis=-1, keepdims=True)
    m2 = jnp.mean(dxh * xhat, axis=-1, keepdims=True)
    return rstd * (dxh - m1 - xhat * m2)


def _matmul(a, b, *, dn, grid, a_spec, b_spec, o_spec, out_shape, acc_shape, name):
    nk = grid[2]
    direct = out_shape.dtype == F32

    def body(a_ref, b_ref, o_ref, *scratch):
        prod = lax.dot_general(a_ref[...], b_ref[...], (dn, ((), ())), preferred_element_type=F32)
        if nk == 1:
            o_ref[...] = prod.astype(o_ref.dtype)
            return
        acc = o_ref if direct else scratch[0]
        k = pl.program_id(2)

        @pl.when(k == 0)
        def _():
            acc[...] = prod

        @pl.when(k > 0)
        def _():
            acc[...] += prod

        if not direct:
            @pl.when(k == nk - 1)
            def _():
                o_ref[...] = acc[...].astype(o_ref.dtype)

    scratch = [] if (direct or nk == 1) else [pltpu.VMEM(acc_shape, F32)]
    return pl.pallas_call(
        body, name=name, grid=grid, in_specs=[a_spec, b_spec], out_specs=o_spec, out_shape=out_shape,
        scratch_shapes=scratch, compiler_params=_params(("parallel", "parallel", "arbitrary")),
    )(a, b)


def _row_tile(m, want):
    t = min(m, want)
    assert m % t == 0
    return t


def _mm_rows(a, w, *, dn, name, out_dtype=F32, tm=512):
    m, k = a.shape
    n = w.shape[1] if dn == NN else w.shape[0]
    tm = _row_tile(m, tm)
    return _matmul(
        a, w, dn=dn, grid=(m // tm, 1, 1), name=name,
        a_spec=pl.BlockSpec((tm, k), lambda i, j, kk: (i, 0)),
        b_spec=pl.BlockSpec(w.shape, lambda i, j, kk: (0, 0)),
        o_spec=pl.BlockSpec((tm, n), lambda i, j, kk: (i, 0)),
        out_shape=jax.ShapeDtypeStruct((m, n), out_dtype), acc_shape=(tm, n))


def _mm_tn(a, b, *, name, tk=512):
    m, ka = a.shape
    n = b.shape[1]
    tk = _row_tile(m, tk)
    return _matmul(
        a, b, dn=TN, grid=(1, 1, m // tk), name=name,
        a_spec=pl.BlockSpec((tk, ka), lambda i, j, kk: (kk, 0)),
        b_spec=pl.BlockSpec((tk, n), lambda i, j, kk: (kk, 0)),
        o_spec=pl.BlockSpec((ka, n), lambda i, j, kk: (0, 0)),
        out_shape=jax.ShapeDtypeStruct((ka, n), F32), acc_shape=(ka, n))


def _mod_matmul(x, mod, w8, bias8, *, flat_out, name, tm=512):
    m, k = x.shape
    nb, _, ns = w8.shape
    tm = _row_tile(m, tm)

    def body(x_ref, mod_ref, w_ref, b_ref, o_ref, h_ref, hs):
        @pl.when(pl.program_id(1) == 0)
        def _():
            h = (x_ref[...] * mod_ref[0:1, :] + mod_ref[1:2, :]).astype(BF)
            hs[...] = h
            h_ref[...] = h

        o_ref[...] = jnp.dot(hs[...], w_ref[...], preferred_element_type=F32) + b_ref[...]

    if flat_out:
        o_spec = pl.BlockSpec((tm, ns), lambda i, j: (i, j))
        o_shape = jax.ShapeDtypeStruct((m, nb * ns), F32)
    else:
        o_spec = pl.BlockSpec((None, tm, ns), lambda i, j: (j, i, 0))
        o_shape = jax.ShapeDtypeStruct((nb, m, ns), F32)
    return pl.pallas_call(
        body, name=name, grid=(m // tm, nb),
        in_specs=[pl.BlockSpec((tm, k), lambda i, j: (i, 0)),
                  pl.BlockSpec((2, k), lambda i, j: (0, 0)),
                  pl.BlockSpec((None, k, ns), lambda i, j: (j, 0, 0)),
                  pl.BlockSpec((None, 1, ns), lambda i, j: (j, 0, 0))],
        out_specs=[o_spec, pl.BlockSpec((tm, k), lambda i, j: (i, 0))],
        out_shape=[o_shape, jax.ShapeDtypeStruct((m, k), BF)],
        scratch_shapes=[pltpu.VMEM((tm, k), BF)],
        compiler_params=_params(("parallel", "arbitrary")),
    )(x, mod, w8, bias8)


def _seg_spec(tm, d, s):
    return pl.BlockSpec((tm, d), lambda i, s=s: (i, s))


def _prev_halo_spec(tm, d, s):
    hb = tm // HALO
    return pl.BlockSpec((HALO, d), lambda i, s=s: (jnp.maximum(i * hb - 1, 0), s))


def _next_halo_spec(tm, d, s, m):
    hb = tm // HALO
    last = m // HALO - 1
    return pl.BlockSpec((HALO, d), lambda i, s=s: (jnp.minimum((i + 1) * hb, last), s))


def _spatial_mix(wm_ref, src, dst, bias_ref, tm, d):
    for n in range(tm // GMLP_BLOCK):
        for g in range(d // GMLP_BLOCK):
            rs = slice(n * GMLP_BLOCK, (n + 1) * GMLP_BLOCK)
            cs = slice(g * GMLP_BLOCK, (g + 1) * GMLP_BLOCK)
            v = jnp.dot(wm_ref[g], src[rs, cs], preferred_element_type=F32)
            if bias_ref is not None:
                v = v + bias_ref[:, cs]
            dst[rs, cs] = v


def _mix_fwd(z, conv_a, lnv, wm, bias_full, *, name, tm=256):
    m, d9 = z.shape
    d = d9 // 9
    tm = _row_tile(m, tm)
    grp = d // len(POOL_WINDOWS)

    def body(zb, zc, zx, zu, zv, zp, zc_h, zx_h, zp_h, ca_ref, lnv_ref, wm_ref, bias_ref,
             ua_ref, ub_ref, d_ref, ext, vn_s, mixed_s):
        i = pl.program_id(0)
        first = i == 0
        pa = zc[...] * zx[...]
        ext[0:HALO, :] = jnp.where(first, 0.0, zc_h[...] * zx_h[...])
        ext[HALO:HALO + tm, :] = pa
        w = ca_ref[...]
        conv = w[0:1, :] * ext[pl.ds(HALO - 2, tm), :] + w[1:2, :] * ext[pl.ds(HALO - 1, tm), :] + w[2:3, :] * pa
        ua_ref[...] = (zb[...] * conv).astype(BF)
        p = zp[...]
        ext[0:HALO, :] = jnp.where(first, 0.0, zp_h[...])
        ext[HALO:HALO + tm, :] = p
        t = (i * tm + lax.broadcasted_iota(jnp.int32, (tm, 1), 0) + 1).astype(F32)
        for k, win in enumerate(POOL_WINDOWS):
            cs = slice(k * grp, (k + 1) * grp)
            s = p[:, cs]
            for j in range(1, win):
                s = s + ext[pl.ds(HALO - j, tm), cs]
            d_ref[:, cs] = (s / jnp.minimum(t, float(win)) - p[:, cs]).astype(BF)
        gv = _gelu(zv[...])
        vhat, _ = _ln_stats(gv)
        vn_s[...] = (vhat * lnv_ref[0:1, :] + lnv_ref[1:2, :]).astype(BF)
        _spatial_mix(wm_ref, vn_s, mixed_s, bias_ref, tm, d)
        ub_ref[...] = (_gelu(zu[...]) * mixed_s[...]).astype(BF)

    full = lambda a: pl.BlockSpec(a.shape, lambda i: (0,) * a.ndim)
    out = jax.ShapeDtypeStruct((m, d), BF)
    o_spec = pl.BlockSpec((tm, d), lambda i: (i, 0))
    return pl.pallas_call(
        body, name=name, grid=(m // tm,),
        in_specs=[_seg_spec(tm, d, s) for s in range(6)] + [_prev_halo_spec(tm, d, s) for s in (1, 2, 5)]
        + [full(conv_a), full(lnv), full(wm), full(bias_full)],
        out_specs=[o_spec, o_spec, o_spec], out_shape=[out, out, out],
        scratch_shapes=[pltpu.VMEM((HALO + tm, d), F32), pltpu.VMEM((tm, d), BF), pltpu.VMEM((tm, d), F32)],
        compiler_params=_params(("arbitrary",)),
    )(z, z, z, z, z, z, z, z, z, conv_a, lnv, wm, bias_full)


def _pool_proj(dd, w_pool, *, dn, name, out_dtype=F32, tm=512):
    m, d = dd.shape
    ng, grp, _ = w_pool.shape
    tm = _row_tile(m, tm)
    return _matmul(
        dd, w_pool, dn=dn, grid=(m // tm, ng, 1), name=name,
        a_spec=pl.BlockSpec((tm, grp), lambda i, j, kk: (i, j)),
        b_spec=pl.BlockSpec((None, grp, grp), lambda i, j, kk: (j, 0, 0)),
        o_spec=pl.BlockSpec((tm, grp), lambda i, j, kk: (i, j)),
        out_shape=jax.ShapeDtypeStruct((m, d), out_dtype), acc_shape=(tm, grp))


def _merge(z, ya, yb, ycp, scale, *, name, tm=512):
    m, d = ya.shape
    tm = _row_tile(m, tm)

    def body(ga, gb, gc, ya_ref, yb_ref, yc_ref, sc_ref, o_ref):
        o_ref[...] = (jax.nn.sigmoid(ga[...]) * ya_ref[...] + jax.nn.sigmoid(gb[...]) * yb_ref[...]
                      + jax.nn.sigmoid(gc[...]) * (yc_ref[...] * sc_ref[...])).astype(BF)

    row = pl.BlockSpec((tm, d), lambda i: (i, 0))
    return pl.pallas_call(
        body, name=name, grid=(m // tm,),
        in_specs=[_seg_spec(tm, d, 6), _seg_spec(tm, d, 7), _seg_spec(tm, d, 8), row, row, row,
                  pl.BlockSpec((1, d), lambda i: (0, 0))],
        out_specs=row, out_shape=jax.ShapeDtypeStruct((m, d), BF),
        compiler_params=_params(("parallel",)),
    )(z, z, z, ya, yb, ycp, scale)


def _resid_ln(xp, ys, vec, alpha, *, name, tm=512):
    m, d = xp.shape
    tm = _row_tile(m, tm)

    def body(xp_ref, ys_ref, v_ref, o_ref):
        xhat, _ = _ln_stats(alpha * xp_ref[...] + v_ref[0:1, :] * ys_ref[...])
        o_ref[...] = xhat * v_ref[1:2, :] + v_ref[2:3, :]

    row = pl.BlockSpec((tm, d), lambda i: (i, 0))
    return pl.pallas_call(
        body, name=name, grid=(m // tm,),
        in_specs=[row, row, pl.BlockSpec(vec.shape, lambda i: (0, 0))],
        out_specs=row, out_shape=jax.ShapeDtypeStruct((m, d), F32),
        compiler_params=_params(("parallel",)),
    )(xp, ys, vec)


def _ffn_fwd(up4, cw, cb, *, name, tm=512):
    _, nj, m, fs = up4.shape
    tm = _row_tile(m, tm)
    hb = tm // HALO

    def body(up_ref, ah_ref, cw_ref, cb_ref, f_ref, ext):
        first = pl.program_id(1) == 0
        a = up_ref[0]
        ext[0:HALO, :] = jnp.where(first, 0.0, ah_ref[...])
        ext[HALO:HALO + tm, :] = a
        w = cw_ref[...]
        ca = (w[0:1, :] * ext[pl.ds(HALO - 2, tm), :] + w[1:2, :] * ext[pl.ds(HALO - 1, tm), :]
              + w[2:3, :] * a + cb_ref[...])
        f_ref[...] = (_gelu(ca) * up_ref[1]).astype(BF)

    return pl.pallas_call(
        body, name=name, grid=(nj, m // tm),
        in_specs=[pl.BlockSpec((2, None, tm, fs), lambda j, i: (0, j, i, 0)),
                  pl.BlockSpec((None, None, HALO, fs), lambda j, i: (0, j, jnp.maximum(i * hb - 1, 0), 0)),
                  pl.BlockSpec((None, 3, fs), lambda j, i: (j, 0, 0)),
                  pl.BlockSpec((None, 1, fs), lambda j, i: (j, 0, 0))],
        out_specs=pl.BlockSpec((None, tm, fs), lambda j, i: (j, i, 0)),
        out_shape=jax.ShapeDtypeStruct((nj, m, fs), BF),
        scratch_shapes=[pltpu.VMEM((HALO + tm, fs), F32)],
        compiler_params=_params(("parallel", "arbitrary")),
    )(up4, up4, cw, cb)


def _down_proj(f4, wd4, *, name, tm=512):
    nj, m, fs = f4.shape
    d = wd4.shape[2]
    tm = _row_tile(m, tm)
    return _matmul(
        f4, wd4, dn=NN, grid=(m // tm, 1, nj), name=name,
        a_spec=pl.BlockSpec((None, tm, fs), lambda i, j, kk: (kk, i, 0)),
        b_spec=pl.BlockSpec((None, fs, d), lambda i, j, kk: (kk, 0, 0)),
        o_spec=pl.BlockSpec((tm, d), lambda i, j, kk: (i, 0)),
        out_shape=jax.ShapeDtypeStruct((m, d), F32), acc_shape=(tm, d))


def _loss_grad(y, tgt, *, name, tm=512):
    m, d = y.shape
    tm = _row_tile(m, tm)
    ni = m // tm

    def body(y_ref, t_ref, dy_ref, l_ref, acc):
        i = pl.program_id(0)
        e = y_ref[...] - t_ref[...]
        dy_ref[...] = e * (1.0 / d)
        part = jnp.sum((e * e).reshape(tm // 8, 8, d), axis=0)

        @pl.when(i == 0)
        def _():
            acc[...] = part

        @pl.when(i > 0)
        def _():
            acc[...] += part

        @pl.when(i == ni - 1)
        def _():
            l_ref[...] = jnp.full((8, 128), 0.5 / d, F32) * jnp.sum(acc[...])

    row = pl.BlockSpec((tm, d), lambda i: (i, 0))
    return pl.pallas_call(
        body, name=name, grid=(ni,), in_specs=[row, row],
        out_specs=[row, pl.BlockSpec((8, 128), lambda i: (0, 0))],
        out_shape=[jax.ShapeDtypeStruct((m, d), F32), jax.ShapeDtypeStruct((8, 128), F32)],
        scratch_shapes=[pltpu.VMEM((8, d), F32)],
        compiler_params=_params(("arbitrary",)),
    )(y, tgt)


def _resid_ln_bwd(dpart, dh, xmod, mvec, xp, ys, vec, alpha, *, name, tm=256):
    m, d = dpart.shape
    tm = _row_tile(m, tm)
    has_dh = dh is not None
    has_ln = xp is not None

    def body(*refs):
        refs = list(refs)
        dpart_ref = refs.pop(0)
        if has_dh:
            dh_ref, xm_ref, mv_ref = refs.pop(0), refs.pop(0), refs.pop(0)
        if has_ln:
            xp_ref, ys_ref, v_ref = refs.pop(0), refs.pop(0), refs.pop(0)
            dys_ref, dxp_ref, red_ref = refs
        else:
            dx_ref, red_ref = refs
        i = pl.program_id(0)
        dtot = dpart_ref[...]
        rows = [jnp.zeros((1, d), F32)] * 5
        if has_dh:
            dhv = dh_ref[...]
            dtot = dtot + dhv * mv_ref[...]
            rows[0] = _rowsum(dhv * xm_ref[...])
            rows[1] = _rowsum(dhv)
        if has_ln:
            ys = ys_ref[...]
            gt = v_ref[0:1, :]
            xhat, rstd = _ln_stats(alpha * xp_ref[...] + gt * ys)
            rows[2] = _rowsum(dtot * xhat)
            rows[3] = _rowsum(dtot)
            dr = _ln_bwd(dtot, xhat, rstd, v_ref[1:2, :])
            rows[4] = _rowsum(dr * ys)
            dys_ref[...] = (dr * gt).astype(BF)
            dxp_ref[...] = alpha * dr
        else:
            dx_ref[...] = dtot
        red = jnp.concatenate(rows + [jnp.zeros((3, d), F32)], axis=0)

        @pl.when(i == 0)
        def _():
            red_ref[...] = red

        @pl.when(i > 0)
        def _():
            red_ref[...] += red

    row = pl.BlockSpec((tm, d), lambda i: (i, 0))
    vrow = lambda a: pl.BlockSpec(a.shape, lambda i: (0, 0))
    args, specs = [dpart], [row]
    if has_dh:
        args += [dh, xmod, mvec]
        specs += [row, row, vrow(mvec)]
    if has_ln:
        args += [xp, ys, vec]
        specs += [row, row, vrow(vec)]
        out_specs = [row, row, pl.BlockSpec((8, d), lambda i: (0, 0))]
        out_shape = [jax.ShapeDtypeStruct((m, d), BF), jax.ShapeDtypeStruct((m, d), F32),
                     jax.ShapeDtypeStruct((8, d), F32)]
    else:
        out_specs = [row, pl.BlockSpec((8, d), lambda i: (0, 0))]
        out_shape = [jax.ShapeDtypeStruct((m, d), F32), jax.ShapeDtypeStruct((8, d), F32)]
    return pl.pallas_call(
        body, name=name, grid=(m // tm,), in_specs=specs, out_specs=out_specs, out_shape=out_shape,
        compiler_params=_params(("arbitrary",)),
    )(*args)


def _down_bwd(dy, wd4, *, name, tm=512):
    m, d = dy.shape
    nj, fs, _ = wd4.shape
    tm = _row_tile(m, tm)
    return _matmul(
        dy, wd4, dn=NT, grid=(m // tm, nj, 1), name=name,
        a_spec=pl.BlockSpec((tm, d), lambda i, j, kk: (i, 0)),
        b_spec=pl.BlockSpec((None, fs, d), lambda i, j, kk: (j, 0, 0)),
        o_spec=pl.BlockSpec((None, tm, fs), lambda i, j, kk: (j, i, 0)),
        out_shape=jax.ShapeDtypeStruct((nj, m, fs), F32), acc_shape=(tm, fs))


def _tn_shards_lhs(f4, dy, *, name, tk=512):
    nj, m, fs = f4.shape
    d = dy.shape[1]
    tk = _row_tile(m, tk)
    return _matmul(
        f4, dy, dn=TN, grid=(nj, 1, m // tk), name=name,
        a_spec=pl.BlockSpec((None, tk, fs), lambda i, j, kk: (i, kk, 0)),
        b_spec=pl.BlockSpec((tk, d), lambda i, j, kk: (kk, 0)),
        o_spec=pl.BlockSpec((None, fs, d), lambda i, j, kk: (i, 0, 0)),
        out_shape=jax.ShapeDtypeStruct((nj, fs, d), F32), acc_shape=(fs, d))


def _tn_shards_rhs(h, d8, *, name, tk=512):
    m, k = h.shape
    nb, _, ns = d8.shape
    tk = _row_tile(m, tk)
    return _matmul(
        h, d8, dn=TN, grid=(nb, 1, m // tk), name=name,
        a_spec=pl.BlockSpec((tk, k), lambda i, j, kk: (kk, 0)),
        b_spec=pl.BlockSpec((None, tk, ns), lambda i, j, kk: (i, kk, 0)),
        o_spec=pl.BlockSpec((None, k, ns), lambda i, j, kk: (i, 0, 0)),
        out_shape=jax.ShapeDtypeStruct((nb, k, ns), F32), acc_shape=(k, ns))


def _tn_cols_rhs(h, dz, nb, *, name, tk=512):
    m, k = h.shape
    ns = dz.shape[1] // nb
    tk = _row_tile(m, tk)
    return _matmul(
        h, dz, dn=TN, grid=(nb, 1, m // tk), name=name,
        a_spec=pl.BlockSpec((tk, k), lambda i, j, kk: (kk, 0)),
        b_spec=pl.BlockSpec((tk, ns), lambda i, j, kk: (kk, i)),
        o_spec=pl.BlockSpec((None, k, ns), lambda i, j, kk: (i, 0, 0)),
        out_shape=jax.ShapeDtypeStruct((nb, k, ns), F32), acc_shape=(k, ns))


def _nt_shards(d8, w8, *, name, tm=512):
    nb, m, ns = d8.shape
    k = w8.shape[1]
    tm = _row_tile(m, tm)
    return _matmul(
        d8, w8, dn=NT, grid=(m // tm, 1, nb), name=name,
        a_spec=pl.BlockSpec((None, tm, ns), lambda i, j, kk: (kk, i, 0)),
        b_spec=pl.BlockSpec((None, k, ns), lambda i, j, kk: (kk, 0, 0)),
        o_spec=pl.BlockSpec((tm, k), lambda i, j, kk: (i, 0)),
        out_shape=jax.ShapeDtypeStruct((m, k), F32), acc_shape=(tm, k))


def _nt_cols(dz, w8, *, name, tm=512):
    m = dz.shape[0]
    nb, k, ns = w8.shape
    tm = _row_tile(m, tm)
    return _matmul(
        dz, w8, dn=NT, grid=(m // tm, 1, nb), name=name,
        a_spec=pl.BlockSpec((tm, ns), lambda i, j, kk: (i, kk)),
        b_spec=pl.BlockSpec((None, k, ns), lambda i, j, kk: (kk, 0, 0)),
        o_spec=pl.BlockSpec((tm, k), lambda i, j, kk: (i, 0)),
        out_shape=jax.ShapeDtypeStruct((m, k), F32), acc_shape=(tm, k))


def _tn_pool(dd, dyc, ng, *, name, tk=512):
    m, d = dd.shape
    grp = d // ng
    tk = _row_tile(m, tk)
    return _matmul(
        dd, dyc, dn=TN, grid=(ng, 1, m // tk), name=name,
        a_spec=pl.BlockSpec((tk, grp), lambda i, j, kk: (kk, i)),
        b_spec=pl.BlockSpec((tk, grp), lambda i, j, kk: (kk, i)),
        o_spec=pl.BlockSpec((None, grp, grp), lambda i, j, kk: (i, 0, 0)),
        out_shape=jax.ShapeDtypeStruct((ng, grp, grp), F32), acc_shape=(grp, grp))


def _ffn_bwd(up4, df4, cw, cb, *, name, tm=256):
    _, nj, m, fs = up4.shape
    tm = _row_tile(m, tm)
    hb = tm // HALO
    ni = m // tm
    last_hb = m // HALO - 1
    ext_rows = tm + 8

    def body(up_ref, ap_ref, un_ref, df_ref, dfn_ref, cw_ref, cb_ref, dup_ref, red_ref, ext, dca_s):
        i = pl.program_id(1)
        a = up_ref[0]
        g = up_ref[1]
        df = df_ref[...]
        ext[0:HALO, :] = jnp.where(i == 0, 0.0, ap_ref[...])
        ext[HALO:HALO + tm, :] = a
        ext[HALO + tm:2 * HALO + tm, :] = un_ref[0]
        w = cw_ref[...]
        w0, w1, w2 = w[0:1, :], w[1:2, :], w[2:3, :]
        a1 = ext[pl.ds(HALO - 1, ext_rows), :]
        a2 = ext[pl.ds(HALO - 2, ext_rows), :]
        cae = w0 * a2 + w1 * a1 + w2 * ext[pl.ds(HALO, ext_rows), :] + cb_ref[...]
        act, dact = _gelu_parts(cae)
        dfe = jnp.concatenate([df, dfn_ref[0:8, :]], axis=0)
        ge = jnp.concatenate([g, un_ref[1][0:8, :]], axis=0)
        row = lax.broadcasted_iota(jnp.int32, (ext_rows, 1), 0)
        dcae = jnp.where((row < tm) | (i < ni - 1), dfe * ge * dact, 0.0)
        dca_s[...] = dcae
        dca = dcae[0:tm, :]
        dup_a = w2 * dca + w1 * dca_s[pl.ds(1, tm), :] + w0 * dca_s[pl.ds(2, tm), :]
        dup_g = df * act[0:tm, :]
        dup_ref[0] = dup_a.astype(BF)
        dup_ref[1] = dup_g.astype(BF)
        red = jnp.concatenate([
            _rowsum(dca * a2[0:tm, :]), _rowsum(dca * a1[0:tm, :]), _rowsum(dca * a), _rowsum(dca),
            _rowsum(dup_a), _rowsum(dup_g), jnp.zeros((2, fs), F32)], axis=0)

        @pl.when(i == 0)
        def _():
            red_ref[...] = red

        @pl.when(i > 0)
        def _():
            red_ref[...] += red

    nxt = lambda j, i: jnp.minimum((i + 1) * hb, last_hb)
    return pl.pallas_call(
        body, name=name, grid=(nj, ni),
        in_specs=[pl.BlockSpec((2, None, tm, fs), lambda j, i: (0, j, i, 0)),
                  pl.BlockSpec((None, None, HALO, fs), lambda j, i: (0, j, jnp.maximum(i * hb - 1, 0), 0)),
                  pl.BlockSpec((2, None, HALO, fs), lambda j, i: (0, j, nxt(j, i), 0)),
                  pl.BlockSpec((None, tm, fs), lambda j, i: (j, i, 0)),
                  pl.BlockSpec((None, HALO, fs), lambda j, i: (j, nxt(j, i), 0)),
                  pl.BlockSpec((None, 3, fs), lambda j, i: (j, 0, 0)),
                  pl.BlockSpec((None, 1, fs), lambda j, i: (j, 0, 0))],
        out_specs=[pl.BlockSpec((2, None, tm, fs), lambda j, i: (0, j, i, 0)),
                   pl.BlockSpec((None, 8, fs), lambda j, i: (j, 0, 0))],
        out_shape=[jax.ShapeDtypeStruct((2, nj, m, fs), BF), jax.ShapeDtypeStruct((nj, 8, fs), F32)],
        scratch_shapes=[pltpu.VMEM((2 * HALO + tm, fs), F32), pltpu.VMEM((ext_rows, fs), F32)],
        compiler_params=_params(("parallel", "arbitrary")),
    )(up4, up4, up4, df4, df4, cw, cb)


def _gate_bwd(dm, z, ya, yb, ycp, scale, *, name, tm=256):
    m, d = dm.shape
    tm = _row_tile(m, tm)

    def body(dm_ref, ga, gb, gc, ya_ref, yb_ref, yc_ref, sc_ref, dya_ref, dyb_ref, dyc_ref, dz_ref, red_ref):
        i = pl.program_id(0)
        dmv = dm_ref[...]
        sa, sb, sc = jax.nn.sigmoid(ga[...]), jax.nn.sigmoid(gb[...]), jax.nn.sigmoid(gc[...])
        scale_v = sc_ref[...]
        ycp_v = yc_ref[...]
        dya_ref[...] = (dmv * sa).astype(BF)
        dyb_ref[...] = (dmv * sb).astype(BF)
        dyc = dmv * sc
        dyc_ref[...] = (dyc * scale_v).astype(BF)
        dga = dmv * ya_ref[...] * (sa * (1.0 - sa))
        dgb = dmv * yb_ref[...] * (sb * (1.0 - sb))
        dgc = dmv * (ycp_v * scale_v) * (sc * (1.0 - sc))
        dz_ref[:, 0:d] = dga.astype(BF)
        dz_ref[:, d:2 * d] = dgb.astype(BF)
        dz_ref[:, 2 * d:3 * d] = dgc.astype(BF)
        red = jnp.concatenate([_rowsum(dyc * ycp_v), _rowsum(dga), _rowsum(dgb), _rowsum(dgc),
                               jnp.zeros((4, d), F32)], axis=0)

        @pl.when(i == 0)
        def _():
            red_ref[...] = red

        @pl.when(i > 0)
        def _():
            red_ref[...] += red

    row = pl.BlockSpec((tm, d), lambda i: (i, 0))
    obf = jax.ShapeDtypeStruct((m, d), BF)
    return pl.pallas_call(
        body, name=name, grid=(m // tm,),
        in_specs=[row, _seg_spec(tm, d, 6), _seg_spec(tm, d, 7), _seg_spec(tm, d, 8), row, row, row,
                  pl.BlockSpec((1, d), lambda i: (0, 0))],
        out_specs=[row, row, row, pl.BlockSpec((tm, 3 * d), lambda i: (i, 2)), pl.BlockSpec((8, d), lambda i: (0, 0))],
        out_shape=[obf, obf, obf, jax.ShapeDtypeStruct((m, 9 * d), BF), jax.ShapeDtypeStruct((8, d), F32)],
        compiler_params=_params(("arbitrary",)),
    )(dm, z, z, z, ya, yb, ycp, scale)


def _mix_bwd(dz, dua, dub, ddd, z, conv_a, lnv, wm, wmt, bias_full, mask, *, name, tm=128):
    m, d = dua.shape
    tm = _row_tile(m, tm)
    ni = m // tm
    grp = d // len(POOL_WINDOWS)
    ng = d // GMLP_BLOCK
    ext_rows = tm + 8

    def body(dz_in, dua_ref, dub_ref, dd_ref, zb, zc, zx, zu, zv, zp, zc_h, zx_h, dua_n, zb_n, dd_n,
             ca_ref, lnv_ref, wm_ref, wmt_ref, bias_ref, mask_ref,
             dz_ref, red_ref, dws_ref, dbs_ref, ext, sh_s, vn_s, mixed_s, dmx_s, dvn_s, dbs_acc):
        del dz_in
        i = pl.program_id(0)
        rows = []
        zbv, zcv, zxv = zb[...], zc[...], zx[...]
        pa = zcv * zxv
        ext[0:HALO, :] = jnp.where(i == 0, 0.0, zc_h[...] * zx_h[...])
        ext[HALO:HALO + tm, :] = pa
        w = ca_ref[...]
        w0, w1, w2 = w[0:1, :], w[1:2, :], w[2:3, :]
        p1 = ext[pl.ds(HALO - 1, tm), :]
        p2 = ext[pl.ds(HALO - 2, tm), :]
        conv = w0 * p2 + w1 * p1 + w2 * pa
        duav = dua_ref[...]
        dzb = duav * conv
        dca = duav * zbv
        dca_n = jnp.where(i < ni - 1, dua_n[0:8, :] * zb_n[0:8, :], 0.0)
        sh_s[0:tm, :] = dca
        sh_s[tm:tm + 8, :] = dca_n
        dpa = w2 * dca + w1 * sh_s[pl.ds(1, tm), :] + w0 * sh_s[pl.ds(2, tm), :]
        dzc = dpa * zxv
        dzx = dpa * zcv
        dz_ref[:, 0:d] = dzb.astype(BF)
        dz_ref[:, d:2 * d] = dzc.astype(BF)
        dz_ref[:, 2 * d:3 * d] = dzx.astype(BF)
        rows += [_rowsum(dzb), _rowsum(dzc), _rowsum(dzx)]
        dconv = [_rowsum(dca * p2), _rowsum(dca * p1), _rowsum(dca * pa)]
        zuv, zvv = zu[...], zv[...]
        gu, dgu_dz = _gelu_parts(zuv)
        gv, dgv_dz = _gelu_parts(zvv)
        vhat, rstd = _ln_stats(gv)
        gain = lnv_ref[0:1, :]
        vn_s[...] = (vhat * gain + lnv_ref[1:2, :]).astype(BF)
        _spatial_mix(wm_ref, vn_s, mixed_s, bias_ref, tm, d)
        dubv = dub_ref[...]
        dzu = dubv * mixed_s[...] * dgu_dz
        dmixed = dubv * gu
        dmx_s[...] = dmixed.astype(BF)
        _spatial_mix(wmt_ref, dmx_s, dvn_s, None, tm, d)
        dvn = dvn_s[...]
        dzv = _ln_bwd(dvn, vhat, rstd, gain) * dgv_dz
        dz_ref[:, 3 * d:4 * d] = dzu.astype(BF)
        dz_ref[:, 4 * d:5 * d] = dzv.astype(BF)
        rows += [_rowsum(dzu), _rowsum(dzv)]
        dlnv = [_rowsum(dvn * vhat), _rowsum(dvn)]
        dbs_part = dmixed[0:GMLP_BLOCK, :]
        for n in range(1, tm // GMLP_BLOCK):
            dbs_part = dbs_part + dmixed[n * GMLP_BLOCK:(n + 1) * GMLP_BLOCK, :]
        ddv = dd_ref[...]
        t = (i * tm + lax.broadcasted_iota(jnp.int32, (ext_rows + 8, 1), 0) + 1).astype(F32)
        dde = jnp.concatenate([ddv, jnp.where(i < ni - 1, dd_n[...], 0.0)], axis=0)
        for k, win in enumerate(POOL_WINDOWS):
            cs = slice(k * grp, (k + 1) * grp)
            ext[0:tm + HALO, cs] = dde[:, cs] / jnp.minimum(t, float(win))
        dzp_parts = []
        for k, win in enumerate(POOL_WINDOWS):
            cs = slice(k * grp, (k + 1) * grp)
            s = ext[0:tm, cs]
            for j in range(1, win):
                s = s + ext[pl.ds(j, tm), cs]
            dzp_parts.append(s - ddv[:, cs])
        dzp = jnp.concatenate(dzp_parts, axis=1)
        dz_ref[:, 5 * d:6 * d] = dzp.astype(BF)
        rows += [_rowsum(dzp)]
        red = jnp.concatenate(rows + dconv + dlnv + [jnp.zeros((5, d), F32)], axis=0)

        @pl.when(i == 0)
        def _():
            red_ref[...] = red
            dbs_acc[...] = dbs_part
            dws_ref[...] = jnp.zeros_like(dws_ref)

        @pl.when(i > 0)
        def _():
            red_ref[...] += red
            dbs_acc[...] += dbs_part

        for n in range(tm // GMLP_BLOCK):
            for g in range(ng):
                rs = slice(n * GMLP_BLOCK, (n + 1) * GMLP_BLOCK)
                cs = slice(g * GMLP_BLOCK, (g + 1) * GMLP_BLOCK)
                dws_ref[g] += mask_ref[...] * lax.dot_general(
                    dmx_s[rs, cs], vn_s[rs, cs], (NT, ((), ())), preferred_element_type=F32)

        @pl.when(i == ni - 1)
        def _():
            lane = lax.broadcasted_iota(jnp.int32, (GMLP_BLOCK, GMLP_BLOCK), 1)
            out = jnp.zeros((GMLP_BLOCK, GMLP_BLOCK), F32)
            for g in range(ng):
                sg = jnp.sum(dbs_acc[:, g * GMLP_BLOCK:(g + 1) * GMLP_BLOCK], axis=1, keepdims=True)
                out = out + jnp.where(lane == g, sg, 0.0)
            dbs_ref[...] = out

    row = pl.BlockSpec((tm, d), lambda i: (i, 0))
    full = lambda a: pl.BlockSpec(a.shape, lambda i: (0,) * a.ndim)
    hb = tm // HALO
    last_hb = m // HALO - 1
    nrow = pl.BlockSpec((HALO, d), lambda i: (jnp.minimum((i + 1) * hb, last_hb), 0))
    return pl.pallas_call(
        body, name=name, grid=(ni,),
        in_specs=[pl.BlockSpec(memory_space=pl.ANY), row, row, row]
        + [_seg_spec(tm, d, s) for s in range(6)]
        + [_prev_halo_spec(tm, d, 1), _prev_halo_spec(tm, d, 2), nrow, _next_halo_spec(tm, d, 0, m), nrow]
        + [full(conv_a), full(lnv), full(wm), full(wmt), full(bias_full), full(mask)],
        out_specs=[pl.BlockSpec((tm, 6 * d), lambda i: (i, 0)), pl.BlockSpec((16, d), lambda i: (0, 0)),
                   full(wm), pl.BlockSpec((GMLP_BLOCK, GMLP_BLOCK), lambda i: (0, 0))],
        out_shape=[jax.ShapeDtypeStruct(dz.shape, BF), jax.ShapeDtypeStruct((16, d), F32),
                   jax.ShapeDtypeStruct(wm.shape, F32), jax.ShapeDtypeStruct((GMLP_BLOCK, GMLP_BLOCK), F32)],
        scratch_shapes=[pltpu.VMEM((2 * HALO + tm, d), F32), pltpu.VMEM((tm + 8, d), F32),
                        pltpu.VMEM((tm, d), BF), pltpu.VMEM((tm, d), F32), pltpu.VMEM((tm, d), BF),
                        pltpu.VMEM((tm, d), F32), pltpu.VMEM((GMLP_BLOCK, d), F32)],
        input_output_aliases={0: 0},
        compiler_params=_params(("arbitrary",)),
    )(dz, dua, dub, ddd, z, z, z, z, z, z, z, z, dua, z, ddd, conv_a, lnv, wm, wmt, bias_full, mask)


def _layer_fwd(x, w, alpha, tag):
    z, h = _mod_matmul(x, w["mod1"], w["w_in8"], w["b_in8"], flat_out=True, name="in_proj" + tag)
    ua, ub, dd = _mix_fwd(z, w["conv_a"], w["lnv"], w["wm"], w["bias_full"], name="mix_fwd" + tag)
    ya = _mm_rows(ua, w["w_a_out"], dn=NN, name="a_out" + tag)
    yb = _mm_rows(ub, w["w_b_out"], dn=NN, name="b_out" + tag)
    ycp = _pool_proj(dd, w["w_pool"], dn=NN, name="pool_proj" + tag)
    merged = _merge(z, ya, yb, ycp, w["pool_scale"], name="merge" + tag)
    o = _mm_rows(merged, w["w_o"], dn=NN, name="o_proj" + tag)
    x1 = _resid_ln(x, o, w["ln1"], alpha, name="ln1" + tag)
    up8, h2 = _mod_matmul(x1, w["mod2"], w["w_up8"], w["b_up8"], flat_out=False, name="up_proj" + tag)
    up4 = up8.reshape((2, up8.shape[0] // 2) + up8.shape[1:])
    f4 = _ffn_fwd(up4, w["cw"], w["cb"], name="ffn_fwd" + tag)
    y2 = _down_proj(f4, w["wd4"], name="down_proj" + tag)
    x2 = _resid_ln(x1, y2, w["ln2"], alpha, name="ln2" + tag)
    saved = dict(x=x, z=z, h=h, ua=ua, ub=ub, dd=dd, ya=ya, yb=yb, ycp=ycp, merged=merged, o=o, x1=x1,
                 up4=up4, h2=h2, f4=f4, y2=y2)
    return x2, saved


def _layer_bwd(dpart, dh_above, xmod_above, m_above, w, s, alpha, tag):
    dy2, dx1p, red2 = _resid_ln_bwd(dpart, dh_above, xmod_above, m_above, s["x1"], s["y2"], w["ln2"], alpha,
                                    name="ln2_bwd" + tag)
    df4 = _down_bwd(dy2, w["wd4"], name="down_bwd" + tag)
    gw_down4 = _tn_shards_lhs(s["f4"], dy2, name="gw_down" + tag)
    dup4, redf = _ffn_bwd(s["up4"], df4, w["cw"], w["cb"], name="ffn_bwd" + tag)
    dup8 = dup4.reshape((dup4.shape[0] * dup4.shape[1],) + dup4.shape[2:])
    gw_up8 = _tn_shards_rhs(s["h2"], dup8, name="gw_up" + tag)
    dh2 = _nt_shards(dup8, w["w_up8"], name="up_bwd" + tag)
    do, dxp, red1 = _resid_ln_bwd(dx1p, dh2, s["x1"], w["mod2"][0:1], s["x"], s["o"], w["ln1"], alpha,
                                  name="ln1_bwd" + tag)
    dm = _mm_rows(do, w["w_o"], dn=NT, name="o_bwd" + tag)
    gw_o = _mm_tn(s["merged"], do, name="gw_o" + tag)
    dya, dyb, dyc, dz, redg = _gate_bwd(dm, s["z"], s["ya"], s["yb"], s["ycp"], w["pool_scale"], name="gate_bwd" + tag)
    dua = _mm_rows(dya, w["w_a_out"], dn=NT, name="a_out_bwd" + tag)
    dub = _mm_rows(dyb, w["w_b_out"], dn=NT, name="b_out_bwd" + tag)
    ddd = _pool_proj(dyc, w["w_pool"], dn=NT, name="pool_bwd" + tag)
    gw_a_out = _mm_tn(s["ua"], dya, name="gw_a_out" + tag)
    gw_b_out = _mm_tn(s["ub"], dyb, name="gw_b_out" + tag)
    gw_pool = _tn_pool(s["dd"], dyc, w["w_pool"].shape[0], name="gw_pool" + tag)
    dz, redm, dws, dbs = _mix_bwd(dz, dua, dub, ddd, s["z"], w["conv_a"], w["lnv"], w["wm"], w["wmt"],
                                  w["bias_full"], w["mask"], name="mix_bwd" + tag)
    gw_in8 = _tn_cols_rhs(s["h"], dz, w["w_in8"].shape[0], name="gw_in" + tag)
    dh = _nt_cols(dz, w["w_in8"], name="in_bwd" + tag)
    big = dict(w_in=gw_in8, w_a_out=gw_a_out, w_b_out=gw_b_out, w_pool=gw_pool, w_o=gw_o, w_up=gw_up8,
               w_down=gw_down4)
    reds = dict(red2=red2, redf=redf, red1=red1, redg=redg, redm=redm, dws=dws, dbs=dbs)
    return dxp, dh, big, reds


def _local_step(x, tgt, ws, alpha):
    depth = len(ws)
    saved = []
    y = x
    for l in range(depth):
        y, s = _layer_fwd(y, ws[l], alpha, "_l%d" % l)
        saved.append(s)
    dpart, loss_blk = _loss_grad(y, tgt, name="loss_grad")
    dh = xmod = mvec = None
    bigs, reds = [None] * depth, [None] * depth
    for l in reversed(range(depth)):
        dpart, dh, bigs[l], reds[l] = _layer_bwd(dpart, dh, xmod, mvec, ws[l], saved[l], alpha, "_l%d" % l)
        xmod, mvec = saved[l]["x"], ws[l]["mod1"][0:1]
    grad_x, red0 = _resid_ln_bwd(dpart, dh, xmod, mvec, None, None, None, alpha, name="in_bwd_tail")
    d_ada = []
    for l in range(depth):
        below = red0 if l == 0 else reds[l - 1]["red2"]
        r1, r2 = reds[l]["red1"], reds[l]["red2"]
        d_ada.append(jnp.stack([below[1], below[0], r1[4], r1[1], r1[0], r2[4]]))
    return loss_blk, grad_x, bigs, reds, jnp.stack(d_ada)


def _small_grads(r):
    redm, redg, redf = r["redm"], r["redg"], r["redf"]
    ng = r["dws"].shape[0]
    return dict(
        b_in=jnp.concatenate([redm[0:6], redg[1:4]], axis=0).reshape(-1),
        conv_a=redm[6:9], ln_v_g=redm[9], ln_v_b=redm[10],
        w_spatial=r["dws"], b_spatial=r["dbs"][:, :ng].T,
        pool_scale=redg[0], ln1_g=r["red1"][2], ln1_b=r["red1"][3],
        b_up=jnp.concatenate([redf[:, 4, :].reshape(-1), redf[:, 5, :].reshape(-1)]),
        conv_ffn=jnp.transpose(redf[:, 0:3, :], (1, 0, 2)).reshape(3, -1), conv_ffn_b=redf[:, 3, :].reshape(-1),
        ln2_g=r["red2"][2], ln2_b=r["red2"][3])


def _layer_weights(l, ada, big, conv_a, conv_ffn, p):
    d = ada.shape[-1]
    sh1, sc1, gt1, sh2, sc2, gt2 = (ada[l, k][None, :] for k in range(6))
    nb = big["w_up"].shape[1]
    fs = big["w_up"].shape[3]
    nj = nb // 2
    pos = jnp.arange(GMLP_BLOCK)
    allowed = (pos[None, :] // CHUNK) <= (pos[:, None] // CHUNK)
    wmask = jnp.where(allowed[None], p["w_spatial"][l], 0.0)
    return dict(
        mod1=jnp.concatenate([1.0 + sc1, sh1]), mod2=jnp.concatenate([1.0 + sc2, sh2]),
        ln1=jnp.concatenate([gt1, p["ln1_g"][l][None], p["ln1_b"][l][None]]),
        ln2=jnp.concatenate([gt2, p["ln2_g"][l][None], p["ln2_b"][l][None]]),
        w_in8=big["w_in"][l], b_in8=p["b_in"][l].reshape(N_DEV, 1, -1),
        w_a_out=big["w_a_out"][l].reshape(d, d), w_b_out=big["w_b_out"][l].reshape(d, d),
        w_o=big["w_o"][l].reshape(d, d),
        w_pool=big["w_pool"][l].reshape(len(POOL_WINDOWS), d // len(POOL_WINDOWS), d // len(POOL_WINDOWS)),
        w_up8=big["w_up"][l], b_up8=p["b_up"][l].reshape(nb, 1, fs),
        wd4=big["w_down"][l].reshape(nj, fs, d),
        conv_a=conv_a[l], lnv=jnp.stack([p["ln_v_g"][l], p["ln_v_b"][l]]),
        wm=wmask.astype(BF), wmt=jnp.transpose(wmask, (0, 2, 1)).astype(BF),
        bias_full=jnp.repeat(p["b_spatial"][l].T, GMLP_BLOCK, axis=1), mask=allowed.astype(F32),
        pool_scale=p["pool_scale"][l][None],
        cw=jnp.transpose(conv_ffn[l].reshape(3, nj, fs), (1, 0, 2)), cb=p["conv_ffn_b"][l].reshape(nj, 1, fs))


ANY = pl.BlockSpec(memory_space=pl.ANY)


def _place():
    x, y, c = lax.axis_index("x"), lax.axis_index("y"), lax.axis_index("c")
    chips = [(1 - x, y), (x, 1 - y), (1 - x, 1 - y)]
    return x, y, c, chips


def _allgather_vmem(xs, *, name):
    r, cdim = xs.shape

    def body(x_ref, out_ref, send_sems, recv_sems, local_sem):
        x, y, c, chips = _place()
        me, sibling = (x, y, c), (x, y, 1 - c)

        def rows(px, py, pc):
            return out_ref.at[pl.ds((4 * px + 2 * py + pc) * r, r), :]

        def copy(k, block, to, src=None):
            return pltpu.make_async_remote_copy(
                src_ref=rows(*block) if src is None else src, dst_ref=rows(*block),
                send_sem=send_sems.at[k], recv_sem=recv_sems.at[k], device_id=to, device_id_type=MESH)

        mine = pltpu.make_async_copy(x_ref, rows(*me), local_sem)
        mine.start()
        first = [copy(0, me, sibling, src=x_ref)]
        first += [copy(1 + j, me, (*chip, c), src=x_ref) for j, chip in enumerate(chips)]
        for cp in first:
            cp.start()
        passed = [copy(4 + j, (*chip, c), sibling) for j, chip in enumerate(chips)]
        for j, chip in enumerate(chips):
            copy(1 + j, (*chip, c), me).wait_recv()
            passed[j].start()
        copy(0, sibling, me).wait_recv()
        for j, chip in enumerate(chips):
            copy(4 + j, (*chip, 1 - c), me).wait_recv()
        for cp in first + passed:
            cp.wait_send()
        mine.wait()

    return pl.pallas_call(
        body, name=name, out_shape=jax.ShapeDtypeStruct((N_DEV * r, cdim), xs.dtype),
        in_specs=[pl.BlockSpec(memory_space=pltpu.VMEM)], out_specs=pl.BlockSpec(memory_space=pltpu.VMEM),
        scratch_shapes=[pltpu.SemaphoreType.DMA((7,)), pltpu.SemaphoreType.DMA((7,)), pltpu.SemaphoreType.DMA],
        compiler_params=_params(),
    )(xs)


def _gather_weights(shards, *, name):
    n = len(shards)

    def body(*refs):
        ins, outs = refs[:n], refs[n:2 * n]
        send_sems, recv_sems, local_sems = refs[2 * n:]
        x, y, c, chips = _place()
        me, sibling = (x, y, c), (x, y, 1 - c)

        def slot(a, px, py, pc):
            return outs[a].at[:, 4 * px + 2 * py + pc]

        def copy(a, k, block, to, src=None):
            return pltpu.make_async_remote_copy(
                src_ref=slot(a, *block) if src is None else src, dst_ref=slot(a, *block),
                send_sem=send_sems.at[7 * a + k], recv_sem=recv_sems.at[7 * a + k], device_id=to,
                device_id_type=MESH)

        mine = [pltpu.make_async_copy(ins[a], slot(a, *me), local_sems.at[a]) for a in range(n)]
        for cp in mine:
            cp.start()
        first = []
        for j, chip in enumerate(chips):
            first += [copy(a, 1 + j, me, (*chip, c), src=ins[a]) for a in range(n)]
        first += [copy(a, 0, me, sibling, src=ins[a]) for a in range(n)]
        for cp in first:
            cp.start()
        passed = []
        for j, chip in enumerate(chips):
            for a in range(n):
                copy(a, 1 + j, (*chip, c), me).wait_recv()
                fwd = copy(a, 4 + j, (*chip, c), sibling)
                fwd.start()
                passed.append(fwd)
        for a in range(n):
            copy(a, 0, sibling, me).wait_recv()
        for j, chip in enumerate(chips):
            for a in range(n):
                copy(a, 4 + j, (*chip, 1 - c), me).wait_recv()
        for cp in first + passed:
            cp.wait_send()
        for cp in mine:
            cp.wait()

    out_shape = [jax.ShapeDtypeStruct((s.shape[0], N_DEV) + s.shape[1:], s.dtype) for s in shards]
    return pl.pallas_call(
        body, name=name, out_shape=out_shape, in_specs=[ANY] * n, out_specs=[ANY] * n,
        scratch_shapes=[pltpu.SemaphoreType.DMA((7 * n,)), pltpu.SemaphoreType.DMA((7 * n,)),
                        pltpu.SemaphoreType.DMA((n,))],
        compiler_params=_params(),
    )(*shards)


def _pair_exchange(grads, *, name):
    n = len(grads)

    def body(*refs):
        ins, outs = refs[:n], refs[n:2 * n]
        send_sems, recv_sems = refs[2 * n:]
        x, y, c, _ = _place()
        copies = []
        for a in range(n):
            for q in range(4):
                copies.append(pltpu.make_async_remote_copy(
                    src_ref=ins[a].at[:, 2 * q + (1 - c)], dst_ref=outs[a].at[q],
                    send_sem=send_sems.at[4 * a + q], recv_sem=recv_sems.at[4 * a + q],
                    device_id=(x, y, 1 - c), device_id_type=MESH))
        for cp in copies:
            cp.start()
        for cp in copies:
            cp.wait_recv()
        for cp in copies:
            cp.wait_send()

    out_shape = [jax.ShapeDtypeStruct((4, g.shape[0]) + g.shape[2:], g.dtype) for g in grads]
    return pl.pallas_call(
        body, name=name, out_shape=out_shape, in_specs=[ANY] * n, out_specs=[ANY] * n,
        scratch_shapes=[pltpu.SemaphoreType.DMA((4 * n,)), pltpu.SemaphoreType.DMA((4 * n,))],
        compiler_params=_params(),
    )(*grads)


def _pick_tile(r, cap):
    best = None
    for t in range(8, min(r, cap) + 1, 8):
        if r % t == 0:
            best = t
    return best if best is not None else r


def _pair_sum(g, r1, cidx, *, name):
    p, _, r, cdim = g.shape
    tr = _pick_tile(r, 256)

    def body(c_ref, g_ref, r_ref, o_ref):
        del c_ref
        o_ref[...] = (g_ref[...] + r_ref[...]).astype(BF)

    grid_spec = pltpu.PrefetchScalarGridSpec(
        num_scalar_prefetch=1, grid=(4, r // tr),
        in_specs=[pl.BlockSpec((p, None, tr, cdim), lambda q, i, c: (0, 2 * q + c[0], i, 0)),
                  pl.BlockSpec((None, p, tr, cdim), lambda q, i, c: (q, 0, i, 0))],
        out_specs=pl.BlockSpec((None, p, tr, cdim), lambda q, i, c: (q, 0, i, 0)))
    return pl.pallas_call(
        body, name=name, grid_spec=grid_spec, out_shape=jax.ShapeDtypeStruct((4, p, r, cdim), BF),
        compiler_params=_params(("arbitrary", "arbitrary")),
    )(cidx, g, r1)


def _chip_exchange(bufs, depth, *, name):
    n = len(bufs)

    def body(*refs):
        ins, outs = refs[:n], refs[n:n + n // depth]
        send_sems, recv_sems, local_sems = refs[n + n // depth:]
        x, y, c, chips = _place()
        myq = 2 * x + y
        local, sends, recvs = [], [], []
        for i in range(n):
            a, l = divmod(i, depth)
            local.append(pltpu.make_async_copy(ins[i].at[myq], outs[a].at[myq, l], local_sems.at[i]))
            for j, (cx, cy) in enumerate(chips):
                q = 2 * cx + cy
                sends.append(pltpu.make_async_remote_copy(
                    src_ref=ins[i].at[q], dst_ref=outs[a].at[myq, l],
                    send_sem=send_sems.at[3 * i + j], recv_sem=recv_sems.at[3 * i + j],
                    device_id=(cx, cy, c), device_id_type=MESH))
                recvs.append(pltpu.make_async_remote_copy(
                    src_ref=ins[i].at[q], dst_ref=outs[a].at[q, l],
                    send_sem=send_sems.at[3 * i + j], recv_sem=recv_sems.at[3 * i + j],
                    device_id=(cx, cy, c), device_id_type=MESH))
        for cp in local + sends:
            cp.start()
        for cp in recvs:
            cp.wait_recv()
        for cp in sends:
            cp.wait_send()
        for cp in local:
            cp.wait()

    out_shape = [jax.ShapeDtypeStruct((4, depth) + bufs[a * depth].shape[1:], BF) for a in range(n // depth)]
    return pl.pallas_call(
        body, name=name, out_shape=out_shape, in_specs=[ANY] * n, out_specs=[ANY] * (n // depth),
        scratch_shapes=[pltpu.SemaphoreType.DMA((3 * n,)), pltpu.SemaphoreType.DMA((3 * n,)),
                        pltpu.SemaphoreType.DMA((n,))],
        compiler_params=_params(),
    )(*bufs)


def _ada_fwd(c_all, w_ada, *, name):
    depth, d, ns = w_ada.shape
    nb = c_all.shape[0]

    def body(c_ref, w_ref, o_ref):
        cv = c_ref[...]
        act = cv * jax.nn.sigmoid(cv)
        o_ref[...] = jnp.dot(act, w_ref[...], preferred_element_type=F32, precision=lax.Precision.HIGHEST)

    return pl.pallas_call(
        body, name=name, grid=(depth,),
        in_specs=[pl.BlockSpec((nb, d), lambda l: (0, 0)), pl.BlockSpec((None, d, ns), lambda l: (l, 0, 0))],
        out_specs=pl.BlockSpec((None, nb, ns), lambda l: (l, 0, 0)),
        out_shape=jax.ShapeDtypeStruct((depth, nb, ns), F32), compiler_params=_params(("parallel",)),
    )(c_all, w_ada)


def _ada_bwd(ct, dmine, dall, *, name):
    depth, nb, ns = dmine.shape
    d = ct.shape[0]

    def body(ct_ref, dm_ref, da_ref, gw_ref, gb_ref):
        cv = ct_ref[...]
        act = cv * jax.nn.sigmoid(cv)
        gw_ref[...] = jnp.dot(act, dm_ref[...], preferred_element_type=F32, precision=lax.Precision.HIGHEST)
        s = da_ref[0]
        for b in range(1, nb):
            s = s + da_ref[b]
        gb_ref[...] = s

    return pl.pallas_call(
        body, name=name, grid=(depth,),
        in_specs=[pl.BlockSpec((d, nb), lambda l: (0, 0)), pl.BlockSpec((None, nb, ns), lambda l: (l, 0, 0)),
                  pl.BlockSpec(dall.shape, lambda l: (0, 0, 0))],
        out_specs=[pl.BlockSpec((None, d, ns), lambda l: (l, 0, 0)), pl.BlockSpec(dall.shape[1:], lambda l: (0, 0))],
        out_shape=[jax.ShapeDtypeStruct((depth, d, ns), F32), jax.ShapeDtypeStruct(dall.shape[1:], F32)],
        compiler_params=_params(("arbitrary",)),
    )(ct, dmine, dall)


def _sum_parts(parts, *, name):
    p, r, cdim = parts.shape
    tr = _pick_tile(r, 512)

    def body(p_ref, o_ref):
        s = p_ref[0]
        for k in range(1, p):
            s = s + p_ref[k]
        o_ref[...] = s

    return pl.pallas_call(
        body, name=name, grid=(r // tr,),
        in_specs=[pl.BlockSpec((p, tr, cdim), lambda i: (0, i, 0))], out_specs=pl.BlockSpec((tr, cdim), lambda i: (i, 0)),
        out_shape=jax.ShapeDtypeStruct((r, cdim), F32), compiler_params=_params(("parallel",)),
    )(parts)


def _adamw(parts, w, m, v, *, name):
    p, depth, r, cdim = parts.shape
    tr = _pick_tile(r, 256)

    def body(p_ref, w_ref, m_ref, v_ref, g_out, d_out, m_out, v_out):
        g = p_ref[0].astype(F32)
        for k in range(1, p):
            g = g + p_ref[k].astype(F32)
        m2 = ADAM_B1 * m_ref[...] + (1.0 - ADAM_B1) * g
        v2 = ADAM_B2 * v_ref[...] + (1.0 - ADAM_B2) * (g * g)
        m_hat = m2 / (1.0 - ADAM_B1 ** ADAM_STEP)
        v_hat = v2 / (1.0 - ADAM_B2 ** ADAM_STEP)
        g_out[...] = g
        d_out[...] = -ADAM_LR * (m_hat / (jnp.sqrt(v_hat) + ADAM_EPS) + ADAM_WD * w_ref[...])
        m_out[...] = m2
        v_out[...] = v2

    blk = pl.BlockSpec((None, tr, cdim), lambda l, i: (l, i, 0))
    out = jax.ShapeDtypeStruct((depth, r, cdim), F32)
    return pl.pallas_call(
        body, name=name, grid=(depth, r // tr),
        in_specs=[pl.BlockSpec((p, None, tr, cdim), lambda l, i: (0, l, i, 0)), blk, blk, blk],
        out_specs=[blk, blk, blk, blk], out_shape=[out, out, out, out],
        compiler_params=_params(("parallel", "parallel")),
    )(parts, w, m, v)


BIG = ("w_in", "w_a_out", "w_b_out", "w_pool", "w_o", "w_up", "w_down")
SMALL_REPLICATED = ("b_in", "ln_v_g", "ln_v_b", "w_spatial", "b_spatial", "pool_scale", "ln1_g", "ln1_b", "b_up",
                    "conv_ffn_b", "ln2_g", "ln2_b")
SMALL_SHARDED = ("conv_a", "conv_ffn")
WEIGHTS = ("w_ada", "b_ada", "w_in", "b_in", "conv_a", "w_a_out", "ln_v_g", "ln_v_b", "w_spatial", "b_spatial",
           "w_b_out", "w_pool", "pool_scale", "w_o", "ln1_g", "ln1_b", "w_up", "b_up", "conv_ffn", "conv_ffn_b",
           "w_down", "ln2_g", "ln2_b")
LANES = 128


def _as_rows(flat, mult=8):
    n = flat.shape[0]
    pad = (-n) % (LANES * mult)
    if pad:
        flat = jnp.concatenate([flat, jnp.zeros((pad,), flat.dtype)])
    return flat.reshape(-1, LANES)


def _shard3(a):
    return a.reshape((-1,) + a.shape[-2:])


def kernel(x, c, w_ada, b_ada, w_in, b_in, conv_a, w_a_out, ln_v_g, ln_v_b, w_spatial, b_spatial, w_b_out, w_pool, pool_scale, w_o, ln1_g, ln1_b, w_up, b_up, conv_ffn, conv_ffn_b, w_down, ln2_g, ln2_b, loss_target, m_w_ada, m_b_ada, m_w_in, m_b_in, m_conv_a, m_w_a_out, m_ln_v_g, m_ln_v_b, m_w_spatial, m_b_spatial, m_w_b_out, m_w_pool, m_pool_scale, m_w_o, m_ln1_g, m_ln1_b, m_w_up, m_b_up, m_conv_ffn, m_conv_ffn_b, m_w_down, m_ln2_g, m_ln2_b, v_w_ada, v_b_ada, v_w_in, v_b_in, v_conv_a, v_w_a_out, v_ln_v_g, v_ln_v_b, v_w_spatial, v_b_spatial, v_w_b_out, v_w_pool, v_pool_scale, v_w_o, v_ln1_g, v_ln1_b, v_w_up, v_b_up, v_conv_ffn, v_conv_ffn_b, v_w_down, v_ln2_g, v_ln2_b):
    p = dict(locals())
    depth, d = w_in.shape[0], w_in.shape[1]
    alpha = (2 * depth) ** 0.25
    me = 4 * lax.axis_index("x") + 2 * lax.axis_index("y") + lax.axis_index("c")
    cidx = lax.axis_index("c").astype(jnp.int32).reshape(1)

    n_ca, n_cf = conv_a.size, conv_ffn.size
    packed = _as_rows(jnp.concatenate([c.reshape(-1), conv_a.reshape(-1), conv_ffn.reshape(-1)]))
    got = _allgather_vmem(packed, name="gather_cond").reshape(N_DEV, -1)
    c_all = got[:, :d]
    ct = c_all.T
    conv_a_full = jnp.transpose(got[:, d:d + n_ca].reshape((N_DEV,) + conv_a.shape), (1, 2, 0, 3)).reshape(depth, 3, -1)
    conv_ffn_full = jnp.transpose(got[:, d + n_ca:d + n_ca + n_cf].reshape((N_DEV,) + conv_ffn.shape),
                                  (1, 2, 0, 3)).reshape(depth, 3, -1)

    ns_ada = w_ada.shape[2]
    ada_part = _ada_fwd(c_all, w_ada, name="ada_fwd")
    ada_all = _allgather_vmem(_as_rows(ada_part.reshape(-1)), name="gather_ada")
    ada_all = ada_all.reshape(N_DEV, depth, N_DEV, ns_ada)
    ada_mine = lax.dynamic_index_in_dim(ada_all, me, axis=2, keepdims=False)
    ada = jnp.transpose(ada_mine, (1, 0, 2)).reshape(depth, -1) + b_ada
    ada = ada.reshape(depth, 6, d)

    shards = [_shard3(p[n].astype(BF)) for n in BIG]
    full = _gather_weights(shards, name="gather_weights")
    big = {}
    for n, f in zip(BIG, full):
        big[n] = f.reshape((depth, -1) + f.shape[1:]) if n == "w_pool" else f
    ws = [_layer_weights(l, ada, big, conv_a_full, conv_ffn_full, p) for l in range(depth)]

    loss_blk, grad_x, bigs, reds, d_ada = _local_step(x[0], loss_target[0], ws, alpha)
    loss = lax.psum(loss_blk[0, 0], ("x", "y", "c"))

    dada_all = _allgather_vmem(_as_rows(d_ada.reshape(-1)), name="gather_dada")
    dada_all = dada_all.reshape(N_DEV, -1, LANES)
    dflat = dada_all.reshape(N_DEV, depth, 6 * d)
    dmine = lax.dynamic_slice_in_dim(dflat, me * ns_ada, ns_ada, axis=2)
    gw_ada, gb_rows = _ada_bwd(ct, jnp.transpose(dmine, (1, 0, 2)), dada_all, name="ada_bwd")
    gb_ada = gb_rows.reshape(-1)[:depth * 6 * d].reshape(depth, 6 * d)

    def chunks(n, g):
        if n == "w_pool":
            return g.reshape(g.shape[0], N_DEV, g.shape[1] // N_DEV, g.shape[2])
        if n in ("w_in", "w_up"):
            return g[None]
        return g.reshape(1, N_DEV, -1, g.shape[-1])

    grads4 = [chunks(n, bigs[l][n]) for n in BIG for l in range(depth)]
    recv1 = _pair_exchange(grads4, name="rs_pair")
    sendbufs = [_pair_sum(g, r, cidx, name="rs_sum_%d" % i) for i, (g, r) in enumerate(zip(grads4, recv1))]
    recv2 = _chip_exchange(sendbufs, depth, name="rs_chip")

    out = {}
    for n, parts in zip(BIG, recv2):
        shard_shape = p[n].shape
        w3 = p[n].reshape(depth, -1, shard_shape[-1])
        parts4 = parts.reshape((4,) + w3.shape)
        res = _adamw(parts4, w3, p["m_" + n].reshape(w3.shape), p["v_" + n].reshape(w3.shape), name="adamw_" + n)
        out[n] = [r.reshape(shard_shape) for r in res]
    out["w_ada"] = _adamw(gw_ada[None], w_ada, m_w_ada, v_w_ada, name="adamw_w_ada")

    smalls = [_small_grads(r) for r in reds]
    order = SMALL_REPLICATED + SMALL_SHARDED
    payload = jnp.concatenate([smalls[l][n].reshape(-1) for n in order for l in range(depth)])
    n_pay = payload.shape[0]
    gathered = _allgather_vmem(_as_rows(payload), name="gather_small")
    gsum = _sum_parts(gathered.reshape(N_DEV, -1, LANES), name="sum_small").reshape(-1)[:n_pay]
    n_rep = sum(p[n].size for n in SMALL_REPLICATED)
    ga_full = gsum[n_rep:n_rep + depth * 3 * d].reshape(depth, 3, d)
    gf_full = gsum[n_rep + depth * 3 * d:].reshape(depth, 3, -1)
    ca_w, cf_w = conv_a.shape[2], conv_ffn.shape[2]
    g_ca = lax.dynamic_slice_in_dim(ga_full, me * ca_w, ca_w, axis=2)
    g_cf = lax.dynamic_slice_in_dim(gf_full, me * cf_w, cf_w, axis=2)
    names = ("b_ada",) + order
    gflat = _as_rows(jnp.concatenate([gb_ada.reshape(-1), gsum[:n_rep], g_ca.reshape(-1), g_cf.reshape(-1)]))
    pack = lambda pre: _as_rows(jnp.concatenate([p[pre + n].reshape(-1) for n in names]))
    res = _adamw(gflat[None, None], pack("")[None], pack("m_")[None], pack("v_")[None], name="adamw_small")
    off = 0
    for n in names:
        size = p[n].size
        out[n] = [r.reshape(-1)[off:off + size].reshape(p[n].shape) for r in res]
        off += size

    return (loss, grad_x[None]) + tuple(out[n][k] for k in range(4) for n in WEIGHTS)
```

```python
import functools

import jax
import jax.numpy as jnp
from jax import lax
from jax.experimental import pallas as pl
from jax.experimental.pallas import tpu as pltpu

F32 = jnp.float32
BF = jnp.bfloat16
MESH = pl.DeviceIdType.MESH

LN_EPS = 1e-5
POOL_WINDOWS = (2, 4, 8, 16)
GMLP_BLOCK = 128
CHUNK = 64
HALO = 16
ADAM_LR, ADAM_B1, ADAM_B2, ADAM_EPS, ADAM_WD, ADAM_STEP = 0.001, 0.9, 0.999, 1e-08, 0.01, 10
N_DEV = 8
VMEM_LIMIT = 56 * 1024 * 1024

NN = ((1,), (0,))
NT = ((1,), (1,))
TN = ((0,), (0,))


def _params(sem=None, vmem=VMEM_LIMIT, **kw):
    if sem is not None:
        kw["dimension_semantics"] = sem
    return pltpu.CompilerParams(vmem_limit_bytes=vmem, **kw)


class _Phase:
    def __init__(self, ins, out_shapes, aliases, n_remote, n_local, build):
        self.ins, self.out_shapes, self.aliases = list(ins), list(out_shapes), dict(aliases)
        self.n_remote, self.n_local, self.build = n_remote, n_local, build
        self.results = None


def _pcall(body, args, *, name, grid, in_specs, out_specs, out_shape, scratch_shapes=(), sem=None, aliases=None,
           phases=()):
    aliases = dict(aliases or {})
    if not phases:
        return pl.pallas_call(
            body, name=name, grid=grid, in_specs=list(in_specs), out_specs=out_specs, out_shape=out_shape,
            scratch_shapes=list(scratch_shapes), input_output_aliases=aliases, compiler_params=_params(sem),
        )(*args)
    single = not isinstance(out_shape, (list, tuple))
    o_specs = [out_specs] if single else list(out_specs)
    o_shapes = [out_shape] if single else list(out_shape)
    n_in, n_out, n_scr = len(args), len(o_shapes), len(scratch_shapes)
    ex_args, ex_out, sems = [], [], []
    for ph in phases:
        for src, dst in ph.aliases.items():
            aliases[n_in + len(ex_args) + src] = n_out + len(ex_out) + dst
        ex_args += ph.ins
        ex_out += ph.out_shapes
        sems += [pltpu.SemaphoreType.DMA((max(ph.n_remote, 1),)), pltpu.SemaphoreType.DMA((max(ph.n_remote, 1),)),
                 pltpu.SemaphoreType.DMA((max(ph.n_local, 1),))]

    def wrapped(*refs):
        pos = n_in
        ph_in = []
        for ph in phases:
            ph_in.append(refs[pos:pos + len(ph.ins)])
            pos += len(ph.ins)
        base_out = refs[pos:pos + n_out]
        pos += n_out
        ph_out = []
        for ph in phases:
            ph_out.append(refs[pos:pos + len(ph.out_shapes)])
            pos += len(ph.out_shapes)
        base_scr = refs[pos:pos + n_scr]
        ph_sems = refs[pos + n_scr:]
        first = last = None
        for ax, n in enumerate(grid):
            pid = pl.program_id(ax)
            first = (pid == 0) if first is None else first & (pid == 0)
            last = (pid == n - 1) if last is None else last & (pid == n - 1)

        def ops(k):
            return phases[k].build(ph_in[k], ph_out[k], *ph_sems[3 * k:3 * k + 3])

        @pl.when(first)
        def _():
            for k in range(len(phases)):
                for cp in ops(k)["start"]:
                    cp.start()

        body(*refs[:n_in], *base_out, *base_scr)

        @pl.when(last)
        def _():
            for k in range(len(phases)):
                o = ops(k)
                for cp in o["recv"]:
                    cp.wait_recv()
                for cp in o["send"]:
                    cp.wait_send()
                for cp in o["local"]:
                    cp.wait()

    hbm = pl.BlockSpec(memory_space=pl.ANY)
    res = pl.pallas_call(
        wrapped, name=name, grid=grid, in_specs=list(in_specs) + [hbm] * len(ex_args),
        out_specs=o_specs + [hbm] * len(ex_out), out_shape=o_shapes + ex_out,
        scratch_shapes=list(scratch_shapes) + sems, input_output_aliases=aliases,
        compiler_params=_params(("arbitrary",) * len(grid)),
    )(*args, *ex_args)
    pos = n_out
    for ph in phases:
        ph.results = list(res[pos:pos + len(ph.out_shapes)])
        pos += len(ph.out_shapes)
    return res[0] if single else list(res[:n_out])


def _gelu_parts(x):
    k = 0.7978845608028654
    x2 = x * x
    t = jnp.tanh(k * (x + 0.044715 * (x2 * x)))
    cdf = 0.5 * (1.0 + t)
    dcdf = 0.5 * (1.0 - t * t) * (k * (1.0 + 3.0 * 0.044715 * x2))
    return x * cdf, cdf + x * dcdf


def _gelu(x):
    t = jnp.tanh(0.7978845608028654 * (x + 0.044715 * (x * x * x)))
    return x * (0.5 * (1.0 + t))


def _rowsum(v):
    return jnp.sum(v, axis=0, keepdims=True)


def _ln_stats(r):
    mu = jnp.mean(r, axis=-1, keepdims=True)
    xc = r - mu
    var = jnp.mean(xc * xc, axis=-1, keepdims=True)
    rstd = lax.rsqrt(var + LN_EPS)
    return xc * rstd, rstd


def _ln_bwd(dy, xhat, rstd, gain):
    dxh = dy * gain
    m1 = jnp.mean(dxh, axis=-1, keepdims=True)
    m2 = jnp.mean(dxh * xhat, axis=-1, keepdims=True)
    return rstd * (dxh - m1 - xhat * m2)


def _matmul(a, b, *, dn, grid, a_spec, b_spec, o_spec, out_shape, acc_shape, name, phases=()):
    nk = grid[2]
    direct = out_shape.dtype == F32

    def body(a_ref, b_ref, o_ref, *scratch):
        prod = lax.dot_general(a_ref[...], b_ref[...], (dn, ((), ())), preferred_element_type=F32)
        if nk == 1:
            o_ref[...] = prod.astype(o_ref.dtype)
            return
        acc = o_ref if direct else scratch[0]
        k = pl.program_id(2)

        @pl.when(k == 0)
        def _():
            acc[...] = prod

        @pl.when(k > 0)
        def _():
            acc[...] += prod

        if not direct:
            @pl.when(k == nk - 1)
            def _():
                o_ref[...] = acc[...].astype(o_ref.dtype)

    scratch = [] if (direct or nk == 1) else [pltpu.VMEM(acc_shape, F32)]
    return _pcall(body, (a, b), name=name, grid=grid, in_specs=[a_spec, b_spec], out_specs=o_spec,
                  out_shape=out_shape, scratch_shapes=scratch, sem=("parallel", "parallel", "arbitrary"),
                  phases=phases)


def _row_tile(m, want):
    t = min(m, want)
    assert m % t == 0
    return t


def _mm_rows(a, w, *, dn, name, out_dtype=F32, tm=512):
    m, k = a.shape
    n = w.shape[1] if dn == NN else w.shape[0]
    tm = _row_tile(m, tm)
    return _matmul(
        a, w, dn=dn, grid=(m // tm, 1, 1), name=name,
        a_spec=pl.BlockSpec((tm, k), lambda i, j, kk: (i, 0)),
        b_spec=pl.BlockSpec(w.shape, lambda i, j, kk: (0, 0)),
        o_spec=pl.BlockSpec((tm, n), lambda i, j, kk: (i, 0)),
        out_shape=jax.ShapeDtypeStruct((m, n), out_dtype), acc_shape=(tm, n))


def _mm_tn(a, b, *, name, tk=512):
    m, ka = a.shape
    n = b.shape[1]
    tk = _row_tile(m, tk)
    return _matmul(
        a, b, dn=TN, grid=(1, 1, m // tk), name=name,
        a_spec=pl.BlockSpec((tk, ka), lambda i, j, kk: (kk, 0)),
        b_spec=pl.BlockSpec((tk, n), lambda i, j, kk: (kk, 0)),
        o_spec=pl.BlockSpec((ka, n), lambda i, j, kk: (0, 0)),
        out_shape=jax.ShapeDtypeStruct((ka, n), F32), acc_shape=(ka, n))


def _mod_matmul(x, mod, w8, bias8, *, flat_out, name, tm=512, phases=()):
    m, k = x.shape
    nb, _, ns = w8.shape
    tm = _row_tile(m, tm)

    def body(x_ref, mod_ref, w_ref, b_ref, o_ref, h_ref, hs):
        @pl.when(pl.program_id(1) == 0)
        def _():
            h = (x_ref[...] * mod_ref[0:1, :] + mod_ref[1:2, :]).astype(BF)
            hs[...] = h
            h_ref[...] = h

        o_ref[...] = jnp.dot(hs[...], w_ref[...], preferred_element_type=F32) + b_ref[...]

    if flat_out:
        o_spec = pl.BlockSpec((tm, ns), lambda i, j: (i, j))
        o_shape = jax.ShapeDtypeStruct((m, nb * ns), F32)
    else:
        o_spec = pl.BlockSpec((None, tm, ns), lambda i, j: (j, i, 0))
        o_shape = jax.ShapeDtypeStruct((nb, m, ns), F32)
    return _pcall(
        body, (x, mod, w8, bias8), name=name, grid=(m // tm, nb),
        in_specs=[pl.BlockSpec((tm, k), lambda i, j: (i, 0)),
                  pl.BlockSpec((2, k), lambda i, j: (0, 0)),
                  pl.BlockSpec((None, k, ns), lambda i, j: (j, 0, 0)),
                  pl.BlockSpec((None, 1, ns), lambda i, j: (j, 0, 0))],
        out_specs=[o_spec, pl.BlockSpec((tm, k), lambda i, j: (i, 0))],
        out_shape=[o_shape, jax.ShapeDtypeStruct((m, k), BF)],
        scratch_shapes=[pltpu.VMEM((tm, k), BF)], sem=("parallel", "arbitrary"), phases=phases)


def _seg_spec(tm, d, s):
    return pl.BlockSpec((tm, d), lambda i, s=s: (i, s))


def _prev_halo_spec(tm, d, s):
    hb = tm // HALO
    return pl.BlockSpec((HALO, d), lambda i, s=s: (jnp.maximum(i * hb - 1, 0), s))


def _next_halo_spec(tm, d, s, m):
    hb = tm // HALO
    last = m // HALO - 1
    return pl.BlockSpec((HALO, d), lambda i, s=s: (jnp.minimum((i + 1) * hb, last), s))


def _spatial_mix(wm_ref, src, dst, bias_ref, tm, d):
    for n in range(tm // GMLP_BLOCK):
        for g in range(d // GMLP_BLOCK):
            rs = slice(n * GMLP_BLOCK, (n + 1) * GMLP_BLOCK)
            cs = slice(g * GMLP_BLOCK, (g + 1) * GMLP_BLOCK)
            v = jnp.dot(wm_ref[g], src[rs, cs], preferred_element_type=F32)
            if bias_ref is not None:
                v = v + bias_ref[:, cs]
            dst[rs, cs] = v


def _mix_fwd(z, conv_a, lnv, wm, bias_full, *, name, tm=256, phases=()):
    m, d9 = z.shape
    d = d9 // 9
    tm = _row_tile(m, tm)
    grp = d // len(POOL_WINDOWS)

    def body(zb, zc, zx, zu, zv, zp, zc_h, zx_h, zp_h, ca_ref, lnv_ref, wm_ref, bias_ref,
             ua_ref, ub_ref, d_ref, ext, vn_s, mixed_s):
        i = pl.program_id(0)
        first = i == 0
        pa = zc[...] * zx[...]
        ext[0:HALO, :] = jnp.where(first, 0.0, zc_h[...] * zx_h[...])
        ext[HALO:HALO + tm, :] = pa
        w = ca_ref[...]
        conv = w[0:1, :] * ext[pl.ds(HALO - 2, tm), :] + w[1:2, :] * ext[pl.ds(HALO - 1, tm), :] + w[2:3, :] * pa
        ua_ref[...] = (zb[...] * conv).astype(BF)
        p = zp[...]
        ext[0:HALO, :] = jnp.where(first, 0.0, zp_h[...])
        ext[HALO:HALO + tm, :] = p
        t = (i * tm + lax.broadcasted_iota(jnp.int32, (tm, 1), 0) + 1).astype(F32)
        for k, win in enumerate(POOL_WINDOWS):
            cs = slice(k * grp, (k + 1) * grp)
            s = p[:, cs]
            for j in range(1, win):
                s = s + ext[pl.ds(HALO - j, tm), cs]
            d_ref[:, cs] = (s / jnp.minimum(t, float(win)) - p[:, cs]).astype(BF)
        gv = _gelu(zv[...])
        vhat, _ = _ln_stats(gv)
        vn_s[...] = (vhat * lnv_ref[0:1, :] + lnv_ref[1:2, :]).astype(BF)
        _spatial_mix(wm_ref, vn_s, mixed_s, bias_ref, tm, d)
        ub_ref[...] = (_gelu(zu[...]) * mixed_s[...]).astype(BF)

    full = lambda a: pl.BlockSpec(a.shape, lambda i: (0,) * a.ndim)
    out = jax.ShapeDtypeStruct((m, d), BF)
    o_spec = pl.BlockSpec((tm, d), lambda i: (i, 0))
    return _pcall(
        body, (z, z, z, z, z, z, z, z, z, conv_a, lnv, wm, bias_full), name=name, grid=(m // tm,),
        in_specs=[_seg_spec(tm, d, s) for s in range(6)] + [_prev_halo_spec(tm, d, s) for s in (1, 2, 5)]
        + [full(conv_a), full(lnv), full(wm), full(bias_full)],
        out_specs=[o_spec, o_spec, o_spec], out_shape=[out, out, out],
        scratch_shapes=[pltpu.VMEM((HALO + tm, d), F32), pltpu.VMEM((tm, d), BF), pltpu.VMEM((tm, d), F32)],
        sem=("arbitrary",), phases=phases)


def _pool_proj(dd, w_pool, *, dn, name, out_dtype=F32, tm=512):
    m, d = dd.shape
    ng, grp, _ = w_pool.shape
    tm = _row_tile(m, tm)
    return _matmul(
        dd, w_pool, dn=dn, grid=(m // tm, ng, 1), name=name,
        a_spec=pl.BlockSpec((tm, grp), lambda i, j, kk: (i, j)),
        b_spec=pl.BlockSpec((None, grp, grp), lambda i, j, kk: (j, 0, 0)),
        o_spec=pl.BlockSpec((tm, grp), lambda i, j, kk: (i, j)),
        out_shape=jax.ShapeDtypeStruct((m, d), out_dtype), acc_shape=(tm, grp))


def _merge(z, ya, yb, ycp, scale, *, name, tm=512):
    m, d = ya.shape
    tm = _row_tile(m, tm)

    def body(ga, gb, gc, ya_ref, yb_ref, yc_ref, sc_ref, o_ref):
        o_ref[...] = (jax.nn.sigmoid(ga[...]) * ya_ref[...] + jax.nn.sigmoid(gb[...]) * yb_ref[...]
                      + jax.nn.sigmoid(gc[...]) * (yc_ref[...] * sc_ref[...])).astype(BF)

    row = pl.BlockSpec((tm, d), lambda i: (i, 0))
    return pl.pallas_call(
        body, name=name, grid=(m // tm,),
        in_specs=[_seg_spec(tm, d, 6), _seg_spec(tm, d, 7), _seg_spec(tm, d, 8), row, row, row,
                  pl.BlockSpec((1, d), lambda i: (0, 0))],
        out_specs=row, out_shape=jax.ShapeDtypeStruct((m, d), BF),
        compiler_params=_params(("parallel",)),
    )(z, z, z, ya, yb, ycp, scale)


def _resid_ln(xp, ys, vec, alpha, *, name, tm=512):
    m, d = xp.shape
    tm = _row_tile(m, tm)

    def body(xp_ref, ys_ref, v_ref, o_ref):
        xhat, _ = _ln_stats(alpha * xp_ref[...] + v_ref[0:1, :] * ys_ref[...])
        o_ref[...] = xhat * v_ref[1:2, :] + v_ref[2:3, :]

    row = pl.BlockSpec((tm, d), lambda i: (i, 0))
    return pl.pallas_call(
        body, name=name, grid=(m // tm,),
        in_specs=[row, row, pl.BlockSpec(vec.shape, lambda i: (0, 0))],
        out_specs=row, out_shape=jax.ShapeDtypeStruct((m, d), F32),
        compiler_params=_params(("parallel",)),
    )(xp, ys, vec)


def _ffn_fwd(up4, cw, cb, *, name, tm=512, phases=()):
    _, nj, m, fs = up4.shape
    tm = _row_tile(m, tm)
    hb = tm // HALO

    def body(up_ref, ah_ref, cw_ref, cb_ref, f_ref, ext):
        first = pl.program_id(1) == 0
        a = up_ref[0]
        ext[0:HALO, :] = jnp.where(first, 0.0, ah_ref[...])
        ext[HALO:HALO + tm, :] = a
        w = cw_ref[...]
        ca = (w[0:1, :] * ext[pl.ds(HALO - 2, tm), :] + w[1:2, :] * ext[pl.ds(HALO - 1, tm), :]
              + w[2:3, :] * a + cb_ref[...])
        f_ref[...] = (_gelu(ca) * up_ref[1]).astype(BF)

    return _pcall(
        body, (up4, up4, cw, cb), name=name, grid=(nj, m // tm),
        in_specs=[pl.BlockSpec((2, None, tm, fs), lambda j, i: (0, j, i, 0)),
                  pl.BlockSpec((None, None, HALO, fs), lambda j, i: (0, j, jnp.maximum(i * hb - 1, 0), 0)),
                  pl.BlockSpec((None, 3, fs), lambda j, i: (j, 0, 0)),
                  pl.BlockSpec((None, 1, fs), lambda j, i: (j, 0, 0))],
        out_specs=pl.BlockSpec((None, tm, fs), lambda j, i: (j, i, 0)),
        out_shape=jax.ShapeDtypeStruct((nj, m, fs), BF),
        scratch_shapes=[pltpu.VMEM((HALO + tm, fs), F32)], sem=("parallel", "arbitrary"), phases=phases)


def _down_proj(f4, wd4, *, name, tm=512):
    nj, m, fs = f4.shape
    d = wd4.shape[2]
    tm = _row_tile(m, tm)
    return _matmul(
        f4, wd4, dn=NN, grid=(m // tm, 1, nj), name=name,
        a_spec=pl.BlockSpec((None, tm, fs), lambda i, j, kk: (kk, i, 0)),
        b_spec=pl.BlockSpec((None, fs, d), lambda i, j, kk: (kk, 0, 0)),
        o_spec=pl.BlockSpec((tm, d), lambda i, j, kk: (i, 0)),
        out_shape=jax.ShapeDtypeStruct((m, d), F32), acc_shape=(tm, d))


def _loss_grad(y, tgt, *, name, tm=512):
    m, d = y.shape
    tm = _row_tile(m, tm)
    ni = m // tm

    def body(y_ref, t_ref, dy_ref, l_ref, acc):
        i = pl.program_id(0)
        e = y_ref[...] - t_ref[...]
        dy_ref[...] = e * (1.0 / d)
        part = jnp.sum((e * e).reshape(tm // 8, 8, d), axis=0)

        @pl.when(i == 0)
        def _():
            acc[...] = part

        @pl.when(i > 0)
        def _():
            acc[...] += part

        @pl.when(i == ni - 1)
        def _():
            l_ref[...] = jnp.full((8, 128), 0.5 / d, F32) * jnp.sum(acc[...])

    row = pl.BlockSpec((tm, d), lambda i: (i, 0))
    return pl.pallas_call(
        body, name=name, grid=(ni,), in_specs=[row, row],
        out_specs=[row, pl.BlockSpec((8, 128), lambda i: (0, 0))],
        out_shape=[jax.ShapeDtypeStruct((m, d), F32), jax.ShapeDtypeStruct((8, 128), F32)],
        scratch_shapes=[pltpu.VMEM((8, d), F32)],
        compiler_params=_params(("arbitrary",)),
    )(y, tgt)


def _resid_ln_bwd(dpart, dh, xmod, mvec, xp, ys, vec, alpha, *, name, tm=256, phases=()):
    m, d = dpart.shape
    tm = _row_tile(m, tm)
    has_dh = dh is not None
    has_ln = xp is not None

    def body(*refs):
        refs = list(refs)
        dpart_ref = refs.pop(0)
        if has_dh:
            dh_ref, xm_ref, mv_ref = refs.pop(0), refs.pop(0), refs.pop(0)
        if has_ln:
            xp_ref, ys_ref, v_ref = refs.pop(0), refs.pop(0), refs.pop(0)
            dys_ref, dxp_ref, red_ref = refs
        else:
            dx_ref, red_ref = refs
        i = pl.program_id(0)
        dtot = dpart_ref[...]
        rows = [jnp.zeros((1, d), F32)] * 5
        if has_dh:
            dhv = dh_ref[...]
            dtot = dtot + dhv * mv_ref[...]
            rows[0] = _rowsum(dhv * xm_ref[...])
            rows[1] = _rowsum(dhv)
        if has_ln:
            ys = ys_ref[...]
            gt = v_ref[0:1, :]
            xhat, rstd = _ln_stats(alpha * xp_ref[...] + gt * ys)
            rows[2] = _rowsum(dtot * xhat)
            rows[3] = _rowsum(dtot)
            dr = _ln_bwd(dtot, xhat, rstd, v_ref[1:2, :])
            rows[4] = _rowsum(dr * ys)
            dys_ref[...] = (dr * gt).astype(BF)
            dxp_ref[...] = alpha * dr
        else:
            dx_ref[...] = dtot
        red = jnp.concatenate(rows + [jnp.zeros((3, d), F32)], axis=0)

        @pl.when(i == 0)
        def _():
            red_ref[...] = red

        @pl.when(i > 0)
        def _():
            red_ref[...] += red

    row = pl.BlockSpec((tm, d), lambda i: (i, 0))
    vrow = lambda a: pl.BlockSpec(a.shape, lambda i: (0, 0))
    args, specs = [dpart], [row]
    if has_dh:
        args += [dh, xmod, mvec]
        specs += [row, row, vrow(mvec)]
    if has_ln:
        args += [xp, ys, vec]
        specs += [row, row, vrow(vec)]
        out_specs = [row, row, pl.BlockSpec((8, d), lambda i: (0, 0))]
        out_shape = [jax.ShapeDtypeStruct((m, d), BF), jax.ShapeDtypeStruct((m, d), F32),
                     jax.ShapeDtypeStruct((8, d), F32)]
    else:
        out_specs = [row, pl.BlockSpec((8, d), lambda i: (0, 0))]
        out_shape = [jax.ShapeDtypeStruct((m, d), F32), jax.ShapeDtypeStruct((8, d), F32)]
    return _pcall(body, args, name=name, grid=(m // tm,), in_specs=specs, out_specs=out_specs, out_shape=out_shape,
                  sem=("arbitrary",), phases=phases)


def _down_bwd(dy, wd4, *, name, tm=512, phases=()):
    m, d = dy.shape
    nj, fs, _ = wd4.shape
    tm = _row_tile(m, tm)
    return _matmul(
        dy, wd4, dn=NT, grid=(m // tm, nj, 1), name=name,
        a_spec=pl.BlockSpec((tm, d), lambda i, j, kk: (i, 0)),
        b_spec=pl.BlockSpec((None, fs, d), lambda i, j, kk: (j, 0, 0)),
        o_spec=pl.BlockSpec((None, tm, fs), lambda i, j, kk: (j, i, 0)),
        out_shape=jax.ShapeDtypeStruct((nj, m, fs), F32), acc_shape=(tm, fs), phases=phases)


def _tn_shards_lhs(f4, dy, *, name, tk=512):
    nj, m, fs = f4.shape
    d = dy.shape[1]
    tk = _row_tile(m, tk)
    return _matmul(
        f4, dy, dn=TN, grid=(nj, 1, m // tk), name=name,
        a_spec=pl.BlockSpec((None, tk, fs), lambda i, j, kk: (i, kk, 0)),
        b_spec=pl.BlockSpec((tk, d), lambda i, j, kk: (kk, 0)),
        o_spec=pl.BlockSpec((None, fs, d), lambda i, j, kk: (i, 0, 0)),
        out_shape=jax.ShapeDtypeStruct((nj, fs, d), F32), acc_shape=(fs, d))


def _tn_shards_rhs(h, d8, *, name, tk=512, phases=()):
    m, k = h.shape
    nb, _, ns = d8.shape
    tk = _row_tile(m, tk)
    return _matmul(
        h, d8, dn=TN, grid=(nb, 1, m // tk), name=name,
        a_spec=pl.BlockSpec((tk, k), lambda i, j, kk: (kk, 0)),
        b_spec=pl.BlockSpec((None, tk, ns), lambda i, j, kk: (i, kk, 0)),
        o_spec=pl.BlockSpec((None, k, ns), lambda i, j, kk: (i, 0, 0)),
        out_shape=jax.ShapeDtypeStruct((nb, k, ns), F32), acc_shape=(k, ns), phases=phases)


def _tn_cols_rhs(h, dz, nb, *, name, tk=512, phases=()):
    m, k = h.shape
    ns = dz.shape[1] // nb
    tk = _row_tile(m, tk)
    return _matmul(
        h, dz, dn=TN, grid=(nb, 1, m // tk), name=name,
        a_spec=pl.BlockSpec((tk, k), lambda i, j, kk: (kk, 0)),
        b_spec=pl.BlockSpec((tk, ns), lambda i, j, kk: (kk, i)),
        o_spec=pl.BlockSpec((None, k, ns), lambda i, j, kk: (i, 0, 0)),
        out_shape=jax.ShapeDtypeStruct((nb, k, ns), F32), acc_shape=(k, ns), phases=phases)


def _nt_shards(d8, w8, *, name, tm=512, phases=()):
    nb, m, ns = d8.shape
    k = w8.shape[1]
    tm = _row_tile(m, tm)
    return _matmul(
        d8, w8, dn=NT, grid=(m // tm, 1, nb), name=name,
        a_spec=pl.BlockSpec((None, tm, ns), lambda i, j, kk: (kk, i, 0)),
        b_spec=pl.BlockSpec((None, k, ns), lambda i, j, kk: (kk, 0, 0)),
        o_spec=pl.BlockSpec((tm, k), lambda i, j, kk: (i, 0)),
        out_shape=jax.ShapeDtypeStruct((m, k), F32), acc_shape=(tm, k), phases=phases)


def _nt_cols(dz, w8, *, name, tm=512, phases=()):
    m = dz.shape[0]
    nb, k, ns = w8.shape
    tm = _row_tile(m, tm)
    return _matmul(
        dz, w8, dn=NT, grid=(m // tm, 1, nb), name=name,
        a_spec=pl.BlockSpec((tm, ns), lambda i, j, kk: (i, kk)),
        b_spec=pl.BlockSpec((None, k, ns), lambda i, j, kk: (kk, 0, 0)),
        o_spec=pl.BlockSpec((tm, k), lambda i, j, kk: (i, 0)),
        out_shape=jax.ShapeDtypeStruct((m, k), F32), acc_shape=(tm, k), phases=phases)


def _tn_pool(dd, dyc, ng, *, name, tk=512):
    m, d = dd.shape
    grp = d // ng
    tk = _row_tile(m, tk)
    return _matmul(
        dd, dyc, dn=TN, grid=(ng, 1, m // tk), name=name,
        a_spec=pl.BlockSpec((tk, grp), lambda i, j, kk: (kk, i)),
        b_spec=pl.BlockSpec((tk, grp), lambda i, j, kk: (kk, i)),
        o_spec=pl.BlockSpec((None, grp, grp), lambda i, j, kk: (i, 0, 0)),
        out_shape=jax.ShapeDtypeStruct((ng, grp, grp), F32), acc_shape=(grp, grp))


def _ffn_bwd(up4, df4, cw, cb, *, name, tm=256, phases=()):
    _, nj, m, fs = up4.shape
    tm = _row_tile(m, tm)
    hb = tm // HALO
    ni = m // tm
    last_hb = m // HALO - 1
    ext_rows = tm + 8

    def body(up_ref, ap_ref, un_ref, df_ref, dfn_ref, cw_ref, cb_ref, dup_ref, red_ref, ext, dca_s):
        i = pl.program_id(1)
        a = up_ref[0]
        g = up_ref[1]
        df = df_ref[...]
        ext[0:HALO, :] = jnp.where(i == 0, 0.0, ap_ref[...])
        ext[HALO:HALO + tm, :] = a
        ext[HALO + tm:2 * HALO + tm, :] = un_ref[0]
        w = cw_ref[...]
        w0, w1, w2 = w[0:1, :], w[1:2, :], w[2:3, :]
        a1 = ext[pl.ds(HALO - 1, ext_rows), :]
        a2 = ext[pl.ds(HALO - 2, ext_rows), :]
        cae = w0 * a2 + w1 * a1 + w2 * ext[pl.ds(HALO, ext_rows), :] + cb_ref[...]
        act, dact = _gelu_parts(cae)
        dfe = jnp.concatenate([df, dfn_ref[0:8, :]], axis=0)
        ge = jnp.concatenate([g, un_ref[1][0:8, :]], axis=0)
        row = lax.broadcasted_iota(jnp.int32, (ext_rows, 1), 0)
        dcae = jnp.where((row < tm) | (i < ni - 1), dfe * ge * dact, 0.0)
        dca_s[...] = dcae
        dca = dcae[0:tm, :]
        dup_a = w2 * dca + w1 * dca_s[pl.ds(1, tm), :] + w0 * dca_s[pl.ds(2, tm), :]
        dup_g = df * act[0:tm, :]
        dup_ref[0] = dup_a.astype(BF)
        dup_ref[1] = dup_g.astype(BF)
        red = jnp.concatenate([
            _rowsum(dca * a2[0:tm, :]), _rowsum(dca * a1[0:tm, :]), _rowsum(dca * a), _rowsum(dca),
            _rowsum(dup_a), _rowsum(dup_g), jnp.zeros((2, fs), F32)], axis=0)

        @pl.when(i == 0)
        def _():
            red_ref[...] = red

        @pl.when(i > 0)
        def _():
            red_ref[...] += red

    nxt = lambda j, i: jnp.minimum((i + 1) * hb, last_hb)
    return _pcall(
        body, (up4, up4, up4, df4, df4, cw, cb), name=name, grid=(nj, ni), sem=("parallel", "arbitrary"), phases=phases,
        in_specs=[pl.BlockSpec((2, None, tm, fs), lambda j, i: (0, j, i, 0)),
                  pl.BlockSpec((None, None, HALO, fs), lambda j, i: (0, j, jnp.maximum(i * hb - 1, 0), 0)),
                  pl.BlockSpec((2, None, HALO, fs), lambda j, i: (0, j, nxt(j, i), 0)),
                  pl.BlockSpec((None, tm, fs), lambda j, i: (j, i, 0)),
                  pl.BlockSpec((None, HALO, fs), lambda j, i: (j, nxt(j, i), 0)),
                  pl.BlockSpec((None, 3, fs), lambda j, i: (j, 0, 0)),
                  pl.BlockSpec((None, 1, fs), lambda j, i: (j, 0, 0))],
        out_specs=[pl.BlockSpec((2, None, tm, fs), lambda j, i: (0, j, i, 0)),
                   pl.BlockSpec((None, 8, fs), lambda j, i: (j, 0, 0))],
        out_shape=[jax.ShapeDtypeStruct((2, nj, m, fs), BF), jax.ShapeDtypeStruct((nj, 8, fs), F32)],
        scratch_shapes=[pltpu.VMEM((2 * HALO + tm, fs), F32), pltpu.VMEM((ext_rows, fs), F32)])


def _gate_bwd(dm, z, ya, yb, ycp, scale, *, name, tm=256):
    m, d = dm.shape
    tm = _row_tile(m, tm)

    def body(dm_ref, ga, gb, gc, ya_ref, yb_ref, yc_ref, sc_ref, dya_ref, dyb_ref, dyc_ref, dz_ref, red_ref):
        i = pl.program_id(0)
        dmv = dm_ref[...]
        sa, sb, sc = jax.nn.sigmoid(ga[...]), jax.nn.sigmoid(gb[...]), jax.nn.sigmoid(gc[...])
        scale_v = sc_ref[...]
        ycp_v = yc_ref[...]
        dya_ref[...] = (dmv * sa).astype(BF)
        dyb_ref[...] = (dmv * sb).astype(BF)
        dyc = dmv * sc
        dyc_ref[...] = (dyc * scale_v).astype(BF)
        dga = dmv * ya_ref[...] * (sa * (1.0 - sa))
        dgb = dmv * yb_ref[...] * (sb * (1.0 - sb))
        dgc = dmv * (ycp_v * scale_v) * (sc * (1.0 - sc))
        dz_ref[:, 0:d] = dga.astype(BF)
        dz_ref[:, d:2 * d] = dgb.astype(BF)
        dz_ref[:, 2 * d:3 * d] = dgc.astype(BF)
        red = jnp.concatenate([_rowsum(dyc * ycp_v), _rowsum(dga), _rowsum(dgb), _rowsum(dgc),
                               jnp.zeros((4, d), F32)], axis=0)

        @pl.when(i == 0)
        def _():
            red_ref[...] = red

        @pl.when(i > 0)
        def _():
            red_ref[...] += red

    row = pl.BlockSpec((tm, d), lambda i: (i, 0))
    obf = jax.ShapeDtypeStruct((m, d), BF)
    return pl.pallas_call(
        body, name=name, grid=(m // tm,),
        in_specs=[row, _seg_spec(tm, d, 6), _seg_spec(tm, d, 7), _seg_spec(tm, d, 8), row, row, row,
                  pl.BlockSpec((1, d), lambda i: (0, 0))],
        out_specs=[row, row, row, pl.BlockSpec((tm, 3 * d), lambda i: (i, 2)), pl.BlockSpec((8, d), lambda i: (0, 0))],
        out_shape=[obf, obf, obf, jax.ShapeDtypeStruct((m, 9 * d), BF), jax.ShapeDtypeStruct((8, d), F32)],
        compiler_params=_params(("arbitrary",)),
    )(dm, z, z, z, ya, yb, ycp, scale)


def _mix_bwd(dz, dua, dub, ddd, z, conv_a, lnv, wm, wmt, bias_full, mask, *, name, tm=128, phases=()):
    m, d = dua.shape
    tm = _row_tile(m, tm)
    ni = m // tm
    grp = d // len(POOL_WINDOWS)
    ng = d // GMLP_BLOCK
    ext_rows = tm + 8

    def body(dz_in, dua_ref, dub_ref, dd_ref, zb, zc, zx, zu, zv, zp, zc_h, zx_h, dua_n, zb_n, dd_n,
             ca_ref, lnv_ref, wm_ref, wmt_ref, bias_ref, mask_ref,
             dz_ref, red_ref, dws_ref, dbs_ref, ext, sh_s, vn_s, mixed_s, dmx_s, dvn_s, dbs_acc):
        del dz_in
        i = pl.program_id(0)
        rows = []
        zbv, zcv, zxv = zb[...], zc[...], zx[...]
        pa = zcv * zxv
        ext[0:HALO, :] = jnp.where(i == 0, 0.0, zc_h[...] * zx_h[...])
        ext[HALO:HALO + tm, :] = pa
        w = ca_ref[...]
        w0, w1, w2 = w[0:1, :], w[1:2, :], w[2:3, :]
        p1 = ext[pl.ds(HALO - 1, tm), :]
        p2 = ext[pl.ds(HALO - 2, tm), :]
        conv = w0 * p2 + w1 * p1 + w2 * pa
        duav = dua_ref[...]
        dzb = duav * conv
        dca = duav * zbv
        dca_n = jnp.where(i < ni - 1, dua_n[0:8, :] * zb_n[0:8, :], 0.0)
        sh_s[0:tm, :] = dca
        sh_s[tm:tm + 8, :] = dca_n
        dpa = w2 * dca + w1 * sh_s[pl.ds(1, tm), :] + w0 * sh_s[pl.ds(2, tm), :]
        dzc = dpa * zxv
        dzx = dpa * zcv
        dz_ref[:, 0:d] = dzb.astype(BF)
        dz_ref[:, d:2 * d] = dzc.astype(BF)
        dz_ref[:, 2 * d:3 * d] = dzx.astype(BF)
        rows += [_rowsum(dzb), _rowsum(dzc), _rowsum(dzx)]
        dconv = [_rowsum(dca * p2), _rowsum(dca * p1), _rowsum(dca * pa)]
        zuv, zvv = zu[...], zv[...]
        gu, dgu_dz = _gelu_parts(zuv)
        gv, dgv_dz = _gelu_parts(zvv)
        vhat, rstd = _ln_stats(gv)
        gain = lnv_ref[0:1, :]
        vn_s[...] = (vhat * gain + lnv_ref[1:2, :]).astype(BF)
        _spatial_mix(wm_ref, vn_s, mixed_s, bias_ref, tm, d)
        dubv = dub_ref[...]
        dzu = dubv * mixed_s[...] * dgu_dz
        dmixed = dubv * gu
        dmx_s[...] = dmixed.astype(BF)
        _spatial_mix(wmt_ref, dmx_s, dvn_s, None, tm, d)
        dvn = dvn_s[...]
        dzv = _ln_bwd(dvn, vhat, rstd, gain) * dgv_dz
        dz_ref[:, 3 * d:4 * d] = dzu.astype(BF)
        dz_ref[:, 4 * d:5 * d] = dzv.astype(BF)
        rows += [_rowsum(dzu), _rowsum(dzv)]
        dlnv = [_rowsum(dvn * vhat), _rowsum(dvn)]
        dbs_part = dmixed[0:GMLP_BLOCK, :]
        for n in range(1, tm // GMLP_BLOCK):
            dbs_part = dbs_part + dmixed[n * GMLP_BLOCK:(n + 1) * GMLP_BLOCK, :]
        ddv = dd_ref[...]
        t = (i * tm + lax.broadcasted_iota(jnp.int32, (ext_rows + 8, 1), 0) + 1).astype(F32)
        dde = jnp.concatenate([ddv, jnp.where(i < ni - 1, dd_n[...], 0.0)], axis=0)
        for k, win in enumerate(POOL_WINDOWS):
            cs = slice(k * grp, (k + 1) * grp)
            ext[0:tm + HALO, cs] = dde[:, cs] / jnp.minimum(t, float(win))
        dzp_parts = []
        for k, win in enumerate(POOL_WINDOWS):
            cs = slice(k * grp, (k + 1) * grp)
            s = ext[0:tm, cs]
            for j in range(1, win):
                s = s + ext[pl.ds(j, tm), cs]
            dzp_parts.append(s - ddv[:, cs])
        dzp = jnp.concatenate(dzp_parts, axis=1)
        dz_ref[:, 5 * d:6 * d] = dzp.astype(BF)
        rows += [_rowsum(dzp)]
        red = jnp.concatenate(rows + dconv + dlnv + [jnp.zeros((5, d), F32)], axis=0)

        @pl.when(i == 0)
        def _():
            red_ref[...] = red
            dbs_acc[...] = dbs_part
            dws_ref[...] = jnp.zeros_like(dws_ref)

        @pl.when(i > 0)
        def _():
            red_ref[...] += red
            dbs_acc[...] += dbs_part

        for n in range(tm // GMLP_BLOCK):
            for g in range(ng):
                rs = slice(n * GMLP_BLOCK, (n + 1) * GMLP_BLOCK)
                cs = slice(g * GMLP_BLOCK, (g + 1) * GMLP_BLOCK)
                dws_ref[g] += mask_ref[...] * lax.dot_general(
                    dmx_s[rs, cs], vn_s[rs, cs], (NT, ((), ())), preferred_element_type=F32)

        @pl.when(i == ni - 1)
        def _():
            lane = lax.broadcasted_iota(jnp.int32, (GMLP_BLOCK, GMLP_BLOCK), 1)
            out = jnp.zeros((GMLP_BLOCK, GMLP_BLOCK), F32)
            for g in range(ng):
                sg = jnp.sum(dbs_acc[:, g * GMLP_BLOCK:(g + 1) * GMLP_BLOCK], axis=1, keepdims=True)
                out = out + jnp.where(lane == g, sg, 0.0)
            dbs_ref[...] = out

    row = pl.BlockSpec((tm, d), lambda i: (i, 0))
    full = lambda a: pl.BlockSpec(a.shape, lambda i: (0,) * a.ndim)
    hb = tm // HALO
    last_hb = m // HALO - 1
    nrow = pl.BlockSpec((HALO, d), lambda i: (jnp.minimum((i + 1) * hb, last_hb), 0))
    return _pcall(
        body, (dz, dua, dub, ddd, z, z, z, z, z, z, z, z, dua, z, ddd, conv_a, lnv, wm, wmt, bias_full, mask),
        name=name, grid=(ni,), sem=("arbitrary",), aliases={0: 0}, phases=phases,
        in_specs=[pl.BlockSpec(memory_space=pl.ANY), row, row, row]
        + [_seg_spec(tm, d, s) for s in range(6)]
        + [_prev_halo_spec(tm, d, 1), _prev_halo_spec(tm, d, 2), nrow, _next_halo_spec(tm, d, 0, m), nrow]
        + [full(conv_a), full(lnv), full(wm), full(wmt), full(bias_full), full(mask)],
        out_specs=[pl.BlockSpec((tm, 6 * d), lambda i: (i, 0)), pl.BlockSpec((16, d), lambda i: (0, 0)),
                   full(wm), pl.BlockSpec((GMLP_BLOCK, GMLP_BLOCK), lambda i: (0, 0))],
        out_shape=[jax.ShapeDtypeStruct(dz.shape, BF), jax.ShapeDtypeStruct((16, d), F32),
                   jax.ShapeDtypeStruct(wm.shape, F32), jax.ShapeDtypeStruct((GMLP_BLOCK, GMLP_BLOCK), F32)],
        scratch_shapes=[pltpu.VMEM((2 * HALO + tm, d), F32), pltpu.VMEM((tm + 8, d), F32),
                        pltpu.VMEM((tm, d), BF), pltpu.VMEM((tm, d), F32), pltpu.VMEM((tm, d), BF),
                        pltpu.VMEM((tm, d), F32), pltpu.VMEM((GMLP_BLOCK, d), F32)])


REST = ("w_a_out", "w_b_out", "w_pool", "w_o", "w_up", "w_down")


def _remote(src, dst, ssem, rsem, k, to):
    return pltpu.make_async_remote_copy(src_ref=src, dst_ref=dst, send_sem=ssem.at[k], recv_sem=rsem.at[k],
                                        device_id=to, device_id_type=MESH)


def _gather_phase1(shards):
    n = len(shards)

    def build(ins, outs, ssem, rsem, lsem):
        x, y, c, chips = _place()
        me = 4 * x + 2 * y + c
        local = [pltpu.make_async_copy(ins[a], outs[a].at[:, me], lsem.at[a]) for a in range(n)]
        sends, recvs = [], []
        for j, (cx, cy) in enumerate(chips):
            for a in range(n):
                sends.append(_remote(ins[a], outs[a].at[:, me], ssem, rsem, 4 * a + 1 + j, (cx, cy, c)))
                recvs.append(_remote(ins[a], outs[a].at[:, 4 * cx + 2 * cy + c], ssem, rsem, 4 * a + 1 + j, (cx, cy, c)))
        for a in range(n):
            sends.append(_remote(ins[a], outs[a].at[:, me], ssem, rsem, 4 * a, (x, y, 1 - c)))
            recvs.append(_remote(ins[a], outs[a].at[:, 4 * x + 2 * y + 1 - c], ssem, rsem, 4 * a, (x, y, 1 - c)))
        return dict(start=local + sends, recv=recvs, send=sends, local=local)

    outs = [jax.ShapeDtypeStruct((s.shape[0], N_DEV) + s.shape[1:], s.dtype) for s in shards]
    return _Phase(shards, outs, {}, 4 * n, n, build)


def _gather_phase2(fulls):
    n = len(fulls)

    def build(ins, outs, ssem, rsem, lsem):
        x, y, c, chips = _place()
        sends, recvs = [], []
        for j, (cx, cy) in enumerate(chips):
            for a in range(n):
                mine, theirs = 4 * cx + 2 * cy + c, 4 * cx + 2 * cy + 1 - c
                sends.append(_remote(ins[a].at[:, mine], outs[a].at[:, mine], ssem, rsem, 3 * a + j, (x, y, 1 - c)))
                recvs.append(_remote(ins[a].at[:, theirs], outs[a].at[:, theirs], ssem, rsem, 3 * a + j, (x, y, 1 - c)))
        return dict(start=sends, recv=recvs, send=sends, local=[])

    outs = [jax.ShapeDtypeStruct(f.shape, f.dtype) for f in fulls]
    return _Phase(fulls, outs, {a: a for a in range(n)}, 3 * n, 0, build)


def _pair_phase(grads):
    n = len(grads)

    def build(ins, outs, ssem, rsem, lsem):
        x, y, c, _ = _place()
        cps = [_remote(ins[a].at[:, 2 * q + (1 - c)], outs[a].at[q], ssem, rsem, 4 * a + q, (x, y, 1 - c))
               for a in range(n) for q in range(4)]
        return dict(start=cps, recv=cps, send=cps, local=[])

    outs = [jax.ShapeDtypeStruct((4, g.shape[0]) + g.shape[2:], g.dtype) for g in grads]
    return _Phase(grads, outs, {}, 4 * n, 0, build)


def _chip_phase(bufs, accs, l, depth):
    n = len(bufs)
    has = accs is not None

    def build(ins, outs, ssem, rsem, lsem):
        x, y, c, chips = _place()
        myq = 2 * x + y
        local, sends, recvs = [], [], []
        for a in range(n):
            local.append(pltpu.make_async_copy(ins[a].at[myq], outs[a].at[myq, l], lsem.at[a]))
            for j, (cx, cy) in enumerate(chips):
                q = 2 * cx + cy
                sends.append(_remote(ins[a].at[q], outs[a].at[myq, l], ssem, rsem, 3 * a + j, (cx, cy, c)))
                recvs.append(_remote(ins[a].at[q], outs[a].at[q, l], ssem, rsem, 3 * a + j, (cx, cy, c)))
        return dict(start=local + sends, recv=recvs, send=sends, local=local)

    outs = [jax.ShapeDtypeStruct((4, depth) + b.shape[1:], b.dtype) for b in bufs]
    return _Phase(list(bufs) + (list(accs) if has else []), outs, {n + a: a for a in range(n)} if has else {},
                  3 * n, n, build)


def _grad_chunks(n, g):
    if n == "w_pool":
        return g.reshape(g.shape[0], N_DEV, g.shape[1] // N_DEV, g.shape[2])
    if n in ("w_in", "w_up"):
        return g[None]
    return g.reshape(1, N_DEV, -1, g.shape[-1])


class _ReduceScatter:
    def __init__(self, depth, cidx):
        self.depth, self.cidx, self.acc, self.count = depth, cidx, {}, 0

    def pair(self, names, grads):
        return _pair_phase([_grad_chunks(n, grads[n]) for n in names])

    def sums(self, names, grads, phase):
        out = []
        for n, r1 in zip(names, phase.results):
            out.append(_pair_sum(_grad_chunks(n, grads[n]), r1, self.cidx, name="rs_sum_%d" % self.count))
            self.count += 1
        return out

    def chip(self, names, bufs, l):
        accs = [self.acc[n] for n in names] if names[0] in self.acc else None
        return _chip_phase(bufs, accs, l, self.depth)

    def done(self, names, phase):
        for n, r in zip(names, phase.results):
            self.acc[n] = r


def _rest_views(fulls, d):
    a_out, b_out, pool, o, up, down = fulls
    grp = d // len(POOL_WINDOWS)
    return dict(w_a_out=a_out.reshape(d, d), w_b_out=b_out.reshape(d, d), w_o=o.reshape(d, d),
                w_pool=pool.reshape(len(POOL_WINDOWS), grp, grp), w_up8=up[0],
                wd4=down.reshape(N_DEV // 2, -1, d))


def _layer_fwd(x, w, alpha, tag, rest_shards=None, next_in_shard=None):
    d = x.shape[1]
    g1 = _gather_phase1(rest_shards) if rest_shards is not None else None
    z, h = _mod_matmul(x, w["mod1"], w["w_in8"], w["b_in8"], flat_out=True, name="in_proj" + tag,
                       phases=[g1] if g1 else ())
    g2 = _gather_phase2(g1.results) if g1 else None
    ua, ub, dd = _mix_fwd(z, w["conv_a"], w["lnv"], w["wm"], w["bias_full"], name="mix_fwd" + tag,
                          phases=[g2] if g2 else ())
    if g2:
        w.update(_rest_views(g2.results, d))
    ya = _mm_rows(ua, w["w_a_out"], dn=NN, name="a_out" + tag)
    yb = _mm_rows(ub, w["w_b_out"], dn=NN, name="b_out" + tag)
    ycp = _pool_proj(dd, w["w_pool"], dn=NN, name="pool_proj" + tag)
    merged = _merge(z, ya, yb, ycp, w["pool_scale"], name="merge" + tag)
    o = _mm_rows(merged, w["w_o"], dn=NN, name="o_proj" + tag)
    x1 = _resid_ln(x, o, w["ln1"], alpha, name="ln1" + tag)
    n1 = _gather_phase1([next_in_shard]) if next_in_shard is not None else None
    up8, h2 = _mod_matmul(x1, w["mod2"], w["w_up8"], w["b_up8"], flat_out=False, name="up_proj" + tag,
                          phases=[n1] if n1 else ())
    up4 = up8.reshape((2, up8.shape[0] // 2) + up8.shape[1:])
    n2 = _gather_phase2(n1.results) if n1 else None
    f4 = _ffn_fwd(up4, w["cw"], w["cb"], name="ffn_fwd" + tag, phases=[n2] if n2 else ())
    y2 = _down_proj(f4, w["wd4"], name="down_proj" + tag)
    x2 = _resid_ln(x1, y2, w["ln2"], alpha, name="ln2" + tag)
    saved = dict(x=x, z=z, h=h, ua=ua, ub=ub, dd=dd, ya=ya, yb=yb, ycp=ycp, merged=merged, o=o, x1=x1,
                 up4=up4, h2=h2, f4=f4, y2=y2)
    return x2, saved, (n2.results[0][0] if n2 else None)


def _layer_bwd(dpart, dh_above, xmod_above, m_above, w, s, alpha, tag, l=0, above=None, rs=None):
    first, rest = ("w_in",), REST
    ph = lambda p: [p] if p is not None else ()
    r1a = rs.pair(first, above) if above else None
    dy2, dx1p, red2 = _resid_ln_bwd(dpart, dh_above, xmod_above, m_above, s["x1"], s["y2"], w["ln2"], alpha,
                                    name="ln2_bwd" + tag, phases=ph(r1a))
    r1b = rs.pair(rest, above) if above else None
    df4 = _down_bwd(dy2, w["wd4"], name="down_bwd" + tag, phases=ph(r1b))
    if above:
        sb_a, sb_b = rs.sums(first, above, r1a), rs.sums(rest, above, r1b)
    gw_down4 = _tn_shards_lhs(s["f4"], dy2, name="gw_down" + tag)
    r3a = rs.chip(first, sb_a, l + 1) if above else None
    dup4, redf = _ffn_bwd(s["up4"], df4, w["cw"], w["cb"], name="ffn_bwd" + tag, phases=ph(r3a))
    dup8 = dup4.reshape((dup4.shape[0] * dup4.shape[1],) + dup4.shape[2:])
    r3b = None
    if above:
        rs.done(first, r3a)
        r3b = rs.chip(rest, sb_b, l + 1)
    gw_up8 = _tn_shards_rhs(s["h2"], dup8, name="gw_up" + tag, phases=ph(r3b))
    if above:
        rs.done(rest, r3b)
    own = rs is not None and l == 0
    big = dict(w_up=gw_up8, w_down=gw_down4)
    early = ("w_down", "w_up")
    o1 = rs.pair(early, big) if own else None
    dh2 = _nt_shards(dup8, w["w_up8"], name="up_bwd" + tag, phases=ph(o1))
    if own:
        sb_o = rs.sums(early, big, o1)
    do, dxp, red1 = _resid_ln_bwd(dx1p, dh2, s["x1"], w["mod2"][0:1], s["x"], s["o"], w["ln1"], alpha,
                                  name="ln1_bwd" + tag)
    dm = _mm_rows(do, w["w_o"], dn=NT, name="o_bwd" + tag)
    big["w_o"] = _mm_tn(s["merged"], do, name="gw_o" + tag)
    dya, dyb, dyc, dz, redg = _gate_bwd(dm, s["z"], s["ya"], s["yb"], s["ycp"], w["pool_scale"], name="gate_bwd" + tag)
    dua = _mm_rows(dya, w["w_a_out"], dn=NT, name="a_out_bwd" + tag)
    dub = _mm_rows(dyb, w["w_b_out"], dn=NT, name="b_out_bwd" + tag)
    ddd = _pool_proj(dyc, w["w_pool"], dn=NT, name="pool_bwd" + tag)
    big["w_a_out"] = _mm_tn(s["ua"], dya, name="gw_a_out" + tag)
    big["w_b_out"] = _mm_tn(s["ub"], dyb, name="gw_b_out" + tag)
    big["w_pool"] = _tn_pool(s["dd"], dyc, w["w_pool"].shape[0], name="gw_pool" + tag)
    o3 = rs.chip(early, sb_o, l) if own else None
    dz, redm, dws, dbs = _mix_bwd(dz, dua, dub, ddd, s["z"], w["conv_a"], w["lnv"], w["wm"], w["wmt"],
                                  w["bias_full"], w["mask"], name="mix_bwd" + tag, phases=ph(o3))
    mid = ("w_o", "w_a_out", "w_b_out", "w_pool")
    o1b = None
    if own:
        rs.done(early, o3)
        o1b = rs.pair(mid, big)
    big["w_in"] = _tn_cols_rhs(s["h"], dz, w["w_in8"].shape[0], name="gw_in" + tag, phases=ph(o1b))
    tail_phases, pending = (), None
    if own:
        sb_m = rs.sums(mid, big, o1b)
        o1c, o3b = rs.pair(first, big), rs.chip(mid, sb_m, l)
        tail_phases = [o1c, o3b]
    dh = _nt_cols(dz, w["w_in8"], name="in_bwd" + tag, phases=tail_phases)
    if own:
        rs.done(mid, o3b)
        pending = (first, rs.chip(first, rs.sums(first, big, o1c), l))
    reds = dict(red2=red2, redf=redf, red1=red1, redg=redg, redm=redm, dws=dws, dbs=dbs)
    return dxp, dh, big, reds, pending


def _local_step(x, tgt, ws, alpha, shards=None, rs=None):
    depth = len(ws)
    saved = []
    y = x
    for l in range(depth):
        rest = [shards[l][n] for n in REST] if shards else None
        nxt = shards[l + 1]["w_in"] if shards and l + 1 < depth else None
        y, s, next_in = _layer_fwd(y, ws[l], alpha, "_l%d" % l, rest, nxt)
        if next_in is not None:
            ws[l + 1]["w_in8"] = next_in
        saved.append(s)
    dpart, loss_blk = _loss_grad(y, tgt, name="loss_grad")
    dh = xmod = mvec = above = pending = None
    bigs, reds = [None] * depth, [None] * depth
    for l in reversed(range(depth)):
        dpart, dh, bigs[l], reds[l], pending = _layer_bwd(dpart, dh, xmod, mvec, ws[l], saved[l], alpha, "_l%d" % l,
                                                          l, above if rs else None, rs)
        xmod, mvec, above = saved[l]["x"], ws[l]["mod1"][0:1], bigs[l]
    grad_x, red0 = _resid_ln_bwd(dpart, dh, xmod, mvec, None, None, None, alpha, name="in_bwd_tail",
                                 phases=[pending[1]] if pending else ())
    if pending:
        rs.done(*pending)
    d_ada = []
    for l in range(depth):
        below = red0 if l == 0 else reds[l - 1]["red2"]
        r1, r2 = reds[l]["red1"], reds[l]["red2"]
        d_ada.append(jnp.stack([below[1], below[0], r1[4], r1[1], r1[0], r2[4]]))
    return loss_blk, grad_x, bigs, reds, jnp.stack(d_ada)


def _small_grads(r):
    redm, redg, redf = r["redm"], r["redg"], r["redf"]
    ng = r["dws"].shape[0]
    return dict(
        b_in=jnp.concatenate([redm[0:6], redg[1:4]], axis=0).reshape(-1),
        conv_a=redm[6:9], ln_v_g=redm[9], ln_v_b=redm[10],
        w_spatial=r["dws"], b_spatial=r["dbs"][:, :ng].T,
        pool_scale=redg[0], ln1_g=r["red1"][2], ln1_b=r["red1"][3],
        b_up=jnp.concatenate([redf[:, 4, :].reshape(-1), redf[:, 5, :].reshape(-1)]),
        conv_ffn=jnp.transpose(redf[:, 0:3, :], (1, 0, 2)).reshape(3, -1), conv_ffn_b=redf[:, 3, :].reshape(-1),
        ln2_g=r["red2"][2], ln2_b=r["red2"][3])


def _layer_weights(l, ada, conv_a, conv_ffn, p):
    sh1, sc1, gt1, sh2, sc2, gt2 = (ada[l, k][None, :] for k in range(6))
    nb = N_DEV
    fs = p["b_up"].shape[1] // nb
    nj = nb // 2
    pos = jnp.arange(GMLP_BLOCK)
    allowed = (pos[None, :] // CHUNK) <= (pos[:, None] // CHUNK)
    wmask = jnp.where(allowed[None], p["w_spatial"][l], 0.0)
    return dict(
        mod1=jnp.concatenate([1.0 + sc1, sh1]), mod2=jnp.concatenate([1.0 + sc2, sh2]),
        ln1=jnp.concatenate([gt1, p["ln1_g"][l][None], p["ln1_b"][l][None]]),
        ln2=jnp.concatenate([gt2, p["ln2_g"][l][None], p["ln2_b"][l][None]]),
        b_in8=p["b_in"][l].reshape(N_DEV, 1, -1), b_up8=p["b_up"][l].reshape(nb, 1, fs),
        conv_a=conv_a[l], lnv=jnp.stack([p["ln_v_g"][l], p["ln_v_b"][l]]),
        wm=wmask.astype(BF), wmt=jnp.transpose(wmask, (0, 2, 1)).astype(BF),
        bias_full=jnp.repeat(p["b_spatial"][l].T, GMLP_BLOCK, axis=1), mask=allowed.astype(F32),
        pool_scale=p["pool_scale"][l][None],
        cw=jnp.transpose(conv_ffn[l].reshape(3, nj, fs), (1, 0, 2)), cb=p["conv_ffn_b"][l].reshape(nj, 1, fs))


ANY = pl.BlockSpec(memory_space=pl.ANY)


def _place():
    x, y, c = lax.axis_index("x"), lax.axis_index("y"), lax.axis_index("c")
    chips = [(1 - x, y), (x, 1 - y), (1 - x, 1 - y)]
    return x, y, c, chips


def _allgather_vmem(xs, *, name):
    r, cdim = xs.shape

    def body(x_ref, out_ref, send_sems, recv_sems, local_sem):
        x, y, c, chips = _place()
        me, sibling = (x, y, c), (x, y, 1 - c)

        def rows(px, py, pc):
            return out_ref.at[pl.ds((4 * px + 2 * py + pc) * r, r), :]

        def copy(k, block, to, src=None):
            return pltpu.make_async_remote_copy(
                src_ref=rows(*block) if src is None else src, dst_ref=rows(*block),
                send_sem=send_sems.at[k], recv_sem=recv_sems.at[k], device_id=to, device_id_type=MESH)

        mine = pltpu.make_async_copy(x_ref, rows(*me), local_sem)
        mine.start()
        first = [copy(0, me, sibling, src=x_ref)]
        first += [copy(1 + j, me, (*chip, c), src=x_ref) for j, chip in enumerate(chips)]
        for cp in first:
            cp.start()
        passed = [copy(4 + j, (*chip, c), sibling) for j, chip in enumerate(chips)]
        for j, chip in enumerate(chips):
            copy(1 + j, (*chip, c), me).wait_recv()
            passed[j].start()
        copy(0, sibling, me).wait_recv()
        for j, chip in enumerate(chips):
            copy(4 + j, (*chip, 1 - c), me).wait_recv()
        for cp in first + passed:
            cp.wait_send()
        mine.wait()

    return pl.pallas_call(
        body, name=name, out_shape=jax.ShapeDtypeStruct((N_DEV * r, cdim), xs.dtype),
        in_specs=[pl.BlockSpec(memory_space=pltpu.VMEM)], out_specs=pl.BlockSpec(memory_space=pltpu.VMEM),
        scratch_shapes=[pltpu.SemaphoreType.DMA((7,)), pltpu.SemaphoreType.DMA((7,)), pltpu.SemaphoreType.DMA],
        compiler_params=_params(),
    )(xs)


def _gather_weights(shards, *, name):
    n = len(shards)

    def body(*refs):
        ins, outs = refs[:n], refs[n:2 * n]
        send_sems, recv_sems, local_sems = refs[2 * n:]
        x, y, c, chips = _place()
        me, sibling = (x, y, c), (x, y, 1 - c)

        def slot(a, px, py, pc):
            return outs[a].at[:, 4 * px + 2 * py + pc]

        def copy(a, k, block, to, src=None):
            return pltpu.make_async_remote_copy(
                src_ref=slot(a, *block) if src is None else src, dst_ref=slot(a, *block),
                send_sem=send_sems.at[7 * a + k], recv_sem=recv_sems.at[7 * a + k], device_id=to,
                device_id_type=MESH)

        mine = [pltpu.make_async_copy(ins[a], slot(a, *me), local_sems.at[a]) for a in range(n)]
        for cp in mine:
            cp.start()
        first = []
        for j, chip in enumerate(chips):
            first += [copy(a, 1 + j, me, (*chip, c), src=ins[a]) for a in range(n)]
        first += [copy(a, 0, me, sibling, src=ins[a]) for a in range(n)]
        for cp in first:
            cp.start()
        passed = []
        for j, chip in enumerate(chips):
            for a in range(n):
                copy(a, 1 + j, (*chip, c), me).wait_recv()
                fwd = copy(a, 4 + j, (*chip, c), sibling)
                fwd.start()
                passed.append(fwd)
        for a in range(n):
            copy(a, 0, sibling, me).wait_recv()
        for j, chip in enumerate(chips):
            for a in range(n):
                copy(a, 4 + j, (*chip, 1 - c), me).wait_recv()
        for cp in first + passed:
            cp.wait_send()
        for cp in mine:
            cp.wait()

    out_shape = [jax.ShapeDtypeStruct((s.shape[0], N_DEV) + s.shape[1:], s.dtype) for s in shards]
    return pl.pallas_call(
        body, name=name, out_shape=out_shape, in_specs=[ANY] * n, out_specs=[ANY] * n,
        scratch_shapes=[pltpu.SemaphoreType.DMA((7 * n,)), pltpu.SemaphoreType.DMA((7 * n,)),
                        pltpu.SemaphoreType.DMA((n,))],
        compiler_params=_params(),
    )(*shards)


def _pick_tile(r, cap):
    best = None
    for t in range(8, min(r, cap) + 1, 8):
        if r % t == 0:
            best = t
    return best if best is not None else r


def _pair_sum(g, r1, cidx, *, name):
    p, _, r, cdim = g.shape
    tr = _pick_tile(r, 256)

    def body(c_ref, g_ref, r_ref, o_ref):
        del c_ref
        o_ref[...] = (g_ref[...] + r_ref[...]).astype(BF)

    grid_spec = pltpu.PrefetchScalarGridSpec(
        num_scalar_prefetch=1, grid=(4, r // tr),
        in_specs=[pl.BlockSpec((p, None, tr, cdim), lambda q, i, c: (0, 2 * q + c[0], i, 0)),
                  pl.BlockSpec((None, p, tr, cdim), lambda q, i, c: (q, 0, i, 0))],
        out_specs=pl.BlockSpec((None, p, tr, cdim), lambda q, i, c: (q, 0, i, 0)))
    return pl.pallas_call(
        body, name=name, grid_spec=grid_spec, out_shape=jax.ShapeDtypeStruct((4, p, r, cdim), BF),
        compiler_params=_params(("arbitrary", "arbitrary")),
    )(cidx, g, r1)


def _ada_fwd(c_all, w_ada, *, name):
    depth, d, ns = w_ada.shape
    nb = c_all.shape[0]

    def body(c_ref, w_ref, o_ref):
        cv = c_ref[...]
        act = cv * jax.nn.sigmoid(cv)
        o_ref[...] = jnp.dot(act, w_ref[...], preferred_element_type=F32, precision=lax.Precision.HIGHEST)

    return pl.pallas_call(
        body, name=name, grid=(depth,),
        in_specs=[pl.BlockSpec((nb, d), lambda l: (0, 0)), pl.BlockSpec((None, d, ns), lambda l: (l, 0, 0))],
        out_specs=pl.BlockSpec((None, nb, ns), lambda l: (l, 0, 0)),
        out_shape=jax.ShapeDtypeStruct((depth, nb, ns), F32), compiler_params=_params(("parallel",)),
    )(c_all, w_ada)


def _ada_bwd(ct, dmine, dall, *, name):
    depth, nb, ns = dmine.shape
    d = ct.shape[0]

    def body(ct_ref, dm_ref, da_ref, gw_ref, gb_ref):
        cv = ct_ref[...]
        act = cv * jax.nn.sigmoid(cv)
        gw_ref[...] = jnp.dot(act, dm_ref[...], preferred_element_type=F32, precision=lax.Precision.HIGHEST)
        s = da_ref[0]
        for b in range(1, nb):
            s = s + da_ref[b]
        gb_ref[...] = s

    return pl.pallas_call(
        body, name=name, grid=(depth,),
        in_specs=[pl.BlockSpec((d, nb), lambda l: (0, 0)), pl.BlockSpec((None, nb, ns), lambda l: (l, 0, 0)),
                  pl.BlockSpec(dall.shape, lambda l: (0, 0, 0))],
        out_specs=[pl.BlockSpec((None, d, ns), lambda l: (l, 0, 0)), pl.BlockSpec(dall.shape[1:], lambda l: (0, 0))],
        out_shape=[jax.ShapeDtypeStruct((depth, d, ns), F32), jax.ShapeDtypeStruct(dall.shape[1:], F32)],
        compiler_params=_params(("arbitrary",)),
    )(ct, dmine, dall)


def _sum_parts(parts, *, name):
    p, r, cdim = parts.shape
    tr = _pick_tile(r, 512)

    def body(p_ref, o_ref):
        s = p_ref[0]
        for k in range(1, p):
            s = s + p_ref[k]
        o_ref[...] = s

    return pl.pallas_call(
        body, name=name, grid=(r // tr,),
        in_specs=[pl.BlockSpec((p, tr, cdim), lambda i: (0, i, 0))], out_specs=pl.BlockSpec((tr, cdim), lambda i: (i, 0)),
        out_shape=jax.ShapeDtypeStruct((r, cdim), F32), compiler_params=_params(("parallel",)),
    )(parts)


def _adamw(parts, w, m, v, *, name):
    p, depth, r, cdim = parts.shape
    tr = _pick_tile(r, 256)

    def body(p_ref, w_ref, m_ref, v_ref, g_out, d_out, m_out, v_out):
        g = p_ref[0].astype(F32)
        for k in range(1, p):
            g = g + p_ref[k].astype(F32)
        m2 = ADAM_B1 * m_ref[...] + (1.0 - ADAM_B1) * g
        v2 = ADAM_B2 * v_ref[...] + (1.0 - ADAM_B2) * (g * g)
        m_hat = m2 / (1.0 - ADAM_B1 ** ADAM_STEP)
        v_hat = v2 / (1.0 - ADAM_B2 ** ADAM_STEP)
        g_out[...] = g
        d_out[...] = -ADAM_LR * (m_hat / (jnp.sqrt(v_hat) + ADAM_EPS) + ADAM_WD * w_ref[...])
        m_out[...] = m2
        v_out[...] = v2

    blk = pl.BlockSpec((None, tr, cdim), lambda l, i: (l, i, 0))
    out = jax.ShapeDtypeStruct((depth, r, cdim), F32)
    return pl.pallas_call(
        body, name=name, grid=(depth, r // tr),
        in_specs=[pl.BlockSpec((p, None, tr, cdim), lambda l, i: (0, l, i, 0)), blk, blk, blk],
        out_specs=[blk, blk, blk, blk], out_shape=[out, out, out, out],
        compiler_params=_params(("parallel", "parallel")),
    )(parts, w, m, v)


BIG = ("w_in", "w_a_out", "w_b_out", "w_pool", "w_o", "w_up", "w_down")
SMALL_REPLICATED = ("b_in", "ln_v_g", "ln_v_b", "w_spatial", "b_spatial", "pool_scale", "ln1_g", "ln1_b", "b_up",
                    "conv_ffn_b", "ln2_g", "ln2_b")
SMALL_SHARDED = ("conv_a", "conv_ffn")
WEIGHTS = ("w_ada", "b_ada", "w_in", "b_in", "conv_a", "w_a_out", "ln_v_g", "ln_v_b", "w_spatial", "b_spatial",
           "w_b_out", "w_pool", "pool_scale", "w_o", "ln1_g", "ln1_b", "w_up", "b_up", "conv_ffn", "conv_ffn_b",
           "w_down", "ln2_g", "ln2_b")
LANES = 128


def _as_rows(flat, mult=8):
    n = flat.shape[0]
    pad = (-n) % (LANES * mult)
    if pad:
        flat = jnp.concatenate([flat, jnp.zeros((pad,), flat.dtype)])
    return flat.reshape(-1, LANES)


def _shard3(a):
    return a.reshape((-1,) + a.shape[-2:])


def kernel(x, c, w_ada, b_ada, w_in, b_in, conv_a, w_a_out, ln_v_g, ln_v_b, w_spatial, b_spatial, w_b_out, w_pool, pool_scale, w_o, ln1_g, ln1_b, w_up, b_up, conv_ffn, conv_ffn_b, w_down, ln2_g, ln2_b, loss_target, m_w_ada, m_b_ada, m_w_in, m_b_in, m_conv_a, m_w_a_out, m_ln_v_g, m_ln_v_b, m_w_spatial, m_b_spatial, m_w_b_out, m_w_pool, m_pool_scale, m_w_o, m_ln1_g, m_ln1_b, m_w_up, m_b_up, m_conv_ffn, m_conv_ffn_b, m_w_down, m_ln2_g, m_ln2_b, v_w_ada, v_b_ada, v_w_in, v_b_in, v_conv_a, v_w_a_out, v_ln_v_g, v_ln_v_b, v_w_spatial, v_b_spatial, v_w_b_out, v_w_pool, v_pool_scale, v_w_o, v_ln1_g, v_ln1_b, v_w_up, v_b_up, v_conv_ffn, v_conv_ffn_b, v_w_down, v_ln2_g, v_ln2_b):
    p = dict(locals())
    depth, d = w_in.shape[0], w_in.shape[1]
    alpha = (2 * depth) ** 0.25
    me = 4 * lax.axis_index("x") + 2 * lax.axis_index("y") + lax.axis_index("c")
    cidx = lax.axis_index("c").astype(jnp.int32).reshape(1)

    n_ca, n_cf = conv_a.size, conv_ffn.size
    packed = _as_rows(jnp.concatenate([c.reshape(-1), conv_a.reshape(-1), conv_ffn.reshape(-1)]))
    got = _allgather_vmem(packed, name="gather_cond").reshape(N_DEV, -1)
    c_all = got[:, :d]
    ct = c_all.T
    conv_a_full = jnp.transpose(got[:, d:d + n_ca].reshape((N_DEV,) + conv_a.shape), (1, 2, 0, 3)).reshape(depth, 3, -1)
    conv_ffn_full = jnp.transpose(got[:, d + n_ca:d + n_ca + n_cf].reshape((N_DEV,) + conv_ffn.shape),
                                  (1, 2, 0, 3)).reshape(depth, 3, -1)

    ns_ada = w_ada.shape[2]
    ada_part = _ada_fwd(c_all, w_ada, name="ada_fwd")
    ada_all = _allgather_vmem(_as_rows(ada_part.reshape(-1)), name="gather_ada")
    ada_all = ada_all.reshape(N_DEV, depth, N_DEV, ns_ada)
    ada_mine = lax.dynamic_index_in_dim(ada_all, me, axis=2, keepdims=False)
    ada = jnp.transpose(ada_mine, (1, 0, 2)).reshape(depth, -1) + b_ada
    ada = ada.reshape(depth, 6, d)

    shards = [{n: _shard3(p[n][l].astype(BF)) for n in BIG} for l in range(depth)]
    ws = [_layer_weights(l, ada, conv_a_full, conv_ffn_full, p) for l in range(depth)]
    ws[0]["w_in8"] = _gather_weights([shards[0]["w_in"]], name="gather_w_in0")[0][0]

    rs = _ReduceScatter(depth, cidx)
    loss_blk, grad_x, bigs, reds, d_ada = _local_step(x[0], loss_target[0], ws, alpha, shards, rs)
    loss = lax.psum(loss_blk[0, 0], ("x", "y", "c"))

    dada_all = _allgather_vmem(_as_rows(d_ada.reshape(-1)), name="gather_dada")
    dada_all = dada_all.reshape(N_DEV, -1, LANES)
    dflat = dada_all.reshape(N_DEV, depth, 6 * d)
    dmine = lax.dynamic_slice_in_dim(dflat, me * ns_ada, ns_ada, axis=2)
    gw_ada, gb_rows = _ada_bwd(ct, jnp.transpose(dmine, (1, 0, 2)), dada_all, name="ada_bwd")
    gb_ada = gb_rows.reshape(-1)[:depth * 6 * d].reshape(depth, 6 * d)

    out = {}
    for n in BIG:
        parts = rs.acc[n]
        shard_shape = p[n].shape
        w3 = p[n].reshape(depth, -1, shard_shape[-1])
        parts4 = parts.reshape((4,) + w3.shape)
        res = _adamw(parts4, w3, p["m_" + n].reshape(w3.shape), p["v_" + n].reshape(w3.shape), name="adamw_" + n)
        out[n] = [r.reshape(shard_shape) for r in res]
    out["w_ada"] = _adamw(gw_ada[None], w_ada, m_w_ada, v_w_ada, name="adamw_w_ada")

    smalls = [_small_grads(r) for r in reds]
    order = SMALL_REPLICATED + SMALL_SHARDED
    payload = jnp.concatenate([smalls[l][n].reshape(-1) for n in order for l in range(depth)])
    n_pay = payload.shape[0]
    gathered = _allgather_vmem(_as_rows(payload), name="gather_small")
    gsum = _sum_parts(gathered.reshape(N_DEV, -1, LANES), name="sum_small").reshape(-1)[:n_pay]
    n_rep = sum(p[n].size for n in SMALL_REPLICATED)
    ga_full = gsum[n_rep:n_rep + depth * 3 * d].reshape(depth, 3, d)
    gf_full = gsum[n_rep + depth * 3 * d:].reshape(depth, 3, -1)
    ca_w, cf_w = conv_a.shape[2], conv_ffn.shape[2]
    g_ca = lax.dynamic_slice_in_dim(ga_full, me * ca_w, ca_w, axis=2)
    g_cf = lax.dynamic_slice_in_dim(gf_full, me * cf_w, cf_w, axis=2)
    names = ("b_ada",) + order
    gflat = _as_rows(jnp.concatenate([gb_ada.reshape(-1), gsum[:n_rep], g_ca.reshape(-1), g_cf.reshape(-1)]))
    pack = lambda pre: _as_rows(jnp.concatenate([p[pre + n].reshape(-1) for n in names]))
    res = _adamw(gflat[None, None], pack("")[None], pack("m_")[None], pack("v_")[None], name="adamw_small")
    off = 0
    for n in names:
        size = p[n].size
        out[n] = [r.reshape(-1)[off:off + size].reshape(p[n].shape) for r in res]
        off += size

    return (loss, grad_x[None]) + tuple(out[n][k] for k in range(4) for n in WEIGHTS)
```

```python
import functools

import jax
import jax.numpy as jnp
from jax import lax
from jax.experimental import pallas as pl
from jax.experimental.pallas import tpu as pltpu

F32 = jnp.float32
BF = jnp.bfloat16
MESH = pl.DeviceIdType.MESH

LN_EPS = 1e-5
POOL_WINDOWS = (2, 4, 8, 16)
GMLP_BLOCK = 128
CHUNK = 64
HALO = 16
ADAM_LR, ADAM_B1, ADAM_B2, ADAM_EPS, ADAM_WD, ADAM_STEP = 0.001, 0.9, 0.999, 1e-08, 0.01, 10
N_DEV = 8
VMEM_LIMIT = 56 * 1024 * 1024

GRAD_DTYPE = BF

NN = ((1,), (0,))
NT = ((1,), (1,))
TN = ((0,), (0,))


def _params(sem=None, vmem=VMEM_LIMIT, **kw):
    if sem is not None:
        kw["dimension_semantics"] = sem
    return pltpu.CompilerParams(vmem_limit_bytes=vmem, **kw)


class _Phase:
    def __init__(self, ins, out_shapes, aliases, n_remote, n_local, build):
        self.ins, self.out_shapes, self.aliases = list(ins), list(out_shapes), dict(aliases)
        self.n_remote, self.n_local, self.build = n_remote, n_local, build
        self.results = None


def _pcall(body, args, *, name, grid, in_specs, out_specs, out_shape, scratch_shapes=(), sem=None, aliases=None,
           phases=()):
    aliases = dict(aliases or {})
    if not phases:
        return pl.pallas_call(
            body, name=name, grid=grid, in_specs=list(in_specs), out_specs=out_specs, out_shape=out_shape,
            scratch_shapes=list(scratch_shapes), input_output_aliases=aliases, compiler_params=_params(sem),
        )(*args)
    single = not isinstance(out_shape, (list, tuple))
    o_specs = [out_specs] if single else list(out_specs)
    o_shapes = [out_shape] if single else list(out_shape)
    n_in, n_out, n_scr = len(args), len(o_shapes), len(scratch_shapes)
    ex_args, ex_out, sems = [], [], []
    for ph in phases:
        for src, dst in ph.aliases.items():
            aliases[n_in + len(ex_args) + src] = n_out + len(ex_out) + dst
        ex_args += ph.ins
        ex_out += ph.out_shapes
        sems += [pltpu.SemaphoreType.DMA((max(ph.n_remote, 1),)), pltpu.SemaphoreType.DMA((max(ph.n_remote, 1),)),
                 pltpu.SemaphoreType.DMA((max(ph.n_local, 1),))]

    def wrapped(*refs):
        pos = n_in
        ph_in = []
        for ph in phases:
            ph_in.append(refs[pos:pos + len(ph.ins)])
            pos += len(ph.ins)
        base_out = refs[pos:pos + n_out]
        pos += n_out
        ph_out = []
        for ph in phases:
            ph_out.append(refs[pos:pos + len(ph.out_shapes)])
            pos += len(ph.out_shapes)
        base_scr = refs[pos:pos + n_scr]
        ph_sems = refs[pos + n_scr:]
        first = last = None
        for ax, n in enumerate(grid):
            pid = pl.program_id(ax)
            first = (pid == 0) if first is None else first & (pid == 0)
            last = (pid == n - 1) if last is None else last & (pid == n - 1)

        def ops(k):
            return phases[k].build(ph_in[k], ph_out[k], *ph_sems[3 * k:3 * k + 3])

        @pl.when(first)
        def _():
            for k in range(len(phases)):
                for cp in ops(k)["start"]:
                    cp.start()

        body(*refs[:n_in], *base_out, *base_scr)

        @pl.when(last)
        def _():
            for k in range(len(phases)):
                o = ops(k)
                for cp in o["recv"]:
                    cp.wait_recv()
                for cp in o["send"]:
                    cp.wait_send()
                for cp in o["local"]:
                    cp.wait()

    hbm = pl.BlockSpec(memory_space=pl.ANY)
    res = pl.pallas_call(
        wrapped, name=name, grid=grid, in_specs=list(in_specs) + [hbm] * len(ex_args),
        out_specs=o_specs + [hbm] * len(ex_out), out_shape=o_shapes + ex_out,
        scratch_shapes=list(scratch_shapes) + sems, input_output_aliases=aliases,
        compiler_params=_params(("arbitrary",) * len(grid)),
    )(*args, *ex_args)
    pos = n_out
    for ph in phases:
        ph.results = list(res[pos:pos + len(ph.out_shapes)])
        pos += len(ph.out_shapes)
    return res[0] if single else list(res[:n_out])


def _gelu_parts(x):
    k = 0.7978845608028654
    x2 = x * x
    t = jnp.tanh(k * (x + 0.044715 * (x2 * x)))
    cdf = 0.5 * (1.0 + t)
    dcdf = 0.5 * (1.0 - t * t) * (k * (1.0 + 3.0 * 0.044715 * x2))
    return x * cdf, cdf + x * dcdf


def _gelu(x):
    t = jnp.tanh(0.7978845608028654 * (x + 0.044715 * (x * x * x)))
    return x * (0.5 * (1.0 + t))


def _rowsum(v):
    return jnp.sum(v, axis=0, keepdims=True)


def _ln_stats(r):
    mu = jnp.mean(r, axis=-1, keepdims=True)
    xc = r - mu
    var = jnp.mean(xc * xc, axis=-1, keepdims=True)
    rstd = lax.rsqrt(var + LN_EPS)
    return xc * rstd, rstd


def _ln_bwd(dy, xhat, rstd, gain):
    dxh = dy * gain
    m1 = jnp.mean(dxh, axis=-1, keepdims=True)
    m2 = jnp.mean(dxh * xhat, axis=-1, keepdims=True)
    return rstd * (dxh - m1 - xhat * m2)


def _matmul(a, b, *, dn, grid, a_spec, b_spec, o_spec, out_shape, acc_shape, name, phases=()):
    nk = grid[2]
    direct = out_shape.dtype == F32

    def body(a_ref, b_ref, o_ref, *scratch):
        prod = lax.dot_general(a_ref[...], b_ref[...], (dn, ((), ())), preferred_element_type=F32)
        if nk == 1:
            o_ref[...] = prod.astype(o_ref.dtype)
            return
        acc = o_ref if direct else scratch[0]
        k = pl.program_id(2)

        @pl.when(k == 0)
        def _():
            acc[...] = prod

        @pl.when(k > 0)
        def _():
            acc[...] += prod

        if not direct:
            @pl.when(k == nk - 1)
            def _():
                o_ref[...] = acc[...].astype(o_ref.dtype)

    scratch = [] if (direct or nk == 1) else [pltpu.VMEM(acc_shape, F32)]
    return _pcall(body, (a, b), name=name, grid=grid, in_specs=[a_spec, b_spec], out_specs=o_spec,
                  out_shape=out_shape, scratch_shapes=scratch, sem=("parallel", "parallel", "arbitrary"),
                  phases=phases)


def _row_tile(m, want):
    t = min(m, want)
    assert m % t == 0
    return t


def _mm_rows(a, w, *, dn, name, out_dtype=F32, tm=512):
    m, k = a.shape
    n = w.shape[1] if dn == NN else w.shape[0]
    tm = _row_tile(m, tm)
    return _matmul(
        a, w, dn=dn, grid=(m // tm, 1, 1), name=name,
        a_spec=pl.BlockSpec((tm, k), lambda i, j, kk: (i, 0)),
        b_spec=pl.BlockSpec(w.shape, lambda i, j, kk: (0, 0)),
        o_spec=pl.BlockSpec((tm, n), lambda i, j, kk: (i, 0)),
        out_shape=jax.ShapeDtypeStruct((m, n), out_dtype), acc_shape=(tm, n))


def _mm_tn(a, b, *, name, tk=512):
    m, ka = a.shape
    n = b.shape[1]
    tk = _row_tile(m, tk)
    return _matmul(
        a, b, dn=TN, grid=(1, 1, m // tk), name=name,
        a_spec=pl.BlockSpec((tk, ka), lambda i, j, kk: (kk, 0)),
        b_spec=pl.BlockSpec((tk, n), lambda i, j, kk: (kk, 0)),
        o_spec=pl.BlockSpec((ka, n), lambda i, j, kk: (0, 0)),
        out_shape=jax.ShapeDtypeStruct((ka, n), GRAD_DTYPE), acc_shape=(ka, n))


def _mod_matmul(x, mod, w8, bias8, *, flat_out, name, tm=1024, phases=()):
    m, k = x.shape
    nb, _, ns = w8.shape
    tm = _row_tile(m, tm)

    def body(x_ref, mod_ref, w_ref, b_ref, o_ref, h_ref, hs):
        @pl.when(pl.program_id(1) == 0)
        def _():
            h = (x_ref[...] * mod_ref[0:1, :] + mod_ref[1:2, :]).astype(BF)
            hs[...] = h
            h_ref[...] = h

        o_ref[...] = jnp.dot(hs[...], w_ref[...], preferred_element_type=F32) + b_ref[...]

    if flat_out:
        o_spec = pl.BlockSpec((tm, ns), lambda i, j: (i, j))
        o_shape = jax.ShapeDtypeStruct((m, nb * ns), F32)
    else:
        o_spec = pl.BlockSpec((None, tm, ns), lambda i, j: (j, i, 0))
        o_shape = jax.ShapeDtypeStruct((nb, m, ns), F32)
    return _pcall(
        body, (x, mod, w8, bias8), name=name, grid=(m // tm, nb),
        in_specs=[pl.BlockSpec((tm, k), lambda i, j: (i, 0)),
                  pl.BlockSpec((2, k), lambda i, j: (0, 0)),
                  pl.BlockSpec((None, k, ns), lambda i, j: (j, 0, 0)),
                  pl.BlockSpec((None, 1, ns), lambda i, j: (j, 0, 0))],
        out_specs=[o_spec, pl.BlockSpec((tm, k), lambda i, j: (i, 0))],
        out_shape=[o_shape, jax.ShapeDtypeStruct((m, k), BF)],
        scratch_shapes=[pltpu.VMEM((tm, k), BF)], sem=("parallel", "arbitrary"), phases=phases)


def _seg_spec(tm, d, s):
    return pl.BlockSpec((tm, d), lambda i, s=s: (i, s))


def _prev_halo_spec(tm, d, s):
    hb = tm // HALO
    return pl.BlockSpec((HALO, d), lambda i, s=s: (jnp.maximum(i * hb - 1, 0), s))


def _next_halo_spec(tm, d, s, m):
    hb = tm // HALO
    last = m // HALO - 1
    return pl.BlockSpec((HALO, d), lambda i, s=s: (jnp.minimum((i + 1) * hb, last), s))


def _spatial_mix(wm_ref, src, dst, bias_ref, tm, d):
    for n in range(tm // GMLP_BLOCK):
        for g in range(d // GMLP_BLOCK):
            rs = slice(n * GMLP_BLOCK, (n + 1) * GMLP_BLOCK)
            cs = slice(g * GMLP_BLOCK, (g + 1) * GMLP_BLOCK)
            v = jnp.dot(wm_ref[g], src[rs, cs], preferred_element_type=F32)
            if bias_ref is not None:
                v = v + bias_ref[:, cs]
            dst[rs, cs] = v


def _mix_fwd(z, conv_a, lnv, wm, bias_full, *, name, tm=256, phases=()):
    m, d9 = z.shape
    d = d9 // 9
    tm = _row_tile(m, tm)
    grp = d // len(POOL_WINDOWS)

    def body(zb, zc, zx, zu, zv, zp, zc_h, zx_h, zp_h, ca_ref, lnv_ref, wm_ref, bias_ref,
             ua_ref, ub_ref, d_ref, ext, vn_s, mixed_s):
        i = pl.program_id(0)
        first = i == 0
        pa = zc[...] * zx[...]
        ext[0:HALO, :] = jnp.where(first, 0.0, zc_h[...] * zx_h[...])
        ext[HALO:HALO + tm, :] = pa
        w = ca_ref[...]
        conv = w[0:1, :] * ext[pl.ds(HALO - 2, tm), :] + w[1:2, :] * ext[pl.ds(HALO - 1, tm), :] + w[2:3, :] * pa
        ua_ref[...] = (zb[...] * conv).astype(BF)
        p = zp[...]
        ext[0:HALO, :] = jnp.where(first, 0.0, zp_h[...])
        ext[HALO:HALO + tm, :] = p
        t = (i * tm + lax.broadcasted_iota(jnp.int32, (tm, 1), 0) + 1).astype(F32)
        for k, win in enumerate(POOL_WINDOWS):
            cs = slice(k * grp, (k + 1) * grp)
            s = p[:, cs]
            for j in range(1, win):
                s = s + ext[pl.ds(HALO - j, tm), cs]
            d_ref[:, cs] = (s / jnp.minimum(t, float(win)) - p[:, cs]).astype(BF)
        gv = _gelu(zv[...])
        vhat, _ = _ln_stats(gv)
        vn_s[...] = (vhat * lnv_ref[0:1, :] + lnv_ref[1:2, :]).astype(BF)
        _spatial_mix(wm_ref, vn_s, mixed_s, bias_ref, tm, d)
        ub_ref[...] = (_gelu(zu[...]) * mixed_s[...]).astype(BF)

    full = lambda a: pl.BlockSpec(a.shape, lambda i: (0,) * a.ndim)
    out = jax.ShapeDtypeStruct((m, d), BF)
    o_spec = pl.BlockSpec((tm, d), lambda i: (i, 0))
    return _pcall(
        body, (z, z, z, z, z, z, z, z, z, conv_a, lnv, wm, bias_full), name=name, grid=(m // tm,),
        in_specs=[_seg_spec(tm, d, s) for s in range(6)] + [_prev_halo_spec(tm, d, s) for s in (1, 2, 5)]
        + [full(conv_a), full(lnv), full(wm), full(bias_full)],
        out_specs=[o_spec, o_spec, o_spec], out_shape=[out, out, out],
        scratch_shapes=[pltpu.VMEM((HALO + tm, d), F32), pltpu.VMEM((tm, d), BF), pltpu.VMEM((tm, d), F32)],
        sem=("arbitrary",), phases=phases)


def _pool_proj(dd, w_pool, *, dn, name, out_dtype=F32, tm=512):
    m, d = dd.shape
    ng, grp, _ = w_pool.shape
    tm = _row_tile(m, tm)
    return _matmul(
        dd, w_pool, dn=dn, grid=(m // tm, ng, 1), name=name,
        a_spec=pl.BlockSpec((tm, grp), lambda i, j, kk: (i, j)),
        b_spec=pl.BlockSpec((None, grp, grp), lambda i, j, kk: (j, 0, 0)),
        o_spec=pl.BlockSpec((tm, grp), lambda i, j, kk: (i, j)),
        out_shape=jax.ShapeDtypeStruct((m, d), out_dtype), acc_shape=(tm, grp))


def _merge(z, ya, yb, ycp, scale, *, name, tm=512):
    m, d = ya.shape
    tm = _row_tile(m, tm)

    def body(ga, gb, gc, ya_ref, yb_ref, yc_ref, sc_ref, o_ref):
        o_ref[...] = (jax.nn.sigmoid(ga[...]) * ya_ref[...] + jax.nn.sigmoid(gb[...]) * yb_ref[...]
                      + jax.nn.sigmoid(gc[...]) * (yc_ref[...] * sc_ref[...])).astype(BF)

    row = pl.BlockSpec((tm, d), lambda i: (i, 0))
    return pl.pallas_call(
        body, name=name, grid=(m // tm,),
        in_specs=[_seg_spec(tm, d, 6), _seg_spec(tm, d, 7), _seg_spec(tm, d, 8), row, row, row,
                  pl.BlockSpec((1, d), lambda i: (0, 0))],
        out_specs=row, out_shape=jax.ShapeDtypeStruct((m, d), BF),
        compiler_params=_params(("parallel",)),
    )(z, z, z, ya, yb, ycp, scale)


def _resid_ln(xp, ys, vec, alpha, *, name, tm=512):
    m, d = xp.shape
    tm = _row_tile(m, tm)

    def body(xp_ref, ys_ref, v_ref, o_ref):
        xhat, _ = _ln_stats(alpha * xp_ref[...] + v_ref[0:1, :] * ys_ref[...])
        o_ref[...] = xhat * v_ref[1:2, :] + v_ref[2:3, :]

    row = pl.BlockSpec((tm, d), lambda i: (i, 0))
    return pl.pallas_call(
        body, name=name, grid=(m // tm,),
        in_specs=[row, row, pl.BlockSpec(vec.shape, lambda i: (0, 0))],
        out_specs=row, out_shape=jax.ShapeDtypeStruct((m, d), F32),
        compiler_params=_params(("parallel",)),
    )(xp, ys, vec)


def _ffn_fwd(up4, cw, cb, *, name, tm=512, phases=()):
    _, nj, m, fs = up4.shape
    tm = _row_tile(m, tm)
    hb = tm // HALO

    def body(up_ref, ah_ref, cw_ref, cb_ref, f_ref, ext):
        first = pl.program_id(1) == 0
        a = up_ref[0]
        ext[0:HALO, :] = jnp.where(first, 0.0, ah_ref[...])
        ext[HALO:HALO + tm, :] = a
        w = cw_ref[...]
        ca = (w[0:1, :] * ext[pl.ds(HALO - 2, tm), :] + w[1:2, :] * ext[pl.ds(HALO - 1, tm), :]
              + w[2:3, :] * a + cb_ref[...])
        f_ref[...] = (_gelu(ca) * up_ref[1]).astype(BF)

    return _pcall(
        body, (up4, up4, cw, cb), name=name, grid=(nj, m // tm),
        in_specs=[pl.BlockSpec((2, None, tm, fs), lambda j, i: (0, j, i, 0)),
                  pl.BlockSpec((None, None, HALO, fs), lambda j, i: (0, j, jnp.maximum(i * hb - 1, 0), 0)),
                  pl.BlockSpec((None, 3, fs), lambda j, i: (j, 0, 0)),
                  pl.BlockSpec((None, 1, fs), lambda j, i: (j, 0, 0))],
        out_specs=pl.BlockSpec((None, tm, fs), lambda j, i: (j, i, 0)),
        out_shape=jax.ShapeDtypeStruct((nj, m, fs), BF),
        scratch_shapes=[pltpu.VMEM((HALO + tm, fs), F32)], sem=("parallel", "arbitrary"), phases=phases)


def _down_proj(f4, wd4, *, name, tm=512):
    nj, m, fs = f4.shape
    d = wd4.shape[2]
    tm = _row_tile(m, tm)
    return _matmul(
        f4, wd4, dn=NN, grid=(m // tm, 1, nj), name=name,
        a_spec=pl.BlockSpec((None, tm, fs), lambda i, j, kk: (kk, i, 0)),
        b_spec=pl.BlockSpec((None, fs, d), lambda i, j, kk: (kk, 0, 0)),
        o_spec=pl.BlockSpec((tm, d), lambda i, j, kk: (i, 0)),
        out_shape=jax.ShapeDtypeStruct((m, d), F32), acc_shape=(tm, d))


def _loss_grad(y, tgt, *, name, tm=512):
    m, d = y.shape
    tm = _row_tile(m, tm)
    ni = m // tm

    def body(y_ref, t_ref, dy_ref, l_ref, acc):
        i = pl.program_id(0)
        e = y_ref[...] - t_ref[...]
        dy_ref[...] = e * (1.0 / d)
        part = jnp.sum((e * e).reshape(tm // 8, 8, d), axis=0)

        @pl.when(i == 0)
        def _():
            acc[...] = part

        @pl.when(i > 0)
        def _():
            acc[...] += part

        @pl.when(i == ni - 1)
        def _():
            l_ref[...] = jnp.full((8, 128), 0.5 / d, F32) * jnp.sum(acc[...])

    row = pl.BlockSpec((tm, d), lambda i: (i, 0))
    return pl.pallas_call(
        body, name=name, grid=(ni,), in_specs=[row, row],
        out_specs=[row, pl.BlockSpec((8, 128), lambda i: (0, 0))],
        out_shape=[jax.ShapeDtypeStruct((m, d), F32), jax.ShapeDtypeStruct((8, 128), F32)],
        scratch_shapes=[pltpu.VMEM((8, d), F32)],
        compiler_params=_params(("arbitrary",)),
    )(y, tgt)


def _resid_ln_bwd(dpart, dh, xmod, mvec, xp, ys, vec, alpha, *, name, tm=256, phases=()):
    m, d = dpart.shape
    tm = _row_tile(m, tm)
    has_dh = dh is not None
    has_ln = xp is not None

    def body(*refs):
        refs = list(refs)
        dpart_ref = refs.pop(0)
        if has_dh:
            dh_ref, xm_ref, mv_ref = refs.pop(0), refs.pop(0), refs.pop(0)
        if has_ln:
            xp_ref, ys_ref, v_ref = refs.pop(0), refs.pop(0), refs.pop(0)
            dys_ref, dxp_ref, red_ref = refs
        else:
            dx_ref, red_ref = refs
        i = pl.program_id(0)
        dtot = dpart_ref[...]
        rows = [jnp.zeros((1, d), F32)] * 5
        if has_dh:
            dhv = dh_ref[...]
            dtot = dtot + dhv * mv_ref[...]
            rows[0] = _rowsum(dhv * xm_ref[...])
            rows[1] = _rowsum(dhv)
        if has_ln:
            ys = ys_ref[...]
            gt = v_ref[0:1, :]
            xhat, rstd = _ln_stats(alpha * xp_ref[...] + gt * ys)
            rows[2] = _rowsum(dtot * xhat)
            rows[3] = _rowsum(dtot)
            dr = _ln_bwd(dtot, xhat, rstd, v_ref[1:2, :])
            rows[4] = _rowsum(dr * ys)
            dys_ref[...] = (dr * gt).astype(BF)
            dxp_ref[...] = alpha * dr
        else:
            dx_ref[...] = dtot
        red = jnp.concatenate(rows + [jnp.zeros((3, d), F32)], axis=0)

        @pl.when(i == 0)
        def _():
            red_ref[...] = red

        @pl.when(i > 0)
        def _():
            red_ref[...] += red

    row = pl.BlockSpec((tm, d), lambda i: (i, 0))
    vrow = lambda a: pl.BlockSpec(a.shape, lambda i: (0, 0))
    args, specs = [dpart], [row]
    if has_dh:
        args += [dh, xmod, mvec]
        specs += [row, row, vrow(mvec)]
    if has_ln:
        args += [xp, ys, vec]
        specs += [row, row, vrow(vec)]
        out_specs = [row, row, pl.BlockSpec((8, d), lambda i: (0, 0))]
        out_shape = [jax.ShapeDtypeStruct((m, d), BF), jax.ShapeDtypeStruct((m, d), F32),
                     jax.ShapeDtypeStruct((8, d), F32)]
    else:
        out_specs = [row, pl.BlockSpec((8, d), lambda i: (0, 0))]
        out_shape = [jax.ShapeDtypeStruct((m, d), F32), jax.ShapeDtypeStruct((8, d), F32)]
    return _pcall(body, args, name=name, grid=(m // tm,), in_specs=specs, out_specs=out_specs, out_shape=out_shape,
                  sem=("arbitrary",), phases=phases)


def _down_bwd(dy, wd4, *, name, tm=512, phases=()):
    m, d = dy.shape
    nj, fs, _ = wd4.shape
    tm = _row_tile(m, tm)
    return _matmul(
        dy, wd4, dn=NT, grid=(m // tm, nj, 1), name=name,
        a_spec=pl.BlockSpec((tm, d), lambda i, j, kk: (i, 0)),
        b_spec=pl.BlockSpec((None, fs, d), lambda i, j, kk: (j, 0, 0)),
        o_spec=pl.BlockSpec((None, tm, fs), lambda i, j, kk: (j, i, 0)),
        out_shape=jax.ShapeDtypeStruct((nj, m, fs), F32), acc_shape=(tm, fs), phases=phases)


def _tn_shards_lhs(f4, dy, *, name, tk=512):
    nj, m, fs = f4.shape
    d = dy.shape[1]
    tk = _row_tile(m, tk)
    return _matmul(
        f4, dy, dn=TN, grid=(nj, 1, m // tk), name=name,
        a_spec=pl.BlockSpec((None, tk, fs), lambda i, j, kk: (i, kk, 0)),
        b_spec=pl.BlockSpec((tk, d), lambda i, j, kk: (kk, 0)),
        o_spec=pl.BlockSpec((None, fs, d), lambda i, j, kk: (i, 0, 0)),
        out_shape=jax.ShapeDtypeStruct((nj, fs, d), GRAD_DTYPE), acc_shape=(fs, d))


def _tn_shards_rhs(h, d8, *, name, tk=512, phases=()):
    m, k = h.shape
    nb, _, ns = d8.shape
    tk = _row_tile(m, tk)
    return _matmul(
        h, d8, dn=TN, grid=(nb, 1, m // tk), name=name,
        a_spec=pl.BlockSpec((tk, k), lambda i, j, kk: (kk, 0)),
        b_spec=pl.BlockSpec((None, tk, ns), lambda i, j, kk: (i, kk, 0)),
        o_spec=pl.BlockSpec((None, k, ns), lambda i, j, kk: (i, 0, 0)),
        out_shape=jax.ShapeDtypeStruct((nb, k, ns), GRAD_DTYPE), acc_shape=(k, ns), phases=phases)


def _tn_cols_rhs(h, dz, nb, *, name, tk=512, phases=()):
    m, k = h.shape
    ns = dz.shape[1] // nb
    tk = _row_tile(m, tk)
    return _matmul(
        h, dz, dn=TN, grid=(nb, 1, m // tk), name=name,
        a_spec=pl.BlockSpec((tk, k), lambda i, j, kk: (kk, 0)),
        b_spec=pl.BlockSpec((tk, ns), lambda i, j, kk: (kk, i)),
        o_spec=pl.BlockSpec((None, k, ns), lambda i, j, kk: (i, 0, 0)),
        out_shape=jax.ShapeDtypeStruct((nb, k, ns), GRAD_DTYPE), acc_shape=(k, ns), phases=phases)


def _nt_shards(d8, w8, *, name, tm=512, phases=()):
    nb, m, ns = d8.shape
    k = w8.shape[1]
    tm = _row_tile(m, tm)
    return _matmul(
        d8, w8, dn=NT, grid=(m // tm, 1, nb), name=name,
        a_spec=pl.BlockSpec((None, tm, ns), lambda i, j, kk: (kk, i, 0)),
        b_spec=pl.BlockSpec((None, k, ns), lambda i, j, kk: (kk, 0, 0)),
        o_spec=pl.BlockSpec((tm, k), lambda i, j, kk: (i, 0)),
        out_shape=jax.ShapeDtypeStruct((m, k), F32), acc_shape=(tm, k), phases=phases)


def _nt_cols(dz, w8, *, name, tm=512, phases=()):
    m = dz.shape[0]
    nb, k, ns = w8.shape
    tm = _row_tile(m, tm)
    return _matmul(
        dz, w8, dn=NT, grid=(m // tm, 1, nb), name=name,
        a_spec=pl.BlockSpec((tm, ns), lambda i, j, kk: (i, kk)),
        b_spec=pl.BlockSpec((None, k, ns), lambda i, j, kk: (kk, 0, 0)),
        o_spec=pl.BlockSpec((tm, k), lambda i, j, kk: (i, 0)),
        out_shape=jax.ShapeDtypeStruct((m, k), F32), acc_shape=(tm, k), phases=phases)


def _tn_pool(dd, dyc, ng, *, name, tk=512):
    m, d = dd.shape
    grp = d // ng
    tk = _row_tile(m, tk)
    return _matmul(
        dd, dyc, dn=TN, grid=(ng, 1, m // tk), name=name,
        a_spec=pl.BlockSpec((tk, grp), lambda i, j, kk: (kk, i)),
        b_spec=pl.BlockSpec((tk, grp), lambda i, j, kk: (kk, i)),
        o_spec=pl.BlockSpec((None, grp, grp), lambda i, j, kk: (i, 0, 0)),
        out_shape=jax.ShapeDtypeStruct((ng, grp, grp), GRAD_DTYPE), acc_shape=(grp, grp))


def _ffn_bwd(up4, df4, cw, cb, *, name, tm=256, phases=()):
    _, nj, m, fs = up4.shape
    tm = _row_tile(m, tm)
    hb = tm // HALO
    ni = m // tm
    last_hb = m // HALO - 1
    ext_rows = tm + 8

    def body(up_ref, ap_ref, un_ref, df_ref, dfn_ref, cw_ref, cb_ref, dup_ref, red_ref, ext, dca_s):
        i = pl.program_id(1)
        a = up_ref[0]
        g = up_ref[1]
        df = df_ref[...]
        ext[0:HALO, :] = jnp.where(i == 0, 0.0, ap_ref[...])
        ext[HALO:HALO + tm, :] = a
        ext[HALO + tm:2 * HALO + tm, :] = un_ref[0]
        w = cw_ref[...]
        w0, w1, w2 = w[0:1, :], w[1:2, :], w[2:3, :]
        a1 = ext[pl.ds(HALO - 1, ext_rows), :]
        a2 = ext[pl.ds(HALO - 2, ext_rows), :]
        cae = w0 * a2 + w1 * a1 + w2 * ext[pl.ds(HALO, ext_rows), :] + cb_ref[...]
        act, dact = _gelu_parts(cae)
        dfe = jnp.concatenate([df, dfn_ref[0:8, :]], axis=0)
        ge = jnp.concatenate([g, un_ref[1][0:8, :]], axis=0)
        row = lax.broadcasted_iota(jnp.int32, (ext_rows, 1), 0)
        dcae = jnp.where((row < tm) | (i < ni - 1), dfe * ge * dact, 0.0)
        dca_s[...] = dcae
        dca = dcae[0:tm, :]
        dup_a = w2 * dca + w1 * dca_s[pl.ds(1, tm), :] + w0 * dca_s[pl.ds(2, tm), :]
        dup_g = df * act[0:tm, :]
        dup_ref[0] = dup_a.astype(BF)
        dup_ref[1] = dup_g.astype(BF)
        red = jnp.concatenate([
            _rowsum(dca * a2[0:tm, :]), _rowsum(dca * a1[0:tm, :]), _rowsum(dca * a), _rowsum(dca),
            _rowsum(dup_a), _rowsum(dup_g), jnp.zeros((2, fs), F32)], axis=0)

        @pl.when(i == 0)
        def _():
            red_ref[...] = red

        @pl.when(i > 0)
        def _():
            red_ref[...] += red

    nxt = lambda j, i: jnp.minimum((i + 1) * hb, last_hb)
    return _pcall(
        body, (up4, up4, up4, df4, df4, cw, cb), name=name, grid=(nj, ni), sem=("parallel", "arbitrary"), phases=phases,
        in_specs=[pl.BlockSpec((2, None, tm, fs), lambda j, i: (0, j, i, 0)),
                  pl.BlockSpec((None, None, HALO, fs), lambda j, i: (0, j, jnp.maximum(i * hb - 1, 0), 0)),
                  pl.BlockSpec((2, None, HALO, fs), lambda j, i: (0, j, nxt(j, i), 0)),
                  pl.BlockSpec((None, tm, fs), lambda j, i: (j, i, 0)),
                  pl.BlockSpec((None, HALO, fs), lambda j, i: (j, nxt(j, i), 0)),
                  pl.BlockSpec((None, 3, fs), lambda j, i: (j, 0, 0)),
                  pl.BlockSpec((None, 1, fs), lambda j, i: (j, 0, 0))],
        out_specs=[pl.BlockSpec((2, None, tm, fs), lambda j, i: (0, j, i, 0)),
                   pl.BlockSpec((None, 8, fs), lambda j, i: (j, 0, 0))],
        out_shape=[jax.ShapeDtypeStruct((2, nj, m, fs), BF), jax.ShapeDtypeStruct((nj, 8, fs), F32)],
        scratch_shapes=[pltpu.VMEM((2 * HALO + tm, fs), F32), pltpu.VMEM((ext_rows, fs), F32)])


def _gate_bwd(dm, z, ya, yb, ycp, scale, *, name, tm=256):
    m, d = dm.shape
    tm = _row_tile(m, tm)

    def body(dm_ref, ga, gb, gc, ya_ref, yb_ref, yc_ref, sc_ref, dya_ref, dyb_ref, dyc_ref, dz_ref, red_ref):
        i = pl.program_id(0)
        dmv = dm_ref[...]
        sa, sb, sc = jax.nn.sigmoid(ga[...]), jax.nn.sigmoid(gb[...]), jax.nn.sigmoid(gc[...])
        scale_v = sc_ref[...]
        ycp_v = yc_ref[...]
        dya_ref[...] = (dmv * sa).astype(BF)
        dyb_ref[...] = (dmv * sb).astype(BF)
        dyc = dmv * sc
        dyc_ref[...] = (dyc * scale_v).astype(BF)
        dga = dmv * ya_ref[...] * (sa * (1.0 - sa))
        dgb = dmv * yb_ref[...] * (sb * (1.0 - sb))
        dgc = dmv * (ycp_v * scale_v) * (sc * (1.0 - sc))
        dz_ref[:, 0:d] = dga.astype(BF)
        dz_ref[:, d:2 * d] = dgb.astype(BF)
        dz_ref[:, 2 * d:3 * d] = dgc.astype(BF)
        red = jnp.concatenate([_rowsum(dyc * ycp_v), _rowsum(dga), _rowsum(dgb), _rowsum(dgc),
                               jnp.zeros((4, d), F32)], axis=0)

        @pl.when(i == 0)
        def _():
            red_ref[...] = red

        @pl.when(i > 0)
        def _():
            red_ref[...] += red

    row = pl.BlockSpec((tm, d), lambda i: (i, 0))
    obf = jax.ShapeDtypeStruct((m, d), BF)
    return pl.pallas_call(
        body, name=name, grid=(m // tm,),
        in_specs=[row, _seg_spec(tm, d, 6), _seg_spec(tm, d, 7), _seg_spec(tm, d, 8), row, row, row,
                  pl.BlockSpec((1, d), lambda i: (0, 0))],
        out_specs=[row, row, row, pl.BlockSpec((tm, 3 * d), lambda i: (i, 2)), pl.BlockSpec((8, d), lambda i: (0, 0))],
        out_shape=[obf, obf, obf, jax.ShapeDtypeStruct((m, 9 * d), BF), jax.ShapeDtypeStruct((8, d), F32)],
        compiler_params=_params(("arbitrary",)),
    )(dm, z, z, z, ya, yb, ycp, scale)


def _mix_bwd(dz, dua, dub, ddd, z, conv_a, lnv, wm, wmt, bias_full, mask, *, name, tm=128, phases=()):
    m, d = dua.shape
    tm = _row_tile(m, tm)
    ni = m // tm
    grp = d // len(POOL_WINDOWS)
    ng = d // GMLP_BLOCK
    ext_rows = tm + 8

    def body(dz_in, dua_ref, dub_ref, dd_ref, zb, zc, zx, zu, zv, zp, zc_h, zx_h, dua_n, zb_n, dd_n,
             ca_ref, lnv_ref, wm_ref, wmt_ref, bias_ref, mask_ref,
             dz_ref, red_ref, dws_ref, dbs_ref, ext, sh_s, vn_s, mixed_s, dmx_s, dvn_s, dbs_acc):
        del dz_in
        i = pl.program_id(0)
        rows = []
        zbv, zcv, zxv = zb[...], zc[...], zx[...]
        pa = zcv * zxv
        ext[0:HALO, :] = jnp.where(i == 0, 0.0, zc_h[...] * zx_h[...])
        ext[HALO:HALO + tm, :] = pa
        w = ca_ref[...]
        w0, w1, w2 = w[0:1, :], w[1:2, :], w[2:3, :]
        p1 = ext[pl.ds(HALO - 1, tm), :]
        p2 = ext[pl.ds(HALO - 2, tm), :]
        conv = w0 * p2 + w1 * p1 + w2 * pa
        duav = dua_ref[...]
        dzb = duav * conv
        dca = duav * zbv
        dca_n = jnp.where(i < ni - 1, dua_n[0:8, :] * zb_n[0:8, :], 0.0)
        sh_s[0:tm, :] = dca
        sh_s[tm:tm + 8, :] = dca_n
        dpa = w2 * dca + w1 * sh_s[pl.ds(1, tm), :] + w0 * sh_s[pl.ds(2, tm), :]
        dzc = dpa * zxv
        dzx = dpa * zcv
        dz_ref[:, 0:d] = dzb.astype(BF)
        dz_ref[:, d:2 * d] = dzc.astype(BF)
        dz_ref[:, 2 * d:3 * d] = dzx.astype(BF)
        rows += [_rowsum(dzb), _rowsum(dzc), _rowsum(dzx)]
        dconv = [_rowsum(dca * p2), _rowsum(dca * p1), _rowsum(dca * pa)]
        zuv, zvv = zu[...], zv[...]
        gu, dgu_dz = _gelu_parts(zuv)
        gv, dgv_dz = _gelu_parts(zvv)
        vhat, rstd = _ln_stats(gv)
        gain = lnv_ref[0:1, :]
        vn_s[...] = (vhat * gain + lnv_ref[1:2, :]).astype(BF)
        _spatial_mix(wm_ref, vn_s, mixed_s, bias_ref, tm, d)
        dubv = dub_ref[...]
        dzu = dubv * mixed_s[...] * dgu_dz
        dmixed = dubv * gu
        dmx_s[...] = dmixed.astype(BF)
        _spatial_mix(wmt_ref, dmx_s, dvn_s, None, tm, d)
        dvn = dvn_s[...]
        dzv = _ln_bwd(dvn, vhat, rstd, gain) * dgv_dz
        dz_ref[:, 3 * d:4 * d] = dzu.astype(BF)
        dz_ref[:, 4 * d:5 * d] = dzv.astype(BF)
        rows += [_rowsum(dzu), _rowsum(dzv)]
        dlnv = [_rowsum(dvn * vhat), _rowsum(dvn)]
        dbs_part = dmixed[0:GMLP_BLOCK, :]
        for n in range(1, tm // GMLP_BLOCK):
            dbs_part = dbs_part + dmixed[n * GMLP_BLOCK:(n + 1) * GMLP_BLOCK, :]
        ddv = dd_ref[...]
        t = (i * tm + lax.broadcasted_iota(jnp.int32, (ext_rows + 8, 1), 0) + 1).astype(F32)
        dde = jnp.concatenate([ddv, jnp.where(i < ni - 1, dd_n[...], 0.0)], axis=0)
        for k, win in enumerate(POOL_WINDOWS):
            cs = slice(k * grp, (k + 1) * grp)
            ext[0:tm + HALO, cs] = dde[:, cs] / jnp.minimum(t, float(win))
        dzp_parts = []
        for k, win in enumerate(POOL_WINDOWS):
            cs = slice(k * grp, (k + 1) * grp)
            s = ext[0:tm, cs]
            for j in range(1, win):
                s = s + ext[pl.ds(j, tm), cs]
            dzp_parts.append(s - ddv[:, cs])
        dzp = jnp.concatenate(dzp_parts, axis=1)
        dz_ref[:, 5 * d:6 * d] = dzp.astype(BF)
        rows += [_rowsum(dzp)]
        red = jnp.concatenate(rows + dconv + dlnv + [jnp.zeros((5, d), F32)], axis=0)

        @pl.when(i == 0)
        def _():
            red_ref[...] = red
            dbs_acc[...] = dbs_part
            dws_ref[...] = jnp.zeros_like(dws_ref)

        @pl.when(i > 0)
        def _():
            red_ref[...] += red
            dbs_acc[...] += dbs_part

        for n in range(tm // GMLP_BLOCK):
            for g in range(ng):
                rs = slice(n * GMLP_BLOCK, (n + 1) * GMLP_BLOCK)
                cs = slice(g * GMLP_BLOCK, (g + 1) * GMLP_BLOCK)
                dws_ref[g] += mask_ref[...] * lax.dot_general(
                    dmx_s[rs, cs], vn_s[rs, cs], (NT, ((), ())), preferred_element_type=F32)

        @pl.when(i == ni - 1)
        def _():
            lane = lax.broadcasted_iota(jnp.int32, (GMLP_BLOCK, GMLP_BLOCK), 1)
            out = jnp.zeros((GMLP_BLOCK, GMLP_BLOCK), F32)
            for g in range(ng):
                sg = jnp.sum(dbs_acc[:, g * GMLP_BLOCK:(g + 1) * GMLP_BLOCK], axis=1, keepdims=True)
                out = out + jnp.where(lane == g, sg, 0.0)
            dbs_ref[...] = out

    row = pl.BlockSpec((tm, d), lambda i: (i, 0))
    full = lambda a: pl.BlockSpec(a.shape, lambda i: (0,) * a.ndim)
    hb = tm // HALO
    last_hb = m // HALO - 1
    nrow = pl.BlockSpec((HALO, d), lambda i: (jnp.minimum((i + 1) * hb, last_hb), 0))
    return _pcall(
        body, (dz, dua, dub, ddd, z, z, z, z, z, z, z, z, dua, z, ddd, conv_a, lnv, wm, wmt, bias_full, mask),
        name=name, grid=(ni,), sem=("arbitrary",), aliases={0: 0}, phases=phases,
        in_specs=[pl.BlockSpec(memory_space=pl.ANY), row, row, row]
        + [_seg_spec(tm, d, s) for s in range(6)]
        + [_prev_halo_spec(tm, d, 1), _prev_halo_spec(tm, d, 2), nrow, _next_halo_spec(tm, d, 0, m), nrow]
        + [full(conv_a), full(lnv), full(wm), full(wmt), full(bias_full), full(mask)],
        out_specs=[pl.BlockSpec((tm, 6 * d), lambda i: (i, 0)), pl.BlockSpec((16, d), lambda i: (0, 0)),
                   full(wm), pl.BlockSpec((GMLP_BLOCK, GMLP_BLOCK), lambda i: (0, 0))],
        out_shape=[jax.ShapeDtypeStruct(dz.shape, BF), jax.ShapeDtypeStruct((16, d), F32),
                   jax.ShapeDtypeStruct(wm.shape, F32), jax.ShapeDtypeStruct((GMLP_BLOCK, GMLP_BLOCK), F32)],
        scratch_shapes=[pltpu.VMEM((2 * HALO + tm, d), F32), pltpu.VMEM((tm + 8, d), F32),
                        pltpu.VMEM((tm, d), BF), pltpu.VMEM((tm, d), F32), pltpu.VMEM((tm, d), BF),
                        pltpu.VMEM((tm, d), F32), pltpu.VMEM((GMLP_BLOCK, d), F32)])


REST = ("w_a_out", "w_b_out", "w_pool", "w_o", "w_up", "w_down")


def _remote(src, dst, ssem, rsem, k, to):
    return pltpu.make_async_remote_copy(src_ref=src, dst_ref=dst, send_sem=ssem.at[k], recv_sem=rsem.at[k],
                                        device_id=to, device_id_type=MESH)


def _gather_phase1(shards):
    n = len(shards)

    def build(ins, outs, ssem, rsem, lsem):
        x, y, c, chips = _place()
        me = 4 * x + 2 * y + c
        local = [pltpu.make_async_copy(ins[a], outs[a].at[:, me], lsem.at[a]) for a in range(n)]
        sends, recvs = [], []
        for j, (cx, cy) in enumerate(chips):
            for a in range(n):
                sends.append(_remote(ins[a], outs[a].at[:, me], ssem, rsem, 4 * a + 1 + j, (cx, cy, c)))
                recvs.append(_remote(ins[a], outs[a].at[:, 4 * cx + 2 * cy + c], ssem, rsem, 4 * a + 1 + j, (cx, cy, c)))
        for a in range(n):
            sends.append(_remote(ins[a], outs[a].at[:, me], ssem, rsem, 4 * a, (x, y, 1 - c)))
            recvs.append(_remote(ins[a], outs[a].at[:, 4 * x + 2 * y + 1 - c], ssem, rsem, 4 * a, (x, y, 1 - c)))
        return dict(start=local + sends, recv=recvs, send=sends, local=local)

    outs = [jax.ShapeDtypeStruct((s.shape[0], N_DEV) + s.shape[1:], s.dtype) for s in shards]
    return _Phase(shards, outs, {}, 4 * n, n, build)


def _gather_phase2(fulls):
    n = len(fulls)

    def build(ins, outs, ssem, rsem, lsem):
        x, y, c, chips = _place()
        sends, recvs = [], []
        for j, (cx, cy) in enumerate(chips):
            for a in range(n):
                mine, theirs = 4 * cx + 2 * cy + c, 4 * cx + 2 * cy + 1 - c
                sends.append(_remote(ins[a].at[:, mine], outs[a].at[:, mine], ssem, rsem, 3 * a + j, (x, y, 1 - c)))
                recvs.append(_remote(ins[a].at[:, theirs], outs[a].at[:, theirs], ssem, rsem, 3 * a + j, (x, y, 1 - c)))
        return dict(start=sends, recv=recvs, send=sends, local=[])

    outs = [jax.ShapeDtypeStruct(f.shape, f.dtype) for f in fulls]
    return _Phase(fulls, outs, {a: a for a in range(n)}, 3 * n, 0, build)


def _pair_phase(grads):
    n = len(grads)

    def build(ins, outs, ssem, rsem, lsem):
        x, y, c, _ = _place()
        cps = [_remote(ins[a].at[:, 2 * q + (1 - c)], outs[a].at[q], ssem, rsem, 4 * a + q, (x, y, 1 - c))
               for a in range(n) for q in range(4)]
        return dict(start=cps, recv=cps, send=cps, local=[])

    outs = [jax.ShapeDtypeStruct((4, g.shape[0]) + g.shape[2:], g.dtype) for g in grads]
    return _Phase(grads, outs, {}, 4 * n, 0, build)


def _chip_phase(bufs, accs, l, depth):
    n = len(bufs)
    has = accs is not None

    def build(ins, outs, ssem, rsem, lsem):
        x, y, c, chips = _place()
        myq = 2 * x + y
        local, sends, recvs = [], [], []
        for a in range(n):
            local.append(pltpu.make_async_copy(ins[a].at[myq], outs[a].at[myq, l], lsem.at[a]))
            for j, (cx, cy) in enumerate(chips):
                q = 2 * cx + cy
                sends.append(_remote(ins[a].at[q], outs[a].at[myq, l], ssem, rsem, 3 * a + j, (cx, cy, c)))
                recvs.append(_remote(ins[a].at[q], outs[a].at[q, l], ssem, rsem, 3 * a + j, (cx, cy, c)))
        return dict(start=local + sends, recv=recvs, send=sends, local=local)

    outs = [jax.ShapeDtypeStruct((4, depth) + b.shape[1:], b.dtype) for b in bufs]
    return _Phase(list(bufs) + (list(accs) if has else []), outs, {n + a: a for a in range(n)} if has else {},
                  3 * n, n, build)


def _grad_chunks(n, g):
    if n == "w_pool":
        return g.reshape(g.shape[0], N_DEV, g.shape[1] // N_DEV, g.shape[2])
    if n in ("w_in", "w_up"):
        return g[None]
    return g.reshape(1, N_DEV, -1, g.shape[-1])


class _ReduceScatter:
    def __init__(self, depth, cidx):
        self.depth, self.cidx, self.acc, self.count = depth, cidx, {}, 0
        self.small_gathered = None

    def pair(self, names, grads):
        return _pair_phase([_grad_chunks(n, grads[n]) for n in names])

    def sums(self, names, grads, phase):
        out = []
        for n, r1 in zip(names, phase.results):
            out.append(_pair_sum(_grad_chunks(n, grads[n]), r1, self.cidx, name="rs_sum_%d" % self.count))
            self.count += 1
        return out

    def chip(self, names, bufs, l):
        accs = [self.acc[n] for n in names] if names[0] in self.acc else None
        return _chip_phase(bufs, accs, l, self.depth)

    def done(self, names, phase):
        for n, r in zip(names, phase.results):
            self.acc[n] = r


def _rest_views(fulls, d):
    a_out, b_out, pool, o, up, down = fulls
    grp = d // len(POOL_WINDOWS)
    return dict(w_a_out=a_out.reshape(d, d), w_b_out=b_out.reshape(d, d), w_o=o.reshape(d, d),
                w_pool=pool.reshape(len(POOL_WINDOWS), grp, grp), w_up8=up[0],
                wd4=down.reshape(N_DEV // 2, -1, d))


def _layer_fwd(x, w, alpha, tag, rest_shards=None, next_in_shard=None):
    d = x.shape[1]
    g1 = _gather_phase1(rest_shards) if rest_shards is not None else None
    z, h = _mod_matmul(x, w["mod1"], w["w_in8"], w["b_in8"], flat_out=True, name="in_proj" + tag,
                       phases=[g1] if g1 else ())
    g2 = _gather_phase2(g1.results) if g1 else None
    ua, ub, dd = _mix_fwd(z, w["conv_a"], w["lnv"], w["wm"], w["bias_full"], name="mix_fwd" + tag,
                          phases=[g2] if g2 else ())
    if g2:
        w.update(_rest_views(g2.results, d))
    ya = _mm_rows(ua, w["w_a_out"], dn=NN, name="a_out" + tag)
    yb = _mm_rows(ub, w["w_b_out"], dn=NN, name="b_out" + tag)
    ycp = _pool_proj(dd, w["w_pool"], dn=NN, name="pool_proj" + tag)
    merged = _merge(z, ya, yb, ycp, w["pool_scale"], name="merge" + tag)
    o = _mm_rows(merged, w["w_o"], dn=NN, name="o_proj" + tag)
    x1 = _resid_ln(x, o, w["ln1"], alpha, name="ln1" + tag)
    n1 = _gather_phase1([next_in_shard]) if next_in_shard is not None else None
    up8, h2 = _mod_matmul(x1, w["mod2"], w["w_up8"], w["b_up8"], flat_out=False, name="up_proj" + tag,
                          phases=[n1] if n1 else ())
    up4 = up8.reshape((2, up8.shape[0] // 2) + up8.shape[1:])
    n2 = _gather_phase2(n1.results) if n1 else None
    f4 = _ffn_fwd(up4, w["cw"], w["cb"], name="ffn_fwd" + tag, phases=[n2] if n2 else ())
    y2 = _down_proj(f4, w["wd4"], name="down_proj" + tag)
    x2 = _resid_ln(x1, y2, w["ln2"], alpha, name="ln2" + tag)
    saved = dict(x=x, z=z, h=h, ua=ua, ub=ub, dd=dd, ya=ya, yb=yb, ycp=ycp, merged=merged, o=o, x1=x1,
                 up4=up4, h2=h2, f4=f4, y2=y2)
    return x2, saved, (n2.results[0][0] if n2 else None)


def _layer_bwd(dpart, dh_above, xmod_above, m_above, w, s, alpha, tag, l=0, above=None, rs=None, upper_reds=()):
    first, rest = ("w_in",), REST
    ph = lambda p: [p] if p is not None else ()
    r1a = rs.pair(first, above) if above else None
    dy2, dx1p, red2 = _resid_ln_bwd(dpart, dh_above, xmod_above, m_above, s["x1"], s["y2"], w["ln2"], alpha,
                                    name="ln2_bwd" + tag, phases=ph(r1a))
    r1b = rs.pair(rest, above) if above else None
    df4 = _down_bwd(dy2, w["wd4"], name="down_bwd" + tag, phases=ph(r1b))
    if above:
        sb_a, sb_b = rs.sums(first, above, r1a), rs.sums(rest, above, r1b)
    gw_down4 = _tn_shards_lhs(s["f4"], dy2, name="gw_down" + tag)
    if above:
        bufs = dict(zip(first + rest, sb_a + sb_b))
        heavy = ("w_in", "w_up")
        light = tuple(n for n in rest if n not in heavy)
    r3a = rs.chip(heavy, [bufs[n] for n in heavy], l + 1) if above else None
    dup4, redf = _ffn_bwd(s["up4"], df4, w["cw"], w["cb"], name="ffn_bwd" + tag, phases=ph(r3a))
    dup8 = dup4.reshape((dup4.shape[0] * dup4.shape[1],) + dup4.shape[2:])
    r3b = None
    if above:
        rs.done(heavy, r3a)
        r3b = rs.chip(light, [bufs[n] for n in light], l + 1)
    gw_up8 = _tn_shards_rhs(s["h2"], dup8, name="gw_up" + tag, phases=ph(r3b))
    if above:
        rs.done(light, r3b)
    own = rs is not None and l == 0
    big = dict(w_up=gw_up8, w_down=gw_down4)
    early = ("w_down", "w_up")
    o1 = rs.pair(early, big) if own else None
    dh2 = _nt_shards(dup8, w["w_up8"], name="up_bwd" + tag, phases=ph(o1))
    if own:
        sb_o = rs.sums(early, big, o1)
    do, dxp, red1 = _resid_ln_bwd(dx1p, dh2, s["x1"], w["mod2"][0:1], s["x"], s["o"], w["ln1"], alpha,
                                  name="ln1_bwd" + tag)
    dm = _mm_rows(do, w["w_o"], dn=NT, name="o_bwd" + tag)
    big["w_o"] = _mm_tn(s["merged"], do, name="gw_o" + tag)
    dya, dyb, dyc, dz, redg = _gate_bwd(dm, s["z"], s["ya"], s["yb"], s["ycp"], w["pool_scale"], name="gate_bwd" + tag)
    dua = _mm_rows(dya, w["w_a_out"], dn=NT, name="a_out_bwd" + tag)
    dub = _mm_rows(dyb, w["w_b_out"], dn=NT, name="b_out_bwd" + tag)
    ddd = _pool_proj(dyc, w["w_pool"], dn=NT, name="pool_bwd" + tag)
    big["w_a_out"] = _mm_tn(s["ua"], dya, name="gw_a_out" + tag)
    big["w_b_out"] = _mm_tn(s["ub"], dyb, name="gw_b_out" + tag)
    big["w_pool"] = _tn_pool(s["dd"], dyc, w["w_pool"].shape[0], name="gw_pool" + tag)
    o3 = rs.chip(early, sb_o, l) if own else None
    dz, redm, dws, dbs = _mix_bwd(dz, dua, dub, ddd, s["z"], w["conv_a"], w["lnv"], w["wm"], w["wmt"],
                                  w["bias_full"], w["mask"], name="mix_bwd" + tag, phases=ph(o3))
    reds = dict(red2=red2, redf=redf, red1=red1, redg=redg, redm=redm, dws=dws, dbs=dbs)
    mid = ("w_o", "w_a_out", "w_b_out", "w_pool")
    o1b = sg1 = None
    if own:
        rs.done(early, o3)
        o1b = rs.pair(mid, big)
        sg1 = _gather_phase1([_small_payload([reds] + list(upper_reds))])
    big["w_in"] = _tn_cols_rhs(s["h"], dz, w["w_in8"].shape[0], name="gw_in" + tag,
                               phases=[o1b, sg1] if own else ())
    tail_phases, pending = (), None
    if own:
        sb_m = rs.sums(mid, big, o1b)
        o1c, o3b, sg2 = rs.pair(first, big), rs.chip(mid, sb_m, l), _gather_phase2(sg1.results)
        tail_phases = [o1c, o3b, sg2]
    dh = _nt_cols(dz, w["w_in8"], name="in_bwd" + tag, phases=tail_phases)
    if own:
        rs.done(mid, o3b)
        rs.small_gathered = sg2.results[0]
        pending = (first, rs.chip(first, rs.sums(first, big, o1c), l))
    return dxp, dh, big, reds, pending


def _local_step(x, tgt, ws, alpha, shards=None, rs=None):
    depth = len(ws)
    saved = []
    y = x
    for l in range(depth):
        rest = [shards[l][n] for n in REST] if shards else None
        nxt = shards[l + 1]["w_in"] if shards and l + 1 < depth else None
        y, s, next_in = _layer_fwd(y, ws[l], alpha, "_l%d" % l, rest, nxt)
        if next_in is not None:
            ws[l + 1]["w_in8"] = next_in
        saved.append(s)
    dpart, loss_blk = _loss_grad(y, tgt, name="loss_grad")
    dh = xmod = mvec = above = pending = None
    bigs, reds = [None] * depth, [None] * depth
    for l in reversed(range(depth)):
        dpart, dh, bigs[l], reds[l], pending = _layer_bwd(dpart, dh, xmod, mvec, ws[l], saved[l], alpha, "_l%d" % l,
                                                          l, above if rs else None, rs, reds[l + 1:])
        xmod, mvec, above = saved[l]["x"], ws[l]["mod1"][0:1], bigs[l]
    grad_x, red0 = _resid_ln_bwd(dpart, dh, xmod, mvec, None, None, None, alpha, name="in_bwd_tail",
                                 phases=[pending[1]] if pending else ())
    if pending:
        rs.done(*pending)
    d_ada = []
    for l in range(depth):
        below = red0 if l == 0 else reds[l - 1]["red2"]
        r1, r2 = reds[l]["red1"], reds[l]["red2"]
        d_ada.append(jnp.stack([below[1], below[0], r1[4], r1[1], r1[0], r2[4]]))
    return loss_blk, grad_x, bigs, reds, jnp.stack(d_ada)


def _small_grads(r):
    redm, redg, redf = r["redm"], r["redg"], r["redf"]
    ng = r["dws"].shape[0]
    return dict(
        b_in=jnp.concatenate([redm[0:6], redg[1:4]], axis=0).reshape(-1),
        conv_a=redm[6:9], ln_v_g=redm[9], ln_v_b=redm[10],
        w_spatial=r["dws"], b_spatial=r["dbs"][:, :ng].T,
        pool_scale=redg[0], ln1_g=r["red1"][2], ln1_b=r["red1"][3],
        b_up=jnp.concatenate([redf[:, 4, :].reshape(-1), redf[:, 5, :].reshape(-1)]),
        conv_ffn=jnp.transpose(redf[:, 0:3, :], (1, 0, 2)).reshape(3, -1), conv_ffn_b=redf[:, 3, :].reshape(-1),
        ln2_g=r["red2"][2], ln2_b=r["red2"][3])


def _small_payload(reds):
    smalls = [_small_grads(r) for r in reds]
    order = SMALL_REPLICATED + SMALL_SHARDED
    flat = jnp.concatenate([smalls[l][n].reshape(-1) for n in order for l in range(len(reds))])
    return _as_rows(flat)[None]


def _layer_weights(l, ada, conv_a, conv_ffn, p):
    sh1, sc1, gt1, sh2, sc2, gt2 = (ada[l, k][None, :] for k in range(6))
    nb = N_DEV
    fs = p["b_up"].shape[1] // nb
    nj = nb // 2
    pos = jnp.arange(GMLP_BLOCK)
    allowed = (pos[None, :] // CHUNK) <= (pos[:, None] // CHUNK)
    wmask = jnp.where(allowed[None], p["w_spatial"][l], 0.0)
    return dict(
        mod1=jnp.concatenate([1.0 + sc1, sh1]), mod2=jnp.concatenate([1.0 + sc2, sh2]),
        ln1=jnp.concatenate([gt1, p["ln1_g"][l][None], p["ln1_b"][l][None]]),
        ln2=jnp.concatenate([gt2, p["ln2_g"][l][None], p["ln2_b"][l][None]]),
        b_in8=p["b_in"][l].reshape(N_DEV, 1, -1), b_up8=p["b_up"][l].reshape(nb, 1, fs),
        conv_a=conv_a[l], lnv=jnp.stack([p["ln_v_g"][l], p["ln_v_b"][l]]),
        wm=wmask.astype(BF), wmt=jnp.transpose(wmask, (0, 2, 1)).astype(BF),
        bias_full=jnp.repeat(p["b_spatial"][l].T, GMLP_BLOCK, axis=1), mask=allowed.astype(F32),
        pool_scale=p["pool_scale"][l][None],
        cw=jnp.transpose(conv_ffn[l].reshape(3, nj, fs), (1, 0, 2)), cb=p["conv_ffn_b"][l].reshape(nj, 1, fs))


ANY = pl.BlockSpec(memory_space=pl.ANY)


def _place():
    x, y, c = lax.axis_index("x"), lax.axis_index("y"), lax.axis_index("c")
    chips = [(1 - x, y), (x, 1 - y), (1 - x, 1 - y)]
    return x, y, c, chips


def _allgather_vmem(xs, *, name):
    r, cdim = xs.shape

    def body(x_ref, out_ref, send_sems, recv_sems, local_sem):
        x, y, c, chips = _place()
        me, sibling = (x, y, c), (x, y, 1 - c)

        def rows(px, py, pc):
            return out_ref.at[pl.ds((4 * px + 2 * py + pc) * r, r), :]

        def copy(k, block, to, src=None):
            return pltpu.make_async_remote_copy(
                src_ref=rows(*block) if src is None else src, dst_ref=rows(*block),
                send_sem=send_sems.at[k], recv_sem=recv_sems.at[k], device_id=to, device_id_type=MESH)

        mine = pltpu.make_async_copy(x_ref, rows(*me), local_sem)
        mine.start()
        first = [copy(0, me, sibling, src=x_ref)]
        first += [copy(1 + j, me, (*chip, c), src=x_ref) for j, chip in enumerate(chips)]
        for cp in first:
            cp.start()
        passed = [copy(4 + j, (*chip, c), sibling) for j, chip in enumerate(chips)]
        for j, chip in enumerate(chips):
            copy(1 + j, (*chip, c), me).wait_recv()
            passed[j].start()
        copy(0, sibling, me).wait_recv()
        for j, chip in enumerate(chips):
            copy(4 + j, (*chip, 1 - c), me).wait_recv()
        for cp in first + passed:
            cp.wait_send()
        mine.wait()

    return pl.pallas_call(
        body, name=name, out_shape=jax.ShapeDtypeStruct((N_DEV * r, cdim), xs.dtype),
        in_specs=[pl.BlockSpec(memory_space=pltpu.VMEM)], out_specs=pl.BlockSpec(memory_space=pltpu.VMEM),
        scratch_shapes=[pltpu.SemaphoreType.DMA((7,)), pltpu.SemaphoreType.DMA((7,)), pltpu.SemaphoreType.DMA],
        compiler_params=_params(),
    )(xs)


def _gather_weights(shards, *, name):
    n = len(shards)

    def body(*refs):
        ins, outs = refs[:n], refs[n:2 * n]
        send_sems, recv_sems, local_sems = refs[2 * n:]
        x, y, c, chips = _place()
        me, sibling = (x, y, c), (x, y, 1 - c)

        def slot(a, px, py, pc):
            return outs[a].at[:, 4 * px + 2 * py + pc]

        def copy(a, k, block, to, src=None):
            return pltpu.make_async_remote_copy(
                src_ref=slot(a, *block) if src is None else src, dst_ref=slot(a, *block),
                send_sem=send_sems.at[7 * a + k], recv_sem=recv_sems.at[7 * a + k], device_id=to,
                device_id_type=MESH)

        mine = [pltpu.make_async_copy(ins[a], slot(a, *me), local_sems.at[a]) for a in range(n)]
        for cp in mine:
            cp.start()
        first = []
        for j, chip in enumerate(chips):
            first += [copy(a, 1 + j, me, (*chip, c), src=ins[a]) for a in range(n)]
        first += [copy(a, 0, me, sibling, src=ins[a]) for a in range(n)]
        for cp in first:
            cp.start()
        passed = []
        for j, chip in enumerate(chips):
            for a in range(n):
                copy(a, 1 + j, (*chip, c), me).wait_recv()
                fwd = copy(a, 4 + j, (*chip, c), sibling)
                fwd.start()
                passed.append(fwd)
        for a in range(n):
            copy(a, 0, sibling, me).wait_recv()
        for j, chip in enumerate(chips):
            for a in range(n):
                copy(a, 4 + j, (*chip, 1 - c), me).wait_recv()
        for cp in first + passed:
            cp.wait_send()
        for cp in mine:
            cp.wait()

    out_shape = [jax.ShapeDtypeStruct((s.shape[0], N_DEV) + s.shape[1:], s.dtype) for s in shards]
    return pl.pallas_call(
        body, name=name, out_shape=out_shape, in_specs=[ANY] * n, out_specs=[ANY] * n,
        scratch_shapes=[pltpu.SemaphoreType.DMA((7 * n,)), pltpu.SemaphoreType.DMA((7 * n,)),
                        pltpu.SemaphoreType.DMA((n,))],
        compiler_params=_params(),
    )(*shards)


def _pick_tile(r, cap):
    best = None
    for t in range(8, min(r, cap) + 1, 8):
        if r % t == 0:
            best = t
    return best if best is not None else r


def _pair_sum(g, r1, cidx, *, name):
    p, _, r, cdim = g.shape
    tr = _pick_tile(r, 256)

    def body(c_ref, g_ref, r_ref, o_ref):
        del c_ref
        o_ref[...] = (g_ref[...].astype(F32) + r_ref[...].astype(F32)).astype(BF)

    grid_spec = pltpu.PrefetchScalarGridSpec(
        num_scalar_prefetch=1, grid=(4, r // tr),
        in_specs=[pl.BlockSpec((p, None, tr, cdim), lambda q, i, c: (0, 2 * q + c[0], i, 0)),
                  pl.BlockSpec((None, p, tr, cdim), lambda q, i, c: (q, 0, i, 0))],
        out_specs=pl.BlockSpec((None, p, tr, cdim), lambda q, i, c: (q, 0, i, 0)))
    return pl.pallas_call(
        body, name=name, grid_spec=grid_spec, out_shape=jax.ShapeDtypeStruct((4, p, r, cdim), BF),
        compiler_params=_params(("arbitrary", "arbitrary")),
    )(cidx, g, r1)


def _ada_fwd(c_all, w_ada, *, name):
    depth, d, ns = w_ada.shape
    nb = c_all.shape[0]

    def body(c_ref, w_ref, o_ref):
        cv = c_ref[...]
        act = cv * jax.nn.sigmoid(cv)
        o_ref[...] = jnp.dot(act, w_ref[...], preferred_element_type=F32, precision=lax.Precision.HIGHEST)

    return pl.pallas_call(
        body, name=name, grid=(depth,),
        in_specs=[pl.BlockSpec((nb, d), lambda l: (0, 0)), pl.BlockSpec((None, d, ns), lambda l: (l, 0, 0))],
        out_specs=pl.BlockSpec((None, nb, ns), lambda l: (l, 0, 0)),
        out_shape=jax.ShapeDtypeStruct((depth, nb, ns), F32), compiler_params=_params(("parallel",)),
    )(c_all, w_ada)


def _ada_bwd(ct, dmine, dall, *, name):
    depth, nb, ns = dmine.shape
    d = ct.shape[0]

    def body(ct_ref, dm_ref, da_ref, gw_ref, gb_ref):
        cv = ct_ref[...]
        act = cv * jax.nn.sigmoid(cv)
        gw_ref[...] = jnp.dot(act, dm_ref[...], preferred_element_type=F32, precision=lax.Precision.HIGHEST)
        s = da_ref[0]
        for b in range(1, nb):
            s = s + da_ref[b]
        gb_ref[...] = s

    return pl.pallas_call(
        body, name=name, grid=(depth,),
        in_specs=[pl.BlockSpec((d, nb), lambda l: (0, 0)), pl.BlockSpec((None, nb, ns), lambda l: (l, 0, 0)),
                  pl.BlockSpec(dall.shape, lambda l: (0, 0, 0))],
        out_specs=[pl.BlockSpec((None, d, ns), lambda l: (l, 0, 0)), pl.BlockSpec(dall.shape[1:], lambda l: (0, 0))],
        out_shape=[jax.ShapeDtypeStruct((depth, d, ns), F32), jax.ShapeDtypeStruct(dall.shape[1:], F32)],
        compiler_params=_params(("arbitrary",)),
    )(ct, dmine, dall)


def _sum_parts(parts, *, name):
    p, r, cdim = parts.shape
    tr = _pick_tile(r, 512)

    def body(p_ref, o_ref):
        s = p_ref[0]
        for k in range(1, p):
            s = s + p_ref[k]
        o_ref[...] = s

    return pl.pallas_call(
        body, name=name, grid=(r // tr,),
        in_specs=[pl.BlockSpec((p, tr, cdim), lambda i: (0, i, 0))], out_specs=pl.BlockSpec((tr, cdim), lambda i: (i, 0)),
        out_shape=jax.ShapeDtypeStruct((r, cdim), F32), compiler_params=_params(("parallel",)),
    )(parts)


def _adamw(parts, w, m, v, *, name):
    p, depth, r, cdim = parts.shape
    tr = _pick_tile(r, 256)

    def body(p_ref, w_ref, m_ref, v_ref, g_out, d_out, m_out, v_out):
        g = p_ref[0].astype(F32)
        for k in range(1, p):
            g = g + p_ref[k].astype(F32)
        m2 = ADAM_B1 * m_ref[...] + (1.0 - ADAM_B1) * g
        v2 = ADAM_B2 * v_ref[...] + (1.0 - ADAM_B2) * (g * g)
        m_hat = m2 / (1.0 - ADAM_B1 ** ADAM_STEP)
        v_hat = v2 / (1.0 - ADAM_B2 ** ADAM_STEP)
        g_out[...] = g
        d_out[...] = -ADAM_LR * (m_hat / (jnp.sqrt(v_hat) + ADAM_EPS) + ADAM_WD * w_ref[...])
        m_out[...] = m2
        v_out[...] = v2

    blk = pl.BlockSpec((None, tr, cdim), lambda l, i: (l, i, 0))
    out = jax.ShapeDtypeStruct((depth, r, cdim), F32)
    return pl.pallas_call(
        body, name=name, grid=(depth, r // tr),
        in_specs=[pl.BlockSpec((p, None, tr, cdim), lambda l, i: (0, l, i, 0)), blk, blk, blk],
        out_specs=[blk, blk, blk, blk], out_shape=[out, out, out, out],
        compiler_params=_params(("parallel", "parallel")),
    )(parts, w, m, v)


BIG = ("w_in", "w_a_out", "w_b_out", "w_pool", "w_o", "w_up", "w_down")
SMALL_REPLICATED = ("b_in", "ln_v_g", "ln_v_b", "w_spatial", "b_spatial", "pool_scale", "ln1_g", "ln1_b", "b_up",
                    "conv_ffn_b", "ln2_g", "ln2_b")
SMALL_SHARDED = ("conv_a", "conv_ffn")
WEIGHTS = ("w_ada", "b_ada", "w_in", "b_in", "conv_a", "w_a_out", "ln_v_g", "ln_v_b", "w_spatial", "b_spatial",
           "w_b_out", "w_pool", "pool_scale", "w_o", "ln1_g", "ln1_b", "w_up", "b_up", "conv_ffn", "conv_ffn_b",
           "w_down", "ln2_g", "ln2_b")
LANES = 128


def _as_rows(flat, mult=8):
    n = flat.shape[0]
    pad = (-n) % (LANES * mult)
    if pad:
        flat = jnp.concatenate([flat, jnp.zeros((pad,), flat.dtype)])
    return flat.reshape(-1, LANES)


def _shard3(a):
    return a.reshape((-1,) + a.shape[-2:])


def kernel(x, c, w_ada, b_ada, w_in, b_in, conv_a, w_a_out, ln_v_g, ln_v_b, w_spatial, b_spatial, w_b_out, w_pool, pool_scale, w_o, ln1_g, ln1_b, w_up, b_up, conv_ffn, conv_ffn_b, w_down, ln2_g, ln2_b, loss_target, m_w_ada, m_b_ada, m_w_in, m_b_in, m_conv_a, m_w_a_out, m_ln_v_g, m_ln_v_b, m_w_spatial, m_b_spatial, m_w_b_out, m_w_pool, m_pool_scale, m_w_o, m_ln1_g, m_ln1_b, m_w_up, m_b_up, m_conv_ffn, m_conv_ffn_b, m_w_down, m_ln2_g, m_ln2_b, v_w_ada, v_b_ada, v_w_in, v_b_in, v_conv_a, v_w_a_out, v_ln_v_g, v_ln_v_b, v_w_spatial, v_b_spatial, v_w_b_out, v_w_pool, v_pool_scale, v_w_o, v_ln1_g, v_ln1_b, v_w_up, v_b_up, v_conv_ffn, v_conv_ffn_b, v_w_down, v_ln2_g, v_ln2_b):
    p = dict(locals())
    depth, d = w_in.shape[0], w_in.shape[1]
    alpha = (2 * depth) ** 0.25
    me = 4 * lax.axis_index("x") + 2 * lax.axis_index("y") + lax.axis_index("c")
    cidx = lax.axis_index("c").astype(jnp.int32).reshape(1)

    n_ca, n_cf = conv_a.size, conv_ffn.size
    packed = _as_rows(jnp.concatenate([c.reshape(-1), conv_a.reshape(-1), conv_ffn.reshape(-1)]))
    got = _allgather_vmem(packed, name="gather_cond").reshape(N_DEV, -1)
    c_all = got[:, :d]
    ct = c_all.T
    conv_a_full = jnp.transpose(got[:, d:d + n_ca].reshape((N_DEV,) + conv_a.shape), (1, 2, 0, 3)).reshape(depth, 3, -1)
    conv_ffn_full = jnp.transpose(got[:, d + n_ca:d + n_ca + n_cf].reshape((N_DEV,) + conv_ffn.shape),
                                  (1, 2, 0, 3)).reshape(depth, 3, -1)

    ns_ada = w_ada.shape[2]
    ada_part = _ada_fwd(c_all, w_ada, name="ada_fwd")
    ada_all = _allgather_vmem(_as_rows(ada_part.reshape(-1)), name="gather_ada")
    ada_all = ada_all.reshape(N_DEV, depth, N_DEV, ns_ada)
    ada_mine = lax.dynamic_index_in_dim(ada_all, me, axis=2, keepdims=False)
    ada = jnp.transpose(ada_mine, (1, 0, 2)).reshape(depth, -1) + b_ada
    ada = ada.reshape(depth, 6, d)

    shards = [{n: _shard3(p[n][l].astype(BF)) for n in BIG} for l in range(depth)]
    ws = [_layer_weights(l, ada, conv_a_full, conv_ffn_full, p) for l in range(depth)]
    ws[0]["w_in8"] = _gather_weights([shards[0]["w_in"]], name="gather_w_in0")[0][0]

    rs = _ReduceScatter(depth, cidx)
    loss_blk, grad_x, bigs, reds, d_ada = _local_step(x[0], loss_target[0], ws, alpha, shards, rs)
    loss = lax.psum(loss_blk[0, 0], ("x", "y", "c"))

    dada_all = _allgather_vmem(_as_rows(d_ada.reshape(-1)), name="gather_dada")
    dada_all = dada_all.reshape(N_DEV, -1, LANES)
    dflat = dada_all.reshape(N_DEV, depth, 6 * d)
    dmine = lax.dynamic_slice_in_dim(dflat, me * ns_ada, ns_ada, axis=2)
    gw_ada, gb_rows = _ada_bwd(ct, jnp.transpose(dmine, (1, 0, 2)), dada_all, name="ada_bwd")
    gb_ada = gb_rows.reshape(-1)[:depth * 6 * d].reshape(depth, 6 * d)

    out = {}
    for n in BIG:
        parts = rs.acc[n]
        shard_shape = p[n].shape
        w3 = p[n].reshape(depth, -1, shard_shape[-1])
        parts4 = parts.reshape((4,) + w3.shape)
        res = _adamw(parts4, w3, p["m_" + n].reshape(w3.shape), p["v_" + n].reshape(w3.shape), name="adamw_" + n)
        out[n] = [r.reshape(shard_shape) for r in res]
    out["w_ada"] = _adamw(gw_ada[None], w_ada, m_w_ada, v_w_ada, name="adamw_w_ada")

    order = SMALL_REPLICATED + SMALL_SHARDED
    n_rep = sum(p[n].size for n in SMALL_REPLICATED)
    n_pay = n_rep + N_DEV * (conv_a.size + conv_ffn.size)
    gsum = _sum_parts(rs.small_gathered.reshape(N_DEV, -1, LANES), name="sum_small").reshape(-1)[:n_pay]
    ga_full = gsum[n_rep:n_rep + depth * 3 * d].reshape(depth, 3, d)
    gf_full = gsum[n_rep + depth * 3 * d:].reshape(depth, 3, -1)
    ca_w, cf_w = conv_a.shape[2], conv_ffn.shape[2]
    g_ca = lax.dynamic_slice_in_dim(ga_full, me * ca_w, ca_w, axis=2)
    g_cf = lax.dynamic_slice_in_dim(gf_full, me * cf_w, cf_w, axis=2)
    names = ("b_ada",) + order
    gflat = _as_rows(jnp.concatenate([gb_ada.reshape(-1), gsum[:n_rep], g_ca.reshape(-1), g_cf.reshape(-1)]))
    pack = lambda pre: _as_rows(jnp.concatenate([p[pre + n].reshape(-1) for n in names]))
    res = _adamw(gflat[None, None], pack("")[None], pack("m_")[None], pack("v_")[None], name="adamw_small")
    off = 0
    for n in names:
        size = p[n].size
        out[n] = [r.reshape(-1)[off:off + size].reshape(p[n].shape) for r in res]
        off += size

    return (loss, grad_x[None]) + tuple(out[n][k] for k in range(4) for n in WEIGHTS)
```

```python
import functools

import jax
import jax.numpy as jnp
from jax import lax
from jax.experimental import pallas as pl
from jax.experimental.pallas import tpu as pltpu

F32 = jnp.float32
BF = jnp.bfloat16
MESH = pl.DeviceIdType.MESH

LN_EPS = 1e-5
POOL_WINDOWS = (2, 4, 8, 16)
GMLP_BLOCK = 128
CHUNK = 64
HALO = 16
ADAM_LR, ADAM_B1, ADAM_B2, ADAM_EPS, ADAM_WD, ADAM_STEP = 0.001, 0.9, 0.999, 1e-08, 0.01, 10
N_DEV = 8
VMEM_LIMIT = 56 * 1024 * 1024

GRAD_DTYPE = BF
ACT_DTYPE = BF

NN = ((1,), (0,))
NT = ((1,), (1,))
TN = ((0,), (0,))


def _params(sem=None, vmem=VMEM_LIMIT, **kw):
    if sem is not None:
        kw["dimension_semantics"] = sem
    return pltpu.CompilerParams(vmem_limit_bytes=vmem, **kw)


class _Phase:
    def __init__(self, ins, out_shapes, aliases, n_remote, n_local, build):
        self.ins, self.out_shapes, self.aliases = list(ins), list(out_shapes), dict(aliases)
        self.n_remote, self.n_local, self.build = n_remote, n_local, build
        self.results = None


def _pcall(body, args, *, name, grid, in_specs, out_specs, out_shape, scratch_shapes=(), sem=None, aliases=None,
           phases=()):
    aliases = dict(aliases or {})
    if not phases:
        return pl.pallas_call(
            body, name=name, grid=grid, in_specs=list(in_specs), out_specs=out_specs, out_shape=out_shape,
            scratch_shapes=list(scratch_shapes), input_output_aliases=aliases, compiler_params=_params(sem),
        )(*args)
    single = not isinstance(out_shape, (list, tuple))
    o_specs = [out_specs] if single else list(out_specs)
    o_shapes = [out_shape] if single else list(out_shape)
    n_in, n_out, n_scr = len(args), len(o_shapes), len(scratch_shapes)
    ex_args, ex_out, sems = [], [], []
    for ph in phases:
        for src, dst in ph.aliases.items():
            aliases[n_in + len(ex_args) + src] = n_out + len(ex_out) + dst
        ex_args += ph.ins
        ex_out += ph.out_shapes
        sems += [pltpu.SemaphoreType.DMA((max(ph.n_remote, 1),)), pltpu.SemaphoreType.DMA((max(ph.n_remote, 1),)),
                 pltpu.SemaphoreType.DMA((max(ph.n_local, 1),))]

    def wrapped(*refs):
        pos = n_in
        ph_in = []
        for ph in phases:
            ph_in.append(refs[pos:pos + len(ph.ins)])
            pos += len(ph.ins)
        base_out = refs[pos:pos + n_out]
        pos += n_out
        ph_out = []
        for ph in phases:
            ph_out.append(refs[pos:pos + len(ph.out_shapes)])
            pos += len(ph.out_shapes)
        base_scr = refs[pos:pos + n_scr]
        ph_sems = refs[pos + n_scr:]
        first = last = None
        for ax, n in enumerate(grid):
            pid = pl.program_id(ax)
            first = (pid == 0) if first is None else first & (pid == 0)
            last = (pid == n - 1) if last is None else last & (pid == n - 1)

        def ops(k):
            return phases[k].build(ph_in[k], ph_out[k], *ph_sems[3 * k:3 * k + 3])

        @pl.when(first)
        def _():
            for k in range(len(phases)):
                for cp in ops(k)["start"]:
                    cp.start()

        body(*refs[:n_in], *base_out, *base_scr)

        @pl.when(last)
        def _():
            for k in range(len(phases)):
                o = ops(k)
                for cp in o["recv"]:
                    cp.wait_recv()
                for cp in o["send"]:
                    cp.wait_send()
                for cp in o["local"]:
                    cp.wait()

    hbm = pl.BlockSpec(memory_space=pl.ANY)
    res = pl.pallas_call(
        wrapped, name=name, grid=grid, in_specs=list(in_specs) + [hbm] * len(ex_args),
        out_specs=o_specs + [hbm] * len(ex_out), out_shape=o_shapes + ex_out,
        scratch_shapes=list(scratch_shapes) + sems, input_output_aliases=aliases,
        compiler_params=_params(("arbitrary",) * len(grid)),
    )(*args, *ex_args)
    pos = n_out
    for ph in phases:
        ph.results = list(res[pos:pos + len(ph.out_shapes)])
        pos += len(ph.out_shapes)
    return res[0] if single else list(res[:n_out])


def _gelu_parts(x):
    k = 0.7978845608028654
    x2 = x * x
    t = jnp.tanh(k * (x + 0.044715 * (x2 * x)))
    cdf = 0.5 * (1.0 + t)
    dcdf = 0.5 * (1.0 - t * t) * (k * (1.0 + 3.0 * 0.044715 * x2))
    return x * cdf, cdf + x * dcdf


def _gelu(x):
    t = jnp.tanh(0.7978845608028654 * (x + 0.044715 * (x * x * x)))
    return x * (0.5 * (1.0 + t))


def _rowsum(v):
    return jnp.sum(v, axis=0, keepdims=True)


def _ln_stats(r):
    mu = jnp.mean(r, axis=-1, keepdims=True)
    xc = r - mu
    var = jnp.mean(xc * xc, axis=-1, keepdims=True)
    rstd = lax.rsqrt(var + LN_EPS)
    return xc * rstd, rstd


def _ln_bwd(dy, xhat, rstd, gain):
    dxh = dy * gain
    m1 = jnp.mean(dxh, axis=-1, keepdims=True)
    m2 = jnp.mean(dxh * xhat, axis=-1, keepdims=True)
    return rstd * (dxh - m1 - xhat * m2)


def _matmul(a, b, *, dn, grid, a_spec, b_spec, o_spec, out_shape, acc_shape, name, phases=()):
    nk = grid[2]
    direct = out_shape.dtype == F32

    def body(a_ref, b_ref, o_ref, *scratch):
        prod = lax.dot_general(a_ref[...], b_ref[...], (dn, ((), ())), preferred_element_type=F32)
        if nk == 1:
            o_ref[...] = prod.astype(o_ref.dtype)
            return
        acc = o_ref if direct else scratch[0]
        k = pl.program_id(2)

        @pl.when(k == 0)
        def _():
            acc[...] = prod

        @pl.when(k > 0)
        def _():
            acc[...] += prod

        if not direct:
            @pl.when(k == nk - 1)
            def _():
                o_ref[...] = acc[...].astype(o_ref.dtype)

    scratch = [] if (direct or nk == 1) else [pltpu.VMEM(acc_shape, F32)]
    return _pcall(body, (a, b), name=name, grid=grid, in_specs=[a_spec, b_spec], out_specs=o_spec,
                  out_shape=out_shape, scratch_shapes=scratch, sem=("parallel", "parallel", "arbitrary"),
                  phases=phases)


def _row_tile(m, want):
    t = min(m, want)
    assert m % t == 0
    return t


def _mm_rows(a, w, *, dn, name, out_dtype=F32, tm=2048):
    m, k = a.shape
    n = w.shape[1] if dn == NN else w.shape[0]
    tm = _row_tile(m, tm)
    return _matmul(
        a, w, dn=dn, grid=(m // tm, 1, 1), name=name,
        a_spec=pl.BlockSpec((tm, k), lambda i, j, kk: (i, 0)),
        b_spec=pl.BlockSpec(w.shape, lambda i, j, kk: (0, 0)),
        o_spec=pl.BlockSpec((tm, n), lambda i, j, kk: (i, 0)),
        out_shape=jax.ShapeDtypeStruct((m, n), out_dtype), acc_shape=(tm, n))


def _mm_tn(a, b, *, name, tk=512):
    m, ka = a.shape
    n = b.shape[1]
    tk = _row_tile(m, tk)
    return _matmul(
        a, b, dn=TN, grid=(1, 1, m // tk), name=name,
        a_spec=pl.BlockSpec((tk, ka), lambda i, j, kk: (kk, 0)),
        b_spec=pl.BlockSpec((tk, n), lambda i, j, kk: (kk, 0)),
        o_spec=pl.BlockSpec((ka, n), lambda i, j, kk: (0, 0)),
        out_shape=jax.ShapeDtypeStruct((ka, n), GRAD_DTYPE), acc_shape=(ka, n))


def _mod_matmul(x, mod, w8, bias8, *, flat_out, name, tm=2048, phases=()):
    m, k = x.shape
    nb, _, ns = w8.shape
    tm = _row_tile(m, tm)

    def body(x_ref, mod_ref, w_ref, b_ref, o_ref, h_ref, hs):
        @pl.when(pl.program_id(1) == 0)
        def _():
            h = (x_ref[...] * mod_ref[0:1, :] + mod_ref[1:2, :]).astype(BF)
            hs[...] = h
            h_ref[...] = h

        o_ref[...] = (jnp.dot(hs[...], w_ref[...], preferred_element_type=F32) + b_ref[...]).astype(o_ref.dtype)

    if flat_out:
        o_spec = pl.BlockSpec((tm, ns), lambda i, j: (i, j))
        o_shape = jax.ShapeDtypeStruct((m, nb * ns), ACT_DTYPE)
    else:
        o_spec = pl.BlockSpec((None, tm, ns), lambda i, j: (j, i, 0))
        o_shape = jax.ShapeDtypeStruct((nb, m, ns), ACT_DTYPE)
    return _pcall(
        body, (x, mod, w8, bias8), name=name, grid=(m // tm, nb),
        in_specs=[pl.BlockSpec((tm, k), lambda i, j: (i, 0)),
                  pl.BlockSpec((2, k), lambda i, j: (0, 0)),
                  pl.BlockSpec((None, k, ns), lambda i, j: (j, 0, 0)),
                  pl.BlockSpec((None, 1, ns), lambda i, j: (j, 0, 0))],
        out_specs=[o_spec, pl.BlockSpec((tm, k), lambda i, j: (i, 0))],
        out_shape=[o_shape, jax.ShapeDtypeStruct((m, k), BF)],
        scratch_shapes=[pltpu.VMEM((tm, k), BF)], sem=("parallel", "arbitrary"), phases=phases)


def _seg_spec(tm, d, s):
    return pl.BlockSpec((tm, d), lambda i, s=s: (i, s))


def _prev_halo_spec(tm, d, s):
    hb = tm // HALO
    return pl.BlockSpec((HALO, d), lambda i, s=s: (jnp.maximum(i * hb - 1, 0), s))


def _next_halo_spec(tm, d, s, m):
    hb = tm // HALO
    last = m // HALO - 1
    return pl.BlockSpec((HALO, d), lambda i, s=s: (jnp.minimum((i + 1) * hb, last), s))


def _spatial_mix(wm_ref, src, dst, bias_ref, tm, d):
    for n in range(tm // GMLP_BLOCK):
        for g in range(d // GMLP_BLOCK):
            rs = slice(n * GMLP_BLOCK, (n + 1) * GMLP_BLOCK)
            cs = slice(g * GMLP_BLOCK, (g + 1) * GMLP_BLOCK)
            v = jnp.dot(wm_ref[g], src[rs, cs], preferred_element_type=F32)
            if bias_ref is not None:
                v = v + bias_ref[:, cs]
            dst[rs, cs] = v


def _mix_fwd(z, conv_a, lnv, wm, bias_full, *, name, tm=256, phases=()):
    m, d9 = z.shape
    d = d9 // 9
    tm = _row_tile(m, tm)
    grp = d // len(POOL_WINDOWS)

    def body(zb, zc, zx, zu, zv, zp, zc_h, zx_h, zp_h, ca_ref, lnv_ref, wm_ref, bias_ref,
             ua_ref, ub_ref, d_ref, ext, vn_s, mixed_s):
        i = pl.program_id(0)
        first = i == 0
        f32 = lambda r: r[...].astype(F32)
        pa = f32(zc) * f32(zx)
        ext[0:HALO, :] = jnp.where(first, 0.0, f32(zc_h) * f32(zx_h))
        ext[HALO:HALO + tm, :] = pa
        w = ca_ref[...]
        conv = w[0:1, :] * ext[pl.ds(HALO - 2, tm), :] + w[1:2, :] * ext[pl.ds(HALO - 1, tm), :] + w[2:3, :] * pa
        ua_ref[...] = (f32(zb) * conv).astype(BF)
        p = f32(zp)
        ext[0:HALO, :] = jnp.where(first, 0.0, f32(zp_h))
        ext[HALO:HALO + tm, :] = p
        t = (i * tm + lax.broadcasted_iota(jnp.int32, (tm, 1), 0) + 1).astype(F32)
        for k, win in enumerate(POOL_WINDOWS):
            cs = slice(k * grp, (k + 1) * grp)
            s = p[:, cs]
            for j in range(1, win):
                s = s + ext[pl.ds(HALO - j, tm), cs]
            d_ref[:, cs] = (s / jnp.minimum(t, float(win)) - p[:, cs]).astype(BF)
        gv = _gelu(f32(zv))
        vhat, _ = _ln_stats(gv)
        vn_s[...] = (vhat * lnv_ref[0:1, :] + lnv_ref[1:2, :]).astype(BF)
        _spatial_mix(wm_ref, vn_s, mixed_s, bias_ref, tm, d)
        ub_ref[...] = (_gelu(f32(zu)) * mixed_s[...]).astype(BF)

    full = lambda a: pl.BlockSpec(a.shape, lambda i: (0,) * a.ndim)
    out = jax.ShapeDtypeStruct((m, d), BF)
    o_spec = pl.BlockSpec((tm, d), lambda i: (i, 0))
    return _pcall(
        body, (z, z, z, z, z, z, z, z, z, conv_a, lnv, wm, bias_full), name=name, grid=(m // tm,),
        in_specs=[_seg_spec(tm, d, s) for s in range(6)] + [_prev_halo_spec(tm, d, s) for s in (1, 2, 5)]
        + [full(conv_a), full(lnv), full(wm), full(bias_full)],
        out_specs=[o_spec, o_spec, o_spec], out_shape=[out, out, out],
        scratch_shapes=[pltpu.VMEM((HALO + tm, d), F32), pltpu.VMEM((tm, d), BF), pltpu.VMEM((tm, d), F32)],
        sem=("arbitrary",), phases=phases)


def _pool_proj(dd, w_pool, *, dn, name, out_dtype=F32, tm=512):
    m, d = dd.shape
    ng, grp, _ = w_pool.shape
    tm = _row_tile(m, tm)
    return _matmul(
        dd, w_pool, dn=dn, grid=(m // tm, ng, 1), name=name,
        a_spec=pl.BlockSpec((tm, grp), lambda i, j, kk: (i, j)),
        b_spec=pl.BlockSpec((None, grp, grp), lambda i, j, kk: (j, 0, 0)),
        o_spec=pl.BlockSpec((tm, grp), lambda i, j, kk: (i, j)),
        out_shape=jax.ShapeDtypeStruct((m, d), out_dtype), acc_shape=(tm, grp))


def _merge(z, ya, yb, ycp, scale, *, name, tm=512):
    m, d = ya.shape
    tm = _row_tile(m, tm)

    def body(ga, gb, gc, ya_ref, yb_ref, yc_ref, sc_ref, o_ref):
        f32 = lambda r: r[...].astype(F32)
        o_ref[...] = (jax.nn.sigmoid(f32(ga)) * f32(ya_ref) + jax.nn.sigmoid(f32(gb)) * f32(yb_ref)
                      + jax.nn.sigmoid(f32(gc)) * (f32(yc_ref) * sc_ref[...])).astype(BF)

    row = pl.BlockSpec((tm, d), lambda i: (i, 0))
    return pl.pallas_call(
        body, name=name, grid=(m // tm,),
        in_specs=[_seg_spec(tm, d, 6), _seg_spec(tm, d, 7), _seg_spec(tm, d, 8), row, row, row,
                  pl.BlockSpec((1, d), lambda i: (0, 0))],
        out_specs=row, out_shape=jax.ShapeDtypeStruct((m, d), BF),
        compiler_params=_params(("parallel",)),
    )(z, z, z, ya, yb, ycp, scale)


def _resid_ln(xp, ys, vec, alpha, *, name, tm=512):
    m, d = xp.shape
    tm = _row_tile(m, tm)

    def body(xp_ref, ys_ref, v_ref, o_ref):
        xhat, _ = _ln_stats(alpha * xp_ref[...] + v_ref[0:1, :] * ys_ref[...])
        o_ref[...] = xhat * v_ref[1:2, :] + v_ref[2:3, :]

    row = pl.BlockSpec((tm, d), lambda i: (i, 0))
    return pl.pallas_call(
        body, name=name, grid=(m // tm,),
        in_specs=[row, row, pl.BlockSpec(vec.shape, lambda i: (0, 0))],
        out_specs=row, out_shape=jax.ShapeDtypeStruct((m, d), F32),
        compiler_params=_params(("parallel",)),
    )(xp, ys, vec)


def _ffn_fwd(up4, cw, cb, *, name, tm=512, phases=()):
    _, nj, m, fs = up4.shape
    tm = _row_tile(m, tm)
    hb = tm // HALO

    def body(up_ref, ah_ref, cw_ref, cb_ref, f_ref, ext):
        first = pl.program_id(1) == 0
        a = up_ref[0].astype(F32)
        ext[0:HALO, :] = jnp.where(first, 0.0, ah_ref[...].astype(F32))
        ext[HALO:HALO + tm, :] = a
        w = cw_ref[...]
        ca = (w[0:1, :] * ext[pl.ds(HALO - 2, tm), :] + w[1:2, :] * ext[pl.ds(HALO - 1, tm), :]
              + w[2:3, :] * a + cb_ref[...])
        f_ref[...] = (_gelu(ca) * up_ref[1].astype(F32)).astype(BF)

    return _pcall(
        body, (up4, up4, cw, cb), name=name, grid=(nj, m // tm),
        in_specs=[pl.BlockSpec((2, None, tm, fs), lambda j, i: (0, j, i, 0)),
                  pl.BlockSpec((None, None, HALO, fs), lambda j, i: (0, j, jnp.maximum(i * hb - 1, 0), 0)),
                  pl.BlockSpec((None, 3, fs), lambda j, i: (j, 0, 0)),
                  pl.BlockSpec((None, 1, fs), lambda j, i: (j, 0, 0))],
        out_specs=pl.BlockSpec((None, tm, fs), lambda j, i: (j, i, 0)),
        out_shape=jax.ShapeDtypeStruct((nj, m, fs), BF),
        scratch_shapes=[pltpu.VMEM((HALO + tm, fs), F32)], sem=("parallel", "arbitrary"), phases=phases)


def _down_proj(f4, wd4, *, name, tm=2048):
    nj, m, fs = f4.shape
    d = wd4.shape[2]
    tm = _row_tile(m, tm)
    return _matmul(
        f4, wd4, dn=NN, grid=(m // tm, 1, nj), name=name,
        a_spec=pl.BlockSpec((None, tm, fs), lambda i, j, kk: (kk, i, 0)),
        b_spec=pl.BlockSpec((None, fs, d), lambda i, j, kk: (kk, 0, 0)),
        o_spec=pl.BlockSpec((tm, d), lambda i, j, kk: (i, 0)),
        out_shape=jax.ShapeDtypeStruct((m, d), F32), acc_shape=(tm, d))


def _loss_grad(y, tgt, *, name, tm=512):
    m, d = y.shape
    tm = _row_tile(m, tm)
    ni = m // tm

    def body(y_ref, t_ref, dy_ref, l_ref, acc):
        i = pl.program_id(0)
        e = y_ref[...] - t_ref[...]
        dy_ref[...] = e * (1.0 / d)
        part = jnp.sum((e * e).reshape(tm // 8, 8, d), axis=0)

        @pl.when(i == 0)
        def _():
            acc[...] = part

        @pl.when(i > 0)
        def _():
            acc[...] += part

        @pl.when(i == ni - 1)
        def _():
            l_ref[...] = jnp.full((8, 128), 0.5 / d, F32) * jnp.sum(acc[...])

    row = pl.BlockSpec((tm, d), lambda i: (i, 0))
    return pl.pallas_call(
        body, name=name, grid=(ni,), in_specs=[row, row],
        out_specs=[row, pl.BlockSpec((8, 128), lambda i: (0, 0))],
        out_shape=[jax.ShapeDtypeStruct((m, d), F32), jax.ShapeDtypeStruct((8, 128), F32)],
        scratch_shapes=[pltpu.VMEM((8, d), F32)],
        compiler_params=_params(("arbitrary",)),
    )(y, tgt)


def _resid_ln_bwd(dpart, dh, xmod, mvec, xp, ys, vec, alpha, *, name, tm=256, phases=()):
    m, d = dpart.shape
    tm = _row_tile(m, tm)
    has_dh = dh is not None
    has_ln = xp is not None

    def body(*refs):
        refs = list(refs)
        dpart_ref = refs.pop(0)
        if has_dh:
            dh_ref, xm_ref, mv_ref = refs.pop(0), refs.pop(0), refs.pop(0)
        if has_ln:
            xp_ref, ys_ref, v_ref = refs.pop(0), refs.pop(0), refs.pop(0)
            dys_ref, dxp_ref, red_ref = refs
        else:
            dx_ref, red_ref = refs
        i = pl.program_id(0)
        dtot = dpart_ref[...]
        rows = [jnp.zeros((1, d), F32)] * 5
        if has_dh:
            dhv = dh_ref[...]
            dtot = dtot + dhv * mv_ref[...]
            rows[0] = _rowsum(dhv * xm_ref[...])
            rows[1] = _rowsum(dhv)
        if has_ln:
            ys = ys_ref[...]
            gt = v_ref[0:1, :]
            xhat, rstd = _ln_stats(alpha * xp_ref[...] + gt * ys)
            rows[2] = _rowsum(dtot * xhat)
            rows[3] = _rowsum(dtot)
            dr = _ln_bwd(dtot, xhat, rstd, v_ref[1:2, :])
            rows[4] = _rowsum(dr * ys)
            dys_ref[...] = (dr * gt).astype(BF)
            dxp_ref[...] = alpha * dr
        else:
            dx_ref[...] = dtot
        red = jnp.concatenate(rows + [jnp.zeros((3, d), F32)], axis=0)

        @pl.when(i == 0)
        def _():
            red_ref[...] = red

        @pl.when(i > 0)
        def _():
            red_ref[...] += red

    row = pl.BlockSpec((tm, d), lambda i: (i, 0))
    vrow = lambda a: pl.BlockSpec(a.shape, lambda i: (0, 0))
    args, specs = [dpart], [row]
    if has_dh:
        args += [dh, xmod, mvec]
        specs += [row, row, vrow(mvec)]
    if has_ln:
        args += [xp, ys, vec]
        specs += [row, row, vrow(vec)]
        out_specs = [row, row, pl.BlockSpec((8, d), lambda i: (0, 0))]
        out_shape = [jax.ShapeDtypeStruct((m, d), BF), jax.ShapeDtypeStruct((m, d), F32),
                     jax.ShapeDtypeStruct((8, d), F32)]
    else:
        out_specs = [row, pl.BlockSpec((8, d), lambda i: (0, 0))]
        out_shape = [jax.ShapeDtypeStruct((m, d), F32), jax.ShapeDtypeStruct((8, d), F32)]
    return _pcall(body, args, name=name, grid=(m // tm,), in_specs=specs, out_specs=out_specs, out_shape=out_shape,
                  sem=("arbitrary",), phases=phases)


def _down_bwd(dy, wd4, *, name, tm=2048, phases=()):
    m, d = dy.shape
    nj, fs, _ = wd4.shape
    tm = _row_tile(m, tm)
    return _matmul(
        dy, wd4, dn=NT, grid=(m // tm, nj, 1), name=name,
        a_spec=pl.BlockSpec((tm, d), lambda i, j, kk: (i, 0)),
        b_spec=pl.BlockSpec((None, fs, d), lambda i, j, kk: (j, 0, 0)),
        o_spec=pl.BlockSpec((None, tm, fs), lambda i, j, kk: (j, i, 0)),
        out_shape=jax.ShapeDtypeStruct((nj, m, fs), ACT_DTYPE), acc_shape=(tm, fs), phases=phases)


def _tn_shards_lhs(f4, dy, *, name, tk=512):
    nj, m, fs = f4.shape
    d = dy.shape[1]
    tk = _row_tile(m, tk)
    return _matmul(
        f4, dy, dn=TN, grid=(nj, 1, m // tk), name=name,
        a_spec=pl.BlockSpec((None, tk, fs), lambda i, j, kk: (i, kk, 0)),
        b_spec=pl.BlockSpec((tk, d), lambda i, j, kk: (kk, 0)),
        o_spec=pl.BlockSpec((None, fs, d), lambda i, j, kk: (i, 0, 0)),
        out_shape=jax.ShapeDtypeStruct((nj, fs, d), GRAD_DTYPE), acc_shape=(fs, d))


def _tn_shards_rhs(h, d8, *, name, tk=512, phases=()):
    m, k = h.shape
    nb, _, ns = d8.shape
    tk = _row_tile(m, tk)
    return _matmul(
        h, d8, dn=TN, grid=(nb, 1, m // tk), name=name,
        a_spec=pl.BlockSpec((tk, k), lambda i, j, kk: (kk, 0)),
        b_spec=pl.BlockSpec((None, tk, ns), lambda i, j, kk: (i, kk, 0)),
        o_spec=pl.BlockSpec((None, k, ns), lambda i, j, kk: (i, 0, 0)),
        out_shape=jax.ShapeDtypeStruct((nb, k, ns), GRAD_DTYPE), acc_shape=(k, ns), phases=phases)


def _tn_cols_rhs(h, dz, nb, *, name, tk=512, phases=()):
    m, k = h.shape
    ns = dz.shape[1] // nb
    tk = _row_tile(m, tk)
    return _matmul(
        h, dz, dn=TN, grid=(nb, 1, m // tk), name=name,
        a_spec=pl.BlockSpec((tk, k), lambda i, j, kk: (kk, 0)),
        b_spec=pl.BlockSpec((tk, ns), lambda i, j, kk: (kk, i)),
        o_spec=pl.BlockSpec((None, k, ns), lambda i, j, kk: (i, 0, 0)),
        out_shape=jax.ShapeDtypeStruct((nb, k, ns), GRAD_DTYPE), acc_shape=(k, ns), phases=phases)


def _nt_shards(d8, w8, *, name, tm=2048, phases=()):
    nb, m, ns = d8.shape
    k = w8.shape[1]
    tm = _row_tile(m, tm)
    return _matmul(
        d8, w8, dn=NT, grid=(m // tm, 1, nb), name=name,
        a_spec=pl.BlockSpec((None, tm, ns), lambda i, j, kk: (kk, i, 0)),
        b_spec=pl.BlockSpec((None, k, ns), lambda i, j, kk: (kk, 0, 0)),
        o_spec=pl.BlockSpec((tm, k), lambda i, j, kk: (i, 0)),
        out_shape=jax.ShapeDtypeStruct((m, k), F32), acc_shape=(tm, k), phases=phases)


def _nt_cols(dz, w8, *, name, tm=2048, phases=()):
    m = dz.shape[0]
    nb, k, ns = w8.shape
    tm = _row_tile(m, tm)
    return _matmul(
        dz, w8, dn=NT, grid=(m // tm, 1, nb), name=name,
        a_spec=pl.BlockSpec((tm, ns), lambda i, j, kk: (i, kk)),
        b_spec=pl.BlockSpec((None, k, ns), lambda i, j, kk: (kk, 0, 0)),
        o_spec=pl.BlockSpec((tm, k), lambda i, j, kk: (i, 0)),
        out_shape=jax.ShapeDtypeStruct((m, k), F32), acc_shape=(tm, k), phases=phases)


def _tn_pool(dd, dyc, ng, *, name, tk=512):
    m, d = dd.shape
    grp = d // ng
    tk = _row_tile(m, tk)
    return _matmul(
        dd, dyc, dn=TN, grid=(ng, 1, m // tk), name=name,
        a_spec=pl.BlockSpec((tk, grp), lambda i, j, kk: (kk, i)),
        b_spec=pl.BlockSpec((tk, grp), lambda i, j, kk: (kk, i)),
        o_spec=pl.BlockSpec((None, grp, grp), lambda i, j, kk: (i, 0, 0)),
        out_shape=jax.ShapeDtypeStruct((ng, grp, grp), GRAD_DTYPE), acc_shape=(grp, grp))


def _ffn_bwd(up4, df4, cw, cb, *, name, tm=256, phases=()):
    _, nj, m, fs = up4.shape
    tm = _row_tile(m, tm)
    hb = tm // HALO
    ni = m // tm
    last_hb = m // HALO - 1
    ext_rows = tm + 8

    def body(up_ref, ap_ref, un_ref, df_ref, dfn_ref, cw_ref, cb_ref, dup_ref, red_ref, ext, dca_s):
        i = pl.program_id(1)
        a = up_ref[0].astype(F32)
        g = up_ref[1].astype(F32)
        df = df_ref[...].astype(F32)
        ext[0:HALO, :] = jnp.where(i == 0, 0.0, ap_ref[...].astype(F32))
        ext[HALO:HALO + tm, :] = a
        ext[HALO + tm:2 * HALO + tm, :] = un_ref[0].astype(F32)
        w = cw_ref[...]
        w0, w1, w2 = w[0:1, :], w[1:2, :], w[2:3, :]
        a1 = ext[pl.ds(HALO - 1, ext_rows), :]
        a2 = ext[pl.ds(HALO - 2, ext_rows), :]
        cae = w0 * a2 + w1 * a1 + w2 * ext[pl.ds(HALO, ext_rows), :] + cb_ref[...]
        act, dact = _gelu_parts(cae)
        dfe = jnp.concatenate([df, dfn_ref[...].astype(F32)[0:8, :]], axis=0)
        ge = jnp.concatenate([g, un_ref[1].astype(F32)[0:8, :]], axis=0)
        row = lax.broadcasted_iota(jnp.int32, (ext_rows, 1), 0)
        dcae = jnp.where((row < tm) | (i < ni - 1), dfe * ge * dact, 0.0)
        dca_s[...] = dcae
        dca = dcae[0:tm, :]
        dup_a = w2 * dca + w1 * dca_s[pl.ds(1, tm), :] + w0 * dca_s[pl.ds(2, tm), :]
        dup_g = df * act[0:tm, :]
        dup_ref[0] = dup_a.astype(BF)
        dup_ref[1] = dup_g.astype(BF)
        red = jnp.concatenate([
            _rowsum(dca * a2[0:tm, :]), _rowsum(dca * a1[0:tm, :]), _rowsum(dca * a), _rowsum(dca),
            _rowsum(dup_a), _rowsum(dup_g), jnp.zeros((2, fs), F32)], axis=0)

        @pl.when(i == 0)
        def _():
            red_ref[...] = red

        @pl.when(i > 0)
        def _():
            red_ref[...] += red

    nxt = lambda j, i: jnp.minimum((i + 1) * hb, last_hb)
    return _pcall(
        body, (up4, up4, up4, df4, df4, cw, cb), name=name, grid=(nj, ni), sem=("parallel", "arbitrary"), phases=phases,
        in_specs=[pl.BlockSpec((2, None, tm, fs), lambda j, i: (0, j, i, 0)),
                  pl.BlockSpec((None, None, HALO, fs), lambda j, i: (0, j, jnp.maximum(i * hb - 1, 0), 0)),
                  pl.BlockSpec((2, None, HALO, fs), lambda j, i: (0, j, nxt(j, i), 0)),
                  pl.BlockSpec((None, tm, fs), lambda j, i: (j, i, 0)),
                  pl.BlockSpec((None, HALO, fs), lambda j, i: (j, nxt(j, i), 0)),
                  pl.BlockSpec((None, 3, fs), lambda j, i: (j, 0, 0)),
                  pl.BlockSpec((None, 1, fs), lambda j, i: (j, 0, 0))],
        out_specs=[pl.BlockSpec((2, None, tm, fs), lambda j, i: (0, j, i, 0)),
                   pl.BlockSpec((None, 8, fs), lambda j, i: (j, 0, 0))],
        out_shape=[jax.ShapeDtypeStruct((2, nj, m, fs), BF), jax.ShapeDtypeStruct((nj, 8, fs), F32)],
        scratch_shapes=[pltpu.VMEM((2 * HALO + tm, fs), F32), pltpu.VMEM((ext_rows, fs), F32)])


def _gate_bwd(dm, z, ya, yb, ycp, scale, *, name, tm=256):
    m, d = dm.shape
    tm = _row_tile(m, tm)

    def body(dm_ref, ga, gb, gc, ya_ref, yb_ref, yc_ref, sc_ref, dya_ref, dyb_ref, dyc_ref, dz_ref, red_ref):
        i = pl.program_id(0)
        f32 = lambda r: r[...].astype(F32)
        dmv = f32(dm_ref)
        sa, sb, sc = jax.nn.sigmoid(f32(ga)), jax.nn.sigmoid(f32(gb)), jax.nn.sigmoid(f32(gc))
        scale_v = sc_ref[...]
        ycp_v = f32(yc_ref)
        dya_ref[...] = (dmv * sa).astype(BF)
        dyb_ref[...] = (dmv * sb).astype(BF)
        dyc = dmv * sc
        dyc_ref[...] = (dyc * scale_v).astype(BF)
        dga = dmv * f32(ya_ref) * (sa * (1.0 - sa))
        dgb = dmv * f32(yb_ref) * (sb * (1.0 - sb))
        dgc = dmv * (ycp_v * scale_v) * (sc * (1.0 - sc))
        dz_ref[:, 0:d] = dga.astype(BF)
        dz_ref[:, d:2 * d] = dgb.astype(BF)
        dz_ref[:, 2 * d:3 * d] = dgc.astype(BF)
        red = jnp.concatenate([_rowsum(dyc * ycp_v), _rowsum(dga), _rowsum(dgb), _rowsum(dgc),
                               jnp.zeros((4, d), F32)], axis=0)

        @pl.when(i == 0)
        def _():
            red_ref[...] = red

        @pl.when(i > 0)
        def _():
            red_ref[...] += red

    row = pl.BlockSpec((tm, d), lambda i: (i, 0))
    obf = jax.ShapeDtypeStruct((m, d), BF)
    return pl.pallas_call(
        body, name=name, grid=(m // tm,),
        in_specs=[row, _seg_spec(tm, d, 6), _seg_spec(tm, d, 7), _seg_spec(tm, d, 8), row, row, row,
                  pl.BlockSpec((1, d), lambda i: (0, 0))],
        out_specs=[row, row, row, pl.BlockSpec((tm, 3 * d), lambda i: (i, 2)), pl.BlockSpec((8, d), lambda i: (0, 0))],
        out_shape=[obf, obf, obf, jax.ShapeDtypeStruct((m, 9 * d), BF), jax.ShapeDtypeStruct((8, d), F32)],
        compiler_params=_params(("arbitrary",)),
    )(dm, z, z, z, ya, yb, ycp, scale)


def _mix_bwd(dz, dua, dub, ddd, z, conv_a, lnv, wm, wmt, bias_full, mask, *, name, tm=128, phases=()):
    m, d = dua.shape
    tm = _row_tile(m, tm)
    ni = m // tm
    grp = d // len(POOL_WINDOWS)
    ng = d // GMLP_BLOCK
    ext_rows = tm + 8

    def body(dz_in, dua_ref, dub_ref, dd_ref, zb, zc, zx, zu, zv, zp, zc_h, zx_h, dua_n, zb_n, dd_n,
             ca_ref, lnv_ref, wm_ref, wmt_ref, bias_ref, mask_ref,
             dz_ref, red_ref, dws_ref, dbs_ref, ext, sh_s, vn_s, mixed_s, dmx_s, dvn_s, dbs_acc):
        del dz_in
        i = pl.program_id(0)
        rows = []
        f32 = lambda r: r[...].astype(F32)
        zbv, zcv, zxv = f32(zb), f32(zc), f32(zx)
        pa = zcv * zxv
        ext[0:HALO, :] = jnp.where(i == 0, 0.0, f32(zc_h) * f32(zx_h))
        ext[HALO:HALO + tm, :] = pa
        w = ca_ref[...]
        w0, w1, w2 = w[0:1, :], w[1:2, :], w[2:3, :]
        p1 = ext[pl.ds(HALO - 1, tm), :]
        p2 = ext[pl.ds(HALO - 2, tm), :]
        conv = w0 * p2 + w1 * p1 + w2 * pa
        duav = f32(dua_ref)
        dzb = duav * conv
        dca = duav * zbv
        dca_n = jnp.where(i < ni - 1, f32(dua_n)[0:8, :] * f32(zb_n)[0:8, :], 0.0)
        sh_s[0:tm, :] = dca
        sh_s[tm:tm + 8, :] = dca_n
        dpa = w2 * dca + w1 * sh_s[pl.ds(1, tm), :] + w0 * sh_s[pl.ds(2, tm), :]
        dzc = dpa * zxv
        dzx = dpa * zcv
        dz_ref[:, 0:d] = dzb.astype(BF)
        dz_ref[:, d:2 * d] = dzc.astype(BF)
        dz_ref[:, 2 * d:3 * d] = dzx.astype(BF)
        rows += [_rowsum(dzb), _rowsum(dzc), _rowsum(dzx)]
        dconv = [_rowsum(dca * p2), _rowsum(dca * p1), _rowsum(dca * pa)]
        zuv, zvv = f32(zu), f32(zv)
        gu, dgu_dz = _gelu_parts(zuv)
        gv, dgv_dz = _gelu_parts(zvv)
        vhat, rstd = _ln_stats(gv)
        gain = lnv_ref[0:1, :]
        vn_s[...] = (vhat * gain + lnv_ref[1:2, :]).astype(BF)
        _spatial_mix(wm_ref, vn_s, mixed_s, bias_ref, tm, d)
        dubv = f32(dub_ref)
        dzu = dubv * mixed_s[...] * dgu_dz
        dmixed = dubv * gu
        dmx_s[...] = dmixed.astype(BF)
        _spatial_mix(wmt_ref, dmx_s, dvn_s, None, tm, d)
        dvn = dvn_s[...]
        dzv = _ln_bwd(dvn, vhat, rstd, gain) * dgv_dz
        dz_ref[:, 3 * d:4 * d] = dzu.astype(BF)
        dz_ref[:, 4 * d:5 * d] = dzv.astype(BF)
        rows += [_rowsum(dzu), _rowsum(dzv)]
        dlnv = [_rowsum(dvn * vhat), _rowsum(dvn)]
        dbs_part = dmixed[0:GMLP_BLOCK, :]
        for n in range(1, tm // GMLP_BLOCK):
            dbs_part = dbs_part + dmixed[n * GMLP_BLOCK:(n + 1) * GMLP_BLOCK, :]
        ddv = f32(dd_ref)
        t = (i * tm + lax.broadcasted_iota(jnp.int32, (ext_rows + 8, 1), 0) + 1).astype(F32)
        dde = jnp.concatenate([ddv, jnp.where(i < ni - 1, f32(dd_n), 0.0)], axis=0)
        for k, win in enumerate(POOL_WINDOWS):
            cs = slice(k * grp, (k + 1) * grp)
            ext[0:tm + HALO, cs] = dde[:, cs] / jnp.minimum(t, float(win))
        dzp_parts = []
        for k, win in enumerate(POOL_WINDOWS):
            cs = slice(k * grp, (k + 1) * grp)
            s = ext[0:tm, cs]
            for j in range(1, win):
                s = s + ext[pl.ds(j, tm), cs]
            dzp_parts.append(s - ddv[:, cs])
        dzp = jnp.concatenate(dzp_parts, axis=1)
        dz_ref[:, 5 * d:6 * d] = dzp.astype(BF)
        rows += [_rowsum(dzp)]
        red = jnp.concatenate(rows + dconv + dlnv + [jnp.zeros((5, d), F32)], axis=0)

        @pl.when(i == 0)
        def _():
            red_ref[...] = red
            dbs_acc[...] = dbs_part
            dws_ref[...] = jnp.zeros_like(dws_ref)

        @pl.when(i > 0)
        def _():
            red_ref[...] += red
            dbs_acc[...] += dbs_part

        for n in range(tm // GMLP_BLOCK):
            for g in range(ng):
                rs = slice(n * GMLP_BLOCK, (n + 1) * GMLP_BLOCK)
                cs = slice(g * GMLP_BLOCK, (g + 1) * GMLP_BLOCK)
                dws_ref[g] += mask_ref[...] * lax.dot_general(
                    dmx_s[rs, cs], vn_s[rs, cs], (NT, ((), ())), preferred_element_type=F32)

        @pl.when(i == ni - 1)
        def _():
            lane = lax.broadcasted_iota(jnp.int32, (GMLP_BLOCK, GMLP_BLOCK), 1)
            out = jnp.zeros((GMLP_BLOCK, GMLP_BLOCK), F32)
            for g in range(ng):
                sg = jnp.sum(dbs_acc[:, g * GMLP_BLOCK:(g + 1) * GMLP_BLOCK], axis=1, keepdims=True)
                out = out + jnp.where(lane == g, sg, 0.0)
            dbs_ref[...] = out

    row = pl.BlockSpec((tm, d), lambda i: (i, 0))
    full = lambda a: pl.BlockSpec(a.shape, lambda i: (0,) * a.ndim)
    hb = tm // HALO
    last_hb = m // HALO - 1
    nrow = pl.BlockSpec((HALO, d), lambda i: (jnp.minimum((i + 1) * hb, last_hb), 0))
    return _pcall(
        body, (dz, dua, dub, ddd, z, z, z, z, z, z, z, z, dua, z, ddd, conv_a, lnv, wm, wmt, bias_full, mask),
        name=name, grid=(ni,), sem=("arbitrary",), aliases={0: 0}, phases=phases,
        in_specs=[pl.BlockSpec(memory_space=pl.ANY), row, row, row]
        + [_seg_spec(tm, d, s) for s in range(6)]
        + [_prev_halo_spec(tm, d, 1), _prev_halo_spec(tm, d, 2), nrow, _next_halo_spec(tm, d, 0, m), nrow]
        + [full(conv_a), full(lnv), full(wm), full(wmt), full(bias_full), full(mask)],
        out_specs=[pl.BlockSpec((tm, 6 * d), lambda i: (i, 0)), pl.BlockSpec((16, d), lambda i: (0, 0)),
                   full(wm), pl.BlockSpec((GMLP_BLOCK, GMLP_BLOCK), lambda i: (0, 0))],
        out_shape=[jax.ShapeDtypeStruct(dz.shape, BF), jax.ShapeDtypeStruct((16, d), F32),
                   jax.ShapeDtypeStruct(wm.shape, F32), jax.ShapeDtypeStruct((GMLP_BLOCK, GMLP_BLOCK), F32)],
        scratch_shapes=[pltpu.VMEM((2 * HALO + tm, d), F32), pltpu.VMEM((tm + 8, d), F32),
                        pltpu.VMEM((tm, d), BF), pltpu.VMEM((tm, d), F32), pltpu.VMEM((tm, d), BF),
                        pltpu.VMEM((tm, d), F32), pltpu.VMEM((GMLP_BLOCK, d), F32)])


REST = ("w_a_out", "w_b_out", "w_pool", "w_o", "w_up", "w_down")


def _remote(src, dst, ssem, rsem, k, to):
    return pltpu.make_async_remote_copy(src_ref=src, dst_ref=dst, send_sem=ssem.at[k], recv_sem=rsem.at[k],
                                        device_id=to, device_id_type=MESH)


def _gather_phase1(shards):
    n = len(shards)

    def build(ins, outs, ssem, rsem, lsem):
        x, y, c, chips = _place()
        me = 4 * x + 2 * y + c
        local = [pltpu.make_async_copy(ins[a], outs[a].at[:, me], lsem.at[a]) for a in range(n)]
        sends, recvs = [], []
        for j, (cx, cy) in enumerate(chips):
            for a in range(n):
                sends.append(_remote(ins[a], outs[a].at[:, me], ssem, rsem, 4 * a + 1 + j, (cx, cy, c)))
                recvs.append(_remote(ins[a], outs[a].at[:, 4 * cx + 2 * cy + c], ssem, rsem, 4 * a + 1 + j, (cx, cy, c)))
        for a in range(n):
            sends.append(_remote(ins[a], outs[a].at[:, me], ssem, rsem, 4 * a, (x, y, 1 - c)))
            recvs.append(_remote(ins[a], outs[a].at[:, 4 * x + 2 * y + 1 - c], ssem, rsem, 4 * a, (x, y, 1 - c)))
        return dict(start=local + sends, recv=recvs, send=sends, local=local)

    outs = [jax.ShapeDtypeStruct((s.shape[0], N_DEV) + s.shape[1:], s.dtype) for s in shards]
    return _Phase(shards, outs, {}, 4 * n, n, build)


def _gather_phase2(fulls):
    n = len(fulls)

    def build(ins, outs, ssem, rsem, lsem):
        x, y, c, chips = _place()
        sends, recvs = [], []
        for j, (cx, cy) in enumerate(chips):
            for a in range(n):
                mine, theirs = 4 * cx + 2 * cy + c, 4 * cx + 2 * cy + 1 - c
                sends.append(_remote(ins[a].at[:, mine], outs[a].at[:, mine], ssem, rsem, 3 * a + j, (x, y, 1 - c)))
                recvs.append(_remote(ins[a].at[:, theirs], outs[a].at[:, theirs], ssem, rsem, 3 * a + j, (x, y, 1 - c)))
        return dict(start=sends, recv=recvs, send=sends, local=[])

    outs = [jax.ShapeDtypeStruct(f.shape, f.dtype) for f in fulls]
    return _Phase(fulls, outs, {a: a for a in range(n)}, 3 * n, 0, build)


def _pair_phase(grads):
    n = len(grads)

    def build(ins, outs, ssem, rsem, lsem):
        x, y, c, _ = _place()
        cps = [_remote(ins[a].at[:, 2 * q + (1 - c)], outs[a].at[q], ssem, rsem, 4 * a + q, (x, y, 1 - c))
               for a in range(n) for q in range(4)]
        return dict(start=cps, recv=cps, send=cps, local=[])

    outs = [jax.ShapeDtypeStruct((4, g.shape[0]) + g.shape[2:], g.dtype) for g in grads]
    return _Phase(grads, outs, {}, 4 * n, 0, build)


def _chip_phase(bufs, accs, l, depth):
    n = len(bufs)
    has = accs is not None

    def build(ins, outs, ssem, rsem, lsem):
        x, y, c, chips = _place()
        myq = 2 * x + y
        local, sends, recvs = [], [], []
        for a in range(n):
            local.append(pltpu.make_async_copy(ins[a].at[myq], outs[a].at[myq, l], lsem.at[a]))
            for j, (cx, cy) in enumerate(chips):
                q = 2 * cx + cy
                sends.append(_remote(ins[a].at[q], outs[a].at[myq, l], ssem, rsem, 3 * a + j, (cx, cy, c)))
                recvs.append(_remote(ins[a].at[q], outs[a].at[q, l], ssem, rsem, 3 * a + j, (cx, cy, c)))
        return dict(start=local + sends, recv=recvs, send=sends, local=local)

    outs = [jax.ShapeDtypeStruct((4, depth) + b.shape[1:], b.dtype) for b in bufs]
    return _Phase(list(bufs) + (list(accs) if has else []), outs, {n + a: a for a in range(n)} if has else {},
                  3 * n, n, build)


def _grad_chunks(n, g):
    if n == "w_pool":
        return g.reshape(g.shape[0], N_DEV, g.shape[1] // N_DEV, g.shape[2])
    if n in ("w_in", "w_up"):
        return g[None]
    return g.reshape(1, N_DEV, -1, g.shape[-1])


class _ReduceScatter:
    def __init__(self, depth, cidx):
        self.depth, self.cidx, self.acc, self.count = depth, cidx, {}, 0
        self.small_gathered = None

    def pair(self, names, grads):
        return _pair_phase([_grad_chunks(n, grads[n]) for n in names])

    def sums(self, names, grads, phase):
        out = []
        for n, r1 in zip(names, phase.results):
            out.append(_pair_sum(_grad_chunks(n, grads[n]), r1, self.cidx, name="rs_sum_%d" % self.count))
            self.count += 1
        return out

    def chip(self, names, bufs, l):
        accs = [self.acc[n] for n in names] if names[0] in self.acc else None
        return _chip_phase(bufs, accs, l, self.depth)

    def done(self, names, phase):
        for n, r in zip(names, phase.results):
            self.acc[n] = r


def _rest_views(fulls, d):
    a_out, b_out, pool, o, up, down = fulls
    grp = d // len(POOL_WINDOWS)
    return dict(w_a_out=a_out.reshape(d, d), w_b_out=b_out.reshape(d, d), w_o=o.reshape(d, d),
                w_pool=pool.reshape(len(POOL_WINDOWS), grp, grp), w_up8=up[0],
                wd4=down.reshape(N_DEV // 2, -1, d))


def _layer_fwd(x, w, alpha, tag, rest_shards=None, next_in_shard=None):
    d = x.shape[1]
    g1 = _gather_phase1(rest_shards) if rest_shards is not None else None
    z, h = _mod_matmul(x, w["mod1"], w["w_in8"], w["b_in8"], flat_out=True, name="in_proj" + tag,
                       phases=[g1] if g1 else ())
    g2 = _gather_phase2(g1.results) if g1 else None
    ua, ub, dd = _mix_fwd(z, w["conv_a"], w["lnv"], w["wm"], w["bias_full"], name="mix_fwd" + tag,
                          phases=[g2] if g2 else ())
    if g2:
        w.update(_rest_views(g2.results, d))
    ya = _mm_rows(ua, w["w_a_out"], dn=NN, name="a_out" + tag, out_dtype=ACT_DTYPE)
    yb = _mm_rows(ub, w["w_b_out"], dn=NN, name="b_out" + tag, out_dtype=ACT_DTYPE)
    ycp = _pool_proj(dd, w["w_pool"], dn=NN, name="pool_proj" + tag, out_dtype=ACT_DTYPE)
    merged = _merge(z, ya, yb, ycp, w["pool_scale"], name="merge" + tag)
    o = _mm_rows(merged, w["w_o"], dn=NN, name="o_proj" + tag)
    x1 = _resid_ln(x, o, w["ln1"], alpha, name="ln1" + tag)
    n1 = _gather_phase1([next_in_shard]) if next_in_shard is not None else None
    up8, h2 = _mod_matmul(x1, w["mod2"], w["w_up8"], w["b_up8"], flat_out=False, name="up_proj" + tag,
                          phases=[n1] if n1 else ())
    up4 = up8.reshape((2, up8.shape[0] // 2) + up8.shape[1:])
    n2 = _gather_phase2(n1.results) if n1 else None
    f4 = _ffn_fwd(up4, w["cw"], w["cb"], name="ffn_fwd" + tag, phases=[n2] if n2 else ())
    y2 = _down_proj(f4, w["wd4"], name="down_proj" + tag)
    x2 = _resid_ln(x1, y2, w["ln2"], alpha, name="ln2" + tag)
    saved = dict(x=x, z=z, h=h, ua=ua, ub=ub, dd=dd, ya=ya, yb=yb, ycp=ycp, merged=merged, o=o, x1=x1,
                 up4=up4, h2=h2, f4=f4, y2=y2)
    return x2, saved, (n2.results[0][0] if n2 else None)


def _layer_bwd(dpart, dh_above, xmod_above, m_above, w, s, alpha, tag, l=0, above=None, rs=None, upper_reds=()):
    first, rest = ("w_in",), REST
    ph = lambda p: [p] if p is not None else ()
    r1a = rs.pair(first, above) if above else None
    dy2, dx1p, red2 = _resid_ln_bwd(dpart, dh_above, xmod_above, m_above, s["x1"], s["y2"], w["ln2"], alpha,
                                    name="ln2_bwd" + tag, phases=ph(r1a))
    r1b = rs.pair(rest, above) if above else None
    df4 = _down_bwd(dy2, w["wd4"], name="down_bwd" + tag, phases=ph(r1b))
    if above:
        sb_a, sb_b = rs.sums(first, above, r1a), rs.sums(rest, above, r1b)
    gw_down4 = _tn_shards_lhs(s["f4"], dy2, name="gw_down" + tag)
    if above:
        bufs = dict(zip(first + rest, sb_a + sb_b))
        heavy = ("w_in", "w_up")
        light = tuple(n for n in rest if n not in heavy)
    r3a = rs.chip(heavy, [bufs[n] for n in heavy], l + 1) if above else None
    dup4, redf = _ffn_bwd(s["up4"], df4, w["cw"], w["cb"], name="ffn_bwd" + tag, phases=ph(r3a))
    dup8 = dup4.reshape((dup4.shape[0] * dup4.shape[1],) + dup4.shape[2:])
    r3b = None
    if above:
        rs.done(heavy, r3a)
        r3b = rs.chip(light, [bufs[n] for n in light], l + 1)
    gw_up8 = _tn_shards_rhs(s["h2"], dup8, name="gw_up" + tag, phases=ph(r3b))
    if above:
        rs.done(light, r3b)
    own = rs is not None and l == 0
    big = dict(w_up=gw_up8, w_down=gw_down4)
    early = ("w_down", "w_up")
    o1 = rs.pair(early, big) if own else None
    dh2 = _nt_shards(dup8, w["w_up8"], name="up_bwd" + tag, phases=ph(o1))
    if own:
        sb_o = rs.sums(early, big, o1)
    do, dxp, red1 = _resid_ln_bwd(dx1p, dh2, s["x1"], w["mod2"][0:1], s["x"], s["o"], w["ln1"], alpha,
                                  name="ln1_bwd" + tag)
    dm = _mm_rows(do, w["w_o"], dn=NT, name="o_bwd" + tag, out_dtype=ACT_DTYPE)
    big["w_o"] = _mm_tn(s["merged"], do, name="gw_o" + tag)
    dya, dyb, dyc, dz, redg = _gate_bwd(dm, s["z"], s["ya"], s["yb"], s["ycp"], w["pool_scale"], name="gate_bwd" + tag)
    dua = _mm_rows(dya, w["w_a_out"], dn=NT, name="a_out_bwd" + tag, out_dtype=ACT_DTYPE)
    dub = _mm_rows(dyb, w["w_b_out"], dn=NT, name="b_out_bwd" + tag, out_dtype=ACT_DTYPE)
    ddd = _pool_proj(dyc, w["w_pool"], dn=NT, name="pool_bwd" + tag, out_dtype=ACT_DTYPE)
    big["w_a_out"] = _mm_tn(s["ua"], dya, name="gw_a_out" + tag)
    big["w_b_out"] = _mm_tn(s["ub"], dyb, name="gw_b_out" + tag)
    big["w_pool"] = _tn_pool(s["dd"], dyc, w["w_pool"].shape[0], name="gw_pool" + tag)
    o3 = rs.chip(early, sb_o, l) if own else None
    dz, redm, dws, dbs = _mix_bwd(dz, dua, dub, ddd, s["z"], w["conv_a"], w["lnv"], w["wm"], w["wmt"],
                                  w["bias_full"], w["mask"], name="mix_bwd" + tag, phases=ph(o3))
    reds = dict(red2=red2, redf=redf, red1=red1, redg=redg, redm=redm, dws=dws, dbs=dbs)
    mid = ("w_o", "w_a_out", "w_b_out", "w_pool")
    o1b = sg1 = None
    if own:
        rs.done(early, o3)
        o1b = rs.pair(mid, big)
        sg1 = _gather_phase1([_small_payload([reds] + list(upper_reds))])
    big["w_in"] = _tn_cols_rhs(s["h"], dz, w["w_in8"].shape[0], name="gw_in" + tag,
                               phases=[o1b, sg1] if own else ())
    tail_phases, pending = (), None
    if own:
        sb_m = rs.sums(mid, big, o1b)
        o1c, o3b, sg2 = rs.pair(first, big), rs.chip(mid, sb_m, l), _gather_phase2(sg1.results)
        tail_phases = [o1c, o3b, sg2]
    dh = _nt_cols(dz, w["w_in8"], name="in_bwd" + tag, phases=tail_phases)
    if own:
        rs.done(mid, o3b)
        rs.small_gathered = sg2.results[0]
        pending = (first, rs.chip(first, rs.sums(first, big, o1c), l))
    return dxp, dh, big, reds, pending


def _local_step(x, tgt, ws, alpha, shards=None, rs=None):
    depth = len(ws)
    saved = []
    y = x
    for l in range(depth):
        rest = [shards[l][n] for n in REST] if shards else None
        nxt = shards[l + 1]["w_in"] if shards and l + 1 < depth else None
        y, s, next_in = _layer_fwd(y, ws[l], alpha, "_l%d" % l, rest, nxt)
        if next_in is not None:
            ws[l + 1]["w_in8"] = next_in
        saved.append(s)
    dpart, loss_blk = _loss_grad(y, tgt, name="loss_grad")
    dh = xmod = mvec = above = pending = None
    bigs, reds = [None] * depth, [None] * depth
    for l in reversed(range(depth)):
        dpart, dh, bigs[l], reds[l], pending = _layer_bwd(dpart, dh, xmod, mvec, ws[l], saved[l], alpha, "_l%d" % l,
                                                          l, above if rs else None, rs, reds[l + 1:])
        xmod, mvec, above = saved[l]["x"], ws[l]["mod1"][0:1], bigs[l]
    grad_x, red0 = _resid_ln_bwd(dpart, dh, xmod, mvec, None, None, None, alpha, name="in_bwd_tail",
                                 phases=[pending[1]] if pending else ())
    if pending:
        rs.done(*pending)
    d_ada = []
    for l in range(depth):
        below = red0 if l == 0 else reds[l - 1]["red2"]
        r1, r2 = reds[l]["red1"], reds[l]["red2"]
        d_ada.append(jnp.stack([below[1], below[0], r1[4], r1[1], r1[0], r2[4]]))
    return loss_blk, grad_x, bigs, reds, jnp.stack(d_ada)


def _small_grads(r):
    redm, redg, redf = r["redm"], r["redg"], r["redf"]
    ng = r["dws"].shape[0]
    return dict(
        b_in=jnp.concatenate([redm[0:6], redg[1:4]], axis=0).reshape(-1),
        conv_a=redm[6:9], ln_v_g=redm[9], ln_v_b=redm[10],
        w_spatial=r["dws"], b_spatial=r["dbs"][:, :ng].T,
        pool_scale=redg[0], ln1_g=r["red1"][2], ln1_b=r["red1"][3],
        b_up=jnp.concatenate([redf[:, 4, :].reshape(-1), redf[:, 5, :].reshape(-1)]),
        conv_ffn=jnp.transpose(redf[:, 0:3, :], (1, 0, 2)).reshape(3, -1), conv_ffn_b=redf[:, 3, :].reshape(-1),
        ln2_g=r["red2"][2], ln2_b=r["red2"][3])


def _small_payload(reds):
    smalls = [_small_grads(r) for r in reds]
    order = SMALL_REPLICATED + SMALL_SHARDED
    flat = jnp.concatenate([smalls[l][n].reshape(-1) for n in order for l in range(len(reds))])
    return _as_rows(flat)[None]


def _layer_weights(l, ada, conv_a, conv_ffn, p):
    sh1, sc1, gt1, sh2, sc2, gt2 = (ada[l, k][None, :] for k in range(6))
    nb = N_DEV
    fs = p["b_up"].shape[1] // nb
    nj = nb // 2
    pos = jnp.arange(GMLP_BLOCK)
    allowed = (pos[None, :] // CHUNK) <= (pos[:, None] // CHUNK)
    wmask = jnp.where(allowed[None], p["w_spatial"][l], 0.0)
    return dict(
        mod1=jnp.concatenate([1.0 + sc1, sh1]), mod2=jnp.concatenate([1.0 + sc2, sh2]),
        ln1=jnp.concatenate([gt1, p["ln1_g"][l][None], p["ln1_b"][l][None]]),
        ln2=jnp.concatenate([gt2, p["ln2_g"][l][None], p["ln2_b"][l][None]]),
        b_in8=p["b_in"][l].reshape(N_DEV, 1, -1), b_up8=p["b_up"][l].reshape(nb, 1, fs),
        conv_a=conv_a[l], lnv=jnp.stack([p["ln_v_g"][l], p["ln_v_b"][l]]),
        wm=wmask.astype(BF), wmt=jnp.transpose(wmask, (0, 2, 1)).astype(BF),
        bias_full=jnp.repeat(p["b_spatial"][l].T, GMLP_BLOCK, axis=1), mask=allowed.astype(F32),
        pool_scale=p["pool_scale"][l][None],
        cw=jnp.transpose(conv_ffn[l].reshape(3, nj, fs), (1, 0, 2)), cb=p["conv_ffn_b"][l].reshape(nj, 1, fs))


ANY = pl.BlockSpec(memory_space=pl.ANY)


def _place():
    x, y, c = lax.axis_index("x"), lax.axis_index("y"), lax.axis_index("c")
    chips = [(1 - x, y), (x, 1 - y), (1 - x, 1 - y)]
    return x, y, c, chips


def _allgather_vmem(xs, *, name):
    r, cdim = xs.shape

    def body(x_ref, out_ref, send_sems, recv_sems, local_sem):
        x, y, c, chips = _place()
        me, sibling = (x, y, c), (x, y, 1 - c)

        def rows(px, py, pc):
            return out_ref.at[pl.ds((4 * px + 2 * py + pc) * r, r), :]

        def copy(k, block, to, src=None):
            return pltpu.make_async_remote_copy(
                src_ref=rows(*block) if src is None else src, dst_ref=rows(*block),
                send_sem=send_sems.at[k], recv_sem=recv_sems.at[k], device_id=to, device_id_type=MESH)

        mine = pltpu.make_async_copy(x_ref, rows(*me), local_sem)
        mine.start()
        first = [copy(0, me, sibling, src=x_ref)]
        first += [copy(1 + j, me, (*chip, c), src=x_ref) for j, chip in enumerate(chips)]
        for cp in first:
            cp.start()
        passed = [copy(4 + j, (*chip, c), sibling) for j, chip in enumerate(chips)]
        for j, chip in enumerate(chips):
            copy(1 + j, (*chip, c), me).wait_recv()
            passed[j].start()
        copy(0, sibling, me).wait_recv()
        for j, chip in enumerate(chips):
            copy(4 + j, (*chip, 1 - c), me).wait_recv()
        for cp in first + passed:
            cp.wait_send()
        mine.wait()

    return pl.pallas_call(
        body, name=name, out_shape=jax.ShapeDtypeStruct((N_DEV * r, cdim), xs.dtype),
        in_specs=[pl.BlockSpec(memory_space=pltpu.VMEM)], out_specs=pl.BlockSpec(memory_space=pltpu.VMEM),
        scratch_shapes=[pltpu.SemaphoreType.DMA((7,)), pltpu.SemaphoreType.DMA((7,)), pltpu.SemaphoreType.DMA],
        compiler_params=_params(),
    )(xs)


def _gather_weights(shards, *, name):
    n = len(shards)

    def body(*refs):
        ins, outs = refs[:n], refs[n:2 * n]
        send_sems, recv_sems, local_sems = refs[2 * n:]
        x, y, c, chips = _place()
        me, sibling = (x, y, c), (x, y, 1 - c)

        def slot(a, px, py, pc):
            return outs[a].at[:, 4 * px + 2 * py + pc]

        def copy(a, k, block, to, src=None):
            return pltpu.make_async_remote_copy(
                src_ref=slot(a, *block) if src is None else src, dst_ref=slot(a, *block),
                send_sem=send_sems.at[7 * a + k], recv_sem=recv_sems.at[7 * a + k], device_id=to,
                device_id_type=MESH)

        mine = [pltpu.make_async_copy(ins[a], slot(a, *me), local_sems.at[a]) for a in range(n)]
        for cp in mine:
            cp.start()
        first = []
        for j, chip in enumerate(chips):
            first += [copy(a, 1 + j, me, (*chip, c), src=ins[a]) for a in range(n)]
        first += [copy(a, 0, me, sibling, src=ins[a]) for a in range(n)]
        for cp in first:
            cp.start()
        passed = []
        for j, chip in enumerate(chips):
            for a in range(n):
                copy(a, 1 + j, (*chip, c), me).wait_recv()
                fwd = copy(a, 4 + j, (*chip, c), sibling)
                fwd.start()
                passed.append(fwd)
        for a in range(n):
            copy(a, 0, sibling, me).wait_recv()
        for j, chip in enumerate(chips):
            for a in range(n):
                copy(a, 4 + j, (*chip, 1 - c), me).wait_recv()
        for cp in first + passed:
            cp.wait_send()
        for cp in mine:
            cp.wait()

    out_shape = [jax.ShapeDtypeStruct((s.shape[0], N_DEV) + s.shape[1:], s.dtype) for s in shards]
    return pl.pallas_call(
        body, name=name, out_shape=out_shape, in_specs=[ANY] * n, out_specs=[ANY] * n,
        scratch_shapes=[pltpu.SemaphoreType.DMA((7 * n,)), pltpu.SemaphoreType.DMA((7 * n,)),
                        pltpu.SemaphoreType.DMA((n,))],
        compiler_params=_params(),
    )(*shards)


def _pick_tile(r, cap):
    best = None
    for t in range(8, min(r, cap) + 1, 8):
        if r % t == 0:
            best = t
    return best if best is not None else r


def _pair_sum(g, r1, cidx, *, name):
    p, _, r, cdim = g.shape
    tr = _pick_tile(r, 256)

    def body(c_ref, g_ref, r_ref, o_ref):
        del c_ref
        o_ref[...] = (g_ref[...].astype(F32) + r_ref[...].astype(F32)).astype(BF)

    grid_spec = pltpu.PrefetchScalarGridSpec(
        num_scalar_prefetch=1, grid=(4, r // tr),
        in_specs=[pl.BlockSpec((p, None, tr, cdim), lambda q, i, c: (0, 2 * q + c[0], i, 0)),
                  pl.BlockSpec((None, p, tr, cdim), lambda q, i, c: (q, 0, i, 0))],
        out_specs=pl.BlockSpec((None, p, tr, cdim), lambda q, i, c: (q, 0, i, 0)))
    return pl.pallas_call(
        body, name=name, grid_spec=grid_spec, out_shape=jax.ShapeDtypeStruct((4, p, r, cdim), BF),
        compiler_params=_params(("arbitrary", "arbitrary")),
    )(cidx, g, r1)


def _ada_fwd(c_all, w_ada, *, name):
    depth, d, ns = w_ada.shape
    nb = c_all.shape[0]

    def body(c_ref, w_ref, o_ref):
        cv = c_ref[...]
        act = cv * jax.nn.sigmoid(cv)
        o_ref[...] = jnp.dot(act, w_ref[...], preferred_element_type=F32, precision=lax.Precision.HIGHEST)

    return pl.pallas_call(
        body, name=name, grid=(depth,),
        in_specs=[pl.BlockSpec((nb, d), lambda l: (0, 0)), pl.BlockSpec((None, d, ns), lambda l: (l, 0, 0))],
        out_specs=pl.BlockSpec((None, nb, ns), lambda l: (l, 0, 0)),
        out_shape=jax.ShapeDtypeStruct((depth, nb, ns), F32), compiler_params=_params(("parallel",)),
    )(c_all, w_ada)


def _ada_bwd(ct, dmine, dall, *, name):
    depth, nb, ns = dmine.shape
    d = ct.shape[0]

    def body(ct_ref, dm_ref, da_ref, gw_ref, gb_ref):
        cv = ct_ref[...]
        act = cv * jax.nn.sigmoid(cv)
        gw_ref[...] = jnp.dot(act, dm_ref[...], preferred_element_type=F32, precision=lax.Precision.HIGHEST)
        s = da_ref[0]
        for b in range(1, nb):
            s = s + da_ref[b]
        gb_ref[...] = s

    return pl.pallas_call(
        body, name=name, grid=(depth,),
        in_specs=[pl.BlockSpec((d, nb), lambda l: (0, 0)), pl.BlockSpec((None, nb, ns), lambda l: (l, 0, 0)),
                  pl.BlockSpec(dall.shape, lambda l: (0, 0, 0))],
        out_specs=[pl.BlockSpec((None, d, ns), lambda l: (l, 0, 0)), pl.BlockSpec(dall.shape[1:], lambda l: (0, 0))],
        out_shape=[jax.ShapeDtypeStruct((depth, d, ns), F32), jax.ShapeDtypeStruct(dall.shape[1:], F32)],
        compiler_params=_params(("arbitrary",)),
    )(ct, dmine, dall)


def _sum_parts(parts, *, name):
    p, r, cdim = parts.shape
    tr = _pick_tile(r, 512)

    def body(p_ref, o_ref):
        s = p_ref[0]
        for k in range(1, p):
            s = s + p_ref[k]
        o_ref[...] = s

    return pl.pallas_call(
        body, name=name, grid=(r // tr,),
        in_specs=[pl.BlockSpec((p, tr, cdim), lambda i: (0, i, 0))], out_specs=pl.BlockSpec((tr, cdim), lambda i: (i, 0)),
        out_shape=jax.ShapeDtypeStruct((r, cdim), F32), compiler_params=_params(("parallel",)),
    )(parts)


def _adamw(parts, w, m, v, *, name):
    p, depth, r, cdim = parts.shape
    tr = _pick_tile(r, 256)

    def body(p_ref, w_ref, m_ref, v_ref, g_out, d_out, m_out, v_out):
        g = p_ref[0].astype(F32)
        for k in range(1, p):
            g = g + p_ref[k].astype(F32)
        m2 = ADAM_B1 * m_ref[...] + (1.0 - ADAM_B1) * g
        v2 = ADAM_B2 * v_ref[...] + (1.0 - ADAM_B2) * (g * g)
        m_hat = m2 / (1.0 - ADAM_B1 ** ADAM_STEP)
        v_hat = v2 / (1.0 - ADAM_B2 ** ADAM_STEP)
        g_out[...] = g
        d_out[...] = -ADAM_LR * (m_hat / (jnp.sqrt(v_hat) + ADAM_EPS) + ADAM_WD * w_ref[...])
        m_out[...] = m2
        v_out[...] = v2

    blk = pl.BlockSpec((None, tr, cdim), lambda l, i: (l, i, 0))
    out = jax.ShapeDtypeStruct((depth, r, cdim), F32)
    return pl.pallas_call(
        body, name=name, grid=(depth, r // tr),
        in_specs=[pl.BlockSpec((p, None, tr, cdim), lambda l, i: (0, l, i, 0)), blk, blk, blk],
        out_specs=[blk, blk, blk, blk], out_shape=[out, out, out, out],
        compiler_params=_params(("parallel", "parallel")),
    )(parts, w, m, v)


BIG = ("w_in", "w_a_out", "w_b_out", "w_pool", "w_o", "w_up", "w_down")
SMALL_REPLICATED = ("b_in", "ln_v_g", "ln_v_b", "w_spatial", "b_spatial", "pool_scale", "ln1_g", "ln1_b", "b_up",
                    "conv_ffn_b", "ln2_g", "ln2_b")
SMALL_SHARDED = ("conv_a", "conv_ffn")
WEIGHTS = ("w_ada", "b_ada", "w_in", "b_in", "conv_a", "w_a_out", "ln_v_g", "ln_v_b", "w_spatial", "b_spatial",
           "w_b_out", "w_pool", "pool_scale", "w_o", "ln1_g", "ln1_b", "w_up", "b_up", "conv_ffn", "conv_ffn_b",
           "w_down", "ln2_g", "ln2_b")
LANES = 128


def _as_rows(flat, mult=8):
    n = flat.shape[0]
    pad = (-n) % (LANES * mult)
    if pad:
        flat = jnp.concatenate([flat, jnp.zeros((pad,), flat.dtype)])
    return flat.reshape(-1, LANES)


def _shard3(a):
    return a.reshape((-1,) + a.shape[-2:])


def kernel(x, c, w_ada, b_ada, w_in, b_in, conv_a, w_a_out, ln_v_g, ln_v_b, w_spatial, b_spatial, w_b_out, w_pool, pool_scale, w_o, ln1_g, ln1_b, w_up, b_up, conv_ffn, conv_ffn_b, w_down, ln2_g, ln2_b, loss_target, m_w_ada, m_b_ada, m_w_in, m_b_in, m_conv_a, m_w_a_out, m_ln_v_g, m_ln_v_b, m_w_spatial, m_b_spatial, m_w_b_out, m_w_pool, m_pool_scale, m_w_o, m_ln1_g, m_ln1_b, m_w_up, m_b_up, m_conv_ffn, m_conv_ffn_b, m_w_down, m_ln2_g, m_ln2_b, v_w_ada, v_b_ada, v_w_in, v_b_in, v_conv_a, v_w_a_out, v_ln_v_g, v_ln_v_b, v_w_spatial, v_b_spatial, v_w_b_out, v_w_pool, v_pool_scale, v_w_o, v_ln1_g, v_ln1_b, v_w_up, v_b_up, v_conv_ffn, v_conv_ffn_b, v_w_down, v_ln2_g, v_ln2_b):
    p = dict(locals())
    depth, d = w_in.shape[0], w_in.shape[1]
    alpha = (2 * depth) ** 0.25
    me = 4 * lax.axis_index("x") + 2 * lax.axis_index("y") + lax.axis_index("c")
    cidx = lax.axis_index("c").astype(jnp.int32).reshape(1)

    n_ca, n_cf = conv_a.size, conv_ffn.size
    packed = _as_rows(jnp.concatenate([c.reshape(-1), conv_a.reshape(-1), conv_ffn.reshape(-1)]))
    got = _allgather_vmem(packed, name="gather_cond").reshape(N_DEV, -1)
    c_all = got[:, :d]
    ct = c_all.T
    conv_a_full = jnp.transpose(got[:, d:d + n_ca].reshape((N_DEV,) + conv_a.shape), (1, 2, 0, 3)).reshape(depth, 3, -1)
    conv_ffn_full = jnp.transpose(got[:, d + n_ca:d + n_ca + n_cf].reshape((N_DEV,) + conv_ffn.shape),
                                  (1, 2, 0, 3)).reshape(depth, 3, -1)

    ns_ada = w_ada.shape[2]
    ada_part = _ada_fwd(c_all, w_ada, name="ada_fwd")
    ada_all = _allgather_vmem(_as_rows(ada_part.reshape(-1)), name="gather_ada")
    ada_all = ada_all.reshape(N_DEV, depth, N_DEV, ns_ada)
    ada_mine = lax.dynamic_index_in_dim(ada_all, me, axis=2, keepdims=False)
    ada = jnp.transpose(ada_mine, (1, 0, 2)).reshape(depth, -1) + b_ada
    ada = ada.reshape(depth, 6, d)

    shards = [{n: _shard3(p[n][l].astype(BF)) for n in BIG} for l in range(depth)]
    ws = [_layer_weights(l, ada, conv_a_full, conv_ffn_full, p) for l in range(depth)]
    ws[0]["w_in8"] = _gather_weights([shards[0]["w_in"]], name="gather_w_in0")[0][0]

    rs = _ReduceScatter(depth, cidx)
    loss_blk, grad_x, bigs, reds, d_ada = _local_step(x[0], loss_target[0], ws, alpha, shards, rs)
    loss = lax.psum(loss_blk[0, 0], ("x", "y", "c"))

    dada_all = _allgather_vmem(_as_rows(d_ada.reshape(-1)), name="gather_dada")
    dada_all = dada_all.reshape(N_DEV, -1, LANES)
    dflat = dada_all.reshape(N_DEV, depth, 6 * d)
    dmine = lax.dynamic_slice_in_dim(dflat, me * ns_ada, ns_ada, axis=2)
    gw_ada, gb_rows = _ada_bwd(ct, jnp.transpose(dmine, (1, 0, 2)), dada_all, name="ada_bwd")
    gb_ada = gb_rows.reshape(-1)[:depth * 6 * d].reshape(depth, 6 * d)

    out = {}
    for n in BIG:
        parts = rs.acc[n]
        shard_shape = p[n].shape
        w3 = p[n].reshape(depth, -1, shard_shape[-1])
        parts4 = parts.reshape((4,) + w3.shape)
        res = _adamw(parts4, w3, p["m_" + n].reshape(w3.shape), p["v_" + n].reshape(w3.shape), name="adamw_" + n)
        out[n] = [r.reshape(shard_shape) for r in res]
    out["w_ada"] = _adamw(gw_ada[None], w_ada, m_w_ada, v_w_ada, name="adamw_w_ada")

    order = SMALL_REPLICATED + SMALL_SHARDED
    n_rep = sum(p[n].size for n in SMALL_REPLICATED)
    n_pay = n_rep + N_DEV * (conv_a.size + conv_ffn.size)
    gsum = _sum_parts(rs.small_gathered.reshape(N_DEV, -1, LANES), name="sum_small").reshape(-1)[:n_pay]
    ga_full = gsum[n_rep:n_rep + depth * 3 * d].reshape(depth, 3, d)
    gf_full = gsum[n_rep + depth * 3 * d:].reshape(depth, 3, -1)
    ca_w, cf_w = conv_a.shape[2], conv_ffn.shape[2]
    g_ca = lax.dynamic_slice_in_dim(ga_full, me * ca_w, ca_w, axis=2)
    g_cf = lax.dynamic_slice_in_dim(gf_full, me * cf_w, cf_w, axis=2)
    names = ("b_ada",) + order
    gflat = _as_rows(jnp.concatenate([gb_ada.reshape(-1), gsum[:n_rep], g_ca.reshape(-1), g_cf.reshape(-1)]))
    pack = lambda pre: _as_rows(jnp.concatenate([p[pre + n].reshape(-1) for n in names]))
    res = _adamw(gflat[None, None], pack("")[None], pack("m_")[None], pack("v_")[None], name="adamw_small")
    off = 0
    for n in names:
        size = p[n].size
        out[n] = [r.reshape(-1)[off:off + size].reshape(p[n].shape) for r in res]
        off += size

    return (loss, grad_x[None]) + tuple(out[n][k] for k in range(4) for n in WEIGHTS)
```

```python
import functools

import jax
import jax.numpy as jnp
from jax import lax
from jax.experimental import pallas as pl
from jax.experimental.pallas import tpu as pltpu

F32 = jnp.float32
BF = jnp.bfloat16
MESH = pl.DeviceIdType.MESH

LN_EPS = 1e-5
POOL_WINDOWS = (2, 4, 8, 16)
GMLP_BLOCK = 128
CHUNK = 64
HALO = 16
ADAM_LR, ADAM_B1, ADAM_B2, ADAM_EPS, ADAM_WD, ADAM_STEP = 0.001, 0.9, 0.999, 1e-08, 0.01, 10
N_DEV = 8
VMEM_LIMIT = 56 * 1024 * 1024

GRAD_DTYPE = BF
ACT_DTYPE = BF

NN = ((1,), (0,))
NT = ((1,), (1,))
TN = ((0,), (0,))


def _params(sem=None, vmem=VMEM_LIMIT, **kw):
    if sem is not None:
        kw["dimension_semantics"] = sem
    return pltpu.CompilerParams(vmem_limit_bytes=vmem, **kw)


class _Phase:
    def __init__(self, ins, out_shapes, aliases, n_remote, n_local, build):
        self.ins, self.out_shapes, self.aliases = list(ins), list(out_shapes), dict(aliases)
        self.n_remote, self.n_local, self.build = n_remote, n_local, build
        self.results = None


def _pcall(body, args, *, name, grid, in_specs, out_specs, out_shape, scratch_shapes=(), sem=None, aliases=None,
           phases=()):
    aliases = dict(aliases or {})
    if not phases:
        return pl.pallas_call(
            body, name=name, grid=grid, in_specs=list(in_specs), out_specs=out_specs, out_shape=out_shape,
            scratch_shapes=list(scratch_shapes), input_output_aliases=aliases, compiler_params=_params(sem),
        )(*args)
    single = not isinstance(out_shape, (list, tuple))
    o_specs = [out_specs] if single else list(out_specs)
    o_shapes = [out_shape] if single else list(out_shape)
    n_in, n_out, n_scr = len(args), len(o_shapes), len(scratch_shapes)
    ex_args, ex_out, sems = [], [], []
    for ph in phases:
        for src, dst in ph.aliases.items():
            aliases[n_in + len(ex_args) + src] = n_out + len(ex_out) + dst
        ex_args += ph.ins
        ex_out += ph.out_shapes
        sems += [pltpu.SemaphoreType.DMA((max(ph.n_remote, 1),)), pltpu.SemaphoreType.DMA((max(ph.n_remote, 1),)),
                 pltpu.SemaphoreType.DMA((max(ph.n_local, 1),))]

    def wrapped(*refs):
        pos = n_in
        ph_in = []
        for ph in phases:
            ph_in.append(refs[pos:pos + len(ph.ins)])
            pos += len(ph.ins)
        base_out = refs[pos:pos + n_out]
        pos += n_out
        ph_out = []
        for ph in phases:
            ph_out.append(refs[pos:pos + len(ph.out_shapes)])
            pos += len(ph.out_shapes)
        base_scr = refs[pos:pos + n_scr]
        ph_sems = refs[pos + n_scr:]
        first = last = None
        for ax, n in enumerate(grid):
            pid = pl.program_id(ax)
            first = (pid == 0) if first is None else first & (pid == 0)
            last = (pid == n - 1) if last is None else last & (pid == n - 1)

        def ops(k):
            return phases[k].build(ph_in[k], ph_out[k], *ph_sems[3 * k:3 * k + 3])

        @pl.when(first)
        def _():
            for k in range(len(phases)):
                for cp in ops(k)["start"]:
                    cp.start()

        body(*refs[:n_in], *base_out, *base_scr)

        @pl.when(last)
        def _():
            for k in range(len(phases)):
                o = ops(k)
                for cp in o["recv"]:
                    cp.wait_recv()
                for cp in o["send"]:
                    cp.wait_send()
                for cp in o["local"]:
                    cp.wait()

    hbm = pl.BlockSpec(memory_space=pl.ANY)
    res = pl.pallas_call(
        wrapped, name=name, grid=grid, in_specs=list(in_specs) + [hbm] * len(ex_args),
        out_specs=o_specs + [hbm] * len(ex_out), out_shape=o_shapes + ex_out,
        scratch_shapes=list(scratch_shapes) + sems, input_output_aliases=aliases,
        compiler_params=_params(("arbitrary",) * len(grid)),
    )(*args, *ex_args)
    pos = n_out
    for ph in phases:
        ph.results = list(res[pos:pos + len(ph.out_shapes)])
        pos += len(ph.out_shapes)
    return res[0] if single else list(res[:n_out])


def _gelu_parts(x):
    k = 0.7978845608028654
    x2 = x * x
    t = jnp.tanh(k * (x + 0.044715 * (x2 * x)))
    cdf = 0.5 * (1.0 + t)
    dcdf = 0.5 * (1.0 - t * t) * (k * (1.0 + 3.0 * 0.044715 * x2))
    return x * cdf, cdf + x * dcdf


def _gelu(x):
    t = jnp.tanh(0.7978845608028654 * (x + 0.044715 * (x * x * x)))
    return x * (0.5 * (1.0 + t))


def _rowsum(v):
    return jnp.sum(v, axis=0, keepdims=True)


def _ln_stats(r):
    mu = jnp.mean(r, axis=-1, keepdims=True)
    xc = r - mu
    var = jnp.mean(xc * xc, axis=-1, keepdims=True)
    rstd = lax.rsqrt(var + LN_EPS)
    return xc * rstd, rstd


def _ln_bwd(dy, xhat, rstd, gain):
    dxh = dy * gain
    m1 = jnp.mean(dxh, axis=-1, keepdims=True)
    m2 = jnp.mean(dxh * xhat, axis=-1, keepdims=True)
    return rstd * (dxh - m1 - xhat * m2)


def _matmul(a, b, *, dn, grid, a_spec, b_spec, o_spec, out_shape, acc_shape, name, phases=()):
    nk = grid[2]
    direct = out_shape.dtype == F32

    def body(a_ref, b_ref, o_ref, *scratch):
        prod = lax.dot_general(a_ref[...], b_ref[...], (dn, ((), ())), preferred_element_type=F32)
        if nk == 1:
            o_ref[...] = prod.astype(o_ref.dtype)
            return
        acc = o_ref if direct else scratch[0]
        k = pl.program_id(2)

        @pl.when(k == 0)
        def _():
            acc[...] = prod

        @pl.when(k > 0)
        def _():
            acc[...] += prod

        if not direct:
            @pl.when(k == nk - 1)
            def _():
                o_ref[...] = acc[...].astype(o_ref.dtype)

    scratch = [] if (direct or nk == 1) else [pltpu.VMEM(acc_shape, F32)]
    return _pcall(body, (a, b), name=name, grid=grid, in_specs=[a_spec, b_spec], out_specs=o_spec,
                  out_shape=out_shape, scratch_shapes=scratch, sem=("parallel", "parallel", "arbitrary"),
                  phases=phases)


def _row_tile(m, want):
    t = min(m, want)
    assert m % t == 0
    return t


def _mm_rows(a, w, *, dn, name, out_dtype=F32, tm=2048):
    m, k = a.shape
    n = w.shape[1] if dn == NN else w.shape[0]
    tm = _row_tile(m, tm)
    return _matmul(
        a, w, dn=dn, grid=(m // tm, 1, 1), name=name,
        a_spec=pl.BlockSpec((tm, k), lambda i, j, kk: (i, 0)),
        b_spec=pl.BlockSpec(w.shape, lambda i, j, kk: (0, 0)),
        o_spec=pl.BlockSpec((tm, n), lambda i, j, kk: (i, 0)),
        out_shape=jax.ShapeDtypeStruct((m, n), out_dtype), acc_shape=(tm, n))


def _mm_tn(a, b, *, name, tk=512):
    m, ka = a.shape
    n = b.shape[1]
    tk = _row_tile(m, tk)
    return _matmul(
        a, b, dn=TN, grid=(1, 1, m // tk), name=name,
        a_spec=pl.BlockSpec((tk, ka), lambda i, j, kk: (kk, 0)),
        b_spec=pl.BlockSpec((tk, n), lambda i, j, kk: (kk, 0)),
        o_spec=pl.BlockSpec((ka, n), lambda i, j, kk: (0, 0)),
        out_shape=jax.ShapeDtypeStruct((ka, n), GRAD_DTYPE), acc_shape=(ka, n))


def _mod_matmul(x, mod, w8, bias8, *, flat_out, name, tm=2048, phases=()):
    m, k = x.shape
    nb, _, ns = w8.shape
    tm = _row_tile(m, tm)

    def body(x_ref, mod_ref, w_ref, b_ref, o_ref, h_ref, hs):
        @pl.when(pl.program_id(1) == 0)
        def _():
            h = (x_ref[...] * mod_ref[0:1, :] + mod_ref[1:2, :]).astype(BF)
            hs[...] = h
            h_ref[...] = h

        o_ref[...] = (jnp.dot(hs[...], w_ref[...], preferred_element_type=F32) + b_ref[...]).astype(o_ref.dtype)

    if flat_out:
        o_spec = pl.BlockSpec((tm, ns), lambda i, j: (i, j))
        o_shape = jax.ShapeDtypeStruct((m, nb * ns), ACT_DTYPE)
    else:
        o_spec = pl.BlockSpec((None, tm, ns), lambda i, j: (j, i, 0))
        o_shape = jax.ShapeDtypeStruct((nb, m, ns), ACT_DTYPE)
    return _pcall(
        body, (x, mod, w8, bias8), name=name, grid=(m // tm, nb),
        in_specs=[pl.BlockSpec((tm, k), lambda i, j: (i, 0)),
                  pl.BlockSpec((2, k), lambda i, j: (0, 0)),
                  pl.BlockSpec((None, k, ns), lambda i, j: (j, 0, 0)),
                  pl.BlockSpec((None, 1, ns), lambda i, j: (j, 0, 0))],
        out_specs=[o_spec, pl.BlockSpec((tm, k), lambda i, j: (i, 0))],
        out_shape=[o_shape, jax.ShapeDtypeStruct((m, k), BF)],
        scratch_shapes=[pltpu.VMEM((tm, k), BF)], sem=("parallel", "arbitrary"), phases=phases)


def _seg_spec(tm, d, s):
    return pl.BlockSpec((tm, d), lambda i, s=s: (i, s))


def _prev_halo_spec(tm, d, s):
    hb = tm // HALO
    return pl.BlockSpec((HALO, d), lambda i, s=s: (jnp.maximum(i * hb - 1, 0), s))


def _next_halo_spec(tm, d, s, m):
    hb = tm // HALO
    last = m // HALO - 1
    return pl.BlockSpec((HALO, d), lambda i, s=s: (jnp.minimum((i + 1) * hb, last), s))


def _spatial_mix(wm_ref, src, dst, bias_ref, tm, d):
    for n in range(tm // GMLP_BLOCK):
        for g in range(d // GMLP_BLOCK):
            rs = slice(n * GMLP_BLOCK, (n + 1) * GMLP_BLOCK)
            cs = slice(g * GMLP_BLOCK, (g + 1) * GMLP_BLOCK)
            v = jnp.dot(wm_ref[g], src[rs, cs], preferred_element_type=F32)
            if bias_ref is not None:
                v = v + bias_ref[:, cs]
            dst[rs, cs] = v


def _mix_fwd(z, conv_a, lnv, wm, bias_full, *, name, tm=256, phases=()):
    m, d9 = z.shape
    d = d9 // 9
    tm = _row_tile(m, tm)
    grp = d // len(POOL_WINDOWS)

    def body(zb, zc, zx, zu, zv, zp, zc_h, zx_h, zp_h, ca_ref, lnv_ref, wm_ref, bias_ref,
             ua_ref, ub_ref, d_ref, ext, vn_s, mixed_s):
        i = pl.program_id(0)
        first = i == 0
        f32 = lambda r: r[...].astype(F32)
        pa = f32(zc) * f32(zx)
        ext[0:HALO, :] = jnp.where(first, 0.0, f32(zc_h) * f32(zx_h))
        ext[HALO:HALO + tm, :] = pa
        w = ca_ref[...]
        conv = w[0:1, :] * ext[pl.ds(HALO - 2, tm), :] + w[1:2, :] * ext[pl.ds(HALO - 1, tm), :] + w[2:3, :] * pa
        ua_ref[...] = (f32(zb) * conv).astype(BF)
        p = f32(zp)
        ext[0:HALO, :] = jnp.where(first, 0.0, f32(zp_h))
        ext[HALO:HALO + tm, :] = p
        t = (i * tm + lax.broadcasted_iota(jnp.int32, (tm, 1), 0) + 1).astype(F32)
        for k, win in enumerate(POOL_WINDOWS):
            cs = slice(k * grp, (k + 1) * grp)
            s = p[:, cs]
            for j in range(1, win):
                s = s + ext[pl.ds(HALO - j, tm), cs]
            d_ref[:, cs] = (s / jnp.minimum(t, float(win)) - p[:, cs]).astype(BF)
        gv = _gelu(f32(zv))
        vhat, _ = _ln_stats(gv)
        vn_s[...] = (vhat * lnv_ref[0:1, :] + lnv_ref[1:2, :]).astype(BF)
        _spatial_mix(wm_ref, vn_s, mixed_s, bias_ref, tm, d)
        ub_ref[...] = (_gelu(f32(zu)) * mixed_s[...]).astype(BF)

    full = lambda a: pl.BlockSpec(a.shape, lambda i: (0,) * a.ndim)
    out = jax.ShapeDtypeStruct((m, d), BF)
    o_spec = pl.BlockSpec((tm, d), lambda i: (i, 0))
    return _pcall(
        body, (z, z, z, z, z, z, z, z, z, conv_a, lnv, wm, bias_full), name=name, grid=(m // tm,),
        in_specs=[_seg_spec(tm, d, s) for s in range(6)] + [_prev_halo_spec(tm, d, s) for s in (1, 2, 5)]
        + [full(conv_a), full(lnv), full(wm), full(bias_full)],
        out_specs=[o_spec, o_spec, o_spec], out_shape=[out, out, out],
        scratch_shapes=[pltpu.VMEM((HALO + tm, d), F32), pltpu.VMEM((tm, d), BF), pltpu.VMEM((tm, d), F32)],
        sem=("arbitrary",), phases=phases)


def _pool_proj(dd, w_pool, *, dn, name, out_dtype=F32, tm=512):
    m, d = dd.shape
    ng, grp, _ = w_pool.shape
    tm = _row_tile(m, tm)
    return _matmul(
        dd, w_pool, dn=dn, grid=(m // tm, ng, 1), name=name,
        a_spec=pl.BlockSpec((tm, grp), lambda i, j, kk: (i, j)),
        b_spec=pl.BlockSpec((None, grp, grp), lambda i, j, kk: (j, 0, 0)),
        o_spec=pl.BlockSpec((tm, grp), lambda i, j, kk: (i, j)),
        out_shape=jax.ShapeDtypeStruct((m, d), out_dtype), acc_shape=(tm, grp))


def _merge(z, ya, yb, ycp, scale, *, name, tm=512):
    m, d = ya.shape
    tm = _row_tile(m, tm)

    def body(ga, gb, gc, ya_ref, yb_ref, yc_ref, sc_ref, o_ref):
        f32 = lambda r: r[...].astype(F32)
        o_ref[...] = (jax.nn.sigmoid(f32(ga)) * f32(ya_ref) + jax.nn.sigmoid(f32(gb)) * f32(yb_ref)
                      + jax.nn.sigmoid(f32(gc)) * (f32(yc_ref) * sc_ref[...])).astype(BF)

    row = pl.BlockSpec((tm, d), lambda i: (i, 0))
    return pl.pallas_call(
        body, name=name, grid=(m // tm,),
        in_specs=[_seg_spec(tm, d, 6), _seg_spec(tm, d, 7), _seg_spec(tm, d, 8), row, row, row,
                  pl.BlockSpec((1, d), lambda i: (0, 0))],
        out_specs=row, out_shape=jax.ShapeDtypeStruct((m, d), BF),
        compiler_params=_params(("parallel",)),
    )(z, z, z, ya, yb, ycp, scale)


def _resid_ln(xp, ys, vec, alpha, *, name, tm=512):
    m, d = xp.shape
    tm = _row_tile(m, tm)

    def body(xp_ref, ys_ref, v_ref, o_ref):
        xhat, _ = _ln_stats(alpha * xp_ref[...] + v_ref[0:1, :] * ys_ref[...])
        o_ref[...] = xhat * v_ref[1:2, :] + v_ref[2:3, :]

    row = pl.BlockSpec((tm, d), lambda i: (i, 0))
    return pl.pallas_call(
        body, name=name, grid=(m // tm,),
        in_specs=[row, row, pl.BlockSpec(vec.shape, lambda i: (0, 0))],
        out_specs=row, out_shape=jax.ShapeDtypeStruct((m, d), F32),
        compiler_params=_params(("parallel",)),
    )(xp, ys, vec)


def _ffn_fwd(up4, cw, cb, *, name, tm=512, phases=()):
    _, nj, m, fs = up4.shape
    tm = _row_tile(m, tm)
    hb = tm // HALO

    def body(up_ref, ah_ref, cw_ref, cb_ref, f_ref, ext):
        first = pl.program_id(1) == 0
        a = up_ref[0].astype(F32)
        ext[0:HALO, :] = jnp.where(first, 0.0, ah_ref[...].astype(F32))
        ext[HALO:HALO + tm, :] = a
        w = cw_ref[...]
        ca = (w[0:1, :] * ext[pl.ds(HALO - 2, tm), :] + w[1:2, :] * ext[pl.ds(HALO - 1, tm), :]
              + w[2:3, :] * a + cb_ref[...])
        f_ref[...] = (_gelu(ca) * up_ref[1].astype(F32)).astype(BF)

    return _pcall(
        body, (up4, up4, cw, cb), name=name, grid=(nj, m // tm),
        in_specs=[pl.BlockSpec((2, None, tm, fs), lambda j, i: (0, j, i, 0)),
                  pl.BlockSpec((None, None, HALO, fs), lambda j, i: (0, j, jnp.maximum(i * hb - 1, 0), 0)),
                  pl.BlockSpec((None, 3, fs), lambda j, i: (j, 0, 0)),
                  pl.BlockSpec((None, 1, fs), lambda j, i: (j, 0, 0))],
        out_specs=pl.BlockSpec((None, tm, fs), lambda j, i: (j, i, 0)),
        out_shape=jax.ShapeDtypeStruct((nj, m, fs), BF),
        scratch_shapes=[pltpu.VMEM((HALO + tm, fs), F32)], sem=("parallel", "arbitrary"), phases=phases)


def _down_proj(f4, wd4, *, name, tm=2048, phases=()):
    nj, m, fs = f4.shape
    d = wd4.shape[2]
    tm = _row_tile(m, tm)
    return _matmul(
        f4, wd4, dn=NN, grid=(m // tm, 1, nj), name=name,
        a_spec=pl.BlockSpec((None, tm, fs), lambda i, j, kk: (kk, i, 0)),
        b_spec=pl.BlockSpec((None, fs, d), lambda i, j, kk: (kk, 0, 0)),
        o_spec=pl.BlockSpec((tm, d), lambda i, j, kk: (i, 0)),
        out_shape=jax.ShapeDtypeStruct((m, d), F32), acc_shape=(tm, d), phases=phases)


def _loss_grad(y, tgt, *, name, tm=512):
    m, d = y.shape
    tm = _row_tile(m, tm)
    ni = m // tm

    def body(y_ref, t_ref, dy_ref, l_ref, acc):
        i = pl.program_id(0)
        e = y_ref[...] - t_ref[...]
        dy_ref[...] = e * (1.0 / d)
        part = jnp.sum((e * e).reshape(tm // 8, 8, d), axis=0)

        @pl.when(i == 0)
        def _():
            acc[...] = part

        @pl.when(i > 0)
        def _():
            acc[...] += part

        @pl.when(i == ni - 1)
        def _():
            l_ref[...] = jnp.full((8, 128), 0.5 / d, F32) * jnp.sum(acc[...])

    row = pl.BlockSpec((tm, d), lambda i: (i, 0))
    return pl.pallas_call(
        body, name=name, grid=(ni,), in_specs=[row, row],
        out_specs=[row, pl.BlockSpec((8, 128), lambda i: (0, 0))],
        out_shape=[jax.ShapeDtypeStruct((m, d), F32), jax.ShapeDtypeStruct((8, 128), F32)],
        scratch_shapes=[pltpu.VMEM((8, d), F32)],
        compiler_params=_params(("arbitrary",)),
    )(y, tgt)


def _resid_ln_bwd(dpart, dh, xmod, mvec, xp, ys, vec, alpha, *, name, tm=256, phases=()):
    m, d = dpart.shape
    tm = _row_tile(m, tm)
    has_dh = dh is not None
    has_ln = xp is not None

    def body(*refs):
        refs = list(refs)
        dpart_ref = refs.pop(0)
        if has_dh:
            dh_ref, xm_ref, mv_ref = refs.pop(0), refs.pop(0), refs.pop(0)
        if has_ln:
            xp_ref, ys_ref, v_ref = refs.pop(0), refs.pop(0), refs.pop(0)
            dys_ref, dxp_ref, red_ref = refs
        else:
            dx_ref, red_ref = refs
        i = pl.program_id(0)
        dtot = dpart_ref[...]
        rows = [jnp.zeros((1, d), F32)] * 5
        if has_dh:
            dhv = dh_ref[...]
            dtot = dtot + dhv * mv_ref[...]
            rows[0] = _rowsum(dhv * xm_ref[...])
            rows[1] = _rowsum(dhv)
        if has_ln:
            ys = ys_ref[...]
            gt = v_ref[0:1, :]
            xhat, rstd = _ln_stats(alpha * xp_ref[...] + gt * ys)
            rows[2] = _rowsum(dtot * xhat)
            rows[3] = _rowsum(dtot)
            dr = _ln_bwd(dtot, xhat, rstd, v_ref[1:2, :])
            rows[4] = _rowsum(dr * ys)
            dys_ref[...] = (dr * gt).astype(BF)
            dxp_ref[...] = alpha * dr
        else:
            dx_ref[...] = dtot
        red = jnp.concatenate(rows + [jnp.zeros((3, d), F32)], axis=0)

        @pl.when(i == 0)
        def _():
            red_ref[...] = red

        @pl.when(i > 0)
        def _():
            red_ref[...] += red

    row = pl.BlockSpec((tm, d), lambda i: (i, 0))
    vrow = lambda a: pl.BlockSpec(a.shape, lambda i: (0, 0))
    args, specs = [dpart], [row]
    if has_dh:
        args += [dh, xmod, mvec]
        specs += [row, row, vrow(mvec)]
    if has_ln:
        args += [xp, ys, vec]
        specs += [row, row, vrow(vec)]
        out_specs = [row, row, pl.BlockSpec((8, d), lambda i: (0, 0))]
        out_shape = [jax.ShapeDtypeStruct((m, d), BF), jax.ShapeDtypeStruct((m, d), F32),
                     jax.ShapeDtypeStruct((8, d), F32)]
    else:
        out_specs = [row, pl.BlockSpec((8, d), lambda i: (0, 0))]
        out_shape = [jax.ShapeDtypeStruct((m, d), F32), jax.ShapeDtypeStruct((8, d), F32)]
    return _pcall(body, args, name=name, grid=(m // tm,), in_specs=specs, out_specs=out_specs, out_shape=out_shape,
                  sem=("arbitrary",), phases=phases)


def _down_bwd(dy, wd4, *, name, tm=2048, phases=()):
    m, d = dy.shape
    nj, fs, _ = wd4.shape
    tm = _row_tile(m, tm)
    return _matmul(
        dy, wd4, dn=NT, grid=(m // tm, nj, 1), name=name,
        a_spec=pl.BlockSpec((tm, d), lambda i, j, kk: (i, 0)),
        b_spec=pl.BlockSpec((None, fs, d), lambda i, j, kk: (j, 0, 0)),
        o_spec=pl.BlockSpec((None, tm, fs), lambda i, j, kk: (j, i, 0)),
        out_shape=jax.ShapeDtypeStruct((nj, m, fs), ACT_DTYPE), acc_shape=(tm, fs), phases=phases)


def _tn_shards_lhs(f4, dy, *, name, tk=512):
    nj, m, fs = f4.shape
    d = dy.shape[1]
    tk = _row_tile(m, tk)
    return _matmul(
        f4, dy, dn=TN, grid=(nj, 1, m // tk), name=name,
        a_spec=pl.BlockSpec((None, tk, fs), lambda i, j, kk: (i, kk, 0)),
        b_spec=pl.BlockSpec((tk, d), lambda i, j, kk: (kk, 0)),
        o_spec=pl.BlockSpec((None, fs, d), lambda i, j, kk: (i, 0, 0)),
        out_shape=jax.ShapeDtypeStruct((nj, fs, d), GRAD_DTYPE), acc_shape=(fs, d))


def _tn_shards_rhs(h, d8, *, name, tk=512, phases=()):
    m, k = h.shape
    nb, _, ns = d8.shape
    tk = _row_tile(m, tk)
    return _matmul(
        h, d8, dn=TN, grid=(nb, 1, m // tk), name=name,
        a_spec=pl.BlockSpec((tk, k), lambda i, j, kk: (kk, 0)),
        b_spec=pl.BlockSpec((None, tk, ns), lambda i, j, kk: (i, kk, 0)),
        o_spec=pl.BlockSpec((None, k, ns), lambda i, j, kk: (i, 0, 0)),
        out_shape=jax.ShapeDtypeStruct((nb, k, ns), GRAD_DTYPE), acc_shape=(k, ns), phases=phases)


def _tn_cols_rhs(h, dz, nb, *, name, tk=512, phases=()):
    m, k = h.shape
    ns = dz.shape[1] // nb
    tk = _row_tile(m, tk)
    return _matmul(
        h, dz, dn=TN, grid=(nb, 1, m // tk), name=name,
        a_spec=pl.BlockSpec((tk, k), lambda i, j, kk: (kk, 0)),
        b_spec=pl.BlockSpec((tk, ns), lambda i, j, kk: (kk, i)),
        o_spec=pl.BlockSpec((None, k, ns), lambda i, j, kk: (i, 0, 0)),
        out_shape=jax.ShapeDtypeStruct((nb, k, ns), GRAD_DTYPE), acc_shape=(k, ns), phases=phases)


def _nt_shards(d8, w8, *, name, tm=2048, phases=()):
    nb, m, ns = d8.shape
    k = w8.shape[1]
    tm = _row_tile(m, tm)
    return _matmul(
        d8, w8, dn=NT, grid=(m // tm, 1, nb), name=name,
        a_spec=pl.BlockSpec((None, tm, ns), lambda i, j, kk: (kk, i, 0)),
        b_spec=pl.BlockSpec((None, k, ns), lambda i, j, kk: (kk, 0, 0)),
        o_spec=pl.BlockSpec((tm, k), lambda i, j, kk: (i, 0)),
        out_shape=jax.ShapeDtypeStruct((m, k), F32), acc_shape=(tm, k), phases=phases)


def _nt_cols(dz, w8, *, name, tm=2048, phases=()):
    m = dz.shape[0]
    nb, k, ns = w8.shape
    tm = _row_tile(m, tm)
    return _matmul(
        dz, w8, dn=NT, grid=(m // tm, 1, nb), name=name,
        a_spec=pl.BlockSpec((tm, ns), lambda i, j, kk: (i, kk)),
        b_spec=pl.BlockSpec((None, k, ns), lambda i, j, kk: (kk, 0, 0)),
        o_spec=pl.BlockSpec((tm, k), lambda i, j, kk: (i, 0)),
        out_shape=jax.ShapeDtypeStruct((m, k), F32), acc_shape=(tm, k), phases=phases)


def _tn_pool(dd, dyc, ng, *, name, tk=512):
    m, d = dd.shape
    grp = d // ng
    tk = _row_tile(m, tk)
    return _matmul(
        dd, dyc, dn=TN, grid=(ng, 1, m // tk), name=name,
        a_spec=pl.BlockSpec((tk, grp), lambda i, j, kk: (kk, i)),
        b_spec=pl.BlockSpec((tk, grp), lambda i, j, kk: (kk, i)),
        o_spec=pl.BlockSpec((None, grp, grp), lambda i, j, kk: (i, 0, 0)),
        out_shape=jax.ShapeDtypeStruct((ng, grp, grp), GRAD_DTYPE), acc_shape=(grp, grp))


def _ffn_bwd(up4, df4, cw, cb, *, name, tm=256, phases=()):
    _, nj, m, fs = up4.shape
    tm = _row_tile(m, tm)
    hb = tm // HALO
    ni = m // tm
    last_hb = m // HALO - 1
    ext_rows = tm + 8

    def body(up_ref, ap_ref, un_ref, df_ref, dfn_ref, cw_ref, cb_ref, dup_ref, red_ref, ext, dca_s):
        i = pl.program_id(1)
        a = up_ref[0].astype(F32)
        g = up_ref[1].astype(F32)
        df = df_ref[...].astype(F32)
        ext[0:HALO, :] = jnp.where(i == 0, 0.0, ap_ref[...].astype(F32))
        ext[HALO:HALO + tm, :] = a
        ext[HALO + tm:2 * HALO + tm, :] = un_ref[0].astype(F32)
        w = cw_ref[...]
        w0, w1, w2 = w[0:1, :], w[1:2, :], w[2:3, :]
        a1 = ext[pl.ds(HALO - 1, ext_rows), :]
        a2 = ext[pl.ds(HALO - 2, ext_rows), :]
        cae = w0 * a2 + w1 * a1 + w2 * ext[pl.ds(HALO, ext_rows), :] + cb_ref[...]
        act, dact = _gelu_parts(cae)
        dfe = jnp.concatenate([df, dfn_ref[...].astype(F32)[0:8, :]], axis=0)
        ge = jnp.concatenate([g, un_ref[1].astype(F32)[0:8, :]], axis=0)
        row = lax.broadcasted_iota(jnp.int32, (ext_rows, 1), 0)
        dcae = jnp.where((row < tm) | (i < ni - 1), dfe * ge * dact, 0.0)
        dca_s[...] = dcae
        dca = dcae[0:tm, :]
        dup_a = w2 * dca + w1 * dca_s[pl.ds(1, tm), :] + w0 * dca_s[pl.ds(2, tm), :]
        dup_g = df * act[0:tm, :]
        dup_ref[0] = dup_a.astype(BF)
        dup_ref[1] = dup_g.astype(BF)
        red = jnp.concatenate([
            _rowsum(dca * a2[0:tm, :]), _rowsum(dca * a1[0:tm, :]), _rowsum(dca * a), _rowsum(dca),
            _rowsum(dup_a), _rowsum(dup_g), jnp.zeros((2, fs), F32)], axis=0)

        @pl.when(i == 0)
        def _():
            red_ref[...] = red

        @pl.when(i > 0)
        def _():
            red_ref[...] += red

    nxt = lambda j, i: jnp.minimum((i + 1) * hb, last_hb)
    return _pcall(
        body, (up4, up4, up4, df4, df4, cw, cb), name=name, grid=(nj, ni), sem=("parallel", "arbitrary"), phases=phases,
        in_specs=[pl.BlockSpec((2, None, tm, fs), lambda j, i: (0, j, i, 0)),
                  pl.BlockSpec((None, None, HALO, fs), lambda j, i: (0, j, jnp.maximum(i * hb - 1, 0), 0)),
                  pl.BlockSpec((2, None, HALO, fs), lambda j, i: (0, j, nxt(j, i), 0)),
                  pl.BlockSpec((None, tm, fs), lambda j, i: (j, i, 0)),
                  pl.BlockSpec((None, HALO, fs), lambda j, i: (j, nxt(j, i), 0)),
                  pl.BlockSpec((None, 3, fs), lambda j, i: (j, 0, 0)),
                  pl.BlockSpec((None, 1, fs), lambda j, i: (j, 0, 0))],
        out_specs=[pl.BlockSpec((2, None, tm, fs), lambda j, i: (0, j, i, 0)),
                   pl.BlockSpec((None, 8, fs), lambda j, i: (j, 0, 0))],
        out_shape=[jax.ShapeDtypeStruct((2, nj, m, fs), BF), jax.ShapeDtypeStruct((nj, 8, fs), F32)],
        scratch_shapes=[pltpu.VMEM((2 * HALO + tm, fs), F32), pltpu.VMEM((ext_rows, fs), F32)])


def _gate_bwd(dm, z, ya, yb, ycp, scale, *, name, tm=256):
    m, d = dm.shape
    tm = _row_tile(m, tm)

    def body(dm_ref, ga, gb, gc, ya_ref, yb_ref, yc_ref, sc_ref, dya_ref, dyb_ref, dyc_ref, dz_ref, red_ref):
        i = pl.program_id(0)
        f32 = lambda r: r[...].astype(F32)
        dmv = f32(dm_ref)
        sa, sb, sc = jax.nn.sigmoid(f32(ga)), jax.nn.sigmoid(f32(gb)), jax.nn.sigmoid(f32(gc))
        scale_v = sc_ref[...]
        ycp_v = f32(yc_ref)
        dya_ref[...] = (dmv * sa).astype(BF)
        dyb_ref[...] = (dmv * sb).astype(BF)
        dyc = dmv * sc
        dyc_ref[...] = (dyc * scale_v).astype(BF)
        dga = dmv * f32(ya_ref) * (sa * (1.0 - sa))
        dgb = dmv * f32(yb_ref) * (sb * (1.0 - sb))
        dgc = dmv * (ycp_v * scale_v) * (sc * (1.0 - sc))
        dz_ref[:, 0:d] = dga.astype(BF)
        dz_ref[:, d:2 * d] = dgb.astype(BF)
        dz_ref[:, 2 * d:3 * d] = dgc.astype(BF)
        red = jnp.concatenate([_rowsum(dyc * ycp_v), _rowsum(dga), _rowsum(dgb), _rowsum(dgc),
                               jnp.zeros((4, d), F32)], axis=0)

        @pl.when(i == 0)
        def _():
            red_ref[...] = red

        @pl.when(i > 0)
        def _():
            red_ref[...] += red

    row = pl.BlockSpec((tm, d), lambda i: (i, 0))
    obf = jax.ShapeDtypeStruct((m, d), BF)
    return pl.pallas_call(
        body, name=name, grid=(m // tm,),
        in_specs=[row, _seg_spec(tm, d, 6), _seg_spec(tm, d, 7), _seg_spec(tm, d, 8), row, row, row,
                  pl.BlockSpec((1, d), lambda i: (0, 0))],
        out_specs=[row, row, row, pl.BlockSpec((tm, 3 * d), lambda i: (i, 2)), pl.BlockSpec((8, d), lambda i: (0, 0))],
        out_shape=[obf, obf, obf, jax.ShapeDtypeStruct((m, 9 * d), BF), jax.ShapeDtypeStruct((8, d), F32)],
        compiler_params=_params(("arbitrary",)),
    )(dm, z, z, z, ya, yb, ycp, scale)


def _mix_bwd(dz, dua, dub, ddd, z, conv_a, lnv, wm, wmt, bias_full, mask, *, name, tm=128, phases=()):
    m, d = dua.shape
    tm = _row_tile(m, tm)
    ni = m // tm
    grp = d // len(POOL_WINDOWS)
    ng = d // GMLP_BLOCK
    ext_rows = tm + 8

    def body(dz_in, dua_ref, dub_ref, dd_ref, zb, zc, zx, zu, zv, zp, zc_h, zx_h, dua_n, zb_n, dd_n,
             ca_ref, lnv_ref, wm_ref, wmt_ref, bias_ref, mask_ref,
             dz_ref, red_ref, dws_ref, dbs_ref, ext, sh_s, vn_s, mixed_s, dmx_s, dvn_s, dbs_acc):
        del dz_in
        i = pl.program_id(0)
        rows = []
        f32 = lambda r: r[...].astype(F32)
        zbv, zcv, zxv = f32(zb), f32(zc), f32(zx)
        pa = zcv * zxv
        ext[0:HALO, :] = jnp.where(i == 0, 0.0, f32(zc_h) * f32(zx_h))
        ext[HALO:HALO + tm, :] = pa
        w = ca_ref[...]
        w0, w1, w2 = w[0:1, :], w[1:2, :], w[2:3, :]
        p1 = ext[pl.ds(HALO - 1, tm), :]
        p2 = ext[pl.ds(HALO - 2, tm), :]
        conv = w0 * p2 + w1 * p1 + w2 * pa
        duav = f32(dua_ref)
        dzb = duav * conv
        dca = duav * zbv
        dca_n = jnp.where(i < ni - 1, f32(dua_n)[0:8, :] * f32(zb_n)[0:8, :], 0.0)
        sh_s[0:tm, :] = dca
        sh_s[tm:tm + 8, :] = dca_n
        dpa = w2 * dca + w1 * sh_s[pl.ds(1, tm), :] + w0 * sh_s[pl.ds(2, tm), :]
        dzc = dpa * zxv
        dzx = dpa * zcv
        dz_ref[:, 0:d] = dzb.astype(BF)
        dz_ref[:, d:2 * d] = dzc.astype(BF)
        dz_ref[:, 2 * d:3 * d] = dzx.astype(BF)
        rows += [_rowsum(dzb), _rowsum(dzc), _rowsum(dzx)]
        dconv = [_rowsum(dca * p2), _rowsum(dca * p1), _rowsum(dca * pa)]
        zuv, zvv = f32(zu), f32(zv)
        gu, dgu_dz = _gelu_parts(zuv)
        gv, dgv_dz = _gelu_parts(zvv)
        vhat, rstd = _ln_stats(gv)
        gain = lnv_ref[0:1, :]
        vn_s[...] = (vhat * gain + lnv_ref[1:2, :]).astype(BF)
        _spatial_mix(wm_ref, vn_s, mixed_s, bias_ref, tm, d)
        dubv = f32(dub_ref)
        dzu = dubv * mixed_s[...] * dgu_dz
        dmixed = dubv * gu
        dmx_s[...] = dmixed.astype(BF)
        _spatial_mix(wmt_ref, dmx_s, dvn_s, None, tm, d)
        dvn = dvn_s[...]
        dzv = _ln_bwd(dvn, vhat, rstd, gain) * dgv_dz
        dz_ref[:, 3 * d:4 * d] = dzu.astype(BF)
        dz_ref[:, 4 * d:5 * d] = dzv.astype(BF)
        rows += [_rowsum(dzu), _rowsum(dzv)]
        dlnv = [_rowsum(dvn * vhat), _rowsum(dvn)]
        dbs_part = dmixed[0:GMLP_BLOCK, :]
        for n in range(1, tm // GMLP_BLOCK):
            dbs_part = dbs_part + dmixed[n * GMLP_BLOCK:(n + 1) * GMLP_BLOCK, :]
        ddv = f32(dd_ref)
        t = (i * tm + lax.broadcasted_iota(jnp.int32, (ext_rows + 8, 1), 0) + 1).astype(F32)
        dde = jnp.concatenate([ddv, jnp.where(i < ni - 1, f32(dd_n), 0.0)], axis=0)
        for k, win in enumerate(POOL_WINDOWS):
            cs = slice(k * grp, (k + 1) * grp)
            ext[0:tm + HALO, cs] = dde[:, cs] / jnp.minimum(t, float(win))
        dzp_parts = []
        for k, win in enumerate(POOL_WINDOWS):
            cs = slice(k * grp, (k + 1) * grp)
            s = ext[0:tm, cs]
            for j in range(1, win):
                s = s + ext[pl.ds(j, tm), cs]
            dzp_parts.append(s - ddv[:, cs])
        dzp = jnp.concatenate(dzp_parts, axis=1)
        dz_ref[:, 5 * d:6 * d] = dzp.astype(BF)
        rows += [_rowsum(dzp)]
        red = jnp.concatenate(rows + dconv + dlnv + [jnp.zeros((5, d), F32)], axis=0)

        @pl.when(i == 0)
        def _():
            red_ref[...] = red
            dbs_acc[...] = dbs_part
            dws_ref[...] = jnp.zeros_like(dws_ref)

        @pl.when(i > 0)
        def _():
            red_ref[...] += red
            dbs_acc[...] += dbs_part

        for n in range(tm // GMLP_BLOCK):
            for g in range(ng):
                rs = slice(n * GMLP_BLOCK, (n + 1) * GMLP_BLOCK)
                cs = slice(g * GMLP_BLOCK, (g + 1) * GMLP_BLOCK)
                dws_ref[g] += mask_ref[...] * lax.dot_general(
                    dmx_s[rs, cs], vn_s[rs, cs], (NT, ((), ())), preferred_element_type=F32)

        @pl.when(i == ni - 1)
        def _():
            lane = lax.broadcasted_iota(jnp.int32, (GMLP_BLOCK, GMLP_BLOCK), 1)
            out = jnp.zeros((GMLP_BLOCK, GMLP_BLOCK), F32)
            for g in range(ng):
                sg = jnp.sum(dbs_acc[:, g * GMLP_BLOCK:(g + 1) * GMLP_BLOCK], axis=1, keepdims=True)
                out = out + jnp.where(lane == g, sg, 0.0)
            dbs_ref[...] = out

    row = pl.BlockSpec((tm, d), lambda i: (i, 0))
    full = lambda a: pl.BlockSpec(a.shape, lambda i: (0,) * a.ndim)
    hb = tm // HALO
    last_hb = m // HALO - 1
    nrow = pl.BlockSpec((HALO, d), lambda i: (jnp.minimum((i + 1) * hb, last_hb), 0))
    return _pcall(
        body, (dz, dua, dub, ddd, z, z, z, z, z, z, z, z, dua, z, ddd, conv_a, lnv, wm, wmt, bias_full, mask),
        name=name, grid=(ni,), sem=("arbitrary",), aliases={0: 0}, phases=phases,
        in_specs=[pl.BlockSpec(memory_space=pl.ANY), row, row, row]
        + [_seg_spec(tm, d, s) for s in range(6)]
        + [_prev_halo_spec(tm, d, 1), _prev_halo_spec(tm, d, 2), nrow, _next_halo_spec(tm, d, 0, m), nrow]
        + [full(conv_a), full(lnv), full(wm), full(wmt), full(bias_full), full(mask)],
        out_specs=[pl.BlockSpec((tm, 6 * d), lambda i: (i, 0)), pl.BlockSpec((16, d), lambda i: (0, 0)),
                   full(wm), pl.BlockSpec((GMLP_BLOCK, GMLP_BLOCK), lambda i: (0, 0))],
        out_shape=[jax.ShapeDtypeStruct(dz.shape, BF), jax.ShapeDtypeStruct((16, d), F32),
                   jax.ShapeDtypeStruct(wm.shape, F32), jax.ShapeDtypeStruct((GMLP_BLOCK, GMLP_BLOCK), F32)],
        scratch_shapes=[pltpu.VMEM((2 * HALO + tm, d), F32), pltpu.VMEM((tm + 8, d), F32),
                        pltpu.VMEM((tm, d), BF), pltpu.VMEM((tm, d), F32), pltpu.VMEM((tm, d), BF),
                        pltpu.VMEM((tm, d), F32), pltpu.VMEM((GMLP_BLOCK, d), F32)])


REST = ("w_a_out", "w_b_out", "w_pool", "w_o", "w_up", "w_down")


def _remote(src, dst, ssem, rsem, k, to):
    return pltpu.make_async_remote_copy(src_ref=src, dst_ref=dst, send_sem=ssem.at[k], recv_sem=rsem.at[k],
                                        device_id=to, device_id_type=MESH)


def _gather_phase1(shards):
    n = len(shards)

    def build(ins, outs, ssem, rsem, lsem):
        x, y, c, chips = _place()
        me = 4 * x + 2 * y + c
        local = [pltpu.make_async_copy(ins[a], outs[a].at[:, me], lsem.at[a]) for a in range(n)]
        sends, recvs = [], []
        for j, (cx, cy) in enumerate(chips):
            for a in range(n):
                sends.append(_remote(ins[a], outs[a].at[:, me], ssem, rsem, 4 * a + 1 + j, (cx, cy, c)))
                recvs.append(_remote(ins[a], outs[a].at[:, 4 * cx + 2 * cy + c], ssem, rsem, 4 * a + 1 + j, (cx, cy, c)))
        for a in range(n):
            sends.append(_remote(ins[a], outs[a].at[:, me], ssem, rsem, 4 * a, (x, y, 1 - c)))
            recvs.append(_remote(ins[a], outs[a].at[:, 4 * x + 2 * y + 1 - c], ssem, rsem, 4 * a, (x, y, 1 - c)))
        return dict(start=local + sends, recv=recvs, send=sends, local=local)

    outs = [jax.ShapeDtypeStruct((s.shape[0], N_DEV) + s.shape[1:], s.dtype) for s in shards]
    return _Phase(shards, outs, {}, 4 * n, n, build)


def _gather_phase2(fulls):
    n = len(fulls)

    def build(ins, outs, ssem, rsem, lsem):
        x, y, c, chips = _place()
        sends, recvs = [], []
        for j, (cx, cy) in enumerate(chips):
            for a in range(n):
                mine, theirs = 4 * cx + 2 * cy + c, 4 * cx + 2 * cy + 1 - c
                sends.append(_remote(ins[a].at[:, mine], outs[a].at[:, mine], ssem, rsem, 3 * a + j, (x, y, 1 - c)))
                recvs.append(_remote(ins[a].at[:, theirs], outs[a].at[:, theirs], ssem, rsem, 3 * a + j, (x, y, 1 - c)))
        return dict(start=sends, recv=recvs, send=sends, local=[])

    outs = [jax.ShapeDtypeStruct(f.shape, f.dtype) for f in fulls]
    return _Phase(fulls, outs, {a: a for a in range(n)}, 3 * n, 0, build)


def _pair_phase(grads):
    n = len(grads)

    def build(ins, outs, ssem, rsem, lsem):
        x, y, c, _ = _place()
        cps = [_remote(ins[a].at[:, 2 * q + (1 - c)], outs[a].at[q], ssem, rsem, 4 * a + q, (x, y, 1 - c))
               for a in range(n) for q in range(4)]
        return dict(start=cps, recv=cps, send=cps, local=[])

    outs = [jax.ShapeDtypeStruct((4, g.shape[0]) + g.shape[2:], g.dtype) for g in grads]
    return _Phase(grads, outs, {}, 4 * n, 0, build)


def _chip_phase(bufs, accs, l, depth):
    n = len(bufs)
    has = accs is not None

    def build(ins, outs, ssem, rsem, lsem):
        x, y, c, chips = _place()
        myq = 2 * x + y
        local, sends, recvs = [], [], []
        for a in range(n):
            local.append(pltpu.make_async_copy(ins[a].at[myq], outs[a].at[myq, l], lsem.at[a]))
            for j, (cx, cy) in enumerate(chips):
                q = 2 * cx + cy
                sends.append(_remote(ins[a].at[q], outs[a].at[myq, l], ssem, rsem, 3 * a + j, (cx, cy, c)))
                recvs.append(_remote(ins[a].at[q], outs[a].at[q, l], ssem, rsem, 3 * a + j, (cx, cy, c)))
        return dict(start=local + sends, recv=recvs, send=sends, local=local)

    outs = [jax.ShapeDtypeStruct((4, depth) + b.shape[1:], b.dtype) for b in bufs]
    return _Phase(list(bufs) + (list(accs) if has else []), outs, {n + a: a for a in range(n)} if has else {},
                  3 * n, n, build)


def _grad_chunks(n, g):
    if n == "w_pool":
        return g.reshape(g.shape[0], N_DEV, g.shape[1] // N_DEV, g.shape[2])
    if n in ("w_in", "w_up"):
        return g[None]
    return g.reshape(1, N_DEV, -1, g.shape[-1])


class _ReduceScatter:
    def __init__(self, depth, cidx):
        self.depth, self.cidx, self.acc, self.count = depth, cidx, {}, 0
        self.small_gathered = None

    def pair(self, names, grads):
        return _pair_phase([_grad_chunks(n, grads[n]) for n in names])

    def sums(self, names, grads, phase):
        out = []
        for n, r1 in zip(names, phase.results):
            out.append(_pair_sum(_grad_chunks(n, grads[n]), r1, self.cidx, name="rs_sum_%d" % self.count))
            self.count += 1
        return out

    def chip(self, names, bufs, l):
        accs = [self.acc[n] for n in names] if names[0] in self.acc else None
        return _chip_phase(bufs, accs, l, self.depth)

    def done(self, names, phase):
        for n, r in zip(names, phase.results):
            self.acc[n] = r


def _rest_views(fulls, d):
    a_out, b_out, pool, o, up, down = fulls
    grp = d // len(POOL_WINDOWS)
    return dict(w_a_out=a_out.reshape(d, d), w_b_out=b_out.reshape(d, d), w_o=o.reshape(d, d),
                w_pool=pool.reshape(len(POOL_WINDOWS), grp, grp), w_up8=up[0],
                wd4=down.reshape(N_DEV // 2, -1, d))


class _GatherPlan:
    def __init__(self):
        self.jobs, self.part, self.full = [], {}, {}

    def add(self, key, shard, first, second):
        self.jobs.append((key, shard, first, second))

    def phases(self, name):
        j1 = [j for j in self.jobs if j[2] == name]
        j2 = [j for j in self.jobs if j[3] == name]
        tagged = []
        if j1:
            tagged.append((self.part, j1, _gather_phase1([j[1] for j in j1])))
        if j2:
            tagged.append((self.full, j2, _gather_phase2([self.part[j[0]] for j in j2])))
        return tagged

    @staticmethod
    def collect(tagged):
        for store, jobs, phase in tagged:
            for j, r in zip(jobs, phase.results):
                store[j[0]] = r


def _layer_fwd(x, w, alpha, tag, plan=None):
    d = x.shape[1]
    grp = d // len(POOL_WINDOWS)

    def carried(kernel, *args, name, **kw):
        tagged = plan.phases(name) if plan else []
        out = kernel(*args, name=name, phases=[t[2] for t in tagged], **kw)
        _GatherPlan.collect(tagged)
        return out

    def weight(n, shape):
        return plan.full[n + tag].reshape(shape) if plan else w[n]

    z, h = carried(_mod_matmul, x, w["mod1"], w["w_in8"], w["b_in8"], flat_out=True, name="in_proj" + tag)
    ua, ub, dd = carried(_mix_fwd, z, w["conv_a"], w["lnv"], w["wm"], w["bias_full"], name="mix_fwd" + tag)
    w["w_a_out"], w["w_b_out"], w["w_o"] = (weight(n, (d, d)) for n in ("w_a_out", "w_b_out", "w_o"))
    w["w_pool"] = weight("w_pool", (len(POOL_WINDOWS), grp, grp))
    w["w_up8"] = weight("w_up8", (N_DEV, d, -1))
    ya = _mm_rows(ua, w["w_a_out"], dn=NN, name="a_out" + tag, out_dtype=ACT_DTYPE)
    yb = _mm_rows(ub, w["w_b_out"], dn=NN, name="b_out" + tag, out_dtype=ACT_DTYPE)
    ycp = _pool_proj(dd, w["w_pool"], dn=NN, name="pool_proj" + tag, out_dtype=ACT_DTYPE)
    merged = _merge(z, ya, yb, ycp, w["pool_scale"], name="merge" + tag)
    o = _mm_rows(merged, w["w_o"], dn=NN, name="o_proj" + tag)
    x1 = _resid_ln(x, o, w["ln1"], alpha, name="ln1" + tag)
    up8, h2 = carried(_mod_matmul, x1, w["mod2"], w["w_up8"], w["b_up8"], flat_out=False, name="up_proj" + tag)
    up4 = up8.reshape((2, up8.shape[0] // 2) + up8.shape[1:])
    f4 = carried(_ffn_fwd, up4, w["cw"], w["cb"], name="ffn_fwd" + tag)
    w["wd4"] = weight("wd4", (N_DEV // 2, -1, d))
    y2 = carried(_down_proj, f4, w["wd4"], name="down_proj" + tag)
    x2 = _resid_ln(x1, y2, w["ln2"], alpha, name="ln2" + tag)
    saved = dict(x=x, z=z, h=h, ua=ua, ub=ub, dd=dd, ya=ya, yb=yb, ycp=ycp, merged=merged, o=o, x1=x1,
                 up4=up4, h2=h2, f4=f4, y2=y2)
    return x2, saved


def _layer_bwd(dpart, dh_above, xmod_above, m_above, w, s, alpha, tag, l=0, above=None, rs=None, upper_reds=()):
    first, rest = ("w_in",), REST
    ph = lambda p: [p] if p is not None else ()
    r1a = rs.pair(first, above) if above else None
    dy2, dx1p, red2 = _resid_ln_bwd(dpart, dh_above, xmod_above, m_above, s["x1"], s["y2"], w["ln2"], alpha,
                                    name="ln2_bwd" + tag, phases=ph(r1a))
    r1b = rs.pair(rest, above) if above else None
    df4 = _down_bwd(dy2, w["wd4"], name="down_bwd" + tag, phases=ph(r1b))
    if above:
        sb_a, sb_b = rs.sums(first, above, r1a), rs.sums(rest, above, r1b)
    gw_down4 = _tn_shards_lhs(s["f4"], dy2, name="gw_down" + tag)
    if above:
        bufs = dict(zip(first + rest, sb_a + sb_b))
        heavy = ("w_in", "w_up")
        light = tuple(n for n in rest if n not in heavy)
    r3a = rs.chip(heavy, [bufs[n] for n in heavy], l + 1) if above else None
    dup4, redf = _ffn_bwd(s["up4"], df4, w["cw"], w["cb"], name="ffn_bwd" + tag, phases=ph(r3a))
    dup8 = dup4.reshape((dup4.shape[0] * dup4.shape[1],) + dup4.shape[2:])
    r3b = None
    if above:
        rs.done(heavy, r3a)
        r3b = rs.chip(light, [bufs[n] for n in light], l + 1)
    gw_up8 = _tn_shards_rhs(s["h2"], dup8, name="gw_up" + tag, phases=ph(r3b))
    if above:
        rs.done(light, r3b)
    own = rs is not None and l == 0
    big = dict(w_up=gw_up8, w_down=gw_down4)
    early = ("w_down", "w_up")
    o1 = rs.pair(early, big) if own else None
    dh2 = _nt_shards(dup8, w["w_up8"], name="up_bwd" + tag, phases=ph(o1))
    if own:
        sb_o = rs.sums(early, big, o1)
    do, dxp, red1 = _resid_ln_bwd(dx1p, dh2, s["x1"], w["mod2"][0:1], s["x"], s["o"], w["ln1"], alpha,
                                  name="ln1_bwd" + tag)
    dm = _mm_rows(do, w["w_o"], dn=NT, name="o_bwd" + tag, out_dtype=ACT_DTYPE)
    big["w_o"] = _mm_tn(s["merged"], do, name="gw_o" + tag)
    dya, dyb, dyc, dz, redg = _gate_bwd(dm, s["z"], s["ya"], s["yb"], s["ycp"], w["pool_scale"], name="gate_bwd" + tag)
    dua = _mm_rows(dya, w["w_a_out"], dn=NT, name="a_out_bwd" + tag, out_dtype=ACT_DTYPE)
    dub = _mm_rows(dyb, w["w_b_out"], dn=NT, name="b_out_bwd" + tag, out_dtype=ACT_DTYPE)
    ddd = _pool_proj(dyc, w["w_pool"], dn=NT, name="pool_bwd" + tag, out_dtype=ACT_DTYPE)
    big["w_a_out"] = _mm_tn(s["ua"], dya, name="gw_a_out" + tag)
    big["w_b_out"] = _mm_tn(s["ub"], dyb, name="gw_b_out" + tag)
    big["w_pool"] = _tn_pool(s["dd"], dyc, w["w_pool"].shape[0], name="gw_pool" + tag)
    o3 = rs.chip(early, sb_o, l) if own else None
    dz, redm, dws, dbs = _mix_bwd(dz, dua, dub, ddd, s["z"], w["conv_a"], w["lnv"], w["wm"], w["wmt"],
                                  w["bias_full"], w["mask"], name="mix_bwd" + tag, phases=ph(o3))
    reds = dict(red2=red2, redf=redf, red1=red1, redg=redg, redm=redm, dws=dws, dbs=dbs)
    mid = ("w_o", "w_a_out", "w_b_out", "w_pool")
    o1b = sg1 = None
    if own:
        rs.done(early, o3)
        o1b = rs.pair(mid, big)
        sg1 = _gather_phase1([_small_payload([reds] + list(upper_reds))])
    big["w_in"] = _tn_cols_rhs(s["h"], dz, w["w_in8"].shape[0], name="gw_in" + tag,
                               phases=[o1b, sg1] if own else ())
    tail_phases, pending = (), None
    if own:
        sb_m = rs.sums(mid, big, o1b)
        o1c, o3b, sg2 = rs.pair(first, big), rs.chip(mid, sb_m, l), _gather_phase2(sg1.results)
        tail_phases = [o1c, o3b, sg2]
    dh = _nt_cols(dz, w["w_in8"], name="in_bwd" + tag, phases=tail_phases)
    if own:
        rs.done(mid, o3b)
        rs.small_gathered = sg2.results[0]
        pending = (first, rs.chip(first, rs.sums(first, big, o1c), l))
    return dxp, dh, big, reds, pending


def _local_step(x, tgt, ws, alpha, plan=None, rs=None):
    depth = len(ws)
    saved = []
    y = x
    for l in range(depth):
        if plan and l > 0:
            ws[l]["w_in8"] = plan.full["w_in8_l%d" % l][0]
        y, s = _layer_fwd(y, ws[l], alpha, "_l%d" % l, plan)
        saved.append(s)
    dpart, loss_blk = _loss_grad(y, tgt, name="loss_grad")
    dh = xmod = mvec = above = pending = None
    bigs, reds = [None] * depth, [None] * depth
    for l in reversed(range(depth)):
        dpart, dh, bigs[l], reds[l], pending = _layer_bwd(dpart, dh, xmod, mvec, ws[l], saved[l], alpha, "_l%d" % l,
                                                          l, above if rs else None, rs, reds[l + 1:])
        xmod, mvec, above = saved[l]["x"], ws[l]["mod1"][0:1], bigs[l]
    grad_x, red0 = _resid_ln_bwd(dpart, dh, xmod, mvec, None, None, None, alpha, name="in_bwd_tail",
                                 phases=[pending[1]] if pending else ())
    if pending:
        rs.done(*pending)
    d_ada = []
    for l in range(depth):
        below = red0 if l == 0 else reds[l - 1]["red2"]
        r1, r2 = reds[l]["red1"], reds[l]["red2"]
        d_ada.append(jnp.stack([below[1], below[0], r1[4], r1[1], r1[0], r2[4]]))
    return loss_blk, grad_x, bigs, reds, jnp.stack(d_ada)


def _small_grads(r):
    redm, redg, redf = r["redm"], r["redg"], r["redf"]
    ng = r["dws"].shape[0]
    return dict(
        b_in=jnp.concatenate([redm[0:6], redg[1:4]], axis=0).reshape(-1),
        conv_a=redm[6:9], ln_v_g=redm[9], ln_v_b=redm[10],
        w_spatial=r["dws"], b_spatial=r["dbs"][:, :ng].T,
        pool_scale=redg[0], ln1_g=r["red1"][2], ln1_b=r["red1"][3],
        b_up=jnp.concatenate([redf[:, 4, :].reshape(-1), redf[:, 5, :].reshape(-1)]),
        conv_ffn=jnp.transpose(redf[:, 0:3, :], (1, 0, 2)).reshape(3, -1), conv_ffn_b=redf[:, 3, :].reshape(-1),
        ln2_g=r["red2"][2], ln2_b=r["red2"][3])


def _small_payload(reds):
    smalls = [_small_grads(r) for r in reds]
    order = SMALL_REPLICATED + SMALL_SHARDED
    flat = jnp.concatenate([smalls[l][n].reshape(-1) for n in order for l in range(len(reds))])
    return _as_rows(flat)[None]


def _layer_weights(l, ada, conv_a, conv_ffn, p):
    sh1, sc1, gt1, sh2, sc2, gt2 = (ada[l, k][None, :] for k in range(6))
    nb = N_DEV
    fs = p["b_up"].shape[1] // nb
    nj = nb // 2
    pos = jnp.arange(GMLP_BLOCK)
    allowed = (pos[None, :] // CHUNK) <= (pos[:, None] // CHUNK)
    wmask = jnp.where(allowed[None], p["w_spatial"][l], 0.0)
    return dict(
        mod1=jnp.concatenate([1.0 + sc1, sh1]), mod2=jnp.concatenate([1.0 + sc2, sh2]),
        ln1=jnp.concatenate([gt1, p["ln1_g"][l][None], p["ln1_b"][l][None]]),
        ln2=jnp.concatenate([gt2, p["ln2_g"][l][None], p["ln2_b"][l][None]]),
        b_in8=p["b_in"][l].reshape(N_DEV, 1, -1), b_up8=p["b_up"][l].reshape(nb, 1, fs),
        conv_a=conv_a[l], lnv=jnp.stack([p["ln_v_g"][l], p["ln_v_b"][l]]),
        wm=wmask.astype(BF), wmt=jnp.transpose(wmask, (0, 2, 1)).astype(BF),
        bias_full=jnp.repeat(p["b_spatial"][l].T, GMLP_BLOCK, axis=1), mask=allowed.astype(F32),
        pool_scale=p["pool_scale"][l][None],
        cw=jnp.transpose(conv_ffn[l].reshape(3, nj, fs), (1, 0, 2)), cb=p["conv_ffn_b"][l].reshape(nj, 1, fs))


ANY = pl.BlockSpec(memory_space=pl.ANY)


def _place():
    x, y, c = lax.axis_index("x"), lax.axis_index("y"), lax.axis_index("c")
    chips = [(1 - x, y), (x, 1 - y), (1 - x, 1 - y)]
    return x, y, c, chips


def _allgather_vmem(xs, *, name):
    r, cdim = xs.shape

    def body(x_ref, out_ref, send_sems, recv_sems, local_sem):
        x, y, c, chips = _place()
        me, sibling = (x, y, c), (x, y, 1 - c)

        def rows(px, py, pc):
            return out_ref.at[pl.ds((4 * px + 2 * py + pc) * r, r), :]

        def copy(k, block, to, src=None):
            return pltpu.make_async_remote_copy(
                src_ref=rows(*block) if src is None else src, dst_ref=rows(*block),
                send_sem=send_sems.at[k], recv_sem=recv_sems.at[k], device_id=to, device_id_type=MESH)

        mine = pltpu.make_async_copy(x_ref, rows(*me), local_sem)
        mine.start()
        first = [copy(0, me, sibling, src=x_ref)]
        first += [copy(1 + j, me, (*chip, c), src=x_ref) for j, chip in enumerate(chips)]
        for cp in first:
            cp.start()
        passed = [copy(4 + j, (*chip, c), sibling) for j, chip in enumerate(chips)]
        for j, chip in enumerate(chips):
            copy(1 + j, (*chip, c), me).wait_recv()
            passed[j].start()
        copy(0, sibling, me).wait_recv()
        for j, chip in enumerate(chips):
            copy(4 + j, (*chip, 1 - c), me).wait_recv()
        for cp in first + passed:
            cp.wait_send()
        mine.wait()

    return pl.pallas_call(
        body, name=name, out_shape=jax.ShapeDtypeStruct((N_DEV * r, cdim), xs.dtype),
        in_specs=[pl.BlockSpec(memory_space=pltpu.VMEM)], out_specs=pl.BlockSpec(memory_space=pltpu.VMEM),
        scratch_shapes=[pltpu.SemaphoreType.DMA((7,)), pltpu.SemaphoreType.DMA((7,)), pltpu.SemaphoreType.DMA],
        compiler_params=_params(),
    )(xs)


def _gather_weights(shards, *, name):
    n = len(shards)

    def body(*refs):
        ins, outs = refs[:n], refs[n:2 * n]
        send_sems, recv_sems, local_sems = refs[2 * n:]
        x, y, c, chips = _place()
        me, sibling = (x, y, c), (x, y, 1 - c)

        def slot(a, px, py, pc):
            return outs[a].at[:, 4 * px + 2 * py + pc]

        def copy(a, k, block, to, src=None):
            return pltpu.make_async_remote_copy(
                src_ref=slot(a, *block) if src is None else src, dst_ref=slot(a, *block),
                send_sem=send_sems.at[7 * a + k], recv_sem=recv_sems.at[7 * a + k], device_id=to,
                device_id_type=MESH)

        mine = [pltpu.make_async_copy(ins[a], slot(a, *me), local_sems.at[a]) for a in range(n)]
        for cp in mine:
            cp.start()
        first = []
        for j, chip in enumerate(chips):
            first += [copy(a, 1 + j, me, (*chip, c), src=ins[a]) for a in range(n)]
        first += [copy(a, 0, me, sibling, src=ins[a]) for a in range(n)]
        for cp in first:
            cp.start()
        passed = []
        for j, chip in enumerate(chips):
            for a in range(n):
                copy(a, 1 + j, (*chip, c), me).wait_recv()
                fwd = copy(a, 4 + j, (*chip, c), sibling)
                fwd.start()
                passed.append(fwd)
        for a in range(n):
            copy(a, 0, sibling, me).wait_recv()
        for j, chip in enumerate(chips):
            for a in range(n):
                copy(a, 4 + j, (*chip, 1 - c), me).wait_recv()
        for cp in first + passed:
            cp.wait_send()
        for cp in mine:
            cp.wait()

    out_shape = [jax.ShapeDtypeStruct((s.shape[0], N_DEV) + s.shape[1:], s.dtype) for s in shards]
    return pl.pallas_call(
        body, name=name, out_shape=out_shape, in_specs=[ANY] * n, out_specs=[ANY] * n,
        scratch_shapes=[pltpu.SemaphoreType.DMA((7 * n,)), pltpu.SemaphoreType.DMA((7 * n,)),
                        pltpu.SemaphoreType.DMA((n,))],
        compiler_params=_params(),
    )(*shards)


def _pick_tile(r, cap):
    best = None
    for t in range(8, min(r, cap) + 1, 8):
        if r % t == 0:
            best = t
    return best if best is not None else r


def _pair_sum(g, r1, cidx, *, name):
    p, _, r, cdim = g.shape
    tr = _pick_tile(r, 256)

    def body(c_ref, g_ref, r_ref, o_ref):
        del c_ref
        o_ref[...] = (g_ref[...].astype(F32) + r_ref[...].astype(F32)).astype(BF)

    grid_spec = pltpu.PrefetchScalarGridSpec(
        num_scalar_prefetch=1, grid=(4, r // tr),
        in_specs=[pl.BlockSpec((p, None, tr, cdim), lambda q, i, c: (0, 2 * q + c[0], i, 0)),
                  pl.BlockSpec((None, p, tr, cdim), lambda q, i, c: (q, 0, i, 0))],
        out_specs=pl.BlockSpec((None, p, tr, cdim), lambda q, i, c: (q, 0, i, 0)))
    return pl.pallas_call(
        body, name=name, grid_spec=grid_spec, out_shape=jax.ShapeDtypeStruct((4, p, r, cdim), BF),
        compiler_params=_params(("arbitrary", "arbitrary")),
    )(cidx, g, r1)


def _ada_fwd(c_all, w_ada, *, name):
    depth, d, ns = w_ada.shape
    nb = c_all.shape[0]

    def body(c_ref, w_ref, o_ref):
        cv = c_ref[...]
        act = cv * jax.nn.sigmoid(cv)
        o_ref[...] = jnp.dot(act, w_ref[...], preferred_element_type=F32, precision=lax.Precision.HIGHEST)

    return pl.pallas_call(
        body, name=name, grid=(depth,),
        in_specs=[pl.BlockSpec((nb, d), lambda l: (0, 0)), pl.BlockSpec((None, d, ns), lambda l: (l, 0, 0))],
        out_specs=pl.BlockSpec((None, nb, ns), lambda l: (l, 0, 0)),
        out_shape=jax.ShapeDtypeStruct((depth, nb, ns), F32), compiler_params=_params(("parallel",)),
    )(c_all, w_ada)


def _ada_bwd(ct, dmine, dall, *, name):
    depth, nb, ns = dmine.shape
    d = ct.shape[0]

    def body(ct_ref, dm_ref, da_ref, gw_ref, gb_ref):
        cv = ct_ref[...]
        act = cv * jax.nn.sigmoid(cv)
        gw_ref[...] = jnp.dot(act, dm_ref[...], preferred_element_type=F32, precision=lax.Precision.HIGHEST)
        s = da_ref[0]
        for b in range(1, nb):
            s = s + da_ref[b]
        gb_ref[...] = s

    return pl.pallas_call(
        body, name=name, grid=(depth,),
        in_specs=[pl.BlockSpec((d, nb), lambda l: (0, 0)), pl.BlockSpec((None, nb, ns), lambda l: (l, 0, 0)),
                  pl.BlockSpec(dall.shape, lambda l: (0, 0, 0))],
        out_specs=[pl.BlockSpec((None, d, ns), lambda l: (l, 0, 0)), pl.BlockSpec(dall.shape[1:], lambda l: (0, 0))],
        out_shape=[jax.ShapeDtypeStruct((depth, d, ns), F32), jax.ShapeDtypeStruct(dall.shape[1:], F32)],
        compiler_params=_params(("arbitrary",)),
    )(ct, dmine, dall)


def _sum_parts(parts, *, name):
    p, r, cdim = parts.shape
    tr = _pick_tile(r, 512)

    def body(p_ref, o_ref):
        s = p_ref[0]
        for k in range(1, p):
            s = s + p_ref[k]
        o_ref[...] = s

    return pl.pallas_call(
        body, name=name, grid=(r // tr,),
        in_specs=[pl.BlockSpec((p, tr, cdim), lambda i: (0, i, 0))], out_specs=pl.BlockSpec((tr, cdim), lambda i: (i, 0)),
        out_shape=jax.ShapeDtypeStruct((r, cdim), F32), compiler_params=_params(("parallel",)),
    )(parts)


def _adamw(parts, w, m, v, *, name):
    p, depth, r, cdim = parts.shape
    tr = _pick_tile(r, 256)

    def body(p_ref, w_ref, m_ref, v_ref, g_out, d_out, m_out, v_out):
        g = p_ref[0].astype(F32)
        for k in range(1, p):
            g = g + p_ref[k].astype(F32)
        m2 = ADAM_B1 * m_ref[...] + (1.0 - ADAM_B1) * g
        v2 = ADAM_B2 * v_ref[...] + (1.0 - ADAM_B2) * (g * g)
        m_hat = m2 / (1.0 - ADAM_B1 ** ADAM_STEP)
        v_hat = v2 / (1.0 - ADAM_B2 ** ADAM_STEP)
        g_out[...] = g
        d_out[...] = -ADAM_LR * (m_hat / (jnp.sqrt(v_hat) + ADAM_EPS) + ADAM_WD * w_ref[...])
        m_out[...] = m2
        v_out[...] = v2

    blk = pl.BlockSpec((None, tr, cdim), lambda l, i: (l, i, 0))
    out = jax.ShapeDtypeStruct((depth, r, cdim), F32)
    return pl.pallas_call(
        body, name=name, grid=(depth, r // tr),
        in_specs=[pl.BlockSpec((p, None, tr, cdim), lambda l, i: (0, l, i, 0)), blk, blk, blk],
        out_specs=[blk, blk, blk, blk], out_shape=[out, out, out, out],
        compiler_params=_params(("parallel", "parallel")),
    )(parts, w, m, v)


BIG = ("w_in", "w_a_out", "w_b_out", "w_pool", "w_o", "w_up", "w_down")
SMALL_REPLICATED = ("b_in", "ln_v_g", "ln_v_b", "w_spatial", "b_spatial", "pool_scale", "ln1_g", "ln1_b", "b_up",
                    "conv_ffn_b", "ln2_g", "ln2_b")
SMALL_SHARDED = ("conv_a", "conv_ffn")
WEIGHTS = ("w_ada", "b_ada", "w_in", "b_in", "conv_a", "w_a_out", "ln_v_g", "ln_v_b", "w_spatial", "b_spatial",
           "w_b_out", "w_pool", "pool_scale", "w_o", "ln1_g", "ln1_b", "w_up", "b_up", "conv_ffn", "conv_ffn_b",
           "w_down", "ln2_g", "ln2_b")
LANES = 128


def _as_rows(flat, mult=8):
    n = flat.shape[0]
    pad = (-n) % (LANES * mult)
    if pad:
        flat = jnp.concatenate([flat, jnp.zeros((pad,), flat.dtype)])
    return flat.reshape(-1, LANES)


def _shard3(a):
    return a.reshape((-1,) + a.shape[-2:])


def kernel(x, c, w_ada, b_ada, w_in, b_in, conv_a, w_a_out, ln_v_g, ln_v_b, w_spatial, b_spatial, w_b_out, w_pool, pool_scale, w_o, ln1_g, ln1_b, w_up, b_up, conv_ffn, conv_ffn_b, w_down, ln2_g, ln2_b, loss_target, m_w_ada, m_b_ada, m_w_in, m_b_in, m_conv_a, m_w_a_out, m_ln_v_g, m_ln_v_b, m_w_spatial, m_b_spatial, m_w_b_out, m_w_pool, m_pool_scale, m_w_o, m_ln1_g, m_ln1_b, m_w_up, m_b_up, m_conv_ffn, m_conv_ffn_b, m_w_down, m_ln2_g, m_ln2_b, v_w_ada, v_b_ada, v_w_in, v_b_in, v_conv_a, v_w_a_out, v_ln_v_g, v_ln_v_b, v_w_spatial, v_b_spatial, v_w_b_out, v_w_pool, v_pool_scale, v_w_o, v_ln1_g, v_ln1_b, v_w_up, v_b_up, v_conv_ffn, v_conv_ffn_b, v_w_down, v_ln2_g, v_ln2_b):
    p = dict(locals())
    depth, d = w_in.shape[0], w_in.shape[1]
    alpha = (2 * depth) ** 0.25
    me = 4 * lax.axis_index("x") + 2 * lax.axis_index("y") + lax.axis_index("c")
    cidx = lax.axis_index("c").astype(jnp.int32).reshape(1)

    n_ca, n_cf = conv_a.size, conv_ffn.size
    packed = _as_rows(jnp.concatenate([c.reshape(-1), conv_a.reshape(-1), conv_ffn.reshape(-1)]))
    got = _allgather_vmem(packed, name="gather_cond").reshape(N_DEV, -1)
    c_all = got[:, :d]
    ct = c_all.T
    conv_a_full = jnp.transpose(got[:, d:d + n_ca].reshape((N_DEV,) + conv_a.shape), (1, 2, 0, 3)).reshape(depth, 3, -1)
    conv_ffn_full = jnp.transpose(got[:, d + n_ca:d + n_ca + n_cf].reshape((N_DEV,) + conv_ffn.shape),
                                  (1, 2, 0, 3)).reshape(depth, 3, -1)

    ns_ada = w_ada.shape[2]
    ada_part = _ada_fwd(c_all, w_ada, name="ada_fwd")
    ada_all = _allgather_vmem(_as_rows(ada_part.reshape(-1)), name="gather_ada")
    ada_all = ada_all.reshape(N_DEV, depth, N_DEV, ns_ada)
    ada_mine = lax.dynamic_index_in_dim(ada_all, me, axis=2, keepdims=False)
    ada = jnp.transpose(ada_mine, (1, 0, 2)).reshape(depth, -1) + b_ada
    ada = ada.reshape(depth, 6, d)

    shards = [{n: _shard3(p[n][l].astype(BF)) for n in BIG} for l in range(depth)]
    ws = [_layer_weights(l, ada, conv_a_full, conv_ffn_full, p) for l in range(depth)]
    ws[0]["w_in8"] = _gather_weights([shards[0]["w_in"]], name="gather_w_in0")[0][0]
    plan = _GatherPlan()
    four = ("w_a_out", "w_b_out", "w_pool", "w_o")
    for l in range(depth):
        t, prev, sh = "_l%d" % l, "_l%d" % (l - 1), shards[l]
        if l == 0:
            for n in four:
                plan.add(n + t, sh[n], "in_proj" + t, "mix_fwd" + t)
            plan.add("w_up8" + t, sh["w_up"], "in_proj" + t, "mix_fwd" + t)
            plan.add("wd4" + t, sh["w_down"], "mix_fwd" + t, "up_proj" + t)
        else:
            plan.add("w_in8" + t, sh["w_in"], "ffn_fwd" + prev, "down_proj" + prev)
            for n in four:
                plan.add(n + t, sh[n], "up_proj" + prev, "ffn_fwd" + prev)
            plan.add("w_up8" + t, sh["w_up"], "in_proj" + t, "mix_fwd" + t)
            plan.add("wd4" + t, sh["w_down"], "in_proj" + t, "mix_fwd" + t)

    rs = _ReduceScatter(depth, cidx)
    loss_blk, grad_x, bigs, reds, d_ada = _local_step(x[0], loss_target[0], ws, alpha, plan, rs)
    loss = lax.psum(loss_blk[0, 0], ("x", "y", "c"))

    dada_all = _allgather_vmem(_as_rows(d_ada.reshape(-1)), name="gather_dada")
    dada_all = dada_all.reshape(N_DEV, -1, LANES)
    dflat = dada_all.reshape(N_DEV, depth, 6 * d)
    dmine = lax.dynamic_slice_in_dim(dflat, me * ns_ada, ns_ada, axis=2)
    gw_ada, gb_rows = _ada_bwd(ct, jnp.transpose(dmine, (1, 0, 2)), dada_all, name="ada_bwd")
    gb_ada = gb_rows.reshape(-1)[:depth * 6 * d].reshape(depth, 6 * d)

    out = {}
    for n in BIG:
        parts = rs.acc[n]
        shard_shape = p[n].shape
        w3 = p[n].reshape(depth, -1, shard_shape[-1])
        parts4 = parts.reshape((4,) + w3.shape)
        res = _adamw(parts4, w3, p["m_" + n].reshape(w3.shape), p["v_" + n].reshape(w3.shape), name="adamw_" + n)
        out[n] = [r.reshape(shard_shape) for r in res]
    out["w_ada"] = _adamw(gw_ada[None], w_ada, m_w_ada, v_w_ada, name="adamw_w_ada")

    order = SMALL_REPLICATED + SMALL_SHARDED
    n_rep = sum(p[n].size for n in SMALL_REPLICATED)
    n_pay = n_rep + N_DEV * (conv_a.size + conv_ffn.size)
    gsum = _sum_parts(rs.small_gathered.reshape(N_DEV, -1, LANES), name="sum_small").reshape(-1)[:n_pay]
    ga_full = gsum[n_rep:n_rep + depth * 3 * d].reshape(depth, 3, d)
    gf_full = gsum[n_rep + depth * 3 * d:].reshape(depth, 3, -1)
    ca_w, cf_w = conv_a.shape[2], conv_ffn.shape[2]
    g_ca = lax.dynamic_slice_in_dim(ga_full, me * ca_w, ca_w, axis=2)
    g_cf = lax.dynamic_slice_in_dim(gf_full, me * cf_w, cf_w, axis=2)
    names = ("b_ada",) + order
    gflat = _as_rows(jnp.concatenate([gb_ada.reshape(-1), gsum[:n_rep], g_ca.reshape(-1), g_cf.reshape(-1)]))
    pack = lambda pre: _as_rows(jnp.concatenate([p[pre + n].reshape(-1) for n in names]))
    res = _adamw(gflat[None, None], pack("")[None], pack("m_")[None], pack("v_")[None], name="adamw_small")
    off = 0
    for n in names:
        size = p[n].size
        out[n] = [r.reshape(-1)[off:off + size].reshape(p[n].shape) for r in res]
        off += size

    return (loss, grad_x[None]) + tuple(out[n][k] for k in range(4) for n in WEIGHTS)
```

```python
import functools

import jax
import jax.numpy as jnp
from jax import lax
from jax.experimental import pallas as pl
from jax.experimental.pallas import tpu as pltpu

F32 = jnp.float32
BF = jnp.bfloat16
MESH = pl.DeviceIdType.MESH

LN_EPS = 1e-5
POOL_WINDOWS = (2, 4, 8, 16)
GMLP_BLOCK = 128
CHUNK = 64
HALO = 16
ADAM_LR, ADAM_B1, ADAM_B2, ADAM_EPS, ADAM_WD, ADAM_STEP = 0.001, 0.9, 0.999, 1e-08, 0.01, 10
N_DEV = 8
VMEM_LIMIT = 56 * 1024 * 1024

GRAD_DTYPE = BF
ACT_DTYPE = BF

NN = ((1,), (0,))
NT = ((1,), (1,))
TN = ((0,), (0,))


def _params(sem=None, vmem=VMEM_LIMIT, **kw):
    if sem is not None:
        kw["dimension_semantics"] = sem
    return pltpu.CompilerParams(vmem_limit_bytes=vmem, **kw)


class _Phase:
    def __init__(self, ins, out_shapes, aliases, n_remote, n_local, build):
        self.ins, self.out_shapes, self.aliases = list(ins), list(out_shapes), dict(aliases)
        self.n_remote, self.n_local, self.build = n_remote, n_local, build
        self.results = None


def _pcall(body, args, *, name, grid, in_specs, out_specs, out_shape, scratch_shapes=(), sem=None, aliases=None,
           phases=()):
    aliases = dict(aliases or {})
    if not phases:
        return pl.pallas_call(
            body, name=name, grid=grid, in_specs=list(in_specs), out_specs=out_specs, out_shape=out_shape,
            scratch_shapes=list(scratch_shapes), input_output_aliases=aliases, compiler_params=_params(sem),
        )(*args)
    single = not isinstance(out_shape, (list, tuple))
    o_specs = [out_specs] if single else list(out_specs)
    o_shapes = [out_shape] if single else list(out_shape)
    n_in, n_out, n_scr = len(args), len(o_shapes), len(scratch_shapes)
    ex_args, ex_out, sems = [], [], []
    for ph in phases:
        for src, dst in ph.aliases.items():
            aliases[n_in + len(ex_args) + src] = n_out + len(ex_out) + dst
        ex_args += ph.ins
        ex_out += ph.out_shapes
        sems += [pltpu.SemaphoreType.DMA((max(ph.n_remote, 1),)), pltpu.SemaphoreType.DMA((max(ph.n_remote, 1),)),
                 pltpu.SemaphoreType.DMA((max(ph.n_local, 1),))]

    def wrapped(*refs):
        pos = n_in
        ph_in = []
        for ph in phases:
            ph_in.append(refs[pos:pos + len(ph.ins)])
            pos += len(ph.ins)
        base_out = refs[pos:pos + n_out]
        pos += n_out
        ph_out = []
        for ph in phases:
            ph_out.append(refs[pos:pos + len(ph.out_shapes)])
            pos += len(ph.out_shapes)
        base_scr = refs[pos:pos + n_scr]
        ph_sems = refs[pos + n_scr:]
        first = last = None
        for ax, n in enumerate(grid):
            pid = pl.program_id(ax)
            first = (pid == 0) if first is None else first & (pid == 0)
            last = (pid == n - 1) if last is None else last & (pid == n - 1)

        def ops(k):
            return phases[k].build(ph_in[k], ph_out[k], *ph_sems[3 * k:3 * k + 3])

        @pl.when(first)
        def _():
            for k in range(len(phases)):
                for cp in ops(k)["start"]:
                    cp.start()

        body(*refs[:n_in], *base_out, *base_scr)

        @pl.when(last)
        def _():
            for k in range(len(phases)):
                o = ops(k)
                for cp in o["recv"]:
                    cp.wait_recv()
                for cp in o["send"]:
                    cp.wait_send()
                for cp in o["local"]:
                    cp.wait()

    hbm = pl.BlockSpec(memory_space=pl.ANY)
    res = pl.pallas_call(
        wrapped, name=name, grid=grid, in_specs=list(in_specs) + [hbm] * len(ex_args),
        out_specs=o_specs + [hbm] * len(ex_out), out_shape=o_shapes + ex_out,
        scratch_shapes=list(scratch_shapes) + sems, input_output_aliases=aliases,
        compiler_params=_params(("arbitrary",) * len(grid)),
    )(*args, *ex_args)
    pos = n_out
    for ph in phases:
        ph.results = list(res[pos:pos + len(ph.out_shapes)])
        pos += len(ph.out_shapes)
    return res[0] if single else list(res[:n_out])


def _gelu_parts(x):
    k = 0.7978845608028654
    x2 = x * x
    t = jnp.tanh(k * (x + 0.044715 * (x2 * x)))
    cdf = 0.5 * (1.0 + t)
    dcdf = 0.5 * (1.0 - t * t) * (k * (1.0 + 3.0 * 0.044715 * x2))
    return x * cdf, cdf + x * dcdf


def _gelu(x):
    t = jnp.tanh(0.7978845608028654 * (x + 0.044715 * (x * x * x)))
    return x * (0.5 * (1.0 + t))


def _rowsum(v):
    return jnp.sum(v, axis=0, keepdims=True)


def _ln_stats(r):
    mu = jnp.mean(r, axis=-1, keepdims=True)
    xc = r - mu
    var = jnp.mean(xc * xc, axis=-1, keepdims=True)
    rstd = lax.rsqrt(var + LN_EPS)
    return xc * rstd, rstd


def _ln_bwd(dy, xhat, rstd, gain):
    dxh = dy * gain
    m1 = jnp.mean(dxh, axis=-1, keepdims=True)
    m2 = jnp.mean(dxh * xhat, axis=-1, keepdims=True)
    return rstd * (dxh - m1 - xhat * m2)


def _matmul(a, b, *, dn, grid, a_spec, b_spec, o_spec, out_shape, acc_shape, name, phases=(), into=None):
    nk = grid[2]
    direct = out_shape.dtype == F32

    def body(a_ref, b_ref, *rest):
        o_ref, scratch = (rest[1], rest[2:]) if into is not None else (rest[0], rest[1:])
        prod = lax.dot_general(a_ref[...], b_ref[...], (dn, ((), ())), preferred_element_type=F32)
        if nk == 1:
            o_ref[...] = prod.astype(o_ref.dtype)
            return
        acc = o_ref if direct else scratch[0]
        k = pl.program_id(2)

        @pl.when(k == 0)
        def _():
            acc[...] = prod

        @pl.when(k > 0)
        def _():
            acc[...] += prod

        if not direct:
            @pl.when(k == nk - 1)
            def _():
                o_ref[...] = acc[...].astype(o_ref.dtype)

    scratch = [] if (direct or nk == 1) else [pltpu.VMEM(acc_shape, F32)]
    args, in_specs, aliases = (a, b), [a_spec, b_spec], None
    if into is not None:
        args, in_specs, aliases = (a, b, into), in_specs + [pl.BlockSpec(memory_space=pl.ANY)], {2: 0}
    return _pcall(body, args, name=name, grid=grid, in_specs=in_specs, out_specs=o_spec,
                  out_shape=out_shape, scratch_shapes=scratch, sem=("parallel", "parallel", "arbitrary"),
                  aliases=aliases, phases=phases)


def _row_tile(m, want):
    t = min(m, want)
    assert m % t == 0
    return t


def _mm_rows(a, w, *, dn, name, out_dtype=F32, tm=2048):
    m, k = a.shape
    n = w.shape[1] if dn == NN else w.shape[0]
    tm = _row_tile(m, tm)
    return _matmul(
        a, w, dn=dn, grid=(m // tm, 1, 1), name=name,
        a_spec=pl.BlockSpec((tm, k), lambda i, j, kk: (i, 0)),
        b_spec=pl.BlockSpec(w.shape, lambda i, j, kk: (0, 0)),
        o_spec=pl.BlockSpec((tm, n), lambda i, j, kk: (i, 0)),
        out_shape=jax.ShapeDtypeStruct((m, n), out_dtype), acc_shape=(tm, n))


def _mm_tn(a, b, *, name, tk=2048):
    m, ka = a.shape
    n = b.shape[1]
    tk = _row_tile(m, tk)
    return _matmul(
        a, b, dn=TN, grid=(1, 1, m // tk), name=name,
        a_spec=pl.BlockSpec((tk, ka), lambda i, j, kk: (kk, 0)),
        b_spec=pl.BlockSpec((tk, n), lambda i, j, kk: (kk, 0)),
        o_spec=pl.BlockSpec((ka, n), lambda i, j, kk: (0, 0)),
        out_shape=jax.ShapeDtypeStruct((ka, n), GRAD_DTYPE), acc_shape=(ka, n))


def _mod_matmul(x, mod, w8, bias8, *, flat_out, name, tm=2048, phases=()):
    m, k = x.shape
    nb, _, ns = w8.shape
    tm = _row_tile(m, tm)

    def body(x_ref, mod_ref, w_ref, b_ref, o_ref, h_ref, hs):
        @pl.when(pl.program_id(1) == 0)
        def _():
            h = (x_ref[...] * mod_ref[0:1, :] + mod_ref[1:2, :]).astype(BF)
            hs[...] = h
            h_ref[...] = h

        o_ref[...] = (jnp.dot(hs[...], w_ref[...], preferred_element_type=F32) + b_ref[...]).astype(o_ref.dtype)

    if flat_out:
        o_spec = pl.BlockSpec((tm, ns), lambda i, j: (i, j))
        o_shape = jax.ShapeDtypeStruct((m, nb * ns), ACT_DTYPE)
    else:
        o_spec = pl.BlockSpec((None, tm, ns), lambda i, j: (j, i, 0))
        o_shape = jax.ShapeDtypeStruct((nb, m, ns), ACT_DTYPE)
    return _pcall(
        body, (x, mod, w8, bias8), name=name, grid=(m // tm, nb),
        in_specs=[pl.BlockSpec((tm, k), lambda i, j: (i, 0)),
                  pl.BlockSpec((2, k), lambda i, j: (0, 0)),
                  pl.BlockSpec((None, k, ns), lambda i, j: (j, 0, 0)),
                  pl.BlockSpec((None, 1, ns), lambda i, j: (j, 0, 0))],
        out_specs=[o_spec, pl.BlockSpec((tm, k), lambda i, j: (i, 0))],
        out_shape=[o_shape, jax.ShapeDtypeStruct((m, k), BF)],
        scratch_shapes=[pltpu.VMEM((tm, k), BF)], sem=("parallel", "arbitrary"), phases=phases)


def _seg_spec(tm, d, s):
    return pl.BlockSpec((tm, d), lambda i, s=s: (i, s))


def _prev_halo_spec(tm, d, s):
    hb = tm // HALO
    return pl.BlockSpec((HALO, d), lambda i, s=s: (jnp.maximum(i * hb - 1, 0), s))


def _next_halo_spec(tm, d, s, m):
    hb = tm // HALO
    last = m // HALO - 1
    return pl.BlockSpec((HALO, d), lambda i, s=s: (jnp.minimum((i + 1) * hb, last), s))


def _spatial_mix(wm_ref, src, dst, bias_ref, tm, d):
    for n in range(tm // GMLP_BLOCK):
        for g in range(d // GMLP_BLOCK):
            rs = slice(n * GMLP_BLOCK, (n + 1) * GMLP_BLOCK)
            cs = slice(g * GMLP_BLOCK, (g + 1) * GMLP_BLOCK)
            v = jnp.dot(wm_ref[g], src[rs, cs], preferred_element_type=F32)
            if bias_ref is not None:
                v = v + bias_ref[:, cs]
            dst[rs, cs] = v


def _mix_fwd(z, conv_a, lnv, wm, bias_full, *, name, tm=256, phases=()):
    m, d9 = z.shape
    d = d9 // 9
    tm = _row_tile(m, tm)
    grp = d // len(POOL_WINDOWS)

    def body(zb, zc, zx, zu, zv, zp, zc_h, zx_h, zp_h, ca_ref, lnv_ref, wm_ref, bias_ref,
             ua_ref, ub_ref, d_ref, ext, vn_s, mixed_s):
        i = pl.program_id(0)
        first = i == 0
        f32 = lambda r: r[...].astype(F32)
        pa = f32(zc) * f32(zx)
        ext[0:HALO, :] = jnp.where(first, 0.0, f32(zc_h) * f32(zx_h))
        ext[HALO:HALO + tm, :] = pa
        w = ca_ref[...]
        conv = w[0:1, :] * ext[pl.ds(HALO - 2, tm), :] + w[1:2, :] * ext[pl.ds(HALO - 1, tm), :] + w[2:3, :] * pa
        ua_ref[...] = (f32(zb) * conv).astype(BF)
        p = f32(zp)
        ext[0:HALO, :] = jnp.where(first, 0.0, f32(zp_h))
        ext[HALO:HALO + tm, :] = p
        t = (i * tm + lax.broadcasted_iota(jnp.int32, (tm, 1), 0) + 1).astype(F32)
        for k, win in enumerate(POOL_WINDOWS):
            cs = slice(k * grp, (k + 1) * grp)
            s = p[:, cs]
            for j in range(1, win):
                s = s + ext[pl.ds(HALO - j, tm), cs]
            d_ref[:, cs] = (s / jnp.minimum(t, float(win)) - p[:, cs]).astype(BF)
        gv = _gelu(f32(zv))
        vhat, _ = _ln_stats(gv)
        vn_s[...] = (vhat * lnv_ref[0:1, :] + lnv_ref[1:2, :]).astype(BF)
        _spatial_mix(wm_ref, vn_s, mixed_s, bias_ref, tm, d)
        ub_ref[...] = (_gelu(f32(zu)) * mixed_s[...]).astype(BF)

    full = lambda a: pl.BlockSpec(a.shape, lambda i: (0,) * a.ndim)
    out = jax.ShapeDtypeStruct((m, d), BF)
    o_spec = pl.BlockSpec((tm, d), lambda i: (i, 0))
    return _pcall(
        body, (z, z, z, z, z, z, z, z, z, conv_a, lnv, wm, bias_full), name=name, grid=(m // tm,),
        in_specs=[_seg_spec(tm, d, s) for s in range(6)] + [_prev_halo_spec(tm, d, s) for s in (1, 2, 5)]
        + [full(conv_a), full(lnv), full(wm), full(bias_full)],
        out_specs=[o_spec, o_spec, o_spec], out_shape=[out, out, out],
        scratch_shapes=[pltpu.VMEM((HALO + tm, d), F32), pltpu.VMEM((tm, d), BF), pltpu.VMEM((tm, d), F32)],
        sem=("arbitrary",), phases=phases)


def _pool_proj(dd, w_pool, *, dn, name, out_dtype=F32, tm=512):
    m, d = dd.shape
    ng, grp, _ = w_pool.shape
    tm = _row_tile(m, tm)
    return _matmul(
        dd, w_pool, dn=dn, grid=(m // tm, ng, 1), name=name,
        a_spec=pl.BlockSpec((tm, grp), lambda i, j, kk: (i, j)),
        b_spec=pl.BlockSpec((None, grp, grp), lambda i, j, kk: (j, 0, 0)),
        o_spec=pl.BlockSpec((tm, grp), lambda i, j, kk: (i, j)),
        out_shape=jax.ShapeDtypeStruct((m, d), out_dtype), acc_shape=(tm, grp))


def _merge(z, ya, yb, ycp, scale, *, name, tm=512):
    m, d = ya.shape
    tm = _row_tile(m, tm)

    def body(ga, gb, gc, ya_ref, yb_ref, yc_ref, sc_ref, o_ref):
        f32 = lambda r: r[...].astype(F32)
        o_ref[...] = (jax.nn.sigmoid(f32(ga)) * f32(ya_ref) + jax.nn.sigmoid(f32(gb)) * f32(yb_ref)
                      + jax.nn.sigmoid(f32(gc)) * (f32(yc_ref) * sc_ref[...])).astype(BF)

    row = pl.BlockSpec((tm, d), lambda i: (i, 0))
    return pl.pallas_call(
        body, name=name, grid=(m // tm,),
        in_specs=[_seg_spec(tm, d, 6), _seg_spec(tm, d, 7), _seg_spec(tm, d, 8), row, row, row,
                  pl.BlockSpec((1, d), lambda i: (0, 0))],
        out_specs=row, out_shape=jax.ShapeDtypeStruct((m, d), BF),
        compiler_params=_params(("parallel",)),
    )(z, z, z, ya, yb, ycp, scale)


def _resid_ln(xp, ys, vec, alpha, *, name, tm=512):
    m, d = xp.shape
    tm = _row_tile(m, tm)

    def body(xp_ref, ys_ref, v_ref, o_ref):
        xhat, _ = _ln_stats(alpha * xp_ref[...] + v_ref[0:1, :] * ys_ref[...])
        o_ref[...] = xhat * v_ref[1:2, :] + v_ref[2:3, :]

    row = pl.BlockSpec((tm, d), lambda i: (i, 0))
    return pl.pallas_call(
        body, name=name, grid=(m // tm,),
        in_specs=[row, row, pl.BlockSpec(vec.shape, lambda i: (0, 0))],
        out_specs=row, out_shape=jax.ShapeDtypeStruct((m, d), F32),
        compiler_params=_params(("parallel",)),
    )(xp, ys, vec)


def _ffn_fwd(up4, cw, cb, *, name, tm=512, phases=()):
    _, nj, m, fs = up4.shape
    tm = _row_tile(m, tm)
    hb = tm // HALO

    def body(up_ref, ah_ref, cw_ref, cb_ref, f_ref, ext):
        first = pl.program_id(1) == 0
        a = up_ref[0].astype(F32)
        ext[0:HALO, :] = jnp.where(first, 0.0, ah_ref[...].astype(F32))
        ext[HALO:HALO + tm, :] = a
        w = cw_ref[...]
        ca = (w[0:1, :] * ext[pl.ds(HALO - 2, tm), :] + w[1:2, :] * ext[pl.ds(HALO - 1, tm), :]
              + w[2:3, :] * a + cb_ref[...])
        f_ref[...] = (_gelu(ca) * up_ref[1].astype(F32)).astype(BF)

    return _pcall(
        body, (up4, up4, cw, cb), name=name, grid=(nj, m // tm),
        in_specs=[pl.BlockSpec((2, None, tm, fs), lambda j, i: (0, j, i, 0)),
                  pl.BlockSpec((None, None, HALO, fs), lambda j, i: (0, j, jnp.maximum(i * hb - 1, 0), 0)),
                  pl.BlockSpec((None, 3, fs), lambda j, i: (j, 0, 0)),
                  pl.BlockSpec((None, 1, fs), lambda j, i: (j, 0, 0))],
        out_specs=pl.BlockSpec((None, tm, fs), lambda j, i: (j, i, 0)),
        out_shape=jax.ShapeDtypeStruct((nj, m, fs), BF),
        scratch_shapes=[pltpu.VMEM((HALO + tm, fs), F32)], sem=("parallel", "arbitrary"), phases=phases)


def _down_proj(f4, wd4, *, name, tm=2048, phases=()):
    nj, m, fs = f4.shape
    d = wd4.shape[2]
    tm = _row_tile(m, tm)
    return _matmul(
        f4, wd4, dn=NN, grid=(m // tm, 1, nj), name=name,
        a_spec=pl.BlockSpec((None, tm, fs), lambda i, j, kk: (kk, i, 0)),
        b_spec=pl.BlockSpec((None, fs, d), lambda i, j, kk: (kk, 0, 0)),
        o_spec=pl.BlockSpec((tm, d), lambda i, j, kk: (i, 0)),
        out_shape=jax.ShapeDtypeStruct((m, d), F32), acc_shape=(tm, d), phases=phases)


def _loss_grad(y, tgt, *, name, tm=512):
    m, d = y.shape
    tm = _row_tile(m, tm)
    ni = m // tm

    def body(y_ref, t_ref, dy_ref, l_ref, acc):
        i = pl.program_id(0)
        e = y_ref[...] - t_ref[...]
        dy_ref[...] = e * (1.0 / d)
        part = jnp.sum((e * e).reshape(tm // 8, 8, d), axis=0)

        @pl.when(i == 0)
        def _():
            acc[...] = part

        @pl.when(i > 0)
        def _():
            acc[...] += part

        @pl.when(i == ni - 1)
        def _():
            l_ref[...] = jnp.full((8, 128), 0.5 / d, F32) * jnp.sum(acc[...])

    row = pl.BlockSpec((tm, d), lambda i: (i, 0))
    return pl.pallas_call(
        body, name=name, grid=(ni,), in_specs=[row, row],
        out_specs=[row, pl.BlockSpec((8, 128), lambda i: (0, 0))],
        out_shape=[jax.ShapeDtypeStruct((m, d), F32), jax.ShapeDtypeStruct((8, 128), F32)],
        scratch_shapes=[pltpu.VMEM((8, d), F32)],
        compiler_params=_params(("arbitrary",)),
    )(y, tgt)


def _resid_ln_bwd(dpart, dh, xmod, mvec, xp, ys, vec, alpha, *, name, tm=256, phases=()):
    m, d = dpart.shape
    tm = _row_tile(m, tm)
    has_dh = dh is not None
    has_ln = xp is not None

    def body(*refs):
        refs = list(refs)
        dpart_ref = refs.pop(0)
        if has_dh:
            dh_ref, xm_ref, mv_ref = refs.pop(0), refs.pop(0), refs.pop(0)
        if has_ln:
            xp_ref, ys_ref, v_ref = refs.pop(0), refs.pop(0), refs.pop(0)
            dys_ref, dxp_ref, red_ref = refs
        else:
            dx_ref, red_ref = refs
        i = pl.program_id(0)
        dtot = dpart_ref[...]
        rows = [jnp.zeros((1, d), F32)] * 5
        if has_dh:
            dhv = dh_ref[...]
            dtot = dtot + dhv * mv_ref[...]
            rows[0] = _rowsum(dhv * xm_ref[...])
            rows[1] = _rowsum(dhv)
        if has_ln:
            ys = ys_ref[...]
            gt = v_ref[0:1, :]
            xhat, rstd = _ln_stats(alpha * xp_ref[...] + gt * ys)
            rows[2] = _rowsum(dtot * xhat)
            rows[3] = _rowsum(dtot)
            dr = _ln_bwd(dtot, xhat, rstd, v_ref[1:2, :])
            rows[4] = _rowsum(dr * ys)
            dys_ref[...] = (dr * gt).astype(BF)
            dxp_ref[...] = alpha * dr
        else:
            dx_ref[...] = dtot
        red = jnp.concatenate(rows + [jnp.zeros((3, d), F32)], axis=0)

        @pl.when(i == 0)
        def _():
            red_ref[...] = red

        @pl.when(i > 0)
        def _():
            red_ref[...] += red

    row = pl.BlockSpec((tm, d), lambda i: (i, 0))
    vrow = lambda a: pl.BlockSpec(a.shape, lambda i: (0, 0))
    args, specs = [dpart], [row]
    if has_dh:
        args += [dh, xmod, mvec]
        specs += [row, row, vrow(mvec)]
    if has_ln:
        args += [xp, ys, vec]
        specs += [row, row, vrow(vec)]
        out_specs = [row, row, pl.BlockSpec((8, d), lambda i: (0, 0))]
        out_shape = [jax.ShapeDtypeStruct((m, d), BF), jax.ShapeDtypeStruct((m, d), F32),
                     jax.ShapeDtypeStruct((8, d), F32)]
    else:
        out_specs = [row, pl.BlockSpec((8, d), lambda i: (0, 0))]
        out_shape = [jax.ShapeDtypeStruct((m, d), F32), jax.ShapeDtypeStruct((8, d), F32)]
    return _pcall(body, args, name=name, grid=(m // tm,), in_specs=specs, out_specs=out_specs, out_shape=out_shape,
                  sem=("arbitrary",), phases=phases)


def _down_bwd(dy, wd4, *, name, tm=2048, phases=()):
    m, d = dy.shape
    nj, fs, _ = wd4.shape
    tm = _row_tile(m, tm)
    return _matmul(
        dy, wd4, dn=NT, grid=(m // tm, nj, 1), name=name,
        a_spec=pl.BlockSpec((tm, d), lambda i, j, kk: (i, 0)),
        b_spec=pl.BlockSpec((None, fs, d), lambda i, j, kk: (j, 0, 0)),
        o_spec=pl.BlockSpec((None, tm, fs), lambda i, j, kk: (j, i, 0)),
        out_shape=jax.ShapeDtypeStruct((nj, m, fs), ACT_DTYPE), acc_shape=(tm, fs), phases=phases)


def _tn_shards_lhs(f4, dy, *, name, tk=2048, phases=()):
    nj, m, fs = f4.shape
    d = dy.shape[1]
    tk = _row_tile(m, tk)
    return _matmul(
        f4, dy, dn=TN, grid=(nj, 1, m // tk), name=name,
        a_spec=pl.BlockSpec((None, tk, fs), lambda i, j, kk: (i, kk, 0)),
        b_spec=pl.BlockSpec((tk, d), lambda i, j, kk: (kk, 0)),
        o_spec=pl.BlockSpec((None, fs, d), lambda i, j, kk: (i, 0, 0)),
        out_shape=jax.ShapeDtypeStruct((nj, fs, d), GRAD_DTYPE), acc_shape=(fs, d), phases=phases)


def _tn_shards_rhs(h, d8, *, name, tk=2048, phases=()):
    m, k = h.shape
    nb, _, ns = d8.shape
    tk = _row_tile(m, tk)
    return _matmul(
        h, d8, dn=TN, grid=(nb, 1, m // tk), name=name,
        a_spec=pl.BlockSpec((tk, k), lambda i, j, kk: (kk, 0)),
        b_spec=pl.BlockSpec((None, tk, ns), lambda i, j, kk: (i, kk, 0)),
        o_spec=pl.BlockSpec((None, k, ns), lambda i, j, kk: (i, 0, 0)),
        out_shape=jax.ShapeDtypeStruct((nb, k, ns), GRAD_DTYPE), acc_shape=(k, ns), phases=phases)


def _tn_cols_rhs(h, dz, nb, *, name, tk=2048, phases=()):
    m, k = h.shape
    ns = dz.shape[1] // nb
    tk = _row_tile(m, tk)
    return _matmul(
        h, dz, dn=TN, grid=(nb, 1, m // tk), name=name,
        a_spec=pl.BlockSpec((tk, k), lambda i, j, kk: (kk, 0)),
        b_spec=pl.BlockSpec((tk, ns), lambda i, j, kk: (kk, i)),
        o_spec=pl.BlockSpec((None, k, ns), lambda i, j, kk: (i, 0, 0)),
        out_shape=jax.ShapeDtypeStruct((nb, k, ns), GRAD_DTYPE), acc_shape=(k, ns), phases=phases)


def _nt_shards(d8, w8, *, name, tm=2048, phases=()):
    nb, m, ns = d8.shape
    k = w8.shape[1]
    tm = _row_tile(m, tm)
    return _matmul(
        d8, w8, dn=NT, grid=(m // tm, 1, nb), name=name,
        a_spec=pl.BlockSpec((None, tm, ns), lambda i, j, kk: (kk, i, 0)),
        b_spec=pl.BlockSpec((None, k, ns), lambda i, j, kk: (kk, 0, 0)),
        o_spec=pl.BlockSpec((tm, k), lambda i, j, kk: (i, 0)),
        out_shape=jax.ShapeDtypeStruct((m, k), F32), acc_shape=(tm, k), phases=phases)


def _nt_cols(dz, w8, *, name, tm=2048, phases=(), tiles=None, into=None):
    m = dz.shape[0]
    nb, k, ns = w8.shape
    tm = _row_tile(m, tm)
    first, count = tiles if tiles is not None else (0, m // tm)
    return _matmul(
        dz, w8, dn=NT, grid=(count, 1, nb), name=name,
        a_spec=pl.BlockSpec((tm, ns), lambda i, j, kk: (i + first, kk)),
        b_spec=pl.BlockSpec((None, k, ns), lambda i, j, kk: (kk, 0, 0)),
        o_spec=pl.BlockSpec((tm, k), lambda i, j, kk: (i + first, 0)),
        out_shape=jax.ShapeDtypeStruct((m, k), F32), acc_shape=(tm, k), phases=phases, into=into)


def _tn_pool(dd, dyc, ng, *, name, tk=2048):
    m, d = dd.shape
    grp = d // ng
    tk = _row_tile(m, tk)
    return _matmul(
        dd, dyc, dn=TN, grid=(ng, 1, m // tk), name=name,
        a_spec=pl.BlockSpec((tk, grp), lambda i, j, kk: (kk, i)),
        b_spec=pl.BlockSpec((tk, grp), lambda i, j, kk: (kk, i)),
        o_spec=pl.BlockSpec((None, grp, grp), lambda i, j, kk: (i, 0, 0)),
        out_shape=jax.ShapeDtypeStruct((ng, grp, grp), GRAD_DTYPE), acc_shape=(grp, grp))


def _ffn_bwd(up4, df4, cw, cb, *, name, tm=256, phases=()):
    _, nj, m, fs = up4.shape
    tm = _row_tile(m, tm)
    hb = tm // HALO
    ni = m // tm
    last_hb = m // HALO - 1
    ext_rows = tm + 8

    def body(up_ref, ap_ref, un_ref, df_ref, dfn_ref, cw_ref, cb_ref, dup_ref, red_ref, ext, dca_s):
        i = pl.program_id(1)
        a = up_ref[0].astype(F32)
        g = up_ref[1].astype(F32)
        df = df_ref[...].astype(F32)
        ext[0:HALO, :] = jnp.where(i == 0, 0.0, ap_ref[...].astype(F32))
        ext[HALO:HALO + tm, :] = a
        ext[HALO + tm:2 * HALO + tm, :] = un_ref[0].astype(F32)
        w = cw_ref[...]
        w0, w1, w2 = w[0:1, :], w[1:2, :], w[2:3, :]
        a1 = ext[pl.ds(HALO - 1, ext_rows), :]
        a2 = ext[pl.ds(HALO - 2, ext_rows), :]
        cae = w0 * a2 + w1 * a1 + w2 * ext[pl.ds(HALO, ext_rows), :] + cb_ref[...]
        act, dact = _gelu_parts(cae)
        dfe = jnp.concatenate([df, dfn_ref[...].astype(F32)[0:8, :]], axis=0)
        ge = jnp.concatenate([g, un_ref[1].astype(F32)[0:8, :]], axis=0)
        row = lax.broadcasted_iota(jnp.int32, (ext_rows, 1), 0)
        dcae = jnp.where((row < tm) | (i < ni - 1), dfe * ge * dact, 0.0)
        dca_s[...] = dcae
        dca = dcae[0:tm, :]
        dup_a = w2 * dca + w1 * dca_s[pl.ds(1, tm), :] + w0 * dca_s[pl.ds(2, tm), :]
        dup_g = df * act[0:tm, :]
        dup_ref[0] = dup_a.astype(BF)
        dup_ref[1] = dup_g.astype(BF)
        red = jnp.concatenate([
            _rowsum(dca * a2[0:tm, :]), _rowsum(dca * a1[0:tm, :]), _rowsum(dca * a), _rowsum(dca),
            _rowsum(dup_a), _rowsum(dup_g), jnp.zeros((2, fs), F32)], axis=0)

        @pl.when(i == 0)
        def _():
            red_ref[...] = red

        @pl.when(i > 0)
        def _():
            red_ref[...] += red

    nxt = lambda j, i: jnp.minimum((i + 1) * hb, last_hb)
    return _pcall(
        body, (up4, up4, up4, df4, df4, cw, cb), name=name, grid=(nj, ni), sem=("parallel", "arbitrary"), phases=phases,
        in_specs=[pl.BlockSpec((2, None, tm, fs), lambda j, i: (0, j, i, 0)),
                  pl.BlockSpec((None, None, HALO, fs), lambda j, i: (0, j, jnp.maximum(i * hb - 1, 0), 0)),
                  pl.BlockSpec((2, None, HALO, fs), lambda j, i: (0, j, nxt(j, i), 0)),
                  pl.BlockSpec((None, tm, fs), lambda j, i: (j, i, 0)),
                  pl.BlockSpec((None, HALO, fs), lambda j, i: (j, nxt(j, i), 0)),
                  pl.BlockSpec((None, 3, fs), lambda j, i: (j, 0, 0)),
                  pl.BlockSpec((None, 1, fs), lambda j, i: (j, 0, 0))],
        out_specs=[pl.BlockSpec((2, None, tm, fs), lambda j, i: (0, j, i, 0)),
                   pl.BlockSpec((None, 8, fs), lambda j, i: (j, 0, 0))],
        out_shape=[jax.ShapeDtypeStruct((2, nj, m, fs), BF), jax.ShapeDtypeStruct((nj, 8, fs), F32)],
        scratch_shapes=[pltpu.VMEM((2 * HALO + tm, fs), F32), pltpu.VMEM((ext_rows, fs), F32)])


def _gate_bwd(dm, z, ya, yb, ycp, scale, *, name, tm=256):
    m, d = dm.shape
    tm = _row_tile(m, tm)

    def body(dm_ref, ga, gb, gc, ya_ref, yb_ref, yc_ref, sc_ref, dya_ref, dyb_ref, dyc_ref, dz_ref, red_ref):
        i = pl.program_id(0)
        f32 = lambda r: r[...].astype(F32)
        dmv = f32(dm_ref)
        sa, sb, sc = jax.nn.sigmoid(f32(ga)), jax.nn.sigmoid(f32(gb)), jax.nn.sigmoid(f32(gc))
        scale_v = sc_ref[...]
        ycp_v = f32(yc_ref)
        dya_ref[...] = (dmv * sa).astype(BF)
        dyb_ref[...] = (dmv * sb).astype(BF)
        dyc = dmv * sc
        dyc_ref[...] = (dyc * scale_v).astype(BF)
        dga = dmv * f32(ya_ref) * (sa * (1.0 - sa))
        dgb = dmv * f32(yb_ref) * (sb * (1.0 - sb))
        dgc = dmv * (ycp_v * scale_v) * (sc * (1.0 - sc))
        dz_ref[:, 0:d] = dga.astype(BF)
        dz_ref[:, d:2 * d] = dgb.astype(BF)
        dz_ref[:, 2 * d:3 * d] = dgc.astype(BF)
        red = jnp.concatenate([_rowsum(dyc * ycp_v), _rowsum(dga), _rowsum(dgb), _rowsum(dgc),
                               jnp.zeros((4, d), F32)], axis=0)

        @pl.when(i == 0)
        def _():
            red_ref[...] = red

        @pl.when(i > 0)
        def _():
            red_ref[...] += red

    row = pl.BlockSpec((tm, d), lambda i: (i, 0))
    obf = jax.ShapeDtypeStruct((m, d), BF)
    return pl.pallas_call(
        body, name=name, grid=(m // tm,),
        in_specs=[row, _seg_spec(tm, d, 6), _seg_spec(tm, d, 7), _seg_spec(tm, d, 8), row, row, row,
                  pl.BlockSpec((1, d), lambda i: (0, 0))],
        out_specs=[row, row, row, pl.BlockSpec((tm, 3 * d), lambda i: (i, 2)), pl.BlockSpec((8, d), lambda i: (0, 0))],
        out_shape=[obf, obf, obf, jax.ShapeDtypeStruct((m, 9 * d), BF), jax.ShapeDtypeStruct((8, d), F32)],
        compiler_params=_params(("arbitrary",)),
    )(dm, z, z, z, ya, yb, ycp, scale)


def _mix_bwd(dz, dua, dub, ddd, z, conv_a, lnv, wm, wmt, bias_full, mask, *, name, tm=128, phases=()):
    m, d = dua.shape
    tm = _row_tile(m, tm)
    ni = m // tm
    grp = d // len(POOL_WINDOWS)
    ng = d // GMLP_BLOCK
    ext_rows = tm + 8

    def body(dz_in, dua_ref, dub_ref, dd_ref, zb, zc, zx, zu, zv, zp, zc_h, zx_h, dua_n, zb_n, dd_n,
             ca_ref, lnv_ref, wm_ref, wmt_ref, bias_ref, mask_ref,
             dz_ref, red_ref, dws_ref, dbs_ref, ext, sh_s, vn_s, mixed_s, dmx_s, dvn_s, dbs_acc):
        del dz_in
        i = pl.program_id(0)
        rows = []
        f32 = lambda r: r[...].astype(F32)
        zbv, zcv, zxv = f32(zb), f32(zc), f32(zx)
        pa = zcv * zxv
        ext[0:HALO, :] = jnp.where(i == 0, 0.0, f32(zc_h) * f32(zx_h))
        ext[HALO:HALO + tm, :] = pa
        w = ca_ref[...]
        w0, w1, w2 = w[0:1, :], w[1:2, :], w[2:3, :]
        p1 = ext[pl.ds(HALO - 1, tm), :]
        p2 = ext[pl.ds(HALO - 2, tm), :]
        conv = w0 * p2 + w1 * p1 + w2 * pa
        duav = f32(dua_ref)
        dzb = duav * conv
        dca = duav * zbv
        dca_n = jnp.where(i < ni - 1, f32(dua_n)[0:8, :] * f32(zb_n)[0:8, :], 0.0)
        sh_s[0:tm, :] = dca
        sh_s[tm:tm + 8, :] = dca_n
        dpa = w2 * dca + w1 * sh_s[pl.ds(1, tm), :] + w0 * sh_s[pl.ds(2, tm), :]
        dzc = dpa * zxv
        dzx = dpa * zcv
        dz_ref[:, 0:d] = dzb.astype(BF)
        dz_ref[:, d:2 * d] = dzc.astype(BF)
        dz_ref[:, 2 * d:3 * d] = dzx.astype(BF)
        rows += [_rowsum(dzb), _rowsum(dzc), _rowsum(dzx)]
        dconv = [_rowsum(dca * p2), _rowsum(dca * p1), _rowsum(dca * pa)]
        zuv, zvv = f32(zu), f32(zv)
        gu, dgu_dz = _gelu_parts(zuv)
        gv, dgv_dz = _gelu_parts(zvv)
        vhat, rstd = _ln_stats(gv)
        gain = lnv_ref[0:1, :]
        vn_s[...] = (vhat * gain + lnv_ref[1:2, :]).astype(BF)
        _spatial_mix(wm_ref, vn_s, mixed_s, bias_ref, tm, d)
        dubv = f32(dub_ref)
        dzu = dubv * mixed_s[...] * dgu_dz
        dmixed = dubv * gu
        dmx_s[...] = dmixed.astype(BF)
        _spatial_mix(wmt_ref, dmx_s, dvn_s, None, tm, d)
        dvn = dvn_s[...]
        dzv = _ln_bwd(dvn, vhat, rstd, gain) * dgv_dz
        dz_ref[:, 3 * d:4 * d] = dzu.astype(BF)
        dz_ref[:, 4 * d:5 * d] = dzv.astype(BF)
        rows += [_rowsum(dzu), _rowsum(dzv)]
        dlnv = [_rowsum(dvn * vhat), _rowsum(dvn)]
        dbs_part = dmixed[0:GMLP_BLOCK, :]
        for n in range(1, tm // GMLP_BLOCK):
            dbs_part = dbs_part + dmixed[n * GMLP_BLOCK:(n + 1) * GMLP_BLOCK, :]
        ddv = f32(dd_ref)
        t = (i * tm + lax.broadcasted_iota(jnp.int32, (ext_rows + 8, 1), 0) + 1).astype(F32)
        dde = jnp.concatenate([ddv, jnp.where(i < ni - 1, f32(dd_n), 0.0)], axis=0)
        for k, win in enumerate(POOL_WINDOWS):
            cs = slice(k * grp, (k + 1) * grp)
            ext[0:tm + HALO, cs] = dde[:, cs] / jnp.minimum(t, float(win))
        dzp_parts = []
        for k, win in enumerate(POOL_WINDOWS):
            cs = slice(k * grp, (k + 1) * grp)
            s = ext[0:tm, cs]
            for j in range(1, win):
                s = s + ext[pl.ds(j, tm), cs]
            dzp_parts.append(s - ddv[:, cs])
        dzp = jnp.concatenate(dzp_parts, axis=1)
        dz_ref[:, 5 * d:6 * d] = dzp.astype(BF)
        rows += [_rowsum(dzp)]
        red = jnp.concatenate(rows + dconv + dlnv + [jnp.zeros((5, d), F32)], axis=0)

        @pl.when(i == 0)
        def _():
            red_ref[...] = red
            dbs_acc[...] = dbs_part
            dws_ref[...] = jnp.zeros_like(dws_ref)

        @pl.when(i > 0)
        def _():
            red_ref[...] += red
            dbs_acc[...] += dbs_part

        for n in range(tm // GMLP_BLOCK):
            for g in range(ng):
                rs = slice(n * GMLP_BLOCK, (n + 1) * GMLP_BLOCK)
                cs = slice(g * GMLP_BLOCK, (g + 1) * GMLP_BLOCK)
                dws_ref[g] += mask_ref[...] * lax.dot_general(
                    dmx_s[rs, cs], vn_s[rs, cs], (NT, ((), ())), preferred_element_type=F32)

        @pl.when(i == ni - 1)
        def _():
            lane = lax.broadcasted_iota(jnp.int32, (GMLP_BLOCK, GMLP_BLOCK), 1)
            out = jnp.zeros((GMLP_BLOCK, GMLP_BLOCK), F32)
            for g in range(ng):
                sg = jnp.sum(dbs_acc[:, g * GMLP_BLOCK:(g + 1) * GMLP_BLOCK], axis=1, keepdims=True)
                out = out + jnp.where(lane == g, sg, 0.0)
            dbs_ref[...] = out

    row = pl.BlockSpec((tm, d), lambda i: (i, 0))
    full = lambda a: pl.BlockSpec(a.shape, lambda i: (0,) * a.ndim)
    hb = tm // HALO
    last_hb = m // HALO - 1
    nrow = pl.BlockSpec((HALO, d), lambda i: (jnp.minimum((i + 1) * hb, last_hb), 0))
    return _pcall(
        body, (dz, dua, dub, ddd, z, z, z, z, z, z, z, z, dua, z, ddd, conv_a, lnv, wm, wmt, bias_full, mask),
        name=name, grid=(ni,), sem=("arbitrary",), aliases={0: 0}, phases=phases,
        in_specs=[pl.BlockSpec(memory_space=pl.ANY), row, row, row]
        + [_seg_spec(tm, d, s) for s in range(6)]
        + [_prev_halo_spec(tm, d, 1), _prev_halo_spec(tm, d, 2), nrow, _next_halo_spec(tm, d, 0, m), nrow]
        + [full(conv_a), full(lnv), full(wm), full(wmt), full(bias_full), full(mask)],
        out_specs=[pl.BlockSpec((tm, 6 * d), lambda i: (i, 0)), pl.BlockSpec((16, d), lambda i: (0, 0)),
                   full(wm), pl.BlockSpec((GMLP_BLOCK, GMLP_BLOCK), lambda i: (0, 0))],
        out_shape=[jax.ShapeDtypeStruct(dz.shape, BF), jax.ShapeDtypeStruct((16, d), F32),
                   jax.ShapeDtypeStruct(wm.shape, F32), jax.ShapeDtypeStruct((GMLP_BLOCK, GMLP_BLOCK), F32)],
        scratch_shapes=[pltpu.VMEM((2 * HALO + tm, d), F32), pltpu.VMEM((tm + 8, d), F32),
                        pltpu.VMEM((tm, d), BF), pltpu.VMEM((tm, d), F32), pltpu.VMEM((tm, d), BF),
                        pltpu.VMEM((tm, d), F32), pltpu.VMEM((GMLP_BLOCK, d), F32)])


REST = ("w_a_out", "w_b_out", "w_pool", "w_o", "w_up", "w_down")


def _remote(src, dst, ssem, rsem, k, to):
    return pltpu.make_async_remote_copy(src_ref=src, dst_ref=dst, send_sem=ssem.at[k], recv_sem=rsem.at[k],
                                        device_id=to, device_id_type=MESH)


def _gather_phase1(shards):
    n = len(shards)

    def build(ins, outs, ssem, rsem, lsem):
        x, y, c, chips = _place()
        me = 4 * x + 2 * y + c
        local = [pltpu.make_async_copy(ins[a], outs[a].at[:, me], lsem.at[a]) for a in range(n)]
        sends, recvs = [], []
        for j, (cx, cy) in enumerate(chips):
            for a in range(n):
                sends.append(_remote(ins[a], outs[a].at[:, me], ssem, rsem, 4 * a + 1 + j, (cx, cy, c)))
                recvs.append(_remote(ins[a], outs[a].at[:, 4 * cx + 2 * cy + c], ssem, rsem, 4 * a + 1 + j, (cx, cy, c)))
        for a in range(n):
            sends.append(_remote(ins[a], outs[a].at[:, me], ssem, rsem, 4 * a, (x, y, 1 - c)))
            recvs.append(_remote(ins[a], outs[a].at[:, 4 * x + 2 * y + 1 - c], ssem, rsem, 4 * a, (x, y, 1 - c)))
        return dict(start=local + sends, recv=recvs, send=sends, local=local)

    outs = [jax.ShapeDtypeStruct((s.shape[0], N_DEV) + s.shape[1:], s.dtype) for s in shards]
    return _Phase(shards, outs, {}, 4 * n, n, build)


def _gather_phase2(fulls):
    n = len(fulls)

    def build(ins, outs, ssem, rsem, lsem):
        x, y, c, chips = _place()
        sends, recvs = [], []
        for j, (cx, cy) in enumerate(chips):
            for a in range(n):
                mine, theirs = 4 * cx + 2 * cy + c, 4 * cx + 2 * cy + 1 - c
                sends.append(_remote(ins[a].at[:, mine], outs[a].at[:, mine], ssem, rsem, 3 * a + j, (x, y, 1 - c)))
                recvs.append(_remote(ins[a].at[:, theirs], outs[a].at[:, theirs], ssem, rsem, 3 * a + j, (x, y, 1 - c)))
        return dict(start=sends, recv=recvs, send=sends, local=[])

    outs = [jax.ShapeDtypeStruct(f.shape, f.dtype) for f in fulls]
    return _Phase(fulls, outs, {a: a for a in range(n)}, 3 * n, 0, build)


def _pair_phase(grads):
    n = len(grads)

    def build(ins, outs, ssem, rsem, lsem):
        x, y, c, _ = _place()
        cps = [_remote(ins[a].at[:, 2 * q + (1 - c)], outs[a].at[q], ssem, rsem, 4 * a + q, (x, y, 1 - c))
               for a in range(n) for q in range(4)]
        return dict(start=cps, recv=cps, send=cps, local=[])

    outs = [jax.ShapeDtypeStruct((4, g.shape[0]) + g.shape[2:], g.dtype) for g in grads]
    return _Phase(grads, outs, {}, 4 * n, 0, build)


def _chip_phase(bufs, accs, l, depth):
    n = len(bufs)
    has = accs is not None

    def build(ins, outs, ssem, rsem, lsem):
        x, y, c, chips = _place()
        myq = 2 * x + y
        local, sends, recvs = [], [], []
        for a in range(n):
            local.append(pltpu.make_async_copy(ins[a].at[myq], outs[a].at[myq, l], lsem.at[a]))
            for j, (cx, cy) in enumerate(chips):
                q = 2 * cx + cy
                sends.append(_remote(ins[a].at[q], outs[a].at[myq, l], ssem, rsem, 3 * a + j, (cx, cy, c)))
                recvs.append(_remote(ins[a].at[q], outs[a].at[q, l], ssem, rsem, 3 * a + j, (cx, cy, c)))
        return dict(start=local + sends, recv=recvs, send=sends, local=local)

    outs = [jax.ShapeDtypeStruct((4, depth) + b.shape[1:], b.dtype) for b in bufs]
    return _Phase(list(bufs) + (list(accs) if has else []), outs, {n + a: a for a in range(n)} if has else {},
                  3 * n, n, build)


def _grad_chunks(n, g):
    if n == "w_pool":
        return g.reshape(g.shape[0], N_DEV, g.shape[1] // N_DEV, g.shape[2])
    if n in ("w_in", "w_up"):
        return g[None]
    return g.reshape(1, N_DEV, -1, g.shape[-1])


class _ReduceScatter:
    def __init__(self, depth, cidx):
        self.depth, self.cidx, self.acc, self.count = depth, cidx, {}, 0
        self.small_gathered = None

    def pair(self, names, grads):
        return _pair_phase([_grad_chunks(n, grads[n]) for n in names])

    def sums(self, names, grads, phase):
        out = []
        for n, r1 in zip(names, phase.results):
            out.append(_pair_sum(_grad_chunks(n, grads[n]), r1, self.cidx, name="rs_sum_%d" % self.count))
            self.count += 1
        return out

    def chip(self, names, bufs, l):
        accs = [self.acc[n] for n in names] if names[0] in self.acc else None
        return _chip_phase(bufs, accs, l, self.depth)

    def done(self, names, phase):
        for n, r in zip(names, phase.results):
            self.acc[n] = r


def _rest_views(fulls, d):
    a_out, b_out, pool, o, up, down = fulls
    grp = d // len(POOL_WINDOWS)
    return dict(w_a_out=a_out.reshape(d, d), w_b_out=b_out.reshape(d, d), w_o=o.reshape(d, d),
                w_pool=pool.reshape(len(POOL_WINDOWS), grp, grp), w_up8=up[0],
                wd4=down.reshape(N_DEV // 2, -1, d))


class _GatherPlan:
    def __init__(self):
        self.jobs, self.part, self.full = [], {}, {}

    def add(self, key, shard, first, second):
        self.jobs.append((key, shard, first, second))

    def phases(self, name):
        j1 = [j for j in self.jobs if j[2] == name]
        j2 = [j for j in self.jobs if j[3] == name]
        tagged = []
        if j1:
            tagged.append((self.part, j1, _gather_phase1([j[1] for j in j1])))
        if j2:
            tagged.append((self.full, j2, _gather_phase2([self.part[j[0]] for j in j2])))
        return tagged

    @staticmethod
    def collect(tagged):
        for store, jobs, phase in tagged:
            for j, r in zip(jobs, phase.results):
                store[j[0]] = r


def _layer_fwd(x, w, alpha, tag, plan=None):
    d = x.shape[1]
    grp = d // len(POOL_WINDOWS)

    def carried(kernel, *args, name, **kw):
        tagged = plan.phases(name) if plan else []
        out = kernel(*args, name=name, phases=[t[2] for t in tagged], **kw)
        _GatherPlan.collect(tagged)
        return out

    def weight(n, shape):
        return plan.full[n + tag].reshape(shape) if plan else w[n]

    z, h = carried(_mod_matmul, x, w["mod1"], w["w_in8"], w["b_in8"], flat_out=True, name="in_proj" + tag)
    ua, ub, dd = carried(_mix_fwd, z, w["conv_a"], w["lnv"], w["wm"], w["bias_full"], name="mix_fwd" + tag)
    w["w_a_out"], w["w_b_out"], w["w_o"] = (weight(n, (d, d)) for n in ("w_a_out", "w_b_out", "w_o"))
    w["w_pool"] = weight("w_pool", (len(POOL_WINDOWS), grp, grp))
    w["w_up8"] = weight("w_up8", (N_DEV, d, -1))
    ya = _mm_rows(ua, w["w_a_out"], dn=NN, name="a_out" + tag, out_dtype=ACT_DTYPE)
    yb = _mm_rows(ub, w["w_b_out"], dn=NN, name="b_out" + tag, out_dtype=ACT_DTYPE)
    ycp = _pool_proj(dd, w["w_pool"], dn=NN, name="pool_proj" + tag, out_dtype=ACT_DTYPE)
    merged = _merge(z, ya, yb, ycp, w["pool_scale"], name="merge" + tag)
    o = _mm_rows(merged, w["w_o"], dn=NN, name="o_proj" + tag)
    x1 = _resid_ln(x, o, w["ln1"], alpha, name="ln1" + tag)
    up8, h2 = carried(_mod_matmul, x1, w["mod2"], w["w_up8"], w["b_up8"], flat_out=False, name="up_proj" + tag)
    up4 = up8.reshape((2, up8.shape[0] // 2) + up8.shape[1:])
    f4 = carried(_ffn_fwd, up4, w["cw"], w["cb"], name="ffn_fwd" + tag)
    w["wd4"] = weight("wd4", (N_DEV // 2, -1, d))
    y2 = carried(_down_proj, f4, w["wd4"], name="down_proj" + tag)
    x2 = _resid_ln(x1, y2, w["ln2"], alpha, name="ln2" + tag)
    saved = dict(x=x, z=z, h=h, ua=ua, ub=ub, dd=dd, ya=ya, yb=yb, ycp=ycp, merged=merged, o=o, x1=x1,
                 up4=up4, h2=h2, f4=f4, y2=y2)
    return x2, saved


def _layer_bwd(dpart, dh_above, xmod_above, m_above, w, s, alpha, tag, l=0, above=None, rs=None, upper_reds=()):
    first, rest = ("w_in",), REST
    ph = lambda p: [p] if p is not None else ()
    r1a = rs.pair(first, above) if above else None
    dy2, dx1p, red2 = _resid_ln_bwd(dpart, dh_above, xmod_above, m_above, s["x1"], s["y2"], w["ln2"], alpha,
                                    name="ln2_bwd" + tag, phases=ph(r1a))
    r1b = rs.pair(rest, above) if above else None
    df4 = _down_bwd(dy2, w["wd4"], name="down_bwd" + tag, phases=ph(r1b))
    if above:
        sb_a, sb_b = rs.sums(first, above, r1a), rs.sums(rest, above, r1b)
    gw_down4 = _tn_shards_lhs(s["f4"], dy2, name="gw_down" + tag)
    if above:
        bufs = dict(zip(first + rest, sb_a + sb_b))
        heavy = ("w_in", "w_up")
        light = tuple(n for n in rest if n not in heavy)
    r3a = rs.chip(heavy, [bufs[n] for n in heavy], l + 1) if above else None
    dup4, redf = _ffn_bwd(s["up4"], df4, w["cw"], w["cb"], name="ffn_bwd" + tag, phases=ph(r3a))
    dup8 = dup4.reshape((dup4.shape[0] * dup4.shape[1],) + dup4.shape[2:])
    r3b = None
    if above:
        rs.done(heavy, r3a)
        r3b = rs.chip(light, [bufs[n] for n in light], l + 1)
    gw_up8 = _tn_shards_lhs(dup8, s["h2"], name="gw_up" + tag, phases=ph(r3b))
    if above:
        rs.done(light, r3b)
    own = rs is not None and l == 0
    big = dict(w_up=gw_up8, w_down=gw_down4)
    early = ("w_down", "w_up")
    o1 = rs.pair(early, big) if own else None
    dh2 = _nt_shards(dup8, w["w_up8"], name="up_bwd" + tag, phases=ph(o1))
    if own:
        sb_o = rs.sums(early, big, o1)
    do, dxp, red1 = _resid_ln_bwd(dx1p, dh2, s["x1"], w["mod2"][0:1], s["x"], s["o"], w["ln1"], alpha,
                                  name="ln1_bwd" + tag)
    dm = _mm_rows(do, w["w_o"], dn=NT, name="o_bwd" + tag, out_dtype=ACT_DTYPE)
    big["w_o"] = _mm_tn(s["merged"], do, name="gw_o" + tag)
    dya, dyb, dyc, dz, redg = _gate_bwd(dm, s["z"], s["ya"], s["yb"], s["ycp"], w["pool_scale"], name="gate_bwd" + tag)
    dua = _mm_rows(dya, w["w_a_out"], dn=NT, name="a_out_bwd" + tag, out_dtype=ACT_DTYPE)
    dub = _mm_rows(dyb, w["w_b_out"], dn=NT, name="b_out_bwd" + tag, out_dtype=ACT_DTYPE)
    ddd = _pool_proj(dyc, w["w_pool"], dn=NT, name="pool_bwd" + tag, out_dtype=ACT_DTYPE)
    big["w_a_out"] = _mm_tn(s["ua"], dya, name="gw_a_out" + tag)
    big["w_b_out"] = _mm_tn(s["ub"], dyb, name="gw_b_out" + tag)
    big["w_pool"] = _tn_pool(s["dd"], dyc, w["w_pool"].shape[0], name="gw_pool" + tag)
    o3 = rs.chip(early, sb_o, l) if own else None
    dz, redm, dws, dbs = _mix_bwd(dz, dua, dub, ddd, s["z"], w["conv_a"], w["lnv"], w["wm"], w["wmt"],
                                  w["bias_full"], w["mask"], name="mix_bwd" + tag, phases=ph(o3))
    reds = dict(red2=red2, redf=redf, red1=red1, redg=redg, redm=redm, dws=dws, dbs=dbs)
    mid = ("w_o", "w_a_out", "w_b_out", "w_pool")
    o1b = sg1 = None
    if own:
        rs.done(early, o3)
        o1b = rs.pair(mid, big)
        sg1 = _gather_phase1([_small_payload([reds] + list(upper_reds))])
    big["w_in"] = _tn_cols_rhs(s["h"], dz, w["w_in8"].shape[0], name="gw_in" + tag,
                               phases=[o1b, sg1] if own else ())
    pending = None
    if own:
        sb_m = rs.sums(mid, big, o1b)
        o1c, o3b, sg2 = rs.pair(first, big), rs.chip(mid, sb_m, l), _gather_phase2(sg1.results)
        n_tiles = dz.shape[0] // _row_tile(dz.shape[0], 2048)
        head = max(n_tiles // 2, 1)
        dh = _nt_cols(dz, w["w_in8"], name="in_bwd" + tag + "_a", phases=[o1c, o3b, sg2], tiles=(0, head))
        rs.done(mid, o3b)
        rs.small_gathered = sg2.results[0]
        o3c = rs.chip(first, rs.sums(first, big, o1c), l)
        if n_tiles > head:
            dh = _nt_cols(dz, w["w_in8"], name="in_bwd" + tag + "_b", phases=[o3c], tiles=(head, n_tiles - head),
                          into=dh)
            rs.done(first, o3c)
        else:
            pending = (first, o3c)
    else:
        dh = _nt_cols(dz, w["w_in8"], name="in_bwd" + tag)
    return dxp, dh, big, reds, pending


def _local_step(x, tgt, ws, alpha, plan=None, rs=None):
    depth = len(ws)
    saved = []
    y = x
    for l in range(depth):
        if plan and l > 0:
            ws[l]["w_in8"] = plan.full["w_in8_l%d" % l][0]
        y, s = _layer_fwd(y, ws[l], alpha, "_l%d" % l, plan)
        saved.append(s)
    dpart, loss_blk = _loss_grad(y, tgt, name="loss_grad")
    dh = xmod = mvec = above = pending = None
    bigs, reds = [None] * depth, [None] * depth
    for l in reversed(range(depth)):
        dpart, dh, bigs[l], reds[l], pending = _layer_bwd(dpart, dh, xmod, mvec, ws[l], saved[l], alpha, "_l%d" % l,
                                                          l, above if rs else None, rs, reds[l + 1:])
        xmod, mvec, above = saved[l]["x"], ws[l]["mod1"][0:1], bigs[l]
    grad_x, red0 = _resid_ln_bwd(dpart, dh, xmod, mvec, None, None, None, alpha, name="in_bwd_tail",
                                 phases=[pending[1]] if pending else ())
    if pending:
        rs.done(*pending)
    d_ada = []
    for l in range(depth):
        below = red0 if l == 0 else reds[l - 1]["red2"]
        r1, r2 = reds[l]["red1"], reds[l]["red2"]
        d_ada.append(jnp.stack([below[1], below[0], r1[4], r1[1], r1[0], r2[4]]))
    return loss_blk, grad_x, bigs, reds, jnp.stack(d_ada)


def _small_grads(r):
    redm, redg, redf = r["redm"], r["redg"], r["redf"]
    ng = r["dws"].shape[0]
    return dict(
        b_in=jnp.concatenate([redm[0:6], redg[1:4]], axis=0).reshape(-1),
        conv_a=redm[6:9], ln_v_g=redm[9], ln_v_b=redm[10],
        w_spatial=r["dws"], b_spatial=r["dbs"][:, :ng].T,
        pool_scale=redg[0], ln1_g=r["red1"][2], ln1_b=r["red1"][3],
        b_up=jnp.concatenate([redf[:, 4, :].reshape(-1), redf[:, 5, :].reshape(-1)]),
        conv_ffn=jnp.transpose(redf[:, 0:3, :], (1, 0, 2)).reshape(3, -1), conv_ffn_b=redf[:, 3, :].reshape(-1),
        ln2_g=r["red2"][2], ln2_b=r["red2"][3])


def _small_payload(reds):
    smalls = [_small_grads(r) for r in reds]
    order = SMALL_REPLICATED + SMALL_SHARDED
    flat = jnp.concatenate([smalls[l][n].reshape(-1) for n in order for l in range(len(reds))])
    return _as_rows(flat)[None]


def _layer_weights(l, ada, conv_a, conv_ffn, p):
    sh1, sc1, gt1, sh2, sc2, gt2 = (ada[l, k][None, :] for k in range(6))
    nb = N_DEV
    fs = p["b_up"].shape[1] // nb
    nj = nb // 2
    pos = jnp.arange(GMLP_BLOCK)
    allowed = (pos[None, :] // CHUNK) <= (pos[:, None] // CHUNK)
    wmask = jnp.where(allowed[None], p["w_spatial"][l], 0.0)
    return dict(
        mod1=jnp.concatenate([1.0 + sc1, sh1]), mod2=jnp.concatenate([1.0 + sc2, sh2]),
        ln1=jnp.concatenate([gt1, p["ln1_g"][l][None], p["ln1_b"][l][None]]),
        ln2=jnp.concatenate([gt2, p["ln2_g"][l][None], p["ln2_b"][l][None]]),
        b_in8=p["b_in"][l].reshape(N_DEV, 1, -1), b_up8=p["b_up"][l].reshape(nb, 1, fs),
        conv_a=conv_a[l], lnv=jnp.stack([p["ln_v_g"][l], p["ln_v_b"][l]]),
        wm=wmask.astype(BF), wmt=jnp.transpose(wmask, (0, 2, 1)).astype(BF),
        bias_full=jnp.repeat(p["b_spatial"][l].T, GMLP_BLOCK, axis=1), mask=allowed.astype(F32),
        pool_scale=p["pool_scale"][l][None],
        cw=jnp.transpose(conv_ffn[l].reshape(3, nj, fs), (1, 0, 2)), cb=p["conv_ffn_b"][l].reshape(nj, 1, fs))


ANY = pl.BlockSpec(memory_space=pl.ANY)


def _place():
    x, y, c = lax.axis_index("x"), lax.axis_index("y"), lax.axis_index("c")
    chips = [(1 - x, y), (x, 1 - y), (1 - x, 1 - y)]
    return x, y, c, chips


def _allgather_vmem(xs, *, name):
    r, cdim = xs.shape

    def body(x_ref, out_ref, send_sems, recv_sems, local_sem):
        x, y, c, chips = _place()
        me, sibling = (x, y, c), (x, y, 1 - c)

        def rows(px, py, pc):
            return out_ref.at[pl.ds((4 * px + 2 * py + pc) * r, r), :]

        def copy(k, block, to, src=None):
            return pltpu.make_async_remote_copy(
                src_ref=rows(*block) if src is None else src, dst_ref=rows(*block),
                send_sem=send_sems.at[k], recv_sem=recv_sems.at[k], device_id=to, device_id_type=MESH)

        mine = pltpu.make_async_copy(x_ref, rows(*me), local_sem)
        mine.start()
        first = [copy(0, me, sibling, src=x_ref)]
        first += [copy(1 + j, me, (*chip, c), src=x_ref) for j, chip in enumerate(chips)]
        for cp in first:
            cp.start()
        passed = [copy(4 + j, (*chip, c), sibling) for j, chip in enumerate(chips)]
        for j, chip in enumerate(chips):
            copy(1 + j, (*chip, c), me).wait_recv()
            passed[j].start()
        copy(0, sibling, me).wait_recv()
        for j, chip in enumerate(chips):
            copy(4 + j, (*chip, 1 - c), me).wait_recv()
        for cp in first + passed:
            cp.wait_send()
        mine.wait()

    return pl.pallas_call(
        body, name=name, out_shape=jax.ShapeDtypeStruct((N_DEV * r, cdim), xs.dtype),
        in_specs=[pl.BlockSpec(memory_space=pltpu.VMEM)], out_specs=pl.BlockSpec(memory_space=pltpu.VMEM),
        scratch_shapes=[pltpu.SemaphoreType.DMA((7,)), pltpu.SemaphoreType.DMA((7,)), pltpu.SemaphoreType.DMA],
        compiler_params=_params(),
    )(xs)


def _gather_weights(shards, *, name):
    n = len(shards)

    def body(*refs):
        ins, outs = refs[:n], refs[n:2 * n]
        send_sems, recv_sems, local_sems = refs[2 * n:]
        x, y, c, chips = _place()
        me, sibling = (x, y, c), (x, y, 1 - c)

        def slot(a, px, py, pc):
            return outs[a].at[:, 4 * px + 2 * py + pc]

        def copy(a, k, block, to, src=None):
            return pltpu.make_async_remote_copy(
                src_ref=slot(a, *block) if src is None else src, dst_ref=slot(a, *block),
                send_sem=send_sems.at[7 * a + k], recv_sem=recv_sems.at[7 * a + k], device_id=to,
                device_id_type=MESH)

        mine = [pltpu.make_async_copy(ins[a], slot(a, *me), local_sems.at[a]) for a in range(n)]
        for cp in mine:
            cp.start()
        first = []
        for j, chip in enumerate(chips):
            first += [copy(a, 1 + j, me, (*chip, c), src=ins[a]) for a in range(n)]
        first += [copy(a, 0, me, sibling, src=ins[a]) for a in range(n)]
        for cp in first:
            cp.start()
        passed = []
        for j, chip in enumerate(chips):
            for a in range(n):
                copy(a, 1 + j, (*chip, c), me).wait_recv()
                fwd = copy(a, 4 + j, (*chip, c), sibling)
                fwd.start()
                passed.append(fwd)
        for a in range(n):
            copy(a, 0, sibling, me).wait_recv()
        for j, chip in enumerate(chips):
            for a in range(n):
                copy(a, 4 + j, (*chip, 1 - c), me).wait_recv()
        for cp in first + passed:
            cp.wait_send()
        for cp in mine:
            cp.wait()

    out_shape = [jax.ShapeDtypeStruct((s.shape[0], N_DEV) + s.shape[1:], s.dtype) for s in shards]
    return pl.pallas_call(
        body, name=name, out_shape=out_shape, in_specs=[ANY] * n, out_specs=[ANY] * n,
        scratch_shapes=[pltpu.SemaphoreType.DMA((7 * n,)), pltpu.SemaphoreType.DMA((7 * n,)),
                        pltpu.SemaphoreType.DMA((n,))],
        compiler_params=_params(),
    )(*shards)


def _pick_tile(r, cap):
    best = None
    for t in range(8, min(r, cap) + 1, 8):
        if r % t == 0:
            best = t
    return best if best is not None else r


def _pair_sum(g, r1, cidx, *, name):
    p, _, r, cdim = g.shape
    tr = _pick_tile(r, 256)

    def body(c_ref, g_ref, r_ref, o_ref):
        del c_ref
        o_ref[...] = (g_ref[...].astype(F32) + r_ref[...].astype(F32)).astype(BF)

    grid_spec = pltpu.PrefetchScalarGridSpec(
        num_scalar_prefetch=1, grid=(4, r // tr),
        in_specs=[pl.BlockSpec((p, None, tr, cdim), lambda q, i, c: (0, 2 * q + c[0], i, 0)),
                  pl.BlockSpec((None, p, tr, cdim), lambda q, i, c: (q, 0, i, 0))],
        out_specs=pl.BlockSpec((None, p, tr, cdim), lambda q, i, c: (q, 0, i, 0)))
    return pl.pallas_call(
        body, name=name, grid_spec=grid_spec, out_shape=jax.ShapeDtypeStruct((4, p, r, cdim), BF),
        compiler_params=_params(("arbitrary", "arbitrary")),
    )(cidx, g, r1)


def _ada_fwd(c_all, w_ada, *, name):
    depth, d, ns = w_ada.shape
    nb = c_all.shape[0]

    def body(c_ref, w_ref, o_ref):
        cv = c_ref[...]
        act = cv * jax.nn.sigmoid(cv)
        o_ref[...] = jnp.dot(act, w_ref[...], preferred_element_type=F32, precision=lax.Precision.HIGHEST)

    return pl.pallas_call(
        body, name=name, grid=(depth,),
        in_specs=[pl.BlockSpec((nb, d), lambda l: (0, 0)), pl.BlockSpec((None, d, ns), lambda l: (l, 0, 0))],
        out_specs=pl.BlockSpec((None, nb, ns), lambda l: (l, 0, 0)),
        out_shape=jax.ShapeDtypeStruct((depth, nb, ns), F32), compiler_params=_params(("parallel",)),
    )(c_all, w_ada)


def _ada_bwd(ct, dmine, dall, *, name):
    depth, nb, ns = dmine.shape
    d = ct.shape[0]

    def body(ct_ref, dm_ref, da_ref, gw_ref, gb_ref):
        cv = ct_ref[...]
        act = cv * jax.nn.sigmoid(cv)
        gw_ref[...] = jnp.dot(act, dm_ref[...], preferred_element_type=F32, precision=lax.Precision.HIGHEST)
        s = da_ref[0]
        for b in range(1, nb):
            s = s + da_ref[b]
        gb_ref[...] = s

    return pl.pallas_call(
        body, name=name, grid=(depth,),
        in_specs=[pl.BlockSpec((d, nb), lambda l: (0, 0)), pl.BlockSpec((None, nb, ns), lambda l: (l, 0, 0)),
                  pl.BlockSpec(dall.shape, lambda l: (0, 0, 0))],
        out_specs=[pl.BlockSpec((None, d, ns), lambda l: (l, 0, 0)), pl.BlockSpec(dall.shape[1:], lambda l: (0, 0))],
        out_shape=[jax.ShapeDtypeStruct((depth, d, ns), F32), jax.ShapeDtypeStruct(dall.shape[1:], F32)],
        compiler_params=_params(("arbitrary",)),
    )(ct, dmine, dall)


def _sum_parts(parts, *, name):
    p, r, cdim = parts.shape
    tr = _pick_tile(r, 512)

    def body(p_ref, o_ref):
        s = p_ref[0]
        for k in range(1, p):
            s = s + p_ref[k]
        o_ref[...] = s

    return pl.pallas_call(
        body, name=name, grid=(r // tr,),
        in_specs=[pl.BlockSpec((p, tr, cdim), lambda i: (0, i, 0))], out_specs=pl.BlockSpec((tr, cdim), lambda i: (i, 0)),
        out_shape=jax.ShapeDtypeStruct((r, cdim), F32), compiler_params=_params(("parallel",)),
    )(parts)


def _adamw(parts, w, m, v, *, name):
    p, depth, r, cdim = parts.shape
    tr = _pick_tile(r, 256)

    def body(p_ref, w_ref, m_ref, v_ref, g_out, d_out, m_out, v_out):
        g = p_ref[0].astype(F32)
        for k in range(1, p):
            g = g + p_ref[k].astype(F32)
        m2 = ADAM_B1 * m_ref[...] + (1.0 - ADAM_B1) * g
        v2 = ADAM_B2 * v_ref[...] + (1.0 - ADAM_B2) * (g * g)
        m_hat = m2 / (1.0 - ADAM_B1 ** ADAM_STEP)
        v_hat = v2 / (1.0 - ADAM_B2 ** ADAM_STEP)
        g_out[...] = g
        d_out[...] = -ADAM_LR * (m_hat / (jnp.sqrt(v_hat) + ADAM_EPS) + ADAM_WD * w_ref[...])
        m_out[...] = m2
        v_out[...] = v2

    blk = pl.BlockSpec((None, tr, cdim), lambda l, i: (l, i, 0))
    out = jax.ShapeDtypeStruct((depth, r, cdim), F32)
    return pl.pallas_call(
        body, name=name, grid=(depth, r // tr),
        in_specs=[pl.BlockSpec((p, None, tr, cdim), lambda l, i: (0, l, i, 0)), blk, blk, blk],
        out_specs=[blk, blk, blk, blk], out_shape=[out, out, out, out],
        compiler_params=_params(("parallel", "parallel")),
    )(parts, w, m, v)


BIG = ("w_in", "w_a_out", "w_b_out", "w_pool", "w_o", "w_up", "w_down")
SMALL_REPLICATED = ("b_in", "ln_v_g", "ln_v_b", "w_spatial", "b_spatial", "pool_scale", "ln1_g", "ln1_b", "b_up",
                    "conv_ffn_b", "ln2_g", "ln2_b")
SMALL_SHARDED = ("conv_a", "conv_ffn")
WEIGHTS = ("w_ada", "b_ada", "w_in", "b_in", "conv_a", "w_a_out", "ln_v_g", "ln_v_b", "w_spatial", "b_spatial",
           "w_b_out", "w_pool", "pool_scale", "w_o", "ln1_g", "ln1_b", "w_up", "b_up", "conv_ffn", "conv_ffn_b",
           "w_down", "ln2_g", "ln2_b")
LANES = 128


def _as_rows(flat, mult=8):
    n = flat.shape[0]
    pad = (-n) % (LANES * mult)
    if pad:
        flat = jnp.concatenate([flat, jnp.zeros((pad,), flat.dtype)])
    return flat.reshape(-1, LANES)


def _shard3(a):
    return a.reshape((-1,) + a.shape[-2:])


def kernel(x, c, w_ada, b_ada, w_in, b_in, conv_a, w_a_out, ln_v_g, ln_v_b, w_spatial, b_spatial, w_b_out, w_pool, pool_scale, w_o, ln1_g, ln1_b, w_up, b_up, conv_ffn, conv_ffn_b, w_down, ln2_g, ln2_b, loss_target, m_w_ada, m_b_ada, m_w_in, m_b_in, m_conv_a, m_w_a_out, m_ln_v_g, m_ln_v_b, m_w_spatial, m_b_spatial, m_w_b_out, m_w_pool, m_pool_scale, m_w_o, m_ln1_g, m_ln1_b, m_w_up, m_b_up, m_conv_ffn, m_conv_ffn_b, m_w_down, m_ln2_g, m_ln2_b, v_w_ada, v_b_ada, v_w_in, v_b_in, v_conv_a, v_w_a_out, v_ln_v_g, v_ln_v_b, v_w_spatial, v_b_spatial, v_w_b_out, v_w_pool, v_pool_scale, v_w_o, v_ln1_g, v_ln1_b, v_w_up, v_b_up, v_conv_ffn, v_conv_ffn_b, v_w_down, v_ln2_g, v_ln2_b):
    p = dict(locals())
    depth, d = w_in.shape[0], w_in.shape[1]
    alpha = (2 * depth) ** 0.25
    me = 4 * lax.axis_index("x") + 2 * lax.axis_index("y") + lax.axis_index("c")
    cidx = lax.axis_index("c").astype(jnp.int32).reshape(1)

    n_ca, n_cf = conv_a.size, conv_ffn.size
    packed = _as_rows(jnp.concatenate([c.reshape(-1), conv_a.reshape(-1), conv_ffn.reshape(-1)]))
    got = _allgather_vmem(packed, name="gather_cond").reshape(N_DEV, -1)
    c_all = got[:, :d]
    ct = c_all.T
    conv_a_full = jnp.transpose(got[:, d:d + n_ca].reshape((N_DEV,) + conv_a.shape), (1, 2, 0, 3)).reshape(depth, 3, -1)
    conv_ffn_full = jnp.transpose(got[:, d + n_ca:d + n_ca + n_cf].reshape((N_DEV,) + conv_ffn.shape),
                                  (1, 2, 0, 3)).reshape(depth, 3, -1)

    ns_ada = w_ada.shape[2]
    ada_part = _ada_fwd(c_all, w_ada, name="ada_fwd")
    ada_all = _allgather_vmem(_as_rows(ada_part.reshape(-1)), name="gather_ada")
    ada_all = ada_all.reshape(N_DEV, depth, N_DEV, ns_ada)
    ada_mine = lax.dynamic_index_in_dim(ada_all, me, axis=2, keepdims=False)
    ada = jnp.transpose(ada_mine, (1, 0, 2)).reshape(depth, -1) + b_ada
    ada = ada.reshape(depth, 6, d)

    shards = [{n: _shard3(p[n][l].astype(BF)) for n in BIG} for l in range(depth)]
    ws = [_layer_weights(l, ada, conv_a_full, conv_ffn_full, p) for l in range(depth)]
    ws[0]["w_in8"] = _gather_weights([shards[0]["w_in"]], name="gather_w_in0")[0][0]
    plan = _GatherPlan()
    four = ("w_a_out", "w_b_out", "w_pool", "w_o")
    for l in range(depth):
        t, prev, sh = "_l%d" % l, "_l%d" % (l - 1), shards[l]
        if l == 0:
            for n in four:
                plan.add(n + t, sh[n], "in_proj" + t, "mix_fwd" + t)
            plan.add("w_up8" + t, sh["w_up"], "in_proj" + t, "mix_fwd" + t)
            plan.add("wd4" + t, sh["w_down"], "mix_fwd" + t, "up_proj" + t)
        else:
            plan.add("w_in8" + t, sh["w_in"], "ffn_fwd" + prev, "down_proj" + prev)
            for n in four:
                plan.add(n + t, sh[n], "up_proj" + prev, "ffn_fwd" + prev)
            plan.add("w_up8" + t, sh["w_up"], "in_proj" + t, "mix_fwd" + t)
            plan.add("wd4" + t, sh["w_down"], "in_proj" + t, "mix_fwd" + t)

    rs = _ReduceScatter(depth, cidx)
    loss_blk, grad_x, bigs, reds, d_ada = _local_step(x[0], loss_target[0], ws, alpha, plan, rs)
    loss = lax.psum(loss_blk[0, 0], ("x", "y", "c"))

    dada_all = _allgather_vmem(_as_rows(d_ada.reshape(-1)), name="gather_dada")
    dada_all = dada_all.reshape(N_DEV, -1, LANES)
    dflat = dada_all.reshape(N_DEV, depth, 6 * d)
    dmine = lax.dynamic_slice_in_dim(dflat, me * ns_ada, ns_ada, axis=2)
    gw_ada, gb_rows = _ada_bwd(ct, jnp.transpose(dmine, (1, 0, 2)), dada_all, name="ada_bwd")
    gb_ada = gb_rows.reshape(-1)[:depth * 6 * d].reshape(depth, 6 * d)

    out = {}
    for n in BIG:
        parts = rs.acc[n]
        view = (lambda a: jnp.swapaxes(a, 1, 2)) if n == "w_up" else (lambda a: a)
        shard_shape = view(p[n]).shape
        w3 = view(p[n]).reshape(depth, -1, shard_shape[-1])
        parts4 = parts.reshape((4,) + w3.shape)
        res = _adamw(parts4, w3, view(p["m_" + n]).reshape(w3.shape), view(p["v_" + n]).reshape(w3.shape),
                     name="adamw_" + n)
        out[n] = [view(r.reshape(shard_shape)) for r in res]
    out["w_ada"] = _adamw(gw_ada[None], w_ada, m_w_ada, v_w_ada, name="adamw_w_ada")

    order = SMALL_REPLICATED + SMALL_SHARDED
    n_rep = sum(p[n].size for n in SMALL_REPLICATED)
    n_pay = n_rep + N_DEV * (conv_a.size + conv_ffn.size)
    gsum = _sum_parts(rs.small_gathered.reshape(N_DEV, -1, LANES), name="sum_small").reshape(-1)[:n_pay]
    ga_full = gsum[n_rep:n_rep + depth * 3 * d].reshape(depth, 3, d)
    gf_full = gsum[n_rep + depth * 3 * d:].reshape(depth, 3, -1)
    ca_w, cf_w = conv_a.shape[2], conv_ffn.shape[2]
    g_ca = lax.dynamic_slice_in_dim(ga_full, me * ca_w, ca_w, axis=2)
    g_cf = lax.dynamic_slice_in_dim(gf_full, me * cf_w, cf_w, axis=2)
    names = ("b_ada",) + order
    gflat = _as_rows(jnp.concatenate([gb_ada.reshape(-1), gsum[:n_rep], g_ca.reshape(-1), g_cf.reshape(-1)]))
    pack = lambda pre: _as_rows(jnp.concatenate([p[pre + n].reshape(-1) for n in names]))
    res = _adamw(gflat[None, None], pack("")[None], pack("m_")[None], pack("v_")[None], name="adamw_small")
    off = 0
    for n in names:
        size = p[n].size
        out[n] = [r.reshape(-1)[off:off + size].reshape(p[n].shape) for r in res]
        off += size

    return (loss, grad_x[None]) + tuple(out[n][k] for k in range(4) for n in WEIGHTS)
```

```python
import functools

import jax
import jax.numpy as jnp
from jax import lax
from jax.experimental import pallas as pl
from jax.experimental.pallas import tpu as pltpu

F32 = jnp.float32
BF = jnp.bfloat16
MESH = pl.DeviceIdType.MESH

LN_EPS = 1e-5
POOL_WINDOWS = (2, 4, 8, 16)
GMLP_BLOCK = 128
CHUNK = 64
HALO = 16
ADAM_LR, ADAM_B1, ADAM_B2, ADAM_EPS, ADAM_WD, ADAM_STEP = 0.001, 0.9, 0.999, 1e-08, 0.01, 10
N_DEV = 8
VMEM_LIMIT = 56 * 1024 * 1024

GRAD_DTYPE = BF
ACT_DTYPE = BF

NN = ((1,), (0,))
NT = ((1,), (1,))
TN = ((0,), (0,))


def _params(sem=None, vmem=VMEM_LIMIT, **kw):
    if sem is not None:
        kw["dimension_semantics"] = sem
    return pltpu.CompilerParams(vmem_limit_bytes=vmem, **kw)


class _Phase:
    def __init__(self, ins, out_shapes, aliases, n_remote, n_local, build):
        self.ins, self.out_shapes, self.aliases = list(ins), list(out_shapes), dict(aliases)
        self.n_remote, self.n_local, self.build = n_remote, n_local, build
        self.results = None


def _pcall(body, args, *, name, grid, in_specs, out_specs, out_shape, scratch_shapes=(), sem=None, aliases=None,
           phases=()):
    aliases = dict(aliases or {})
    if not phases:
        return pl.pallas_call(
            body, name=name, grid=grid, in_specs=list(in_specs), out_specs=out_specs, out_shape=out_shape,
            scratch_shapes=list(scratch_shapes), input_output_aliases=aliases, compiler_params=_params(sem),
        )(*args)
    single = not isinstance(out_shape, (list, tuple))
    o_specs = [out_specs] if single else list(out_specs)
    o_shapes = [out_shape] if single else list(out_shape)
    n_in, n_out, n_scr = len(args), len(o_shapes), len(scratch_shapes)
    ex_args, ex_out, sems = [], [], []
    for ph in phases:
        for src, dst in ph.aliases.items():
            aliases[n_in + len(ex_args) + src] = n_out + len(ex_out) + dst
        ex_args += ph.ins
        ex_out += ph.out_shapes
        sems += [pltpu.SemaphoreType.DMA((max(ph.n_remote, 1),)), pltpu.SemaphoreType.DMA((max(ph.n_remote, 1),)),
                 pltpu.SemaphoreType.DMA((max(ph.n_local, 1),))]

    def wrapped(*refs):
        pos = n_in
        ph_in = []
        for ph in phases:
            ph_in.append(refs[pos:pos + len(ph.ins)])
            pos += len(ph.ins)
        base_out = refs[pos:pos + n_out]
        pos += n_out
        ph_out = []
        for ph in phases:
            ph_out.append(refs[pos:pos + len(ph.out_shapes)])
            pos += len(ph.out_shapes)
        base_scr = refs[pos:pos + n_scr]
        ph_sems = refs[pos + n_scr:]
        first = last = None
        for ax, n in enumerate(grid):
            pid = pl.program_id(ax)
            first = (pid == 0) if first is None else first & (pid == 0)
            last = (pid == n - 1) if last is None else last & (pid == n - 1)

        def ops(k):
            return phases[k].build(ph_in[k], ph_out[k], *ph_sems[3 * k:3 * k + 3])

        @pl.when(first)
        def _():
            for k in range(len(phases)):
                for cp in ops(k)["start"]:
                    cp.start()

        body(*refs[:n_in], *base_out, *base_scr)

        @pl.when(last)
        def _():
            for k in range(len(phases)):
                o = ops(k)
                for cp in o["recv"]:
                    cp.wait_recv()
                for cp in o["send"]:
                    cp.wait_send()
                for cp in o["local"]:
                    cp.wait()

    hbm = pl.BlockSpec(memory_space=pl.ANY)
    res = pl.pallas_call(
        wrapped, name=name, grid=grid, in_specs=list(in_specs) + [hbm] * len(ex_args),
        out_specs=o_specs + [hbm] * len(ex_out), out_shape=o_shapes + ex_out,
        scratch_shapes=list(scratch_shapes) + sems, input_output_aliases=aliases,
        compiler_params=_params(("arbitrary",) * len(grid)),
    )(*args, *ex_args)
    pos = n_out
    for ph in phases:
        ph.results = list(res[pos:pos + len(ph.out_shapes)])
        pos += len(ph.out_shapes)
    return res[0] if single else list(res[:n_out])


def _gelu_parts(x):
    k = 0.7978845608028654
    x2 = x * x
    t = jnp.tanh(k * (x + 0.044715 * (x2 * x)))
    cdf = 0.5 * (1.0 + t)
    dcdf = 0.5 * (1.0 - t * t) * (k * (1.0 + 3.0 * 0.044715 * x2))
    return x * cdf, cdf + x * dcdf


def _gelu(x):
    t = jnp.tanh(0.7978845608028654 * (x + 0.044715 * (x * x * x)))
    return x * (0.5 * (1.0 + t))


def _rowsum(v):
    return jnp.sum(v, axis=0, keepdims=True)


def _ln_stats(r):
    mu = jnp.mean(r, axis=-1, keepdims=True)
    xc = r - mu
    var = jnp.mean(xc * xc, axis=-1, keepdims=True)
    rstd = lax.rsqrt(var + LN_EPS)
    return xc * rstd, rstd


def _ln_bwd(dy, xhat, rstd, gain):
    dxh = dy * gain
    m1 = jnp.mean(dxh, axis=-1, keepdims=True)
    m2 = jnp.mean(dxh * xhat, axis=-1, keepdims=True)
    return rstd * (dxh - m1 - xhat * m2)


def _matmul(a, b, *, dn, grid, a_spec, b_spec, o_spec, out_shape, acc_shape, name, phases=(), into=None):
    nk = grid[2]
    direct = out_shape.dtype == F32

    def body(a_ref, b_ref, *rest):
        o_ref, scratch = (rest[1], rest[2:]) if into is not None else (rest[0], rest[1:])
        prod = lax.dot_general(a_ref[...], b_ref[...], (dn, ((), ())), preferred_element_type=F32)
        if nk == 1:
            o_ref[...] = prod.astype(o_ref.dtype)
            return
        acc = o_ref if direct else scratch[0]
        k = pl.program_id(2)

        @pl.when(k == 0)
        def _():
            acc[...] = prod

        @pl.when(k > 0)
        def _():
            acc[...] += prod

        if not direct:
            @pl.when(k == nk - 1)
            def _():
                o_ref[...] = acc[...].astype(o_ref.dtype)

    scratch = [] if (direct or nk == 1) else [pltpu.VMEM(acc_shape, F32)]
    args, in_specs, aliases = (a, b), [a_spec, b_spec], None
    if into is not None:
        args, in_specs, aliases = (a, b, into), in_specs + [pl.BlockSpec(memory_space=pl.ANY)], {2: 0}
    return _pcall(body, args, name=name, grid=grid, in_specs=in_specs, out_specs=o_spec,
                  out_shape=out_shape, scratch_shapes=scratch, sem=("parallel", "parallel", "arbitrary"),
                  aliases=aliases, phases=phases)


def _row_tile(m, want):
    t = min(m, want)
    assert m % t == 0
    return t


def _mm_rows(a, w, *, dn, name, out_dtype=F32, tm=2048):
    m, k = a.shape
    n = w.shape[1] if dn == NN else w.shape[0]
    tm = _row_tile(m, tm)
    return _matmul(
        a, w, dn=dn, grid=(m // tm, 1, 1), name=name,
        a_spec=pl.BlockSpec((tm, k), lambda i, j, kk: (i, 0)),
        b_spec=pl.BlockSpec(w.shape, lambda i, j, kk: (0, 0)),
        o_spec=pl.BlockSpec((tm, n), lambda i, j, kk: (i, 0)),
        out_shape=jax.ShapeDtypeStruct((m, n), out_dtype), acc_shape=(tm, n))


def _mm_tn(a, b, *, name, tk=2048):
    m, ka = a.shape
    n = b.shape[1]
    tk = _row_tile(m, tk)
    return _matmul(
        a, b, dn=TN, grid=(1, 1, m // tk), name=name,
        a_spec=pl.BlockSpec((tk, ka), lambda i, j, kk: (kk, 0)),
        b_spec=pl.BlockSpec((tk, n), lambda i, j, kk: (kk, 0)),
        o_spec=pl.BlockSpec((ka, n), lambda i, j, kk: (0, 0)),
        out_shape=jax.ShapeDtypeStruct((ka, n), GRAD_DTYPE), acc_shape=(ka, n))


def _mod_matmul(x, mod, w8, bias8, *, flat_out, name, tm=2048, phases=()):
    m, k = x.shape
    nb, _, ns = w8.shape
    tm = _row_tile(m, tm)

    def body(x_ref, mod_ref, w_ref, b_ref, o_ref, h_ref, hs):
        @pl.when(pl.program_id(1) == 0)
        def _():
            h = (x_ref[...] * mod_ref[0:1, :] + mod_ref[1:2, :]).astype(BF)
            hs[...] = h
            h_ref[...] = h

        o_ref[...] = (jnp.dot(hs[...], w_ref[...], preferred_element_type=F32) + b_ref[...]).astype(o_ref.dtype)

    if flat_out:
        o_spec = pl.BlockSpec((tm, ns), lambda i, j: (i, j))
        o_shape = jax.ShapeDtypeStruct((m, nb * ns), ACT_DTYPE)
    else:
        o_spec = pl.BlockSpec((None, tm, ns), lambda i, j: (j, i, 0))
        o_shape = jax.ShapeDtypeStruct((nb, m, ns), ACT_DTYPE)
    return _pcall(
        body, (x, mod, w8, bias8), name=name, grid=(m // tm, nb),
        in_specs=[pl.BlockSpec((tm, k), lambda i, j: (i, 0)),
                  pl.BlockSpec((2, k), lambda i, j: (0, 0)),
                  pl.BlockSpec((None, k, ns), lambda i, j: (j, 0, 0)),
                  pl.BlockSpec((None, 1, ns), lambda i, j: (j, 0, 0))],
        out_specs=[o_spec, pl.BlockSpec((tm, k), lambda i, j: (i, 0))],
        out_shape=[o_shape, jax.ShapeDtypeStruct((m, k), BF)],
        scratch_shapes=[pltpu.VMEM((tm, k), BF)], sem=("parallel", "arbitrary"), phases=phases)


def _seg_spec(tm, d, s):
    return pl.BlockSpec((tm, d), lambda i, s=s: (i, s))


def _prev_halo_spec(tm, d, s):
    hb = tm // HALO
    return pl.BlockSpec((HALO, d), lambda i, s=s: (jnp.maximum(i * hb - 1, 0), s))


def _next_halo_spec(tm, d, s, m):
    hb = tm // HALO
    last = m // HALO - 1
    return pl.BlockSpec((HALO, d), lambda i, s=s: (jnp.minimum((i + 1) * hb, last), s))


def _spatial_mix(wm_ref, src, dst, bias_ref, tm, d):
    for n in range(tm // GMLP_BLOCK):
        for g in range(d // GMLP_BLOCK):
            rs = slice(n * GMLP_BLOCK, (n + 1) * GMLP_BLOCK)
            cs = slice(g * GMLP_BLOCK, (g + 1) * GMLP_BLOCK)
            v = jnp.dot(wm_ref[g], src[rs, cs], preferred_element_type=F32)
            if bias_ref is not None:
                v = v + bias_ref[:, cs]
            dst[rs, cs] = v


def _mix_fwd(z, conv_a, lnv, wm, bias_full, *, name, tm=256, phases=()):
    m, d9 = z.shape
    d = d9 // 9
    tm = _row_tile(m, tm)
    grp = d // len(POOL_WINDOWS)

    def body(zb, zc, zx, zu, zv, zp, zc_h, zx_h, zp_h, ca_ref, lnv_ref, wm_ref, bias_ref,
             ua_ref, ub_ref, d_ref, ext, vn_s, mixed_s):
        i = pl.program_id(0)
        first = i == 0
        f32 = lambda r: r[...].astype(F32)
        pa = f32(zc) * f32(zx)
        ext[0:HALO, :] = jnp.where(first, 0.0, f32(zc_h) * f32(zx_h))
        ext[HALO:HALO + tm, :] = pa
        w = ca_ref[...]
        conv = w[0:1, :] * ext[pl.ds(HALO - 2, tm), :] + w[1:2, :] * ext[pl.ds(HALO - 1, tm), :] + w[2:3, :] * pa
        ua_ref[...] = (f32(zb) * conv).astype(BF)
        p = f32(zp)
        ext[0:HALO, :] = jnp.where(first, 0.0, f32(zp_h))
        ext[HALO:HALO + tm, :] = p
        t = (i * tm + lax.broadcasted_iota(jnp.int32, (tm, 1), 0) + 1).astype(F32)
        for k, win in enumerate(POOL_WINDOWS):
            cs = slice(k * grp, (k + 1) * grp)
            s = p[:, cs]
            for j in range(1, win):
                s = s + ext[pl.ds(HALO - j, tm), cs]
            d_ref[:, cs] = (s / jnp.minimum(t, float(win)) - p[:, cs]).astype(BF)
        gv = _gelu(f32(zv))
        vhat, _ = _ln_stats(gv)
        vn_s[...] = (vhat * lnv_ref[0:1, :] + lnv_ref[1:2, :]).astype(BF)
        _spatial_mix(wm_ref, vn_s, mixed_s, bias_ref, tm, d)
        ub_ref[...] = (_gelu(f32(zu)) * mixed_s[...]).astype(BF)

    full = lambda a: pl.BlockSpec(a.shape, lambda i: (0,) * a.ndim)
    out = jax.ShapeDtypeStruct((m, d), BF)
    o_spec = pl.BlockSpec((tm, d), lambda i: (i, 0))
    return _pcall(
        body, (z, z, z, z, z, z, z, z, z, conv_a, lnv, wm, bias_full), name=name, grid=(m // tm,),
        in_specs=[_seg_spec(tm, d, s) for s in range(6)] + [_prev_halo_spec(tm, d, s) for s in (1, 2, 5)]
        + [full(conv_a), full(lnv), full(wm), full(bias_full)],
        out_specs=[o_spec, o_spec, o_spec], out_shape=[out, out, out],
        scratch_shapes=[pltpu.VMEM((HALO + tm, d), F32), pltpu.VMEM((tm, d), BF), pltpu.VMEM((tm, d), F32)],
        sem=("arbitrary",), phases=phases)


def _pool_proj(dd, w_pool, *, dn, name, out_dtype=F32, tm=512):
    m, d = dd.shape
    ng, grp, _ = w_pool.shape
    tm = _row_tile(m, tm)
    return _matmul(
        dd, w_pool, dn=dn, grid=(m // tm, ng, 1), name=name,
        a_spec=pl.BlockSpec((tm, grp), lambda i, j, kk: (i, j)),
        b_spec=pl.BlockSpec((None, grp, grp), lambda i, j, kk: (j, 0, 0)),
        o_spec=pl.BlockSpec((tm, grp), lambda i, j, kk: (i, j)),
        out_shape=jax.ShapeDtypeStruct((m, d), out_dtype), acc_shape=(tm, grp))


def _merge(z, ya, yb, ycp, scale, *, name, tm=512):
    m, d = ya.shape
    tm = _row_tile(m, tm)

    def body(ga, gb, gc, ya_ref, yb_ref, yc_ref, sc_ref, o_ref):
        f32 = lambda r: r[...].astype(F32)
        o_ref[...] = (jax.nn.sigmoid(f32(ga)) * f32(ya_ref) + jax.nn.sigmoid(f32(gb)) * f32(yb_ref)
                      + jax.nn.sigmoid(f32(gc)) * (f32(yc_ref) * sc_ref[...])).astype(BF)

    row = pl.BlockSpec((tm, d), lambda i: (i, 0))
    return pl.pallas_call(
        body, name=name, grid=(m // tm,),
        in_specs=[_seg_spec(tm, d, 6), _seg_spec(tm, d, 7), _seg_spec(tm, d, 8), row, row, row,
                  pl.BlockSpec((1, d), lambda i: (0, 0))],
        out_specs=row, out_shape=jax.ShapeDtypeStruct((m, d), BF),
        compiler_params=_params(("parallel",)),
    )(z, z, z, ya, yb, ycp, scale)


def _resid_ln(xp, ys, vec, alpha, *, name, tm=512):
    m, d = xp.shape
    tm = _row_tile(m, tm)

    def body(xp_ref, ys_ref, v_ref, o_ref):
        xhat, _ = _ln_stats(alpha * xp_ref[...] + v_ref[0:1, :] * ys_ref[...])
        o_ref[...] = xhat * v_ref[1:2, :] + v_ref[2:3, :]

    row = pl.BlockSpec((tm, d), lambda i: (i, 0))
    return pl.pallas_call(
        body, name=name, grid=(m // tm,),
        in_specs=[row, row, pl.BlockSpec(vec.shape, lambda i: (0, 0))],
        out_specs=row, out_shape=jax.ShapeDtypeStruct((m, d), F32),
        compiler_params=_params(("parallel",)),
    )(xp, ys, vec)


def _ffn_fwd(up4, cw, cb, *, name, tm=512, phases=()):
    _, nj, m, fs = up4.shape
    tm = _row_tile(m, tm)
    hb = tm // HALO

    def body(up_ref, ah_ref, cw_ref, cb_ref, f_ref, ext):
        first = pl.program_id(1) == 0
        ext[0:HALO, :] = jnp.where(first, 0.0, ah_ref[...].astype(F32))
        ext[HALO:HALO + tm, :] = up_ref[0].astype(F32)
        w = cw_ref[...]
        w0, w1, w2, bias = w[0:1, :], w[1:2, :], w[2:3, :], cb_ref[...]
        rc = 16

        def step(c, carry):
            r0 = pl.multiple_of(c * rc, rc)
            win = ext[pl.ds(r0 + HALO - 8, rc + 8), :]
            a0, a1, a2 = win[8:8 + rc], pltpu.roll(win, 1, 0)[8:8 + rc], pltpu.roll(win, 2, 0)[8:8 + rc]
            ca = w0 * a2 + w1 * a1 + w2 * a0 + bias
            f_ref[pl.ds(r0, rc), :] = (_gelu(ca) * up_ref[1, pl.ds(r0, rc), :].astype(F32)).astype(BF)
            return carry

        lax.fori_loop(0, tm // rc, step, 0)

    return _pcall(
        body, (up4, up4, cw, cb), name=name, grid=(nj, m // tm),
        in_specs=[pl.BlockSpec((2, None, tm, fs), lambda j, i: (0, j, i, 0)),
                  pl.BlockSpec((None, None, HALO, fs), lambda j, i: (0, j, jnp.maximum(i * hb - 1, 0), 0)),
                  pl.BlockSpec((None, 3, fs), lambda j, i: (j, 0, 0)),
                  pl.BlockSpec((None, 1, fs), lambda j, i: (j, 0, 0))],
        out_specs=pl.BlockSpec((None, tm, fs), lambda j, i: (j, i, 0)),
        out_shape=jax.ShapeDtypeStruct((nj, m, fs), BF),
        scratch_shapes=[pltpu.VMEM((HALO + tm, fs), F32)], sem=("parallel", "arbitrary"), phases=phases)


def _down_proj(f4, wd4, *, name, tm=2048, phases=()):
    nj, m, fs = f4.shape
    d = wd4.shape[2]
    tm = _row_tile(m, tm)
    return _matmul(
        f4, wd4, dn=NN, grid=(m // tm, 1, nj), name=name,
        a_spec=pl.BlockSpec((None, tm, fs), lambda i, j, kk: (kk, i, 0)),
        b_spec=pl.BlockSpec((None, fs, d), lambda i, j, kk: (kk, 0, 0)),
        o_spec=pl.BlockSpec((tm, d), lambda i, j, kk: (i, 0)),
        out_shape=jax.ShapeDtypeStruct((m, d), F32), acc_shape=(tm, d), phases=phases)


def _loss_grad(y, tgt, *, name, tm=512):
    m, d = y.shape
    tm = _row_tile(m, tm)
    ni = m // tm

    def body(y_ref, t_ref, dy_ref, l_ref, acc):
        i = pl.program_id(0)
        e = y_ref[...] - t_ref[...]
        dy_ref[...] = e * (1.0 / d)
        part = jnp.sum((e * e).reshape(tm // 8, 8, d), axis=0)

        @pl.when(i == 0)
        def _():
            acc[...] = part

        @pl.when(i > 0)
        def _():
            acc[...] += part

        @pl.when(i == ni - 1)
        def _():
            l_ref[...] = jnp.full((8, 128), 0.5 / d, F32) * jnp.sum(acc[...])

    row = pl.BlockSpec((tm, d), lambda i: (i, 0))
    return pl.pallas_call(
        body, name=name, grid=(ni,), in_specs=[row, row],
        out_specs=[row, pl.BlockSpec((8, 128), lambda i: (0, 0))],
        out_shape=[jax.ShapeDtypeStruct((m, d), F32), jax.ShapeDtypeStruct((8, 128), F32)],
        scratch_shapes=[pltpu.VMEM((8, d), F32)],
        compiler_params=_params(("arbitrary",)),
    )(y, tgt)


def _resid_ln_bwd(dpart, dh, xmod, mvec, xp, ys, vec, alpha, *, name, tm=256, phases=()):
    m, d = dpart.shape
    tm = _row_tile(m, tm)
    has_dh = dh is not None
    has_ln = xp is not None

    def body(*refs):
        refs = list(refs)
        dpart_ref = refs.pop(0)
        if has_dh:
            dh_ref, xm_ref, mv_ref = refs.pop(0), refs.pop(0), refs.pop(0)
        if has_ln:
            xp_ref, ys_ref, v_ref = refs.pop(0), refs.pop(0), refs.pop(0)
            dys_ref, dxp_ref, red_ref = refs
        else:
            dx_ref, red_ref = refs
        i = pl.program_id(0)
        dtot = dpart_ref[...]
        rows = [jnp.zeros((1, d), F32)] * 5
        if has_dh:
            dhv = dh_ref[...]
            dtot = dtot + dhv * mv_ref[...]
            rows[0] = _rowsum(dhv * xm_ref[...])
            rows[1] = _rowsum(dhv)
        if has_ln:
            ys = ys_ref[...]
            gt = v_ref[0:1, :]
            xhat, rstd = _ln_stats(alpha * xp_ref[...] + gt * ys)
            rows[2] = _rowsum(dtot * xhat)
            rows[3] = _rowsum(dtot)
            dr = _ln_bwd(dtot, xhat, rstd, v_ref[1:2, :])
            rows[4] = _rowsum(dr * ys)
            dys_ref[...] = (dr * gt).astype(BF)
            dxp_ref[...] = alpha * dr
        else:
            dx_ref[...] = dtot
        red = jnp.concatenate(rows + [jnp.zeros((3, d), F32)], axis=0)

        @pl.when(i == 0)
        def _():
            red_ref[...] = red

        @pl.when(i > 0)
        def _():
            red_ref[...] += red

    row = pl.BlockSpec((tm, d), lambda i: (i, 0))
    vrow = lambda a: pl.BlockSpec(a.shape, lambda i: (0, 0))
    args, specs = [dpart], [row]
    if has_dh:
        args += [dh, xmod, mvec]
        specs += [row, row, vrow(mvec)]
    if has_ln:
        args += [xp, ys, vec]
        specs += [row, row, vrow(vec)]
        out_specs = [row, row, pl.BlockSpec((8, d), lambda i: (0, 0))]
        out_shape = [jax.ShapeDtypeStruct((m, d), BF), jax.ShapeDtypeStruct((m, d), F32),
                     jax.ShapeDtypeStruct((8, d), F32)]
    else:
        out_specs = [row, pl.BlockSpec((8, d), lambda i: (0, 0))]
        out_shape = [jax.ShapeDtypeStruct((m, d), F32), jax.ShapeDtypeStruct((8, d), F32)]
    return _pcall(body, args, name=name, grid=(m // tm,), in_specs=specs, out_specs=out_specs, out_shape=out_shape,
                  sem=("arbitrary",), phases=phases)


def _down_bwd(dy, wd4, *, name, tm=2048, phases=()):
    m, d = dy.shape
    nj, fs, _ = wd4.shape
    tm = _row_tile(m, tm)
    return _matmul(
        dy, wd4, dn=NT, grid=(m // tm, nj, 1), name=name,
        a_spec=pl.BlockSpec((tm, d), lambda i, j, kk: (i, 0)),
        b_spec=pl.BlockSpec((None, fs, d), lambda i, j, kk: (j, 0, 0)),
        o_spec=pl.BlockSpec((None, tm, fs), lambda i, j, kk: (j, i, 0)),
        out_shape=jax.ShapeDtypeStruct((nj, m, fs), ACT_DTYPE), acc_shape=(tm, fs), phases=phases)


def _tn_shards_lhs(f4, dy, *, name, tk=2048, phases=()):
    nj, m, fs = f4.shape
    d = dy.shape[1]
    tk = _row_tile(m, tk)
    return _matmul(
        f4, dy, dn=TN, grid=(nj, 1, m // tk), name=name,
        a_spec=pl.BlockSpec((None, tk, fs), lambda i, j, kk: (i, kk, 0)),
        b_spec=pl.BlockSpec((tk, d), lambda i, j, kk: (kk, 0)),
        o_spec=pl.BlockSpec((None, fs, d), lambda i, j, kk: (i, 0, 0)),
        out_shape=jax.ShapeDtypeStruct((nj, fs, d), GRAD_DTYPE), acc_shape=(fs, d), phases=phases)


def _tn_shards_rhs(h, d8, *, name, tk=2048, phases=()):
    m, k = h.shape
    nb, _, ns = d8.shape
    tk = _row_tile(m, tk)
    return _matmul(
        h, d8, dn=TN, grid=(nb, 1, m // tk), name=name,
        a_spec=pl.BlockSpec((tk, k), lambda i, j, kk: (kk, 0)),
        b_spec=pl.BlockSpec((None, tk, ns), lambda i, j, kk: (i, kk, 0)),
        o_spec=pl.BlockSpec((None, k, ns), lambda i, j, kk: (i, 0, 0)),
        out_shape=jax.ShapeDtypeStruct((nb, k, ns), GRAD_DTYPE), acc_shape=(k, ns), phases=phases)


def _tn_cols_rhs(h, dz, nb, *, name, tk=2048, phases=()):
    m, k = h.shape
    ns = dz.shape[1] // nb
    tk = _row_tile(m, tk)
    return _matmul(
        h, dz, dn=TN, grid=(nb, 1, m // tk), name=name,
        a_spec=pl.BlockSpec((tk, k), lambda i, j, kk: (kk, 0)),
        b_spec=pl.BlockSpec((tk, ns), lambda i, j, kk: (kk, i)),
        o_spec=pl.BlockSpec((None, k, ns), lambda i, j, kk: (i, 0, 0)),
        out_shape=jax.ShapeDtypeStruct((nb, k, ns), GRAD_DTYPE), acc_shape=(k, ns), phases=phases)


def _nt_shards(d8, w8, *, name, tm=2048, phases=()):
    nb, m, ns = d8.shape
    k = w8.shape[1]
    tm = _row_tile(m, tm)
    return _matmul(
        d8, w8, dn=NT, grid=(m // tm, 1, nb), name=name,
        a_spec=pl.BlockSpec((None, tm, ns), lambda i, j, kk: (kk, i, 0)),
        b_spec=pl.BlockSpec((None, k, ns), lambda i, j, kk: (kk, 0, 0)),
        o_spec=pl.BlockSpec((tm, k), lambda i, j, kk: (i, 0)),
        out_shape=jax.ShapeDtypeStruct((m, k), F32), acc_shape=(tm, k), phases=phases)


def _nt_cols(dz, w8, *, name, tm=2048, phases=(), tiles=None, into=None):
    m = dz.shape[0]
    nb, k, ns = w8.shape
    tm = _row_tile(m, tm)
    first, count = tiles if tiles is not None else (0, m // tm)
    return _matmul(
        dz, w8, dn=NT, grid=(count, 1, nb), name=name,
        a_spec=pl.BlockSpec((tm, ns), lambda i, j, kk: (i + first, kk)),
        b_spec=pl.BlockSpec((None, k, ns), lambda i, j, kk: (kk, 0, 0)),
        o_spec=pl.BlockSpec((tm, k), lambda i, j, kk: (i + first, 0)),
        out_shape=jax.ShapeDtypeStruct((m, k), F32), acc_shape=(tm, k), phases=phases, into=into)


def _tn_pool(dd, dyc, ng, *, name, tk=2048):
    m, d = dd.shape
    grp = d // ng
    tk = _row_tile(m, tk)
    return _matmul(
        dd, dyc, dn=TN, grid=(ng, 1, m // tk), name=name,
        a_spec=pl.BlockSpec((tk, grp), lambda i, j, kk: (kk, i)),
        b_spec=pl.BlockSpec((tk, grp), lambda i, j, kk: (kk, i)),
        o_spec=pl.BlockSpec((None, grp, grp), lambda i, j, kk: (i, 0, 0)),
        out_shape=jax.ShapeDtypeStruct((ng, grp, grp), GRAD_DTYPE), acc_shape=(grp, grp))


def _ffn_bwd(up4, df4, cw, cb, *, name, tm=256, phases=()):
    _, nj, m, fs = up4.shape
    tm = _row_tile(m, tm)
    hb = tm // HALO
    ni = m // tm
    last_hb = m // HALO - 1
    ext_rows = tm + 8

    rc = 16
    assert tm % rc == 0

    def body(up_ref, ap_ref, un_ref, df_ref, dfn_ref, cw_ref, cb_ref, dup_ref, red_ref, ext, dca_s, racc):
        i = pl.program_id(1)
        ext[0:HALO, :] = jnp.where(i == 0, 0.0, ap_ref[...].astype(F32))
        ext[HALO:HALO + tm, :] = up_ref[0].astype(F32)
        ext[HALO + tm:2 * HALO + tm, :] = un_ref[0].astype(F32)
        racc[...] = jnp.zeros_like(racc)
        w = cw_ref[...]
        w0, w1, w2, bias = w[0:1, :], w[1:2, :], w[2:3, :], cb_ref[...]

        def conv_taps(win, n):
            return (win[8:8 + n], pltpu.roll(win, 1, 0)[8:8 + n], pltpu.roll(win, 2, 0)[8:8 + n])

        def fold(v):
            return v[0:8] + v[8:16]

        def add_red(k, v8):
            racc[8 * k:8 * k + 8, :] += v8

        def first_pass(c, carry):
            r0 = pl.multiple_of(c * rc, rc)
            a0, a1, a2 = conv_taps(ext[pl.ds(r0 + HALO - 8, rc + 8), :], rc)
            act, dact = _gelu_parts(w0 * a2 + w1 * a1 + w2 * a0 + bias)
            dfc = df_ref[pl.ds(r0, rc), :].astype(F32)
            dca = dfc * up_ref[1, pl.ds(r0, rc), :].astype(F32) * dact
            dup_g = dfc * act
            dca_s[pl.ds(r0, rc), :] = dca
            dup_ref[1, pl.ds(r0, rc), :] = dup_g.astype(BF)
            for k, v in enumerate((dca * a2, dca * a1, dca * a0, dca, dup_g)):
                add_red(k if k < 4 else 5, fold(v))
            return carry

        lax.fori_loop(0, tm // rc, first_pass, 0)
        a0, a1, a2 = conv_taps(ext[HALO + tm - 8:HALO + tm + 8, :], 8)
        _, dact = _gelu_parts(w0 * a2 + w1 * a1 + w2 * a0 + bias)
        after = dfn_ref[...].astype(F32)[0:8, :] * un_ref[1].astype(F32)[0:8, :] * dact
        dca_s[tm:tm + 8, :] = jnp.where(i < ni - 1, after, 0.0)
        dca_s[tm + 8:tm + 16, :] = jnp.zeros((8, fs), F32)

        def second_pass(c, carry):
            r0 = pl.multiple_of(c * rc, rc)
            win = dca_s[pl.ds(r0, rc + 8), :]
            up1, up2 = pltpu.roll(win, rc + 7, 0)[0:rc], pltpu.roll(win, rc + 6, 0)[0:rc]
            dup_a = w2 * win[0:rc] + w1 * up1 + w0 * up2
            dup_ref[0, pl.ds(r0, rc), :] = dup_a.astype(BF)
            add_red(4, fold(dup_a))
            return carry

        lax.fori_loop(0, tm // rc, second_pass, 0)
        red = jnp.concatenate([_rowsum(racc[8 * k:8 * k + 8, :]) for k in range(6)] + [jnp.zeros((2, fs), F32)],
                              axis=0)

        @pl.when(i == 0)
        def _():
            red_ref[...] = red

        @pl.when(i > 0)
        def _():
            red_ref[...] += red

    nxt = lambda j, i: jnp.minimum((i + 1) * hb, last_hb)
    return _pcall(
        body, (up4, up4, up4, df4, df4, cw, cb), name=name, grid=(nj, ni), sem=("parallel", "arbitrary"), phases=phases,
        in_specs=[pl.BlockSpec((2, None, tm, fs), lambda j, i: (0, j, i, 0)),
                  pl.BlockSpec((None, None, HALO, fs), lambda j, i: (0, j, jnp.maximum(i * hb - 1, 0), 0)),
                  pl.BlockSpec((2, None, HALO, fs), lambda j, i: (0, j, nxt(j, i), 0)),
                  pl.BlockSpec((None, tm, fs), lambda j, i: (j, i, 0)),
                  pl.BlockSpec((None, HALO, fs), lambda j, i: (j, nxt(j, i), 0)),
                  pl.BlockSpec((None, 3, fs), lambda j, i: (j, 0, 0)),
                  pl.BlockSpec((None, 1, fs), lambda j, i: (j, 0, 0))],
        out_specs=[pl.BlockSpec((2, None, tm, fs), lambda j, i: (0, j, i, 0)),
                   pl.BlockSpec((None, 8, fs), lambda j, i: (j, 0, 0))],
        out_shape=[jax.ShapeDtypeStruct((2, nj, m, fs), BF), jax.ShapeDtypeStruct((nj, 8, fs), F32)],
        scratch_shapes=[pltpu.VMEM((2 * HALO + tm, fs), F32), pltpu.VMEM((tm + 16, fs), F32),
                        pltpu.VMEM((48, fs), F32)])


def _gate_bwd(dm, z, ya, yb, ycp, scale, *, name, tm=256):
    m, d = dm.shape
    tm = _row_tile(m, tm)

    def body(dm_ref, ga, gb, gc, ya_ref, yb_ref, yc_ref, sc_ref, dya_ref, dyb_ref, dyc_ref, dz_ref, red_ref):
        i = pl.program_id(0)
        f32 = lambda r: r[...].astype(F32)
        dmv = f32(dm_ref)
        sa, sb, sc = jax.nn.sigmoid(f32(ga)), jax.nn.sigmoid(f32(gb)), jax.nn.sigmoid(f32(gc))
        scale_v = sc_ref[...]
        ycp_v = f32(yc_ref)
        dya_ref[...] = (dmv * sa).astype(BF)
        dyb_ref[...] = (dmv * sb).astype(BF)
        dyc = dmv * sc
        dyc_ref[...] = (dyc * scale_v).astype(BF)
        dga = dmv * f32(ya_ref) * (sa * (1.0 - sa))
        dgb = dmv * f32(yb_ref) * (sb * (1.0 - sb))
        dgc = dmv * (ycp_v * scale_v) * (sc * (1.0 - sc))
        dz_ref[:, 0:d] = dga.astype(BF)
        dz_ref[:, d:2 * d] = dgb.astype(BF)
        dz_ref[:, 2 * d:3 * d] = dgc.astype(BF)
        red = jnp.concatenate([_rowsum(dyc * ycp_v), _rowsum(dga), _rowsum(dgb), _rowsum(dgc),
                               jnp.zeros((4, d), F32)], axis=0)

        @pl.when(i == 0)
        def _():
            red_ref[...] = red

        @pl.when(i > 0)
        def _():
            red_ref[...] += red

    row = pl.BlockSpec((tm, d), lambda i: (i, 0))
    obf = jax.ShapeDtypeStruct((m, d), BF)
    return pl.pallas_call(
        body, name=name, grid=(m // tm,),
        in_specs=[row, _seg_spec(tm, d, 6), _seg_spec(tm, d, 7), _seg_spec(tm, d, 8), row, row, row,
                  pl.BlockSpec((1, d), lambda i: (0, 0))],
        out_specs=[row, row, row, pl.BlockSpec((tm, 3 * d), lambda i: (i, 2)), pl.BlockSpec((8, d), lambda i: (0, 0))],
        out_shape=[obf, obf, obf, jax.ShapeDtypeStruct((m, 9 * d), BF), jax.ShapeDtypeStruct((8, d), F32)],
        compiler_params=_params(("arbitrary",)),
    )(dm, z, z, z, ya, yb, ycp, scale)


def _mix_bwd(dz, dua, dub, ddd, z, conv_a, lnv, wm, wmt, bias_full, mask, *, name, tm=128, phases=()):
    m, d = dua.shape
    tm = _row_tile(m, tm)
    ni = m // tm
    grp = d // len(POOL_WINDOWS)
    ng = d // GMLP_BLOCK
    ext_rows = tm + 8

    def body(dz_in, dua_ref, dub_ref, dd_ref, zb, zc, zx, zu, zv, zp, zc_h, zx_h, dua_n, zb_n, dd_n,
             ca_ref, lnv_ref, wm_ref, wmt_ref, bias_ref, mask_ref,
             dz_ref, red_ref, dws_ref, dbs_ref, ext, sh_s, vn_s, mixed_s, dmx_s, dvn_s, dbs_acc):
        del dz_in
        i = pl.program_id(0)
        rows = []
        f32 = lambda r: r[...].astype(F32)
        zbv, zcv, zxv = f32(zb), f32(zc), f32(zx)
        pa = zcv * zxv
        ext[0:HALO, :] = jnp.where(i == 0, 0.0, f32(zc_h) * f32(zx_h))
        ext[HALO:HALO + tm, :] = pa
        w = ca_ref[...]
        w0, w1, w2 = w[0:1, :], w[1:2, :], w[2:3, :]
        p1 = ext[pl.ds(HALO - 1, tm), :]
        p2 = ext[pl.ds(HALO - 2, tm), :]
        conv = w0 * p2 + w1 * p1 + w2 * pa
        duav = f32(dua_ref)
        dzb = duav * conv
        dca = duav * zbv
        dca_n = jnp.where(i < ni - 1, f32(dua_n)[0:8, :] * f32(zb_n)[0:8, :], 0.0)
        sh_s[0:tm, :] = dca
        sh_s[tm:tm + 8, :] = dca_n
        dpa = w2 * dca + w1 * sh_s[pl.ds(1, tm), :] + w0 * sh_s[pl.ds(2, tm), :]
        dzc = dpa * zxv
        dzx = dpa * zcv
        dz_ref[:, 0:d] = dzb.astype(BF)
        dz_ref[:, d:2 * d] = dzc.astype(BF)
        dz_ref[:, 2 * d:3 * d] = dzx.astype(BF)
        rows += [_rowsum(dzb), _rowsum(dzc), _rowsum(dzx)]
        dconv = [_rowsum(dca * p2), _rowsum(dca * p1), _rowsum(dca * pa)]
        zuv, zvv = f32(zu), f32(zv)
        gu, dgu_dz = _gelu_parts(zuv)
        gv, dgv_dz = _gelu_parts(zvv)
        vhat, rstd = _ln_stats(gv)
        gain = lnv_ref[0:1, :]
        vn_s[...] = (vhat * gain + lnv_ref[1:2, :]).astype(BF)
        _spatial_mix(wm_ref, vn_s, mixed_s, bias_ref, tm, d)
        dubv = f32(dub_ref)
        dzu = dubv * mixed_s[...] * dgu_dz
        dmixed = dubv * gu
        dmx_s[...] = dmixed.astype(BF)
        _spatial_mix(wmt_ref, dmx_s, dvn_s, None, tm, d)
        dvn = dvn_s[...]
        dzv = _ln_bwd(dvn, vhat, rstd, gain) * dgv_dz
        dz_ref[:, 3 * d:4 * d] = dzu.astype(BF)
        dz_ref[:, 4 * d:5 * d] = dzv.astype(BF)
        rows += [_rowsum(dzu), _rowsum(dzv)]
        dlnv = [_rowsum(dvn * vhat), _rowsum(dvn)]
        dbs_part = dmixed[0:GMLP_BLOCK, :]
        for n in range(1, tm // GMLP_BLOCK):
            dbs_part = dbs_part + dmixed[n * GMLP_BLOCK:(n + 1) * GMLP_BLOCK, :]
        ddv = f32(dd_ref)
        t = (i * tm + lax.broadcasted_iota(jnp.int32, (ext_rows + 8, 1), 0) + 1).astype(F32)
        dde = jnp.concatenate([ddv, jnp.where(i < ni - 1, f32(dd_n), 0.0)], axis=0)
        for k, win in enumerate(POOL_WINDOWS):
            cs = slice(k * grp, (k + 1) * grp)
            ext[0:tm + HALO, cs] = dde[:, cs] / jnp.minimum(t, float(win))
        dzp_parts = []
        for k, win in enumerate(POOL_WINDOWS):
            cs = slice(k * grp, (k + 1) * grp)
            s = ext[0:tm, cs]
            for j in range(1, win):
                s = s + ext[pl.ds(j, tm), cs]
            dzp_parts.append(s - ddv[:, cs])
        dzp = jnp.concatenate(dzp_parts, axis=1)
        dz_ref[:, 5 * d:6 * d] = dzp.astype(BF)
        rows += [_rowsum(dzp)]
        red = jnp.concatenate(rows + dconv + dlnv + [jnp.zeros((5, d), F32)], axis=0)

        @pl.when(i == 0)
        def _():
            red_ref[...] = red
            dbs_acc[...] = dbs_part
            dws_ref[...] = jnp.zeros_like(dws_ref)

        @pl.when(i > 0)
        def _():
            red_ref[...] += red
            dbs_acc[...] += dbs_part

        for n in range(tm // GMLP_BLOCK):
            for g in range(ng):
                rs = slice(n * GMLP_BLOCK, (n + 1) * GMLP_BLOCK)
                cs = slice(g * GMLP_BLOCK, (g + 1) * GMLP_BLOCK)
                dws_ref[g] += mask_ref[...] * lax.dot_general(
                    dmx_s[rs, cs], vn_s[rs, cs], (NT, ((), ())), preferred_element_type=F32)

        @pl.when(i == ni - 1)
        def _():
            lane = lax.broadcasted_iota(jnp.int32, (GMLP_BLOCK, GMLP_BLOCK), 1)
            out = jnp.zeros((GMLP_BLOCK, GMLP_BLOCK), F32)
            for g in range(ng):
                sg = jnp.sum(dbs_acc[:, g * GMLP_BLOCK:(g + 1) * GMLP_BLOCK], axis=1, keepdims=True)
                out = out + jnp.where(lane == g, sg, 0.0)
            dbs_ref[...] = out

    row = pl.BlockSpec((tm, d), lambda i: (i, 0))
    full = lambda a: pl.BlockSpec(a.shape, lambda i: (0,) * a.ndim)
    hb = tm // HALO
    last_hb = m // HALO - 1
    nrow = pl.BlockSpec((HALO, d), lambda i: (jnp.minimum((i + 1) * hb, last_hb), 0))
    return _pcall(
        body, (dz, dua, dub, ddd, z, z, z, z, z, z, z, z, dua, z, ddd, conv_a, lnv, wm, wmt, bias_full, mask),
        name=name, grid=(ni,), sem=("arbitrary",), aliases={0: 0}, phases=phases,
        in_specs=[pl.BlockSpec(memory_space=pl.ANY), row, row, row]
        + [_seg_spec(tm, d, s) for s in range(6)]
        + [_prev_halo_spec(tm, d, 1), _prev_halo_spec(tm, d, 2), nrow, _next_halo_spec(tm, d, 0, m), nrow]
        + [full(conv_a), full(lnv), full(wm), full(wmt), full(bias_full), full(mask)],
        out_specs=[pl.BlockSpec((tm, 6 * d), lambda i: (i, 0)), pl.BlockSpec((16, d), lambda i: (0, 0)),
                   full(wm), pl.BlockSpec((GMLP_BLOCK, GMLP_BLOCK), lambda i: (0, 0))],
        out_shape=[jax.ShapeDtypeStruct(dz.shape, BF), jax.ShapeDtypeStruct((16, d), F32),
                   jax.ShapeDtypeStruct(wm.shape, F32), jax.ShapeDtypeStruct((GMLP_BLOCK, GMLP_BLOCK), F32)],
        scratch_shapes=[pltpu.VMEM((2 * HALO + tm, d), F32), pltpu.VMEM((tm + 8, d), F32),
                        pltpu.VMEM((tm, d), BF), pltpu.VMEM((tm, d), F32), pltpu.VMEM((tm, d), BF),
                        pltpu.VMEM((tm, d), F32), pltpu.VMEM((GMLP_BLOCK, d), F32)])


REST = ("w_a_out", "w_b_out", "w_pool", "w_o", "w_up", "w_down")


def _remote(src, dst, ssem, rsem, k, to):
    return pltpu.make_async_remote_copy(src_ref=src, dst_ref=dst, send_sem=ssem.at[k], recv_sem=rsem.at[k],
                                        device_id=to, device_id_type=MESH)


def _gather_phase1(shards):
    n = len(shards)

    def build(ins, outs, ssem, rsem, lsem):
        x, y, c, chips = _place()
        me = 4 * x + 2 * y + c
        local = [pltpu.make_async_copy(ins[a], outs[a].at[:, me], lsem.at[a]) for a in range(n)]
        sends, recvs = [], []
        for j, (cx, cy) in enumerate(chips):
            for a in range(n):
                sends.append(_remote(ins[a], outs[a].at[:, me], ssem, rsem, 4 * a + 1 + j, (cx, cy, c)))
                recvs.append(_remote(ins[a], outs[a].at[:, 4 * cx + 2 * cy + c], ssem, rsem, 4 * a + 1 + j, (cx, cy, c)))
        for a in range(n):
            sends.append(_remote(ins[a], outs[a].at[:, me], ssem, rsem, 4 * a, (x, y, 1 - c)))
            recvs.append(_remote(ins[a], outs[a].at[:, 4 * x + 2 * y + 1 - c], ssem, rsem, 4 * a, (x, y, 1 - c)))
        return dict(start=local + sends, recv=recvs, send=sends, local=local)

    outs = [jax.ShapeDtypeStruct((s.shape[0], N_DEV) + s.shape[1:], s.dtype) for s in shards]
    return _Phase(shards, outs, {}, 4 * n, n, build)


def _gather_phase2(fulls):
    n = len(fulls)

    def build(ins, outs, ssem, rsem, lsem):
        x, y, c, chips = _place()
        sends, recvs = [], []
        for j, (cx, cy) in enumerate(chips):
            for a in range(n):
                mine, theirs = 4 * cx + 2 * cy + c, 4 * cx + 2 * cy + 1 - c
                sends.append(_remote(ins[a].at[:, mine], outs[a].at[:, mine], ssem, rsem, 3 * a + j, (x, y, 1 - c)))
                recvs.append(_remote(ins[a].at[:, theirs], outs[a].at[:, theirs], ssem, rsem, 3 * a + j, (x, y, 1 - c)))
        return dict(start=sends, recv=recvs, send=sends, local=[])

    outs = [jax.ShapeDtypeStruct(f.shape, f.dtype) for f in fulls]
    return _Phase(fulls, outs, {a: a for a in range(n)}, 3 * n, 0, build)


def _pair_phase(grads):
    n = len(grads)

    def build(ins, outs, ssem, rsem, lsem):
        x, y, c, _ = _place()
        cps = [_remote(ins[a].at[:, 2 * q + (1 - c)], outs[a].at[q], ssem, rsem, 4 * a + q, (x, y, 1 - c))
               for a in range(n) for q in range(4)]
        return dict(start=cps, recv=cps, send=cps, local=[])

    outs = [jax.ShapeDtypeStruct((4, g.shape[0]) + g.shape[2:], g.dtype) for g in grads]
    return _Phase(grads, outs, {}, 4 * n, 0, build)


def _chip_phase(bufs, accs, l, depth):
    n = len(bufs)
    has = accs is not None

    def build(ins, outs, ssem, rsem, lsem):
        x, y, c, chips = _place()
        myq = 2 * x + y
        local, sends, recvs = [], [], []
        for a in range(n):
            local.append(pltpu.make_async_copy(ins[a].at[myq], outs[a].at[myq, l], lsem.at[a]))
            for j, (cx, cy) in enumerate(chips):
                q = 2 * cx + cy
                sends.append(_remote(ins[a].at[q], outs[a].at[myq, l], ssem, rsem, 3 * a + j, (cx, cy, c)))
                recvs.append(_remote(ins[a].at[q], outs[a].at[q, l], ssem, rsem, 3 * a + j, (cx, cy, c)))
        return dict(start=local + sends, recv=recvs, send=sends, local=local)

    outs = [jax.ShapeDtypeStruct((4, depth) + b.shape[1:], b.dtype) for b in bufs]
    return _Phase(list(bufs) + (list(accs) if has else []), outs, {n + a: a for a in range(n)} if has else {},
                  3 * n, n, build)


def _grad_chunks(n, g):
    if n == "w_pool":
        return g.reshape(g.shape[0], N_DEV, g.shape[1] // N_DEV, g.shape[2])
    if n in ("w_in", "w_up"):
        return g[None]
    return g.reshape(1, N_DEV, -1, g.shape[-1])


class _ReduceScatter:
    def __init__(self, depth, cidx):
        self.depth, self.cidx, self.acc, self.count = depth, cidx, {}, 0
        self.small_gathered = None

    def pair(self, names, grads):
        return _pair_phase([_grad_chunks(n, grads[n]) for n in names])

    def sums(self, names, grads, phase):
        out = []
        for n, r1 in zip(names, phase.results):
            out.append(_pair_sum(_grad_chunks(n, grads[n]), r1, self.cidx, name="rs_sum_%d" % self.count))
            self.count += 1
        return out

    def chip(self, names, bufs, l):
        accs = [self.acc[n] for n in names] if names[0] in self.acc else None
        return _chip_phase(bufs, accs, l, self.depth)

    def done(self, names, phase):
        for n, r in zip(names, phase.results):
            self.acc[n] = r


def _rest_views(fulls, d):
    a_out, b_out, pool, o, up, down = fulls
    grp = d // len(POOL_WINDOWS)
    return dict(w_a_out=a_out.reshape(d, d), w_b_out=b_out.reshape(d, d), w_o=o.reshape(d, d),
                w_pool=pool.reshape(len(POOL_WINDOWS), grp, grp), w_up8=up[0],
                wd4=down.reshape(N_DEV // 2, -1, d))


class _GatherPlan:
    def __init__(self):
        self.jobs, self.part, self.full = [], {}, {}

    def add(self, key, shard, first, second):
        self.jobs.append((key, shard, first, second))

    def phases(self, name):
        j1 = [j for j in self.jobs if j[2] == name]
        j2 = [j for j in self.jobs if j[3] == name]
        tagged = []
        if j1:
            tagged.append((self.part, j1, _gather_phase1([j[1] for j in j1])))
        if j2:
            tagged.append((self.full, j2, _gather_phase2([self.part[j[0]] for j in j2])))
        return tagged

    @staticmethod
    def collect(tagged):
        for store, jobs, phase in tagged:
            for j, r in zip(jobs, phase.results):
                store[j[0]] = r


def _layer_fwd(x, w, alpha, tag, plan=None):
    d = x.shape[1]
    grp = d // len(POOL_WINDOWS)

    def carried(kernel, *args, name, **kw):
        tagged = plan.phases(name) if plan else []
        out = kernel(*args, name=name, phases=[t[2] for t in tagged], **kw)
        _GatherPlan.collect(tagged)
        return out

    def weight(n, shape):
        return plan.full[n + tag].reshape(shape) if plan else w[n]

    z, h = carried(_mod_matmul, x, w["mod1"], w["w_in8"], w["b_in8"], flat_out=True, name="in_proj" + tag)
    ua, ub, dd = carried(_mix_fwd, z, w["conv_a"], w["lnv"], w["wm"], w["bias_full"], name="mix_fwd" + tag)
    w["w_a_out"], w["w_b_out"], w["w_o"] = (weight(n, (d, d)) for n in ("w_a_out", "w_b_out", "w_o"))
    w["w_pool"] = weight("w_pool", (len(POOL_WINDOWS), grp, grp))
    w["w_up8"] = weight("w_up8", (N_DEV, d, -1))
    ya = _mm_rows(ua, w["w_a_out"], dn=NN, name="a_out" + tag, out_dtype=ACT_DTYPE)
    yb = _mm_rows(ub, w["w_b_out"], dn=NN, name="b_out" + tag, out_dtype=ACT_DTYPE)
    ycp = _pool_proj(dd, w["w_pool"], dn=NN, name="pool_proj" + tag, out_dtype=ACT_DTYPE)
    merged = _merge(z, ya, yb, ycp, w["pool_scale"], name="merge" + tag)
    o = _mm_rows(merged, w["w_o"], dn=NN, name="o_proj" + tag)
    x1 = _resid_ln(x, o, w["ln1"], alpha, name="ln1" + tag)
    up8, h2 = carried(_mod_matmul, x1, w["mod2"], w["w_up8"], w["b_up8"], flat_out=False, name="up_proj" + tag)
    up4 = up8.reshape((2, up8.shape[0] // 2) + up8.shape[1:])
    f4 = carried(_ffn_fwd, up4, w["cw"], w["cb"], name="ffn_fwd" + tag)
    w["wd4"] = weight("wd4", (N_DEV // 2, -1, d))
    y2 = carried(_down_proj, f4, w["wd4"], name="down_proj" + tag)
    x2 = _resid_ln(x1, y2, w["ln2"], alpha, name="ln2" + tag)
    saved = dict(x=x, z=z, h=h, ua=ua, ub=ub, dd=dd, ya=ya, yb=yb, ycp=ycp, merged=merged, o=o, x1=x1,
                 up4=up4, h2=h2, f4=f4, y2=y2)
    return x2, saved


def _layer_bwd(dpart, dh_above, xmod_above, m_above, w, s, alpha, tag, l=0, above=None, rs=None, upper_reds=()):
    first, rest = ("w_in",), REST
    ph = lambda p: [p] if p is not None else ()
    r1a = rs.pair(first, above) if above else None
    dy2, dx1p, red2 = _resid_ln_bwd(dpart, dh_above, xmod_above, m_above, s["x1"], s["y2"], w["ln2"], alpha,
                                    name="ln2_bwd" + tag, phases=ph(r1a))
    r1b = rs.pair(rest, above) if above else None
    df4 = _down_bwd(dy2, w["wd4"], name="down_bwd" + tag, phases=ph(r1b))
    if above:
        sb_a, sb_b = rs.sums(first, above, r1a), rs.sums(rest, above, r1b)
    gw_down4 = _tn_shards_lhs(s["f4"], dy2, name="gw_down" + tag)
    if above:
        bufs = dict(zip(first + rest, sb_a + sb_b))
        heavy = ("w_in", "w_up")
        light = tuple(n for n in rest if n not in heavy)
    r3a = rs.chip(heavy, [bufs[n] for n in heavy], l + 1) if above else None
    dup4, redf = _ffn_bwd(s["up4"], df4, w["cw"], w["cb"], name="ffn_bwd" + tag, phases=ph(r3a))
    dup8 = dup4.reshape((dup4.shape[0] * dup4.shape[1],) + dup4.shape[2:])
    r3b = None
    if above:
        rs.done(heavy, r3a)
        r3b = rs.chip(light, [bufs[n] for n in light], l + 1)
    gw_up8 = _tn_shards_lhs(dup8, s["h2"], name="gw_up" + tag, phases=ph(r3b))
    if above:
        rs.done(light, r3b)
    own = rs is not None and l == 0
    big = dict(w_up=gw_up8, w_down=gw_down4)
    early = ("w_down", "w_up")
    o1 = rs.pair(early, big) if own else None
    dh2 = _nt_shards(dup8, w["w_up8"], name="up_bwd" + tag, phases=ph(o1))
    if own:
        sb_o = rs.sums(early, big, o1)
    do, dxp, red1 = _resid_ln_bwd(dx1p, dh2, s["x1"], w["mod2"][0:1], s["x"], s["o"], w["ln1"], alpha,
                                  name="ln1_bwd" + tag)
    dm = _mm_rows(do, w["w_o"], dn=NT, name="o_bwd" + tag, out_dtype=ACT_DTYPE)
    big["w_o"] = _mm_tn(s["merged"], do, name="gw_o" + tag)
    dya, dyb, dyc, dz, redg = _gate_bwd(dm, s["z"], s["ya"], s["yb"], s["ycp"], w["pool_scale"], name="gate_bwd" + tag)
    dua = _mm_rows(dya, w["w_a_out"], dn=NT, name="a_out_bwd" + tag, out_dtype=ACT_DTYPE)
    dub = _mm_rows(dyb, w["w_b_out"], dn=NT, name="b_out_bwd" + tag, out_dtype=ACT_DTYPE)
    ddd = _pool_proj(dyc, w["w_pool"], dn=NT, name="pool_bwd" + tag, out_dtype=ACT_DTYPE)
    big["w_a_out"] = _mm_tn(s["ua"], dya, name="gw_a_out" + tag)
    big["w_b_out"] = _mm_tn(s["ub"], dyb, name="gw_b_out" + tag)
    big["w_pool"] = _tn_pool(s["dd"], dyc, w["w_pool"].shape[0], name="gw_pool" + tag)
    o3 = rs.chip(early, sb_o, l) if own else None
    dz, redm, dws, dbs = _mix_bwd(dz, dua, dub, ddd, s["z"], w["conv_a"], w["lnv"], w["wm"], w["wmt"],
                                  w["bias_full"], w["mask"], name="mix_bwd" + tag, phases=ph(o3))
    reds = dict(red2=red2, redf=redf, red1=red1, redg=redg, redm=redm, dws=dws, dbs=dbs)
    mid = ("w_o", "w_a_out", "w_b_out", "w_pool")
    o1b = sg1 = None
    if own:
        rs.done(early, o3)
        o1b = rs.pair(mid, big)
        sg1 = _gather_phase1([_small_payload([reds] + list(upper_reds))])
    big["w_in"] = _tn_cols_rhs(s["h"], dz, w["w_in8"].shape[0], name="gw_in" + tag,
                               phases=[o1b, sg1] if own else ())
    pending = None
    if own:
        sb_m = rs.sums(mid, big, o1b)
        o1c, o3b, sg2 = rs.pair(first, big), rs.chip(mid, sb_m, l), _gather_phase2(sg1.results)
        n_tiles = dz.shape[0] // _row_tile(dz.shape[0], 2048)
        head = max(n_tiles // 2, 1)
        dh = _nt_cols(dz, w["w_in8"], name="in_bwd" + tag + "_a", phases=[o1c, o3b, sg2], tiles=(0, head))
        rs.done(mid, o3b)
        rs.small_gathered = sg2.results[0]
        o3c = rs.chip(first, rs.sums(first, big, o1c), l)
        if n_tiles > head:
            dh = _nt_cols(dz, w["w_in8"], name="in_bwd" + tag + "_b", phases=[o3c], tiles=(head, n_tiles - head),
                          into=dh)
            rs.done(first, o3c)
        else:
            pending = (first, o3c)
    else:
        dh = _nt_cols(dz, w["w_in8"], name="in_bwd" + tag)
    return dxp, dh, big, reds, pending


def _local_step(x, tgt, ws, alpha, plan=None, rs=None):
    depth = len(ws)
    saved = []
    y = x
    for l in range(depth):
        if plan and l > 0:
            ws[l]["w_in8"] = plan.full["w_in8_l%d" % l][0]
        y, s = _layer_fwd(y, ws[l], alpha, "_l%d" % l, plan)
        saved.append(s)
    dpart, loss_blk = _loss_grad(y, tgt, name="loss_grad")
    dh = xmod = mvec = above = pending = None
    bigs, reds = [None] * depth, [None] * depth
    for l in reversed(range(depth)):
        dpart, dh, bigs[l], reds[l], pending = _layer_bwd(dpart, dh, xmod, mvec, ws[l], saved[l], alpha, "_l%d" % l,
                                                          l, above if rs else None, rs, reds[l + 1:])
        xmod, mvec, above = saved[l]["x"], ws[l]["mod1"][0:1], bigs[l]
    grad_x, red0 = _resid_ln_bwd(dpart, dh, xmod, mvec, None, None, None, alpha, name="in_bwd_tail",
                                 phases=[pending[1]] if pending else ())
    if pending:
        rs.done(*pending)
    d_ada = []
    for l in range(depth):
        below = red0 if l == 0 else reds[l - 1]["red2"]
        r1, r2 = reds[l]["red1"], reds[l]["red2"]
        d_ada.append(jnp.stack([below[1], below[0], r1[4], r1[1], r1[0], r2[4]]))
    return loss_blk, grad_x, bigs, reds, jnp.stack(d_ada)


def _small_grads(r):
    redm, redg, redf = r["redm"], r["redg"], r["redf"]
    ng = r["dws"].shape[0]
    return dict(
        b_in=jnp.concatenate([redm[0:6], redg[1:4]], axis=0).reshape(-1),
        conv_a=redm[6:9], ln_v_g=redm[9], ln_v_b=redm[10],
        w_spatial=r["dws"], b_spatial=r["dbs"][:, :ng].T,
        pool_scale=redg[0], ln1_g=r["red1"][2], ln1_b=r["red1"][3],
        b_up=jnp.concatenate([redf[:, 4, :].reshape(-1), redf[:, 5, :].reshape(-1)]),
        conv_ffn=jnp.transpose(redf[:, 0:3, :], (1, 0, 2)).reshape(3, -1), conv_ffn_b=redf[:, 3, :].reshape(-1),
        ln2_g=r["red2"][2], ln2_b=r["red2"][3])


def _small_payload(reds):
    smalls = [_small_grads(r) for r in reds]
    order = SMALL_REPLICATED + SMALL_SHARDED
    flat = jnp.concatenate([smalls[l][n].reshape(-1) for n in order for l in range(len(reds))])
    return _as_rows(flat)[None]


def _layer_weights(l, ada, conv_a, conv_ffn, p):
    sh1, sc1, gt1, sh2, sc2, gt2 = (ada[l, k][None, :] for k in range(6))
    nb = N_DEV
    fs = p["b_up"].shape[1] // nb
    nj = nb // 2
    pos = jnp.arange(GMLP_BLOCK)
    allowed = (pos[None, :] // CHUNK) <= (pos[:, None] // CHUNK)
    wmask = jnp.where(allowed[None], p["w_spatial"][l], 0.0)
    return dict(
        mod1=jnp.concatenate([1.0 + sc1, sh1]), mod2=jnp.concatenate([1.0 + sc2, sh2]),
        ln1=jnp.concatenate([gt1, p["ln1_g"][l][None], p["ln1_b"][l][None]]),
        ln2=jnp.concatenate([gt2, p["ln2_g"][l][None], p["ln2_b"][l][None]]),
        b_in8=p["b_in"][l].reshape(N_DEV, 1, -1), b_up8=p["b_up"][l].reshape(nb, 1, fs),
        conv_a=conv_a[l], lnv=jnp.stack([p["ln_v_g"][l], p["ln_v_b"][l]]),
        wm=wmask.astype(BF), wmt=jnp.transpose(wmask, (0, 2, 1)).astype(BF),
        bias_full=jnp.repeat(p["b_spatial"][l].T, GMLP_BLOCK, axis=1), mask=allowed.astype(F32),
        pool_scale=p["pool_scale"][l][None],
        cw=jnp.transpose(conv_ffn[l].reshape(3, nj, fs), (1, 0, 2)), cb=p["conv_ffn_b"][l].reshape(nj, 1, fs))


ANY = pl.BlockSpec(memory_space=pl.ANY)


def _place():
    x, y, c = lax.axis_index("x"), lax.axis_index("y"), lax.axis_index("c")
    chips = [(1 - x, y), (x, 1 - y), (1 - x, 1 - y)]
    return x, y, c, chips


def _allgather_vmem(xs, *, name):
    r, cdim = xs.shape

    def body(x_ref, out_ref, send_sems, recv_sems, local_sem):
        x, y, c, chips = _place()
        me, sibling = (x, y, c), (x, y, 1 - c)

        def rows(px, py, pc):
            return out_ref.at[pl.ds((4 * px + 2 * py + pc) * r, r), :]

        def copy(k, block, to, src=None):
            return pltpu.make_async_remote_copy(
                src_ref=rows(*block) if src is None else src, dst_ref=rows(*block),
                send_sem=send_sems.at[k], recv_sem=recv_sems.at[k], device_id=to, device_id_type=MESH)

        mine = pltpu.make_async_copy(x_ref, rows(*me), local_sem)
        mine.start()
        first = [copy(0, me, sibling, src=x_ref)]
        first += [copy(1 + j, me, (*chip, c), src=x_ref) for j, chip in enumerate(chips)]
        for cp in first:
            cp.start()
        passed = [copy(4 + j, (*chip, c), sibling) for j, chip in enumerate(chips)]
        for j, chip in enumerate(chips):
            copy(1 + j, (*chip, c), me).wait_recv()
            passed[j].start()
        copy(0, sibling, me).wait_recv()
        for j, chip in enumerate(chips):
            copy(4 + j, (*chip, 1 - c), me).wait_recv()
        for cp in first + passed:
            cp.wait_send()
        mine.wait()

    return pl.pallas_call(
        body, name=name, out_shape=jax.ShapeDtypeStruct((N_DEV * r, cdim), xs.dtype),
        in_specs=[pl.BlockSpec(memory_space=pltpu.VMEM)], out_specs=pl.BlockSpec(memory_space=pltpu.VMEM),
        scratch_shapes=[pltpu.SemaphoreType.DMA((7,)), pltpu.SemaphoreType.DMA((7,)), pltpu.SemaphoreType.DMA],
        compiler_params=_params(),
    )(xs)


def _gather_weights(shards, *, name):
    n = len(shards)

    def body(*refs):
        ins, outs = refs[:n], refs[n:2 * n]
        send_sems, recv_sems, local_sems = refs[2 * n:]
        x, y, c, chips = _place()
        me, sibling = (x, y, c), (x, y, 1 - c)

        def slot(a, px, py, pc):
            return outs[a].at[:, 4 * px + 2 * py + pc]

        def copy(a, k, block, to, src=None):
            return pltpu.make_async_remote_copy(
                src_ref=slot(a, *block) if src is None else src, dst_ref=slot(a, *block),
                send_sem=send_sems.at[7 * a + k], recv_sem=recv_sems.at[7 * a + k], device_id=to,
                device_id_type=MESH)

        mine = [pltpu.make_async_copy(ins[a], slot(a, *me), local_sems.at[a]) for a in range(n)]
        for cp in mine:
            cp.start()
        first = []
        for j, chip in enumerate(chips):
            first += [copy(a, 1 + j, me, (*chip, c), src=ins[a]) for a in range(n)]
        first += [copy(a, 0, me, sibling, src=ins[a]) for a in range(n)]
        for cp in first:
            cp.start()
        passed = []
        for j, chip in enumerate(chips):
            for a in range(n):
                copy(a, 1 + j, (*chip, c), me).wait_recv()
                fwd = copy(a, 4 + j, (*chip, c), sibling)
                fwd.start()
                passed.append(fwd)
        for a in range(n):
            copy(a, 0, sibling, me).wait_recv()
        for j, chip in enumerate(chips):
            for a in range(n):
                copy(a, 4 + j, (*chip, 1 - c), me).wait_recv()
        for cp in first + passed:
            cp.wait_send()
        for cp in mine:
            cp.wait()

    out_shape = [jax.ShapeDtypeStruct((s.shape[0], N_DEV) + s.shape[1:], s.dtype) for s in shards]
    return pl.pallas_call(
        body, name=name, out_shape=out_shape, in_specs=[ANY] * n, out_specs=[ANY] * n,
        scratch_shapes=[pltpu.SemaphoreType.DMA((7 * n,)), pltpu.SemaphoreType.DMA((7 * n,)),
                        pltpu.SemaphoreType.DMA((n,))],
        compiler_params=_params(),
    )(*shards)


def _pick_tile(r, cap):
    best = None
    for t in range(8, min(r, cap) + 1, 8):
        if r % t == 0:
            best = t
    return best if best is not None else r


def _pair_sum(g, r1, cidx, *, name):
    p, _, r, cdim = g.shape
    tr = _pick_tile(r, 256)

    def body(c_ref, g_ref, r_ref, o_ref):
        del c_ref
        o_ref[...] = (g_ref[...].astype(F32) + r_ref[...].astype(F32)).astype(BF)

    grid_spec = pltpu.PrefetchScalarGridSpec(
        num_scalar_prefetch=1, grid=(4, r // tr),
        in_specs=[pl.BlockSpec((p, None, tr, cdim), lambda q, i, c: (0, 2 * q + c[0], i, 0)),
                  pl.BlockSpec((None, p, tr, cdim), lambda q, i, c: (q, 0, i, 0))],
        out_specs=pl.BlockSpec((None, p, tr, cdim), lambda q, i, c: (q, 0, i, 0)))
    return pl.pallas_call(
        body, name=name, grid_spec=grid_spec, out_shape=jax.ShapeDtypeStruct((4, p, r, cdim), BF),
        compiler_params=_params(("arbitrary", "arbitrary")),
    )(cidx, g, r1)


def _ada_fwd(c_all, w_ada, *, name):
    depth, d, ns = w_ada.shape
    nb = c_all.shape[0]

    def body(c_ref, w_ref, o_ref):
        cv = c_ref[...]
        act = cv * jax.nn.sigmoid(cv)
        o_ref[...] = jnp.dot(act, w_ref[...], preferred_element_type=F32, precision=lax.Precision.HIGHEST)

    return pl.pallas_call(
        body, name=name, grid=(depth,),
        in_specs=[pl.BlockSpec((nb, d), lambda l: (0, 0)), pl.BlockSpec((None, d, ns), lambda l: (l, 0, 0))],
        out_specs=pl.BlockSpec((None, nb, ns), lambda l: (l, 0, 0)),
        out_shape=jax.ShapeDtypeStruct((depth, nb, ns), F32), compiler_params=_params(("parallel",)),
    )(c_all, w_ada)


def _ada_bwd(ct, dmine, dall, *, name):
    depth, nb, ns = dmine.shape
    d = ct.shape[0]

    def body(ct_ref, dm_ref, da_ref, gw_ref, gb_ref):
        cv = ct_ref[...]
        act = cv * jax.nn.sigmoid(cv)
        gw_ref[...] = jnp.dot(act, dm_ref[...], preferred_element_type=F32, precision=lax.Precision.HIGHEST)
        s = da_ref[0]
        for b in range(1, nb):
            s = s + da_ref[b]
        gb_ref[...] = s

    return pl.pallas_call(
        body, name=name, grid=(depth,),
        in_specs=[pl.BlockSpec((d, nb), lambda l: (0, 0)), pl.BlockSpec((None, nb, ns), lambda l: (l, 0, 0)),
                  pl.BlockSpec(dall.shape, lambda l: (0, 0, 0))],
        out_specs=[pl.BlockSpec((None, d, ns), lambda l: (l, 0, 0)), pl.BlockSpec(dall.shape[1:], lambda l: (0, 0))],
        out_shape=[jax.ShapeDtypeStruct((depth, d, ns), F32), jax.ShapeDtypeStruct(dall.shape[1:], F32)],
        compiler_params=_params(("arbitrary",)),
    )(ct, dmine, dall)


def _sum_parts(parts, *, name):
    p, r, cdim = parts.shape
    tr = _pick_tile(r, 512)

    def body(p_ref, o_ref):
        s = p_ref[0]
        for k in range(1, p):
            s = s + p_ref[k]
        o_ref[...] = s

    return pl.pallas_call(
        body, name=name, grid=(r // tr,),
        in_specs=[pl.BlockSpec((p, tr, cdim), lambda i: (0, i, 0))], out_specs=pl.BlockSpec((tr, cdim), lambda i: (i, 0)),
        out_shape=jax.ShapeDtypeStruct((r, cdim), F32), compiler_params=_params(("parallel",)),
    )(parts)


def _adamw(parts, w, m, v, *, name):
    p, depth, r, cdim = parts.shape
    tr = _pick_tile(r, 256)

    def body(p_ref, w_ref, m_ref, v_ref, g_out, d_out, m_out, v_out):
        g = p_ref[0].astype(F32)
        for k in range(1, p):
            g = g + p_ref[k].astype(F32)
        m2 = ADAM_B1 * m_ref[...] + (1.0 - ADAM_B1) * g
        v2 = ADAM_B2 * v_ref[...] + (1.0 - ADAM_B2) * (g * g)
        m_hat = m2 / (1.0 - ADAM_B1 ** ADAM_STEP)
        v_hat = v2 / (1.0 - ADAM_B2 ** ADAM_STEP)
        g_out[...] = g
        d_out[...] = -ADAM_LR * (m_hat / (jnp.sqrt(v_hat) + ADAM_EPS) + ADAM_WD * w_ref[...])
        m_out[...] = m2
        v_out[...] = v2

    blk = pl.BlockSpec((None, tr, cdim), lambda l, i: (l, i, 0))
    out = jax.ShapeDtypeStruct((depth, r, cdim), F32)
    return pl.pallas_call(
        body, name=name, grid=(depth, r // tr),
        in_specs=[pl.BlockSpec((p, None, tr, cdim), lambda l, i: (0, l, i, 0)), blk, blk, blk],
        out_specs=[blk, blk, blk, blk], out_shape=[out, out, out, out],
        compiler_params=_params(("parallel", "parallel")),
    )(parts, w, m, v)


BIG = ("w_in", "w_a_out", "w_b_out", "w_pool", "w_o", "w_up", "w_down")
SMALL_REPLICATED = ("b_in", "ln_v_g", "ln_v_b", "w_spatial", "b_spatial", "pool_scale", "ln1_g", "ln1_b", "b_up",
                    "conv_ffn_b", "ln2_g", "ln2_b")
SMALL_SHARDED = ("conv_a", "conv_ffn")
WEIGHTS = ("w_ada", "b_ada", "w_in", "b_in", "conv_a", "w_a_out", "ln_v_g", "ln_v_b", "w_spatial", "b_spatial",
           "w_b_out", "w_pool", "pool_scale", "w_o", "ln1_g", "ln1_b", "w_up", "b_up", "conv_ffn", "conv_ffn_b",
           "w_down", "ln2_g", "ln2_b")
LANES = 128


def _as_rows(flat, mult=8):
    n = flat.shape[0]
    pad = (-n) % (LANES * mult)
    if pad:
        flat = jnp.concatenate([flat, jnp.zeros((pad,), flat.dtype)])
    return flat.reshape(-1, LANES)


def _shard3(a):
    return a.reshape((-1,) + a.shape[-2:])


def kernel(x, c, w_ada, b_ada, w_in, b_in, conv_a, w_a_out, ln_v_g, ln_v_b, w_spatial, b_spatial, w_b_out, w_pool, pool_scale, w_o, ln1_g, ln1_b, w_up, b_up, conv_ffn, conv_ffn_b, w_down, ln2_g, ln2_b, loss_target, m_w_ada, m_b_ada, m_w_in, m_b_in, m_conv_a, m_w_a_out, m_ln_v_g, m_ln_v_b, m_w_spatial, m_b_spatial, m_w_b_out, m_w_pool, m_pool_scale, m_w_o, m_ln1_g, m_ln1_b, m_w_up, m_b_up, m_conv_ffn, m_conv_ffn_b, m_w_down, m_ln2_g, m_ln2_b, v_w_ada, v_b_ada, v_w_in, v_b_in, v_conv_a, v_w_a_out, v_ln_v_g, v_ln_v_b, v_w_spatial, v_b_spatial, v_w_b_out, v_w_pool, v_pool_scale, v_w_o, v_ln1_g, v_ln1_b, v_w_up, v_b_up, v_conv_ffn, v_conv_ffn_b, v_w_down, v_ln2_g, v_ln2_b):
    p = dict(locals())
    depth, d = w_in.shape[0], w_in.shape[1]
    alpha = (2 * depth) ** 0.25
    me = 4 * lax.axis_index("x") + 2 * lax.axis_index("y") + lax.axis_index("c")
    cidx = lax.axis_index("c").astype(jnp.int32).reshape(1)

    n_ca, n_cf = conv_a.size, conv_ffn.size
    packed = _as_rows(jnp.concatenate([c.reshape(-1), conv_a.reshape(-1), conv_ffn.reshape(-1)]))
    got = _allgather_vmem(packed, name="gather_cond").reshape(N_DEV, -1)
    c_all = got[:, :d]
    ct = c_all.T
    conv_a_full = jnp.transpose(got[:, d:d + n_ca].reshape((N_DEV,) + conv_a.shape), (1, 2, 0, 3)).reshape(depth, 3, -1)
    conv_ffn_full = jnp.transpose(got[:, d + n_ca:d + n_ca + n_cf].reshape((N_DEV,) + conv_ffn.shape),
                                  (1, 2, 0, 3)).reshape(depth, 3, -1)

    ns_ada = w_ada.shape[2]
    ada_part = _ada_fwd(c_all, w_ada, name="ada_fwd")
    ada_all = _allgather_vmem(_as_rows(ada_part.reshape(-1)), name="gather_ada")
    ada_all = ada_all.reshape(N_DEV, depth, N_DEV, ns_ada)
    ada_mine = lax.dynamic_index_in_dim(ada_all, me, axis=2, keepdims=False)
    ada = jnp.transpose(ada_mine, (1, 0, 2)).reshape(depth, -1) + b_ada
    ada = ada.reshape(depth, 6, d)

    shards = [{n: _shard3(p[n][l].astype(BF)) for n in BIG} for l in range(depth)]
    ws = [_layer_weights(l, ada, conv_a_full, conv_ffn_full, p) for l in range(depth)]
    ws[0]["w_in8"] = _gather_weights([shards[0]["w_in"]], name="gather_w_in0")[0][0]
    plan = _GatherPlan()
    four = ("w_a_out", "w_b_out", "w_pool", "w_o")
    for l in range(depth):
        t, prev, sh = "_l%d" % l, "_l%d" % (l - 1), shards[l]
        if l == 0:
            for n in four:
                plan.add(n + t, sh[n], "in_proj" + t, "mix_fwd" + t)
            plan.add("w_up8" + t, sh["w_up"], "in_proj" + t, "mix_fwd" + t)
            plan.add("wd4" + t, sh["w_down"], "mix_fwd" + t, "up_proj" + t)
        else:
            plan.add("w_in8" + t, sh["w_in"], "ffn_fwd" + prev, "down_proj" + prev)
            for n in four:
                plan.add(n + t, sh[n], "up_proj" + prev, "ffn_fwd" + prev)
            plan.add("w_up8" + t, sh["w_up"], "in_proj" + t, "mix_fwd" + t)
            plan.add("wd4" + t, sh["w_down"], "in_proj" + t, "mix_fwd" + t)

    rs = _ReduceScatter(depth, cidx)
    loss_blk, grad_x, bigs, reds, d_ada = _local_step(x[0], loss_target[0], ws, alpha, plan, rs)
    loss = lax.psum(loss_blk[0, 0], ("x", "y", "c"))

    dada_all = _allgather_vmem(_as_rows(d_ada.reshape(-1)), name="gather_dada")
    dada_all = dada_all.reshape(N_DEV, -1, LANES)
    dflat = dada_all.reshape(N_DEV, depth, 6 * d)
    dmine = lax.dynamic_slice_in_dim(dflat, me * ns_ada, ns_ada, axis=2)
    gw_ada, gb_rows = _ada_bwd(ct, jnp.transpose(dmine, (1, 0, 2)), dada_all, name="ada_bwd")
    gb_ada = gb_rows.reshape(-1)[:depth * 6 * d].reshape(depth, 6 * d)

    out = {}
    for n in BIG:
        parts = rs.acc[n]
        view = (lambda a: jnp.swapaxes(a, 1, 2)) if n == "w_up" else (lambda a: a)
        shard_shape = view(p[n]).shape
        w3 = view(p[n]).reshape(depth, -1, shard_shape[-1])
        parts4 = parts.reshape((4,) + w3.shape)
        res = _adamw(parts4, w3, view(p["m_" + n]).reshape(w3.shape), view(p["v_" + n]).reshape(w3.shape),
                     name="adamw_" + n)
        out[n] = [view(r.reshape(shard_shape)) for r in res]
    out["w_ada"] = _adamw(gw_ada[None], w_ada, m_w_ada, v_w_ada, name="adamw_w_ada")

    order = SMALL_REPLICATED + SMALL_SHARDED
    n_rep = sum(p[n].size for n in SMALL_REPLICATED)
    n_pay = n_rep + N_DEV * (conv_a.size + conv_ffn.size)
    gsum = _sum_parts(rs.small_gathered.reshape(N_DEV, -1, LANES), name="sum_small").reshape(-1)[:n_pay]
    ga_full = gsum[n_rep:n_rep + depth * 3 * d].reshape(depth, 3, d)
    gf_full = gsum[n_rep + depth * 3 * d:].reshape(depth, 3, -1)
    ca_w, cf_w = conv_a.shape[2], conv_ffn.shape[2]
    g_ca = lax.dynamic_slice_in_dim(ga_full, me * ca_w, ca_w, axis=2)
    g_cf = lax.dynamic_slice_in_dim(gf_full, me * cf_w, cf_w, axis=2)
    names = ("b_ada",) + order
    gflat = _as_rows(jnp.concatenate([gb_ada.reshape(-1), gsum[:n_rep], g_ca.reshape(-1), g_cf.reshape(-1)]))
    pack = lambda pre: _as_rows(jnp.concatenate([p[pre + n].reshape(-1) for n in names]))
    res = _adamw(gflat[None, None], pack("")[None], pack("m_")[None], pack("v_")[None], name="adamw_small")
    off = 0
    for n in names:
        size = p[n].size
        out[n] = [r.reshape(-1)[off:off + size].reshape(p[n].shape) for r in res]
        off += size

    return (loss, grad_x[None]) + tuple(out[n][k] for k in range(4) for n in WEIGHTS)
```

```python
import functools

import jax
import jax.numpy as jnp
from jax import lax
from jax.experimental import pallas as pl
from jax.experimental.pallas import tpu as pltpu

F32 = jnp.float32
BF = jnp.bfloat16
MESH = pl.DeviceIdType.MESH

LN_EPS = 1e-5
POOL_WINDOWS = (2, 4, 8, 16)
GMLP_BLOCK = 128
CHUNK = 64
HALO = 16
ADAM_LR, ADAM_B1, ADAM_B2, ADAM_EPS, ADAM_WD, ADAM_STEP = 0.001, 0.9, 0.999, 1e-08, 0.01, 10
N_DEV = 8
VMEM_LIMIT = 56 * 1024 * 1024

GRAD_DTYPE = BF
ACT_DTYPE = BF

NN = ((1,), (0,))
NT = ((1,), (1,))
TN = ((0,), (0,))


def _params(sem=None, vmem=VMEM_LIMIT, **kw):
    if sem is not None:
        kw["dimension_semantics"] = sem
    return pltpu.CompilerParams(vmem_limit_bytes=vmem, **kw)


class _Phase:
    def __init__(self, ins, out_shapes, aliases, n_remote, n_local, build):
        self.ins, self.out_shapes, self.aliases = list(ins), list(out_shapes), dict(aliases)
        self.n_remote, self.n_local, self.build = n_remote, n_local, build
        self.results = None


def _pcall(body, args, *, name, grid, in_specs, out_specs, out_shape, scratch_shapes=(), sem=None, aliases=None,
           phases=()):
    aliases = dict(aliases or {})
    if not phases:
        return pl.pallas_call(
            body, name=name, grid=grid, in_specs=list(in_specs), out_specs=out_specs, out_shape=out_shape,
            scratch_shapes=list(scratch_shapes), input_output_aliases=aliases, compiler_params=_params(sem),
        )(*args)
    single = not isinstance(out_shape, (list, tuple))
    o_specs = [out_specs] if single else list(out_specs)
    o_shapes = [out_shape] if single else list(out_shape)
    n_in, n_out, n_scr = len(args), len(o_shapes), len(scratch_shapes)
    ex_args, ex_out, sems = [], [], []
    for ph in phases:
        for src, dst in ph.aliases.items():
            aliases[n_in + len(ex_args) + src] = n_out + len(ex_out) + dst
        ex_args += ph.ins
        ex_out += ph.out_shapes
        sems += [pltpu.SemaphoreType.DMA((max(ph.n_remote, 1),)), pltpu.SemaphoreType.DMA((max(ph.n_remote, 1),)),
                 pltpu.SemaphoreType.DMA((max(ph.n_local, 1),))]

    def wrapped(*refs):
        pos = n_in
        ph_in = []
        for ph in phases:
            ph_in.append(refs[pos:pos + len(ph.ins)])
            pos += len(ph.ins)
        base_out = refs[pos:pos + n_out]
        pos += n_out
        ph_out = []
        for ph in phases:
            ph_out.append(refs[pos:pos + len(ph.out_shapes)])
            pos += len(ph.out_shapes)
        base_scr = refs[pos:pos + n_scr]
        ph_sems = refs[pos + n_scr:]
        first = last = None
        for ax, n in enumerate(grid):
            pid = pl.program_id(ax)
            first = (pid == 0) if first is None else first & (pid == 0)
            last = (pid == n - 1) if last is None else last & (pid == n - 1)

        def ops(k):
            return phases[k].build(ph_in[k], ph_out[k], *ph_sems[3 * k:3 * k + 3])

        @pl.when(first)
        def _():
            for k in range(len(phases)):
                for cp in ops(k)["start"]:
                    cp.start()

        body(*refs[:n_in], *base_out, *base_scr)

        @pl.when(last)
        def _():
            for k in range(len(phases)):
                o = ops(k)
                for cp in o["recv"]:
                    cp.wait_recv()
                for cp in o["send"]:
                    cp.wait_send()
                for cp in o["local"]:
                    cp.wait()

    hbm = pl.BlockSpec(memory_space=pl.ANY)
    res = pl.pallas_call(
        wrapped, name=name, grid=grid, in_specs=list(in_specs) + [hbm] * len(ex_args),
        out_specs=o_specs + [hbm] * len(ex_out), out_shape=o_shapes + ex_out,
        scratch_shapes=list(scratch_shapes) + sems, input_output_aliases=aliases,
        compiler_params=_params(("arbitrary",) * len(grid)),
    )(*args, *ex_args)
    pos = n_out
    for ph in phases:
        ph.results = list(res[pos:pos + len(ph.out_shapes)])
        pos += len(ph.out_shapes)
    return res[0] if single else list(res[:n_out])


def _gelu_parts(x):
    k = 0.7978845608028654
    x2 = x * x
    t = jnp.tanh(k * (x + 0.044715 * (x2 * x)))
    cdf = 0.5 * (1.0 + t)
    dcdf = 0.5 * (1.0 - t * t) * (k * (1.0 + 3.0 * 0.044715 * x2))
    return x * cdf, cdf + x * dcdf


def _gelu(x):
    t = jnp.tanh(0.7978845608028654 * (x + 0.044715 * (x * x * x)))
    return x * (0.5 * (1.0 + t))


def _rowsum(v):
    return jnp.sum(v, axis=0, keepdims=True)


def _ln_stats(r):
    mu = jnp.mean(r, axis=-1, keepdims=True)
    xc = r - mu
    var = jnp.mean(xc * xc, axis=-1, keepdims=True)
    rstd = lax.rsqrt(var + LN_EPS)
    return xc * rstd, rstd


def _ln_bwd(dy, xhat, rstd, gain):
    dxh = dy * gain
    m1 = jnp.mean(dxh, axis=-1, keepdims=True)
    m2 = jnp.mean(dxh * xhat, axis=-1, keepdims=True)
    return rstd * (dxh - m1 - xhat * m2)


def _matmul(a, b, *, dn, grid, a_spec, b_spec, o_spec, out_shape, acc_shape, name, phases=(), into=None):
    nk = grid[2]
    direct = out_shape.dtype == F32

    def body(a_ref, b_ref, *rest):
        o_ref, scratch = (rest[1], rest[2:]) if into is not None else (rest[0], rest[1:])
        prod = lax.dot_general(a_ref[...], b_ref[...], (dn, ((), ())), preferred_element_type=F32)
        if nk == 1:
            o_ref[...] = prod.astype(o_ref.dtype)
            return
        acc = o_ref if direct else scratch[0]
        k = pl.program_id(2)

        @pl.when(k == 0)
        def _():
            acc[...] = prod

        @pl.when(k > 0)
        def _():
            acc[...] += prod

        if not direct:
            @pl.when(k == nk - 1)
            def _():
                o_ref[...] = acc[...].astype(o_ref.dtype)

    scratch = [] if (direct or nk == 1) else [pltpu.VMEM(acc_shape, F32)]
    args, in_specs, aliases = (a, b), [a_spec, b_spec], None
    if into is not None:
        args, in_specs, aliases = (a, b, into), in_specs + [pl.BlockSpec(memory_space=pl.ANY)], {2: 0}
    return _pcall(body, args, name=name, grid=grid, in_specs=in_specs, out_specs=o_spec,
                  out_shape=out_shape, scratch_shapes=scratch, sem=("parallel", "parallel", "arbitrary"),
                  aliases=aliases, phases=phases)


def _row_tile(m, want):
    t = min(m, want)
    assert m % t == 0
    return t


def _mm_rows(a, w, *, dn, name, out_dtype=F32, tm=2048):
    m, k = a.shape
    n = w.shape[1] if dn == NN else w.shape[0]
    tm = _row_tile(m, tm)
    return _matmul(
        a, w, dn=dn, grid=(m // tm, 1, 1), name=name,
        a_spec=pl.BlockSpec((tm, k), lambda i, j, kk: (i, 0)),
        b_spec=pl.BlockSpec(w.shape, lambda i, j, kk: (0, 0)),
        o_spec=pl.BlockSpec((tm, n), lambda i, j, kk: (i, 0)),
        out_shape=jax.ShapeDtypeStruct((m, n), out_dtype), acc_shape=(tm, n))


def _mm_tn(a, b, *, name, tk=2048):
    m, ka = a.shape
    n = b.shape[1]
    tk = _row_tile(m, tk)
    return _matmul(
        a, b, dn=TN, grid=(1, 1, m // tk), name=name,
        a_spec=pl.BlockSpec((tk, ka), lambda i, j, kk: (kk, 0)),
        b_spec=pl.BlockSpec((tk, n), lambda i, j, kk: (kk, 0)),
        o_spec=pl.BlockSpec((ka, n), lambda i, j, kk: (0, 0)),
        out_shape=jax.ShapeDtypeStruct((ka, n), GRAD_DTYPE), acc_shape=(ka, n))


def _mod_matmul(x, mod, w8, bias8, *, flat_out, name, tm=2048, phases=()):
    m, k = x.shape
    nb, _, ns = w8.shape
    tm = _row_tile(m, tm)

    def body(x_ref, mod_ref, w_ref, b_ref, o_ref, h_ref, hs):
        @pl.when(pl.program_id(1) == 0)
        def _():
            h = (x_ref[...] * mod_ref[0:1, :] + mod_ref[1:2, :]).astype(BF)
            hs[...] = h
            h_ref[...] = h

        o_ref[...] = (jnp.dot(hs[...], w_ref[...], preferred_element_type=F32) + b_ref[...]).astype(o_ref.dtype)

    if flat_out:
        o_spec = pl.BlockSpec((tm, ns), lambda i, j: (i, j))
        o_shape = jax.ShapeDtypeStruct((m, nb * ns), ACT_DTYPE)
    else:
        o_spec = pl.BlockSpec((None, tm, ns), lambda i, j: (j, i, 0))
        o_shape = jax.ShapeDtypeStruct((nb, m, ns), ACT_DTYPE)
    return _pcall(
        body, (x, mod, w8, bias8), name=name, grid=(m // tm, nb),
        in_specs=[pl.BlockSpec((tm, k), lambda i, j: (i, 0)),
                  pl.BlockSpec((2, k), lambda i, j: (0, 0)),
                  pl.BlockSpec((None, k, ns), lambda i, j: (j, 0, 0)),
                  pl.BlockSpec((None, 1, ns), lambda i, j: (j, 0, 0))],
        out_specs=[o_spec, pl.BlockSpec((tm, k), lambda i, j: (i, 0))],
        out_shape=[o_shape, jax.ShapeDtypeStruct((m, k), BF)],
        scratch_shapes=[pltpu.VMEM((tm, k), BF)], sem=("parallel", "arbitrary"), phases=phases)


def _seg_spec(tm, d, s):
    return pl.BlockSpec((tm, d), lambda i, s=s: (i, s))


def _prev_halo_spec(tm, d, s):
    hb = tm // HALO
    return pl.BlockSpec((HALO, d), lambda i, s=s: (jnp.maximum(i * hb - 1, 0), s))


def _next_halo_spec(tm, d, s, m):
    hb = tm // HALO
    last = m // HALO - 1
    return pl.BlockSpec((HALO, d), lambda i, s=s: (jnp.minimum((i + 1) * hb, last), s))


def _spatial_mix(wm_ref, src, dst, bias_ref, tm, d):
    for n in range(tm // GMLP_BLOCK):
        for g in range(d // GMLP_BLOCK):
            rs = slice(n * GMLP_BLOCK, (n + 1) * GMLP_BLOCK)
            cs = slice(g * GMLP_BLOCK, (g + 1) * GMLP_BLOCK)
            v = jnp.dot(wm_ref[g], src[rs, cs], preferred_element_type=F32)
            if bias_ref is not None:
                v = v + bias_ref[:, cs]
            dst[rs, cs] = v


def _mix_fwd(z, conv_a, lnv, wm, bias_full, *, name, tm=256, phases=()):
    m, d9 = z.shape
    d = d9 // 9
    tm = _row_tile(m, tm)
    grp = d // len(POOL_WINDOWS)

    def body(zb, zc, zx, zu, zv, zp, zc_h, zx_h, zp_h, ca_ref, lnv_ref, wm_ref, bias_ref,
             ua_ref, ub_ref, d_ref, ext, vn_s, mixed_s):
        i = pl.program_id(0)
        first = i == 0
        f32 = lambda r: r[...].astype(F32)
        pa = f32(zc) * f32(zx)
        ext[0:HALO, :] = jnp.where(first, 0.0, f32(zc_h) * f32(zx_h))
        ext[HALO:HALO + tm, :] = pa
        w = ca_ref[...]
        conv = w[0:1, :] * ext[pl.ds(HALO - 2, tm), :] + w[1:2, :] * ext[pl.ds(HALO - 1, tm), :] + w[2:3, :] * pa
        ua_ref[...] = (f32(zb) * conv).astype(BF)
        p = f32(zp)
        ext[0:HALO, :] = jnp.where(first, 0.0, f32(zp_h))
        ext[HALO:HALO + tm, :] = p
        t = (i * tm + lax.broadcasted_iota(jnp.int32, (tm, 1), 0) + 1).astype(F32)
        for k, win in enumerate(POOL_WINDOWS):
            cs = slice(k * grp, (k + 1) * grp)
            s = p[:, cs]
            for j in range(1, win):
                s = s + ext[pl.ds(HALO - j, tm), cs]
            d_ref[:, cs] = (s / jnp.minimum(t, float(win)) - p[:, cs]).astype(BF)
        gv = _gelu(f32(zv))
        vhat, _ = _ln_stats(gv)
        vn_s[...] = (vhat * lnv_ref[0:1, :] + lnv_ref[1:2, :]).astype(BF)
        _spatial_mix(wm_ref, vn_s, mixed_s, bias_ref, tm, d)
        ub_ref[...] = (_gelu(f32(zu)) * mixed_s[...]).astype(BF)

    full = lambda a: pl.BlockSpec(a.shape, lambda i: (0,) * a.ndim)
    out = jax.ShapeDtypeStruct((m, d), BF)
    o_spec = pl.BlockSpec((tm, d), lambda i: (i, 0))
    return _pcall(
        body, (z, z, z, z, z, z, z, z, z, conv_a, lnv, wm, bias_full), name=name, grid=(m // tm,),
        in_specs=[_seg_spec(tm, d, s) for s in range(6)] + [_prev_halo_spec(tm, d, s) for s in (1, 2, 5)]
        + [full(conv_a), full(lnv), full(wm), full(bias_full)],
        out_specs=[o_spec, o_spec, o_spec], out_shape=[out, out, out],
        scratch_shapes=[pltpu.VMEM((HALO + tm, d), F32), pltpu.VMEM((tm, d), BF), pltpu.VMEM((tm, d), F32)],
        sem=("arbitrary",), phases=phases)


def _pool_proj(dd, w_pool, *, dn, name, out_dtype=F32, tm=512):
    m, d = dd.shape
    ng, grp, _ = w_pool.shape
    tm = _row_tile(m, tm)
    return _matmul(
        dd, w_pool, dn=dn, grid=(m // tm, ng, 1), name=name,
        a_spec=pl.BlockSpec((tm, grp), lambda i, j, kk: (i, j)),
        b_spec=pl.BlockSpec((None, grp, grp), lambda i, j, kk: (j, 0, 0)),
        o_spec=pl.BlockSpec((tm, grp), lambda i, j, kk: (i, j)),
        out_shape=jax.ShapeDtypeStruct((m, d), out_dtype), acc_shape=(tm, grp))


def _merge(z, ya, yb, ycp, scale, *, name, tm=512):
    m, d = ya.shape
    tm = _row_tile(m, tm)

    def body(ga, gb, gc, ya_ref, yb_ref, yc_ref, sc_ref, o_ref):
        f32 = lambda r: r[...].astype(F32)
        o_ref[...] = (jax.nn.sigmoid(f32(ga)) * f32(ya_ref) + jax.nn.sigmoid(f32(gb)) * f32(yb_ref)
                      + jax.nn.sigmoid(f32(gc)) * (f32(yc_ref) * sc_ref[...])).astype(BF)

    row = pl.BlockSpec((tm, d), lambda i: (i, 0))
    return pl.pallas_call(
        body, name=name, grid=(m // tm,),
        in_specs=[_seg_spec(tm, d, 6), _seg_spec(tm, d, 7), _seg_spec(tm, d, 8), row, row, row,
                  pl.BlockSpec((1, d), lambda i: (0, 0))],
        out_specs=row, out_shape=jax.ShapeDtypeStruct((m, d), BF),
        compiler_params=_params(("parallel",)),
    )(z, z, z, ya, yb, ycp, scale)


def _resid_ln(xp, ys, vec, alpha, *, name, tm=512):
    m, d = xp.shape
    tm = _row_tile(m, tm)

    def body(xp_ref, ys_ref, v_ref, o_ref):
        xhat, _ = _ln_stats(alpha * xp_ref[...] + v_ref[0:1, :] * ys_ref[...])
        o_ref[...] = xhat * v_ref[1:2, :] + v_ref[2:3, :]

    row = pl.BlockSpec((tm, d), lambda i: (i, 0))
    return pl.pallas_call(
        body, name=name, grid=(m // tm,),
        in_specs=[row, row, pl.BlockSpec(vec.shape, lambda i: (0, 0))],
        out_specs=row, out_shape=jax.ShapeDtypeStruct((m, d), F32),
        compiler_params=_params(("parallel",)),
    )(xp, ys, vec)


def _ffn_fwd(up4, cw, cb, *, name, tm=512, phases=()):
    _, nj, m, fs = up4.shape
    tm = _row_tile(m, tm)
    hb = tm // HALO

    def body(up_ref, ah_ref, cw_ref, cb_ref, f_ref, ext):
        first = pl.program_id(1) == 0
        ext[0:HALO, :] = jnp.where(first, 0.0, ah_ref[...].astype(F32))
        ext[HALO:HALO + tm, :] = up_ref[0].astype(F32)
        w = cw_ref[...]
        w0, w1, w2, bias = w[0:1, :], w[1:2, :], w[2:3, :], cb_ref[...]
        rc = 16

        def step(c, carry):
            r0 = pl.multiple_of(c * rc, rc)
            win = ext[pl.ds(r0 + HALO - 8, rc + 8), :]
            a0, a1, a2 = win[8:8 + rc], pltpu.roll(win, 1, 0)[8:8 + rc], pltpu.roll(win, 2, 0)[8:8 + rc]
            ca = w0 * a2 + w1 * a1 + w2 * a0 + bias
            f_ref[pl.ds(r0, rc), :] = (_gelu(ca) * up_ref[1, pl.ds(r0, rc), :].astype(F32)).astype(BF)
            return carry

        lax.fori_loop(0, tm // rc, step, 0)

    return _pcall(
        body, (up4, up4, cw, cb), name=name, grid=(nj, m // tm),
        in_specs=[pl.BlockSpec((2, None, tm, fs), lambda j, i: (0, j, i, 0)),
                  pl.BlockSpec((None, None, HALO, fs), lambda j, i: (0, j, jnp.maximum(i * hb - 1, 0), 0)),
                  pl.BlockSpec((None, 3, fs), lambda j, i: (j, 0, 0)),
                  pl.BlockSpec((None, 1, fs), lambda j, i: (j, 0, 0))],
        out_specs=pl.BlockSpec((None, tm, fs), lambda j, i: (j, i, 0)),
        out_shape=jax.ShapeDtypeStruct((nj, m, fs), BF),
        scratch_shapes=[pltpu.VMEM((HALO + tm, fs), F32)], sem=("parallel", "arbitrary"), phases=phases)


def _down_proj(f4, wd4, *, name, tm=2048, phases=()):
    nj, m, fs = f4.shape
    d = wd4.shape[2]
    tm = _row_tile(m, tm)
    return _matmul(
        f4, wd4, dn=NN, grid=(m // tm, 1, nj), name=name,
        a_spec=pl.BlockSpec((None, tm, fs), lambda i, j, kk: (kk, i, 0)),
        b_spec=pl.BlockSpec((None, fs, d), lambda i, j, kk: (kk, 0, 0)),
        o_spec=pl.BlockSpec((tm, d), lambda i, j, kk: (i, 0)),
        out_shape=jax.ShapeDtypeStruct((m, d), F32), acc_shape=(tm, d), phases=phases)


def _loss_grad(y, tgt, *, name, tm=512):
    m, d = y.shape
    tm = _row_tile(m, tm)
    ni = m // tm

    def body(y_ref, t_ref, dy_ref, l_ref, acc):
        i = pl.program_id(0)
        e = y_ref[...] - t_ref[...]
        dy_ref[...] = e * (1.0 / d)
        part = jnp.sum((e * e).reshape(tm // 8, 8, d), axis=0)

        @pl.when(i == 0)
        def _():
            acc[...] = part

        @pl.when(i > 0)
        def _():
            acc[...] += part

        @pl.when(i == ni - 1)
        def _():
            l_ref[...] = jnp.full((8, 128), 0.5 / d, F32) * jnp.sum(acc[...])

    row = pl.BlockSpec((tm, d), lambda i: (i, 0))
    return pl.pallas_call(
        body, name=name, grid=(ni,), in_specs=[row, row],
        out_specs=[row, pl.BlockSpec((8, 128), lambda i: (0, 0))],
        out_shape=[jax.ShapeDtypeStruct((m, d), F32), jax.ShapeDtypeStruct((8, 128), F32)],
        scratch_shapes=[pltpu.VMEM((8, d), F32)],
        compiler_params=_params(("arbitrary",)),
    )(y, tgt)


def _resid_ln_bwd(dpart, dh, xmod, mvec, xp, ys, vec, alpha, *, name, tm=256, phases=()):
    m, d = dpart.shape
    tm = _row_tile(m, tm)
    has_dh = dh is not None
    has_ln = xp is not None

    def body(*refs):
        refs = list(refs)
        dpart_ref = refs.pop(0)
        if has_dh:
            dh_ref, xm_ref, mv_ref = refs.pop(0), refs.pop(0), refs.pop(0)
        if has_ln:
            xp_ref, ys_ref, v_ref = refs.pop(0), refs.pop(0), refs.pop(0)
            dys_ref, dxp_ref, red_ref = refs
        else:
            dx_ref, red_ref = refs
        i = pl.program_id(0)
        dtot = dpart_ref[...]
        rows = [jnp.zeros((1, d), F32)] * 5
        if has_dh:
            dhv = dh_ref[...]
            dtot = dtot + dhv * mv_ref[...]
            rows[0] = _rowsum(dhv * xm_ref[...])
            rows[1] = _rowsum(dhv)
        if has_ln:
            ys = ys_ref[...]
            gt = v_ref[0:1, :]
            xhat, rstd = _ln_stats(alpha * xp_ref[...] + gt * ys)
            rows[2] = _rowsum(dtot * xhat)
            rows[3] = _rowsum(dtot)
            dr = _ln_bwd(dtot, xhat, rstd, v_ref[1:2, :])
            rows[4] = _rowsum(dr * ys)
            dys_ref[...] = (dr * gt).astype(BF)
            dxp_ref[...] = alpha * dr
        else:
            dx_ref[...] = dtot
        red = jnp.concatenate(rows + [jnp.zeros((3, d), F32)], axis=0)

        @pl.when(i == 0)
        def _():
            red_ref[...] = red

        @pl.when(i > 0)
        def _():
            red_ref[...] += red

    row = pl.BlockSpec((tm, d), lambda i: (i, 0))
    vrow = lambda a: pl.BlockSpec(a.shape, lambda i: (0, 0))
    args, specs = [dpart], [row]
    if has_dh:
        args += [dh, xmod, mvec]
        specs += [row, row, vrow(mvec)]
    if has_ln:
        args += [xp, ys, vec]
        specs += [row, row, vrow(vec)]
        out_specs = [row, row, pl.BlockSpec((8, d), lambda i: (0, 0))]
        out_shape = [jax.ShapeDtypeStruct((m, d), BF), jax.ShapeDtypeStruct((m, d), F32),
                     jax.ShapeDtypeStruct((8, d), F32)]
    else:
        out_specs = [row, pl.BlockSpec((8, d), lambda i: (0, 0))]
        out_shape = [jax.ShapeDtypeStruct((m, d), F32), jax.ShapeDtypeStruct((8, d), F32)]
    return _pcall(body, args, name=name, grid=(m // tm,), in_specs=specs, out_specs=out_specs, out_shape=out_shape,
                  sem=("arbitrary",), phases=phases)


def _down_bwd(dy, wd4, *, name, tm=2048, phases=()):
    m, d = dy.shape
    nj, fs, _ = wd4.shape
    tm = _row_tile(m, tm)
    return _matmul(
        dy, wd4, dn=NT, grid=(m // tm, nj, 1), name=name,
        a_spec=pl.BlockSpec((tm, d), lambda i, j, kk: (i, 0)),
        b_spec=pl.BlockSpec((None, fs, d), lambda i, j, kk: (j, 0, 0)),
        o_spec=pl.BlockSpec((None, tm, fs), lambda i, j, kk: (j, i, 0)),
        out_shape=jax.ShapeDtypeStruct((nj, m, fs), ACT_DTYPE), acc_shape=(tm, fs), phases=phases)


def _tn_shards_lhs(f4, dy, *, name, tk=2048, phases=()):
    nj, m, fs = f4.shape
    d = dy.shape[1]
    tk = _row_tile(m, tk)
    return _matmul(
        f4, dy, dn=TN, grid=(nj, 1, m // tk), name=name,
        a_spec=pl.BlockSpec((None, tk, fs), lambda i, j, kk: (i, kk, 0)),
        b_spec=pl.BlockSpec((tk, d), lambda i, j, kk: (kk, 0)),
        o_spec=pl.BlockSpec((None, fs, d), lambda i, j, kk: (i, 0, 0)),
        out_shape=jax.ShapeDtypeStruct((nj, fs, d), GRAD_DTYPE), acc_shape=(fs, d), phases=phases)


def _tn_shards_rhs(h, d8, *, name, tk=2048, phases=()):
    m, k = h.shape
    nb, _, ns = d8.shape
    tk = _row_tile(m, tk)
    return _matmul(
        h, d8, dn=TN, grid=(nb, 1, m // tk), name=name,
        a_spec=pl.BlockSpec((tk, k), lambda i, j, kk: (kk, 0)),
        b_spec=pl.BlockSpec((None, tk, ns), lambda i, j, kk: (i, kk, 0)),
        o_spec=pl.BlockSpec((None, k, ns), lambda i, j, kk: (i, 0, 0)),
        out_shape=jax.ShapeDtypeStruct((nb, k, ns), GRAD_DTYPE), acc_shape=(k, ns), phases=phases)


def _tn_cols_rhs(h, dz, nb, *, name, tk=2048, phases=()):
    m, k = h.shape
    ns = dz.shape[1] // nb
    tk = _row_tile(m, tk)
    return _matmul(
        h, dz, dn=TN, grid=(nb, 1, m // tk), name=name,
        a_spec=pl.BlockSpec((tk, k), lambda i, j, kk: (kk, 0)),
        b_spec=pl.BlockSpec((tk, ns), lambda i, j, kk: (kk, i)),
        o_spec=pl.BlockSpec((None, k, ns), lambda i, j, kk: (i, 0, 0)),
        out_shape=jax.ShapeDtypeStruct((nb, k, ns), GRAD_DTYPE), acc_shape=(k, ns), phases=phases)


def _nt_shards(d8, w8, *, name, tm=2048, phases=()):
    nb, m, ns = d8.shape
    k = w8.shape[1]
    tm = _row_tile(m, tm)
    return _matmul(
        d8, w8, dn=NT, grid=(m // tm, 1, nb), name=name,
        a_spec=pl.BlockSpec((None, tm, ns), lambda i, j, kk: (kk, i, 0)),
        b_spec=pl.BlockSpec((None, k, ns), lambda i, j, kk: (kk, 0, 0)),
        o_spec=pl.BlockSpec((tm, k), lambda i, j, kk: (i, 0)),
        out_shape=jax.ShapeDtypeStruct((m, k), F32), acc_shape=(tm, k), phases=phases)


def _nt_cols(dz, w8, *, name, tm=2048, phases=(), tiles=None, into=None):
    m = dz.shape[0]
    nb, k, ns = w8.shape
    tm = _row_tile(m, tm)
    first, count = tiles if tiles is not None else (0, m // tm)
    return _matmul(
        dz, w8, dn=NT, grid=(count, 1, nb), name=name,
        a_spec=pl.BlockSpec((tm, ns), lambda i, j, kk: (i + first, kk)),
        b_spec=pl.BlockSpec((None, k, ns), lambda i, j, kk: (kk, 0, 0)),
        o_spec=pl.BlockSpec((tm, k), lambda i, j, kk: (i + first, 0)),
        out_shape=jax.ShapeDtypeStruct((m, k), F32), acc_shape=(tm, k), phases=phases, into=into)


def _tn_pool(dd, dyc, ng, *, name, tk=2048):
    m, d = dd.shape
    grp = d // ng
    tk = _row_tile(m, tk)
    return _matmul(
        dd, dyc, dn=TN, grid=(ng, 1, m // tk), name=name,
        a_spec=pl.BlockSpec((tk, grp), lambda i, j, kk: (kk, i)),
        b_spec=pl.BlockSpec((tk, grp), lambda i, j, kk: (kk, i)),
        o_spec=pl.BlockSpec((None, grp, grp), lambda i, j, kk: (i, 0, 0)),
        out_shape=jax.ShapeDtypeStruct((ng, grp, grp), GRAD_DTYPE), acc_shape=(grp, grp))


def _ffn_bwd(up4, df4, cw, cb, *, name, tm=256, phases=()):
    _, nj, m, fs = up4.shape
    tm = _row_tile(m, tm)
    hb = tm // HALO
    ni = m // tm
    last_hb = m // HALO - 1
    ext_rows = tm + 8

    rc = 16
    assert tm % rc == 0

    def body(up_ref, ap_ref, un_ref, df_ref, dfn_ref, cw_ref, cb_ref, dup_ref, red_ref, ext, dca_s, racc):
        i = pl.program_id(1)
        ext[0:HALO, :] = jnp.where(i == 0, 0.0, ap_ref[...].astype(F32))
        ext[HALO:HALO + tm, :] = up_ref[0].astype(F32)
        ext[HALO + tm:2 * HALO + tm, :] = un_ref[0].astype(F32)
        racc[...] = jnp.zeros_like(racc)
        w = cw_ref[...]
        w0, w1, w2, bias = w[0:1, :], w[1:2, :], w[2:3, :], cb_ref[...]

        def conv_taps(win, n):
            return (win[8:8 + n], pltpu.roll(win, 1, 0)[8:8 + n], pltpu.roll(win, 2, 0)[8:8 + n])

        def fold(v):
            return v[0:8] + v[8:16]

        def add_red(k, v8):
            racc[8 * k:8 * k + 8, :] += v8

        def first_pass(c, carry):
            r0 = pl.multiple_of(c * rc, rc)
            a0, a1, a2 = conv_taps(ext[pl.ds(r0 + HALO - 8, rc + 8), :], rc)
            act, dact = _gelu_parts(w0 * a2 + w1 * a1 + w2 * a0 + bias)
            dfc = df_ref[pl.ds(r0, rc), :].astype(F32)
            dca = dfc * up_ref[1, pl.ds(r0, rc), :].astype(F32) * dact
            dup_g = dfc * act
            dca_s[pl.ds(r0, rc), :] = dca
            dup_ref[1, pl.ds(r0, rc), :] = dup_g.astype(BF)
            for k, v in enumerate((dca * a2, dca * a1, dca * a0, dca, dup_g)):
                add_red(k if k < 4 else 5, fold(v))
            return carry

        lax.fori_loop(0, tm // rc, first_pass, 0)
        a0, a1, a2 = conv_taps(ext[HALO + tm - 8:HALO + tm + 8, :], 8)
        _, dact = _gelu_parts(w0 * a2 + w1 * a1 + w2 * a0 + bias)
        after = dfn_ref[...].astype(F32)[0:8, :] * un_ref[1].astype(F32)[0:8, :] * dact
        dca_s[tm:tm + 8, :] = jnp.where(i < ni - 1, after, 0.0)
        dca_s[tm + 8:tm + 16, :] = jnp.zeros((8, fs), F32)

        def second_pass(c, carry):
            r0 = pl.multiple_of(c * rc, rc)
            win = dca_s[pl.ds(r0, rc + 8), :]
            up1, up2 = pltpu.roll(win, rc + 7, 0)[0:rc], pltpu.roll(win, rc + 6, 0)[0:rc]
            dup_a = w2 * win[0:rc] + w1 * up1 + w0 * up2
            dup_ref[0, pl.ds(r0, rc), :] = dup_a.astype(BF)
            add_red(4, fold(dup_a))
            return carry

        lax.fori_loop(0, tm // rc, second_pass, 0)
        red = jnp.concatenate([_rowsum(racc[8 * k:8 * k + 8, :]) for k in range(6)] + [jnp.zeros((2, fs), F32)],
                              axis=0)

        @pl.when(i == 0)
        def _():
            red_ref[...] = red

        @pl.when(i > 0)
        def _():
            red_ref[...] += red

    nxt = lambda j, i: jnp.minimum((i + 1) * hb, last_hb)
    return _pcall(
        body, (up4, up4, up4, df4, df4, cw, cb), name=name, grid=(nj, ni), sem=("parallel", "arbitrary"), phases=phases,
        in_specs=[pl.BlockSpec((2, None, tm, fs), lambda j, i: (0, j, i, 0)),
                  pl.BlockSpec((None, None, HALO, fs), lambda j, i: (0, j, jnp.maximum(i * hb - 1, 0), 0)),
                  pl.BlockSpec((2, None, HALO, fs), lambda j, i: (0, j, nxt(j, i), 0)),
                  pl.BlockSpec((None, tm, fs), lambda j, i: (j, i, 0)),
                  pl.BlockSpec((None, HALO, fs), lambda j, i: (j, nxt(j, i), 0)),
                  pl.BlockSpec((None, 3, fs), lambda j, i: (j, 0, 0)),
                  pl.BlockSpec((None, 1, fs), lambda j, i: (j, 0, 0))],
        out_specs=[pl.BlockSpec((2, None, tm, fs), lambda j, i: (0, j, i, 0)),
                   pl.BlockSpec((None, 8, fs), lambda j, i: (j, 0, 0))],
        out_shape=[jax.ShapeDtypeStruct((2, nj, m, fs), BF), jax.ShapeDtypeStruct((nj, 8, fs), F32)],
        scratch_shapes=[pltpu.VMEM((2 * HALO + tm, fs), F32), pltpu.VMEM((tm + 16, fs), F32),
                        pltpu.VMEM((48, fs), F32)])


def _gate_bwd(dm, z, ya, yb, ycp, scale, *, name, tm=256):
    m, d = dm.shape
    tm = _row_tile(m, tm)

    def body(dm_ref, ga, gb, gc, ya_ref, yb_ref, yc_ref, sc_ref, dya_ref, dyb_ref, dyc_ref, dz_ref, red_ref):
        i = pl.program_id(0)
        f32 = lambda r: r[...].astype(F32)
        dmv = f32(dm_ref)
        sa, sb, sc = jax.nn.sigmoid(f32(ga)), jax.nn.sigmoid(f32(gb)), jax.nn.sigmoid(f32(gc))
        scale_v = sc_ref[...]
        ycp_v = f32(yc_ref)
        dya_ref[...] = (dmv * sa).astype(BF)
        dyb_ref[...] = (dmv * sb).astype(BF)
        dyc = dmv * sc
        dyc_ref[...] = (dyc * scale_v).astype(BF)
        dga = dmv * f32(ya_ref) * (sa * (1.0 - sa))
        dgb = dmv * f32(yb_ref) * (sb * (1.0 - sb))
        dgc = dmv * (ycp_v * scale_v) * (sc * (1.0 - sc))
        dz_ref[:, 0:d] = dga.astype(BF)
        dz_ref[:, d:2 * d] = dgb.astype(BF)
        dz_ref[:, 2 * d:3 * d] = dgc.astype(BF)
        red = jnp.concatenate([_rowsum(dyc * ycp_v), _rowsum(dga), _rowsum(dgb), _rowsum(dgc),
                               jnp.zeros((4, d), F32)], axis=0)

        @pl.when(i == 0)
        def _():
            red_ref[...] = red

        @pl.when(i > 0)
        def _():
            red_ref[...] += red

    row = pl.BlockSpec((tm, d), lambda i: (i, 0))
    obf = jax.ShapeDtypeStruct((m, d), BF)
    return pl.pallas_call(
        body, name=name, grid=(m // tm,),
        in_specs=[row, _seg_spec(tm, d, 6), _seg_spec(tm, d, 7), _seg_spec(tm, d, 8), row, row, row,
                  pl.BlockSpec((1, d), lambda i: (0, 0))],
        out_specs=[row, row, row, pl.BlockSpec((tm, 3 * d), lambda i: (i, 2)), pl.BlockSpec((8, d), lambda i: (0, 0))],
        out_shape=[obf, obf, obf, jax.ShapeDtypeStruct((m, 9 * d), BF), jax.ShapeDtypeStruct((8, d), F32)],
        compiler_params=_params(("arbitrary",)),
    )(dm, z, z, z, ya, yb, ycp, scale)


def _mix_bwd(dz, dua, dub, ddd, z, conv_a, lnv, wm, wmt, bias_full, mask, *, name, tm=128, phases=()):
    m, d = dua.shape
    tm = _row_tile(m, tm)
    ni = m // tm
    grp = d // len(POOL_WINDOWS)
    ng = d // GMLP_BLOCK
    ext_rows = tm + 8

    def body(dz_in, dua_ref, dub_ref, dd_ref, zb, zc, zx, zu, zv, zp, zc_h, zx_h, dua_n, zb_n, dd_n,
             ca_ref, lnv_ref, wm_ref, wmt_ref, bias_ref, mask_ref,
             dz_ref, red_ref, dws_ref, dbs_ref, ext, sh_s, vn_s, mixed_s, dmx_s, dvn_s, dbs_acc):
        del dz_in
        i = pl.program_id(0)
        rows = []
        f32 = lambda r: r[...].astype(F32)
        zbv, zcv, zxv = f32(zb), f32(zc), f32(zx)
        pa = zcv * zxv
        ext[0:HALO, :] = jnp.where(i == 0, 0.0, f32(zc_h) * f32(zx_h))
        ext[HALO:HALO + tm, :] = pa
        w = ca_ref[...]
        w0, w1, w2 = w[0:1, :], w[1:2, :], w[2:3, :]
        p1 = ext[pl.ds(HALO - 1, tm), :]
        p2 = ext[pl.ds(HALO - 2, tm), :]
        conv = w0 * p2 + w1 * p1 + w2 * pa
        duav = f32(dua_ref)
        dzb = duav * conv
        dca = duav * zbv
        dca_n = jnp.where(i < ni - 1, f32(dua_n)[0:8, :] * f32(zb_n)[0:8, :], 0.0)
        sh_s[0:tm, :] = dca
        sh_s[tm:tm + 8, :] = dca_n
        dpa = w2 * dca + w1 * sh_s[pl.ds(1, tm), :] + w0 * sh_s[pl.ds(2, tm), :]
        dzc = dpa * zxv
        dzx = dpa * zcv
        dz_ref[:, 0:d] = dzb.astype(BF)
        dz_ref[:, d:2 * d] = dzc.astype(BF)
        dz_ref[:, 2 * d:3 * d] = dzx.astype(BF)
        rows += [_rowsum(dzb), _rowsum(dzc), _rowsum(dzx)]
        dconv = [_rowsum(dca * p2), _rowsum(dca * p1), _rowsum(dca * pa)]
        zuv, zvv = f32(zu), f32(zv)
        gu, dgu_dz = _gelu_parts(zuv)
        gv, dgv_dz = _gelu_parts(zvv)
        vhat, rstd = _ln_stats(gv)
        gain = lnv_ref[0:1, :]
        vn_s[...] = (vhat * gain + lnv_ref[1:2, :]).astype(BF)
        _spatial_mix(wm_ref, vn_s, mixed_s, bias_ref, tm, d)
        dubv = f32(dub_ref)
        dzu = dubv * mixed_s[...] * dgu_dz
        dmixed = dubv * gu
        dmx_s[...] = dmixed.astype(BF)
        _spatial_mix(wmt_ref, dmx_s, dvn_s, None, tm, d)
        dvn = dvn_s[...]
        dzv = _ln_bwd(dvn, vhat, rstd, gain) * dgv_dz
        dz_ref[:, 3 * d:4 * d] = dzu.astype(BF)
        dz_ref[:, 4 * d:5 * d] = dzv.astype(BF)
        rows += [_rowsum(dzu), _rowsum(dzv)]
        dlnv = [_rowsum(dvn * vhat), _rowsum(dvn)]
        dbs_part = dmixed[0:GMLP_BLOCK, :]
        for n in range(1, tm // GMLP_BLOCK):
            dbs_part = dbs_part + dmixed[n * GMLP_BLOCK:(n + 1) * GMLP_BLOCK, :]
        ddv = f32(dd_ref)
        t = (i * tm + lax.broadcasted_iota(jnp.int32, (ext_rows + 8, 1), 0) + 1).astype(F32)
        dde = jnp.concatenate([ddv, jnp.where(i < ni - 1, f32(dd_n), 0.0)], axis=0)
        for k, win in enumerate(POOL_WINDOWS):
            cs = slice(k * grp, (k + 1) * grp)
            ext[0:tm + HALO, cs] = dde[:, cs] / jnp.minimum(t, float(win))
        dzp_parts = []
        for k, win in enumerate(POOL_WINDOWS):
            cs = slice(k * grp, (k + 1) * grp)
            s = ext[0:tm, cs]
            for j in range(1, win):
                s = s + ext[pl.ds(j, tm), cs]
            dzp_parts.append(s - ddv[:, cs])
        dzp = jnp.concatenate(dzp_parts, axis=1)
        dz_ref[:, 5 * d:6 * d] = dzp.astype(BF)
        rows += [_rowsum(dzp)]
        red = jnp.concatenate(rows + dconv + dlnv + [jnp.zeros((5, d), F32)], axis=0)

        @pl.when(i == 0)
        def _():
            red_ref[...] = red
            dbs_acc[...] = dbs_part
            dws_ref[...] = jnp.zeros_like(dws_ref)

        @pl.when(i > 0)
        def _():
            red_ref[...] += red
            dbs_acc[...] += dbs_part

        for n in range(tm // GMLP_BLOCK):
            for g in range(ng):
                rs = slice(n * GMLP_BLOCK, (n + 1) * GMLP_BLOCK)
                cs = slice(g * GMLP_BLOCK, (g + 1) * GMLP_BLOCK)
                dws_ref[g] += mask_ref[...] * lax.dot_general(
                    dmx_s[rs, cs], vn_s[rs, cs], (NT, ((), ())), preferred_element_type=F32)

        @pl.when(i == ni - 1)
        def _():
            lane = lax.broadcasted_iota(jnp.int32, (GMLP_BLOCK, GMLP_BLOCK), 1)
            out = jnp.zeros((GMLP_BLOCK, GMLP_BLOCK), F32)
            for g in range(ng):
                sg = jnp.sum(dbs_acc[:, g * GMLP_BLOCK:(g + 1) * GMLP_BLOCK], axis=1, keepdims=True)
                out = out + jnp.where(lane == g, sg, 0.0)
            dbs_ref[...] = out

    row = pl.BlockSpec((tm, d), lambda i: (i, 0))
    full = lambda a: pl.BlockSpec(a.shape, lambda i: (0,) * a.ndim)
    hb = tm // HALO
    last_hb = m // HALO - 1
    nrow = pl.BlockSpec((HALO, d), lambda i: (jnp.minimum((i + 1) * hb, last_hb), 0))
    return _pcall(
        body, (dz, dua, dub, ddd, z, z, z, z, z, z, z, z, dua, z, ddd, conv_a, lnv, wm, wmt, bias_full, mask),
        name=name, grid=(ni,), sem=("arbitrary",), aliases={0: 0}, phases=phases,
        in_specs=[pl.BlockSpec(memory_space=pl.ANY), row, row, row]
        + [_seg_spec(tm, d, s) for s in range(6)]
        + [_prev_halo_spec(tm, d, 1), _prev_halo_spec(tm, d, 2), nrow, _next_halo_spec(tm, d, 0, m), nrow]
        + [full(conv_a), full(lnv), full(wm), full(wmt), full(bias_full), full(mask)],
        out_specs=[pl.BlockSpec((tm, 6 * d), lambda i: (i, 0)), pl.BlockSpec((16, d), lambda i: (0, 0)),
                   full(wm), pl.BlockSpec((GMLP_BLOCK, GMLP_BLOCK), lambda i: (0, 0))],
        out_shape=[jax.ShapeDtypeStruct(dz.shape, BF), jax.ShapeDtypeStruct((16, d), F32),
                   jax.ShapeDtypeStruct(wm.shape, F32), jax.ShapeDtypeStruct((GMLP_BLOCK, GMLP_BLOCK), F32)],
        scratch_shapes=[pltpu.VMEM((2 * HALO + tm, d), F32), pltpu.VMEM((tm + 8, d), F32),
                        pltpu.VMEM((tm, d), BF), pltpu.VMEM((tm, d), F32), pltpu.VMEM((tm, d), BF),
                        pltpu.VMEM((tm, d), F32), pltpu.VMEM((GMLP_BLOCK, d), F32)])


REST = ("w_a_out", "w_b_out", "w_pool", "w_o", "w_up", "w_down")


def _remote(src, dst, ssem, rsem, k, to):
    return pltpu.make_async_remote_copy(src_ref=src, dst_ref=dst, send_sem=ssem.at[k], recv_sem=rsem.at[k],
                                        device_id=to, device_id_type=MESH)


def _gather_phase1(shards, rows=None, onto=None):
    n = len(shards)
    rows = rows or [None] * n
    onto = onto or [None] * n
    extra = [a for a in range(n) if onto[a] is not None]

    def build(ins, outs, ssem, rsem, lsem):
        x, y, c, chips = _place()
        me = 4 * x + 2 * y + c

        def src(a):
            return ins[a] if rows[a] is None else ins[a].at[:, pl.ds(*rows[a])]

        def dst(a, dev):
            return outs[a].at[:, dev] if rows[a] is None else outs[a].at[:, dev, pl.ds(*rows[a])]

        local = [pltpu.make_async_copy(src(a), dst(a, me), lsem.at[a]) for a in range(n)]
        sends, recvs = [], []
        for j, (cx, cy) in enumerate(chips):
            for a in range(n):
                sends.append(_remote(src(a), dst(a, me), ssem, rsem, 4 * a + 1 + j, (cx, cy, c)))
                recvs.append(_remote(src(a), dst(a, 4 * cx + 2 * cy + c), ssem, rsem, 4 * a + 1 + j, (cx, cy, c)))
        for a in range(n):
            sends.append(_remote(src(a), dst(a, me), ssem, rsem, 4 * a, (x, y, 1 - c)))
            recvs.append(_remote(src(a), dst(a, 4 * x + 2 * y + 1 - c), ssem, rsem, 4 * a, (x, y, 1 - c)))
        return dict(start=local + sends, recv=recvs, send=sends, local=local)

    outs = [jax.ShapeDtypeStruct((s.shape[0], N_DEV) + s.shape[1:], s.dtype) for s in shards]
    return _Phase(list(shards) + [onto[a] for a in extra], outs, {n + k: a for k, a in enumerate(extra)},
                  4 * n, n, build)


def _gather_phase2(fulls):
    n = len(fulls)

    def build(ins, outs, ssem, rsem, lsem):
        x, y, c, chips = _place()
        sends, recvs = [], []
        for j, (cx, cy) in enumerate(chips):
            for a in range(n):
                mine, theirs = 4 * cx + 2 * cy + c, 4 * cx + 2 * cy + 1 - c
                sends.append(_remote(ins[a].at[:, mine], outs[a].at[:, mine], ssem, rsem, 3 * a + j, (x, y, 1 - c)))
                recvs.append(_remote(ins[a].at[:, theirs], outs[a].at[:, theirs], ssem, rsem, 3 * a + j, (x, y, 1 - c)))
        return dict(start=sends, recv=recvs, send=sends, local=[])

    outs = [jax.ShapeDtypeStruct(f.shape, f.dtype) for f in fulls]
    return _Phase(fulls, outs, {a: a for a in range(n)}, 3 * n, 0, build)


def _pair_phase(grads):
    n = len(grads)

    def build(ins, outs, ssem, rsem, lsem):
        x, y, c, _ = _place()
        cps = [_remote(ins[a].at[:, 2 * q + (1 - c)], outs[a].at[q], ssem, rsem, 4 * a + q, (x, y, 1 - c))
               for a in range(n) for q in range(4)]
        return dict(start=cps, recv=cps, send=cps, local=[])

    outs = [jax.ShapeDtypeStruct((4, g.shape[0]) + g.shape[2:], g.dtype) for g in grads]
    return _Phase(grads, outs, {}, 4 * n, 0, build)


def _chip_phase(bufs, accs, l, depth):
    n = len(bufs)
    has = accs is not None

    def build(ins, outs, ssem, rsem, lsem):
        x, y, c, chips = _place()
        myq = 2 * x + y
        local, sends, recvs = [], [], []
        for a in range(n):
            local.append(pltpu.make_async_copy(ins[a].at[myq], outs[a].at[myq, l], lsem.at[a]))
            for j, (cx, cy) in enumerate(chips):
                q = 2 * cx + cy
                sends.append(_remote(ins[a].at[q], outs[a].at[myq, l], ssem, rsem, 3 * a + j, (cx, cy, c)))
                recvs.append(_remote(ins[a].at[q], outs[a].at[q, l], ssem, rsem, 3 * a + j, (cx, cy, c)))
        return dict(start=local + sends, recv=recvs, send=sends, local=local)

    outs = [jax.ShapeDtypeStruct((4, depth) + b.shape[1:], b.dtype) for b in bufs]
    return _Phase(list(bufs) + (list(accs) if has else []), outs, {n + a: a for a in range(n)} if has else {},
                  3 * n, n, build)


def _grad_chunks(n, g):
    if n == "w_pool":
        return g.reshape(g.shape[0], N_DEV, g.shape[1] // N_DEV, g.shape[2])
    if n in ("w_in", "w_up"):
        return g[None]
    return g.reshape(1, N_DEV, -1, g.shape[-1])


class _ReduceScatter:
    def __init__(self, depth, cidx):
        self.depth, self.cidx, self.acc, self.count = depth, cidx, {}, 0
        self.small_gathered = None

    def pair(self, names, grads):
        return _pair_phase([_grad_chunks(n, grads[n]) for n in names])

    def sums(self, names, grads, phase):
        out = []
        for n, r1 in zip(names, phase.results):
            out.append(_pair_sum(_grad_chunks(n, grads[n]), r1, self.cidx, name="rs_sum_%d" % self.count))
            self.count += 1
        return out

    def chip(self, names, bufs, l):
        accs = [self.acc[n] for n in names] if names[0] in self.acc else None
        return _chip_phase(bufs, accs, l, self.depth)

    def done(self, names, phase):
        for n, r in zip(names, phase.results):
            self.acc[n] = r


def _rest_views(fulls, d):
    a_out, b_out, pool, o, up, down = fulls
    grp = d // len(POOL_WINDOWS)
    return dict(w_a_out=a_out.reshape(d, d), w_b_out=b_out.reshape(d, d), w_o=o.reshape(d, d),
                w_pool=pool.reshape(len(POOL_WINDOWS), grp, grp), w_up8=up[0],
                wd4=down.reshape(N_DEV // 2, -1, d))


class _GatherPlan:
    def __init__(self):
        self.jobs, self.part, self.full = [], {}, {}

    def add(self, key, shard, first, second, rows=None, onto=None):
        self.jobs.append((key, shard, first, second, rows, onto))

    def phases(self, name):
        j1 = [j for j in self.jobs if j[2] == name]
        j2 = [j for j in self.jobs if j[3] == name]
        tagged = []
        if j1:
            onto = [self.part[j[5]] if j[5] else None for j in j1]
            tagged.append((self.part, j1, _gather_phase1([j[1] for j in j1], [j[4] for j in j1], onto)))
        if j2:
            tagged.append((self.full, j2, _gather_phase2([self.part[j[0]] for j in j2])))
        return tagged

    @staticmethod
    def collect(tagged):
        for store, jobs, phase in tagged:
            for j, r in zip(jobs, phase.results):
                store[j[0]] = r


def _layer_fwd(x, w, alpha, tag, plan=None):
    d = x.shape[1]
    grp = d // len(POOL_WINDOWS)

    def carried(kernel, *args, name, **kw):
        tagged = plan.phases(name) if plan else []
        out = kernel(*args, name=name, phases=[t[2] for t in tagged], **kw)
        _GatherPlan.collect(tagged)
        return out

    def weight(n, shape):
        return plan.full[n + tag].reshape(shape) if plan else w[n]

    z, h = carried(_mod_matmul, x, w["mod1"], w["w_in8"], w["b_in8"], flat_out=True, name="in_proj" + tag)
    ua, ub, dd = carried(_mix_fwd, z, w["conv_a"], w["lnv"], w["wm"], w["bias_full"], name="mix_fwd" + tag)
    w["w_a_out"], w["w_b_out"], w["w_o"] = (weight(n, (d, d)) for n in ("w_a_out", "w_b_out", "w_o"))
    w["w_pool"] = weight("w_pool", (len(POOL_WINDOWS), grp, grp))
    w["w_up8"] = weight("w_up8", (N_DEV, d, -1))
    ya = _mm_rows(ua, w["w_a_out"], dn=NN, name="a_out" + tag, out_dtype=ACT_DTYPE)
    yb = _mm_rows(ub, w["w_b_out"], dn=NN, name="b_out" + tag, out_dtype=ACT_DTYPE)
    ycp = _pool_proj(dd, w["w_pool"], dn=NN, name="pool_proj" + tag, out_dtype=ACT_DTYPE)
    merged = _merge(z, ya, yb, ycp, w["pool_scale"], name="merge" + tag)
    o = _mm_rows(merged, w["w_o"], dn=NN, name="o_proj" + tag)
    x1 = _resid_ln(x, o, w["ln1"], alpha, name="ln1" + tag)
    up8, h2 = carried(_mod_matmul, x1, w["mod2"], w["w_up8"], w["b_up8"], flat_out=False, name="up_proj" + tag)
    up4 = up8.reshape((2, up8.shape[0] // 2) + up8.shape[1:])
    f4 = carried(_ffn_fwd, up4, w["cw"], w["cb"], name="ffn_fwd" + tag)
    w["wd4"] = weight("wd4", (N_DEV // 2, -1, d))
    y2 = carried(_down_proj, f4, w["wd4"], name="down_proj" + tag)
    x2 = _resid_ln(x1, y2, w["ln2"], alpha, name="ln2" + tag)
    saved = dict(x=x, z=z, h=h, ua=ua, ub=ub, dd=dd, ya=ya, yb=yb, ycp=ycp, merged=merged, o=o, x1=x1,
                 up4=up4, h2=h2, f4=f4, y2=y2)
    return x2, saved


def _layer_bwd(dpart, dh_above, xmod_above, m_above, w, s, alpha, tag, l=0, above=None, rs=None, upper_reds=()):
    first, rest = ("w_in",), REST
    ph = lambda p: [p] if p is not None else ()
    r1a = rs.pair(first, above) if above else None
    dy2, dx1p, red2 = _resid_ln_bwd(dpart, dh_above, xmod_above, m_above, s["x1"], s["y2"], w["ln2"], alpha,
                                    name="ln2_bwd" + tag, phases=ph(r1a))
    r1b = rs.pair(rest, above) if above else None
    df4 = _down_bwd(dy2, w["wd4"], name="down_bwd" + tag, phases=ph(r1b))
    if above:
        sb_a, sb_b = rs.sums(first, above, r1a), rs.sums(rest, above, r1b)
    gw_down4 = _tn_shards_lhs(s["f4"], dy2, name="gw_down" + tag)
    r3a = r3b = r3c = None
    if above:
        bufs = dict(zip(first + rest, sb_a + sb_b))
        light = tuple(n for n in rest if n != "w_up")
        r3a = rs.chip(first, [bufs[n] for n in first], l + 1)
    dup4, redf = _ffn_bwd(s["up4"], df4, w["cw"], w["cb"], name="ffn_bwd" + tag, phases=ph(r3a))
    dup8 = dup4.reshape((dup4.shape[0] * dup4.shape[1],) + dup4.shape[2:])
    if above:
        rs.done(first, r3a)
        r3b = rs.chip(("w_up",), [bufs["w_up"]], l + 1)
    gw_up8 = _tn_shards_lhs(dup8, s["h2"], name="gw_up" + tag, phases=ph(r3b))
    if above:
        rs.done(("w_up",), r3b)
        r3c = rs.chip(light, [bufs[n] for n in light], l + 1)
    own = rs is not None and l == 0
    big = dict(w_up=gw_up8, w_down=gw_down4)
    early = ("w_down", "w_up")
    o1 = rs.pair(early, big) if own else None
    dh2 = _nt_shards(dup8, w["w_up8"], name="up_bwd" + tag, phases=list(ph(r3c)) + list(ph(o1)))
    if above:
        rs.done(light, r3c)
    if own:
        sb_o = rs.sums(early, big, o1)
    do, dxp, red1 = _resid_ln_bwd(dx1p, dh2, s["x1"], w["mod2"][0:1], s["x"], s["o"], w["ln1"], alpha,
                                  name="ln1_bwd" + tag)
    dm = _mm_rows(do, w["w_o"], dn=NT, name="o_bwd" + tag, out_dtype=ACT_DTYPE)
    big["w_o"] = _mm_tn(s["merged"], do, name="gw_o" + tag)
    dya, dyb, dyc, dz, redg = _gate_bwd(dm, s["z"], s["ya"], s["yb"], s["ycp"], w["pool_scale"], name="gate_bwd" + tag)
    dua = _mm_rows(dya, w["w_a_out"], dn=NT, name="a_out_bwd" + tag, out_dtype=ACT_DTYPE)
    dub = _mm_rows(dyb, w["w_b_out"], dn=NT, name="b_out_bwd" + tag, out_dtype=ACT_DTYPE)
    ddd = _pool_proj(dyc, w["w_pool"], dn=NT, name="pool_bwd" + tag, out_dtype=ACT_DTYPE)
    big["w_a_out"] = _mm_tn(s["ua"], dya, name="gw_a_out" + tag)
    big["w_b_out"] = _mm_tn(s["ub"], dyb, name="gw_b_out" + tag)
    big["w_pool"] = _tn_pool(s["dd"], dyc, w["w_pool"].shape[0], name="gw_pool" + tag)
    o3 = rs.chip(early, sb_o, l) if own else None
    dz, redm, dws, dbs = _mix_bwd(dz, dua, dub, ddd, s["z"], w["conv_a"], w["lnv"], w["wm"], w["wmt"],
                                  w["bias_full"], w["mask"], name="mix_bwd" + tag, phases=ph(o3))
    reds = dict(red2=red2, redf=redf, red1=red1, redg=redg, redm=redm, dws=dws, dbs=dbs)
    mid = ("w_o", "w_a_out", "w_b_out", "w_pool")
    o1b = sg1 = None
    if own:
        rs.done(early, o3)
        o1b = rs.pair(mid, big)
        sg1 = _gather_phase1([_small_payload([reds] + list(upper_reds))])
    big["w_in"] = _tn_cols_rhs(s["h"], dz, w["w_in8"].shape[0], name="gw_in" + tag,
                               phases=[o1b, sg1] if own else ())
    pending = None
    if own:
        sb_m = rs.sums(mid, big, o1b)
        o1c, o3b, sg2 = rs.pair(first, big), rs.chip(mid, sb_m, l), _gather_phase2(sg1.results)
        n_tiles = dz.shape[0] // _row_tile(dz.shape[0], 2048)
        head = max(n_tiles // 2, 1)
        dh = _nt_cols(dz, w["w_in8"], name="in_bwd" + tag + "_a", phases=[o1c, o3b, sg2], tiles=(0, head))
        rs.done(mid, o3b)
        rs.small_gathered = sg2.results[0]
        o3c = rs.chip(first, rs.sums(first, big, o1c), l)
        if n_tiles > head:
            dh = _nt_cols(dz, w["w_in8"], name="in_bwd" + tag + "_b", phases=[o3c], tiles=(head, n_tiles - head),
                          into=dh)
            rs.done(first, o3c)
        else:
            pending = (first, o3c)
    else:
        dh = _nt_cols(dz, w["w_in8"], name="in_bwd" + tag)
    return dxp, dh, big, reds, pending


def _local_step(x, tgt, ws, alpha, plan=None, rs=None):
    depth = len(ws)
    saved = []
    y = x
    for l in range(depth):
        if plan and l > 0:
            ws[l]["w_in8"] = plan.full["w_in8_l%d" % l][0]
        y, s = _layer_fwd(y, ws[l], alpha, "_l%d" % l, plan)
        saved.append(s)
    dpart, loss_blk = _loss_grad(y, tgt, name="loss_grad")
    dh = xmod = mvec = above = pending = None
    bigs, reds = [None] * depth, [None] * depth
    for l in reversed(range(depth)):
        dpart, dh, bigs[l], reds[l], pending = _layer_bwd(dpart, dh, xmod, mvec, ws[l], saved[l], alpha, "_l%d" % l,
                                                          l, above if rs else None, rs, reds[l + 1:])
        xmod, mvec, above = saved[l]["x"], ws[l]["mod1"][0:1], bigs[l]
    grad_x, red0 = _resid_ln_bwd(dpart, dh, xmod, mvec, None, None, None, alpha, name="in_bwd_tail",
                                 phases=[pending[1]] if pending else ())
    if pending:
        rs.done(*pending)
    d_ada = []
    for l in range(depth):
        below = red0 if l == 0 else reds[l - 1]["red2"]
        r1, r2 = reds[l]["red1"], reds[l]["red2"]
        d_ada.append(jnp.stack([below[1], below[0], r1[4], r1[1], r1[0], r2[4]]))
    return loss_blk, grad_x, bigs, reds, jnp.stack(d_ada)


def _small_grads(r):
    redm, redg, redf = r["redm"], r["redg"], r["redf"]
    ng = r["dws"].shape[0]
    return dict(
        b_in=jnp.concatenate([redm[0:6], redg[1:4]], axis=0).reshape(-1),
        conv_a=redm[6:9], ln_v_g=redm[9], ln_v_b=redm[10],
        w_spatial=r["dws"], b_spatial=r["dbs"][:, :ng].T,
        pool_scale=redg[0], ln1_g=r["red1"][2], ln1_b=r["red1"][3],
        b_up=jnp.concatenate([redf[:, 4, :].reshape(-1), redf[:, 5, :].reshape(-1)]),
        conv_ffn=jnp.transpose(redf[:, 0:3, :], (1, 0, 2)).reshape(3, -1), conv_ffn_b=redf[:, 3, :].reshape(-1),
        ln2_g=r["red2"][2], ln2_b=r["red2"][3])


def _small_payload(reds):
    smalls = [_small_grads(r) for r in reds]
    order = SMALL_REPLICATED + SMALL_SHARDED
    flat = jnp.concatenate([smalls[l][n].reshape(-1) for n in order for l in range(len(reds))])
    return _as_rows(flat)[None]


def _layer_weights(l, ada, conv_a, conv_ffn, p):
    sh1, sc1, gt1, sh2, sc2, gt2 = (ada[l, k][None, :] for k in range(6))
    nb = N_DEV
    fs = p["b_up"].shape[1] // nb
    nj = nb // 2
    pos = jnp.arange(GMLP_BLOCK)
    allowed = (pos[None, :] // CHUNK) <= (pos[:, None] // CHUNK)
    wmask = jnp.where(allowed[None], p["w_spatial"][l], 0.0)
    return dict(
        mod1=jnp.concatenate([1.0 + sc1, sh1]), mod2=jnp.concatenate([1.0 + sc2, sh2]),
        ln1=jnp.concatenate([gt1, p["ln1_g"][l][None], p["ln1_b"][l][None]]),
        ln2=jnp.concatenate([gt2, p["ln2_g"][l][None], p["ln2_b"][l][None]]),
        b_in8=p["b_in"][l].reshape(N_DEV, 1, -1), b_up8=p["b_up"][l].reshape(nb, 1, fs),
        conv_a=conv_a[l], lnv=jnp.stack([p["ln_v_g"][l], p["ln_v_b"][l]]),
        wm=wmask.astype(BF), wmt=jnp.transpose(wmask, (0, 2, 1)).astype(BF),
        bias_full=jnp.repeat(p["b_spatial"][l].T, GMLP_BLOCK, axis=1), mask=allowed.astype(F32),
        pool_scale=p["pool_scale"][l][None],
        cw=jnp.transpose(conv_ffn[l].reshape(3, nj, fs), (1, 0, 2)), cb=p["conv_ffn_b"][l].reshape(nj, 1, fs))


ANY = pl.BlockSpec(memory_space=pl.ANY)


def _place():
    x, y, c = lax.axis_index("x"), lax.axis_index("y"), lax.axis_index("c")
    chips = [(1 - x, y), (x, 1 - y), (1 - x, 1 - y)]
    return x, y, c, chips


def _allgather_vmem(xs, *, name):
    r, cdim = xs.shape

    def body(x_ref, out_ref, send_sems, recv_sems, local_sem):
        x, y, c, chips = _place()
        me, sibling = (x, y, c), (x, y, 1 - c)

        def rows(px, py, pc):
            return out_ref.at[pl.ds((4 * px + 2 * py + pc) * r, r), :]

        def copy(k, block, to, src=None):
            return pltpu.make_async_remote_copy(
                src_ref=rows(*block) if src is None else src, dst_ref=rows(*block),
                send_sem=send_sems.at[k], recv_sem=recv_sems.at[k], device_id=to, device_id_type=MESH)

        mine = pltpu.make_async_copy(x_ref, rows(*me), local_sem)
        mine.start()
        first = [copy(0, me, sibling, src=x_ref)]
        first += [copy(1 + j, me, (*chip, c), src=x_ref) for j, chip in enumerate(chips)]
        for cp in first:
            cp.start()
        passed = [copy(4 + j, (*chip, c), sibling) for j, chip in enumerate(chips)]
        for j, chip in enumerate(chips):
            copy(1 + j, (*chip, c), me).wait_recv()
            passed[j].start()
        copy(0, sibling, me).wait_recv()
        for j, chip in enumerate(chips):
            copy(4 + j, (*chip, 1 - c), me).wait_recv()
        for cp in first + passed:
            cp.wait_send()
        mine.wait()

    return pl.pallas_call(
        body, name=name, out_shape=jax.ShapeDtypeStruct((N_DEV * r, cdim), xs.dtype),
        in_specs=[pl.BlockSpec(memory_space=pltpu.VMEM)], out_specs=pl.BlockSpec(memory_space=pltpu.VMEM),
        scratch_shapes=[pltpu.SemaphoreType.DMA((7,)), pltpu.SemaphoreType.DMA((7,)), pltpu.SemaphoreType.DMA],
        compiler_params=_params(),
    )(xs)


def _gather_weights(shards, *, name):
    n = len(shards)

    def body(*refs):
        ins, outs = refs[:n], refs[n:2 * n]
        send_sems, recv_sems, local_sems = refs[2 * n:]
        x, y, c, chips = _place()
        me, sibling = (x, y, c), (x, y, 1 - c)

        def slot(a, px, py, pc):
            return outs[a].at[:, 4 * px + 2 * py + pc]

        def copy(a, k, block, to, src=None):
            return pltpu.make_async_remote_copy(
                src_ref=slot(a, *block) if src is None else src, dst_ref=slot(a, *block),
                send_sem=send_sems.at[7 * a + k], recv_sem=recv_sems.at[7 * a + k], device_id=to,
                device_id_type=MESH)

        mine = [pltpu.make_async_copy(ins[a], slot(a, *me), local_sems.at[a]) for a in range(n)]
        for cp in mine:
            cp.start()
        first = []
        for j, chip in enumerate(chips):
            first += [copy(a, 1 + j, me, (*chip, c), src=ins[a]) for a in range(n)]
        first += [copy(a, 0, me, sibling, src=ins[a]) for a in range(n)]
        for cp in first:
            cp.start()
        passed = []
        for j, chip in enumerate(chips):
            for a in range(n):
                copy(a, 1 + j, (*chip, c), me).wait_recv()
                fwd = copy(a, 4 + j, (*chip, c), sibling)
                fwd.start()
                passed.append(fwd)
        for a in range(n):
            copy(a, 0, sibling, me).wait_recv()
        for j, chip in enumerate(chips):
            for a in range(n):
                copy(a, 4 + j, (*chip, 1 - c), me).wait_recv()
        for cp in first + passed:
            cp.wait_send()
        for cp in mine:
            cp.wait()

    out_shape = [jax.ShapeDtypeStruct((s.shape[0], N_DEV) + s.shape[1:], s.dtype) for s in shards]
    return pl.pallas_call(
        body, name=name, out_shape=out_shape, in_specs=[ANY] * n, out_specs=[ANY] * n,
        scratch_shapes=[pltpu.SemaphoreType.DMA((7 * n,)), pltpu.SemaphoreType.DMA((7 * n,)),
                        pltpu.SemaphoreType.DMA((n,))],
        compiler_params=_params(),
    )(*shards)


def _pick_tile(r, cap):
    best = None
    for t in range(8, min(r, cap) + 1, 8):
        if r % t == 0:
            best = t
    return best if best is not None else r


def _pair_sum(g, r1, cidx, *, name):
    p, _, r, cdim = g.shape
    tr = _pick_tile(r, 256)

    def body(c_ref, g_ref, r_ref, o_ref):
        del c_ref
        o_ref[...] = (g_ref[...].astype(F32) + r_ref[...].astype(F32)).astype(BF)

    grid_spec = pltpu.PrefetchScalarGridSpec(
        num_scalar_prefetch=1, grid=(4, r // tr),
        in_specs=[pl.BlockSpec((p, None, tr, cdim), lambda q, i, c: (0, 2 * q + c[0], i, 0)),
                  pl.BlockSpec((None, p, tr, cdim), lambda q, i, c: (q, 0, i, 0))],
        out_specs=pl.BlockSpec((None, p, tr, cdim), lambda q, i, c: (q, 0, i, 0)))
    return pl.pallas_call(
        body, name=name, grid_spec=grid_spec, out_shape=jax.ShapeDtypeStruct((4, p, r, cdim), BF),
        compiler_params=_params(("arbitrary", "arbitrary")),
    )(cidx, g, r1)


def _ada_fwd(c_all, w_ada, *, name):
    depth, d, ns = w_ada.shape
    nb = c_all.shape[0]

    def body(c_ref, w_ref, o_ref):
        cv = c_ref[...]
        act = cv * jax.nn.sigmoid(cv)
        o_ref[...] = jnp.dot(act, w_ref[...], preferred_element_type=F32, precision=lax.Precision.HIGHEST)

    return pl.pallas_call(
        body, name=name, grid=(depth,),
        in_specs=[pl.BlockSpec((nb, d), lambda l: (0, 0)), pl.BlockSpec((None, d, ns), lambda l: (l, 0, 0))],
        out_specs=pl.BlockSpec((None, nb, ns), lambda l: (l, 0, 0)),
        out_shape=jax.ShapeDtypeStruct((depth, nb, ns), F32), compiler_params=_params(("parallel",)),
    )(c_all, w_ada)


def _ada_bwd(ct, dmine, dall, *, name):
    depth, nb, ns = dmine.shape
    d = ct.shape[0]

    def body(ct_ref, dm_ref, da_ref, gw_ref, gb_ref):
        cv = ct_ref[...]
        act = cv * jax.nn.sigmoid(cv)
        gw_ref[...] = jnp.dot(act, dm_ref[...], preferred_element_type=F32, precision=lax.Precision.HIGHEST)
        s = da_ref[0]
        for b in range(1, nb):
            s = s + da_ref[b]
        gb_ref[...] = s

    return pl.pallas_call(
        body, name=name, grid=(depth,),
        in_specs=[pl.BlockSpec((d, nb), lambda l: (0, 0)), pl.BlockSpec((None, nb, ns), lambda l: (l, 0, 0)),
                  pl.BlockSpec(dall.shape, lambda l: (0, 0, 0))],
        out_specs=[pl.BlockSpec((None, d, ns), lambda l: (l, 0, 0)), pl.BlockSpec(dall.shape[1:], lambda l: (0, 0))],
        out_shape=[jax.ShapeDtypeStruct((depth, d, ns), F32), jax.ShapeDtypeStruct(dall.shape[1:], F32)],
        compiler_params=_params(("arbitrary",)),
    )(ct, dmine, dall)


def _sum_parts(parts, *, name):
    p, r, cdim = parts.shape
    tr = _pick_tile(r, 512)

    def body(p_ref, o_ref):
        s = p_ref[0]
        for k in range(1, p):
            s = s + p_ref[k]
        o_ref[...] = s

    return pl.pallas_call(
        body, name=name, grid=(r // tr,),
        in_specs=[pl.BlockSpec((p, tr, cdim), lambda i: (0, i, 0))], out_specs=pl.BlockSpec((tr, cdim), lambda i: (i, 0)),
        out_shape=jax.ShapeDtypeStruct((r, cdim), F32), compiler_params=_params(("parallel",)),
    )(parts)


def _adamw(parts, w, m, v, *, name):
    p, depth, r, cdim = parts.shape
    tr = _pick_tile(r, 256)

    def body(p_ref, w_ref, m_ref, v_ref, g_out, d_out, m_out, v_out):
        g = p_ref[0].astype(F32)
        for k in range(1, p):
            g = g + p_ref[k].astype(F32)
        m2 = ADAM_B1 * m_ref[...] + (1.0 - ADAM_B1) * g
        v2 = ADAM_B2 * v_ref[...] + (1.0 - ADAM_B2) * (g * g)
        m_hat = m2 / (1.0 - ADAM_B1 ** ADAM_STEP)
        v_hat = v2 / (1.0 - ADAM_B2 ** ADAM_STEP)
        g_out[...] = g
        d_out[...] = -ADAM_LR * (m_hat / (jnp.sqrt(v_hat) + ADAM_EPS) + ADAM_WD * w_ref[...])
        m_out[...] = m2
        v_out[...] = v2

    blk = pl.BlockSpec((None, tr, cdim), lambda l, i: (l, i, 0))
    out = jax.ShapeDtypeStruct((depth, r, cdim), F32)
    return pl.pallas_call(
        body, name=name, grid=(depth, r // tr),
        in_specs=[pl.BlockSpec((p, None, tr, cdim), lambda l, i: (0, l, i, 0)), blk, blk, blk],
        out_specs=[blk, blk, blk, blk], out_shape=[out, out, out, out],
        compiler_params=_params(("parallel", "parallel")),
    )(parts, w, m, v)


BIG = ("w_in", "w_a_out", "w_b_out", "w_pool", "w_o", "w_up", "w_down")
SMALL_REPLICATED = ("b_in", "ln_v_g", "ln_v_b", "w_spatial", "b_spatial", "pool_scale", "ln1_g", "ln1_b", "b_up",
                    "conv_ffn_b", "ln2_g", "ln2_b")
SMALL_SHARDED = ("conv_a", "conv_ffn")
WEIGHTS = ("w_ada", "b_ada", "w_in", "b_in", "conv_a", "w_a_out", "ln_v_g", "ln_v_b", "w_spatial", "b_spatial",
           "w_b_out", "w_pool", "pool_scale", "w_o", "ln1_g", "ln1_b", "w_up", "b_up", "conv_ffn", "conv_ffn_b",
           "w_down", "ln2_g", "ln2_b")
LANES = 128


def _as_rows(flat, mult=8):
    n = flat.shape[0]
    pad = (-n) % (LANES * mult)
    if pad:
        flat = jnp.concatenate([flat, jnp.zeros((pad,), flat.dtype)])
    return flat.reshape(-1, LANES)


def _shard3(a):
    return a.reshape((-1,) + a.shape[-2:])


def kernel(x, c, w_ada, b_ada, w_in, b_in, conv_a, w_a_out, ln_v_g, ln_v_b, w_spatial, b_spatial, w_b_out, w_pool, pool_scale, w_o, ln1_g, ln1_b, w_up, b_up, conv_ffn, conv_ffn_b, w_down, ln2_g, ln2_b, loss_target, m_w_ada, m_b_ada, m_w_in, m_b_in, m_conv_a, m_w_a_out, m_ln_v_g, m_ln_v_b, m_w_spatial, m_b_spatial, m_w_b_out, m_w_pool, m_pool_scale, m_w_o, m_ln1_g, m_ln1_b, m_w_up, m_b_up, m_conv_ffn, m_conv_ffn_b, m_w_down, m_ln2_g, m_ln2_b, v_w_ada, v_b_ada, v_w_in, v_b_in, v_conv_a, v_w_a_out, v_ln_v_g, v_ln_v_b, v_w_spatial, v_b_spatial, v_w_b_out, v_w_pool, v_pool_scale, v_w_o, v_ln1_g, v_ln1_b, v_w_up, v_b_up, v_conv_ffn, v_conv_ffn_b, v_w_down, v_ln2_g, v_ln2_b):
    p = dict(locals())
    depth, d = w_in.shape[0], w_in.shape[1]
    alpha = (2 * depth) ** 0.25
    me = 4 * lax.axis_index("x") + 2 * lax.axis_index("y") + lax.axis_index("c")
    cidx = lax.axis_index("c").astype(jnp.int32).reshape(1)

    n_ca, n_cf = conv_a.size, conv_ffn.size
    packed = _as_rows(jnp.concatenate([c.reshape(-1), conv_a.reshape(-1), conv_ffn.reshape(-1)]))
    got = _allgather_vmem(packed, name="gather_cond").reshape(N_DEV, -1)
    c_all = got[:, :d]
    ct = c_all.T
    conv_a_full = jnp.transpose(got[:, d:d + n_ca].reshape((N_DEV,) + conv_a.shape), (1, 2, 0, 3)).reshape(depth, 3, -1)
    conv_ffn_full = jnp.transpose(got[:, d + n_ca:d + n_ca + n_cf].reshape((N_DEV,) + conv_ffn.shape),
                                  (1, 2, 0, 3)).reshape(depth, 3, -1)

    ns_ada = w_ada.shape[2]
    ada_part = _ada_fwd(c_all, w_ada, name="ada_fwd")
    ada_all = _allgather_vmem(_as_rows(ada_part.reshape(-1)), name="gather_ada")
    ada_all = ada_all.reshape(N_DEV, depth, N_DEV, ns_ada)
    ada_mine = lax.dynamic_index_in_dim(ada_all, me, axis=2, keepdims=False)
    ada = jnp.transpose(ada_mine, (1, 0, 2)).reshape(depth, -1) + b_ada
    ada = ada.reshape(depth, 6, d)

    shards = [{n: _shard3(p[n][l].astype(BF)) for n in BIG} for l in range(depth)]
    ws = [_layer_weights(l, ada, conv_a_full, conv_ffn_full, p) for l in range(depth)]
    ws[0]["w_in8"] = _gather_weights([shards[0]["w_in"]], name="gather_w_in0")[0][0]
    plan = _GatherPlan()
    four = ("w_a_out", "w_b_out", "w_pool", "w_o")
    for l in range(depth):
        t, prev, sh = "_l%d" % l, "_l%d" % (l - 1), shards[l]
        if l == 0:
            for n in four:
                plan.add(n + t, sh[n], "in_proj" + t, "mix_fwd" + t)
            plan.add("w_up8" + t, sh["w_up"], "in_proj" + t, "mix_fwd" + t)
            plan.add("wd4" + t, sh["w_down"], "mix_fwd" + t, "up_proj" + t)
        else:
            half = sh["w_in"].shape[1] // 2
            plan.add("w_in8_top" + t, sh["w_in"], "up_proj" + prev, None, rows=(0, half))
            plan.add("w_in8" + t, sh["w_in"], "ffn_fwd" + prev, "down_proj" + prev, rows=(half, half),
                     onto="w_in8_top" + t)
            for n in four:
                plan.add(n + t, sh[n], "down_proj" + prev, "in_proj" + t)
            plan.add("w_up8" + t, sh["w_up"], "in_proj" + t, "mix_fwd" + t)
            plan.add("wd4" + t, sh["w_down"], "in_proj" + t, "mix_fwd" + t)

    rs = _ReduceScatter(depth, cidx)
    loss_blk, grad_x, bigs, reds, d_ada = _local_step(x[0], loss_target[0], ws, alpha, plan, rs)
    loss = lax.psum(loss_blk[0, 0], ("x", "y", "c"))

    dada_all = _allgather_vmem(_as_rows(d_ada.reshape(-1)), name="gather_dada")
    dada_all = dada_all.reshape(N_DEV, -1, LANES)
    dflat = dada_all.reshape(N_DEV, depth, 6 * d)
    dmine = lax.dynamic_slice_in_dim(dflat, me * ns_ada, ns_ada, axis=2)
    gw_ada, gb_rows = _ada_bwd(ct, jnp.transpose(dmine, (1, 0, 2)), dada_all, name="ada_bwd")
    gb_ada = gb_rows.reshape(-1)[:depth * 6 * d].reshape(depth, 6 * d)

    out = {}
    for n in BIG:
        parts = rs.acc[n]
        view = (lambda a: jnp.swapaxes(a, 1, 2)) if n == "w_up" else (lambda a: a)
        shard_shape = view(p[n]).shape
        w3 = view(p[n]).reshape(depth, -1, shard_shape[-1])
        parts4 = parts.reshape((4,) + w3.shape)
        res = _adamw(parts4, w3, view(p["m_" + n]).reshape(w3.shape), view(p["v_" + n]).reshape(w3.shape),
                     name="adamw_" + n)
        out[n] = [view(r.reshape(shard_shape)) for r in res]
    out["w_ada"] = _adamw(gw_ada[None], w_ada, m_w_ada, v_w_ada, name="adamw_w_ada")

    order = SMALL_REPLICATED + SMALL_SHARDED
    n_rep = sum(p[n].size for n in SMALL_REPLICATED)
    n_pay = n_rep + N_DEV * (conv_a.size + conv_ffn.size)
    gsum = _sum_parts(rs.small_gathered.reshape(N_DEV, -1, LANES), name="sum_small").reshape(-1)[:n_pay]
    ga_full = gsum[n_rep:n_rep + depth * 3 * d].reshape(depth, 3, d)
    gf_full = gsum[n_rep + depth * 3 * d:].reshape(depth, 3, -1)
    ca_w, cf_w = conv_a.shape[2], conv_ffn.shape[2]
    g_ca = lax.dynamic_slice_in_dim(ga_full, me * ca_w, ca_w, axis=2)
    g_cf = lax.dynamic_slice_in_dim(gf_full, me * cf_w, cf_w, axis=2)
    names = ("b_ada",) + order
    gflat = _as_rows(jnp.concatenate([gb_ada.reshape(-1), gsum[:n_rep], g_ca.reshape(-1), g_cf.reshape(-1)]))
    pack = lambda pre: _as_rows(jnp.concatenate([p[pre + n].reshape(-1) for n in names]))
    res = _adamw(gflat[None, None], pack("")[None], pack("m_")[None], pack("v_")[None], name="adamw_small")
    off = 0
    for n in names:
        size = p[n].size
        out[n] = [r.reshape(-1)[off:off + size].reshape(p[n].shape) for r in res]
        off += size

    return (loss, grad_x[None]) + tuple(out[n][k] for k in range(4) for n in WEIGHTS)
```

```python
import functools

import jax
import jax.numpy as jnp
from jax import lax
from jax.experimental import pallas as pl
from jax.experimental.pallas import tpu as pltpu

F32 = jnp.float32
BF = jnp.bfloat16
MESH = pl.DeviceIdType.MESH

LN_EPS = 1e-5
POOL_WINDOWS = (2, 4, 8, 16)
GMLP_BLOCK = 128
CHUNK = 64
HALO = 16
ADAM_LR, ADAM_B1, ADAM_B2, ADAM_EPS, ADAM_WD, ADAM_STEP = 0.001, 0.9, 0.999, 1e-08, 0.01, 10
N_DEV = 8
VMEM_LIMIT = 56 * 1024 * 1024

GRAD_DTYPE = BF
ACT_DTYPE = BF

NN = ((1,), (0,))
NT = ((1,), (1,))
TN = ((0,), (0,))


def _params(sem=None, vmem=VMEM_LIMIT, **kw):
    if sem is not None:
        kw["dimension_semantics"] = sem
    return pltpu.CompilerParams(vmem_limit_bytes=vmem, **kw)


class _Phase:
    def __init__(self, ins, out_shapes, aliases, n_remote, n_local, build):
        self.ins, self.out_shapes, self.aliases = list(ins), list(out_shapes), dict(aliases)
        self.n_remote, self.n_local, self.build = n_remote, n_local, build
        self.results = None


def _pcall(body, args, *, name, grid, in_specs, out_specs, out_shape, scratch_shapes=(), sem=None, aliases=None,
           phases=()):
    aliases = dict(aliases or {})
    if not phases:
        return pl.pallas_call(
            body, name=name, grid=grid, in_specs=list(in_specs), out_specs=out_specs, out_shape=out_shape,
            scratch_shapes=list(scratch_shapes), input_output_aliases=aliases, compiler_params=_params(sem),
        )(*args)
    single = not isinstance(out_shape, (list, tuple))
    o_specs = [out_specs] if single else list(out_specs)
    o_shapes = [out_shape] if single else list(out_shape)
    n_in, n_out, n_scr = len(args), len(o_shapes), len(scratch_shapes)
    ex_args, ex_out, sems = [], [], []
    for ph in phases:
        for src, dst in ph.aliases.items():
            aliases[n_in + len(ex_args) + src] = n_out + len(ex_out) + dst
        ex_args += ph.ins
        ex_out += ph.out_shapes
        sems += [pltpu.SemaphoreType.DMA((max(ph.n_remote, 1),)), pltpu.SemaphoreType.DMA((max(ph.n_remote, 1),)),
                 pltpu.SemaphoreType.DMA((max(ph.n_local, 1),))]

    def wrapped(*refs):
        pos = n_in
        ph_in = []
        for ph in phases:
            ph_in.append(refs[pos:pos + len(ph.ins)])
            pos += len(ph.ins)
        base_out = refs[pos:pos + n_out]
        pos += n_out
        ph_out = []
        for ph in phases:
            ph_out.append(refs[pos:pos + len(ph.out_shapes)])
            pos += len(ph.out_shapes)
        base_scr = refs[pos:pos + n_scr]
        ph_sems = refs[pos + n_scr:]
        first = last = None
        for ax, n in enumerate(grid):
            pid = pl.program_id(ax)
            first = (pid == 0) if first is None else first & (pid == 0)
            last = (pid == n - 1) if last is None else last & (pid == n - 1)

        def ops(k):
            return phases[k].build(ph_in[k], ph_out[k], *ph_sems[3 * k:3 * k + 3])

        @pl.when(first)
        def _():
            for k in range(len(phases)):
                for cp in ops(k)["start"]:
                    cp.start()

        body(*refs[:n_in], *base_out, *base_scr)

        @pl.when(last)
        def _():
            for k in range(len(phases)):
                o = ops(k)
                for cp in o["recv"]:
                    cp.wait_recv()
                for cp in o["send"]:
                    cp.wait_send()
                for cp in o["local"]:
                    cp.wait()

    hbm = pl.BlockSpec(memory_space=pl.ANY)
    res = pl.pallas_call(
        wrapped, name=name, grid=grid, in_specs=list(in_specs) + [hbm] * len(ex_args),
        out_specs=o_specs + [hbm] * len(ex_out), out_shape=o_shapes + ex_out,
        scratch_shapes=list(scratch_shapes) + sems, input_output_aliases=aliases,
        compiler_params=_params(("arbitrary",) * len(grid)),
    )(*args, *ex_args)
    pos = n_out
    for ph in phases:
        ph.results = list(res[pos:pos + len(ph.out_shapes)])
        pos += len(ph.out_shapes)
    return res[0] if single else list(res[:n_out])


def _gelu_parts(x):
    k = 0.7978845608028654
    x2 = x * x
    t = jnp.tanh(k * (x + 0.044715 * (x2 * x)))
    cdf = 0.5 * (1.0 + t)
    dcdf = 0.5 * (1.0 - t * t) * (k * (1.0 + 3.0 * 0.044715 * x2))
    return x * cdf, cdf + x * dcdf


def _gelu(x):
    t = jnp.tanh(0.7978845608028654 * (x + 0.044715 * (x * x * x)))
    return x * (0.5 * (1.0 + t))


def _rowsum(v):
    return jnp.sum(v, axis=0, keepdims=True)


def _ln_stats(r):
    mu = jnp.mean(r, axis=-1, keepdims=True)
    xc = r - mu
    var = jnp.mean(xc * xc, axis=-1, keepdims=True)
    rstd = lax.rsqrt(var + LN_EPS)
    return xc * rstd, rstd


def _ln_bwd(dy, xhat, rstd, gain):
    dxh = dy * gain
    m1 = jnp.mean(dxh, axis=-1, keepdims=True)
    m2 = jnp.mean(dxh * xhat, axis=-1, keepdims=True)
    return rstd * (dxh - m1 - xhat * m2)


def _matmul(a, b, *, dn, grid, a_spec, b_spec, o_spec, out_shape, acc_shape, name, phases=(), into=None):
    nk = grid[2]
    direct = out_shape.dtype == F32

    def body(a_ref, b_ref, *rest):
        o_ref, scratch = (rest[1], rest[2:]) if into is not None else (rest[0], rest[1:])
        prod = lax.dot_general(a_ref[...], b_ref[...], (dn, ((), ())), preferred_element_type=F32)
        if nk == 1:
            o_ref[...] = prod.astype(o_ref.dtype)
            return
        acc = o_ref if direct else scratch[0]
        k = pl.program_id(2)

        @pl.when(k == 0)
        def _():
            acc[...] = prod

        @pl.when(k > 0)
        def _():
            acc[...] += prod

        if not direct:
            @pl.when(k == nk - 1)
            def _():
                o_ref[...] = acc[...].astype(o_ref.dtype)

    scratch = [] if (direct or nk == 1) else [pltpu.VMEM(acc_shape, F32)]
    args, in_specs, aliases = (a, b), [a_spec, b_spec], None
    if into is not None:
        args, in_specs, aliases = (a, b, into), in_specs + [pl.BlockSpec(memory_space=pl.ANY)], {2: 0}
    return _pcall(body, args, name=name, grid=grid, in_specs=in_specs, out_specs=o_spec,
                  out_shape=out_shape, scratch_shapes=scratch, sem=("parallel", "parallel", "arbitrary"),
                  aliases=aliases, phases=phases)


def _row_tile(m, want):
    t = min(m, want)
    assert m % t == 0
    return t


def _mm_rows(a, w, *, dn, name, out_dtype=F32, tm=2048):
    m, k = a.shape
    n = w.shape[1] if dn == NN else w.shape[0]
    tm = _row_tile(m, tm)
    return _matmul(
        a, w, dn=dn, grid=(m // tm, 1, 1), name=name,
        a_spec=pl.BlockSpec((tm, k), lambda i, j, kk: (i, 0)),
        b_spec=pl.BlockSpec(w.shape, lambda i, j, kk: (0, 0)),
        o_spec=pl.BlockSpec((tm, n), lambda i, j, kk: (i, 0)),
        out_shape=jax.ShapeDtypeStruct((m, n), out_dtype), acc_shape=(tm, n))


def _mm_tn(a, b, *, name, tk=2048):
    m, ka = a.shape
    n = b.shape[1]
    tk = _row_tile(m, tk)
    return _matmul(
        a, b, dn=TN, grid=(1, 1, m // tk), name=name,
        a_spec=pl.BlockSpec((tk, ka), lambda i, j, kk: (kk, 0)),
        b_spec=pl.BlockSpec((tk, n), lambda i, j, kk: (kk, 0)),
        o_spec=pl.BlockSpec((ka, n), lambda i, j, kk: (0, 0)),
        out_shape=jax.ShapeDtypeStruct((ka, n), GRAD_DTYPE), acc_shape=(ka, n))


def _mod_matmul(x, mod, w8, bias8, *, flat_out, name, tm=2048, phases=()):
    m, k = x.shape
    nb, _, ns = w8.shape
    tm = _row_tile(m, tm)

    def body(x_ref, mod_ref, w_ref, b_ref, o_ref, h_ref, hs):
        @pl.when(pl.program_id(1) == 0)
        def _():
            h = (x_ref[...] * mod_ref[0:1, :] + mod_ref[1:2, :]).astype(BF)
            hs[...] = h
            h_ref[...] = h

        o_ref[...] = (jnp.dot(hs[...], w_ref[...], preferred_element_type=F32) + b_ref[...]).astype(o_ref.dtype)

    if flat_out:
        o_spec = pl.BlockSpec((tm, ns), lambda i, j: (i, j))
        o_shape = jax.ShapeDtypeStruct((m, nb * ns), ACT_DTYPE)
    else:
        o_spec = pl.BlockSpec((None, tm, ns), lambda i, j: (j, i, 0))
        o_shape = jax.ShapeDtypeStruct((nb, m, ns), ACT_DTYPE)
    return _pcall(
        body, (x, mod, w8, bias8), name=name, grid=(m // tm, nb),
        in_specs=[pl.BlockSpec((tm, k), lambda i, j: (i, 0)),
                  pl.BlockSpec((2, k), lambda i, j: (0, 0)),
                  pl.BlockSpec((None, k, ns), lambda i, j: (j, 0, 0)),
                  pl.BlockSpec((None, 1, ns), lambda i, j: (j, 0, 0))],
        out_specs=[o_spec, pl.BlockSpec((tm, k), lambda i, j: (i, 0))],
        out_shape=[o_shape, jax.ShapeDtypeStruct((m, k), BF)],
        scratch_shapes=[pltpu.VMEM((tm, k), BF)], sem=("parallel", "arbitrary"), phases=phases)


def _seg_spec(tm, d, s):
    return pl.BlockSpec((tm, d), lambda i, s=s: (i, s))


def _prev_halo_spec(tm, d, s):
    hb = tm // HALO
    return pl.BlockSpec((HALO, d), lambda i, s=s: (jnp.maximum(i * hb - 1, 0), s))


def _next_halo_spec(tm, d, s, m):
    hb = tm // HALO
    last = m // HALO - 1
    return pl.BlockSpec((HALO, d), lambda i, s=s: (jnp.minimum((i + 1) * hb, last), s))


def _spatial_mix(wm_ref, src, dst, bias_ref, tm, d):
    for n in range(tm // GMLP_BLOCK):
        for g in range(d // GMLP_BLOCK):
            rs = slice(n * GMLP_BLOCK, (n + 1) * GMLP_BLOCK)
            cs = slice(g * GMLP_BLOCK, (g + 1) * GMLP_BLOCK)
            v = jnp.dot(wm_ref[g], src[rs, cs], preferred_element_type=F32)
            if bias_ref is not None:
                v = v + bias_ref[:, cs]
            dst[rs, cs] = v


def _mix_fwd(z, conv_a, lnv, wm, bias_full, *, name, tm=256, phases=()):
    m, d9 = z.shape
    d = d9 // 9
    tm = _row_tile(m, tm)
    grp = d // len(POOL_WINDOWS)

    def body(zb, zc, zx, zu, zv, zp, zc_h, zx_h, zp_h, ca_ref, lnv_ref, wm_ref, bias_ref,
             ua_ref, ub_ref, d_ref, ext, vn_s, mixed_s):
        i = pl.program_id(0)
        first = i == 0
        f32 = lambda r: r[...].astype(F32)
        pa = f32(zc) * f32(zx)
        ext[0:HALO, :] = jnp.where(first, 0.0, f32(zc_h) * f32(zx_h))
        ext[HALO:HALO + tm, :] = pa
        w = ca_ref[...]
        conv = w[0:1, :] * ext[pl.ds(HALO - 2, tm), :] + w[1:2, :] * ext[pl.ds(HALO - 1, tm), :] + w[2:3, :] * pa
        ua_ref[...] = (f32(zb) * conv).astype(BF)
        p = f32(zp)
        ext[0:HALO, :] = jnp.where(first, 0.0, f32(zp_h))
        ext[HALO:HALO + tm, :] = p
        t = (i * tm + lax.broadcasted_iota(jnp.int32, (tm, 1), 0) + 1).astype(F32)
        for k, win in enumerate(POOL_WINDOWS):
            cs = slice(k * grp, (k + 1) * grp)
            s = p[:, cs]
            for j in range(1, win):
                s = s + ext[pl.ds(HALO - j, tm), cs]
            d_ref[:, cs] = (s / jnp.minimum(t, float(win)) - p[:, cs]).astype(BF)
        gv = _gelu(f32(zv))
        vhat, _ = _ln_stats(gv)
        vn_s[...] = (vhat * lnv_ref[0:1, :] + lnv_ref[1:2, :]).astype(BF)
        _spatial_mix(wm_ref, vn_s, mixed_s, bias_ref, tm, d)
        ub_ref[...] = (_gelu(f32(zu)) * mixed_s[...]).astype(BF)

    full = lambda a: pl.BlockSpec(a.shape, lambda i: (0,) * a.ndim)
    out = jax.ShapeDtypeStruct((m, d), BF)
    o_spec = pl.BlockSpec((tm, d), lambda i: (i, 0))
    return _pcall(
        body, (z, z, z, z, z, z, z, z, z, conv_a, lnv, wm, bias_full), name=name, grid=(m // tm,),
        in_specs=[_seg_spec(tm, d, s) for s in range(6)] + [_prev_halo_spec(tm, d, s) for s in (1, 2, 5)]
        + [full(conv_a), full(lnv), full(wm), full(bias_full)],
        out_specs=[o_spec, o_spec, o_spec], out_shape=[out, out, out],
        scratch_shapes=[pltpu.VMEM((HALO + tm, d), F32), pltpu.VMEM((tm, d), BF), pltpu.VMEM((tm, d), F32)],
        sem=("arbitrary",), phases=phases)


def _pool_proj(dd, w_pool, *, dn, name, out_dtype=F32, tm=512):
    m, d = dd.shape
    ng, grp, _ = w_pool.shape
    tm = _row_tile(m, tm)
    return _matmul(
        dd, w_pool, dn=dn, grid=(m // tm, ng, 1), name=name,
        a_spec=pl.BlockSpec((tm, grp), lambda i, j, kk: (i, j)),
        b_spec=pl.BlockSpec((None, grp, grp), lambda i, j, kk: (j, 0, 0)),
        o_spec=pl.BlockSpec((tm, grp), lambda i, j, kk: (i, j)),
        out_shape=jax.ShapeDtypeStruct((m, d), out_dtype), acc_shape=(tm, grp))


def _merge(z, ya, yb, ycp, scale, *, name, tm=512):
    m, d = ya.shape
    tm = _row_tile(m, tm)

    def body(ga, gb, gc, ya_ref, yb_ref, yc_ref, sc_ref, o_ref):
        f32 = lambda r: r[...].astype(F32)
        o_ref[...] = (jax.nn.sigmoid(f32(ga)) * f32(ya_ref) + jax.nn.sigmoid(f32(gb)) * f32(yb_ref)
                      + jax.nn.sigmoid(f32(gc)) * (f32(yc_ref) * sc_ref[...])).astype(BF)

    row = pl.BlockSpec((tm, d), lambda i: (i, 0))
    return pl.pallas_call(
        body, name=name, grid=(m // tm,),
        in_specs=[_seg_spec(tm, d, 6), _seg_spec(tm, d, 7), _seg_spec(tm, d, 8), row, row, row,
                  pl.BlockSpec((1, d), lambda i: (0, 0))],
        out_specs=row, out_shape=jax.ShapeDtypeStruct((m, d), BF),
        compiler_params=_params(("parallel",)),
    )(z, z, z, ya, yb, ycp, scale)


def _resid_ln(xp, ys, vec, alpha, *, name, tm=512):
    m, d = xp.shape
    tm = _row_tile(m, tm)

    def body(xp_ref, ys_ref, v_ref, o_ref):
        xhat, _ = _ln_stats(alpha * xp_ref[...] + v_ref[0:1, :] * ys_ref[...])
        o_ref[...] = xhat * v_ref[1:2, :] + v_ref[2:3, :]

    row = pl.BlockSpec((tm, d), lambda i: (i, 0))
    return pl.pallas_call(
        body, name=name, grid=(m // tm,),
        in_specs=[row, row, pl.BlockSpec(vec.shape, lambda i: (0, 0))],
        out_specs=row, out_shape=jax.ShapeDtypeStruct((m, d), F32),
        compiler_params=_params(("parallel",)),
    )(xp, ys, vec)


def _ffn_fwd(up4, cw, cb, *, name, tm=512, phases=()):
    _, nj, m, fs = up4.shape
    tm = _row_tile(m, tm)
    hb = tm // HALO

    def body(up_ref, ah_ref, cw_ref, cb_ref, f_ref, ext):
        first = pl.program_id(1) == 0
        ext[0:HALO, :] = jnp.where(first, 0.0, ah_ref[...].astype(F32))
        ext[HALO:HALO + tm, :] = up_ref[0].astype(F32)
        w = cw_ref[...]
        w0, w1, w2, bias = w[0:1, :], w[1:2, :], w[2:3, :], cb_ref[...]
        rc = 16

        def step(c, carry):
            r0 = pl.multiple_of(c * rc, rc)
            win = ext[pl.ds(r0 + HALO - 8, rc + 8), :]
            a0, a1, a2 = win[8:8 + rc], pltpu.roll(win, 1, 0)[8:8 + rc], pltpu.roll(win, 2, 0)[8:8 + rc]
            ca = w0 * a2 + w1 * a1 + w2 * a0 + bias
            f_ref[pl.ds(r0, rc), :] = (_gelu(ca) * up_ref[1, pl.ds(r0, rc), :].astype(F32)).astype(BF)
            return carry

        lax.fori_loop(0, tm // rc, step, 0)

    return _pcall(
        body, (up4, up4, cw, cb), name=name, grid=(nj, m // tm),
        in_specs=[pl.BlockSpec((2, None, tm, fs), lambda j, i: (0, j, i, 0)),
                  pl.BlockSpec((None, None, HALO, fs), lambda j, i: (0, j, jnp.maximum(i * hb - 1, 0), 0)),
                  pl.BlockSpec((None, 3, fs), lambda j, i: (j, 0, 0)),
                  pl.BlockSpec((None, 1, fs), lambda j, i: (j, 0, 0))],
        out_specs=pl.BlockSpec((None, tm, fs), lambda j, i: (j, i, 0)),
        out_shape=jax.ShapeDtypeStruct((nj, m, fs), BF),
        scratch_shapes=[pltpu.VMEM((HALO + tm, fs), F32)], sem=("parallel", "arbitrary"), phases=phases)


def _down_proj(f4, wd4, *, name, tm=2048, phases=()):
    nj, m, fs = f4.shape
    d = wd4.shape[2]
    tm = _row_tile(m, tm)
    return _matmul(
        f4, wd4, dn=NN, grid=(m // tm, 1, nj), name=name,
        a_spec=pl.BlockSpec((None, tm, fs), lambda i, j, kk: (kk, i, 0)),
        b_spec=pl.BlockSpec((None, fs, d), lambda i, j, kk: (kk, 0, 0)),
        o_spec=pl.BlockSpec((tm, d), lambda i, j, kk: (i, 0)),
        out_shape=jax.ShapeDtypeStruct((m, d), F32), acc_shape=(tm, d), phases=phases)


def _loss_grad(y, tgt, *, name, tm=512):
    m, d = y.shape
    tm = _row_tile(m, tm)
    ni = m // tm

    def body(y_ref, t_ref, dy_ref, l_ref, acc):
        i = pl.program_id(0)
        e = y_ref[...] - t_ref[...]
        dy_ref[...] = e * (1.0 / d)
        part = jnp.sum((e * e).reshape(tm // 8, 8, d), axis=0)

        @pl.when(i == 0)
        def _():
            acc[...] = part

        @pl.when(i > 0)
        def _():
            acc[...] += part

        @pl.when(i == ni - 1)
        def _():
            l_ref[...] = jnp.full((8, 128), 0.5 / d, F32) * jnp.sum(acc[...])

    row = pl.BlockSpec((tm, d), lambda i: (i, 0))
    return pl.pallas_call(
        body, name=name, grid=(ni,), in_specs=[row, row],
        out_specs=[row, pl.BlockSpec((8, 128), lambda i: (0, 0))],
        out_shape=[jax.ShapeDtypeStruct((m, d), F32), jax.ShapeDtypeStruct((8, 128), F32)],
        scratch_shapes=[pltpu.VMEM((8, d), F32)],
        compiler_params=_params(("arbitrary",)),
    )(y, tgt)


def _resid_ln_bwd(dpart, dh, xmod, mvec, xp, ys, vec, alpha, *, name, tm=256, phases=()):
    m, d = dpart.shape
    tm = _row_tile(m, tm)
    has_dh = dh is not None
    has_ln = xp is not None

    def body(*refs):
        refs = list(refs)
        dpart_ref = refs.pop(0)
        if has_dh:
            dh_ref, xm_ref, mv_ref = refs.pop(0), refs.pop(0), refs.pop(0)
        if has_ln:
            xp_ref, ys_ref, v_ref = refs.pop(0), refs.pop(0), refs.pop(0)
            dys_ref, dxp_ref, red_ref = refs
        else:
            dx_ref, red_ref = refs
        i = pl.program_id(0)
        dtot = dpart_ref[...]
        rows = [jnp.zeros((1, d), F32)] * 5
        if has_dh:
            dhv = dh_ref[...]
            dtot = dtot + dhv * mv_ref[...]
            rows[0] = _rowsum(dhv * xm_ref[...])
            rows[1] = _rowsum(dhv)
        if has_ln:
            ys = ys_ref[...]
            gt = v_ref[0:1, :]
            xhat, rstd = _ln_stats(alpha * xp_ref[...] + gt * ys)
            rows[2] = _rowsum(dtot * xhat)
            rows[3] = _rowsum(dtot)
            dr = _ln_bwd(dtot, xhat, rstd, v_ref[1:2, :])
            rows[4] = _rowsum(dr * ys)
            dys_ref[...] = (dr * gt).astype(BF)
            dxp_ref[...] = alpha * dr
        else:
            dx_ref[...] = dtot
        red = jnp.concatenate(rows + [jnp.zeros((3, d), F32)], axis=0)

        @pl.when(i == 0)
        def _():
            red_ref[...] = red

        @pl.when(i > 0)
        def _():
            red_ref[...] += red

    row = pl.BlockSpec((tm, d), lambda i: (i, 0))
    vrow = lambda a: pl.BlockSpec(a.shape, lambda i: (0, 0))
    args, specs = [dpart], [row]
    if has_dh:
        args += [dh, xmod, mvec]
        specs += [row, row, vrow(mvec)]
    if has_ln:
        args += [xp, ys, vec]
        specs += [row, row, vrow(vec)]
        out_specs = [row, row, pl.BlockSpec((8, d), lambda i: (0, 0))]
        out_shape = [jax.ShapeDtypeStruct((m, d), BF), jax.ShapeDtypeStruct((m, d), F32),
                     jax.ShapeDtypeStruct((8, d), F32)]
    else:
        out_specs = [row, pl.BlockSpec((8, d), lambda i: (0, 0))]
        out_shape = [jax.ShapeDtypeStruct((m, d), F32), jax.ShapeDtypeStruct((8, d), F32)]
    return _pcall(body, args, name=name, grid=(m // tm,), in_specs=specs, out_specs=out_specs, out_shape=out_shape,
                  sem=("arbitrary",), phases=phases)


def _down_bwd(dy, wd4, *, name, tm=2048, phases=()):
    m, d = dy.shape
    nj, fs, _ = wd4.shape
    tm = _row_tile(m, tm)
    return _matmul(
        dy, wd4, dn=NT, grid=(m // tm, nj, 1), name=name,
        a_spec=pl.BlockSpec((tm, d), lambda i, j, kk: (i, 0)),
        b_spec=pl.BlockSpec((None, fs, d), lambda i, j, kk: (j, 0, 0)),
        o_spec=pl.BlockSpec((None, tm, fs), lambda i, j, kk: (j, i, 0)),
        out_shape=jax.ShapeDtypeStruct((nj, m, fs), ACT_DTYPE), acc_shape=(tm, fs), phases=phases)


def _tn_shards_lhs(f4, dy, *, name, tk=2048, phases=()):
    nj, m, fs = f4.shape
    d = dy.shape[1]
    tk = _row_tile(m, tk)
    return _matmul(
        f4, dy, dn=TN, grid=(nj, 1, m // tk), name=name,
        a_spec=pl.BlockSpec((None, tk, fs), lambda i, j, kk: (i, kk, 0)),
        b_spec=pl.BlockSpec((tk, d), lambda i, j, kk: (kk, 0)),
        o_spec=pl.BlockSpec((None, fs, d), lambda i, j, kk: (i, 0, 0)),
        out_shape=jax.ShapeDtypeStruct((nj, fs, d), GRAD_DTYPE), acc_shape=(fs, d), phases=phases)


def _tn_shards_rhs(h, d8, *, name, tk=2048, phases=()):
    m, k = h.shape
    nb, _, ns = d8.shape
    tk = _row_tile(m, tk)
    return _matmul(
        h, d8, dn=TN, grid=(nb, 1, m // tk), name=name,
        a_spec=pl.BlockSpec((tk, k), lambda i, j, kk: (kk, 0)),
        b_spec=pl.BlockSpec((None, tk, ns), lambda i, j, kk: (i, kk, 0)),
        o_spec=pl.BlockSpec((None, k, ns), lambda i, j, kk: (i, 0, 0)),
        out_shape=jax.ShapeDtypeStruct((nb, k, ns), GRAD_DTYPE), acc_shape=(k, ns), phases=phases)


def _tn_cols_rhs(h, dz, nb, *, name, tk=2048, phases=()):
    m, k = h.shape
    ns = dz.shape[1] // nb
    tk = _row_tile(m, tk)
    return _matmul(
        h, dz, dn=TN, grid=(nb, 1, m // tk), name=name,
        a_spec=pl.BlockSpec((tk, k), lambda i, j, kk: (kk, 0)),
        b_spec=pl.BlockSpec((tk, ns), lambda i, j, kk: (kk, i)),
        o_spec=pl.BlockSpec((None, k, ns), lambda i, j, kk: (i, 0, 0)),
        out_shape=jax.ShapeDtypeStruct((nb, k, ns), GRAD_DTYPE), acc_shape=(k, ns), phases=phases)


def _nt_shards(d8, w8, *, name, tm=2048, phases=()):
    nb, m, ns = d8.shape
    k = w8.shape[1]
    tm = _row_tile(m, tm)
    return _matmul(
        d8, w8, dn=NT, grid=(m // tm, 1, nb), name=name,
        a_spec=pl.BlockSpec((None, tm, ns), lambda i, j, kk: (kk, i, 0)),
        b_spec=pl.BlockSpec((None, k, ns), lambda i, j, kk: (kk, 0, 0)),
        o_spec=pl.BlockSpec((tm, k), lambda i, j, kk: (i, 0)),
        out_shape=jax.ShapeDtypeStruct((m, k), F32), acc_shape=(tm, k), phases=phases)


def _nt_cols(dz, w8, *, name, tm=2048, phases=(), tiles=None, into=None):
    m = dz.shape[0]
    nb, k, ns = w8.shape
    tm = _row_tile(m, tm)
    first, count = tiles if tiles is not None else (0, m // tm)
    return _matmul(
        dz, w8, dn=NT, grid=(count, 1, nb), name=name,
        a_spec=pl.BlockSpec((tm, ns), lambda i, j, kk: (i + first, kk)),
        b_spec=pl.BlockSpec((None, k, ns), lambda i, j, kk: (kk, 0, 0)),
        o_spec=pl.BlockSpec((tm, k), lambda i, j, kk: (i + first, 0)),
        out_shape=jax.ShapeDtypeStruct((m, k), F32), acc_shape=(tm, k), phases=phases, into=into)


def _tn_pool(dd, dyc, ng, *, name, tk=2048):
    m, d = dd.shape
    grp = d // ng
    tk = _row_tile(m, tk)
    return _matmul(
        dd, dyc, dn=TN, grid=(ng, 1, m // tk), name=name,
        a_spec=pl.BlockSpec((tk, grp), lambda i, j, kk: (kk, i)),
        b_spec=pl.BlockSpec((tk, grp), lambda i, j, kk: (kk, i)),
        o_spec=pl.BlockSpec((None, grp, grp), lambda i, j, kk: (i, 0, 0)),
        out_shape=jax.ShapeDtypeStruct((ng, grp, grp), GRAD_DTYPE), acc_shape=(grp, grp))


def _ffn_bwd(up4, df4, cw, cb, *, name, tm=256, phases=()):
    _, nj, m, fs = up4.shape
    tm = _row_tile(m, tm)
    hb = tm // HALO
    ni = m // tm
    last_hb = m // HALO - 1
    ext_rows = tm + 8

    rc = 16
    assert tm % rc == 0

    def body(up_ref, ap_ref, un_ref, df_ref, dfn_ref, cw_ref, cb_ref, dup_ref, red_ref, ext, dca_s, racc):
        i = pl.program_id(1)
        ext[0:HALO, :] = jnp.where(i == 0, 0.0, ap_ref[...].astype(F32))
        ext[HALO:HALO + tm, :] = up_ref[0].astype(F32)
        ext[HALO + tm:2 * HALO + tm, :] = un_ref[0].astype(F32)
        racc[...] = jnp.zeros_like(racc)
        w = cw_ref[...]
        w0, w1, w2, bias = w[0:1, :], w[1:2, :], w[2:3, :], cb_ref[...]

        def conv_taps(win, n):
            return (win[8:8 + n], pltpu.roll(win, 1, 0)[8:8 + n], pltpu.roll(win, 2, 0)[8:8 + n])

        def fold(v):
            return v[0:8] + v[8:16]

        def add_red(k, v8):
            racc[8 * k:8 * k + 8, :] += v8

        def first_pass(c, carry):
            r0 = pl.multiple_of(c * rc, rc)
            a0, a1, a2 = conv_taps(ext[pl.ds(r0 + HALO - 8, rc + 8), :], rc)
            act, dact = _gelu_parts(w0 * a2 + w1 * a1 + w2 * a0 + bias)
            dfc = df_ref[pl.ds(r0, rc), :].astype(F32)
            dca = dfc * up_ref[1, pl.ds(r0, rc), :].astype(F32) * dact
            dup_g = dfc * act
            dca_s[pl.ds(r0, rc), :] = dca
            dup_ref[1, pl.ds(r0, rc), :] = dup_g.astype(BF)
            for k, v in enumerate((dca * a2, dca * a1, dca * a0, dca, dup_g)):
                add_red(k if k < 4 else 5, fold(v))
            return carry

        lax.fori_loop(0, tm // rc, first_pass, 0)
        a0, a1, a2 = conv_taps(ext[HALO + tm - 8:HALO + tm + 8, :], 8)
        _, dact = _gelu_parts(w0 * a2 + w1 * a1 + w2 * a0 + bias)
        after = dfn_ref[...].astype(F32)[0:8, :] * un_ref[1].astype(F32)[0:8, :] * dact
        dca_s[tm:tm + 8, :] = jnp.where(i < ni - 1, after, 0.0)
        dca_s[tm + 8:tm + 16, :] = jnp.zeros((8, fs), F32)

        def second_pass(c, carry):
            r0 = pl.multiple_of(c * rc, rc)
            win = dca_s[pl.ds(r0, rc + 8), :]
            up1, up2 = pltpu.roll(win, rc + 7, 0)[0:rc], pltpu.roll(win, rc + 6, 0)[0:rc]
            dup_a = w2 * win[0:rc] + w1 * up1 + w0 * up2
            dup_ref[0, pl.ds(r0, rc), :] = dup_a.astype(BF)
            add_red(4, fold(dup_a))
            return carry

        lax.fori_loop(0, tm // rc, second_pass, 0)
        red = jnp.concatenate([_rowsum(racc[8 * k:8 * k + 8, :]) for k in range(6)] + [jnp.zeros((2, fs), F32)],
                              axis=0)

        @pl.when(i == 0)
        def _():
            red_ref[...] = red

        @pl.when(i > 0)
        def _():
            red_ref[...] += red

    nxt = lambda j, i: jnp.minimum((i + 1) * hb, last_hb)
    return _pcall(
        body, (up4, up4, up4, df4, df4, cw, cb), name=name, grid=(nj, ni), sem=("parallel", "arbitrary"), phases=phases,
        in_specs=[pl.BlockSpec((2, None, tm, fs), lambda j, i: (0, j, i, 0)),
                  pl.BlockSpec((None, None, HALO, fs), lambda j, i: (0, j, jnp.maximum(i * hb - 1, 0), 0)),
                  pl.BlockSpec((2, None, HALO, fs), lambda j, i: (0, j, nxt(j, i), 0)),
                  pl.BlockSpec((None, tm, fs), lambda j, i: (j, i, 0)),
                  pl.BlockSpec((None, HALO, fs), lambda j, i: (j, nxt(j, i), 0)),
                  pl.BlockSpec((None, 3, fs), lambda j, i: (j, 0, 0)),
                  pl.BlockSpec((None, 1, fs), lambda j, i: (j, 0, 0))],
        out_specs=[pl.BlockSpec((2, None, tm, fs), lambda j, i: (0, j, i, 0)),
                   pl.BlockSpec((None, 8, fs), lambda j, i: (j, 0, 0))],
        out_shape=[jax.ShapeDtypeStruct((2, nj, m, fs), BF), jax.ShapeDtypeStruct((nj, 8, fs), F32)],
        scratch_shapes=[pltpu.VMEM((2 * HALO + tm, fs), F32), pltpu.VMEM((tm + 16, fs), F32),
                        pltpu.VMEM((48, fs), F32)])


def _gate_bwd(dm, z, ya, yb, ycp, scale, *, name, tm=256):
    m, d = dm.shape
    tm = _row_tile(m, tm)

    def body(dm_ref, ga, gb, gc, ya_ref, yb_ref, yc_ref, sc_ref, dya_ref, dyb_ref, dyc_ref, dz_ref, red_ref):
        i = pl.program_id(0)
        f32 = lambda r: r[...].astype(F32)
        dmv = f32(dm_ref)
        sa, sb, sc = jax.nn.sigmoid(f32(ga)), jax.nn.sigmoid(f32(gb)), jax.nn.sigmoid(f32(gc))
        scale_v = sc_ref[...]
        ycp_v = f32(yc_ref)
        dya_ref[...] = (dmv * sa).astype(BF)
        dyb_ref[...] = (dmv * sb).astype(BF)
        dyc = dmv * sc
        dyc_ref[...] = (dyc * scale_v).astype(BF)
        dga = dmv * f32(ya_ref) * (sa * (1.0 - sa))
        dgb = dmv * f32(yb_ref) * (sb * (1.0 - sb))
        dgc = dmv * (ycp_v * scale_v) * (sc * (1.0 - sc))
        dz_ref[:, 0:d] = dga.astype(BF)
        dz_ref[:, d:2 * d] = dgb.astype(BF)
        dz_ref[:, 2 * d:3 * d] = dgc.astype(BF)
        red = jnp.concatenate([_rowsum(dyc * ycp_v), _rowsum(dga), _rowsum(dgb), _rowsum(dgc),
                               jnp.zeros((4, d), F32)], axis=0)

        @pl.when(i == 0)
        def _():
            red_ref[...] = red

        @pl.when(i > 0)
        def _():
            red_ref[...] += red

    row = pl.BlockSpec((tm, d), lambda i: (i, 0))
    obf = jax.ShapeDtypeStruct((m, d), BF)
    return pl.pallas_call(
        body, name=name, grid=(m // tm,),
        in_specs=[row, _seg_spec(tm, d, 6), _seg_spec(tm, d, 7), _seg_spec(tm, d, 8), row, row, row,
                  pl.BlockSpec((1, d), lambda i: (0, 0))],
        out_specs=[row, row, row, pl.BlockSpec((tm, 3 * d), lambda i: (i, 2)), pl.BlockSpec((8, d), lambda i: (0, 0))],
        out_shape=[obf, obf, obf, jax.ShapeDtypeStruct((m, 9 * d), BF), jax.ShapeDtypeStruct((8, d), F32)],
        compiler_params=_params(("arbitrary",)),
    )(dm, z, z, z, ya, yb, ycp, scale)


def _mix_bwd(dz, dua, dub, ddd, z, conv_a, lnv, wm, wmt, bias_full, mask, *, name, tm=128, phases=()):
    m, d = dua.shape
    tm = _row_tile(m, tm)
    ni = m // tm
    grp = d // len(POOL_WINDOWS)
    ng = d // GMLP_BLOCK
    ext_rows = tm + 8

    def body(dz_in, dua_ref, dub_ref, dd_ref, zb, zc, zx, zu, zv, zp, zc_h, zx_h, dua_n, zb_n, dd_n,
             ca_ref, lnv_ref, wm_ref, wmt_ref, bias_ref, mask_ref,
             dz_ref, red_ref, dws_ref, dbs_ref, ext, sh_s, vn_s, mixed_s, dmx_s, dvn_s, dbs_acc):
        del dz_in
        i = pl.program_id(0)
        rows = []
        f32 = lambda r: r[...].astype(F32)
        zbv, zcv, zxv = f32(zb), f32(zc), f32(zx)
        pa = zcv * zxv
        ext[0:HALO, :] = jnp.where(i == 0, 0.0, f32(zc_h) * f32(zx_h))
        ext[HALO:HALO + tm, :] = pa
        w = ca_ref[...]
        w0, w1, w2 = w[0:1, :], w[1:2, :], w[2:3, :]
        p1 = ext[pl.ds(HALO - 1, tm), :]
        p2 = ext[pl.ds(HALO - 2, tm), :]
        conv = w0 * p2 + w1 * p1 + w2 * pa
        duav = f32(dua_ref)
        dzb = duav * conv
        dca = duav * zbv
        dca_n = jnp.where(i < ni - 1, f32(dua_n)[0:8, :] * f32(zb_n)[0:8, :], 0.0)
        sh_s[0:tm, :] = dca
        sh_s[tm:tm + 8, :] = dca_n
        dpa = w2 * dca + w1 * sh_s[pl.ds(1, tm), :] + w0 * sh_s[pl.ds(2, tm), :]
        dzc = dpa * zxv
        dzx = dpa * zcv
        dz_ref[:, 0:d] = dzb.astype(BF)
        dz_ref[:, d:2 * d] = dzc.astype(BF)
        dz_ref[:, 2 * d:3 * d] = dzx.astype(BF)
        rows += [_rowsum(dzb), _rowsum(dzc), _rowsum(dzx)]
        dconv = [_rowsum(dca * p2), _rowsum(dca * p1), _rowsum(dca * pa)]
        zuv, zvv = f32(zu), f32(zv)
        gu, dgu_dz = _gelu_parts(zuv)
        gv, dgv_dz = _gelu_parts(zvv)
        vhat, rstd = _ln_stats(gv)
        gain = lnv_ref[0:1, :]
        vn_s[...] = (vhat * gain + lnv_ref[1:2, :]).astype(BF)
        _spatial_mix(wm_ref, vn_s, mixed_s, bias_ref, tm, d)
        dubv = f32(dub_ref)
        dzu = dubv * mixed_s[...] * dgu_dz
        dmixed = dubv * gu
        dmx_s[...] = dmixed.astype(BF)
        _spatial_mix(wmt_ref, dmx_s, dvn_s, None, tm, d)
        dvn = dvn_s[...]
        dzv = _ln_bwd(dvn, vhat, rstd, gain) * dgv_dz
        dz_ref[:, 3 * d:4 * d] = dzu.astype(BF)
        dz_ref[:, 4 * d:5 * d] = dzv.astype(BF)
        rows += [_rowsum(dzu), _rowsum(dzv)]
        dlnv = [_rowsum(dvn * vhat), _rowsum(dvn)]
        dbs_part = dmixed[0:GMLP_BLOCK, :]
        for n in range(1, tm // GMLP_BLOCK):
            dbs_part = dbs_part + dmixed[n * GMLP_BLOCK:(n + 1) * GMLP_BLOCK, :]
        ddv = f32(dd_ref)
        t = (i * tm + lax.broadcasted_iota(jnp.int32, (ext_rows + 8, 1), 0) + 1).astype(F32)
        dde = jnp.concatenate([ddv, jnp.where(i < ni - 1, f32(dd_n), 0.0)], axis=0)
        for k, win in enumerate(POOL_WINDOWS):
            cs = slice(k * grp, (k + 1) * grp)
            ext[0:tm + HALO, cs] = dde[:, cs] / jnp.minimum(t, float(win))
        dzp_parts = []
        for k, win in enumerate(POOL_WINDOWS):
            cs = slice(k * grp, (k + 1) * grp)
            s = ext[0:tm, cs]
            for j in range(1, win):
                s = s + ext[pl.ds(j, tm), cs]
            dzp_parts.append(s - ddv[:, cs])
        dzp = jnp.concatenate(dzp_parts, axis=1)
        dz_ref[:, 5 * d:6 * d] = dzp.astype(BF)
        rows += [_rowsum(dzp)]
        red = jnp.concatenate(rows + dconv + dlnv + [jnp.zeros((5, d), F32)], axis=0)

        @pl.when(i == 0)
        def _():
            red_ref[...] = red
            dbs_acc[...] = dbs_part
            dws_ref[...] = jnp.zeros_like(dws_ref)

        @pl.when(i > 0)
        def _():
            red_ref[...] += red
            dbs_acc[...] += dbs_part

        for n in range(tm // GMLP_BLOCK):
            for g in range(ng):
                rs = slice(n * GMLP_BLOCK, (n + 1) * GMLP_BLOCK)
                cs = slice(g * GMLP_BLOCK, (g + 1) * GMLP_BLOCK)
                dws_ref[g] += mask_ref[...] * lax.dot_general(
                    dmx_s[rs, cs], vn_s[rs, cs], (NT, ((), ())), preferred_element_type=F32)

        @pl.when(i == ni - 1)
        def _():
            lane = lax.broadcasted_iota(jnp.int32, (GMLP_BLOCK, GMLP_BLOCK), 1)
            out = jnp.zeros((GMLP_BLOCK, GMLP_BLOCK), F32)
            for g in range(ng):
                sg = jnp.sum(dbs_acc[:, g * GMLP_BLOCK:(g + 1) * GMLP_BLOCK], axis=1, keepdims=True)
                out = out + jnp.where(lane == g, sg, 0.0)
            dbs_ref[...] = out

    row = pl.BlockSpec((tm, d), lambda i: (i, 0))
    full = lambda a: pl.BlockSpec(a.shape, lambda i: (0,) * a.ndim)
    hb = tm // HALO
    last_hb = m // HALO - 1
    nrow = pl.BlockSpec((HALO, d), lambda i: (jnp.minimum((i + 1) * hb, last_hb), 0))
    return _pcall(
        body, (dz, dua, dub, ddd, z, z, z, z, z, z, z, z, dua, z, ddd, conv_a, lnv, wm, wmt, bias_full, mask),
        name=name, grid=(ni,), sem=("arbitrary",), aliases={0: 0}, phases=phases,
        in_specs=[pl.BlockSpec(memory_space=pl.ANY), row, row, row]
        + [_seg_spec(tm, d, s) for s in range(6)]
        + [_prev_halo_spec(tm, d, 1), _prev_halo_spec(tm, d, 2), nrow, _next_halo_spec(tm, d, 0, m), nrow]
        + [full(conv_a), full(lnv), full(wm), full(wmt), full(bias_full), full(mask)],
        out_specs=[pl.BlockSpec((tm, 6 * d), lambda i: (i, 0)), pl.BlockSpec((16, d), lambda i: (0, 0)),
                   full(wm), pl.BlockSpec((GMLP_BLOCK, GMLP_BLOCK), lambda i: (0, 0))],
        out_shape=[jax.ShapeDtypeStruct(dz.shape, BF), jax.ShapeDtypeStruct((16, d), F32),
                   jax.ShapeDtypeStruct(wm.shape, F32), jax.ShapeDtypeStruct((GMLP_BLOCK, GMLP_BLOCK), F32)],
        scratch_shapes=[pltpu.VMEM((2 * HALO + tm, d), F32), pltpu.VMEM((tm + 8, d), F32),
                        pltpu.VMEM((tm, d), BF), pltpu.VMEM((tm, d), F32), pltpu.VMEM((tm, d), BF),
                        pltpu.VMEM((tm, d), F32), pltpu.VMEM((GMLP_BLOCK, d), F32)])


REST = ("w_a_out", "w_b_out", "w_pool", "w_o", "w_up", "w_down")


def _remote(src, dst, ssem, rsem, k, to):
    return pltpu.make_async_remote_copy(src_ref=src, dst_ref=dst, send_sem=ssem.at[k], recv_sem=rsem.at[k],
                                        device_id=to, device_id_type=MESH)


def _gather_phase1(shards, rows=None, onto=None):
    n = len(shards)
    rows = rows or [None] * n
    onto = onto or [None] * n
    extra = [a for a in range(n) if onto[a] is not None]

    def build(ins, outs, ssem, rsem, lsem):
        x, y, c, chips = _place()
        me = 4 * x + 2 * y + c

        def src(a):
            return ins[a] if rows[a] is None else ins[a].at[:, pl.ds(*rows[a])]

        def dst(a, dev):
            return outs[a].at[:, dev] if rows[a] is None else outs[a].at[:, dev, pl.ds(*rows[a])]

        local = [pltpu.make_async_copy(src(a), dst(a, me), lsem.at[a]) for a in range(n)]
        sends, recvs = [], []
        for j, (cx, cy) in enumerate(chips):
            for a in range(n):
                sends.append(_remote(src(a), dst(a, me), ssem, rsem, 4 * a + 1 + j, (cx, cy, c)))
                recvs.append(_remote(src(a), dst(a, 4 * cx + 2 * cy + c), ssem, rsem, 4 * a + 1 + j, (cx, cy, c)))
        for a in range(n):
            sends.append(_remote(src(a), dst(a, me), ssem, rsem, 4 * a, (x, y, 1 - c)))
            recvs.append(_remote(src(a), dst(a, 4 * x + 2 * y + 1 - c), ssem, rsem, 4 * a, (x, y, 1 - c)))
        return dict(start=local + sends, recv=recvs, send=sends, local=local)

    outs = [jax.ShapeDtypeStruct((s.shape[0], N_DEV) + s.shape[1:], s.dtype) for s in shards]
    return _Phase(list(shards) + [onto[a] for a in extra], outs, {n + k: a for k, a in enumerate(extra)},
                  4 * n, n, build)


def _gather_phase2(fulls):
    n = len(fulls)

    def build(ins, outs, ssem, rsem, lsem):
        x, y, c, chips = _place()
        sends, recvs = [], []
        for j, (cx, cy) in enumerate(chips):
            for a in range(n):
                mine, theirs = 4 * cx + 2 * cy + c, 4 * cx + 2 * cy + 1 - c
                sends.append(_remote(ins[a].at[:, mine], outs[a].at[:, mine], ssem, rsem, 3 * a + j, (x, y, 1 - c)))
                recvs.append(_remote(ins[a].at[:, theirs], outs[a].at[:, theirs], ssem, rsem, 3 * a + j, (x, y, 1 - c)))
        return dict(start=sends, recv=recvs, send=sends, local=[])

    outs = [jax.ShapeDtypeStruct(f.shape, f.dtype) for f in fulls]
    return _Phase(fulls, outs, {a: a for a in range(n)}, 3 * n, 0, build)


def _pair_phase(grads):
    n = len(grads)

    def build(ins, outs, ssem, rsem, lsem):
        x, y, c, _ = _place()
        cps = [_remote(ins[a].at[:, 2 * q + (1 - c)], outs[a].at[q], ssem, rsem, 4 * a + q, (x, y, 1 - c))
               for a in range(n) for q in range(4)]
        return dict(start=cps, recv=cps, send=cps, local=[])

    outs = [jax.ShapeDtypeStruct((4, g.shape[0]) + g.shape[2:], g.dtype) for g in grads]
    return _Phase(grads, outs, {}, 4 * n, 0, build)


def _chip_phase(bufs, accs, l, depth):
    n = len(bufs)
    has = accs is not None

    def build(ins, outs, ssem, rsem, lsem):
        x, y, c, chips = _place()
        myq = 2 * x + y
        local, sends, recvs = [], [], []
        for a in range(n):
            local.append(pltpu.make_async_copy(ins[a].at[myq], outs[a].at[myq, l], lsem.at[a]))
            for j, (cx, cy) in enumerate(chips):
                q = 2 * cx + cy
                sends.append(_remote(ins[a].at[q], outs[a].at[myq, l], ssem, rsem, 3 * a + j, (cx, cy, c)))
                recvs.append(_remote(ins[a].at[q], outs[a].at[q, l], ssem, rsem, 3 * a + j, (cx, cy, c)))
        return dict(start=local + sends, recv=recvs, send=sends, local=local)

    outs = [jax.ShapeDtypeStruct((4, depth) + b.shape[1:], b.dtype) for b in bufs]
    return _Phase(list(bufs) + (list(accs) if has else []), outs, {n + a: a for a in range(n)} if has else {},
                  3 * n, n, build)


def _grad_chunks(n, g):
    if n == "w_pool":
        return g.reshape(g.shape[0], N_DEV, g.shape[1] // N_DEV, g.shape[2])
    if n in ("w_in", "w_up"):
        return g[None]
    return g.reshape(1, N_DEV, -1, g.shape[-1])


class _ReduceScatter:
    def __init__(self, depth, cidx):
        self.depth, self.cidx, self.acc, self.count = depth, cidx, {}, 0
        self.small_gathered = None
        self.presummed = None

    def pair(self, names, grads):
        return _pair_phase([_grad_chunks(n, grads[n]) for n in names])

    def sums(self, names, grads, phase):
        out = []
        for n, r1 in zip(names, phase.results):
            out.append(_pair_sum(_grad_chunks(n, grads[n]), r1, self.cidx, name="rs_sum_%d" % self.count))
            self.count += 1
        return out

    def chip(self, names, bufs, l):
        accs = [self.acc[n] for n in names] if names[0] in self.acc else None
        return _chip_phase(bufs, accs, l, self.depth)

    def done(self, names, phase):
        for n, r in zip(names, phase.results):
            self.acc[n] = r


def _rest_views(fulls, d):
    a_out, b_out, pool, o, up, down = fulls
    grp = d // len(POOL_WINDOWS)
    return dict(w_a_out=a_out.reshape(d, d), w_b_out=b_out.reshape(d, d), w_o=o.reshape(d, d),
                w_pool=pool.reshape(len(POOL_WINDOWS), grp, grp), w_up8=up[0],
                wd4=down.reshape(N_DEV // 2, -1, d))


class _GatherPlan:
    def __init__(self):
        self.jobs, self.part, self.full = [], {}, {}

    def add(self, key, shard, first, second, rows=None, onto=None):
        self.jobs.append((key, shard, first, second, rows, onto))

    def phases(self, name):
        j1 = [j for j in self.jobs if j[2] == name]
        j2 = [j for j in self.jobs if j[3] == name]
        tagged = []
        if j1:
            onto = [self.part[j[5]] if j[5] else None for j in j1]
            tagged.append((self.part, j1, _gather_phase1([j[1] for j in j1], [j[4] for j in j1], onto)))
        if j2:
            tagged.append((self.full, j2, _gather_phase2([self.part[j[0]] for j in j2])))
        return tagged

    @staticmethod
    def collect(tagged):
        for store, jobs, phase in tagged:
            for j, r in zip(jobs, phase.results):
                store[j[0]] = r


def _layer_fwd(x, w, alpha, tag, plan=None):
    d = x.shape[1]
    grp = d // len(POOL_WINDOWS)

    def carried(kernel, *args, name, **kw):
        tagged = plan.phases(name) if plan else []
        out = kernel(*args, name=name, phases=[t[2] for t in tagged], **kw)
        _GatherPlan.collect(tagged)
        return out

    def weight(n, shape):
        return plan.full[n + tag].reshape(shape) if plan else w[n]

    z, h = carried(_mod_matmul, x, w["mod1"], w["w_in8"], w["b_in8"], flat_out=True, name="in_proj" + tag)
    ua, ub, dd = carried(_mix_fwd, z, w["conv_a"], w["lnv"], w["wm"], w["bias_full"], name="mix_fwd" + tag)
    w["w_a_out"], w["w_b_out"], w["w_o"] = (weight(n, (d, d)) for n in ("w_a_out", "w_b_out", "w_o"))
    w["w_pool"] = weight("w_pool", (len(POOL_WINDOWS), grp, grp))
    w["w_up8"] = weight("w_up8", (N_DEV, d, -1))
    ya = _mm_rows(ua, w["w_a_out"], dn=NN, name="a_out" + tag, out_dtype=ACT_DTYPE)
    yb = _mm_rows(ub, w["w_b_out"], dn=NN, name="b_out" + tag, out_dtype=ACT_DTYPE)
    ycp = _pool_proj(dd, w["w_pool"], dn=NN, name="pool_proj" + tag, out_dtype=ACT_DTYPE)
    merged = _merge(z, ya, yb, ycp, w["pool_scale"], name="merge" + tag)
    o = _mm_rows(merged, w["w_o"], dn=NN, name="o_proj" + tag)
    x1 = _resid_ln(x, o, w["ln1"], alpha, name="ln1" + tag)
    up8, h2 = carried(_mod_matmul, x1, w["mod2"], w["w_up8"], w["b_up8"], flat_out=False, name="up_proj" + tag)
    up4 = up8.reshape((2, up8.shape[0] // 2) + up8.shape[1:])
    f4 = carried(_ffn_fwd, up4, w["cw"], w["cb"], name="ffn_fwd" + tag)
    w["wd4"] = weight("wd4", (N_DEV // 2, -1, d))
    y2 = carried(_down_proj, f4, w["wd4"], name="down_proj" + tag)
    x2 = _resid_ln(x1, y2, w["ln2"], alpha, name="ln2" + tag)
    saved = dict(x=x, z=z, h=h, ua=ua, ub=ub, dd=dd, ya=ya, yb=yb, ycp=ycp, merged=merged, o=o, x1=x1,
                 up4=up4, h2=h2, f4=f4, y2=y2)
    return x2, saved


def _layer_bwd(dpart, dh_above, xmod_above, m_above, w, s, alpha, tag, l=0, above=None, rs=None, upper_reds=()):
    first, rest = ("w_in",), REST
    ph = lambda p: [p] if p is not None else ()
    pre = rs.presummed if rs is not None else None
    r1a = rs.pair(first, above) if above and not pre else None
    dy2, dx1p, red2 = _resid_ln_bwd(dpart, dh_above, xmod_above, m_above, s["x1"], s["y2"], w["ln2"], alpha,
                                    name="ln2_bwd" + tag, phases=ph(r1a))
    r1b = rs.pair(rest, above) if above and not pre else None
    df4 = _down_bwd(dy2, w["wd4"], name="down_bwd" + tag, phases=ph(r1b))
    gw_down4 = _tn_shards_lhs(s["f4"], dy2, name="gw_down" + tag)
    r3a = r3b = r3c = None
    if above:
        if pre:
            bufs, rs.presummed = pre, None
        else:
            bufs = dict(zip(first + rest, rs.sums(first, above, r1a) + rs.sums(rest, above, r1b)))
        light = tuple(n for n in rest if n != "w_up")
        r3a = rs.chip(first, [bufs[n] for n in first], l + 1)
    dup4, redf = _ffn_bwd(s["up4"], df4, w["cw"], w["cb"], name="ffn_bwd" + tag, phases=ph(r3a))
    dup8 = dup4.reshape((dup4.shape[0] * dup4.shape[1],) + dup4.shape[2:])
    if above:
        rs.done(first, r3a)
        r3b = rs.chip(("w_up",), [bufs["w_up"]], l + 1)
    gw_up8 = _tn_shards_lhs(dup8, s["h2"], name="gw_up" + tag, phases=ph(r3b))
    if above:
        rs.done(("w_up",), r3b)
        r3c = rs.chip(light, [bufs[n] for n in light], l + 1)
    own = rs is not None and l == 0
    big = dict(w_up=gw_up8, w_down=gw_down4)
    early = ("w_down", "w_up")
    o1 = rs.pair(early, big) if own else None
    dh2 = _nt_shards(dup8, w["w_up8"], name="up_bwd" + tag, phases=list(ph(r3c)) + list(ph(o1)))
    if above:
        rs.done(light, r3c)
    if own:
        sb_o = rs.sums(early, big, o1)
    do, dxp, red1 = _resid_ln_bwd(dx1p, dh2, s["x1"], w["mod2"][0:1], s["x"], s["o"], w["ln1"], alpha,
                                  name="ln1_bwd" + tag)
    dm = _mm_rows(do, w["w_o"], dn=NT, name="o_bwd" + tag, out_dtype=ACT_DTYPE)
    big["w_o"] = _mm_tn(s["merged"], do, name="gw_o" + tag)
    dya, dyb, dyc, dz, redg = _gate_bwd(dm, s["z"], s["ya"], s["yb"], s["ycp"], w["pool_scale"], name="gate_bwd" + tag)
    dua = _mm_rows(dya, w["w_a_out"], dn=NT, name="a_out_bwd" + tag, out_dtype=ACT_DTYPE)
    dub = _mm_rows(dyb, w["w_b_out"], dn=NT, name="b_out_bwd" + tag, out_dtype=ACT_DTYPE)
    ddd = _pool_proj(dyc, w["w_pool"], dn=NT, name="pool_bwd" + tag, out_dtype=ACT_DTYPE)
    big["w_a_out"] = _mm_tn(s["ua"], dya, name="gw_a_out" + tag)
    big["w_b_out"] = _mm_tn(s["ub"], dyb, name="gw_b_out" + tag)
    big["w_pool"] = _tn_pool(s["dd"], dyc, w["w_pool"].shape[0], name="gw_pool" + tag)
    o3 = rs.chip(early, sb_o, l) if own else None
    dz, redm, dws, dbs = _mix_bwd(dz, dua, dub, ddd, s["z"], w["conv_a"], w["lnv"], w["wm"], w["wmt"],
                                  w["bias_full"], w["mask"], name="mix_bwd" + tag, phases=ph(o3))
    reds = dict(red2=red2, redf=redf, red1=red1, redg=redg, redm=redm, dws=dws, dbs=dbs)
    mid = ("w_o", "w_a_out", "w_b_out", "w_pool")
    o1b = sg1 = None
    if own:
        rs.done(early, o3)
        o1b = rs.pair(mid, big)
        sg1 = _gather_phase1([_small_payload([reds] + list(upper_reds))])
    big["w_in"] = _tn_cols_rhs(s["h"], dz, w["w_in8"].shape[0], name="gw_in" + tag,
                               phases=[o1b, sg1] if own else ())
    pending = None
    if own:
        sb_m = rs.sums(mid, big, o1b)
        o1c, o3b, sg2 = rs.pair(first, big), rs.chip(mid, sb_m, l), _gather_phase2(sg1.results)
        split_tm = 1024
        n_tiles = dz.shape[0] // _row_tile(dz.shape[0], split_tm)
        dh = _nt_cols(dz, w["w_in8"], name="in_bwd" + tag + "_a", phases=[o1c, o3b, sg2], tm=split_tm, tiles=(0, 1))
        rs.done(mid, o3b)
        rs.small_gathered = sg2.results[0]
        o3c = rs.chip(first, rs.sums(first, big, o1c), l)
        if n_tiles > 1:
            dh = _nt_cols(dz, w["w_in8"], name="in_bwd" + tag + "_b", phases=[o3c], tm=split_tm,
                          tiles=(1, n_tiles - 1), into=dh)
            rs.done(first, o3c)
        else:
            pending = (first, o3c)
    elif rs is not None:
        pa, pb = rs.pair(first, big), rs.pair(rest, big)
        dh = _nt_cols(dz, w["w_in8"], name="in_bwd" + tag, phases=[pa, pb])
        rs.presummed = dict(zip(first + rest, rs.sums(first, big, pa) + rs.sums(rest, big, pb)))
    else:
        dh = _nt_cols(dz, w["w_in8"], name="in_bwd" + tag)
    return dxp, dh, big, reds, pending


def _local_step(x, tgt, ws, alpha, plan=None, rs=None):
    depth = len(ws)
    saved = []
    y = x
    for l in range(depth):
        if plan and l > 0:
            ws[l]["w_in8"] = plan.full["w_in8_l%d" % l][0]
        y, s = _layer_fwd(y, ws[l], alpha, "_l%d" % l, plan)
        saved.append(s)
    dpart, loss_blk = _loss_grad(y, tgt, name="loss_grad")
    dh = xmod = mvec = above = pending = None
    bigs, reds = [None] * depth, [None] * depth
    for l in reversed(range(depth)):
        dpart, dh, bigs[l], reds[l], pending = _layer_bwd(dpart, dh, xmod, mvec, ws[l], saved[l], alpha, "_l%d" % l,
                                                          l, above if rs else None, rs, reds[l + 1:])
        xmod, mvec, above = saved[l]["x"], ws[l]["mod1"][0:1], bigs[l]
    grad_x, red0 = _resid_ln_bwd(dpart, dh, xmod, mvec, None, None, None, alpha, name="in_bwd_tail",
                                 phases=[pending[1]] if pending else ())
    if pending:
        rs.done(*pending)
    d_ada = []
    for l in range(depth):
        below = red0 if l == 0 else reds[l - 1]["red2"]
        r1, r2 = reds[l]["red1"], reds[l]["red2"]
        d_ada.append(jnp.stack([below[1], below[0], r1[4], r1[1], r1[0], r2[4]]))
    return loss_blk, grad_x, bigs, reds, jnp.stack(d_ada)


def _small_grads(r):
    redm, redg, redf = r["redm"], r["redg"], r["redf"]
    ng = r["dws"].shape[0]
    return dict(
        b_in=jnp.concatenate([redm[0:6], redg[1:4]], axis=0).reshape(-1),
        conv_a=redm[6:9], ln_v_g=redm[9], ln_v_b=redm[10],
        w_spatial=r["dws"], b_spatial=r["dbs"][:, :ng].T,
        pool_scale=redg[0], ln1_g=r["red1"][2], ln1_b=r["red1"][3],
        b_up=jnp.concatenate([redf[:, 4, :].reshape(-1), redf[:, 5, :].reshape(-1)]),
        conv_ffn=jnp.transpose(redf[:, 0:3, :], (1, 0, 2)).reshape(3, -1), conv_ffn_b=redf[:, 3, :].reshape(-1),
        ln2_g=r["red2"][2], ln2_b=r["red2"][3])


def _small_payload(reds):
    smalls = [_small_grads(r) for r in reds]
    order = SMALL_REPLICATED + SMALL_SHARDED
    flat = jnp.concatenate([smalls[l][n].reshape(-1) for n in order for l in range(len(reds))])
    return _as_rows(flat)[None]


def _layer_weights(l, ada, conv_a, conv_ffn, p):
    sh1, sc1, gt1, sh2, sc2, gt2 = (ada[l, k][None, :] for k in range(6))
    nb = N_DEV
    fs = p["b_up"].shape[1] // nb
    nj = nb // 2
    pos = jnp.arange(GMLP_BLOCK)
    allowed = (pos[None, :] // CHUNK) <= (pos[:, None] // CHUNK)
    wmask = jnp.where(allowed[None], p["w_spatial"][l], 0.0)
    return dict(
        mod1=jnp.concatenate([1.0 + sc1, sh1]), mod2=jnp.concatenate([1.0 + sc2, sh2]),
        ln1=jnp.concatenate([gt1, p["ln1_g"][l][None], p["ln1_b"][l][None]]),
        ln2=jnp.concatenate([gt2, p["ln2_g"][l][None], p["ln2_b"][l][None]]),
        b_in8=p["b_in"][l].reshape(N_DEV, 1, -1), b_up8=p["b_up"][l].reshape(nb, 1, fs),
        conv_a=conv_a[l], lnv=jnp.stack([p["ln_v_g"][l], p["ln_v_b"][l]]),
        wm=wmask.astype(BF), wmt=jnp.transpose(wmask, (0, 2, 1)).astype(BF),
        bias_full=jnp.repeat(p["b_spatial"][l].T, GMLP_BLOCK, axis=1), mask=allowed.astype(F32),
        pool_scale=p["pool_scale"][l][None],
        cw=jnp.transpose(conv_ffn[l].reshape(3, nj, fs), (1, 0, 2)), cb=p["conv_ffn_b"][l].reshape(nj, 1, fs))


ANY = pl.BlockSpec(memory_space=pl.ANY)


def _place():
    x, y, c = lax.axis_index("x"), lax.axis_index("y"), lax.axis_index("c")
    chips = [(1 - x, y), (x, 1 - y), (1 - x, 1 - y)]
    return x, y, c, chips


def _allgather_vmem(xs, *, name):
    r, cdim = xs.shape

    def body(x_ref, out_ref, send_sems, recv_sems, local_sem):
        x, y, c, chips = _place()
        me, sibling = (x, y, c), (x, y, 1 - c)

        def rows(px, py, pc):
            return out_ref.at[pl.ds((4 * px + 2 * py + pc) * r, r), :]

        def copy(k, block, to, src=None):
            return pltpu.make_async_remote_copy(
                src_ref=rows(*block) if src is None else src, dst_ref=rows(*block),
                send_sem=send_sems.at[k], recv_sem=recv_sems.at[k], device_id=to, device_id_type=MESH)

        mine = pltpu.make_async_copy(x_ref, rows(*me), local_sem)
        mine.start()
        first = [copy(0, me, sibling, src=x_ref)]
        first += [copy(1 + j, me, (*chip, c), src=x_ref) for j, chip in enumerate(chips)]
        for cp in first:
            cp.start()
        passed = [copy(4 + j, (*chip, c), sibling) for j, chip in enumerate(chips)]
        for j, chip in enumerate(chips):
            copy(1 + j, (*chip, c), me).wait_recv()
            passed[j].start()
        copy(0, sibling, me).wait_recv()
        for j, chip in enumerate(chips):
            copy(4 + j, (*chip, 1 - c), me).wait_recv()
        for cp in first + passed:
            cp.wait_send()
        mine.wait()

    return pl.pallas_call(
        body, name=name, out_shape=jax.ShapeDtypeStruct((N_DEV * r, cdim), xs.dtype),
        in_specs=[pl.BlockSpec(memory_space=pltpu.VMEM)], out_specs=pl.BlockSpec(memory_space=pltpu.VMEM),
        scratch_shapes=[pltpu.SemaphoreType.DMA((7,)), pltpu.SemaphoreType.DMA((7,)), pltpu.SemaphoreType.DMA],
        compiler_params=_params(),
    )(xs)


def _gather_weights(shards, *, name):
    n = len(shards)

    def body(*refs):
        ins, outs = refs[:n], refs[n:2 * n]
        send_sems, recv_sems, local_sems = refs[2 * n:]
        x, y, c, chips = _place()
        me, sibling = (x, y, c), (x, y, 1 - c)

        def slot(a, px, py, pc):
            return outs[a].at[:, 4 * px + 2 * py + pc]

        def copy(a, k, block, to, src=None):
            return pltpu.make_async_remote_copy(
                src_ref=slot(a, *block) if src is None else src, dst_ref=slot(a, *block),
                send_sem=send_sems.at[7 * a + k], recv_sem=recv_sems.at[7 * a + k], device_id=to,
                device_id_type=MESH)

        mine = [pltpu.make_async_copy(ins[a], slot(a, *me), local_sems.at[a]) for a in range(n)]
        for cp in mine:
            cp.start()
        first = []
        for j, chip in enumerate(chips):
            first += [copy(a, 1 + j, me, (*chip, c), src=ins[a]) for a in range(n)]
        first += [copy(a, 0, me, sibling, src=ins[a]) for a in range(n)]
        for cp in first:
            cp.start()
        passed = []
        for j, chip in enumerate(chips):
            for a in range(n):
                copy(a, 1 + j, (*chip, c), me).wait_recv()
                fwd = copy(a, 4 + j, (*chip, c), sibling)
                fwd.start()
                passed.append(fwd)
        for a in range(n):
            copy(a, 0, sibling, me).wait_recv()
        for j, chip in enumerate(chips):
            for a in range(n):
                copy(a, 4 + j, (*chip, 1 - c), me).wait_recv()
        for cp in first + passed:
            cp.wait_send()
        for cp in mine:
            cp.wait()

    out_shape = [jax.ShapeDtypeStruct((s.shape[0], N_DEV) + s.shape[1:], s.dtype) for s in shards]
    return pl.pallas_call(
        body, name=name, out_shape=out_shape, in_specs=[ANY] * n, out_specs=[ANY] * n,
        scratch_shapes=[pltpu.SemaphoreType.DMA((7 * n,)), pltpu.SemaphoreType.DMA((7 * n,)),
                        pltpu.SemaphoreType.DMA((n,))],
        compiler_params=_params(),
    )(*shards)


def _pick_tile(r, cap):
    best = None
    for t in range(8, min(r, cap) + 1, 8):
        if r % t == 0:
            best = t
    return best if best is not None else r


def _pair_sum(g, r1, cidx, *, name):
    p, _, r, cdim = g.shape
    tr = _pick_tile(r, 256)

    def body(c_ref, g_ref, r_ref, o_ref):
        del c_ref
        o_ref[...] = (g_ref[...].astype(F32) + r_ref[...].astype(F32)).astype(BF)

    grid_spec = pltpu.PrefetchScalarGridSpec(
        num_scalar_prefetch=1, grid=(4, r // tr),
        in_specs=[pl.BlockSpec((p, None, tr, cdim), lambda q, i, c: (0, 2 * q + c[0], i, 0)),
                  pl.BlockSpec((None, p, tr, cdim), lambda q, i, c: (q, 0, i, 0))],
        out_specs=pl.BlockSpec((None, p, tr, cdim), lambda q, i, c: (q, 0, i, 0)))
    return pl.pallas_call(
        body, name=name, grid_spec=grid_spec, out_shape=jax.ShapeDtypeStruct((4, p, r, cdim), BF),
        compiler_params=_params(("arbitrary", "arbitrary")),
    )(cidx, g, r1)


def _ada_fwd(c_all, w_ada, *, name):
    depth, d, ns = w_ada.shape
    nb = c_all.shape[0]

    def body(c_ref, w_ref, o_ref):
        cv = c_ref[...]
        act = cv * jax.nn.sigmoid(cv)
        o_ref[...] = jnp.dot(act, w_ref[...], preferred_element_type=F32, precision=lax.Precision.HIGHEST)

    return pl.pallas_call(
        body, name=name, grid=(depth,),
        in_specs=[pl.BlockSpec((nb, d), lambda l: (0, 0)), pl.BlockSpec((None, d, ns), lambda l: (l, 0, 0))],
        out_specs=pl.BlockSpec((None, nb, ns), lambda l: (l, 0, 0)),
        out_shape=jax.ShapeDtypeStruct((depth, nb, ns), F32), compiler_params=_params(("parallel",)),
    )(c_all, w_ada)


def _ada_bwd(ct, dmine, dall, *, name):
    depth, nb, ns = dmine.shape
    d = ct.shape[0]

    def body(ct_ref, dm_ref, da_ref, gw_ref, gb_ref):
        cv = ct_ref[...]
        act = cv * jax.nn.sigmoid(cv)
        gw_ref[...] = jnp.dot(act, dm_ref[...], preferred_element_type=F32, precision=lax.Precision.HIGHEST)
        s = da_ref[0]
        for b in range(1, nb):
            s = s + da_ref[b]
        gb_ref[...] = s

    return pl.pallas_call(
        body, name=name, grid=(depth,),
        in_specs=[pl.BlockSpec((d, nb), lambda l: (0, 0)), pl.BlockSpec((None, nb, ns), lambda l: (l, 0, 0)),
                  pl.BlockSpec(dall.shape, lambda l: (0, 0, 0))],
        out_specs=[pl.BlockSpec((None, d, ns), lambda l: (l, 0, 0)), pl.BlockSpec(dall.shape[1:], lambda l: (0, 0))],
        out_shape=[jax.ShapeDtypeStruct((depth, d, ns), F32), jax.ShapeDtypeStruct(dall.shape[1:], F32)],
        compiler_params=_params(("arbitrary",)),
    )(ct, dmine, dall)


def _sum_parts(parts, *, name):
    p, r, cdim = parts.shape
    tr = _pick_tile(r, 512)

    def body(p_ref, o_ref):
        s = p_ref[0]
        for k in range(1, p):
            s = s + p_ref[k]
        o_ref[...] = s

    return pl.pallas_call(
        body, name=name, grid=(r // tr,),
        in_specs=[pl.BlockSpec((p, tr, cdim), lambda i: (0, i, 0))], out_specs=pl.BlockSpec((tr, cdim), lambda i: (i, 0)),
        out_shape=jax.ShapeDtypeStruct((r, cdim), F32), compiler_params=_params(("parallel",)),
    )(parts)


def _adamw(parts, w, m, v, *, name):
    p, depth, r, cdim = parts.shape
    tr = _pick_tile(r, 256)

    def body(p_ref, w_ref, m_ref, v_ref, g_out, d_out, m_out, v_out):
        g = p_ref[0].astype(F32)
        for k in range(1, p):
            g = g + p_ref[k].astype(F32)
        m2 = ADAM_B1 * m_ref[...] + (1.0 - ADAM_B1) * g
        v2 = ADAM_B2 * v_ref[...] + (1.0 - ADAM_B2) * (g * g)
        m_hat = m2 / (1.0 - ADAM_B1 ** ADAM_STEP)
        v_hat = v2 / (1.0 - ADAM_B2 ** ADAM_STEP)
        g_out[...] = g
        d_out[...] = -ADAM_LR * (m_hat / (jnp.sqrt(v_hat) + ADAM_EPS) + ADAM_WD * w_ref[...])
        m_out[...] = m2
        v_out[...] = v2

    blk = pl.BlockSpec((None, tr, cdim), lambda l, i: (l, i, 0))
    out = jax.ShapeDtypeStruct((depth, r, cdim), F32)
    return pl.pallas_call(
        body, name=name, grid=(depth, r // tr),
        in_specs=[pl.BlockSpec((p, None, tr, cdim), lambda l, i: (0, l, i, 0)), blk, blk, blk],
        out_specs=[blk, blk, blk, blk], out_shape=[out, out, out, out],
        compiler_params=_params(("parallel", "parallel")),
    )(parts, w, m, v)


BIG = ("w_in", "w_a_out", "w_b_out", "w_pool", "w_o", "w_up", "w_down")
SMALL_REPLICATED = ("b_in", "ln_v_g", "ln_v_b", "w_spatial", "b_spatial", "pool_scale", "ln1_g", "ln1_b", "b_up",
                    "conv_ffn_b", "ln2_g", "ln2_b")
SMALL_SHARDED = ("conv_a", "conv_ffn")
WEIGHTS = ("w_ada", "b_ada", "w_in", "b_in", "conv_a", "w_a_out", "ln_v_g", "ln_v_b", "w_spatial", "b_spatial",
           "w_b_out", "w_pool", "pool_scale", "w_o", "ln1_g", "ln1_b", "w_up", "b_up", "conv_ffn", "conv_ffn_b",
           "w_down", "ln2_g", "ln2_b")
LANES = 128


def _as_rows(flat, mult=8):
    n = flat.shape[0]
    pad = (-n) % (LANES * mult)
    if pad:
        flat = jnp.concatenate([flat, jnp.zeros((pad,), flat.dtype)])
    return flat.reshape(-1, LANES)


def _shard3(a):
    return a.reshape((-1,) + a.shape[-2:])


def kernel(x, c, w_ada, b_ada, w_in, b_in, conv_a, w_a_out, ln_v_g, ln_v_b, w_spatial, b_spatial, w_b_out, w_pool, pool_scale, w_o, ln1_g, ln1_b, w_up, b_up, conv_ffn, conv_ffn_b, w_down, ln2_g, ln2_b, loss_target, m_w_ada, m_b_ada, m_w_in, m_b_in, m_conv_a, m_w_a_out, m_ln_v_g, m_ln_v_b, m_w_spatial, m_b_spatial, m_w_b_out, m_w_pool, m_pool_scale, m_w_o, m_ln1_g, m_ln1_b, m_w_up, m_b_up, m_conv_ffn, m_conv_ffn_b, m_w_down, m_ln2_g, m_ln2_b, v_w_ada, v_b_ada, v_w_in, v_b_in, v_conv_a, v_w_a_out, v_ln_v_g, v_ln_v_b, v_w_spatial, v_b_spatial, v_w_b_out, v_w_pool, v_pool_scale, v_w_o, v_ln1_g, v_ln1_b, v_w_up, v_b_up, v_conv_ffn, v_conv_ffn_b, v_w_down, v_ln2_g, v_ln2_b):
    p = dict(locals())
    depth, d = w_in.shape[0], w_in.shape[1]
    alpha = (2 * depth) ** 0.25
    me = 4 * lax.axis_index("x") + 2 * lax.axis_index("y") + lax.axis_index("c")
    cidx = lax.axis_index("c").astype(jnp.int32).reshape(1)

    n_ca, n_cf = conv_a.size, conv_ffn.size
    packed = _as_rows(jnp.concatenate([c.reshape(-1), conv_a.reshape(-1), conv_ffn.reshape(-1)]))
    got = _allgather_vmem(packed, name="gather_cond").reshape(N_DEV, -1)
    c_all = got[:, :d]
    ct = c_all.T
    conv_a_full = jnp.transpose(got[:, d:d + n_ca].reshape((N_DEV,) + conv_a.shape), (1, 2, 0, 3)).reshape(depth, 3, -1)
    conv_ffn_full = jnp.transpose(got[:, d + n_ca:d + n_ca + n_cf].reshape((N_DEV,) + conv_ffn.shape),
                                  (1, 2, 0, 3)).reshape(depth, 3, -1)

    ns_ada = w_ada.shape[2]
    ada_part = _ada_fwd(c_all, w_ada, name="ada_fwd")
    ada_all = _allgather_vmem(_as_rows(ada_part.reshape(-1)), name="gather_ada")
    ada_all = ada_all.reshape(N_DEV, depth, N_DEV, ns_ada)
    ada_mine = lax.dynamic_index_in_dim(ada_all, me, axis=2, keepdims=False)
    ada = jnp.transpose(ada_mine, (1, 0, 2)).reshape(depth, -1) + b_ada
    ada = ada.reshape(depth, 6, d)

    shards = [{n: _shard3(p[n][l].astype(BF)) for n in BIG} for l in range(depth)]
    ws = [_layer_weights(l, ada, conv_a_full, conv_ffn_full, p) for l in range(depth)]
    ws[0]["w_in8"] = _gather_weights([shards[0]["w_in"]], name="gather_w_in0")[0][0]
    plan = _GatherPlan()
    four = ("w_a_out", "w_b_out", "w_pool", "w_o")
    for l in range(depth):
        t, prev, sh = "_l%d" % l, "_l%d" % (l - 1), shards[l]
        if l == 0:
            for n in four:
                plan.add(n + t, sh[n], "in_proj" + t, "mix_fwd" + t)
            plan.add("w_up8" + t, sh["w_up"], "in_proj" + t, "mix_fwd" + t)
            plan.add("wd4" + t, sh["w_down"], "mix_fwd" + t, "up_proj" + t)
        else:
            half = sh["w_in"].shape[1] // 2
            plan.add("w_in8_top" + t, sh["w_in"], "up_proj" + prev, None, rows=(0, half))
            plan.add("w_in8" + t, sh["w_in"], "ffn_fwd" + prev, "down_proj" + prev, rows=(half, half),
                     onto="w_in8_top" + t)
            for n in four:
                plan.add(n + t, sh[n], "down_proj" + prev, "in_proj" + t)
            plan.add("w_up8" + t, sh["w_up"], "in_proj" + t, "mix_fwd" + t)
            plan.add("wd4" + t, sh["w_down"], "in_proj" + t, "mix_fwd" + t)

    rs = _ReduceScatter(depth, cidx)
    loss_blk, grad_x, bigs, reds, d_ada = _local_step(x[0], loss_target[0], ws, alpha, plan, rs)
    loss = lax.psum(loss_blk[0, 0], ("x", "y", "c"))

    dada_all = _allgather_vmem(_as_rows(d_ada.reshape(-1)), name="gather_dada")
    dada_all = dada_all.reshape(N_DEV, -1, LANES)
    dflat = dada_all.reshape(N_DEV, depth, 6 * d)
    dmine = lax.dynamic_slice_in_dim(dflat, me * ns_ada, ns_ada, axis=2)
    gw_ada, gb_rows = _ada_bwd(ct, jnp.transpose(dmine, (1, 0, 2)), dada_all, name="ada_bwd")
    gb_ada = gb_rows.reshape(-1)[:depth * 6 * d].reshape(depth, 6 * d)

    out = {}
    for n in BIG:
        parts = rs.acc[n]
        view = (lambda a: jnp.swapaxes(a, 1, 2)) if n == "w_up" else (lambda a: a)
        shard_shape = view(p[n]).shape
        w3 = view(p[n]).reshape(depth, -1, shard_shape[-1])
        parts4 = parts.reshape((4,) + w3.shape)
        res = _adamw(parts4, w3, view(p["m_" + n]).reshape(w3.shape), view(p["v_" + n]).reshape(w3.shape),
                     name="adamw_" + n)
        out[n] = [view(r.reshape(shard_shape)) for r in res]
    out["w_ada"] = _adamw(gw_ada[None], w_ada, m_w_ada, v_w_ada, name="adamw_w_ada")

    order = SMALL_REPLICATED + SMALL_SHARDED
    n_rep = sum(p[n].size for n in SMALL_REPLICATED)
    n_pay = n_rep + N_DEV * (conv_a.size + conv_ffn.size)
    gsum = _sum_parts(rs.small_gathered.reshape(N_DEV, -1, LANES), name="sum_small").reshape(-1)[:n_pay]
    ga_full = gsum[n_rep:n_rep + depth * 3 * d].reshape(depth, 3, d)
    gf_full = gsum[n_rep + depth * 3 * d:].reshape(depth, 3, -1)
    ca_w, cf_w = conv_a.shape[2], conv_ffn.shape[2]
    g_ca = lax.dynamic_slice_in_dim(ga_full, me * ca_w, ca_w, axis=2)
    g_cf = lax.dynamic_slice_in_dim(gf_full, me * cf_w, cf_w, axis=2)
    names = ("b_ada",) + order
    gflat = _as_rows(jnp.concatenate([gb_ada.reshape(-1), gsum[:n_rep], g_ca.reshape(-1), g_cf.reshape(-1)]))
    pack = lambda pre: _as_rows(jnp.concatenate([p[pre + n].reshape(-1) for n in names]))
    res = _adamw(gflat[None, None], pack("")[None], pack("m_")[None], pack("v_")[None], name="adamw_small")
    off = 0
    for n in names:
        size = p[n].size
        out[n] = [r.reshape(-1)[off:off + size].reshape(p[n].shape) for r in res]
        off += size

    return (loss, grad_x[None]) + tuple(out[n][k] for k in range(4) for n in WEIGHTS)
```

```python
import functools

import jax
import jax.numpy as jnp
from jax import lax
from jax.experimental import pallas as pl
from jax.experimental.pallas import tpu as pltpu

F32 = jnp.float32
BF = jnp.bfloat16
MESH = pl.DeviceIdType.MESH

LN_EPS = 1e-5
POOL_WINDOWS = (2, 4, 8, 16)
GMLP_BLOCK = 128
CHUNK = 64
HALO = 16
ADAM_LR, ADAM_B1, ADAM_B2, ADAM_EPS, ADAM_WD, ADAM_STEP = 0.001, 0.9, 0.999, 1e-08, 0.01, 10
N_DEV = 8
VMEM_LIMIT = 56 * 1024 * 1024

GRAD_DTYPE = BF
ACT_DTYPE = BF

NN = ((1,), (0,))
NT = ((1,), (1,))
TN = ((0,), (0,))


def _params(sem=None, vmem=VMEM_LIMIT, **kw):
    if sem is not None:
        kw["dimension_semantics"] = sem
    return pltpu.CompilerParams(vmem_limit_bytes=vmem, **kw)


class _Phase:
    def __init__(self, ins, out_shapes, aliases, n_remote, n_local, build):
        self.ins, self.out_shapes, self.aliases = list(ins), list(out_shapes), dict(aliases)
        self.n_remote, self.n_local, self.build = n_remote, n_local, build
        self.results = None


def _pcall(body, args, *, name, grid, in_specs, out_specs, out_shape, scratch_shapes=(), sem=None, aliases=None,
           phases=()):
    aliases = dict(aliases or {})
    if not phases:
        return pl.pallas_call(
            body, name=name, grid=grid, in_specs=list(in_specs), out_specs=out_specs, out_shape=out_shape,
            scratch_shapes=list(scratch_shapes), input_output_aliases=aliases, compiler_params=_params(sem),
        )(*args)
    single = not isinstance(out_shape, (list, tuple))
    o_specs = [out_specs] if single else list(out_specs)
    o_shapes = [out_shape] if single else list(out_shape)
    n_in, n_out, n_scr = len(args), len(o_shapes), len(scratch_shapes)
    ex_args, ex_out, sems = [], [], []
    for ph in phases:
        for src, dst in ph.aliases.items():
            aliases[n_in + len(ex_args) + src] = n_out + len(ex_out) + dst
        ex_args += ph.ins
        ex_out += ph.out_shapes
        sems += [pltpu.SemaphoreType.DMA((max(ph.n_remote, 1),)), pltpu.SemaphoreType.DMA((max(ph.n_remote, 1),)),
                 pltpu.SemaphoreType.DMA((max(ph.n_local, 1),))]

    def wrapped(*refs):
        pos = n_in
        ph_in = []
        for ph in phases:
            ph_in.append(refs[pos:pos + len(ph.ins)])
            pos += len(ph.ins)
        base_out = refs[pos:pos + n_out]
        pos += n_out
        ph_out = []
        for ph in phases:
            ph_out.append(refs[pos:pos + len(ph.out_shapes)])
            pos += len(ph.out_shapes)
        base_scr = refs[pos:pos + n_scr]
        ph_sems = refs[pos + n_scr:]
        first = last = None
        for ax, n in enumerate(grid):
            pid = pl.program_id(ax)
            first = (pid == 0) if first is None else first & (pid == 0)
            last = (pid == n - 1) if last is None else last & (pid == n - 1)

        def ops(k):
            return phases[k].build(ph_in[k], ph_out[k], *ph_sems[3 * k:3 * k + 3])

        @pl.when(first)
        def _():
            for k in range(len(phases)):
                for cp in ops(k)["start"]:
                    cp.start()

        body(*refs[:n_in], *base_out, *base_scr)

        @pl.when(last)
        def _():
            for k in range(len(phases)):
                o = ops(k)
                for cp in o["recv"]:
                    cp.wait_recv()
                for cp in o["send"]:
                    cp.wait_send()
                for cp in o["local"]:
                    cp.wait()

    hbm = pl.BlockSpec(memory_space=pl.ANY)
    res = pl.pallas_call(
        wrapped, name=name, grid=grid, in_specs=list(in_specs) + [hbm] * len(ex_args),
        out_specs=o_specs + [hbm] * len(ex_out), out_shape=o_shapes + ex_out,
        scratch_shapes=list(scratch_shapes) + sems, input_output_aliases=aliases,
        compiler_params=_params(("arbitrary",) * len(grid)),
    )(*args, *ex_args)
    pos = n_out
    for ph in phases:
        ph.results = list(res[pos:pos + len(ph.out_shapes)])
        pos += len(ph.out_shapes)
    return res[0] if single else list(res[:n_out])


def _gelu_parts(x):
    k = 0.7978845608028654
    x2 = x * x
    t = jnp.tanh(k * (x + 0.044715 * (x2 * x)))
    cdf = 0.5 * (1.0 + t)
    dcdf = 0.5 * (1.0 - t * t) * (k * (1.0 + 3.0 * 0.044715 * x2))
    return x * cdf, cdf + x * dcdf


def _gelu(x):
    t = jnp.tanh(0.7978845608028654 * (x + 0.044715 * (x * x * x)))
    return x * (0.5 * (1.0 + t))


def _rowsum(v):
    return jnp.sum(v, axis=0, keepdims=True)


def _ln_stats(r):
    mu = jnp.mean(r, axis=-1, keepdims=True)
    xc = r - mu
    var = jnp.mean(xc * xc, axis=-1, keepdims=True)
    rstd = lax.rsqrt(var + LN_EPS)
    return xc * rstd, rstd


def _ln_bwd(dy, xhat, rstd, gain):
    dxh = dy * gain
    m1 = jnp.mean(dxh, axis=-1, keepdims=True)
    m2 = jnp.mean(dxh * xhat, axis=-1, keepdims=True)
    return rstd * (dxh - m1 - xhat * m2)


def _matmul(a, b, *, dn, grid, a_spec, b_spec, o_spec, out_shape, acc_shape, name, phases=(), into=None):
    nk = grid[2]
    direct = out_shape.dtype == F32

    def body(a_ref, b_ref, *rest):
        o_ref, scratch = (rest[1], rest[2:]) if into is not None else (rest[0], rest[1:])
        prod = lax.dot_general(a_ref[...], b_ref[...], (dn, ((), ())), preferred_element_type=F32)
        if nk == 1:
            o_ref[...] = prod.astype(o_ref.dtype)
            return
        acc = o_ref if direct else scratch[0]
        k = pl.program_id(2)

        @pl.when(k == 0)
        def _():
            acc[...] = prod

        @pl.when(k > 0)
        def _():
            acc[...] += prod

        if not direct:
            @pl.when(k == nk - 1)
            def _():
                o_ref[...] = acc[...].astype(o_ref.dtype)

    scratch = [] if (direct or nk == 1) else [pltpu.VMEM(acc_shape, F32)]
    args, in_specs, aliases = (a, b), [a_spec, b_spec], None
    if into is not None:
        args, in_specs, aliases = (a, b, into), in_specs + [pl.BlockSpec(memory_space=pl.ANY)], {2: 0}
    return _pcall(body, args, name=name, grid=grid, in_specs=in_specs, out_specs=o_spec,
                  out_shape=out_shape, scratch_shapes=scratch, sem=("parallel", "parallel", "arbitrary"),
                  aliases=aliases, phases=phases)


def _row_tile(m, want):
    t = min(m, want)
    assert m % t == 0
    return t


def _mm_rows(a, w, *, dn, name, out_dtype=F32, tm=2048):
    m, k = a.shape
    n = w.shape[1] if dn == NN else w.shape[0]
    tm = _row_tile(m, tm)
    return _matmul(
        a, w, dn=dn, grid=(m // tm, 1, 1), name=name,
        a_spec=pl.BlockSpec((tm, k), lambda i, j, kk: (i, 0)),
        b_spec=pl.BlockSpec(w.shape, lambda i, j, kk: (0, 0)),
        o_spec=pl.BlockSpec((tm, n), lambda i, j, kk: (i, 0)),
        out_shape=jax.ShapeDtypeStruct((m, n), out_dtype), acc_shape=(tm, n))


def _mm_tn(a, b, *, name, tk=2048):
    m, ka = a.shape
    n = b.shape[1]
    tk = _row_tile(m, tk)
    return _matmul(
        a, b, dn=TN, grid=(1, 1, m // tk), name=name,
        a_spec=pl.BlockSpec((tk, ka), lambda i, j, kk: (kk, 0)),
        b_spec=pl.BlockSpec((tk, n), lambda i, j, kk: (kk, 0)),
        o_spec=pl.BlockSpec((ka, n), lambda i, j, kk: (0, 0)),
        out_shape=jax.ShapeDtypeStruct((ka, n), GRAD_DTYPE), acc_shape=(ka, n))


def _mod_matmul(x, mod, w8, bias8, *, flat_out, name, tm=2048, phases=()):
    m, k = x.shape
    nb, _, ns = w8.shape
    tm = _row_tile(m, tm)

    def body(x_ref, mod_ref, w_ref, b_ref, o_ref, h_ref, hs):
        @pl.when(pl.program_id(1) == 0)
        def _():
            h = (x_ref[...] * mod_ref[0:1, :] + mod_ref[1:2, :]).astype(BF)
            hs[...] = h
            h_ref[...] = h

        o_ref[...] = (jnp.dot(hs[...], w_ref[...], preferred_element_type=F32) + b_ref[...]).astype(o_ref.dtype)

    if flat_out:
        o_spec = pl.BlockSpec((tm, ns), lambda i, j: (i, j))
        o_shape = jax.ShapeDtypeStruct((m, nb * ns), ACT_DTYPE)
    else:
        o_spec = pl.BlockSpec((None, tm, ns), lambda i, j: (j, i, 0))
        o_shape = jax.ShapeDtypeStruct((nb, m, ns), ACT_DTYPE)
    return _pcall(
        body, (x, mod, w8, bias8), name=name, grid=(m // tm, nb),
        in_specs=[pl.BlockSpec((tm, k), lambda i, j: (i, 0)),
                  pl.BlockSpec((2, k), lambda i, j: (0, 0)),
                  pl.BlockSpec((None, k, ns), lambda i, j: (j, 0, 0)),
                  pl.BlockSpec((None, 1, ns), lambda i, j: (j, 0, 0))],
        out_specs=[o_spec, pl.BlockSpec((tm, k), lambda i, j: (i, 0))],
        out_shape=[o_shape, jax.ShapeDtypeStruct((m, k), BF)],
        scratch_shapes=[pltpu.VMEM((tm, k), BF)], sem=("parallel", "arbitrary"), phases=phases)


def _seg_spec(tm, d, s):
    return pl.BlockSpec((tm, d), lambda i, s=s: (i, s))


def _prev_halo_spec(tm, d, s):
    hb = tm // HALO
    return pl.BlockSpec((HALO, d), lambda i, s=s: (jnp.maximum(i * hb - 1, 0), s))


def _next_halo_spec(tm, d, s, m):
    hb = tm // HALO
    last = m // HALO - 1
    return pl.BlockSpec((HALO, d), lambda i, s=s: (jnp.minimum((i + 1) * hb, last), s))


def _spatial_mix(wm_ref, src, dst, bias_ref, tm, d):
    for n in range(tm // GMLP_BLOCK):
        for g in range(d // GMLP_BLOCK):
            rs = slice(n * GMLP_BLOCK, (n + 1) * GMLP_BLOCK)
            cs = slice(g * GMLP_BLOCK, (g + 1) * GMLP_BLOCK)
            v = jnp.dot(wm_ref[g], src[rs, cs], preferred_element_type=F32)
            if bias_ref is not None:
                v = v + bias_ref[:, cs]
            dst[rs, cs] = v


def _mix_fwd(z, conv_a, lnv, wm, bias_full, *, name, tm=256, phases=()):
    m, d9 = z.shape
    d = d9 // 9
    tm = _row_tile(m, tm)
    grp = d // len(POOL_WINDOWS)

    rc = 16
    assert tm % rc == 0 and HALO == rc

    def body(zb, zc, zx, zu, zv, zp, zc_h, zx_h, zp_h, ca_ref, lnv_ref, wm_ref, bias_ref,
             ua_ref, ub_ref, d_ref, ext, ext_p, vn_s, mixed_s):
        i = pl.program_id(0)
        first = i == 0
        f32 = lambda r: r[...].astype(F32)
        ext[0:HALO, :] = jnp.where(first, 0.0, f32(zc_h) * f32(zx_h))
        ext[HALO:HALO + tm, :] = f32(zc) * f32(zx)
        ext_p[0:HALO, :] = jnp.where(first, 0.0, f32(zp_h))
        ext_p[HALO:HALO + tm, :] = f32(zp)
        w = ca_ref[...]
        w0, w1, w2 = w[0:1, :], w[1:2, :], w[2:3, :]
        gain, shift = lnv_ref[0:1, :], lnv_ref[1:2, :]

        def step(c, carry):
            r0 = pl.multiple_of(c * rc, rc)
            rows = pl.ds(r0, rc)
            win = ext[pl.ds(r0 + HALO - 8, rc + 8), :]
            conv = w0 * pltpu.roll(win, 2, 0)[8:8 + rc] + w1 * pltpu.roll(win, 1, 0)[8:8 + rc] + w2 * win[8:8 + rc]
            ua_ref[rows, :] = (zb[rows, :].astype(F32) * conv).astype(BF)
            t = (i * tm + r0 + lax.broadcasted_iota(jnp.int32, (rc, 1), 0) + 1).astype(F32)
            for k, width in enumerate(POOL_WINDOWS):
                cs = slice(k * grp, (k + 1) * grp)
                x = ext_p[pl.ds(r0, rc + HALO), cs]
                s, span = x, 1
                while span < width:
                    s = s + pltpu.roll(s, span, 0)
                    span *= 2
                d_ref[rows, cs] = (s[HALO:] / jnp.minimum(t, float(width)) - x[HALO:]).astype(BF)
            vhat, _ = _ln_stats(_gelu(zv[rows, :].astype(F32)))
            vn_s[rows, :] = (vhat * gain + shift).astype(BF)
            return carry

        lax.fori_loop(0, tm // rc, step, 0)
        _spatial_mix(wm_ref, vn_s, mixed_s, bias_ref, tm, d)

        def gate(c, carry):
            rows = pl.ds(pl.multiple_of(c * rc, rc), rc)
            ub_ref[rows, :] = (_gelu(zu[rows, :].astype(F32)) * mixed_s[rows, :]).astype(BF)
            return carry

        lax.fori_loop(0, tm // rc, gate, 0)

    full = lambda a: pl.BlockSpec(a.shape, lambda i: (0,) * a.ndim)
    out = jax.ShapeDtypeStruct((m, d), BF)
    o_spec = pl.BlockSpec((tm, d), lambda i: (i, 0))
    return _pcall(
        body, (z, z, z, z, z, z, z, z, z, conv_a, lnv, wm, bias_full), name=name, grid=(m // tm,),
        in_specs=[_seg_spec(tm, d, s) for s in range(6)] + [_prev_halo_spec(tm, d, s) for s in (1, 2, 5)]
        + [full(conv_a), full(lnv), full(wm), full(bias_full)],
        out_specs=[o_spec, o_spec, o_spec], out_shape=[out, out, out],
        scratch_shapes=[pltpu.VMEM((HALO + tm, d), F32), pltpu.VMEM((HALO + tm, d), F32), pltpu.VMEM((tm, d), BF),
                        pltpu.VMEM((tm, d), F32)],
        sem=("arbitrary",), phases=phases)


def _pool_proj(dd, w_pool, *, dn, name, out_dtype=F32, tm=512):
    m, d = dd.shape
    ng, grp, _ = w_pool.shape
    tm = _row_tile(m, tm)
    return _matmul(
        dd, w_pool, dn=dn, grid=(m // tm, ng, 1), name=name,
        a_spec=pl.BlockSpec((tm, grp), lambda i, j, kk: (i, j)),
        b_spec=pl.BlockSpec((None, grp, grp), lambda i, j, kk: (j, 0, 0)),
        o_spec=pl.BlockSpec((tm, grp), lambda i, j, kk: (i, j)),
        out_shape=jax.ShapeDtypeStruct((m, d), out_dtype), acc_shape=(tm, grp))


def _merge(z, ya, yb, ycp, scale, *, name, tm=512):
    m, d = ya.shape
    tm = _row_tile(m, tm)

    def body(ga, gb, gc, ya_ref, yb_ref, yc_ref, sc_ref, o_ref):
        f32 = lambda r: r[...].astype(F32)
        o_ref[...] = (jax.nn.sigmoid(f32(ga)) * f32(ya_ref) + jax.nn.sigmoid(f32(gb)) * f32(yb_ref)
                      + jax.nn.sigmoid(f32(gc)) * (f32(yc_ref) * sc_ref[...])).astype(BF)

    row = pl.BlockSpec((tm, d), lambda i: (i, 0))
    return pl.pallas_call(
        body, name=name, grid=(m // tm,),
        in_specs=[_seg_spec(tm, d, 6), _seg_spec(tm, d, 7), _seg_spec(tm, d, 8), row, row, row,
                  pl.BlockSpec((1, d), lambda i: (0, 0))],
        out_specs=row, out_shape=jax.ShapeDtypeStruct((m, d), BF),
        compiler_params=_params(("parallel",)),
    )(z, z, z, ya, yb, ycp, scale)


def _resid_ln(xp, ys, vec, alpha, *, name, tm=512):
    m, d = xp.shape
    tm = _row_tile(m, tm)

    def body(xp_ref, ys_ref, v_ref, o_ref):
        xhat, _ = _ln_stats(alpha * xp_ref[...] + v_ref[0:1, :] * ys_ref[...])
        o_ref[...] = xhat * v_ref[1:2, :] + v_ref[2:3, :]

    row = pl.BlockSpec((tm, d), lambda i: (i, 0))
    return pl.pallas_call(
        body, name=name, grid=(m // tm,),
        in_specs=[row, row, pl.BlockSpec(vec.shape, lambda i: (0, 0))],
        out_specs=row, out_shape=jax.ShapeDtypeStruct((m, d), F32),
        compiler_params=_params(("parallel",)),
    )(xp, ys, vec)


def _ffn_fwd(up4, cw, cb, *, name, tm=512, phases=()):
    _, nj, m, fs = up4.shape
    tm = _row_tile(m, tm)
    hb = tm // HALO

    def body(up_ref, ah_ref, cw_ref, cb_ref, f_ref, ext):
        first = pl.program_id(1) == 0
        ext[0:HALO, :] = jnp.where(first, 0.0, ah_ref[...].astype(F32))
        ext[HALO:HALO + tm, :] = up_ref[0].astype(F32)
        w = cw_ref[...]
        w0, w1, w2, bias = w[0:1, :], w[1:2, :], w[2:3, :], cb_ref[...]
        rc = 16

        def step(c, carry):
            r0 = pl.multiple_of(c * rc, rc)
            win = ext[pl.ds(r0 + HALO - 8, rc + 8), :]
            a0, a1, a2 = win[8:8 + rc], pltpu.roll(win, 1, 0)[8:8 + rc], pltpu.roll(win, 2, 0)[8:8 + rc]
            ca = w0 * a2 + w1 * a1 + w2 * a0 + bias
            f_ref[pl.ds(r0, rc), :] = (_gelu(ca) * up_ref[1, pl.ds(r0, rc), :].astype(F32)).astype(BF)
            return carry

        lax.fori_loop(0, tm // rc, step, 0)

    return _pcall(
        body, (up4, up4, cw, cb), name=name, grid=(nj, m // tm),
        in_specs=[pl.BlockSpec((2, None, tm, fs), lambda j, i: (0, j, i, 0)),
                  pl.BlockSpec((None, None, HALO, fs), lambda j, i: (0, j, jnp.maximum(i * hb - 1, 0), 0)),
                  pl.BlockSpec((None, 3, fs), lambda j, i: (j, 0, 0)),
                  pl.BlockSpec((None, 1, fs), lambda j, i: (j, 0, 0))],
        out_specs=pl.BlockSpec((None, tm, fs), lambda j, i: (j, i, 0)),
        out_shape=jax.ShapeDtypeStruct((nj, m, fs), BF),
        scratch_shapes=[pltpu.VMEM((HALO + tm, fs), F32)], sem=("parallel", "arbitrary"), phases=phases)


def _down_proj(f4, wd4, *, name, tm=2048, phases=()):
    nj, m, fs = f4.shape
    d = wd4.shape[2]
    tm = _row_tile(m, tm)
    return _matmul(
        f4, wd4, dn=NN, grid=(m // tm, 1, nj), name=name,
        a_spec=pl.BlockSpec((None, tm, fs), lambda i, j, kk: (kk, i, 0)),
        b_spec=pl.BlockSpec((None, fs, d), lambda i, j, kk: (kk, 0, 0)),
        o_spec=pl.BlockSpec((tm, d), lambda i, j, kk: (i, 0)),
        out_shape=jax.ShapeDtypeStruct((m, d), F32), acc_shape=(tm, d), phases=phases)


def _loss_grad(y, tgt, *, name, tm=512):
    m, d = y.shape
    tm = _row_tile(m, tm)
    ni = m // tm

    def body(y_ref, t_ref, dy_ref, l_ref, acc):
        i = pl.program_id(0)
        e = y_ref[...] - t_ref[...]
        dy_ref[...] = e * (1.0 / d)
        part = jnp.sum((e * e).reshape(tm // 8, 8, d), axis=0)

        @pl.when(i == 0)
        def _():
            acc[...] = part

        @pl.when(i > 0)
        def _():
            acc[...] += part

        @pl.when(i == ni - 1)
        def _():
            l_ref[...] = jnp.full((8, 128), 0.5 / d, F32) * jnp.sum(acc[...])

    row = pl.BlockSpec((tm, d), lambda i: (i, 0))
    return pl.pallas_call(
        body, name=name, grid=(ni,), in_specs=[row, row],
        out_specs=[row, pl.BlockSpec((8, 128), lambda i: (0, 0))],
        out_shape=[jax.ShapeDtypeStruct((m, d), F32), jax.ShapeDtypeStruct((8, 128), F32)],
        scratch_shapes=[pltpu.VMEM((8, d), F32)],
        compiler_params=_params(("arbitrary",)),
    )(y, tgt)


def _resid_ln_bwd(dpart, dh, xmod, mvec, xp, ys, vec, alpha, *, name, tm=256, phases=()):
    m, d = dpart.shape
    tm = _row_tile(m, tm)
    has_dh = dh is not None
    has_ln = xp is not None

    def body(*refs):
        refs = list(refs)
        dpart_ref = refs.pop(0)
        if has_dh:
            dh_ref, xm_ref, mv_ref = refs.pop(0), refs.pop(0), refs.pop(0)
        if has_ln:
            xp_ref, ys_ref, v_ref = refs.pop(0), refs.pop(0), refs.pop(0)
            dys_ref, dxp_ref, red_ref = refs
        else:
            dx_ref, red_ref = refs
        i = pl.program_id(0)
        dtot = dpart_ref[...]
        rows = [jnp.zeros((1, d), F32)] * 5
        if has_dh:
            dhv = dh_ref[...]
            dtot = dtot + dhv * mv_ref[...]
            rows[0] = _rowsum(dhv * xm_ref[...])
            rows[1] = _rowsum(dhv)
        if has_ln:
            ys = ys_ref[...]
            gt = v_ref[0:1, :]
            xhat, rstd = _ln_stats(alpha * xp_ref[...] + gt * ys)
            rows[2] = _rowsum(dtot * xhat)
            rows[3] = _rowsum(dtot)
            dr = _ln_bwd(dtot, xhat, rstd, v_ref[1:2, :])
            rows[4] = _rowsum(dr * ys)
            dys_ref[...] = (dr * gt).astype(BF)
            dxp_ref[...] = alpha * dr
        else:
            dx_ref[...] = dtot
        red = jnp.concatenate(rows + [jnp.zeros((3, d), F32)], axis=0)

        @pl.when(i == 0)
        def _():
            red_ref[...] = red

        @pl.when(i > 0)
        def _():
            red_ref[...] += red

    row = pl.BlockSpec((tm, d), lambda i: (i, 0))
    vrow = lambda a: pl.BlockSpec(a.shape, lambda i: (0, 0))
    args, specs = [dpart], [row]
    if has_dh:
        args += [dh, xmod, mvec]
        specs += [row, row, vrow(mvec)]
    if has_ln:
        args += [xp, ys, vec]
        specs += [row, row, vrow(vec)]
        out_specs = [row, row, pl.BlockSpec((8, d), lambda i: (0, 0))]
        out_shape = [jax.ShapeDtypeStruct((m, d), BF), jax.ShapeDtypeStruct((m, d), F32),
                     jax.ShapeDtypeStruct((8, d), F32)]
    else:
        out_specs = [row, pl.BlockSpec((8, d), lambda i: (0, 0))]
        out_shape = [jax.ShapeDtypeStruct((m, d), F32), jax.ShapeDtypeStruct((8, d), F32)]
    return _pcall(body, args, name=name, grid=(m // tm,), in_specs=specs, out_specs=out_specs, out_shape=out_shape,
                  sem=("arbitrary",), phases=phases)


def _down_bwd(dy, wd4, *, name, tm=2048, phases=()):
    m, d = dy.shape
    nj, fs, _ = wd4.shape
    tm = _row_tile(m, tm)
    return _matmul(
        dy, wd4, dn=NT, grid=(m // tm, nj, 1), name=name,
        a_spec=pl.BlockSpec((tm, d), lambda i, j, kk: (i, 0)),
        b_spec=pl.BlockSpec((None, fs, d), lambda i, j, kk: (j, 0, 0)),
        o_spec=pl.BlockSpec((None, tm, fs), lambda i, j, kk: (j, i, 0)),
        out_shape=jax.ShapeDtypeStruct((nj, m, fs), ACT_DTYPE), acc_shape=(tm, fs), phases=phases)


def _tn_shards_lhs(f4, dy, *, name, tk=2048, phases=()):
    nj, m, fs = f4.shape
    d = dy.shape[1]
    tk = _row_tile(m, tk)
    return _matmul(
        f4, dy, dn=TN, grid=(nj, 1, m // tk), name=name,
        a_spec=pl.BlockSpec((None, tk, fs), lambda i, j, kk: (i, kk, 0)),
        b_spec=pl.BlockSpec((tk, d), lambda i, j, kk: (kk, 0)),
        o_spec=pl.BlockSpec((None, fs, d), lambda i, j, kk: (i, 0, 0)),
        out_shape=jax.ShapeDtypeStruct((nj, fs, d), GRAD_DTYPE), acc_shape=(fs, d), phases=phases)


def _tn_shards_rhs(h, d8, *, name, tk=2048, phases=()):
    m, k = h.shape
    nb, _, ns = d8.shape
    tk = _row_tile(m, tk)
    return _matmul(
        h, d8, dn=TN, grid=(nb, 1, m // tk), name=name,
        a_spec=pl.BlockSpec((tk, k), lambda i, j, kk: (kk, 0)),
        b_spec=pl.BlockSpec((None, tk, ns), lambda i, j, kk: (i, kk, 0)),
        o_spec=pl.BlockSpec((None, k, ns), lambda i, j, kk: (i, 0, 0)),
        out_shape=jax.ShapeDtypeStruct((nb, k, ns), GRAD_DTYPE), acc_shape=(k, ns), phases=phases)


def _tn_cols_rhs(h, dz, nb, *, name, tk=2048, phases=()):
    m, k = h.shape
    ns = dz.shape[1] // nb
    tk = _row_tile(m, tk)
    return _matmul(
        h, dz, dn=TN, grid=(nb, 1, m // tk), name=name,
        a_spec=pl.BlockSpec((tk, k), lambda i, j, kk: (kk, 0)),
        b_spec=pl.BlockSpec((tk, ns), lambda i, j, kk: (kk, i)),
        o_spec=pl.BlockSpec((None, k, ns), lambda i, j, kk: (i, 0, 0)),
        out_shape=jax.ShapeDtypeStruct((nb, k, ns), GRAD_DTYPE), acc_shape=(k, ns), phases=phases)


def _nt_shards(d8, w8, *, name, tm=2048, phases=()):
    nb, m, ns = d8.shape
    k = w8.shape[1]
    tm = _row_tile(m, tm)
    return _matmul(
        d8, w8, dn=NT, grid=(m // tm, 1, nb), name=name,
        a_spec=pl.BlockSpec((None, tm, ns), lambda i, j, kk: (kk, i, 0)),
        b_spec=pl.BlockSpec((None, k, ns), lambda i, j, kk: (kk, 0, 0)),
        o_spec=pl.BlockSpec((tm, k), lambda i, j, kk: (i, 0)),
        out_shape=jax.ShapeDtypeStruct((m, k), F32), acc_shape=(tm, k), phases=phases)


def _nt_cols(dz, w8, *, name, tm=2048, phases=(), tiles=None, into=None):
    m = dz.shape[0]
    nb, k, ns = w8.shape
    tm = _row_tile(m, tm)
    first, count = tiles if tiles is not None else (0, m // tm)
    return _matmul(
        dz, w8, dn=NT, grid=(count, 1, nb), name=name,
        a_spec=pl.BlockSpec((tm, ns), lambda i, j, kk: (i + first, kk)),
        b_spec=pl.BlockSpec((None, k, ns), lambda i, j, kk: (kk, 0, 0)),
        o_spec=pl.BlockSpec((tm, k), lambda i, j, kk: (i + first, 0)),
        out_shape=jax.ShapeDtypeStruct((m, k), F32), acc_shape=(tm, k), phases=phases, into=into)


def _tn_pool(dd, dyc, ng, *, name, tk=2048):
    m, d = dd.shape
    grp = d // ng
    tk = _row_tile(m, tk)
    return _matmul(
        dd, dyc, dn=TN, grid=(ng, 1, m // tk), name=name,
        a_spec=pl.BlockSpec((tk, grp), lambda i, j, kk: (kk, i)),
        b_spec=pl.BlockSpec((tk, grp), lambda i, j, kk: (kk, i)),
        o_spec=pl.BlockSpec((None, grp, grp), lambda i, j, kk: (i, 0, 0)),
        out_shape=jax.ShapeDtypeStruct((ng, grp, grp), GRAD_DTYPE), acc_shape=(grp, grp))


def _ffn_bwd(up4, df4, cw, cb, *, name, tm=256, phases=()):
    _, nj, m, fs = up4.shape
    tm = _row_tile(m, tm)
    hb = tm // HALO
    ni = m // tm
    last_hb = m // HALO - 1
    ext_rows = tm + 8

    rc = 16
    assert tm % rc == 0

    def body(up_ref, ap_ref, un_ref, df_ref, dfn_ref, cw_ref, cb_ref, dup_ref, red_ref, ext, dca_s, racc):
        i = pl.program_id(1)
        ext[0:HALO, :] = jnp.where(i == 0, 0.0, ap_ref[...].astype(F32))
        ext[HALO:HALO + tm, :] = up_ref[0].astype(F32)
        ext[HALO + tm:2 * HALO + tm, :] = un_ref[0].astype(F32)
        racc[...] = jnp.zeros_like(racc)
        w = cw_ref[...]
        w0, w1, w2, bias = w[0:1, :], w[1:2, :], w[2:3, :], cb_ref[...]

        def conv_taps(win, n):
            return (win[8:8 + n], pltpu.roll(win, 1, 0)[8:8 + n], pltpu.roll(win, 2, 0)[8:8 + n])

        def fold(v):
            return v[0:8] + v[8:16]

        def add_red(k, v8):
            racc[8 * k:8 * k + 8, :] += v8

        def first_pass(c, carry):
            r0 = pl.multiple_of(c * rc, rc)
            a0, a1, a2 = conv_taps(ext[pl.ds(r0 + HALO - 8, rc + 8), :], rc)
            act, dact = _gelu_parts(w0 * a2 + w1 * a1 + w2 * a0 + bias)
            dfc = df_ref[pl.ds(r0, rc), :].astype(F32)
            dca = dfc * up_ref[1, pl.ds(r0, rc), :].astype(F32) * dact
            dup_g = dfc * act
            dca_s[pl.ds(r0, rc), :] = dca
            dup_ref[1, pl.ds(r0, rc), :] = dup_g.astype(BF)
            for k, v in enumerate((dca * a2, dca * a1, dca * a0, dca, dup_g)):
                add_red(k if k < 4 else 5, fold(v))
            return carry

        lax.fori_loop(0, tm // rc, first_pass, 0)
        a0, a1, a2 = conv_taps(ext[HALO + tm - 8:HALO + tm + 8, :], 8)
        _, dact = _gelu_parts(w0 * a2 + w1 * a1 + w2 * a0 + bias)
        after = dfn_ref[...].astype(F32)[0:8, :] * un_ref[1].astype(F32)[0:8, :] * dact
        dca_s[tm:tm + 8, :] = jnp.where(i < ni - 1, after, 0.0)
        dca_s[tm + 8:tm + 16, :] = jnp.zeros((8, fs), F32)

        def second_pass(c, carry):
            r0 = pl.multiple_of(c * rc, rc)
            win = dca_s[pl.ds(r0, rc + 8), :]
            up1, up2 = pltpu.roll(win, rc + 7, 0)[0:rc], pltpu.roll(win, rc + 6, 0)[0:rc]
            dup_a = w2 * win[0:rc] + w1 * up1 + w0 * up2
            dup_ref[0, pl.ds(r0, rc), :] = dup_a.astype(BF)
            add_red(4, fold(dup_a))
            return carry

        lax.fori_loop(0, tm // rc, second_pass, 0)
        red = jnp.concatenate([_rowsum(racc[8 * k:8 * k + 8, :]) for k in range(6)] + [jnp.zeros((2, fs), F32)],
                              axis=0)

        @pl.when(i == 0)
        def _():
            red_ref[...] = red

        @pl.when(i > 0)
        def _():
            red_ref[...] += red

    nxt = lambda j, i: jnp.minimum((i + 1) * hb, last_hb)
    return _pcall(
        body, (up4, up4, up4, df4, df4, cw, cb), name=name, grid=(nj, ni), sem=("parallel", "arbitrary"), phases=phases,
        in_specs=[pl.BlockSpec((2, None, tm, fs), lambda j, i: (0, j, i, 0)),
                  pl.BlockSpec((None, None, HALO, fs), lambda j, i: (0, j, jnp.maximum(i * hb - 1, 0), 0)),
                  pl.BlockSpec((2, None, HALO, fs), lambda j, i: (0, j, nxt(j, i), 0)),
                  pl.BlockSpec((None, tm, fs), lambda j, i: (j, i, 0)),
                  pl.BlockSpec((None, HALO, fs), lambda j, i: (j, nxt(j, i), 0)),
                  pl.BlockSpec((None, 3, fs), lambda j, i: (j, 0, 0)),
                  pl.BlockSpec((None, 1, fs), lambda j, i: (j, 0, 0))],
        out_specs=[pl.BlockSpec((2, None, tm, fs), lambda j, i: (0, j, i, 0)),
                   pl.BlockSpec((None, 8, fs), lambda j, i: (j, 0, 0))],
        out_shape=[jax.ShapeDtypeStruct((2, nj, m, fs), BF), jax.ShapeDtypeStruct((nj, 8, fs), F32)],
        scratch_shapes=[pltpu.VMEM((2 * HALO + tm, fs), F32), pltpu.VMEM((tm + 16, fs), F32),
                        pltpu.VMEM((48, fs), F32)])


def _gate_bwd(dm, z, ya, yb, ycp, scale, *, name, tm=256):
    m, d = dm.shape
    tm = _row_tile(m, tm)

    def body(dm_ref, ga, gb, gc, ya_ref, yb_ref, yc_ref, sc_ref, dya_ref, dyb_ref, dyc_ref, dz_ref, red_ref):
        i = pl.program_id(0)
        f32 = lambda r: r[...].astype(F32)
        dmv = f32(dm_ref)
        sa, sb, sc = jax.nn.sigmoid(f32(ga)), jax.nn.sigmoid(f32(gb)), jax.nn.sigmoid(f32(gc))
        scale_v = sc_ref[...]
        ycp_v = f32(yc_ref)
        dya_ref[...] = (dmv * sa).astype(BF)
        dyb_ref[...] = (dmv * sb).astype(BF)
        dyc = dmv * sc
        dyc_ref[...] = (dyc * scale_v).astype(BF)
        dga = dmv * f32(ya_ref) * (sa * (1.0 - sa))
        dgb = dmv * f32(yb_ref) * (sb * (1.0 - sb))
        dgc = dmv * (ycp_v * scale_v) * (sc * (1.0 - sc))
        dz_ref[:, 0:d] = dga.astype(BF)
        dz_ref[:, d:2 * d] = dgb.astype(BF)
        dz_ref[:, 2 * d:3 * d] = dgc.astype(BF)
        red = jnp.concatenate([_rowsum(dyc * ycp_v), _rowsum(dga), _rowsum(dgb), _rowsum(dgc),
                               jnp.zeros((4, d), F32)], axis=0)

        @pl.when(i == 0)
        def _():
            red_ref[...] = red

        @pl.when(i > 0)
        def _():
            red_ref[...] += red

    row = pl.BlockSpec((tm, d), lambda i: (i, 0))
    obf = jax.ShapeDtypeStruct((m, d), BF)
    return pl.pallas_call(
        body, name=name, grid=(m // tm,),
        in_specs=[row, _seg_spec(tm, d, 6), _seg_spec(tm, d, 7), _seg_spec(tm, d, 8), row, row, row,
                  pl.BlockSpec((1, d), lambda i: (0, 0))],
        out_specs=[row, row, row, pl.BlockSpec((tm, 3 * d), lambda i: (i, 2)), pl.BlockSpec((8, d), lambda i: (0, 0))],
        out_shape=[obf, obf, obf, jax.ShapeDtypeStruct((m, 9 * d), BF), jax.ShapeDtypeStruct((8, d), F32)],
        compiler_params=_params(("arbitrary",)),
    )(dm, z, z, z, ya, yb, ycp, scale)


def _mix_bwd(dz, dua, dub, ddd, z, conv_a, lnv, wm, wmt, bias_full, mask, *, name, tm=128, phases=()):
    m, d = dua.shape
    tm = _row_tile(m, tm)
    ni = m // tm
    grp = d // len(POOL_WINDOWS)
    ng = d // GMLP_BLOCK
    ext_rows = tm + 8

    def body(dz_in, dua_ref, dub_ref, dd_ref, zb, zc, zx, zu, zv, zp, zc_h, zx_h, dua_n, zb_n, dd_n,
             ca_ref, lnv_ref, wm_ref, wmt_ref, bias_ref, mask_ref,
             dz_ref, red_ref, dws_ref, dbs_ref, ext, sh_s, vn_s, mixed_s, dmx_s, dvn_s, dbs_acc):
        del dz_in
        i = pl.program_id(0)
        rows = []
        f32 = lambda r: r[...].astype(F32)
        zbv, zcv, zxv = f32(zb), f32(zc), f32(zx)
        pa = zcv * zxv
        ext[0:HALO, :] = jnp.where(i == 0, 0.0, f32(zc_h) * f32(zx_h))
        ext[HALO:HALO + tm, :] = pa
        w = ca_ref[...]
        w0, w1, w2 = w[0:1, :], w[1:2, :], w[2:3, :]
        p1 = ext[pl.ds(HALO - 1, tm), :]
        p2 = ext[pl.ds(HALO - 2, tm), :]
        conv = w0 * p2 + w1 * p1 + w2 * pa
        duav = f32(dua_ref)
        dzb = duav * conv
        dca = duav * zbv
        dca_n = jnp.where(i < ni - 1, f32(dua_n)[0:8, :] * f32(zb_n)[0:8, :], 0.0)
        sh_s[0:tm, :] = dca
        sh_s[tm:tm + 8, :] = dca_n
        dpa = w2 * dca + w1 * sh_s[pl.ds(1, tm), :] + w0 * sh_s[pl.ds(2, tm), :]
        dzc = dpa * zxv
        dzx = dpa * zcv
        dz_ref[:, 0:d] = dzb.astype(BF)
        dz_ref[:, d:2 * d] = dzc.astype(BF)
        dz_ref[:, 2 * d:3 * d] = dzx.astype(BF)
        rows += [_rowsum(dzb), _rowsum(dzc), _rowsum(dzx)]
        dconv = [_rowsum(dca * p2), _rowsum(dca * p1), _rowsum(dca * pa)]
        zuv, zvv = f32(zu), f32(zv)
        gu, dgu_dz = _gelu_parts(zuv)
        gv, dgv_dz = _gelu_parts(zvv)
        vhat, rstd = _ln_stats(gv)
        gain = lnv_ref[0:1, :]
        vn_s[...] = (vhat * gain + lnv_ref[1:2, :]).astype(BF)
        _spatial_mix(wm_ref, vn_s, mixed_s, bias_ref, tm, d)
        dubv = f32(dub_ref)
        dzu = dubv * mixed_s[...] * dgu_dz
        dmixed = dubv * gu
        dmx_s[...] = dmixed.astype(BF)
        _spatial_mix(wmt_ref, dmx_s, dvn_s, None, tm, d)
        dvn = dvn_s[...]
        dzv = _ln_bwd(dvn, vhat, rstd, gain) * dgv_dz
        dz_ref[:, 3 * d:4 * d] = dzu.astype(BF)
        dz_ref[:, 4 * d:5 * d] = dzv.astype(BF)
        rows += [_rowsum(dzu), _rowsum(dzv)]
        dlnv = [_rowsum(dvn * vhat), _rowsum(dvn)]
        dbs_part = dmixed[0:GMLP_BLOCK, :]
        for n in range(1, tm // GMLP_BLOCK):
            dbs_part = dbs_part + dmixed[n * GMLP_BLOCK:(n + 1) * GMLP_BLOCK, :]
        ddv = f32(dd_ref)
        t = (i * tm + lax.broadcasted_iota(jnp.int32, (ext_rows + 8, 1), 0) + 1).astype(F32)
        dde = jnp.concatenate([ddv, jnp.where(i < ni - 1, f32(dd_n), 0.0)], axis=0)
        for k, win in enumerate(POOL_WINDOWS):
            cs = slice(k * grp, (k + 1) * grp)
            ext[0:tm + HALO, cs] = dde[:, cs] / jnp.minimum(t, float(win))
        dzp_parts = []
        for k, win in enumerate(POOL_WINDOWS):
            cs = slice(k * grp, (k + 1) * grp)
            s = ext[0:tm, cs]
            for j in range(1, win):
                s = s + ext[pl.ds(j, tm), cs]
            dzp_parts.append(s - ddv[:, cs])
        dzp = jnp.concatenate(dzp_parts, axis=1)
        dz_ref[:, 5 * d:6 * d] = dzp.astype(BF)
        rows += [_rowsum(dzp)]
        red = jnp.concatenate(rows + dconv + dlnv + [jnp.zeros((5, d), F32)], axis=0)

        @pl.when(i == 0)
        def _():
            red_ref[...] = red
            dbs_acc[...] = dbs_part
            dws_ref[...] = jnp.zeros_like(dws_ref)

        @pl.when(i > 0)
        def _():
            red_ref[...] += red
            dbs_acc[...] += dbs_part

        for n in range(tm // GMLP_BLOCK):
            for g in range(ng):
                rs = slice(n * GMLP_BLOCK, (n + 1) * GMLP_BLOCK)
                cs = slice(g * GMLP_BLOCK, (g + 1) * GMLP_BLOCK)
                dws_ref[g] += mask_ref[...] * lax.dot_general(
                    dmx_s[rs, cs], vn_s[rs, cs], (NT, ((), ())), preferred_element_type=F32)

        @pl.when(i == ni - 1)
        def _():
            lane = lax.broadcasted_iota(jnp.int32, (GMLP_BLOCK, GMLP_BLOCK), 1)
            out = jnp.zeros((GMLP_BLOCK, GMLP_BLOCK), F32)
            for g in range(ng):
                sg = jnp.sum(dbs_acc[:, g * GMLP_BLOCK:(g + 1) * GMLP_BLOCK], axis=1, keepdims=True)
                out = out + jnp.where(lane == g, sg, 0.0)
            dbs_ref[...] = out

    row = pl.BlockSpec((tm, d), lambda i: (i, 0))
    full = lambda a: pl.BlockSpec(a.shape, lambda i: (0,) * a.ndim)
    hb = tm // HALO
    last_hb = m // HALO - 1
    nrow = pl.BlockSpec((HALO, d), lambda i: (jnp.minimum((i + 1) * hb, last_hb), 0))
    return _pcall(
        body, (dz, dua, dub, ddd, z, z, z, z, z, z, z, z, dua, z, ddd, conv_a, lnv, wm, wmt, bias_full, mask),
        name=name, grid=(ni,), sem=("arbitrary",), aliases={0: 0}, phases=phases,
        in_specs=[pl.BlockSpec(memory_space=pl.ANY), row, row, row]
        + [_seg_spec(tm, d, s) for s in range(6)]
        + [_prev_halo_spec(tm, d, 1), _prev_halo_spec(tm, d, 2), nrow, _next_halo_spec(tm, d, 0, m), nrow]
        + [full(conv_a), full(lnv), full(wm), full(wmt), full(bias_full), full(mask)],
        out_specs=[pl.BlockSpec((tm, 6 * d), lambda i: (i, 0)), pl.BlockSpec((16, d), lambda i: (0, 0)),
                   full(wm), pl.BlockSpec((GMLP_BLOCK, GMLP_BLOCK), lambda i: (0, 0))],
        out_shape=[jax.ShapeDtypeStruct(dz.shape, BF), jax.ShapeDtypeStruct((16, d), F32),
                   jax.ShapeDtypeStruct(wm.shape, F32), jax.ShapeDtypeStruct((GMLP_BLOCK, GMLP_BLOCK), F32)],
        scratch_shapes=[pltpu.VMEM((2 * HALO + tm, d), F32), pltpu.VMEM((tm + 8, d), F32),
                        pltpu.VMEM((tm, d), BF), pltpu.VMEM((tm, d), F32), pltpu.VMEM((tm, d), BF),
                        pltpu.VMEM((tm, d), F32), pltpu.VMEM((GMLP_BLOCK, d), F32)])


REST = ("w_a_out", "w_b_out", "w_pool", "w_o", "w_up", "w_down")


def _remote(src, dst, ssem, rsem, k, to):
    return pltpu.make_async_remote_copy(src_ref=src, dst_ref=dst, send_sem=ssem.at[k], recv_sem=rsem.at[k],
                                        device_id=to, device_id_type=MESH)


def _gather_phase1(shards, rows=None, onto=None):
    n = len(shards)
    rows = rows or [None] * n
    onto = onto or [None] * n
    extra = [a for a in range(n) if onto[a] is not None]

    def build(ins, outs, ssem, rsem, lsem):
        x, y, c, chips = _place()
        me = 4 * x + 2 * y + c

        def src(a):
            return ins[a] if rows[a] is None else ins[a].at[:, pl.ds(*rows[a])]

        def dst(a, dev):
            return outs[a].at[:, dev] if rows[a] is None else outs[a].at[:, dev, pl.ds(*rows[a])]

        local = [pltpu.make_async_copy(src(a), dst(a, me), lsem.at[a]) for a in range(n)]
        sends, recvs = [], []
        for j, (cx, cy) in enumerate(chips):
            for a in range(n):
                sends.append(_remote(src(a), dst(a, me), ssem, rsem, 4 * a + 1 + j, (cx, cy, c)))
                recvs.append(_remote(src(a), dst(a, 4 * cx + 2 * cy + c), ssem, rsem, 4 * a + 1 + j, (cx, cy, c)))
        for a in range(n):
            sends.append(_remote(src(a), dst(a, me), ssem, rsem, 4 * a, (x, y, 1 - c)))
            recvs.append(_remote(src(a), dst(a, 4 * x + 2 * y + 1 - c), ssem, rsem, 4 * a, (x, y, 1 - c)))
        return dict(start=local + sends, recv=recvs, send=sends, local=local)

    outs = [jax.ShapeDtypeStruct((s.shape[0], N_DEV) + s.shape[1:], s.dtype) for s in shards]
    return _Phase(list(shards) + [onto[a] for a in extra], outs, {n + k: a for k, a in enumerate(extra)},
                  4 * n, n, build)


def _gather_phase2(fulls):
    n = len(fulls)

    def build(ins, outs, ssem, rsem, lsem):
        x, y, c, chips = _place()
        sends, recvs = [], []
        for j, (cx, cy) in enumerate(chips):
            for a in range(n):
                mine, theirs = 4 * cx + 2 * cy + c, 4 * cx + 2 * cy + 1 - c
                sends.append(_remote(ins[a].at[:, mine], outs[a].at[:, mine], ssem, rsem, 3 * a + j, (x, y, 1 - c)))
                recvs.append(_remote(ins[a].at[:, theirs], outs[a].at[:, theirs], ssem, rsem, 3 * a + j, (x, y, 1 - c)))
        return dict(start=sends, recv=recvs, send=sends, local=[])

    outs = [jax.ShapeDtypeStruct(f.shape, f.dtype) for f in fulls]
    return _Phase(fulls, outs, {a: a for a in range(n)}, 3 * n, 0, build)


def _pair_phase(grads):
    n = len(grads)

    def build(ins, outs, ssem, rsem, lsem):
        x, y, c, _ = _place()
        cps = [_remote(ins[a].at[:, 2 * q + (1 - c)], outs[a].at[q], ssem, rsem, 4 * a + q, (x, y, 1 - c))
               for a in range(n) for q in range(4)]
        return dict(start=cps, recv=cps, send=cps, local=[])

    outs = [jax.ShapeDtypeStruct((4, g.shape[0]) + g.shape[2:], g.dtype) for g in grads]
    return _Phase(grads, outs, {}, 4 * n, 0, build)


def _chip_phase(bufs, accs, l, depth):
    n = len(bufs)
    has = accs is not None

    def build(ins, outs, ssem, rsem, lsem):
        x, y, c, chips = _place()
        myq = 2 * x + y
        local, sends, recvs = [], [], []
        for a in range(n):
            local.append(pltpu.make_async_copy(ins[a].at[myq], outs[a].at[myq, l], lsem.at[a]))
            for j, (cx, cy) in enumerate(chips):
                q = 2 * cx + cy
                sends.append(_remote(ins[a].at[q], outs[a].at[myq, l], ssem, rsem, 3 * a + j, (cx, cy, c)))
                recvs.append(_remote(ins[a].at[q], outs[a].at[q, l], ssem, rsem, 3 * a + j, (cx, cy, c)))
        return dict(start=local + sends, recv=recvs, send=sends, local=local)

    outs = [jax.ShapeDtypeStruct((4, depth) + b.shape[1:], b.dtype) for b in bufs]
    return _Phase(list(bufs) + (list(accs) if has else []), outs, {n + a: a for a in range(n)} if has else {},
                  3 * n, n, build)


def _grad_chunks(n, g):
    if n == "w_pool":
        return g.reshape(g.shape[0], N_DEV, g.shape[1] // N_DEV, g.shape[2])
    if n in ("w_in", "w_up"):
        return g[None]
    return g.reshape(1, N_DEV, -1, g.shape[-1])


class _ReduceScatter:
    def __init__(self, depth, cidx):
        self.depth, self.cidx, self.acc, self.count = depth, cidx, {}, 0
        self.small_gathered = None
        self.presummed = None

    def pair(self, names, grads):
        return _pair_phase([_grad_chunks(n, grads[n]) for n in names])

    def sums(self, names, grads, phase):
        out = []
        for n, r1 in zip(names, phase.results):
            out.append(_pair_sum(_grad_chunks(n, grads[n]), r1, self.cidx, name="rs_sum_%d" % self.count))
            self.count += 1
        return out

    def chip(self, names, bufs, l):
        accs = [self.acc[n] for n in names] if names[0] in self.acc else None
        return _chip_phase(bufs, accs, l, self.depth)

    def done(self, names, phase):
        for n, r in zip(names, phase.results):
            self.acc[n] = r


def _rest_views(fulls, d):
    a_out, b_out, pool, o, up, down = fulls
    grp = d // len(POOL_WINDOWS)
    return dict(w_a_out=a_out.reshape(d, d), w_b_out=b_out.reshape(d, d), w_o=o.reshape(d, d),
                w_pool=pool.reshape(len(POOL_WINDOWS), grp, grp), w_up8=up[0],
                wd4=down.reshape(N_DEV // 2, -1, d))


class _GatherPlan:
    def __init__(self):
        self.jobs, self.part, self.full = [], {}, {}

    def add(self, key, shard, first, second, rows=None, onto=None):
        self.jobs.append((key, shard, first, second, rows, onto))

    def phases(self, name):
        j1 = [j for j in self.jobs if j[2] == name]
        j2 = [j for j in self.jobs if j[3] == name]
        tagged = []
        if j1:
            onto = [self.part[j[5]] if j[5] else None for j in j1]
            tagged.append((self.part, j1, _gather_phase1([j[1] for j in j1], [j[4] for j in j1], onto)))
        if j2:
            tagged.append((self.full, j2, _gather_phase2([self.part[j[0]] for j in j2])))
        return tagged

    @staticmethod
    def collect(tagged):
        for store, jobs, phase in tagged:
            for j, r in zip(jobs, phase.results):
                store[j[0]] = r


def _layer_fwd(x, w, alpha, tag, plan=None):
    d = x.shape[1]
    grp = d // len(POOL_WINDOWS)

    def carried(kernel, *args, name, **kw):
        tagged = plan.phases(name) if plan else []
        out = kernel(*args, name=name, phases=[t[2] for t in tagged], **kw)
        _GatherPlan.collect(tagged)
        return out

    def weight(n, shape):
        return plan.full[n + tag].reshape(shape) if plan else w[n]

    z, h = carried(_mod_matmul, x, w["mod1"], w["w_in8"], w["b_in8"], flat_out=True, name="in_proj" + tag)
    ua, ub, dd = carried(_mix_fwd, z, w["conv_a"], w["lnv"], w["wm"], w["bias_full"], name="mix_fwd" + tag)
    w["w_a_out"], w["w_b_out"], w["w_o"] = (weight(n, (d, d)) for n in ("w_a_out", "w_b_out", "w_o"))
    w["w_pool"] = weight("w_pool", (len(POOL_WINDOWS), grp, grp))
    w["w_up8"] = weight("w_up8", (N_DEV, d, -1))
    ya = _mm_rows(ua, w["w_a_out"], dn=NN, name="a_out" + tag, out_dtype=ACT_DTYPE)
    yb = _mm_rows(ub, w["w_b_out"], dn=NN, name="b_out" + tag, out_dtype=ACT_DTYPE)
    ycp = _pool_proj(dd, w["w_pool"], dn=NN, name="pool_proj" + tag, out_dtype=ACT_DTYPE)
    merged = _merge(z, ya, yb, ycp, w["pool_scale"], name="merge" + tag)
    o = _mm_rows(merged, w["w_o"], dn=NN, name="o_proj" + tag)
    x1 = _resid_ln(x, o, w["ln1"], alpha, name="ln1" + tag)
    up8, h2 = carried(_mod_matmul, x1, w["mod2"], w["w_up8"], w["b_up8"], flat_out=False, name="up_proj" + tag)
    up4 = up8.reshape((2, up8.shape[0] // 2) + up8.shape[1:])
    f4 = carried(_ffn_fwd, up4, w["cw"], w["cb"], name="ffn_fwd" + tag)
    w["wd4"] = weight("wd4", (N_DEV // 2, -1, d))
    y2 = carried(_down_proj, f4, w["wd4"], name="down_proj" + tag)
    x2 = _resid_ln(x1, y2, w["ln2"], alpha, name="ln2" + tag)
    saved = dict(x=x, z=z, h=h, ua=ua, ub=ub, dd=dd, ya=ya, yb=yb, ycp=ycp, merged=merged, o=o, x1=x1,
                 up4=up4, h2=h2, f4=f4, y2=y2)
    return x2, saved


def _layer_bwd(dpart, dh_above, xmod_above, m_above, w, s, alpha, tag, l=0, above=None, rs=None, upper_reds=()):
    first, rest = ("w_in",), REST
    ph = lambda p: [p] if p is not None else ()
    pre = rs.presummed if rs is not None else None
    r1a = rs.pair(first, above) if above and not pre else None
    dy2, dx1p, red2 = _resid_ln_bwd(dpart, dh_above, xmod_above, m_above, s["x1"], s["y2"], w["ln2"], alpha,
                                    name="ln2_bwd" + tag, phases=ph(r1a))
    r1b = rs.pair(rest, above) if above and not pre else None
    df4 = _down_bwd(dy2, w["wd4"], name="down_bwd" + tag, phases=ph(r1b))
    gw_down4 = _tn_shards_lhs(s["f4"], dy2, name="gw_down" + tag)
    r3a = r3b = r3c = None
    if above:
        if pre:
            bufs, rs.presummed = pre, None
        else:
            bufs = dict(zip(first + rest, rs.sums(first, above, r1a) + rs.sums(rest, above, r1b)))
        light = tuple(n for n in rest if n != "w_up")
        r3a = rs.chip(first, [bufs[n] for n in first], l + 1)
    dup4, redf = _ffn_bwd(s["up4"], df4, w["cw"], w["cb"], name="ffn_bwd" + tag, phases=ph(r3a))
    dup8 = dup4.reshape((dup4.shape[0] * dup4.shape[1],) + dup4.shape[2:])
    if above:
        rs.done(first, r3a)
        r3b = rs.chip(("w_up",), [bufs["w_up"]], l + 1)
    gw_up8 = _tn_shards_lhs(dup8, s["h2"], name="gw_up" + tag, phases=ph(r3b))
    if above:
        rs.done(("w_up",), r3b)
        r3c = rs.chip(light, [bufs[n] for n in light], l + 1)
    own = rs is not None and l == 0
    big = dict(w_up=gw_up8, w_down=gw_down4)
    early = ("w_down", "w_up")
    o1 = rs.pair(early, big) if own else None
    dh2 = _nt_shards(dup8, w["w_up8"], name="up_bwd" + tag, phases=list(ph(r3c)) + list(ph(o1)))
    if above:
        rs.done(light, r3c)
    if own:
        sb_o = rs.sums(early, big, o1)
    do, dxp, red1 = _resid_ln_bwd(dx1p, dh2, s["x1"], w["mod2"][0:1], s["x"], s["o"], w["ln1"], alpha,
                                  name="ln1_bwd" + tag)
    dm = _mm_rows(do, w["w_o"], dn=NT, name="o_bwd" + tag, out_dtype=ACT_DTYPE)
    big["w_o"] = _mm_tn(s["merged"], do, name="gw_o" + tag)
    dya, dyb, dyc, dz, redg = _gate_bwd(dm, s["z"], s["ya"], s["yb"], s["ycp"], w["pool_scale"], name="gate_bwd" + tag)
    dua = _mm_rows(dya, w["w_a_out"], dn=NT, name="a_out_bwd" + tag, out_dtype=ACT_DTYPE)
    dub = _mm_rows(dyb, w["w_b_out"], dn=NT, name="b_out_bwd" + tag, out_dtype=ACT_DTYPE)
    ddd = _pool_proj(dyc, w["w_pool"], dn=NT, name="pool_bwd" + tag, out_dtype=ACT_DTYPE)
    big["w_a_out"] = _mm_tn(s["ua"], dya, name="gw_a_out" + tag)
    big["w_b_out"] = _mm_tn(s["ub"], dyb, name="gw_b_out" + tag)
    big["w_pool"] = _tn_pool(s["dd"], dyc, w["w_pool"].shape[0], name="gw_pool" + tag)
    o3 = rs.chip(early, sb_o, l) if own else None
    dz, redm, dws, dbs = _mix_bwd(dz, dua, dub, ddd, s["z"], w["conv_a"], w["lnv"], w["wm"], w["wmt"],
                                  w["bias_full"], w["mask"], name="mix_bwd" + tag, phases=ph(o3))
    reds = dict(red2=red2, redf=redf, red1=red1, redg=redg, redm=redm, dws=dws, dbs=dbs)
    mid = ("w_o", "w_a_out", "w_b_out", "w_pool")
    o1b = sg1 = None
    if own:
        rs.done(early, o3)
        o1b = rs.pair(mid, big)
        sg1 = _gather_phase1([_small_payload([reds] + list(upper_reds))])
    big["w_in"] = _tn_cols_rhs(s["h"], dz, w["w_in8"].shape[0], name="gw_in" + tag,
                               phases=[o1b, sg1] if own else ())
    pending = None
    if own:
        sb_m = rs.sums(mid, big, o1b)
        o1c, o3b, sg2 = rs.pair(first, big), rs.chip(mid, sb_m, l), _gather_phase2(sg1.results)
        split_tm = 1024
        n_tiles = dz.shape[0] // _row_tile(dz.shape[0], split_tm)
        dh = _nt_cols(dz, w["w_in8"], name="in_bwd" + tag + "_a", phases=[o1c, o3b, sg2], tm=split_tm, tiles=(0, 1))
        rs.done(mid, o3b)
        rs.small_gathered = sg2.results[0]
        o3c = rs.chip(first, rs.sums(first, big, o1c), l)
        if n_tiles > 1:
            dh = _nt_cols(dz, w["w_in8"], name="in_bwd" + tag + "_b", phases=[o3c], tm=split_tm,
                          tiles=(1, n_tiles - 1), into=dh)
            rs.done(first, o3c)
        else:
            pending = (first, o3c)
    elif rs is not None:
        pa, pb = rs.pair(first, big), rs.pair(rest, big)
        dh = _nt_cols(dz, w["w_in8"], name="in_bwd" + tag, phases=[pa, pb])
        rs.presummed = dict(zip(first + rest, rs.sums(first, big, pa) + rs.sums(rest, big, pb)))
    else:
        dh = _nt_cols(dz, w["w_in8"], name="in_bwd" + tag)
    return dxp, dh, big, reds, pending


def _local_step(x, tgt, ws, alpha, plan=None, rs=None):
    depth = len(ws)
    saved = []
    y = x
    for l in range(depth):
        if plan and l > 0:
            ws[l]["w_in8"] = plan.full["w_in8_l%d" % l][0]
        y, s = _layer_fwd(y, ws[l], alpha, "_l%d" % l, plan)
        saved.append(s)
    dpart, loss_blk = _loss_grad(y, tgt, name="loss_grad")
    dh = xmod = mvec = above = pending = None
    bigs, reds = [None] * depth, [None] * depth
    for l in reversed(range(depth)):
        dpart, dh, bigs[l], reds[l], pending = _layer_bwd(dpart, dh, xmod, mvec, ws[l], saved[l], alpha, "_l%d" % l,
                                                          l, above if rs else None, rs, reds[l + 1:])
        xmod, mvec, above = saved[l]["x"], ws[l]["mod1"][0:1], bigs[l]
    grad_x, red0 = _resid_ln_bwd(dpart, dh, xmod, mvec, None, None, None, alpha, name="in_bwd_tail",
                                 phases=[pending[1]] if pending else ())
    if pending:
        rs.done(*pending)
    d_ada = []
    for l in range(depth):
        below = red0 if l == 0 else reds[l - 1]["red2"]
        r1, r2 = reds[l]["red1"], reds[l]["red2"]
        d_ada.append(jnp.stack([below[1], below[0], r1[4], r1[1], r1[0], r2[4]]))
    return loss_blk, grad_x, bigs, reds, jnp.stack(d_ada)


def _small_grads(r):
    redm, redg, redf = r["redm"], r["redg"], r["redf"]
    ng = r["dws"].shape[0]
    return dict(
        b_in=jnp.concatenate([redm[0:6], redg[1:4]], axis=0).reshape(-1),
        conv_a=redm[6:9], ln_v_g=redm[9], ln_v_b=redm[10],
        w_spatial=r["dws"], b_spatial=r["dbs"][:, :ng].T,
        pool_scale=redg[0], ln1_g=r["red1"][2], ln1_b=r["red1"][3],
        b_up=jnp.concatenate([redf[:, 4, :].reshape(-1), redf[:, 5, :].reshape(-1)]),
        conv_ffn=jnp.transpose(redf[:, 0:3, :], (1, 0, 2)).reshape(3, -1), conv_ffn_b=redf[:, 3, :].reshape(-1),
        ln2_g=r["red2"][2], ln2_b=r["red2"][3])


def _small_payload(reds):
    smalls = [_small_grads(r) for r in reds]
    order = SMALL_REPLICATED + SMALL_SHARDED
    flat = jnp.concatenate([smalls[l][n].reshape(-1) for n in order for l in range(len(reds))])
    return _as_rows(flat)[None]


def _layer_weights(l, ada, conv_a, conv_ffn, p):
    sh1, sc1, gt1, sh2, sc2, gt2 = (ada[l, k][None, :] for k in range(6))
    nb = N_DEV
    fs = p["b_up"].shape[1] // nb
    nj = nb // 2
    pos = jnp.arange(GMLP_BLOCK)
    allowed = (pos[None, :] // CHUNK) <= (pos[:, None] // CHUNK)
    wmask = jnp.where(allowed[None], p["w_spatial"][l], 0.0)
    return dict(
        mod1=jnp.concatenate([1.0 + sc1, sh1]), mod2=jnp.concatenate([1.0 + sc2, sh2]),
        ln1=jnp.concatenate([gt1, p["ln1_g"][l][None], p["ln1_b"][l][None]]),
        ln2=jnp.concatenate([gt2, p["ln2_g"][l][None], p["ln2_b"][l][None]]),
        b_in8=p["b_in"][l].reshape(N_DEV, 1, -1), b_up8=p["b_up"][l].reshape(nb, 1, fs),
        conv_a=conv_a[l], lnv=jnp.stack([p["ln_v_g"][l], p["ln_v_b"][l]]),
        wm=wmask.astype(BF), wmt=jnp.transpose(wmask, (0, 2, 1)).astype(BF),
        bias_full=jnp.repeat(p["b_spatial"][l].T, GMLP_BLOCK, axis=1), mask=allowed.astype(F32),
        pool_scale=p["pool_scale"][l][None],
        cw=jnp.transpose(conv_ffn[l].reshape(3, nj, fs), (1, 0, 2)), cb=p["conv_ffn_b"][l].reshape(nj, 1, fs))


ANY = pl.BlockSpec(memory_space=pl.ANY)


def _place():
    x, y, c = lax.axis_index("x"), lax.axis_index("y"), lax.axis_index("c")
    chips = [(1 - x, y), (x, 1 - y), (1 - x, 1 - y)]
    return x, y, c, chips


def _allgather_vmem(xs, *, name):
    r, cdim = xs.shape

    def body(x_ref, out_ref, send_sems, recv_sems, local_sem):
        x, y, c, chips = _place()
        me, sibling = (x, y, c), (x, y, 1 - c)

        def rows(px, py, pc):
            return out_ref.at[pl.ds((4 * px + 2 * py + pc) * r, r), :]

        def copy(k, block, to, src=None):
            return pltpu.make_async_remote_copy(
                src_ref=rows(*block) if src is None else src, dst_ref=rows(*block),
                send_sem=send_sems.at[k], recv_sem=recv_sems.at[k], device_id=to, device_id_type=MESH)

        mine = pltpu.make_async_copy(x_ref, rows(*me), local_sem)
        mine.start()
        first = [copy(0, me, sibling, src=x_ref)]
        first += [copy(1 + j, me, (*chip, c), src=x_ref) for j, chip in enumerate(chips)]
        for cp in first:
            cp.start()
        passed = [copy(4 + j, (*chip, c), sibling) for j, chip in enumerate(chips)]
        for j, chip in enumerate(chips):
            copy(1 + j, (*chip, c), me).wait_recv()
            passed[j].start()
        copy(0, sibling, me).wait_recv()
        for j, chip in enumerate(chips):
            copy(4 + j, (*chip, 1 - c), me).wait_recv()
        for cp in first + passed:
            cp.wait_send()
        mine.wait()

    return pl.pallas_call(
        body, name=name, out_shape=jax.ShapeDtypeStruct((N_DEV * r, cdim), xs.dtype),
        in_specs=[pl.BlockSpec(memory_space=pltpu.VMEM)], out_specs=pl.BlockSpec(memory_space=pltpu.VMEM),
        scratch_shapes=[pltpu.SemaphoreType.DMA((7,)), pltpu.SemaphoreType.DMA((7,)), pltpu.SemaphoreType.DMA],
        compiler_params=_params(),
    )(xs)


def _gather_weights(shards, *, name):
    n = len(shards)

    def body(*refs):
        ins, outs = refs[:n], refs[n:2 * n]
        send_sems, recv_sems, local_sems = refs[2 * n:]
        x, y, c, chips = _place()
        me, sibling = (x, y, c), (x, y, 1 - c)

        def slot(a, px, py, pc):
            return outs[a].at[:, 4 * px + 2 * py + pc]

        def copy(a, k, block, to, src=None):
            return pltpu.make_async_remote_copy(
                src_ref=slot(a, *block) if src is None else src, dst_ref=slot(a, *block),
                send_sem=send_sems.at[7 * a + k], recv_sem=recv_sems.at[7 * a + k], device_id=to,
                device_id_type=MESH)

        mine = [pltpu.make_async_copy(ins[a], slot(a, *me), local_sems.at[a]) for a in range(n)]
        for cp in mine:
            cp.start()
        first = []
        for j, chip in enumerate(chips):
            first += [copy(a, 1 + j, me, (*chip, c), src=ins[a]) for a in range(n)]
        first += [copy(a, 0, me, sibling, src=ins[a]) for a in range(n)]
        for cp in first:
            cp.start()
        passed = []
        for j, chip in enumerate(chips):
            for a in range(n):
                copy(a, 1 + j, (*chip, c), me).wait_recv()
                fwd = copy(a, 4 + j, (*chip, c), sibling)
                fwd.start()
                passed.append(fwd)
        for a in range(n):
            copy(a, 0, sibling, me).wait_recv()
        for j, chip in enumerate(chips):
            for a in range(n):
                copy(a, 4 + j, (*chip, 1 - c), me).wait_recv()
        for cp in first + passed:
            cp.wait_send()
        for cp in mine:
            cp.wait()

    out_shape = [jax.ShapeDtypeStruct((s.shape[0], N_DEV) + s.shape[1:], s.dtype) for s in shards]
    return pl.pallas_call(
        body, name=name, out_shape=out_shape, in_specs=[ANY] * n, out_specs=[ANY] * n,
        scratch_shapes=[pltpu.SemaphoreType.DMA((7 * n,)), pltpu.SemaphoreType.DMA((7 * n,)),
                        pltpu.SemaphoreType.DMA((n,))],
        compiler_params=_params(),
    )(*shards)


def _pick_tile(r, cap):
    best = None
    for t in range(8, min(r, cap) + 1, 8):
        if r % t == 0:
            best = t
    return best if best is not None else r


def _pair_sum(g, r1, cidx, *, name):
    p, _, r, cdim = g.shape
    tr = _pick_tile(r, 256)

    def body(c_ref, g_ref, r_ref, o_ref):
        del c_ref
        o_ref[...] = (g_ref[...].astype(F32) + r_ref[...].astype(F32)).astype(BF)

    grid_spec = pltpu.PrefetchScalarGridSpec(
        num_scalar_prefetch=1, grid=(4, r // tr),
        in_specs=[pl.BlockSpec((p, None, tr, cdim), lambda q, i, c: (0, 2 * q + c[0], i, 0)),
                  pl.BlockSpec((None, p, tr, cdim), lambda q, i, c: (q, 0, i, 0))],
        out_specs=pl.BlockSpec((None, p, tr, cdim), lambda q, i, c: (q, 0, i, 0)))
    return pl.pallas_call(
        body, name=name, grid_spec=grid_spec, out_shape=jax.ShapeDtypeStruct((4, p, r, cdim), BF),
        compiler_params=_params(("arbitrary", "arbitrary")),
    )(cidx, g, r1)


def _ada_fwd(c_all, w_ada, *, name):
    depth, d, ns = w_ada.shape
    nb = c_all.shape[0]

    def body(c_ref, w_ref, o_ref):
        cv = c_ref[...]
        act = cv * jax.nn.sigmoid(cv)
        o_ref[...] = jnp.dot(act, w_ref[...], preferred_element_type=F32, precision=lax.Precision.HIGHEST)

    return pl.pallas_call(
        body, name=name, grid=(depth,),
        in_specs=[pl.BlockSpec((nb, d), lambda l: (0, 0)), pl.BlockSpec((None, d, ns), lambda l: (l, 0, 0))],
        out_specs=pl.BlockSpec((None, nb, ns), lambda l: (l, 0, 0)),
        out_shape=jax.ShapeDtypeStruct((depth, nb, ns), F32), compiler_params=_params(("parallel",)),
    )(c_all, w_ada)


def _ada_bwd(ct, dmine, dall, *, name):
    depth, nb, ns = dmine.shape
    d = ct.shape[0]

    def body(ct_ref, dm_ref, da_ref, gw_ref, gb_ref):
        cv = ct_ref[...]
        act = cv * jax.nn.sigmoid(cv)
        gw_ref[...] = jnp.dot(act, dm_ref[...], preferred_element_type=F32, precision=lax.Precision.HIGHEST)
        s = da_ref[0]
        for b in range(1, nb):
            s = s + da_ref[b]
        gb_ref[...] = s

    return pl.pallas_call(
        body, name=name, grid=(depth,),
        in_specs=[pl.BlockSpec((d, nb), lambda l: (0, 0)), pl.BlockSpec((None, nb, ns), lambda l: (l, 0, 0)),
                  pl.BlockSpec(dall.shape, lambda l: (0, 0, 0))],
        out_specs=[pl.BlockSpec((None, d, ns), lambda l: (l, 0, 0)), pl.BlockSpec(dall.shape[1:], lambda l: (0, 0))],
        out_shape=[jax.ShapeDtypeStruct((depth, d, ns), F32), jax.ShapeDtypeStruct(dall.shape[1:], F32)],
        compiler_params=_params(("arbitrary",)),
    )(ct, dmine, dall)


def _sum_parts(parts, *, name):
    p, r, cdim = parts.shape
    tr = _pick_tile(r, 512)

    def body(p_ref, o_ref):
        s = p_ref[0]
        for k in range(1, p):
            s = s + p_ref[k]
        o_ref[...] = s

    return pl.pallas_call(
        body, name=name, grid=(r // tr,),
        in_specs=[pl.BlockSpec((p, tr, cdim), lambda i: (0, i, 0))], out_specs=pl.BlockSpec((tr, cdim), lambda i: (i, 0)),
        out_shape=jax.ShapeDtypeStruct((r, cdim), F32), compiler_params=_params(("parallel",)),
    )(parts)


def _adamw(parts, w, m, v, *, name):
    p, depth, r, cdim = parts.shape
    tr = _pick_tile(r, 256)

    def body(p_ref, w_ref, m_ref, v_ref, g_out, d_out, m_out, v_out):
        g = p_ref[0].astype(F32)
        for k in range(1, p):
            g = g + p_ref[k].astype(F32)
        m2 = ADAM_B1 * m_ref[...] + (1.0 - ADAM_B1) * g
        v2 = ADAM_B2 * v_ref[...] + (1.0 - ADAM_B2) * (g * g)
        m_hat = m2 / (1.0 - ADAM_B1 ** ADAM_STEP)
        v_hat = v2 / (1.0 - ADAM_B2 ** ADAM_STEP)
        g_out[...] = g
        d_out[...] = -ADAM_LR * (m_hat / (jnp.sqrt(v_hat) + ADAM_EPS) + ADAM_WD * w_ref[...])
        m_out[...] = m2
        v_out[...] = v2

    blk = pl.BlockSpec((None, tr, cdim), lambda l, i: (l, i, 0))
    out = jax.ShapeDtypeStruct((depth, r, cdim), F32)
    return pl.pallas_call(
        body, name=name, grid=(depth, r // tr),
        in_specs=[pl.BlockSpec((p, None, tr, cdim), lambda l, i: (0, l, i, 0)), blk, blk, blk],
        out_specs=[blk, blk, blk, blk], out_shape=[out, out, out, out],
        compiler_params=_params(("parallel", "parallel")),
    )(parts, w, m, v)


BIG = ("w_in", "w_a_out", "w_b_out", "w_pool", "w_o", "w_up", "w_down")
SMALL_REPLICATED = ("b_in", "ln_v_g", "ln_v_b", "w_spatial", "b_spatial", "pool_scale", "ln1_g", "ln1_b", "b_up",
                    "conv_ffn_b", "ln2_g", "ln2_b")
SMALL_SHARDED = ("conv_a", "conv_ffn")
WEIGHTS = ("w_ada", "b_ada", "w_in", "b_in", "conv_a", "w_a_out", "ln_v_g", "ln_v_b", "w_spatial", "b_spatial",
           "w_b_out", "w_pool", "pool_scale", "w_o", "ln1_g", "ln1_b", "w_up", "b_up", "conv_ffn", "conv_ffn_b",
           "w_down", "ln2_g", "ln2_b")
LANES = 128


def _as_rows(flat, mult=8):
    n = flat.shape[0]
    pad = (-n) % (LANES * mult)
    if pad:
        flat = jnp.concatenate([flat, jnp.zeros((pad,), flat.dtype)])
    return flat.reshape(-1, LANES)


def _shard3(a):
    return a.reshape((-1,) + a.shape[-2:])


def kernel(x, c, w_ada, b_ada, w_in, b_in, conv_a, w_a_out, ln_v_g, ln_v_b, w_spatial, b_spatial, w_b_out, w_pool, pool_scale, w_o, ln1_g, ln1_b, w_up, b_up, conv_ffn, conv_ffn_b, w_down, ln2_g, ln2_b, loss_target, m_w_ada, m_b_ada, m_w_in, m_b_in, m_conv_a, m_w_a_out, m_ln_v_g, m_ln_v_b, m_w_spatial, m_b_spatial, m_w_b_out, m_w_pool, m_pool_scale, m_w_o, m_ln1_g, m_ln1_b, m_w_up, m_b_up, m_conv_ffn, m_conv_ffn_b, m_w_down, m_ln2_g, m_ln2_b, v_w_ada, v_b_ada, v_w_in, v_b_in, v_conv_a, v_w_a_out, v_ln_v_g, v_ln_v_b, v_w_spatial, v_b_spatial, v_w_b_out, v_w_pool, v_pool_scale, v_w_o, v_ln1_g, v_ln1_b, v_w_up, v_b_up, v_conv_ffn, v_conv_ffn_b, v_w_down, v_ln2_g, v_ln2_b):
    p = dict(locals())
    depth, d = w_in.shape[0], w_in.shape[1]
    alpha = (2 * depth) ** 0.25
    me = 4 * lax.axis_index("x") + 2 * lax.axis_index("y") + lax.axis_index("c")
    cidx = lax.axis_index("c").astype(jnp.int32).reshape(1)

    n_ca, n_cf = conv_a.size, conv_ffn.size
    packed = _as_rows(jnp.concatenate([c.reshape(-1), conv_a.reshape(-1), conv_ffn.reshape(-1)]))
    got = _allgather_vmem(packed, name="gather_cond").reshape(N_DEV, -1)
    c_all = got[:, :d]
    ct = c_all.T
    conv_a_full = jnp.transpose(got[:, d:d + n_ca].reshape((N_DEV,) + conv_a.shape), (1, 2, 0, 3)).reshape(depth, 3, -1)
    conv_ffn_full = jnp.transpose(got[:, d + n_ca:d + n_ca + n_cf].reshape((N_DEV,) + conv_ffn.shape),
                                  (1, 2, 0, 3)).reshape(depth, 3, -1)

    ns_ada = w_ada.shape[2]
    ada_part = _ada_fwd(c_all, w_ada, name="ada_fwd")
    ada_all = _allgather_vmem(_as_rows(ada_part.reshape(-1)), name="gather_ada")
    ada_all = ada_all.reshape(N_DEV, depth, N_DEV, ns_ada)
    ada_mine = lax.dynamic_index_in_dim(ada_all, me, axis=2, keepdims=False)
    ada = jnp.transpose(ada_mine, (1, 0, 2)).reshape(depth, -1) + b_ada
    ada = ada.reshape(depth, 6, d)

    shards = [{n: _shard3(p[n][l].astype(BF)) for n in BIG} for l in range(depth)]
    ws = [_layer_weights(l, ada, conv_a_full, conv_ffn_full, p) for l in range(depth)]
    ws[0]["w_in8"] = _gather_weights([shards[0]["w_in"]], name="gather_w_in0")[0][0]
    plan = _GatherPlan()
    four = ("w_a_out", "w_b_out", "w_pool", "w_o")
    for l in range(depth):
        t, prev, sh = "_l%d" % l, "_l%d" % (l - 1), shards[l]
        if l == 0:
            for n in four:
                plan.add(n + t, sh[n], "in_proj" + t, "mix_fwd" + t)
            plan.add("w_up8" + t, sh["w_up"], "in_proj" + t, "mix_fwd" + t)
            plan.add("wd4" + t, sh["w_down"], "mix_fwd" + t, "up_proj" + t)
        else:
            half = sh["w_in"].shape[1] // 2
            plan.add("w_in8_top" + t, sh["w_in"], "up_proj" + prev, None, rows=(0, half))
            plan.add("w_in8" + t, sh["w_in"], "ffn_fwd" + prev, "down_proj" + prev, rows=(half, half),
                     onto="w_in8_top" + t)
            for n in four:
                plan.add(n + t, sh[n], "down_proj" + prev, "in_proj" + t)
            plan.add("w_up8" + t, sh["w_up"], "in_proj" + t, "mix_fwd" + t)
            plan.add("wd4" + t, sh["w_down"], "in_proj" + t, "mix_fwd" + t)

    rs = _ReduceScatter(depth, cidx)
    loss_blk, grad_x, bigs, reds, d_ada = _local_step(x[0], loss_target[0], ws, alpha, plan, rs)
    loss = lax.psum(loss_blk[0, 0], ("x", "y", "c"))

    dada_all = _allgather_vmem(_as_rows(d_ada.reshape(-1)), name="gather_dada")
    dada_all = dada_all.reshape(N_DEV, -1, LANES)
    dflat = dada_all.reshape(N_DEV, depth, 6 * d)
    dmine = lax.dynamic_slice_in_dim(dflat, me * ns_ada, ns_ada, axis=2)
    gw_ada, gb_rows = _ada_bwd(ct, jnp.transpose(dmine, (1, 0, 2)), dada_all, name="ada_bwd")
    gb_ada = gb_rows.reshape(-1)[:depth * 6 * d].reshape(depth, 6 * d)

    out = {}
    for n in BIG:
        parts = rs.acc[n]
        view = (lambda a: jnp.swapaxes(a, 1, 2)) if n == "w_up" else (lambda a: a)
        shard_shape = view(p[n]).shape
        w3 = view(p[n]).reshape(depth, -1, shard_shape[-1])
        parts4 = parts.reshape((4,) + w3.shape)
        res = _adamw(parts4, w3, view(p["m_" + n]).reshape(w3.shape), view(p["v_" + n]).reshape(w3.shape),
                     name="adamw_" + n)
        out[n] = [view(r.reshape(shard_shape)) for r in res]
    out["w_ada"] = _adamw(gw_ada[None], w_ada, m_w_ada, v_w_ada, name="adamw_w_ada")

    order = SMALL_REPLICATED + SMALL_SHARDED
    n_rep = sum(p[n].size for n in SMALL_REPLICATED)
    n_pay = n_rep + N_DEV * (conv_a.size + conv_ffn.size)
    gsum = _sum_parts(rs.small_gathered.reshape(N_DEV, -1, LANES), name="sum_small").reshape(-1)[:n_pay]
    ga_full = gsum[n_rep:n_rep + depth * 3 * d].reshape(depth, 3, d)
    gf_full = gsum[n_rep + depth * 3 * d:].reshape(depth, 3, -1)
    ca_w, cf_w = conv_a.shape[2], conv_ffn.shape[2]
    g_ca = lax.dynamic_slice_in_dim(ga_full, me * ca_w, ca_w, axis=2)
    g_cf = lax.dynamic_slice_in_dim(gf_full, me * cf_w, cf_w, axis=2)
    names = ("b_ada",) + order
    gflat = _as_rows(jnp.concatenate([gb_ada.reshape(-1), gsum[:n_rep], g_ca.reshape(-1), g_cf.reshape(-1)]))
    pack = lambda pre: _as_rows(jnp.concatenate([p[pre + n].reshape(-1) for n in names]))
    res = _adamw(gflat[None, None], pack("")[None], pack("m_")[None], pack("v_")[None], name="adamw_small")
    off = 0
    for n in names:
        size = p[n].size
        out[n] = [r.reshape(-1)[off:off + size].reshape(p[n].shape) for r in res]
        off += size

    return (loss, grad_x[None]) + tuple(out[n][k] for k in range(4) for n in WEIGHTS)
```

```python
import functools

import jax
import jax.numpy as jnp
from jax import lax
from jax.experimental import pallas as pl
from jax.experimental.pallas import tpu as pltpu

F32 = jnp.float32
BF = jnp.bfloat16
MESH = pl.DeviceIdType.MESH

LN_EPS = 1e-5
POOL_WINDOWS = (2, 4, 8, 16)
GMLP_BLOCK = 128
CHUNK = 64
HALO = 16
ADAM_LR, ADAM_B1, ADAM_B2, ADAM_EPS, ADAM_WD, ADAM_STEP = 0.001, 0.9, 0.999, 1e-08, 0.01, 10
N_DEV = 8
VMEM_LIMIT = 56 * 1024 * 1024

GRAD_DTYPE = BF
ACT_DTYPE = BF

NN = ((1,), (0,))
NT = ((1,), (1,))
TN = ((0,), (0,))


def _params(sem=None, vmem=VMEM_LIMIT, **kw):
    if sem is not None:
        kw["dimension_semantics"] = sem
    return pltpu.CompilerParams(vmem_limit_bytes=vmem, **kw)


class _Phase:
    def __init__(self, ins, out_shapes, aliases, n_remote, n_local, build):
        self.ins, self.out_shapes, self.aliases = list(ins), list(out_shapes), dict(aliases)
        self.n_remote, self.n_local, self.build = n_remote, n_local, build
        self.results = None


def _pcall(body, args, *, name, grid, in_specs, out_specs, out_shape, scratch_shapes=(), sem=None, aliases=None,
           phases=()):
    aliases = dict(aliases or {})
    if not phases:
        return pl.pallas_call(
            body, name=name, grid=grid, in_specs=list(in_specs), out_specs=out_specs, out_shape=out_shape,
            scratch_shapes=list(scratch_shapes), input_output_aliases=aliases, compiler_params=_params(sem),
        )(*args)
    single = not isinstance(out_shape, (list, tuple))
    o_specs = [out_specs] if single else list(out_specs)
    o_shapes = [out_shape] if single else list(out_shape)
    n_in, n_out, n_scr = len(args), len(o_shapes), len(scratch_shapes)
    ex_args, ex_out, sems = [], [], []
    for ph in phases:
        for src, dst in ph.aliases.items():
            aliases[n_in + len(ex_args) + src] = n_out + len(ex_out) + dst
        ex_args += ph.ins
        ex_out += ph.out_shapes
        sems += [pltpu.SemaphoreType.DMA((max(ph.n_remote, 1),)), pltpu.SemaphoreType.DMA((max(ph.n_remote, 1),)),
                 pltpu.SemaphoreType.DMA((max(ph.n_local, 1),))]

    def wrapped(*refs):
        pos = n_in
        ph_in = []
        for ph in phases:
            ph_in.append(refs[pos:pos + len(ph.ins)])
            pos += len(ph.ins)
        base_out = refs[pos:pos + n_out]
        pos += n_out
        ph_out = []
        for ph in phases:
            ph_out.append(refs[pos:pos + len(ph.out_shapes)])
            pos += len(ph.out_shapes)
        base_scr = refs[pos:pos + n_scr]
        ph_sems = refs[pos + n_scr:]
        first = last = None
        for ax, n in enumerate(grid):
            pid = pl.program_id(ax)
            first = (pid == 0) if first is None else first & (pid == 0)
            last = (pid == n - 1) if last is None else last & (pid == n - 1)

        def ops(k):
            return phases[k].build(ph_in[k], ph_out[k], *ph_sems[3 * k:3 * k + 3])

        @pl.when(first)
        def _():
            for k in range(len(phases)):
                for cp in ops(k)["start"]:
                    cp.start()

        body(*refs[:n_in], *base_out, *base_scr)

        @pl.when(last)
        def _():
            for k in range(len(phases)):
                o = ops(k)
                for cp in o["recv"]:
                    cp.wait_recv()
                for cp in o["send"]:
                    cp.wait_send()
                for cp in o["local"]:
                    cp.wait()

    hbm = pl.BlockSpec(memory_space=pl.ANY)
    res = pl.pallas_call(
        wrapped, name=name, grid=grid, in_specs=list(in_specs) + [hbm] * len(ex_args),
        out_specs=o_specs + [hbm] * len(ex_out), out_shape=o_shapes + ex_out,
        scratch_shapes=list(scratch_shapes) + sems, input_output_aliases=aliases,
        compiler_params=_params(("arbitrary",) * len(grid)),
    )(*args, *ex_args)
    pos = n_out
    for ph in phases:
        ph.results = list(res[pos:pos + len(ph.out_shapes)])
        pos += len(ph.out_shapes)
    return res[0] if single else list(res[:n_out])


def _gelu_parts(x):
    k = 0.7978845608028654
    x2 = x * x
    t = jnp.tanh(k * (x + 0.044715 * (x2 * x)))
    cdf = 0.5 * (1.0 + t)
    dcdf = 0.5 * (1.0 - t * t) * (k * (1.0 + 3.0 * 0.044715 * x2))
    return x * cdf, cdf + x * dcdf


def _gelu(x):
    t = jnp.tanh(0.7978845608028654 * (x + 0.044715 * (x * x * x)))
    return x * (0.5 * (1.0 + t))


def _rowsum(v):
    return jnp.sum(v, axis=0, keepdims=True)


def _ln_stats(r):
    mu = jnp.mean(r, axis=-1, keepdims=True)
    xc = r - mu
    var = jnp.mean(xc * xc, axis=-1, keepdims=True)
    rstd = lax.rsqrt(var + LN_EPS)
    return xc * rstd, rstd


def _ln_bwd(dy, xhat, rstd, gain):
    dxh = dy * gain
    m1 = jnp.mean(dxh, axis=-1, keepdims=True)
    m2 = jnp.mean(dxh * xhat, axis=-1, keepdims=True)
    return rstd * (dxh - m1 - xhat * m2)


def _matmul(a, b, *, dn, grid, a_spec, b_spec, o_spec, out_shape, acc_shape, name, phases=(), into=None):
    nk = grid[2]
    direct = out_shape.dtype == F32

    def body(a_ref, b_ref, *rest):
        o_ref, scratch = (rest[1], rest[2:]) if into is not None else (rest[0], rest[1:])
        prod = lax.dot_general(a_ref[...], b_ref[...], (dn, ((), ())), preferred_element_type=F32)
        if nk == 1:
            o_ref[...] = prod.astype(o_ref.dtype)
            return
        acc = o_ref if direct else scratch[0]
        k = pl.program_id(2)

        @pl.when(k == 0)
        def _():
            acc[...] = prod

        @pl.when(k > 0)
        def _():
            acc[...] += prod

        if not direct:
            @pl.when(k == nk - 1)
            def _():
                o_ref[...] = acc[...].astype(o_ref.dtype)

    scratch = [] if (direct or nk == 1) else [pltpu.VMEM(acc_shape, F32)]
    args, in_specs, aliases = (a, b), [a_spec, b_spec], None
    if into is not None:
        args, in_specs, aliases = (a, b, into), in_specs + [pl.BlockSpec(memory_space=pl.ANY)], {2: 0}
    return _pcall(body, args, name=name, grid=grid, in_specs=in_specs, out_specs=o_spec,
                  out_shape=out_shape, scratch_shapes=scratch, sem=("parallel", "parallel", "arbitrary"),
                  aliases=aliases, phases=phases)


def _row_tile(m, want):
    t = min(m, want)
    assert m % t == 0
    return t


def _mm_rows(a, w, *, dn, name, out_dtype=F32, tm=2048):
    m, k = a.shape
    n = w.shape[1] if dn == NN else w.shape[0]
    tm = _row_tile(m, tm)
    return _matmul(
        a, w, dn=dn, grid=(m // tm, 1, 1), name=name,
        a_spec=pl.BlockSpec((tm, k), lambda i, j, kk: (i, 0)),
        b_spec=pl.BlockSpec(w.shape, lambda i, j, kk: (0, 0)),
        o_spec=pl.BlockSpec((tm, n), lambda i, j, kk: (i, 0)),
        out_shape=jax.ShapeDtypeStruct((m, n), out_dtype), acc_shape=(tm, n))


def _mm_tn(a, b, *, name, tk=4096):
    m, ka = a.shape
    n = b.shape[1]
    tk = _row_tile(m, tk)
    return _matmul(
        a, b, dn=TN, grid=(1, 1, m // tk), name=name,
        a_spec=pl.BlockSpec((tk, ka), lambda i, j, kk: (kk, 0)),
        b_spec=pl.BlockSpec((tk, n), lambda i, j, kk: (kk, 0)),
        o_spec=pl.BlockSpec((ka, n), lambda i, j, kk: (0, 0)),
        out_shape=jax.ShapeDtypeStruct((ka, n), GRAD_DTYPE), acc_shape=(ka, n))


def _mod_matmul(x, mod, w8, bias8, *, flat_out, name, tm=2048, phases=()):
    m, k = x.shape
    nb, _, ns = w8.shape
    tm = _row_tile(m, tm)

    def body(x_ref, mod_ref, w_ref, b_ref, o_ref, h_ref, hs):
        @pl.when(pl.program_id(1) == 0)
        def _():
            h = (x_ref[...] * mod_ref[0:1, :] + mod_ref[1:2, :]).astype(BF)
            hs[...] = h
            h_ref[...] = h

        o_ref[...] = (jnp.dot(hs[...], w_ref[...], preferred_element_type=F32) + b_ref[...]).astype(o_ref.dtype)

    if flat_out:
        o_spec = pl.BlockSpec((tm, ns), lambda i, j: (i, j))
        o_shape = jax.ShapeDtypeStruct((m, nb * ns), ACT_DTYPE)
    else:
        o_spec = pl.BlockSpec((None, tm, ns), lambda i, j: (j, i, 0))
        o_shape = jax.ShapeDtypeStruct((nb, m, ns), ACT_DTYPE)
    return _pcall(
        body, (x, mod, w8, bias8), name=name, grid=(m // tm, nb),
        in_specs=[pl.BlockSpec((tm, k), lambda i, j: (i, 0)),
                  pl.BlockSpec((2, k), lambda i, j: (0, 0)),
                  pl.BlockSpec((None, k, ns), lambda i, j: (j, 0, 0)),
                  pl.BlockSpec((None, 1, ns), lambda i, j: (j, 0, 0))],
        out_specs=[o_spec, pl.BlockSpec((tm, k), lambda i, j: (i, 0))],
        out_shape=[o_shape, jax.ShapeDtypeStruct((m, k), BF)],
        scratch_shapes=[pltpu.VMEM((tm, k), BF)], sem=("parallel", "arbitrary"), phases=phases)


def _seg_spec(tm, d, s):
    return pl.BlockSpec((tm, d), lambda i, s=s: (i, s))


def _prev_halo_spec(tm, d, s):
    hb = tm // HALO
    return pl.BlockSpec((HALO, d), lambda i, s=s: (jnp.maximum(i * hb - 1, 0), s))


def _next_halo_spec(tm, d, s, m):
    hb = tm // HALO
    last = m // HALO - 1
    return pl.BlockSpec((HALO, d), lambda i, s=s: (jnp.minimum((i + 1) * hb, last), s))


def _spatial_mix(wm_ref, src, dst, bias_ref, tm, d):
    for n in range(tm // GMLP_BLOCK):
        for g in range(d // GMLP_BLOCK):
            rs = slice(n * GMLP_BLOCK, (n + 1) * GMLP_BLOCK)
            cs = slice(g * GMLP_BLOCK, (g + 1) * GMLP_BLOCK)
            v = jnp.dot(wm_ref[g], src[rs, cs], preferred_element_type=F32)
            if bias_ref is not None:
                v = v + bias_ref[:, cs]
            dst[rs, cs] = v


def _mix_fwd(z, conv_a, lnv, wm, bias_full, *, name, tm=256, phases=()):
    m, d9 = z.shape
    d = d9 // 9
    tm = _row_tile(m, tm)
    grp = d // len(POOL_WINDOWS)

    def body(zb, zc, zx, zu, zv, zp, zc_h, zx_h, zp_h, ca_ref, lnv_ref, wm_ref, bias_ref,
             ua_ref, ub_ref, d_ref, ext, vn_s, mixed_s):
        i = pl.program_id(0)
        first = i == 0
        f32 = lambda r: r[...].astype(F32)
        pa = f32(zc) * f32(zx)
        ext[0:HALO, :] = jnp.where(first, 0.0, f32(zc_h) * f32(zx_h))
        ext[HALO:HALO + tm, :] = pa
        w = ca_ref[...]
        conv = w[0:1, :] * ext[pl.ds(HALO - 2, tm), :] + w[1:2, :] * ext[pl.ds(HALO - 1, tm), :] + w[2:3, :] * pa
        ua_ref[...] = (f32(zb) * conv).astype(BF)
        p = f32(zp)
        ext[0:HALO, :] = jnp.where(first, 0.0, f32(zp_h))
        ext[HALO:HALO + tm, :] = p
        t = (i * tm + lax.broadcasted_iota(jnp.int32, (tm, 1), 0) + 1).astype(F32)
        for k, win in enumerate(POOL_WINDOWS):
            cs = slice(k * grp, (k + 1) * grp)
            s = p[:, cs]
            for j in range(1, win):
                s = s + ext[pl.ds(HALO - j, tm), cs]
            d_ref[:, cs] = (s / jnp.minimum(t, float(win)) - p[:, cs]).astype(BF)
        gv = _gelu(f32(zv))
        vhat, _ = _ln_stats(gv)
        vn_s[...] = (vhat * lnv_ref[0:1, :] + lnv_ref[1:2, :]).astype(BF)
        _spatial_mix(wm_ref, vn_s, mixed_s, bias_ref, tm, d)
        ub_ref[...] = (_gelu(f32(zu)) * mixed_s[...]).astype(BF)

    full = lambda a: pl.BlockSpec(a.shape, lambda i: (0,) * a.ndim)
    out = jax.ShapeDtypeStruct((m, d), BF)
    o_spec = pl.BlockSpec((tm, d), lambda i: (i, 0))
    return _pcall(
        body, (z, z, z, z, z, z, z, z, z, conv_a, lnv, wm, bias_full), name=name, grid=(m // tm,),
        in_specs=[_seg_spec(tm, d, s) for s in range(6)] + [_prev_halo_spec(tm, d, s) for s in (1, 2, 5)]
        + [full(conv_a), full(lnv), full(wm), full(bias_full)],
        out_specs=[o_spec, o_spec, o_spec], out_shape=[out, out, out],
        scratch_shapes=[pltpu.VMEM((HALO + tm, d), F32), pltpu.VMEM((tm, d), BF), pltpu.VMEM((tm, d), F32)],
        sem=("arbitrary",), phases=phases)


def _pool_proj(dd, w_pool, *, dn, name, out_dtype=F32, tm=512):
    m, d = dd.shape
    ng, grp, _ = w_pool.shape
    tm = _row_tile(m, tm)
    return _matmul(
        dd, w_pool, dn=dn, grid=(m // tm, ng, 1), name=name,
        a_spec=pl.BlockSpec((tm, grp), lambda i, j, kk: (i, j)),
        b_spec=pl.BlockSpec((None, grp, grp), lambda i, j, kk: (j, 0, 0)),
        o_spec=pl.BlockSpec((tm, grp), lambda i, j, kk: (i, j)),
        out_shape=jax.ShapeDtypeStruct((m, d), out_dtype), acc_shape=(tm, grp))


def _merge(z, ya, yb, ycp, scale, *, name, tm=512):
    m, d = ya.shape
    tm = _row_tile(m, tm)

    def body(ga, gb, gc, ya_ref, yb_ref, yc_ref, sc_ref, o_ref):
        f32 = lambda r: r[...].astype(F32)
        o_ref[...] = (jax.nn.sigmoid(f32(ga)) * f32(ya_ref) + jax.nn.sigmoid(f32(gb)) * f32(yb_ref)
                      + jax.nn.sigmoid(f32(gc)) * (f32(yc_ref) * sc_ref[...])).astype(BF)

    row = pl.BlockSpec((tm, d), lambda i: (i, 0))
    return pl.pallas_call(
        body, name=name, grid=(m // tm,),
        in_specs=[_seg_spec(tm, d, 6), _seg_spec(tm, d, 7), _seg_spec(tm, d, 8), row, row, row,
                  pl.BlockSpec((1, d), lambda i: (0, 0))],
        out_specs=row, out_shape=jax.ShapeDtypeStruct((m, d), BF),
        compiler_params=_params(("parallel",)),
    )(z, z, z, ya, yb, ycp, scale)


def _resid_ln(xp, ys, vec, alpha, *, name, tm=512):
    m, d = xp.shape
    tm = _row_tile(m, tm)

    def body(xp_ref, ys_ref, v_ref, o_ref):
        xhat, _ = _ln_stats(alpha * xp_ref[...] + v_ref[0:1, :] * ys_ref[...])
        o_ref[...] = xhat * v_ref[1:2, :] + v_ref[2:3, :]

    row = pl.BlockSpec((tm, d), lambda i: (i, 0))
    return pl.pallas_call(
        body, name=name, grid=(m // tm,),
        in_specs=[row, row, pl.BlockSpec(vec.shape, lambda i: (0, 0))],
        out_specs=row, out_shape=jax.ShapeDtypeStruct((m, d), F32),
        compiler_params=_params(("parallel",)),
    )(xp, ys, vec)


def _ffn_fwd(up4, cw, cb, *, name, tm=512, phases=()):
    _, nj, m, fs = up4.shape
    tm = _row_tile(m, tm)
    hb = tm // HALO

    def body(up_ref, ah_ref, cw_ref, cb_ref, f_ref, ext):
        first = pl.program_id(1) == 0
        ext[0:HALO, :] = jnp.where(first, 0.0, ah_ref[...].astype(F32))
        ext[HALO:HALO + tm, :] = up_ref[0].astype(F32)
        w = cw_ref[...]
        w0, w1, w2, bias = w[0:1, :], w[1:2, :], w[2:3, :], cb_ref[...]
        rc = 16

        def step(c, carry):
            r0 = pl.multiple_of(c * rc, rc)
            win = ext[pl.ds(r0 + HALO - 8, rc + 8), :]
            a0, a1, a2 = win[8:8 + rc], pltpu.roll(win, 1, 0)[8:8 + rc], pltpu.roll(win, 2, 0)[8:8 + rc]
            ca = w0 * a2 + w1 * a1 + w2 * a0 + bias
            f_ref[pl.ds(r0, rc), :] = (_gelu(ca) * up_ref[1, pl.ds(r0, rc), :].astype(F32)).astype(BF)
            return carry

        lax.fori_loop(0, tm // rc, step, 0)

    return _pcall(
        body, (up4, up4, cw, cb), name=name, grid=(nj, m // tm),
        in_specs=[pl.BlockSpec((2, None, tm, fs), lambda j, i: (0, j, i, 0)),
                  pl.BlockSpec((None, None, HALO, fs), lambda j, i: (0, j, jnp.maximum(i * hb - 1, 0), 0)),
                  pl.BlockSpec((None, 3, fs), lambda j, i: (j, 0, 0)),
                  pl.BlockSpec((None, 1, fs), lambda j, i: (j, 0, 0))],
        out_specs=pl.BlockSpec((None, tm, fs), lambda j, i: (j, i, 0)),
        out_shape=jax.ShapeDtypeStruct((nj, m, fs), BF),
        scratch_shapes=[pltpu.VMEM((HALO + tm, fs), F32)], sem=("parallel", "arbitrary"), phases=phases)


def _down_proj(f4, wd4, *, name, tm=2048, phases=()):
    nj, m, fs = f4.shape
    d = wd4.shape[2]
    tm = _row_tile(m, tm)
    return _matmul(
        f4, wd4, dn=NN, grid=(m // tm, 1, nj), name=name,
        a_spec=pl.BlockSpec((None, tm, fs), lambda i, j, kk: (kk, i, 0)),
        b_spec=pl.BlockSpec((None, fs, d), lambda i, j, kk: (kk, 0, 0)),
        o_spec=pl.BlockSpec((tm, d), lambda i, j, kk: (i, 0)),
        out_shape=jax.ShapeDtypeStruct((m, d), F32), acc_shape=(tm, d), phases=phases)


def _loss_grad(y, tgt, *, name, tm=512):
    m, d = y.shape
    tm = _row_tile(m, tm)
    ni = m // tm

    def body(y_ref, t_ref, dy_ref, l_ref, acc):
        i = pl.program_id(0)
        e = y_ref[...] - t_ref[...]
        dy_ref[...] = e * (1.0 / d)
        part = jnp.sum((e * e).reshape(tm // 8, 8, d), axis=0)

        @pl.when(i == 0)
        def _():
            acc[...] = part

        @pl.when(i > 0)
        def _():
            acc[...] += part

        @pl.when(i == ni - 1)
        def _():
            l_ref[...] = jnp.full((8, 128), 0.5 / d, F32) * jnp.sum(acc[...])

    row = pl.BlockSpec((tm, d), lambda i: (i, 0))
    return pl.pallas_call(
        body, name=name, grid=(ni,), in_specs=[row, row],
        out_specs=[row, pl.BlockSpec((8, 128), lambda i: (0, 0))],
        out_shape=[jax.ShapeDtypeStruct((m, d), F32), jax.ShapeDtypeStruct((8, 128), F32)],
        scratch_shapes=[pltpu.VMEM((8, d), F32)],
        compiler_params=_params(("arbitrary",)),
    )(y, tgt)


def _resid_ln_bwd(dpart, dh, xmod, mvec, xp, ys, vec, alpha, *, name, tm=256, phases=()):
    m, d = dpart.shape
    tm = _row_tile(m, tm)
    has_dh = dh is not None
    has_ln = xp is not None

    def body(*refs):
        refs = list(refs)
        dpart_ref = refs.pop(0)
        if has_dh:
            dh_ref, xm_ref, mv_ref = refs.pop(0), refs.pop(0), refs.pop(0)
        if has_ln:
            xp_ref, ys_ref, v_ref = refs.pop(0), refs.pop(0), refs.pop(0)
            dys_ref, dxp_ref, red_ref = refs
        else:
            dx_ref, red_ref = refs
        i = pl.program_id(0)
        dtot = dpart_ref[...]
        rows = [jnp.zeros((1, d), F32)] * 5
        if has_dh:
            dhv = dh_ref[...]
            dtot = dtot + dhv * mv_ref[...]
            rows[0] = _rowsum(dhv * xm_ref[...])
            rows[1] = _rowsum(dhv)
        if has_ln:
            ys = ys_ref[...]
            gt = v_ref[0:1, :]
            xhat, rstd = _ln_stats(alpha * xp_ref[...] + gt * ys)
            rows[2] = _rowsum(dtot * xhat)
            rows[3] = _rowsum(dtot)
            dr = _ln_bwd(dtot, xhat, rstd, v_ref[1:2, :])
            rows[4] = _rowsum(dr * ys)
            dys_ref[...] = (dr * gt).astype(BF)
            dxp_ref[...] = alpha * dr
        else:
            dx_ref[...] = dtot
        red = jnp.concatenate(rows + [jnp.zeros((3, d), F32)], axis=0)

        @pl.when(i == 0)
        def _():
            red_ref[...] = red

        @pl.when(i > 0)
        def _():
            red_ref[...] += red

    row = pl.BlockSpec((tm, d), lambda i: (i, 0))
    vrow = lambda a: pl.BlockSpec(a.shape, lambda i: (0, 0))
    args, specs = [dpart], [row]
    if has_dh:
        args += [dh, xmod, mvec]
        specs += [row, row, vrow(mvec)]
    if has_ln:
        args += [xp, ys, vec]
        specs += [row, row, vrow(vec)]
        out_specs = [row, row, pl.BlockSpec((8, d), lambda i: (0, 0))]
        out_shape = [jax.ShapeDtypeStruct((m, d), BF), jax.ShapeDtypeStruct((m, d), F32),
                     jax.ShapeDtypeStruct((8, d), F32)]
    else:
        out_specs = [row, pl.BlockSpec((8, d), lambda i: (0, 0))]
        out_shape = [jax.ShapeDtypeStruct((m, d), F32), jax.ShapeDtypeStruct((8, d), F32)]
    return _pcall(body, args, name=name, grid=(m // tm,), in_specs=specs, out_specs=out_specs, out_shape=out_shape,
                  sem=("arbitrary",), phases=phases)


def _down_bwd(dy, wd4, *, name, tm=2048, phases=()):
    m, d = dy.shape
    nj, fs, _ = wd4.shape
    tm = _row_tile(m, tm)
    return _matmul(
        dy, wd4, dn=NT, grid=(m // tm, nj, 1), name=name,
        a_spec=pl.BlockSpec((tm, d), lambda i, j, kk: (i, 0)),
        b_spec=pl.BlockSpec((None, fs, d), lambda i, j, kk: (j, 0, 0)),
        o_spec=pl.BlockSpec((None, tm, fs), lambda i, j, kk: (j, i, 0)),
        out_shape=jax.ShapeDtypeStruct((nj, m, fs), ACT_DTYPE), acc_shape=(tm, fs), phases=phases)


def _tn_shards_lhs(f4, dy, *, name, tk=4096, phases=()):
    nj, m, fs = f4.shape
    d = dy.shape[1]
    tk = _row_tile(m, tk)
    return _matmul(
        f4, dy, dn=TN, grid=(nj, 1, m // tk), name=name,
        a_spec=pl.BlockSpec((None, tk, fs), lambda i, j, kk: (i, kk, 0)),
        b_spec=pl.BlockSpec((tk, d), lambda i, j, kk: (kk, 0)),
        o_spec=pl.BlockSpec((None, fs, d), lambda i, j, kk: (i, 0, 0)),
        out_shape=jax.ShapeDtypeStruct((nj, fs, d), GRAD_DTYPE), acc_shape=(fs, d), phases=phases)


def _tn_shards_rhs(h, d8, *, name, tk=2048, phases=()):
    m, k = h.shape
    nb, _, ns = d8.shape
    tk = _row_tile(m, tk)
    return _matmul(
        h, d8, dn=TN, grid=(nb, 1, m // tk), name=name,
        a_spec=pl.BlockSpec((tk, k), lambda i, j, kk: (kk, 0)),
        b_spec=pl.BlockSpec((None, tk, ns), lambda i, j, kk: (i, kk, 0)),
        o_spec=pl.BlockSpec((None, k, ns), lambda i, j, kk: (i, 0, 0)),
        out_shape=jax.ShapeDtypeStruct((nb, k, ns), GRAD_DTYPE), acc_shape=(k, ns), phases=phases)


def _tn_cols_rhs(h, dz, nb, *, name, tk=4096, phases=()):
    m, k = h.shape
    ns = dz.shape[1] // nb
    tk = _row_tile(m, tk)
    return _matmul(
        h, dz, dn=TN, grid=(nb, 1, m // tk), name=name,
        a_spec=pl.BlockSpec((tk, k), lambda i, j, kk: (kk, 0)),
        b_spec=pl.BlockSpec((tk, ns), lambda i, j, kk: (kk, i)),
        o_spec=pl.BlockSpec((None, k, ns), lambda i, j, kk: (i, 0, 0)),
        out_shape=jax.ShapeDtypeStruct((nb, k, ns), GRAD_DTYPE), acc_shape=(k, ns), phases=phases)


def _nt_shards(d8, w8, *, name, tm=2048, phases=()):
    nb, m, ns = d8.shape
    k = w8.shape[1]
    tm = _row_tile(m, tm)
    return _matmul(
        d8, w8, dn=NT, grid=(m // tm, 1, nb), name=name,
        a_spec=pl.BlockSpec((None, tm, ns), lambda i, j, kk: (kk, i, 0)),
        b_spec=pl.BlockSpec((None, k, ns), lambda i, j, kk: (kk, 0, 0)),
        o_spec=pl.BlockSpec((tm, k), lambda i, j, kk: (i, 0)),
        out_shape=jax.ShapeDtypeStruct((m, k), F32), acc_shape=(tm, k), phases=phases)


def _nt_cols(dz, w8, *, name, tm=2048, phases=(), tiles=None, into=None):
    m = dz.shape[0]
    nb, k, ns = w8.shape
    tm = _row_tile(m, tm)
    first, count = tiles if tiles is not None else (0, m // tm)
    return _matmul(
        dz, w8, dn=NT, grid=(count, 1, nb), name=name,
        a_spec=pl.BlockSpec((tm, ns), lambda i, j, kk: (i + first, kk)),
        b_spec=pl.BlockSpec((None, k, ns), lambda i, j, kk: (kk, 0, 0)),
        o_spec=pl.BlockSpec((tm, k), lambda i, j, kk: (i + first, 0)),
        out_shape=jax.ShapeDtypeStruct((m, k), F32), acc_shape=(tm, k), phases=phases, into=into)


def _tn_pool(dd, dyc, ng, *, name, tk=2048):
    m, d = dd.shape
    grp = d // ng
    tk = _row_tile(m, tk)
    return _matmul(
        dd, dyc, dn=TN, grid=(ng, 1, m // tk), name=name,
        a_spec=pl.BlockSpec((tk, grp), lambda i, j, kk: (kk, i)),
        b_spec=pl.BlockSpec((tk, grp), lambda i, j, kk: (kk, i)),
        o_spec=pl.BlockSpec((None, grp, grp), lambda i, j, kk: (i, 0, 0)),
        out_shape=jax.ShapeDtypeStruct((ng, grp, grp), GRAD_DTYPE), acc_shape=(grp, grp))


def _ffn_bwd(up4, df4, cw, cb, *, name, tm=256, phases=()):
    _, nj, m, fs = up4.shape
    tm = _row_tile(m, tm)
    hb = tm // HALO
    ni = m // tm
    last_hb = m // HALO - 1
    ext_rows = tm + 8

    rc = 16
    assert tm % rc == 0

    def body(up_ref, ap_ref, un_ref, df_ref, dfn_ref, cw_ref, cb_ref, dup_ref, red_ref, ext, dca_s, racc):
        i = pl.program_id(1)
        ext[0:HALO, :] = jnp.where(i == 0, 0.0, ap_ref[...].astype(F32))
        ext[HALO:HALO + tm, :] = up_ref[0].astype(F32)
        ext[HALO + tm:2 * HALO + tm, :] = un_ref[0].astype(F32)
        racc[...] = jnp.zeros_like(racc)
        w = cw_ref[...]
        w0, w1, w2, bias = w[0:1, :], w[1:2, :], w[2:3, :], cb_ref[...]

        def conv_taps(win, n):
            return (win[8:8 + n], pltpu.roll(win, 1, 0)[8:8 + n], pltpu.roll(win, 2, 0)[8:8 + n])

        def fold(v):
            return v[0:8] + v[8:16]

        def add_red(k, v8):
            racc[8 * k:8 * k + 8, :] += v8

        def first_pass(c, carry):
            r0 = pl.multiple_of(c * rc, rc)
            a0, a1, a2 = conv_taps(ext[pl.ds(r0 + HALO - 8, rc + 8), :], rc)
            act, dact = _gelu_parts(w0 * a2 + w1 * a1 + w2 * a0 + bias)
            dfc = df_ref[pl.ds(r0, rc), :].astype(F32)
            dca = dfc * up_ref[1, pl.ds(r0, rc), :].astype(F32) * dact
            dup_g = dfc * act
            dca_s[pl.ds(r0, rc), :] = dca
            dup_ref[1, pl.ds(r0, rc), :] = dup_g.astype(BF)
            for k, v in enumerate((dca * a2, dca * a1, dca * a0, dca, dup_g)):
                add_red(k if k < 4 else 5, fold(v))
            return carry

        lax.fori_loop(0, tm // rc, first_pass, 0)
        a0, a1, a2 = conv_taps(ext[HALO + tm - 8:HALO + tm + 8, :], 8)
        _, dact = _gelu_parts(w0 * a2 + w1 * a1 + w2 * a0 + bias)
        after = dfn_ref[...].astype(F32)[0:8, :] * un_ref[1].astype(F32)[0:8, :] * dact
        dca_s[tm:tm + 8, :] = jnp.where(i < ni - 1, after, 0.0)
        dca_s[tm + 8:tm + 16, :] = jnp.zeros((8, fs), F32)

        def second_pass(c, carry):
            r0 = pl.multiple_of(c * rc, rc)
            win = dca_s[pl.ds(r0, rc + 8), :]
            up1, up2 = pltpu.roll(win, rc + 7, 0)[0:rc], pltpu.roll(win, rc + 6, 0)[0:rc]
            dup_a = w2 * win[0:rc] + w1 * up1 + w0 * up2
            dup_ref[0, pl.ds(r0, rc), :] = dup_a.astype(BF)
            add_red(4, fold(dup_a))
            return carry

        lax.fori_loop(0, tm // rc, second_pass, 0)
        red = jnp.concatenate([_rowsum(racc[8 * k:8 * k + 8, :]) for k in range(6)] + [jnp.zeros((2, fs), F32)],
                              axis=0)

        @pl.when(i == 0)
        def _():
            red_ref[...] = red

        @pl.when(i > 0)
        def _():
            red_ref[...] += red

    nxt = lambda j, i: jnp.minimum((i + 1) * hb, last_hb)
    return _pcall(
        body, (up4, up4, up4, df4, df4, cw, cb), name=name, grid=(nj, ni), sem=("parallel", "arbitrary"), phases=phases,
        in_specs=[pl.BlockSpec((2, None, tm, fs), lambda j, i: (0, j, i, 0)),
                  pl.BlockSpec((None, None, HALO, fs), lambda j, i: (0, j, jnp.maximum(i * hb - 1, 0), 0)),
                  pl.BlockSpec((2, None, HALO, fs), lambda j, i: (0, j, nxt(j, i), 0)),
                  pl.BlockSpec((None, tm, fs), lambda j, i: (j, i, 0)),
                  pl.BlockSpec((None, HALO, fs), lambda j, i: (j, nxt(j, i), 0)),
                  pl.BlockSpec((None, 3, fs), lambda j, i: (j, 0, 0)),
                  pl.BlockSpec((None, 1, fs), lambda j, i: (j, 0, 0))],
        out_specs=[pl.BlockSpec((2, None, tm, fs), lambda j, i: (0, j, i, 0)),
                   pl.BlockSpec((None, 8, fs), lambda j, i: (j, 0, 0))],
        out_shape=[jax.ShapeDtypeStruct((2, nj, m, fs), BF), jax.ShapeDtypeStruct((nj, 8, fs), F32)],
        scratch_shapes=[pltpu.VMEM((2 * HALO + tm, fs), F32), pltpu.VMEM((tm + 16, fs), F32),
                        pltpu.VMEM((48, fs), F32)])


def _gate_bwd(dm, z, ya, yb, ycp, scale, *, name, tm=256):
    m, d = dm.shape
    tm = _row_tile(m, tm)

    def body(dm_ref, ga, gb, gc, ya_ref, yb_ref, yc_ref, sc_ref, dya_ref, dyb_ref, dyc_ref, dz_ref, red_ref):
        i = pl.program_id(0)
        f32 = lambda r: r[...].astype(F32)
        dmv = f32(dm_ref)
        sa, sb, sc = jax.nn.sigmoid(f32(ga)), jax.nn.sigmoid(f32(gb)), jax.nn.sigmoid(f32(gc))
        scale_v = sc_ref[...]
        ycp_v = f32(yc_ref)
        dya_ref[...] = (dmv * sa).astype(BF)
        dyb_ref[...] = (dmv * sb).astype(BF)
        dyc = dmv * sc
        dyc_ref[...] = (dyc * scale_v).astype(BF)
        dga = dmv * f32(ya_ref) * (sa * (1.0 - sa))
        dgb = dmv * f32(yb_ref) * (sb * (1.0 - sb))
        dgc = dmv * (ycp_v * scale_v) * (sc * (1.0 - sc))
        dz_ref[:, 0:d] = dga.astype(BF)
        dz_ref[:, d:2 * d] = dgb.astype(BF)
        dz_ref[:, 2 * d:3 * d] = dgc.astype(BF)
        red = jnp.concatenate([_rowsum(dyc * ycp_v), _rowsum(dga), _rowsum(dgb), _rowsum(dgc),
                               jnp.zeros((4, d), F32)], axis=0)

        @pl.when(i == 0)
        def _():
            red_ref[...] = red

        @pl.when(i > 0)
        def _():
            red_ref[...] += red

    row = pl.BlockSpec((tm, d), lambda i: (i, 0))
    obf = jax.ShapeDtypeStruct((m, d), BF)
    return pl.pallas_call(
        body, name=name, grid=(m // tm,),
        in_specs=[row, _seg_spec(tm, d, 6), _seg_spec(tm, d, 7), _seg_spec(tm, d, 8), row, row, row,
                  pl.BlockSpec((1, d), lambda i: (0, 0))],
        out_specs=[row, row, row, pl.BlockSpec((tm, 3 * d), lambda i: (i, 2)), pl.BlockSpec((8, d), lambda i: (0, 0))],
        out_shape=[obf, obf, obf, jax.ShapeDtypeStruct((m, 9 * d), BF), jax.ShapeDtypeStruct((8, d), F32)],
        compiler_params=_params(("arbitrary",)),
    )(dm, z, z, z, ya, yb, ycp, scale)


def _mix_bwd(dz, dua, dub, ddd, z, conv_a, lnv, wm, wmt, bias_full, mask, *, name, tm=128, phases=()):
    m, d = dua.shape
    tm = _row_tile(m, tm)
    ni = m // tm
    grp = d // len(POOL_WINDOWS)
    ng = d // GMLP_BLOCK
    ext_rows = tm + 8

    def body(dz_in, dua_ref, dub_ref, dd_ref, zb, zc, zx, zu, zv, zp, zc_h, zx_h, dua_n, zb_n, dd_n,
             ca_ref, lnv_ref, wm_ref, wmt_ref, bias_ref, mask_ref,
             dz_ref, red_ref, dws_ref, dbs_ref, ext, sh_s, vn_s, mixed_s, dmx_s, dvn_s, dbs_acc):
        del dz_in
        i = pl.program_id(0)
        rows = []
        f32 = lambda r: r[...].astype(F32)
        zbv, zcv, zxv = f32(zb), f32(zc), f32(zx)
        pa = zcv * zxv
        ext[0:HALO, :] = jnp.where(i == 0, 0.0, f32(zc_h) * f32(zx_h))
        ext[HALO:HALO + tm, :] = pa
        w = ca_ref[...]
        w0, w1, w2 = w[0:1, :], w[1:2, :], w[2:3, :]
        p1 = ext[pl.ds(HALO - 1, tm), :]
        p2 = ext[pl.ds(HALO - 2, tm), :]
        conv = w0 * p2 + w1 * p1 + w2 * pa
        duav = f32(dua_ref)
        dzb = duav * conv
        dca = duav * zbv
        dca_n = jnp.where(i < ni - 1, f32(dua_n)[0:8, :] * f32(zb_n)[0:8, :], 0.0)
        sh_s[0:tm, :] = dca
        sh_s[tm:tm + 8, :] = dca_n
        dpa = w2 * dca + w1 * sh_s[pl.ds(1, tm), :] + w0 * sh_s[pl.ds(2, tm), :]
        dzc = dpa * zxv
        dzx = dpa * zcv
        dz_ref[:, 0:d] = dzb.astype(BF)
        dz_ref[:, d:2 * d] = dzc.astype(BF)
        dz_ref[:, 2 * d:3 * d] = dzx.astype(BF)
        rows += [_rowsum(dzb), _rowsum(dzc), _rowsum(dzx)]
        dconv = [_rowsum(dca * p2), _rowsum(dca * p1), _rowsum(dca * pa)]
        zuv, zvv = f32(zu), f32(zv)
        gu, dgu_dz = _gelu_parts(zuv)
        gv, dgv_dz = _gelu_parts(zvv)
        vhat, rstd = _ln_stats(gv)
        gain = lnv_ref[0:1, :]
        vn_s[...] = (vhat * gain + lnv_ref[1:2, :]).astype(BF)
        _spatial_mix(wm_ref, vn_s, mixed_s, bias_ref, tm, d)
        dubv = f32(dub_ref)
        dzu = dubv * mixed_s[...] * dgu_dz
        dmixed = dubv * gu
        dmx_s[...] = dmixed.astype(BF)
        _spatial_mix(wmt_ref, dmx_s, dvn_s, None, tm, d)
        dvn = dvn_s[...]
        dzv = _ln_bwd(dvn, vhat, rstd, gain) * dgv_dz
        dz_ref[:, 3 * d:4 * d] = dzu.astype(BF)
        dz_ref[:, 4 * d:5 * d] = dzv.astype(BF)
        rows += [_rowsum(dzu), _rowsum(dzv)]
        dlnv = [_rowsum(dvn * vhat), _rowsum(dvn)]
        dbs_part = dmixed[0:GMLP_BLOCK, :]
        for n in range(1, tm // GMLP_BLOCK):
            dbs_part = dbs_part + dmixed[n * GMLP_BLOCK:(n + 1) * GMLP_BLOCK, :]
        ddv = f32(dd_ref)
        t = (i * tm + lax.broadcasted_iota(jnp.int32, (ext_rows + 8, 1), 0) + 1).astype(F32)
        dde = jnp.concatenate([ddv, jnp.where(i < ni - 1, f32(dd_n), 0.0)], axis=0)
        for k, win in enumerate(POOL_WINDOWS):
            cs = slice(k * grp, (k + 1) * grp)
            ext[0:tm + HALO, cs] = dde[:, cs] / jnp.minimum(t, float(win))
        dzp_parts = []
        for k, win in enumerate(POOL_WINDOWS):
            cs = slice(k * grp, (k + 1) * grp)
            s = ext[0:tm, cs]
            for j in range(1, win):
                s = s + ext[pl.ds(j, tm), cs]
            dzp_parts.append(s - ddv[:, cs])
        dzp = jnp.concatenate(dzp_parts, axis=1)
        dz_ref[:, 5 * d:6 * d] = dzp.astype(BF)
        rows += [_rowsum(dzp)]
        red = jnp.concatenate(rows + dconv + dlnv + [jnp.zeros((5, d), F32)], axis=0)

        @pl.when(i == 0)
        def _():
            red_ref[...] = red
            dbs_acc[...] = dbs_part
            dws_ref[...] = jnp.zeros_like(dws_ref)

        @pl.when(i > 0)
        def _():
            red_ref[...] += red
            dbs_acc[...] += dbs_part

        for n in range(tm // GMLP_BLOCK):
            for g in range(ng):
                rs = slice(n * GMLP_BLOCK, (n + 1) * GMLP_BLOCK)
                cs = slice(g * GMLP_BLOCK, (g + 1) * GMLP_BLOCK)
                dws_ref[g] += mask_ref[...] * lax.dot_general(
                    dmx_s[rs, cs], vn_s[rs, cs], (NT, ((), ())), preferred_element_type=F32)

        @pl.when(i == ni - 1)
        def _():
            lane = lax.broadcasted_iota(jnp.int32, (GMLP_BLOCK, GMLP_BLOCK), 1)
            out = jnp.zeros((GMLP_BLOCK, GMLP_BLOCK), F32)
            for g in range(ng):
                sg = jnp.sum(dbs_acc[:, g * GMLP_BLOCK:(g + 1) * GMLP_BLOCK], axis=1, keepdims=True)
                out = out + jnp.where(lane == g, sg, 0.0)
            dbs_ref[...] = out

    row = pl.BlockSpec((tm, d), lambda i: (i, 0))
    full = lambda a: pl.BlockSpec(a.shape, lambda i: (0,) * a.ndim)
    hb = tm // HALO
    last_hb = m // HALO - 1
    nrow = pl.BlockSpec((HALO, d), lambda i: (jnp.minimum((i + 1) * hb, last_hb), 0))
    return _pcall(
        body, (dz, dua, dub, ddd, z, z, z, z, z, z, z, z, dua, z, ddd, conv_a, lnv, wm, wmt, bias_full, mask),
        name=name, grid=(ni,), sem=("arbitrary",), aliases={0: 0}, phases=phases,
        in_specs=[pl.BlockSpec(memory_space=pl.ANY), row, row, row]
        + [_seg_spec(tm, d, s) for s in range(6)]
        + [_prev_halo_spec(tm, d, 1), _prev_halo_spec(tm, d, 2), nrow, _next_halo_spec(tm, d, 0, m), nrow]
        + [full(conv_a), full(lnv), full(wm), full(wmt), full(bias_full), full(mask)],
        out_specs=[pl.BlockSpec((tm, 6 * d), lambda i: (i, 0)), pl.BlockSpec((16, d), lambda i: (0, 0)),
                   full(wm), pl.BlockSpec((GMLP_BLOCK, GMLP_BLOCK), lambda i: (0, 0))],
        out_shape=[jax.ShapeDtypeStruct(dz.shape, BF), jax.ShapeDtypeStruct((16, d), F32),
                   jax.ShapeDtypeStruct(wm.shape, F32), jax.ShapeDtypeStruct((GMLP_BLOCK, GMLP_BLOCK), F32)],
        scratch_shapes=[pltpu.VMEM((2 * HALO + tm, d), F32), pltpu.VMEM((tm + 8, d), F32),
                        pltpu.VMEM((tm, d), BF), pltpu.VMEM((tm, d), F32), pltpu.VMEM((tm, d), BF),
                        pltpu.VMEM((tm, d), F32), pltpu.VMEM((GMLP_BLOCK, d), F32)])


REST = ("w_a_out", "w_b_out", "w_pool", "w_o", "w_up", "w_down")


def _remote(src, dst, ssem, rsem, k, to):
    return pltpu.make_async_remote_copy(src_ref=src, dst_ref=dst, send_sem=ssem.at[k], recv_sem=rsem.at[k],
                                        device_id=to, device_id_type=MESH)


def _gather_phase1(shards, rows=None, onto=None):
    n = len(shards)
    rows = rows or [None] * n
    onto = onto or [None] * n
    extra = [a for a in range(n) if onto[a] is not None]

    def build(ins, outs, ssem, rsem, lsem):
        x, y, c, chips = _place()
        me = 4 * x + 2 * y + c

        def src(a):
            return ins[a] if rows[a] is None else ins[a].at[:, pl.ds(*rows[a])]

        def dst(a, dev):
            return outs[a].at[:, dev] if rows[a] is None else outs[a].at[:, dev, pl.ds(*rows[a])]

        local = [pltpu.make_async_copy(src(a), dst(a, me), lsem.at[a]) for a in range(n)]
        sends, recvs = [], []
        for j, (cx, cy) in enumerate(chips):
            for a in range(n):
                sends.append(_remote(src(a), dst(a, me), ssem, rsem, 4 * a + 1 + j, (cx, cy, c)))
                recvs.append(_remote(src(a), dst(a, 4 * cx + 2 * cy + c), ssem, rsem, 4 * a + 1 + j, (cx, cy, c)))
        for a in range(n):
            sends.append(_remote(src(a), dst(a, me), ssem, rsem, 4 * a, (x, y, 1 - c)))
            recvs.append(_remote(src(a), dst(a, 4 * x + 2 * y + 1 - c), ssem, rsem, 4 * a, (x, y, 1 - c)))
        return dict(start=local + sends, recv=recvs, send=sends, local=local)

    outs = [jax.ShapeDtypeStruct((s.shape[0], N_DEV) + s.shape[1:], s.dtype) for s in shards]
    return _Phase(list(shards) + [onto[a] for a in extra], outs, {n + k: a for k, a in enumerate(extra)},
                  4 * n, n, build)


def _gather_phase2(fulls):
    n = len(fulls)

    def build(ins, outs, ssem, rsem, lsem):
        x, y, c, chips = _place()
        sends, recvs = [], []
        for j, (cx, cy) in enumerate(chips):
            for a in range(n):
                mine, theirs = 4 * cx + 2 * cy + c, 4 * cx + 2 * cy + 1 - c
                sends.append(_remote(ins[a].at[:, mine], outs[a].at[:, mine], ssem, rsem, 3 * a + j, (x, y, 1 - c)))
                recvs.append(_remote(ins[a].at[:, theirs], outs[a].at[:, theirs], ssem, rsem, 3 * a + j, (x, y, 1 - c)))
        return dict(start=sends, recv=recvs, send=sends, local=[])

    outs = [jax.ShapeDtypeStruct(f.shape, f.dtype) for f in fulls]
    return _Phase(fulls, outs, {a: a for a in range(n)}, 3 * n, 0, build)


def _pair_phase(grads):
    n = len(grads)

    def build(ins, outs, ssem, rsem, lsem):
        x, y, c, _ = _place()
        cps = [_remote(ins[a].at[:, 2 * q + (1 - c)], outs[a].at[q], ssem, rsem, 4 * a + q, (x, y, 1 - c))
               for a in range(n) for q in range(4)]
        return dict(start=cps, recv=cps, send=cps, local=[])

    outs = [jax.ShapeDtypeStruct((4, g.shape[0]) + g.shape[2:], g.dtype) for g in grads]
    return _Phase(grads, outs, {}, 4 * n, 0, build)


def _chip_phase(bufs, accs, l, depth):
    n = len(bufs)
    has = accs is not None

    def build(ins, outs, ssem, rsem, lsem):
        x, y, c, chips = _place()
        myq = 2 * x + y
        local, sends, recvs = [], [], []
        for a in range(n):
            local.append(pltpu.make_async_copy(ins[a].at[myq], outs[a].at[myq, l], lsem.at[a]))
            for j, (cx, cy) in enumerate(chips):
                q = 2 * cx + cy
                sends.append(_remote(ins[a].at[q], outs[a].at[myq, l], ssem, rsem, 3 * a + j, (cx, cy, c)))
                recvs.append(_remote(ins[a].at[q], outs[a].at[q, l], ssem, rsem, 3 * a + j, (cx, cy, c)))
        return dict(start=local + sends, recv=recvs, send=sends, local=local)

    outs = [jax.ShapeDtypeStruct((4, depth) + b.shape[1:], b.dtype) for b in bufs]
    return _Phase(list(bufs) + (list(accs) if has else []), outs, {n + a: a for a in range(n)} if has else {},
                  3 * n, n, build)


def _grad_chunks(n, g):
    if n == "w_pool":
        return g.reshape(g.shape[0], N_DEV, g.shape[1] // N_DEV, g.shape[2])
    if n in ("w_in", "w_up"):
        return g[None]
    return g.reshape(1, N_DEV, -1, g.shape[-1])


class _ReduceScatter:
    def __init__(self, depth, cidx):
        self.depth, self.cidx, self.acc, self.count = depth, cidx, {}, 0
        self.small_gathered = None
        self.presummed = None

    def pair(self, names, grads):
        return _pair_phase([_grad_chunks(n, grads[n]) for n in names])

    def sums(self, names, grads, phase):
        out = []
        for n, r1 in zip(names, phase.results):
            out.append(_pair_sum(_grad_chunks(n, grads[n]), r1, self.cidx, name="rs_sum_%d" % self.count))
            self.count += 1
        return out

    def chip(self, names, bufs, l):
        accs = [self.acc[n] for n in names] if names[0] in self.acc else None
        return _chip_phase(bufs, accs, l, self.depth)

    def done(self, names, phase):
        for n, r in zip(names, phase.results):
            self.acc[n] = r


def _rest_views(fulls, d):
    a_out, b_out, pool, o, up, down = fulls
    grp = d // len(POOL_WINDOWS)
    return dict(w_a_out=a_out.reshape(d, d), w_b_out=b_out.reshape(d, d), w_o=o.reshape(d, d),
                w_pool=pool.reshape(len(POOL_WINDOWS), grp, grp), w_up8=up[0],
                wd4=down.reshape(N_DEV // 2, -1, d))


class _GatherPlan:
    def __init__(self):
        self.jobs, self.part, self.full = [], {}, {}

    def add(self, key, shard, first, second, rows=None, onto=None):
        self.jobs.append((key, shard, first, second, rows, onto))

    def phases(self, name):
        j1 = [j for j in self.jobs if j[2] == name]
        j2 = [j for j in self.jobs if j[3] == name]
        tagged = []
        if j1:
            onto = [self.part[j[5]] if j[5] else None for j in j1]
            tagged.append((self.part, j1, _gather_phase1([j[1] for j in j1], [j[4] for j in j1], onto)))
        if j2:
            tagged.append((self.full, j2, _gather_phase2([self.part[j[0]] for j in j2])))
        return tagged

    @staticmethod
    def collect(tagged):
        for store, jobs, phase in tagged:
            for j, r in zip(jobs, phase.results):
                store[j[0]] = r


def _layer_fwd(x, w, alpha, tag, plan=None):
    d = x.shape[1]
    grp = d // len(POOL_WINDOWS)

    def carried(kernel, *args, name, **kw):
        tagged = plan.phases(name) if plan else []
        out = kernel(*args, name=name, phases=[t[2] for t in tagged], **kw)
        _GatherPlan.collect(tagged)
        return out

    def weight(n, shape):
        return plan.full[n + tag].reshape(shape) if plan else w[n]

    z, h = carried(_mod_matmul, x, w["mod1"], w["w_in8"], w["b_in8"], flat_out=True, name="in_proj" + tag)
    ua, ub, dd = carried(_mix_fwd, z, w["conv_a"], w["lnv"], w["wm"], w["bias_full"], name="mix_fwd" + tag)
    w["w_a_out"], w["w_b_out"], w["w_o"] = (weight(n, (d, d)) for n in ("w_a_out", "w_b_out", "w_o"))
    w["w_pool"] = weight("w_pool", (len(POOL_WINDOWS), grp, grp))
    w["w_up8"] = weight("w_up8", (N_DEV, d, -1))
    ya = _mm_rows(ua, w["w_a_out"], dn=NN, name="a_out" + tag, out_dtype=ACT_DTYPE)
    yb = _mm_rows(ub, w["w_b_out"], dn=NN, name="b_out" + tag, out_dtype=ACT_DTYPE)
    ycp = _pool_proj(dd, w["w_pool"], dn=NN, name="pool_proj" + tag, out_dtype=ACT_DTYPE)
    merged = _merge(z, ya, yb, ycp, w["pool_scale"], name="merge" + tag)
    o = _mm_rows(merged, w["w_o"], dn=NN, name="o_proj" + tag)
    x1 = _resid_ln(x, o, w["ln1"], alpha, name="ln1" + tag)
    up8, h2 = carried(_mod_matmul, x1, w["mod2"], w["w_up8"], w["b_up8"], flat_out=False, name="up_proj" + tag)
    up4 = up8.reshape((2, up8.shape[0] // 2) + up8.shape[1:])
    f4 = carried(_ffn_fwd, up4, w["cw"], w["cb"], name="ffn_fwd" + tag)
    w["wd4"] = weight("wd4", (N_DEV // 2, -1, d))
    y2 = carried(_down_proj, f4, w["wd4"], name="down_proj" + tag)
    x2 = _resid_ln(x1, y2, w["ln2"], alpha, name="ln2" + tag)
    saved = dict(x=x, z=z, h=h, ua=ua, ub=ub, dd=dd, ya=ya, yb=yb, ycp=ycp, merged=merged, o=o, x1=x1,
                 up4=up4, h2=h2, f4=f4, y2=y2)
    return x2, saved


def _layer_bwd(dpart, dh_above, xmod_above, m_above, w, s, alpha, tag, l=0, above=None, rs=None, upper_reds=()):
    first, rest = ("w_in",), REST
    ph = lambda p: [p] if p is not None else ()
    pre = rs.presummed if rs is not None else None
    r1a = rs.pair(first, above) if above and not pre else None
    dy2, dx1p, red2 = _resid_ln_bwd(dpart, dh_above, xmod_above, m_above, s["x1"], s["y2"], w["ln2"], alpha,
                                    name="ln2_bwd" + tag, phases=ph(r1a))
    r1b = rs.pair(rest, above) if above and not pre else None
    df4 = _down_bwd(dy2, w["wd4"], name="down_bwd" + tag, phases=ph(r1b))
    gw_down4 = _tn_shards_lhs(s["f4"], dy2, name="gw_down" + tag)
    r3a = r3b = r3c = None
    if above:
        if pre:
            bufs, rs.presummed = pre, None
        else:
            bufs = dict(zip(first + rest, rs.sums(first, above, r1a) + rs.sums(rest, above, r1b)))
        light = tuple(n for n in rest if n != "w_up")
        r3a = rs.chip(first, [bufs[n] for n in first], l + 1)
    dup4, redf = _ffn_bwd(s["up4"], df4, w["cw"], w["cb"], name="ffn_bwd" + tag, phases=ph(r3a))
    dup8 = dup4.reshape((dup4.shape[0] * dup4.shape[1],) + dup4.shape[2:])
    if above:
        rs.done(first, r3a)
        r3b = rs.chip(("w_up",), [bufs["w_up"]], l + 1)
    gw_up8 = _tn_shards_lhs(dup8, s["h2"], name="gw_up" + tag, phases=ph(r3b))
    if above:
        rs.done(("w_up",), r3b)
        r3c = rs.chip(light, [bufs[n] for n in light], l + 1)
    own = rs is not None and l == 0
    big = dict(w_up=gw_up8, w_down=gw_down4)
    early = ("w_down", "w_up")
    o1 = rs.pair(early, big) if own else None
    dh2 = _nt_shards(dup8, w["w_up8"], name="up_bwd" + tag, phases=list(ph(r3c)) + list(ph(o1)))
    if above:
        rs.done(light, r3c)
    if own:
        sb_o = rs.sums(early, big, o1)
    do, dxp, red1 = _resid_ln_bwd(dx1p, dh2, s["x1"], w["mod2"][0:1], s["x"], s["o"], w["ln1"], alpha,
                                  name="ln1_bwd" + tag)
    dm = _mm_rows(do, w["w_o"], dn=NT, name="o_bwd" + tag, out_dtype=ACT_DTYPE)
    big["w_o"] = _mm_tn(s["merged"], do, name="gw_o" + tag)
    dya, dyb, dyc, dz, redg = _gate_bwd(dm, s["z"], s["ya"], s["yb"], s["ycp"], w["pool_scale"], name="gate_bwd" + tag)
    dua = _mm_rows(dya, w["w_a_out"], dn=NT, name="a_out_bwd" + tag, out_dtype=ACT_DTYPE)
    dub = _mm_rows(dyb, w["w_b_out"], dn=NT, name="b_out_bwd" + tag, out_dtype=ACT_DTYPE)
    ddd = _pool_proj(dyc, w["w_pool"], dn=NT, name="pool_bwd" + tag, out_dtype=ACT_DTYPE)
    big["w_a_out"] = _mm_tn(s["ua"], dya, name="gw_a_out" + tag)
    big["w_b_out"] = _mm_tn(s["ub"], dyb, name="gw_b_out" + tag)
    big["w_pool"] = _tn_pool(s["dd"], dyc, w["w_pool"].shape[0], name="gw_pool" + tag)
    o3 = rs.chip(early, sb_o, l) if own else None
    dz, redm, dws, dbs = _mix_bwd(dz, dua, dub, ddd, s["z"], w["conv_a"], w["lnv"], w["wm"], w["wmt"],
                                  w["bias_full"], w["mask"], name="mix_bwd" + tag, phases=ph(o3))
    reds = dict(red2=red2, redf=redf, red1=red1, redg=redg, redm=redm, dws=dws, dbs=dbs)
    mid = ("w_o", "w_a_out", "w_b_out", "w_pool")
    o1b = sg1 = None
    if own:
        rs.done(early, o3)
        o1b = rs.pair(mid, big)
        sg1 = _gather_phase1([_small_payload([reds] + list(upper_reds))])
    big["w_in"] = _tn_cols_rhs(s["h"], dz, w["w_in8"].shape[0], name="gw_in" + tag,
                               phases=[o1b, sg1] if own else ())
    pending = None
    if own:
        sb_m = rs.sums(mid, big, o1b)
        o1c, o3b, sg2 = rs.pair(first, big), rs.chip(mid, sb_m, l), _gather_phase2(sg1.results)
        split_tm = 1024
        n_tiles = dz.shape[0] // _row_tile(dz.shape[0], split_tm)
        dh = _nt_cols(dz, w["w_in8"], name="in_bwd" + tag + "_a", phases=[o1c, o3b, sg2], tm=split_tm, tiles=(0, 1))
        rs.done(mid, o3b)
        rs.small_gathered = sg2.results[0]
        o3c = rs.chip(first, rs.sums(first, big, o1c), l)
        if n_tiles > 1:
            dh = _nt_cols(dz, w["w_in8"], name="in_bwd" + tag + "_b", phases=[o3c], tm=split_tm,
                          tiles=(1, n_tiles - 1), into=dh)
            rs.done(first, o3c)
        else:
            pending = (first, o3c)
    elif rs is not None:
        pa, pb = rs.pair(first, big), rs.pair(rest, big)
        dh = _nt_cols(dz, w["w_in8"], name="in_bwd" + tag, phases=[pa, pb])
        rs.presummed = dict(zip(first + rest, rs.sums(first, big, pa) + rs.sums(rest, big, pb)))
    else:
        dh = _nt_cols(dz, w["w_in8"], name="in_bwd" + tag)
    return dxp, dh, big, reds, pending


def _local_step(x, tgt, ws, alpha, plan=None, rs=None):
    depth = len(ws)
    saved = []
    y = x
    for l in range(depth):
        if plan and l > 0:
            ws[l]["w_in8"] = plan.full["w_in8_l%d" % l][0]
        y, s = _layer_fwd(y, ws[l], alpha, "_l%d" % l, plan)
        saved.append(s)
    dpart, loss_blk = _loss_grad(y, tgt, name="loss_grad")
    dh = xmod = mvec = above = pending = None
    bigs, reds = [None] * depth, [None] * depth
    for l in reversed(range(depth)):
        dpart, dh, bigs[l], reds[l], pending = _layer_bwd(dpart, dh, xmod, mvec, ws[l], saved[l], alpha, "_l%d" % l,
                                                          l, above if rs else None, rs, reds[l + 1:])
        xmod, mvec, above = saved[l]["x"], ws[l]["mod1"][0:1], bigs[l]
    grad_x, red0 = _resid_ln_bwd(dpart, dh, xmod, mvec, None, None, None, alpha, name="in_bwd_tail",
                                 phases=[pending[1]] if pending else ())
    if pending:
        rs.done(*pending)
    d_ada = []
    for l in range(depth):
        below = red0 if l == 0 else reds[l - 1]["red2"]
        r1, r2 = reds[l]["red1"], reds[l]["red2"]
        d_ada.append(jnp.stack([below[1], below[0], r1[4], r1[1], r1[0], r2[4]]))
    return loss_blk, grad_x, bigs, reds, jnp.stack(d_ada)


def _small_grads(r):
    redm, redg, redf = r["redm"], r["redg"], r["redf"]
    ng = r["dws"].shape[0]
    return dict(
        b_in=jnp.concatenate([redm[0:6], redg[1:4]], axis=0).reshape(-1),
        conv_a=redm[6:9], ln_v_g=redm[9], ln_v_b=redm[10],
        w_spatial=r["dws"], b_spatial=r["dbs"][:, :ng].T,
        pool_scale=redg[0], ln1_g=r["red1"][2], ln1_b=r["red1"][3],
        b_up=jnp.concatenate([redf[:, 4, :].reshape(-1), redf[:, 5, :].reshape(-1)]),
        conv_ffn=jnp.transpose(redf[:, 0:3, :], (1, 0, 2)).reshape(3, -1), conv_ffn_b=redf[:, 3, :].reshape(-1),
        ln2_g=r["red2"][2], ln2_b=r["red2"][3])


def _small_payload(reds):
    smalls = [_small_grads(r) for r in reds]
    order = SMALL_REPLICATED + SMALL_SHARDED
    flat = jnp.concatenate([smalls[l][n].reshape(-1) for n in order for l in range(len(reds))])
    return _as_rows(flat)[None]


def _layer_weights(l, ada, conv_a, conv_ffn, p):
    sh1, sc1, gt1, sh2, sc2, gt2 = (ada[l, k][None, :] for k in range(6))
    nb = N_DEV
    fs = p["b_up"].shape[1] // nb
    nj = nb // 2
    pos = jnp.arange(GMLP_BLOCK)
    allowed = (pos[None, :] // CHUNK) <= (pos[:, None] // CHUNK)
    wmask = jnp.where(allowed[None], p["w_spatial"][l], 0.0)
    return dict(
        mod1=jnp.concatenate([1.0 + sc1, sh1]), mod2=jnp.concatenate([1.0 + sc2, sh2]),
        ln1=jnp.concatenate([gt1, p["ln1_g"][l][None], p["ln1_b"][l][None]]),
        ln2=jnp.concatenate([gt2, p["ln2_g"][l][None], p["ln2_b"][l][None]]),
        b_in8=p["b_in"][l].reshape(N_DEV, 1, -1), b_up8=p["b_up"][l].reshape(nb, 1, fs),
        conv_a=conv_a[l], lnv=jnp.stack([p["ln_v_g"][l], p["ln_v_b"][l]]),
        wm=wmask.astype(BF), wmt=jnp.transpose(wmask, (0, 2, 1)).astype(BF),
        bias_full=jnp.repeat(p["b_spatial"][l].T, GMLP_BLOCK, axis=1), mask=allowed.astype(F32),
        pool_scale=p["pool_scale"][l][None],
        cw=jnp.transpose(conv_ffn[l].reshape(3, nj, fs), (1, 0, 2)), cb=p["conv_ffn_b"][l].reshape(nj, 1, fs))


ANY = pl.BlockSpec(memory_space=pl.ANY)


def _place():
    x, y, c = lax.axis_index("x"), lax.axis_index("y"), lax.axis_index("c")
    chips = [(1 - x, y), (x, 1 - y), (1 - x, 1 - y)]
    return x, y, c, chips


def _allgather_vmem(xs, *, name):
    r, cdim = xs.shape

    def body(x_ref, out_ref, send_sems, recv_sems, local_sem):
        x, y, c, chips = _place()
        me, sibling = (x, y, c), (x, y, 1 - c)

        def rows(px, py, pc):
            return out_ref.at[pl.ds((4 * px + 2 * py + pc) * r, r), :]

        def copy(k, block, to, src=None):
            return pltpu.make_async_remote_copy(
                src_ref=rows(*block) if src is None else src, dst_ref=rows(*block),
                send_sem=send_sems.at[k], recv_sem=recv_sems.at[k], device_id=to, device_id_type=MESH)

        mine = pltpu.make_async_copy(x_ref, rows(*me), local_sem)
        mine.start()
        first = [copy(0, me, sibling, src=x_ref)]
        first += [copy(1 + j, me, (*chip, c), src=x_ref) for j, chip in enumerate(chips)]
        for cp in first:
            cp.start()
        passed = [copy(4 + j, (*chip, c), sibling) for j, chip in enumerate(chips)]
        for j, chip in enumerate(chips):
            copy(1 + j, (*chip, c), me).wait_recv()
            passed[j].start()
        copy(0, sibling, me).wait_recv()
        for j, chip in enumerate(chips):
            copy(4 + j, (*chip, 1 - c), me).wait_recv()
        for cp in first + passed:
            cp.wait_send()
        mine.wait()

    return pl.pallas_call(
        body, name=name, out_shape=jax.ShapeDtypeStruct((N_DEV * r, cdim), xs.dtype),
        in_specs=[pl.BlockSpec(memory_space=pltpu.VMEM)], out_specs=pl.BlockSpec(memory_space=pltpu.VMEM),
        scratch_shapes=[pltpu.SemaphoreType.DMA((7,)), pltpu.SemaphoreType.DMA((7,)), pltpu.SemaphoreType.DMA],
        compiler_params=_params(),
    )(xs)


def _gather_weights(shards, *, name):
    n = len(shards)

    def body(*refs):
        ins, outs = refs[:n], refs[n:2 * n]
        send_sems, recv_sems, local_sems = refs[2 * n:]
        x, y, c, chips = _place()
        me, sibling = (x, y, c), (x, y, 1 - c)

        def slot(a, px, py, pc):
            return outs[a].at[:, 4 * px + 2 * py + pc]

        def copy(a, k, block, to, src=None):
            return pltpu.make_async_remote_copy(
                src_ref=slot(a, *block) if src is None else src, dst_ref=slot(a, *block),
                send_sem=send_sems.at[7 * a + k], recv_sem=recv_sems.at[7 * a + k], device_id=to,
                device_id_type=MESH)

        mine = [pltpu.make_async_copy(ins[a], slot(a, *me), local_sems.at[a]) for a in range(n)]
        for cp in mine:
            cp.start()
        first = []
        for j, chip in enumerate(chips):
            first += [copy(a, 1 + j, me, (*chip, c), src=ins[a]) for a in range(n)]
        first += [copy(a, 0, me, sibling, src=ins[a]) for a in range(n)]
        for cp in first:
            cp.start()
        passed = []
        for j, chip in enumerate(chips):
            for a in range(n):
                copy(a, 1 + j, (*chip, c), me).wait_recv()
                fwd = copy(a, 4 + j, (*chip, c), sibling)
                fwd.start()
                passed.append(fwd)
        for a in range(n):
            copy(a, 0, sibling, me).wait_recv()
        for j, chip in enumerate(chips):
            for a in range(n):
                copy(a, 4 + j, (*chip, 1 - c), me).wait_recv()
        for cp in first + passed:
            cp.wait_send()
        for cp in mine:
            cp.wait()

    out_shape = [jax.ShapeDtypeStruct((s.shape[0], N_DEV) + s.shape[1:], s.dtype) for s in shards]
    return pl.pallas_call(
        body, name=name, out_shape=out_shape, in_specs=[ANY] * n, out_specs=[ANY] * n,
        scratch_shapes=[pltpu.SemaphoreType.DMA((7 * n,)), pltpu.SemaphoreType.DMA((7 * n,)),
                        pltpu.SemaphoreType.DMA((n,))],
        compiler_params=_params(),
    )(*shards)


def _pick_tile(r, cap):
    best = None
    for t in range(8, min(r, cap) + 1, 8):
        if r % t == 0:
            best = t
    return best if best is not None else r


def _pair_sum(g, r1, cidx, *, name):
    p, _, r, cdim = g.shape
    tr = _pick_tile(r, 256)

    def body(c_ref, g_ref, r_ref, o_ref):
        del c_ref
        o_ref[...] = (g_ref[...].astype(F32) + r_ref[...].astype(F32)).astype(BF)

    grid_spec = pltpu.PrefetchScalarGridSpec(
        num_scalar_prefetch=1, grid=(4, r // tr),
        in_specs=[pl.BlockSpec((p, None, tr, cdim), lambda q, i, c: (0, 2 * q + c[0], i, 0)),
                  pl.BlockSpec((None, p, tr, cdim), lambda q, i, c: (q, 0, i, 0))],
        out_specs=pl.BlockSpec((None, p, tr, cdim), lambda q, i, c: (q, 0, i, 0)))
    return pl.pallas_call(
        body, name=name, grid_spec=grid_spec, out_shape=jax.ShapeDtypeStruct((4, p, r, cdim), BF),
        compiler_params=_params(("arbitrary", "arbitrary")),
    )(cidx, g, r1)


def _ada_fwd(c_all, w_ada, *, name):
    depth, d, ns = w_ada.shape
    nb = c_all.shape[0]

    def body(c_ref, w_ref, o_ref):
        cv = c_ref[...]
        act = cv * jax.nn.sigmoid(cv)
        o_ref[...] = jnp.dot(act, w_ref[...], preferred_element_type=F32, precision=lax.Precision.HIGHEST)

    return pl.pallas_call(
        body, name=name, grid=(depth,),
        in_specs=[pl.BlockSpec((nb, d), lambda l: (0, 0)), pl.BlockSpec((None, d, ns), lambda l: (l, 0, 0))],
        out_specs=pl.BlockSpec((None, nb, ns), lambda l: (l, 0, 0)),
        out_shape=jax.ShapeDtypeStruct((depth, nb, ns), F32), compiler_params=_params(("parallel",)),
    )(c_all, w_ada)


def _ada_bwd(ct, dmine, dall, *, name):
    depth, nb, ns = dmine.shape
    d = ct.shape[0]

    def body(ct_ref, dm_ref, da_ref, gw_ref, gb_ref):
        cv = ct_ref[...]
        act = cv * jax.nn.sigmoid(cv)
        gw_ref[...] = jnp.dot(act, dm_ref[...], preferred_element_type=F32, precision=lax.Precision.HIGHEST)
        s = da_ref[0]
        for b in range(1, nb):
            s = s + da_ref[b]
        gb_ref[...] = s

    return pl.pallas_call(
        body, name=name, grid=(depth,),
        in_specs=[pl.BlockSpec((d, nb), lambda l: (0, 0)), pl.BlockSpec((None, nb, ns), lambda l: (l, 0, 0)),
                  pl.BlockSpec(dall.shape, lambda l: (0, 0, 0))],
        out_specs=[pl.BlockSpec((None, d, ns), lambda l: (l, 0, 0)), pl.BlockSpec(dall.shape[1:], lambda l: (0, 0))],
        out_shape=[jax.ShapeDtypeStruct((depth, d, ns), F32), jax.ShapeDtypeStruct(dall.shape[1:], F32)],
        compiler_params=_params(("arbitrary",)),
    )(ct, dmine, dall)


def _sum_parts(parts, *, name):
    p, r, cdim = parts.shape
    tr = _pick_tile(r, 512)

    def body(p_ref, o_ref):
        s = p_ref[0]
        for k in range(1, p):
            s = s + p_ref[k]
        o_ref[...] = s

    return pl.pallas_call(
        body, name=name, grid=(r // tr,),
        in_specs=[pl.BlockSpec((p, tr, cdim), lambda i: (0, i, 0))], out_specs=pl.BlockSpec((tr, cdim), lambda i: (i, 0)),
        out_shape=jax.ShapeDtypeStruct((r, cdim), F32), compiler_params=_params(("parallel",)),
    )(parts)


def _adamw(parts, w, m, v, *, name):
    p, depth, r, cdim = parts.shape
    tr = _pick_tile(r, 256)

    def body(p_ref, w_ref, m_ref, v_ref, g_out, d_out, m_out, v_out):
        g = p_ref[0].astype(F32)
        for k in range(1, p):
            g = g + p_ref[k].astype(F32)
        m2 = ADAM_B1 * m_ref[...] + (1.0 - ADAM_B1) * g
        v2 = ADAM_B2 * v_ref[...] + (1.0 - ADAM_B2) * (g * g)
        m_hat = m2 / (1.0 - ADAM_B1 ** ADAM_STEP)
        v_hat = v2 / (1.0 - ADAM_B2 ** ADAM_STEP)
        g_out[...] = g
        d_out[...] = -ADAM_LR * (m_hat / (jnp.sqrt(v_hat) + ADAM_EPS) + ADAM_WD * w_ref[...])
        m_out[...] = m2
        v_out[...] = v2

    blk = pl.BlockSpec((None, tr, cdim), lambda l, i: (l, i, 0))
    out = jax.ShapeDtypeStruct((depth, r, cdim), F32)
    return pl.pallas_call(
        body, name=name, grid=(depth, r // tr),
        in_specs=[pl.BlockSpec((p, None, tr, cdim), lambda l, i: (0, l, i, 0)), blk, blk, blk],
        out_specs=[blk, blk, blk, blk], out_shape=[out, out, out, out],
        compiler_params=_params(("parallel", "parallel")),
    )(parts, w, m, v)


BIG = ("w_in", "w_a_out", "w_b_out", "w_pool", "w_o", "w_up", "w_down")
SMALL_REPLICATED = ("b_in", "ln_v_g", "ln_v_b", "w_spatial", "b_spatial", "pool_scale", "ln1_g", "ln1_b", "b_up",
                    "conv_ffn_b", "ln2_g", "ln2_b")
SMALL_SHARDED = ("conv_a", "conv_ffn")
WEIGHTS = ("w_ada", "b_ada", "w_in", "b_in", "conv_a", "w_a_out", "ln_v_g", "ln_v_b", "w_spatial", "b_spatial",
           "w_b_out", "w_pool", "pool_scale", "w_o", "ln1_g", "ln1_b", "w_up", "b_up", "conv_ffn", "conv_ffn_b",
           "w_down", "ln2_g", "ln2_b")
LANES = 128


def _as_rows(flat, mult=8):
    n = flat.shape[0]
    pad = (-n) % (LANES * mult)
    if pad:
        flat = jnp.concatenate([flat, jnp.zeros((pad,), flat.dtype)])
    return flat.reshape(-1, LANES)


def _shard3(a):
    return a.reshape((-1,) + a.shape[-2:])


def kernel(x, c, w_ada, b_ada, w_in, b_in, conv_a, w_a_out, ln_v_g, ln_v_b, w_spatial, b_spatial, w_b_out, w_pool, pool_scale, w_o, ln1_g, ln1_b, w_up, b_up, conv_ffn, conv_ffn_b, w_down, ln2_g, ln2_b, loss_target, m_w_ada, m_b_ada, m_w_in, m_b_in, m_conv_a, m_w_a_out, m_ln_v_g, m_ln_v_b, m_w_spatial, m_b_spatial, m_w_b_out, m_w_pool, m_pool_scale, m_w_o, m_ln1_g, m_ln1_b, m_w_up, m_b_up, m_conv_ffn, m_conv_ffn_b, m_w_down, m_ln2_g, m_ln2_b, v_w_ada, v_b_ada, v_w_in, v_b_in, v_conv_a, v_w_a_out, v_ln_v_g, v_ln_v_b, v_w_spatial, v_b_spatial, v_w_b_out, v_w_pool, v_pool_scale, v_w_o, v_ln1_g, v_ln1_b, v_w_up, v_b_up, v_conv_ffn, v_conv_ffn_b, v_w_down, v_ln2_g, v_ln2_b):
    p = dict(locals())
    depth, d = w_in.shape[0], w_in.shape[1]
    alpha = (2 * depth) ** 0.25
    me = 4 * lax.axis_index("x") + 2 * lax.axis_index("y") + lax.axis_index("c")
    cidx = lax.axis_index("c").astype(jnp.int32).reshape(1)

    n_ca, n_cf = conv_a.size, conv_ffn.size
    packed = _as_rows(jnp.concatenate([c.reshape(-1), conv_a.reshape(-1), conv_ffn.reshape(-1)]))
    got = _allgather_vmem(packed, name="gather_cond").reshape(N_DEV, -1)
    c_all = got[:, :d]
    ct = c_all.T
    conv_a_full = jnp.transpose(got[:, d:d + n_ca].reshape((N_DEV,) + conv_a.shape), (1, 2, 0, 3)).reshape(depth, 3, -1)
    conv_ffn_full = jnp.transpose(got[:, d + n_ca:d + n_ca + n_cf].reshape((N_DEV,) + conv_ffn.shape),
                                  (1, 2, 0, 3)).reshape(depth, 3, -1)

    ns_ada = w_ada.shape[2]
    ada_part = _ada_fwd(c_all, w_ada, name="ada_fwd")
    ada_all = _allgather_vmem(_as_rows(ada_part.reshape(-1)), name="gather_ada")
    ada_all = ada_all.reshape(N_DEV, depth, N_DEV, ns_ada)
    ada_mine = lax.dynamic_index_in_dim(ada_all, me, axis=2, keepdims=False)
    ada = jnp.transpose(ada_mine, (1, 0, 2)).reshape(depth, -1) + b_ada
    ada = ada.reshape(depth, 6, d)

    shards = [{n: _shard3(p[n][l].astype(BF)) for n in BIG} for l in range(depth)]
    ws = [_layer_weights(l, ada, conv_a_full, conv_ffn_full, p) for l in range(depth)]
    ws[0]["w_in8"] = _gather_weights([shards[0]["w_in"]], name="gather_w_in0")[0][0]
    plan = _GatherPlan()
    four = ("w_a_out", "w_b_out", "w_pool", "w_o")
    for l in range(depth):
        t, prev, sh = "_l%d" % l, "_l%d" % (l - 1), shards[l]
        if l == 0:
            for n in four:
                plan.add(n + t, sh[n], "in_proj" + t, "mix_fwd" + t)
            plan.add("w_up8" + t, sh["w_up"], "in_proj" + t, "mix_fwd" + t)
            plan.add("wd4" + t, sh["w_down"], "mix_fwd" + t, "up_proj" + t)
        else:
            half = sh["w_in"].shape[1] // 2
            plan.add("w_in8_top" + t, sh["w_in"], "up_proj" + prev, None, rows=(0, half))
            plan.add("w_in8" + t, sh["w_in"], "ffn_fwd" + prev, "down_proj" + prev, rows=(half, half),
                     onto="w_in8_top" + t)
            for n in four:
                plan.add(n + t, sh[n], "down_proj" + prev, "in_proj" + t)
            plan.add("w_up8" + t, sh["w_up"], "in_proj" + t, "mix_fwd" + t)
            plan.add("wd4" + t, sh["w_down"], "in_proj" + t, "mix_fwd" + t)

    rs = _ReduceScatter(depth, cidx)
    loss_blk, grad_x, bigs, reds, d_ada = _local_step(x[0], loss_target[0], ws, alpha, plan, rs)
    loss = lax.psum(loss_blk[0, 0], ("x", "y", "c"))

    dada_all = _allgather_vmem(_as_rows(d_ada.reshape(-1)), name="gather_dada")
    dada_all = dada_all.reshape(N_DEV, -1, LANES)
    dflat = dada_all.reshape(N_DEV, depth, 6 * d)
    dmine = lax.dynamic_slice_in_dim(dflat, me * ns_ada, ns_ada, axis=2)
    gw_ada, gb_rows = _ada_bwd(ct, jnp.transpose(dmine, (1, 0, 2)), dada_all, name="ada_bwd")
    gb_ada = gb_rows.reshape(-1)[:depth * 6 * d].reshape(depth, 6 * d)

    out = {}
    for n in BIG:
        parts = rs.acc[n]
        view = (lambda a: jnp.swapaxes(a, 1, 2)) if n == "w_up" else (lambda a: a)
        shard_shape = view(p[n]).shape
        w3 = view(p[n]).reshape(depth, -1, shard_shape[-1])
        parts4 = parts.reshape((4,) + w3.shape)
        res = _adamw(parts4, w3, view(p["m_" + n]).reshape(w3.shape), view(p["v_" + n]).reshape(w3.shape),
                     name="adamw_" + n)
        out[n] = [view(r.reshape(shard_shape)) for r in res]
    out["w_ada"] = _adamw(gw_ada[None], w_ada, m_w_ada, v_w_ada, name="adamw_w_ada")

    order = SMALL_REPLICATED + SMALL_SHARDED
    n_rep = sum(p[n].size for n in SMALL_REPLICATED)
    n_pay = n_rep + N_DEV * (conv_a.size + conv_ffn.size)
    gsum = _sum_parts(rs.small_gathered.reshape(N_DEV, -1, LANES), name="sum_small").reshape(-1)[:n_pay]
    ga_full = gsum[n_rep:n_rep + depth * 3 * d].reshape(depth, 3, d)
    gf_full = gsum[n_rep + depth * 3 * d:].reshape(depth, 3, -1)
    ca_w, cf_w = conv_a.shape[2], conv_ffn.shape[2]
    g_ca = lax.dynamic_slice_in_dim(ga_full, me * ca_w, ca_w, axis=2)
    g_cf = lax.dynamic_slice_in_dim(gf_full, me * cf_w, cf_w, axis=2)
    names = ("b_ada",) + order
    gflat = _as_rows(jnp.concatenate([gb_ada.reshape(-1), gsum[:n_rep], g_ca.reshape(-1), g_cf.reshape(-1)]))
    pack = lambda pre: _as_rows(jnp.concatenate([p[pre + n].reshape(-1) for n in names]))
    res = _adamw(gflat[None, None], pack("")[None], pack("m_")[None], pack("v_")[None], name="adamw_small")
    off = 0
    for n in names:
        size = p[n].size
        out[n] = [r.reshape(-1)[off:off + size].reshape(p[n].shape) for r in res]
        off += size

    return (loss, grad_x[None]) + tuple(out[n][k] for k in range(4) for n in WEIGHTS)
```

```python
import functools

import jax
import jax.numpy as jnp
from jax import lax
from jax.experimental import pallas as pl
from jax.experimental.pallas import tpu as pltpu

F32 = jnp.float32
BF = jnp.bfloat16
MESH = pl.DeviceIdType.MESH

LN_EPS = 1e-5
POOL_WINDOWS = (2, 4, 8, 16)
GMLP_BLOCK = 128
CHUNK = 64
HALO = 16
ADAM_LR, ADAM_B1, ADAM_B2, ADAM_EPS, ADAM_WD, ADAM_STEP = 0.001, 0.9, 0.999, 1e-08, 0.01, 10
N_DEV = 8
VMEM_LIMIT = 56 * 1024 * 1024

GRAD_DTYPE = BF
ACT_DTYPE = BF

NN = ((1,), (0,))
NT = ((1,), (1,))
TN = ((0,), (0,))


def _params(sem=None, vmem=VMEM_LIMIT, **kw):
    if sem is not None:
        kw["dimension_semantics"] = sem
    return pltpu.CompilerParams(vmem_limit_bytes=vmem, **kw)


class _Phase:
    def __init__(self, ins, out_shapes, aliases, n_remote, n_local, build):
        self.ins, self.out_shapes, self.aliases = list(ins), list(out_shapes), dict(aliases)
        self.n_remote, self.n_local, self.build = n_remote, n_local, build
        self.results = None


def _pcall(body, args, *, name, grid, in_specs, out_specs, out_shape, scratch_shapes=(), sem=None, aliases=None,
           phases=()):
    aliases = dict(aliases or {})
    if not phases:
        return pl.pallas_call(
            body, name=name, grid=grid, in_specs=list(in_specs), out_specs=out_specs, out_shape=out_shape,
            scratch_shapes=list(scratch_shapes), input_output_aliases=aliases, compiler_params=_params(sem),
        )(*args)
    single = not isinstance(out_shape, (list, tuple))
    o_specs = [out_specs] if single else list(out_specs)
    o_shapes = [out_shape] if single else list(out_shape)
    n_in, n_out, n_scr = len(args), len(o_shapes), len(scratch_shapes)
    ex_args, ex_out, sems = [], [], []
    for ph in phases:
        for src, dst in ph.aliases.items():
            aliases[n_in + len(ex_args) + src] = n_out + len(ex_out) + dst
        ex_args += ph.ins
        ex_out += ph.out_shapes
        sems += [pltpu.SemaphoreType.DMA((max(ph.n_remote, 1),)), pltpu.SemaphoreType.DMA((max(ph.n_remote, 1),)),
                 pltpu.SemaphoreType.DMA((max(ph.n_local, 1),))]

    def wrapped(*refs):
        pos = n_in
        ph_in = []
        for ph in phases:
            ph_in.append(refs[pos:pos + len(ph.ins)])
            pos += len(ph.ins)
        base_out = refs[pos:pos + n_out]
        pos += n_out
        ph_out = []
        for ph in phases:
            ph_out.append(refs[pos:pos + len(ph.out_shapes)])
            pos += len(ph.out_shapes)
        base_scr = refs[pos:pos + n_scr]
        ph_sems = refs[pos + n_scr:]
        first = last = None
        for ax, n in enumerate(grid):
            pid = pl.program_id(ax)
            first = (pid == 0) if first is None else first & (pid == 0)
            last = (pid == n - 1) if last is None else last & (pid == n - 1)

        def ops(k):
            return phases[k].build(ph_in[k], ph_out[k], *ph_sems[3 * k:3 * k + 3])

        @pl.when(first)
        def _():
            for k in range(len(phases)):
                for cp in ops(k)["start"]:
                    cp.start()

        body(*refs[:n_in], *base_out, *base_scr)

        @pl.when(last)
        def _():
            for k in range(len(phases)):
                o = ops(k)
                for cp in o["recv"]:
                    cp.wait_recv()
                for cp in o["send"]:
                    cp.wait_send()
                for cp in o["local"]:
                    cp.wait()

    hbm = pl.BlockSpec(memory_space=pl.ANY)
    res = pl.pallas_call(
        wrapped, name=name, grid=grid, in_specs=list(in_specs) + [hbm] * len(ex_args),
        out_specs=o_specs + [hbm] * len(ex_out), out_shape=o_shapes + ex_out,
        scratch_shapes=list(scratch_shapes) + sems, input_output_aliases=aliases,
        compiler_params=_params(("arbitrary",) * len(grid)),
    )(*args, *ex_args)
    pos = n_out
    for ph in phases:
        ph.results = list(res[pos:pos + len(ph.out_shapes)])
        pos += len(ph.out_shapes)
    return res[0] if single else list(res[:n_out])


def _gelu_parts(x):
    k = 0.7978845608028654
    x2 = x * x
    t = jnp.tanh(k * (x + 0.044715 * (x2 * x)))
    cdf = 0.5 * (1.0 + t)
    dcdf = 0.5 * (1.0 - t * t) * (k * (1.0 + 3.0 * 0.044715 * x2))
    return x * cdf, cdf + x * dcdf


def _gelu(x):
    t = jnp.tanh(0.7978845608028654 * (x + 0.044715 * (x * x * x)))
    return x * (0.5 * (1.0 + t))


def _rowsum(v):
    return jnp.sum(v, axis=0, keepdims=True)


def _ln_stats(r):
    mu = jnp.mean(r, axis=-1, keepdims=True)
    xc = r - mu
    var = jnp.mean(xc * xc, axis=-1, keepdims=True)
    rstd = lax.rsqrt(var + LN_EPS)
    return xc * rstd, rstd


def _ln_bwd(dy, xhat, rstd, gain):
    dxh = dy * gain
    m1 = jnp.mean(dxh, axis=-1, keepdims=True)
    m2 = jnp.mean(dxh * xhat, axis=-1, keepdims=True)
    return rstd * (dxh - m1 - xhat * m2)


def _matmul(a, b, *, dn, grid, a_spec, b_spec, o_spec, out_shape, acc_shape, name, phases=(), into=None):
    nk = grid[2]
    direct = out_shape.dtype == F32

    def body(a_ref, b_ref, *rest):
        o_ref, scratch = (rest[1], rest[2:]) if into is not None else (rest[0], rest[1:])
        prod = lax.dot_general(a_ref[...], b_ref[...], (dn, ((), ())), preferred_element_type=F32)
        if nk == 1:
            o_ref[...] = prod.astype(o_ref.dtype)
            return
        acc = o_ref if direct else scratch[0]
        k = pl.program_id(2)

        @pl.when(k == 0)
        def _():
            acc[...] = prod

        @pl.when(k > 0)
        def _():
            acc[...] += prod

        if not direct:
            @pl.when(k == nk - 1)
            def _():
                o_ref[...] = acc[...].astype(o_ref.dtype)

    scratch = [] if (direct or nk == 1) else [pltpu.VMEM(acc_shape, F32)]
    args, in_specs, aliases = (a, b), [a_spec, b_spec], None
    if into is not None:
        args, in_specs, aliases = (a, b, into), in_specs + [pl.BlockSpec(memory_space=pl.ANY)], {2: 0}
    return _pcall(body, args, name=name, grid=grid, in_specs=in_specs, out_specs=o_spec,
                  out_shape=out_shape, scratch_shapes=scratch, sem=("parallel", "parallel", "arbitrary"),
                  aliases=aliases, phases=phases)


def _row_tile(m, want):
    t = min(m, want)
    assert m % t == 0
    return t


def _mm_rows(a, w, *, dn, name, out_dtype=F32, tm=2048):
    m, k = a.shape
    n = w.shape[1] if dn == NN else w.shape[0]
    tm = _row_tile(m, tm)
    return _matmul(
        a, w, dn=dn, grid=(m // tm, 1, 1), name=name,
        a_spec=pl.BlockSpec((tm, k), lambda i, j, kk: (i, 0)),
        b_spec=pl.BlockSpec(w.shape, lambda i, j, kk: (0, 0)),
        o_spec=pl.BlockSpec((tm, n), lambda i, j, kk: (i, 0)),
        out_shape=jax.ShapeDtypeStruct((m, n), out_dtype), acc_shape=(tm, n))


def _mm_tn(a, b, *, name, tk=4096):
    m, ka = a.shape
    n = b.shape[1]
    tk = _row_tile(m, tk)
    return _matmul(
        a, b, dn=TN, grid=(1, 1, m // tk), name=name,
        a_spec=pl.BlockSpec((tk, ka), lambda i, j, kk: (kk, 0)),
        b_spec=pl.BlockSpec((tk, n), lambda i, j, kk: (kk, 0)),
        o_spec=pl.BlockSpec((ka, n), lambda i, j, kk: (0, 0)),
        out_shape=jax.ShapeDtypeStruct((ka, n), GRAD_DTYPE), acc_shape=(ka, n))


def _mod_matmul(x, mod, w8, bias8, *, flat_out, name, tm=2048, phases=()):
    m, k = x.shape
    nb, _, ns = w8.shape
    tm = _row_tile(m, tm)

    def body(x_ref, mod_ref, w_ref, b_ref, o_ref, h_ref, hs):
        @pl.when(pl.program_id(1) == 0)
        def _():
            h = (x_ref[...] * mod_ref[0:1, :] + mod_ref[1:2, :]).astype(BF)
            hs[...] = h
            h_ref[...] = h

        o_ref[...] = (jnp.dot(hs[...], w_ref[...], preferred_element_type=F32) + b_ref[...]).astype(o_ref.dtype)

    if flat_out:
        o_spec = pl.BlockSpec((tm, ns), lambda i, j: (i, j))
        o_shape = jax.ShapeDtypeStruct((m, nb * ns), ACT_DTYPE)
    else:
        o_spec = pl.BlockSpec((None, tm, ns), lambda i, j: (j, i, 0))
        o_shape = jax.ShapeDtypeStruct((nb, m, ns), ACT_DTYPE)
    return _pcall(
        body, (x, mod, w8, bias8), name=name, grid=(m // tm, nb),
        in_specs=[pl.BlockSpec((tm, k), lambda i, j: (i, 0)),
                  pl.BlockSpec((2, k), lambda i, j: (0, 0)),
                  pl.BlockSpec((None, k, ns), lambda i, j: (j, 0, 0)),
                  pl.BlockSpec((None, 1, ns), lambda i, j: (j, 0, 0))],
        out_specs=[o_spec, pl.BlockSpec((tm, k), lambda i, j: (i, 0))],
        out_shape=[o_shape, jax.ShapeDtypeStruct((m, k), BF)],
        scratch_shapes=[pltpu.VMEM((tm, k), BF)], sem=("parallel", "arbitrary"), phases=phases)


def _seg_spec(tm, d, s):
    return pl.BlockSpec((tm, d), lambda i, s=s: (i, s))


def _prev_halo_spec(tm, d, s):
    hb = tm // HALO
    return pl.BlockSpec((HALO, d), lambda i, s=s: (jnp.maximum(i * hb - 1, 0), s))


def _next_halo_spec(tm, d, s, m):
    hb = tm // HALO
    last = m // HALO - 1
    return pl.BlockSpec((HALO, d), lambda i, s=s: (jnp.minimum((i + 1) * hb, last), s))


def _spatial_mix(wm_ref, src, dst, bias_ref, tm, d):
    for n in range(tm // GMLP_BLOCK):
        for g in range(d // GMLP_BLOCK):
            rs = slice(n * GMLP_BLOCK, (n + 1) * GMLP_BLOCK)
            cs = slice(g * GMLP_BLOCK, (g + 1) * GMLP_BLOCK)
            v = jnp.dot(wm_ref[g], src[rs, cs], preferred_element_type=F32)
            if bias_ref is not None:
                v = v + bias_ref[:, cs]
            dst[rs, cs] = v


def _mix_fwd(z, conv_a, lnv, wm, bias_full, *, name, tm=256, phases=()):
    m, d9 = z.shape
    d = d9 // 9
    tm = _row_tile(m, tm)
    grp = d // len(POOL_WINDOWS)

    def body(zb, zc, zx, zu, zv, zp, zc_h, zx_h, zp_h, ca_ref, lnv_ref, wm_ref, bias_ref,
             ua_ref, ub_ref, d_ref, ext, vn_s, mixed_s):
        i = pl.program_id(0)
        first = i == 0
        f32 = lambda r: r[...].astype(F32)
        pa = f32(zc) * f32(zx)
        ext[0:HALO, :] = jnp.where(first, 0.0, f32(zc_h) * f32(zx_h))
        ext[HALO:HALO + tm, :] = pa
        w = ca_ref[...]
        conv = w[0:1, :] * ext[pl.ds(HALO - 2, tm), :] + w[1:2, :] * ext[pl.ds(HALO - 1, tm), :] + w[2:3, :] * pa
        ua_ref[...] = (f32(zb) * conv).astype(BF)
        p = f32(zp)
        ext[0:HALO, :] = jnp.where(first, 0.0, f32(zp_h))
        ext[HALO:HALO + tm, :] = p
        t = (i * tm + lax.broadcasted_iota(jnp.int32, (tm, 1), 0) + 1).astype(F32)
        for k, win in enumerate(POOL_WINDOWS):
            cs = slice(k * grp, (k + 1) * grp)
            s = p[:, cs]
            for j in range(1, win):
                s = s + ext[pl.ds(HALO - j, tm), cs]
            d_ref[:, cs] = (s / jnp.minimum(t, float(win)) - p[:, cs]).astype(BF)
        gv = _gelu(f32(zv))
        vhat, _ = _ln_stats(gv)
        vn_s[...] = (vhat * lnv_ref[0:1, :] + lnv_ref[1:2, :]).astype(BF)
        _spatial_mix(wm_ref, vn_s, mixed_s, bias_ref, tm, d)
        ub_ref[...] = (_gelu(f32(zu)) * mixed_s[...]).astype(BF)

    full = lambda a: pl.BlockSpec(a.shape, lambda i: (0,) * a.ndim)
    out = jax.ShapeDtypeStruct((m, d), BF)
    o_spec = pl.BlockSpec((tm, d), lambda i: (i, 0))
    return _pcall(
        body, (z, z, z, z, z, z, z, z, z, conv_a, lnv, wm, bias_full), name=name, grid=(m // tm,),
        in_specs=[_seg_spec(tm, d, s) for s in range(6)] + [_prev_halo_spec(tm, d, s) for s in (1, 2, 5)]
        + [full(conv_a), full(lnv), full(wm), full(bias_full)],
        out_specs=[o_spec, o_spec, o_spec], out_shape=[out, out, out],
        scratch_shapes=[pltpu.VMEM((HALO + tm, d), F32), pltpu.VMEM((tm, d), BF), pltpu.VMEM((tm, d), F32)],
        sem=("arbitrary",), phases=phases)


def _pool_proj(dd, w_pool, *, dn, name, out_dtype=F32, tm=512):
    m, d = dd.shape
    ng, grp, _ = w_pool.shape
    tm = _row_tile(m, tm)
    return _matmul(
        dd, w_pool, dn=dn, grid=(m // tm, ng, 1), name=name,
        a_spec=pl.BlockSpec((tm, grp), lambda i, j, kk: (i, j)),
        b_spec=pl.BlockSpec((None, grp, grp), lambda i, j, kk: (j, 0, 0)),
        o_spec=pl.BlockSpec((tm, grp), lambda i, j, kk: (i, j)),
        out_shape=jax.ShapeDtypeStruct((m, d), out_dtype), acc_shape=(tm, grp))


def _merge(z, ya, yb, ycp, scale, *, name, tm=512):
    m, d = ya.shape
    tm = _row_tile(m, tm)

    def body(ga, gb, gc, ya_ref, yb_ref, yc_ref, sc_ref, o_ref):
        f32 = lambda r: r[...].astype(F32)
        o_ref[...] = (jax.nn.sigmoid(f32(ga)) * f32(ya_ref) + jax.nn.sigmoid(f32(gb)) * f32(yb_ref)
                      + jax.nn.sigmoid(f32(gc)) * (f32(yc_ref) * sc_ref[...])).astype(BF)

    row = pl.BlockSpec((tm, d), lambda i: (i, 0))
    return pl.pallas_call(
        body, name=name, grid=(m // tm,),
        in_specs=[_seg_spec(tm, d, 6), _seg_spec(tm, d, 7), _seg_spec(tm, d, 8), row, row, row,
                  pl.BlockSpec((1, d), lambda i: (0, 0))],
        out_specs=row, out_shape=jax.ShapeDtypeStruct((m, d), BF),
        compiler_params=_params(("parallel",)),
    )(z, z, z, ya, yb, ycp, scale)


def _resid_ln(xp, ys, vec, alpha, *, name, tm=512):
    m, d = xp.shape
    tm = _row_tile(m, tm)

    def body(xp_ref, ys_ref, v_ref, o_ref):
        xhat, _ = _ln_stats(alpha * xp_ref[...] + v_ref[0:1, :] * ys_ref[...])
        o_ref[...] = xhat * v_ref[1:2, :] + v_ref[2:3, :]

    row = pl.BlockSpec((tm, d), lambda i: (i, 0))
    return pl.pallas_call(
        body, name=name, grid=(m // tm,),
        in_specs=[row, row, pl.BlockSpec(vec.shape, lambda i: (0, 0))],
        out_specs=row, out_shape=jax.ShapeDtypeStruct((m, d), F32),
        compiler_params=_params(("parallel",)),
    )(xp, ys, vec)


def _ffn_fwd(up4, cw, cb, *, name, tm=512, phases=()):
    _, nj, m, fs = up4.shape
    tm = _row_tile(m, tm)
    hb = tm // HALO

    def body(up_ref, ah_ref, cw_ref, cb_ref, f_ref, ext):
        first = pl.program_id(1) == 0
        ext[0:HALO, :] = jnp.where(first, 0.0, ah_ref[...].astype(F32))
        ext[HALO:HALO + tm, :] = up_ref[0].astype(F32)
        w = cw_ref[...]
        w0, w1, w2, bias = w[0:1, :], w[1:2, :], w[2:3, :], cb_ref[...]
        rc = 16

        def step(c, carry):
            r0 = pl.multiple_of(c * rc, rc)
            win = ext[pl.ds(r0 + HALO - 8, rc + 8), :]
            a0, a1, a2 = win[8:8 + rc], pltpu.roll(win, 1, 0)[8:8 + rc], pltpu.roll(win, 2, 0)[8:8 + rc]
            ca = w0 * a2 + w1 * a1 + w2 * a0 + bias
            f_ref[pl.ds(r0, rc), :] = (_gelu(ca) * up_ref[1, pl.ds(r0, rc), :].astype(F32)).astype(BF)
            return carry

        lax.fori_loop(0, tm // rc, step, 0)

    return _pcall(
        body, (up4, up4, cw, cb), name=name, grid=(nj, m // tm),
        in_specs=[pl.BlockSpec((2, None, tm, fs), lambda j, i: (0, j, i, 0)),
                  pl.BlockSpec((None, None, HALO, fs), lambda j, i: (0, j, jnp.maximum(i * hb - 1, 0), 0)),
                  pl.BlockSpec((None, 3, fs), lambda j, i: (j, 0, 0)),
                  pl.BlockSpec((None, 1, fs), lambda j, i: (j, 0, 0))],
        out_specs=pl.BlockSpec((None, tm, fs), lambda j, i: (j, i, 0)),
        out_shape=jax.ShapeDtypeStruct((nj, m, fs), BF),
        scratch_shapes=[pltpu.VMEM((HALO + tm, fs), F32)], sem=("parallel", "arbitrary"), phases=phases)


def _down_proj(f4, wd4, *, name, tm=2048, phases=()):
    nj, m, fs = f4.shape
    d = wd4.shape[2]
    tm = _row_tile(m, tm)
    return _matmul(
        f4, wd4, dn=NN, grid=(m // tm, 1, nj), name=name,
        a_spec=pl.BlockSpec((None, tm, fs), lambda i, j, kk: (kk, i, 0)),
        b_spec=pl.BlockSpec((None, fs, d), lambda i, j, kk: (kk, 0, 0)),
        o_spec=pl.BlockSpec((tm, d), lambda i, j, kk: (i, 0)),
        out_shape=jax.ShapeDtypeStruct((m, d), F32), acc_shape=(tm, d), phases=phases)


def _loss_grad(y, tgt, *, name, tm=512):
    m, d = y.shape
    tm = _row_tile(m, tm)
    ni = m // tm

    def body(y_ref, t_ref, dy_ref, l_ref, acc):
        i = pl.program_id(0)
        e = y_ref[...] - t_ref[...]
        dy_ref[...] = e * (1.0 / d)
        part = jnp.sum((e * e).reshape(tm // 8, 8, d), axis=0)

        @pl.when(i == 0)
        def _():
            acc[...] = part

        @pl.when(i > 0)
        def _():
            acc[...] += part

        @pl.when(i == ni - 1)
        def _():
            l_ref[...] = jnp.full((8, 128), 0.5 / d, F32) * jnp.sum(acc[...])

    row = pl.BlockSpec((tm, d), lambda i: (i, 0))
    return pl.pallas_call(
        body, name=name, grid=(ni,), in_specs=[row, row],
        out_specs=[row, pl.BlockSpec((8, 128), lambda i: (0, 0))],
        out_shape=[jax.ShapeDtypeStruct((m, d), F32), jax.ShapeDtypeStruct((8, 128), F32)],
        scratch_shapes=[pltpu.VMEM((8, d), F32)],
        compiler_params=_params(("arbitrary",)),
    )(y, tgt)


def _resid_ln_bwd(dpart, dh, xmod, mvec, xp, ys, vec, alpha, *, name, tm=256, phases=()):
    m, d = dpart.shape
    tm = _row_tile(m, tm)
    has_dh = dh is not None
    has_ln = xp is not None

    def body(*refs):
        refs = list(refs)
        dpart_ref = refs.pop(0)
        if has_dh:
            dh_ref, xm_ref, mv_ref = refs.pop(0), refs.pop(0), refs.pop(0)
        if has_ln:
            xp_ref, ys_ref, v_ref = refs.pop(0), refs.pop(0), refs.pop(0)
            dys_ref, dxp_ref, red_ref = refs
        else:
            dx_ref, red_ref = refs
        i = pl.program_id(0)
        dtot = dpart_ref[...]
        rows = [jnp.zeros((1, d), F32)] * 5
        if has_dh:
            dhv = dh_ref[...]
            dtot = dtot + dhv * mv_ref[...]
            rows[0] = _rowsum(dhv * xm_ref[...])
            rows[1] = _rowsum(dhv)
        if has_ln:
            ys = ys_ref[...]
            gt = v_ref[0:1, :]
            xhat, rstd = _ln_stats(alpha * xp_ref[...] + gt * ys)
            rows[2] = _rowsum(dtot * xhat)
            rows[3] = _rowsum(dtot)
            dr = _ln_bwd(dtot, xhat, rstd, v_ref[1:2, :])
            rows[4] = _rowsum(dr * ys)
            dys_ref[...] = (dr * gt).astype(BF)
            dxp_ref[...] = alpha * dr
        else:
            dx_ref[...] = dtot
        red = jnp.concatenate(rows + [jnp.zeros((3, d), F32)], axis=0)

        @pl.when(i == 0)
        def _():
            red_ref[...] = red

        @pl.when(i > 0)
        def _():
            red_ref[...] += red

    row = pl.BlockSpec((tm, d), lambda i: (i, 0))
    vrow = lambda a: pl.BlockSpec(a.shape, lambda i: (0, 0))
    args, specs = [dpart], [row]
    if has_dh:
        args += [dh, xmod, mvec]
        specs += [row, row, vrow(mvec)]
    if has_ln:
        args += [xp, ys, vec]
        specs += [row, row, vrow(vec)]
        out_specs = [row, row, pl.BlockSpec((8, d), lambda i: (0, 0))]
        out_shape = [jax.ShapeDtypeStruct((m, d), BF), jax.ShapeDtypeStruct((m, d), F32),
                     jax.ShapeDtypeStruct((8, d), F32)]
    else:
        out_specs = [row, pl.BlockSpec((8, d), lambda i: (0, 0))]
        out_shape = [jax.ShapeDtypeStruct((m, d), F32), jax.ShapeDtypeStruct((8, d), F32)]
    return _pcall(body, args, name=name, grid=(m // tm,), in_specs=specs, out_specs=out_specs, out_shape=out_shape,
                  sem=("arbitrary",), phases=phases)


def _down_bwd(dy, wd4, *, name, tm=2048, phases=()):
    m, d = dy.shape
    nj, fs, _ = wd4.shape
    tm = _row_tile(m, tm)
    return _matmul(
        dy, wd4, dn=NT, grid=(m // tm, nj, 1), name=name,
        a_spec=pl.BlockSpec((tm, d), lambda i, j, kk: (i, 0)),
        b_spec=pl.BlockSpec((None, fs, d), lambda i, j, kk: (j, 0, 0)),
        o_spec=pl.BlockSpec((None, tm, fs), lambda i, j, kk: (j, i, 0)),
        out_shape=jax.ShapeDtypeStruct((nj, m, fs), ACT_DTYPE), acc_shape=(tm, fs), phases=phases)


def _tn_shards_lhs(f4, dy, *, name, tk=4096, phases=()):
    nj, m, fs = f4.shape
    d = dy.shape[1]
    tk = _row_tile(m, tk)
    return _matmul(
        f4, dy, dn=TN, grid=(nj, 1, m // tk), name=name,
        a_spec=pl.BlockSpec((None, tk, fs), lambda i, j, kk: (i, kk, 0)),
        b_spec=pl.BlockSpec((tk, d), lambda i, j, kk: (kk, 0)),
        o_spec=pl.BlockSpec((None, fs, d), lambda i, j, kk: (i, 0, 0)),
        out_shape=jax.ShapeDtypeStruct((nj, fs, d), GRAD_DTYPE), acc_shape=(fs, d), phases=phases)


def _tn_shards_rhs(h, d8, *, name, tk=2048, phases=()):
    m, k = h.shape
    nb, _, ns = d8.shape
    tk = _row_tile(m, tk)
    return _matmul(
        h, d8, dn=TN, grid=(nb, 1, m // tk), name=name,
        a_spec=pl.BlockSpec((tk, k), lambda i, j, kk: (kk, 0)),
        b_spec=pl.BlockSpec((None, tk, ns), lambda i, j, kk: (i, kk, 0)),
        o_spec=pl.BlockSpec((None, k, ns), lambda i, j, kk: (i, 0, 0)),
        out_shape=jax.ShapeDtypeStruct((nb, k, ns), GRAD_DTYPE), acc_shape=(k, ns), phases=phases)


def _tn_cols_rhs(h, dz, nb, *, name, tk=4096, phases=()):
    m, k = h.shape
    ns = dz.shape[1] // nb
    tk = _row_tile(m, tk)
    return _matmul(
        h, dz, dn=TN, grid=(nb, 1, m // tk), name=name,
        a_spec=pl.BlockSpec((tk, k), lambda i, j, kk: (kk, 0)),
        b_spec=pl.BlockSpec((tk, ns), lambda i, j, kk: (kk, i)),
        o_spec=pl.BlockSpec((None, k, ns), lambda i, j, kk: (i, 0, 0)),
        out_shape=jax.ShapeDtypeStruct((nb, k, ns), GRAD_DTYPE), acc_shape=(k, ns), phases=phases)


def _nt_shards(d8, w8, *, name, tm=2048, phases=()):
    nb, m, ns = d8.shape
    k = w8.shape[1]
    tm = _row_tile(m, tm)
    return _matmul(
        d8, w8, dn=NT, grid=(m // tm, 1, nb), name=name,
        a_spec=pl.BlockSpec((None, tm, ns), lambda i, j, kk: (kk, i, 0)),
        b_spec=pl.BlockSpec((None, k, ns), lambda i, j, kk: (kk, 0, 0)),
        o_spec=pl.BlockSpec((tm, k), lambda i, j, kk: (i, 0)),
        out_shape=jax.ShapeDtypeStruct((m, k), F32), acc_shape=(tm, k), phases=phases)


def _nt_cols(dz, w8, *, name, tm=2048, phases=(), tiles=None, into=None):
    m = dz.shape[0]
    nb, k, ns = w8.shape
    tm = _row_tile(m, tm)
    first, count = tiles if tiles is not None else (0, m // tm)
    return _matmul(
        dz, w8, dn=NT, grid=(count, 1, nb), name=name,
        a_spec=pl.BlockSpec((tm, ns), lambda i, j, kk: (i + first, kk)),
        b_spec=pl.BlockSpec((None, k, ns), lambda i, j, kk: (kk, 0, 0)),
        o_spec=pl.BlockSpec((tm, k), lambda i, j, kk: (i + first, 0)),
        out_shape=jax.ShapeDtypeStruct((m, k), F32), acc_shape=(tm, k), phases=phases, into=into)


def _tn_pool(dd, dyc, ng, *, name, tk=2048):
    m, d = dd.shape
    grp = d // ng
    tk = _row_tile(m, tk)
    return _matmul(
        dd, dyc, dn=TN, grid=(ng, 1, m // tk), name=name,
        a_spec=pl.BlockSpec((tk, grp), lambda i, j, kk: (kk, i)),
        b_spec=pl.BlockSpec((tk, grp), lambda i, j, kk: (kk, i)),
        o_spec=pl.BlockSpec((None, grp, grp), lambda i, j, kk: (i, 0, 0)),
        out_shape=jax.ShapeDtypeStruct((ng, grp, grp), GRAD_DTYPE), acc_shape=(grp, grp))


def _ffn_bwd(up4, df4, cw, cb, *, name, tm=256, phases=()):
    _, nj, m, fs = up4.shape
    tm = _row_tile(m, tm)
    hb = tm // HALO
    ni = m // tm
    last_hb = m // HALO - 1
    ext_rows = tm + 8

    rc = 16
    assert tm % rc == 0

    def body(up_ref, ap_ref, un_ref, df_ref, dfn_ref, cw_ref, cb_ref, dup_ref, red_ref, ext, dca_s, racc):
        i = pl.program_id(1)
        ext[0:HALO, :] = jnp.where(i == 0, 0.0, ap_ref[...].astype(F32))
        ext[HALO:HALO + tm, :] = up_ref[0].astype(F32)
        ext[HALO + tm:2 * HALO + tm, :] = un_ref[0].astype(F32)
        racc[...] = jnp.zeros_like(racc)
        w = cw_ref[...]
        w0, w1, w2, bias = w[0:1, :], w[1:2, :], w[2:3, :], cb_ref[...]

        def conv_taps(win, n):
            return (win[8:8 + n], pltpu.roll(win, 1, 0)[8:8 + n], pltpu.roll(win, 2, 0)[8:8 + n])

        def fold(v):
            return v[0:8] + v[8:16]

        def add_red(k, v8):
            racc[8 * k:8 * k + 8, :] += v8

        def first_pass(c, carry):
            r0 = pl.multiple_of(c * rc, rc)
            a0, a1, a2 = conv_taps(ext[pl.ds(r0 + HALO - 8, rc + 8), :], rc)
            act, dact = _gelu_parts(w0 * a2 + w1 * a1 + w2 * a0 + bias)
            dfc = df_ref[pl.ds(r0, rc), :].astype(F32)
            dca = dfc * up_ref[1, pl.ds(r0, rc), :].astype(F32) * dact
            dup_g = dfc * act
            dca_s[pl.ds(r0, rc), :] = dca
            dup_ref[1, pl.ds(r0, rc), :] = dup_g.astype(BF)
            for k, v in enumerate((dca * a2, dca * a1, dca * a0, dca, dup_g)):
                add_red(k if k < 4 else 5, fold(v))
            return carry

        lax.fori_loop(0, tm // rc, first_pass, 0)
        a0, a1, a2 = conv_taps(ext[HALO + tm - 8:HALO + tm + 8, :], 8)
        _, dact = _gelu_parts(w0 * a2 + w1 * a1 + w2 * a0 + bias)
        after = dfn_ref[...].astype(F32)[0:8, :] * un_ref[1].astype(F32)[0:8, :] * dact
        dca_s[tm:tm + 8, :] = jnp.where(i < ni - 1, after, 0.0)
        dca_s[tm + 8:tm + 16, :] = jnp.zeros((8, fs), F32)

        def second_pass(c, carry):
            r0 = pl.multiple_of(c * rc, rc)
            win = dca_s[pl.ds(r0, rc + 8), :]
            up1, up2 = pltpu.roll(win, rc + 7, 0)[0:rc], pltpu.roll(win, rc + 6, 0)[0:rc]
            dup_a = w2 * win[0:rc] + w1 * up1 + w0 * up2
            dup_ref[0, pl.ds(r0, rc), :] = dup_a.astype(BF)
            add_red(4, fold(dup_a))
            return carry

        lax.fori_loop(0, tm // rc, second_pass, 0)
        red = jnp.concatenate([_rowsum(racc[8 * k:8 * k + 8, :]) for k in range(6)] + [jnp.zeros((2, fs), F32)],
                              axis=0)

        @pl.when(i == 0)
        def _():
            red_ref[...] = red

        @pl.when(i > 0)
        def _():
            red_ref[...] += red

    nxt = lambda j, i: jnp.minimum((i + 1) * hb, last_hb)
    return _pcall(
        body, (up4, up4, up4, df4, df4, cw, cb), name=name, grid=(nj, ni), sem=("parallel", "arbitrary"), phases=phases,
        in_specs=[pl.BlockSpec((2, None, tm, fs), lambda j, i: (0, j, i, 0)),
                  pl.BlockSpec((None, None, HALO, fs), lambda j, i: (0, j, jnp.maximum(i * hb - 1, 0), 0)),
                  pl.BlockSpec((2, None, HALO, fs), lambda j, i: (0, j, nxt(j, i), 0)),
                  pl.BlockSpec((None, tm, fs), lambda j, i: (j, i, 0)),
                  pl.BlockSpec((None, HALO, fs), lambda j, i: (j, nxt(j, i), 0)),
                  pl.BlockSpec((None, 3, fs), lambda j, i: (j, 0, 0)),
                  pl.BlockSpec((None, 1, fs), lambda j, i: (j, 0, 0))],
        out_specs=[pl.BlockSpec((2, None, tm, fs), lambda j, i: (0, j, i, 0)),
                   pl.BlockSpec((None, 8, fs), lambda j, i: (j, 0, 0))],
        out_shape=[jax.ShapeDtypeStruct((2, nj, m, fs), BF), jax.ShapeDtypeStruct((nj, 8, fs), F32)],
        scratch_shapes=[pltpu.VMEM((2 * HALO + tm, fs), F32), pltpu.VMEM((tm + 16, fs), F32),
                        pltpu.VMEM((48, fs), F32)])


def _gate_bwd(dm, z, ya, yb, ycp, scale, *, name, tm=256):
    m, d = dm.shape
    tm = _row_tile(m, tm)

    def body(dm_ref, ga, gb, gc, ya_ref, yb_ref, yc_ref, sc_ref, dya_ref, dyb_ref, dyc_ref, dz_ref, red_ref):
        i = pl.program_id(0)
        f32 = lambda r: r[...].astype(F32)
        dmv = f32(dm_ref)
        sa, sb, sc = jax.nn.sigmoid(f32(ga)), jax.nn.sigmoid(f32(gb)), jax.nn.sigmoid(f32(gc))
        scale_v = sc_ref[...]
        ycp_v = f32(yc_ref)
        dya_ref[...] = (dmv * sa).astype(BF)
        dyb_ref[...] = (dmv * sb).astype(BF)
        dyc = dmv * sc
        dyc_ref[...] = (dyc * scale_v).astype(BF)
        dga = dmv * f32(ya_ref) * (sa * (1.0 - sa))
        dgb = dmv * f32(yb_ref) * (sb * (1.0 - sb))
        dgc = dmv * (ycp_v * scale_v) * (sc * (1.0 - sc))
        dz_ref[:, 0:d] = dga.astype(BF)
        dz_ref[:, d:2 * d] = dgb.astype(BF)
        dz_ref[:, 2 * d:3 * d] = dgc.astype(BF)
        red = jnp.concatenate([_rowsum(dyc * ycp_v), _rowsum(dga), _rowsum(dgb), _rowsum(dgc),
                               jnp.zeros((4, d), F32)], axis=0)

        @pl.when(i == 0)
        def _():
            red_ref[...] = red

        @pl.when(i > 0)
        def _():
            red_ref[...] += red

    row = pl.BlockSpec((tm, d), lambda i: (i, 0))
    obf = jax.ShapeDtypeStruct((m, d), BF)
    return pl.pallas_call(
        body, name=name, grid=(m // tm,),
        in_specs=[row, _seg_spec(tm, d, 6), _seg_spec(tm, d, 7), _seg_spec(tm, d, 8), row, row, row,
                  pl.BlockSpec((1, d), lambda i: (0, 0))],
        out_specs=[row, row, row, pl.BlockSpec((tm, 3 * d), lambda i: (i, 2)), pl.BlockSpec((8, d), lambda i: (0, 0))],
        out_shape=[obf, obf, obf, jax.ShapeDtypeStruct((m, 9 * d), BF), jax.ShapeDtypeStruct((8, d), F32)],
        compiler_params=_params(("arbitrary",)),
    )(dm, z, z, z, ya, yb, ycp, scale)


def _mix_bwd(dz, dua, dub, ddd, z, conv_a, lnv, wm, wmt, bias_full, mask, *, name, tm=128, phases=()):
    m, d = dua.shape
    tm = _row_tile(m, tm)
    ni = m // tm
    grp = d // len(POOL_WINDOWS)
    ng = d // GMLP_BLOCK
    ext_rows = tm + 8

    def body(dz_in, dua_ref, dub_ref, dd_ref, zb, zc, zx, zu, zv, zp, zc_h, zx_h, dua_n, zb_n, dd_n,
             ca_ref, lnv_ref, wm_ref, wmt_ref, bias_ref, mask_ref,
             dz_ref, red_ref, dws_ref, dbs_ref, ext, sh_s, vn_s, mixed_s, dmx_s, dvn_s, dbs_acc):
        del dz_in
        i = pl.program_id(0)
        rows = []
        f32 = lambda r: r[...].astype(F32)
        zbv, zcv, zxv = f32(zb), f32(zc), f32(zx)
        pa = zcv * zxv
        ext[0:HALO, :] = jnp.where(i == 0, 0.0, f32(zc_h) * f32(zx_h))
        ext[HALO:HALO + tm, :] = pa
        w = ca_ref[...]
        w0, w1, w2 = w[0:1, :], w[1:2, :], w[2:3, :]
        p1 = ext[pl.ds(HALO - 1, tm), :]
        p2 = ext[pl.ds(HALO - 2, tm), :]
        conv = w0 * p2 + w1 * p1 + w2 * pa
        duav = f32(dua_ref)
        dzb = duav * conv
        dca = duav * zbv
        dca_n = jnp.where(i < ni - 1, f32(dua_n)[0:8, :] * f32(zb_n)[0:8, :], 0.0)
        sh_s[0:tm, :] = dca
        sh_s[tm:tm + 8, :] = dca_n
        dpa = w2 * dca + w1 * sh_s[pl.ds(1, tm), :] + w0 * sh_s[pl.ds(2, tm), :]
        dzc = dpa * zxv
        dzx = dpa * zcv
        dz_ref[:, 0:d] = dzb.astype(BF)
        dz_ref[:, d:2 * d] = dzc.astype(BF)
        dz_ref[:, 2 * d:3 * d] = dzx.astype(BF)
        rows += [_rowsum(dzb), _rowsum(dzc), _rowsum(dzx)]
        dconv = [_rowsum(dca * p2), _rowsum(dca * p1), _rowsum(dca * pa)]
        zuv, zvv = f32(zu), f32(zv)
        gu, dgu_dz = _gelu_parts(zuv)
        gv, dgv_dz = _gelu_parts(zvv)
        vhat, rstd = _ln_stats(gv)
        gain = lnv_ref[0:1, :]
        vn_s[...] = (vhat * gain + lnv_ref[1:2, :]).astype(BF)
        _spatial_mix(wm_ref, vn_s, mixed_s, bias_ref, tm, d)
        dubv = f32(dub_ref)
        dzu = dubv * mixed_s[...] * dgu_dz
        dmixed = dubv * gu
        dmx_s[...] = dmixed.astype(BF)
        _spatial_mix(wmt_ref, dmx_s, dvn_s, None, tm, d)
        dvn = dvn_s[...]
        dzv = _ln_bwd(dvn, vhat, rstd, gain) * dgv_dz
        dz_ref[:, 3 * d:4 * d] = dzu.astype(BF)
        dz_ref[:, 4 * d:5 * d] = dzv.astype(BF)
        rows += [_rowsum(dzu), _rowsum(dzv)]
        dlnv = [_rowsum(dvn * vhat), _rowsum(dvn)]
        dbs_part = dmixed[0:GMLP_BLOCK, :]
        for n in range(1, tm // GMLP_BLOCK):
            dbs_part = dbs_part + dmixed[n * GMLP_BLOCK:(n + 1) * GMLP_BLOCK, :]
        ddv = f32(dd_ref)
        t = (i * tm + lax.broadcasted_iota(jnp.int32, (ext_rows + 8, 1), 0) + 1).astype(F32)
        dde = jnp.concatenate([ddv, jnp.where(i < ni - 1, f32(dd_n), 0.0)], axis=0)
        for k, win in enumerate(POOL_WINDOWS):
            cs = slice(k * grp, (k + 1) * grp)
            ext[0:tm + HALO, cs] = dde[:, cs] / jnp.minimum(t, float(win))
        dzp_parts = []
        for k, win in enumerate(POOL_WINDOWS):
            cs = slice(k * grp, (k + 1) * grp)
            s = ext[0:tm, cs]
            for j in range(1, win):
                s = s + ext[pl.ds(j, tm), cs]
            dzp_parts.append(s - ddv[:, cs])
        dzp = jnp.concatenate(dzp_parts, axis=1)
        dz_ref[:, 5 * d:6 * d] = dzp.astype(BF)
        rows += [_rowsum(dzp)]
        red = jnp.concatenate(rows + dconv + dlnv + [jnp.zeros((5, d), F32)], axis=0)

        @pl.when(i == 0)
        def _():
            red_ref[...] = red
            dbs_acc[...] = dbs_part
            dws_ref[...] = jnp.zeros_like(dws_ref)

        @pl.when(i > 0)
        def _():
            red_ref[...] += red
            dbs_acc[...] += dbs_part

        for n in range(tm // GMLP_BLOCK):
            for g in range(ng):
                rs = slice(n * GMLP_BLOCK, (n + 1) * GMLP_BLOCK)
                cs = slice(g * GMLP_BLOCK, (g + 1) * GMLP_BLOCK)
                dws_ref[g] += mask_ref[...] * lax.dot_general(
                    dmx_s[rs, cs], vn_s[rs, cs], (NT, ((), ())), preferred_element_type=F32)

        @pl.when(i == ni - 1)
        def _():
            lane = lax.broadcasted_iota(jnp.int32, (GMLP_BLOCK, GMLP_BLOCK), 1)
            out = jnp.zeros((GMLP_BLOCK, GMLP_BLOCK), F32)
            for g in range(ng):
                sg = jnp.sum(dbs_acc[:, g * GMLP_BLOCK:(g + 1) * GMLP_BLOCK], axis=1, keepdims=True)
                out = out + jnp.where(lane == g, sg, 0.0)
            dbs_ref[...] = out

    row = pl.BlockSpec((tm, d), lambda i: (i, 0))
    full = lambda a: pl.BlockSpec(a.shape, lambda i: (0,) * a.ndim)
    hb = tm // HALO
    last_hb = m // HALO - 1
    nrow = pl.BlockSpec((HALO, d), lambda i: (jnp.minimum((i + 1) * hb, last_hb), 0))
    return _pcall(
        body, (dz, dua, dub, ddd, z, z, z, z, z, z, z, z, dua, z, ddd, conv_a, lnv, wm, wmt, bias_full, mask),
        name=name, grid=(ni,), sem=("arbitrary",), aliases={0: 0}, phases=phases,
        in_specs=[pl.BlockSpec(memory_space=pl.ANY), row, row, row]
        + [_seg_spec(tm, d, s) for s in range(6)]
        + [_prev_halo_spec(tm, d, 1), _prev_halo_spec(tm, d, 2), nrow, _next_halo_spec(tm, d, 0, m), nrow]
        + [full(conv_a), full(lnv), full(wm), full(wmt), full(bias_full), full(mask)],
        out_specs=[pl.BlockSpec((tm, 6 * d), lambda i: (i, 0)), pl.BlockSpec((16, d), lambda i: (0, 0)),
                   full(wm), pl.BlockSpec((GMLP_BLOCK, GMLP_BLOCK), lambda i: (0, 0))],
        out_shape=[jax.ShapeDtypeStruct(dz.shape, BF), jax.ShapeDtypeStruct((16, d), F32),
                   jax.ShapeDtypeStruct(wm.shape, F32), jax.ShapeDtypeStruct((GMLP_BLOCK, GMLP_BLOCK), F32)],
        scratch_shapes=[pltpu.VMEM((2 * HALO + tm, d), F32), pltpu.VMEM((tm + 8, d), F32),
                        pltpu.VMEM((tm, d), BF), pltpu.VMEM((tm, d), F32), pltpu.VMEM((tm, d), BF),
                        pltpu.VMEM((tm, d), F32), pltpu.VMEM((GMLP_BLOCK, d), F32)])


REST = ("w_a_out", "w_b_out", "w_pool", "w_o", "w_up", "w_down")


def _remote(src, dst, ssem, rsem, k, to):
    return pltpu.make_async_remote_copy(src_ref=src, dst_ref=dst, send_sem=ssem.at[k], recv_sem=rsem.at[k],
                                        device_id=to, device_id_type=MESH)


def _gather_phase1(shards, rows=None, onto=None):
    n = len(shards)
    rows = rows or [None] * n
    onto = onto or [None] * n
    extra = [a for a in range(n) if onto[a] is not None]

    def build(ins, outs, ssem, rsem, lsem):
        x, y, c, chips = _place()
        me = 4 * x + 2 * y + c

        def src(a):
            return ins[a] if rows[a] is None else ins[a].at[:, pl.ds(*rows[a])]

        def dst(a, dev):
            return outs[a].at[:, dev] if rows[a] is None else outs[a].at[:, dev, pl.ds(*rows[a])]

        local = [pltpu.make_async_copy(src(a), dst(a, me), lsem.at[a]) for a in range(n)]
        sends, recvs = [], []
        for j, (cx, cy) in enumerate(chips):
            for a in range(n):
                sends.append(_remote(src(a), dst(a, me), ssem, rsem, 4 * a + 1 + j, (cx, cy, c)))
                recvs.append(_remote(src(a), dst(a, 4 * cx + 2 * cy + c), ssem, rsem, 4 * a + 1 + j, (cx, cy, c)))
        for a in range(n):
            sends.append(_remote(src(a), dst(a, me), ssem, rsem, 4 * a, (x, y, 1 - c)))
            recvs.append(_remote(src(a), dst(a, 4 * x + 2 * y + 1 - c), ssem, rsem, 4 * a, (x, y, 1 - c)))
        return dict(start=local + sends, recv=recvs, send=sends, local=local)

    outs = [jax.ShapeDtypeStruct((s.shape[0], N_DEV) + s.shape[1:], s.dtype) for s in shards]
    return _Phase(list(shards) + [onto[a] for a in extra], outs, {n + k: a for k, a in enumerate(extra)},
                  4 * n, n, build)


def _gather_phase2(fulls):
    n = len(fulls)

    def build(ins, outs, ssem, rsem, lsem):
        x, y, c, chips = _place()
        sends, recvs = [], []
        for j, (cx, cy) in enumerate(chips):
            for a in range(n):
                mine, theirs = 4 * cx + 2 * cy + c, 4 * cx + 2 * cy + 1 - c
                sends.append(_remote(ins[a].at[:, mine], outs[a].at[:, mine], ssem, rsem, 3 * a + j, (x, y, 1 - c)))
                recvs.append(_remote(ins[a].at[:, theirs], outs[a].at[:, theirs], ssem, rsem, 3 * a + j, (x, y, 1 - c)))
        return dict(start=sends, recv=recvs, send=sends, local=[])

    outs = [jax.ShapeDtypeStruct(f.shape, f.dtype) for f in fulls]
    return _Phase(fulls, outs, {a: a for a in range(n)}, 3 * n, 0, build)


def _pair_phase(grads):
    n = len(grads)

    def build(ins, outs, ssem, rsem, lsem):
        x, y, c, _ = _place()
        cps = [_remote(ins[a].at[:, 2 * q + (1 - c)], outs[a].at[q], ssem, rsem, 4 * a + q, (x, y, 1 - c))
               for a in range(n) for q in range(4)]
        return dict(start=cps, recv=cps, send=cps, local=[])

    outs = [jax.ShapeDtypeStruct((4, g.shape[0]) + g.shape[2:], g.dtype) for g in grads]
    return _Phase(grads, outs, {}, 4 * n, 0, build)


def _chip_phase(bufs, accs, l, depth):
    n = len(bufs)
    has = accs is not None

    def build(ins, outs, ssem, rsem, lsem):
        x, y, c, chips = _place()
        myq = 2 * x + y
        local, sends, recvs = [], [], []
        for a in range(n):
            local.append(pltpu.make_async_copy(ins[a].at[myq], outs[a].at[myq, l], lsem.at[a]))
            for j, (cx, cy) in enumerate(chips):
                q = 2 * cx + cy
                sends.append(_remote(ins[a].at[q], outs[a].at[myq, l], ssem, rsem, 3 * a + j, (cx, cy, c)))
                recvs.append(_remote(ins[a].at[q], outs[a].at[q, l], ssem, rsem, 3 * a + j, (cx, cy, c)))
        return dict(start=local + sends, recv=recvs, send=sends, local=local)

    outs = [jax.ShapeDtypeStruct((4, depth) + b.shape[1:], b.dtype) for b in bufs]
    return _Phase(list(bufs) + (list(accs) if has else []), outs, {n + a: a for a in range(n)} if has else {},
                  3 * n, n, build)


def _grad_chunks(n, g):
    if n == "w_pool":
        return g.reshape(g.shape[0], N_DEV, g.shape[1] // N_DEV, g.shape[2])
    if n in ("w_in", "w_up"):
        return g[None]
    return g.reshape(1, N_DEV, -1, g.shape[-1])


class _ReduceScatter:
    def __init__(self, depth, cidx):
        self.depth, self.cidx, self.acc, self.count = depth, cidx, {}, 0
        self.small_gathered = None
        self.presummed = None

    def pair(self, names, grads):
        return _pair_phase([_grad_chunks(n, grads[n]) for n in names])

    def sums(self, names, grads, phase):
        out = []
        for n, r1 in zip(names, phase.results):
            out.append(_pair_sum(_grad_chunks(n, grads[n]), r1, self.cidx, name="rs_sum_%d" % self.count))
            self.count += 1
        return out

    def chip(self, names, bufs, l):
        accs = [self.acc[n] for n in names] if names[0] in self.acc else None
        return _chip_phase(bufs, accs, l, self.depth)

    def done(self, names, phase):
        for n, r in zip(names, phase.results):
            self.acc[n] = r


def _rest_views(fulls, d):
    a_out, b_out, pool, o, up, down = fulls
    grp = d // len(POOL_WINDOWS)
    return dict(w_a_out=a_out.reshape(d, d), w_b_out=b_out.reshape(d, d), w_o=o.reshape(d, d),
                w_pool=pool.reshape(len(POOL_WINDOWS), grp, grp), w_up8=up[0],
                wd4=down.reshape(N_DEV // 2, -1, d))


class _GatherPlan:
    def __init__(self):
        self.jobs, self.part, self.full = [], {}, {}

    def add(self, key, shard, first, second, rows=None, onto=None):
        self.jobs.append((key, shard, first, second, rows, onto))

    def phases(self, name):
        j1 = [j for j in self.jobs if j[2] == name]
        j2 = [j for j in self.jobs if j[3] == name]
        tagged = []
        if j1:
            onto = [self.part[j[5]] if j[5] else None for j in j1]
            tagged.append((self.part, j1, _gather_phase1([j[1] for j in j1], [j[4] for j in j1], onto)))
        if j2:
            tagged.append((self.full, j2, _gather_phase2([self.part[j[0]] for j in j2])))
        return tagged

    @staticmethod
    def collect(tagged):
        for store, jobs, phase in tagged:
            for j, r in zip(jobs, phase.results):
                store[j[0]] = r


def _layer_fwd(x, w, alpha, tag, plan=None):
    d = x.shape[1]
    grp = d // len(POOL_WINDOWS)

    def carried(kernel, *args, name, **kw):
        tagged = plan.phases(name) if plan else []
        out = kernel(*args, name=name, phases=[t[2] for t in tagged], **kw)
        _GatherPlan.collect(tagged)
        return out

    def weight(n, shape):
        return plan.full[n + tag].reshape(shape) if plan else w[n]

    z, h = carried(_mod_matmul, x, w["mod1"], w["w_in8"], w["b_in8"], flat_out=True, name="in_proj" + tag)
    ua, ub, dd = carried(_mix_fwd, z, w["conv_a"], w["lnv"], w["wm"], w["bias_full"], name="mix_fwd" + tag)
    w["w_a_out"], w["w_b_out"], w["w_o"] = (weight(n, (d, d)) for n in ("w_a_out", "w_b_out", "w_o"))
    w["w_pool"] = weight("w_pool", (len(POOL_WINDOWS), grp, grp))
    w["w_up8"] = weight("w_up8", (N_DEV, d, -1))
    ya = _mm_rows(ua, w["w_a_out"], dn=NN, name="a_out" + tag, out_dtype=ACT_DTYPE)
    yb = _mm_rows(ub, w["w_b_out"], dn=NN, name="b_out" + tag, out_dtype=ACT_DTYPE)
    ycp = _pool_proj(dd, w["w_pool"], dn=NN, name="pool_proj" + tag, out_dtype=ACT_DTYPE)
    merged = _merge(z, ya, yb, ycp, w["pool_scale"], name="merge" + tag)
    o = _mm_rows(merged, w["w_o"], dn=NN, name="o_proj" + tag)
    x1 = _resid_ln(x, o, w["ln1"], alpha, name="ln1" + tag)
    up8, h2 = carried(_mod_matmul, x1, w["mod2"], w["w_up8"], w["b_up8"], flat_out=False, name="up_proj" + tag)
    up4 = up8.reshape((2, up8.shape[0] // 2) + up8.shape[1:])
    f4 = carried(_ffn_fwd, up4, w["cw"], w["cb"], name="ffn_fwd" + tag)
    w["wd4"] = weight("wd4", (N_DEV // 2, -1, d))
    y2 = carried(_down_proj, f4, w["wd4"], name="down_proj" + tag)
    x2 = _resid_ln(x1, y2, w["ln2"], alpha, name="ln2" + tag)
    saved = dict(x=x, z=z, h=h, ua=ua, ub=ub, dd=dd, ya=ya, yb=yb, ycp=ycp, merged=merged, o=o, x1=x1,
                 up4=up4, h2=h2, f4=f4, y2=y2)
    return x2, saved


def _layer_bwd(dpart, dh_above, xmod_above, m_above, w, s, alpha, tag, l=0, above=None, rs=None, upper_reds=()):
    first, rest = ("w_in",), REST
    ph = lambda p: [p] if p is not None else ()
    pre = rs.presummed if rs is not None else None
    r1a = rs.pair(first, above) if above and not pre else None
    dy2, dx1p, red2 = _resid_ln_bwd(dpart, dh_above, xmod_above, m_above, s["x1"], s["y2"], w["ln2"], alpha,
                                    name="ln2_bwd" + tag, phases=ph(r1a))
    r1b = rs.pair(rest, above) if above and not pre else None
    df4 = _down_bwd(dy2, w["wd4"], name="down_bwd" + tag, phases=ph(r1b))
    gw_down4 = _tn_shards_lhs(s["f4"], dy2, name="gw_down" + tag)
    r3a = r3b = r3c = None
    if above:
        if pre:
            bufs, rs.presummed = pre, None
        else:
            bufs = dict(zip(first + rest, rs.sums(first, above, r1a) + rs.sums(rest, above, r1b)))
        light = tuple(n for n in rest if n != "w_up")
        r3a = rs.chip(first, [bufs[n] for n in first], l + 1)
    dup4, redf = _ffn_bwd(s["up4"], df4, w["cw"], w["cb"], name="ffn_bwd" + tag, phases=ph(r3a))
    dup8 = dup4.reshape((dup4.shape[0] * dup4.shape[1],) + dup4.shape[2:])
    if above:
        rs.done(first, r3a)
        r3b = rs.chip(("w_up",), [bufs["w_up"]], l + 1)
    gw_up8 = _tn_shards_lhs(dup8, s["h2"], name="gw_up" + tag, phases=ph(r3b))
    if above:
        rs.done(("w_up",), r3b)
        r3c = rs.chip(light, [bufs[n] for n in light], l + 1)
    own = rs is not None and l == 0
    big = dict(w_up=gw_up8, w_down=gw_down4)
    early = ("w_down", "w_up")
    o1 = rs.pair(early, big) if own else None
    dh2 = _nt_shards(dup8, w["w_up8"], name="up_bwd" + tag, phases=list(ph(r3c)) + list(ph(o1)))
    if above:
        rs.done(light, r3c)
    if own:
        sb_o = rs.sums(early, big, o1)
    do, dxp, red1 = _resid_ln_bwd(dx1p, dh2, s["x1"], w["mod2"][0:1], s["x"], s["o"], w["ln1"], alpha,
                                  name="ln1_bwd" + tag)
    dm = _mm_rows(do, w["w_o"], dn=NT, name="o_bwd" + tag, out_dtype=ACT_DTYPE)
    big["w_o"] = _mm_tn(s["merged"], do, name="gw_o" + tag)
    dya, dyb, dyc, dz, redg = _gate_bwd(dm, s["z"], s["ya"], s["yb"], s["ycp"], w["pool_scale"], name="gate_bwd" + tag)
    dua = _mm_rows(dya, w["w_a_out"], dn=NT, name="a_out_bwd" + tag, out_dtype=ACT_DTYPE)
    dub = _mm_rows(dyb, w["w_b_out"], dn=NT, name="b_out_bwd" + tag, out_dtype=ACT_DTYPE)
    ddd = _pool_proj(dyc, w["w_pool"], dn=NT, name="pool_bwd" + tag, out_dtype=ACT_DTYPE)
    big["w_a_out"] = _mm_tn(s["ua"], dya, name="gw_a_out" + tag)
    big["w_b_out"] = _mm_tn(s["ub"], dyb, name="gw_b_out" + tag)
    big["w_pool"] = _tn_pool(s["dd"], dyc, w["w_pool"].shape[0], name="gw_pool" + tag)
    o3 = rs.chip(early, sb_o, l) if own else None
    dz, redm, dws, dbs = _mix_bwd(dz, dua, dub, ddd, s["z"], w["conv_a"], w["lnv"], w["wm"], w["wmt"],
                                  w["bias_full"], w["mask"], name="mix_bwd" + tag, phases=ph(o3))
    reds = dict(red2=red2, redf=redf, red1=red1, redg=redg, redm=redm, dws=dws, dbs=dbs)
    mid = ("w_o", "w_a_out", "w_b_out", "w_pool")
    o1b = sg1 = None
    if own:
        rs.done(early, o3)
        o1b = rs.pair(mid, big)
        sg1 = _gather_phase1([_small_payload([reds] + list(upper_reds))])
    big["w_in"] = _tn_cols_rhs(s["h"], dz, w["w_in8"].shape[0], name="gw_in" + tag,
                               phases=[o1b, sg1] if own else ())
    pending = None
    if own:
        sb_m = rs.sums(mid, big, o1b)
        o1c, o3b, sg2 = rs.pair(first, big), rs.chip(mid, sb_m, l), _gather_phase2(sg1.results)
        split_tm = 1024
        n_tiles = dz.shape[0] // _row_tile(dz.shape[0], split_tm)
        dh = _nt_cols(dz, w["w_in8"], name="in_bwd" + tag + "_a", phases=[o1c, o3b, sg2], tm=split_tm, tiles=(0, 1))
        rs.done(mid, o3b)
        rs.small_gathered = sg2.results[0]
        o3c = rs.chip(first, rs.sums(first, big, o1c), l)
        if n_tiles > 1:
            dh = _nt_cols(dz, w["w_in8"], name="in_bwd" + tag + "_b", phases=[o3c], tm=split_tm,
                          tiles=(1, n_tiles - 1), into=dh)
            rs.done(first, o3c)
        else:
            pending = (first, o3c)
    elif rs is not None:
        pa, pb = rs.pair(first, big), rs.pair(rest, big)
        dh = _nt_cols(dz, w["w_in8"], name="in_bwd" + tag, phases=[pa, pb])
        rs.presummed = dict(zip(first + rest, rs.sums(first, big, pa) + rs.sums(rest, big, pb)))
    else:
        dh = _nt_cols(dz, w["w_in8"], name="in_bwd" + tag)
    return dxp, dh, big, reds, pending


def _local_step(x, tgt, ws, alpha, plan=None, rs=None):
    depth = len(ws)
    saved = []
    y = x
    for l in range(depth):
        if plan and l > 0:
            ws[l]["w_in8"] = plan.full["w_in8_l%d" % l][0]
        y, s = _layer_fwd(y, ws[l], alpha, "_l%d" % l, plan)
        saved.append(s)
    dpart, loss_blk = _loss_grad(y, tgt, name="loss_grad")
    dh = xmod = mvec = above = pending = None
    bigs, reds = [None] * depth, [None] * depth
    for l in reversed(range(depth)):
        dpart, dh, bigs[l], reds[l], pending = _layer_bwd(dpart, dh, xmod, mvec, ws[l], saved[l], alpha, "_l%d" % l,
                                                          l, above if rs else None, rs, reds[l + 1:])
        xmod, mvec, above = saved[l]["x"], ws[l]["mod1"][0:1], bigs[l]
    grad_x, red0 = _resid_ln_bwd(dpart, dh, xmod, mvec, None, None, None, alpha, name="in_bwd_tail",
                                 phases=[pending[1]] if pending else ())
    if pending:
        rs.done(*pending)
    d_ada = []
    for l in range(depth):
        below = red0 if l == 0 else reds[l - 1]["red2"]
        r1, r2 = reds[l]["red1"], reds[l]["red2"]
        d_ada.append(jnp.stack([below[1], below[0], r1[4], r1[1], r1[0], r2[4]]))
    return loss_blk, grad_x, bigs, reds, jnp.stack(d_ada)


def _small_grads(r):
    redm, redg, redf = r["redm"], r["redg"], r["redf"]
    ng = r["dws"].shape[0]
    return dict(
        b_in=jnp.concatenate([redm[0:6], redg[1:4]], axis=0).reshape(-1),
        conv_a=redm[6:9], ln_v_g=redm[9], ln_v_b=redm[10],
        w_spatial=r["dws"], b_spatial=r["dbs"][:, :ng].T,
        pool_scale=redg[0], ln1_g=r["red1"][2], ln1_b=r["red1"][3],
        b_up=jnp.concatenate([redf[:, 4, :].reshape(-1), redf[:, 5, :].reshape(-1)]),
        conv_ffn=jnp.transpose(redf[:, 0:3, :], (1, 0, 2)).reshape(3, -1), conv_ffn_b=redf[:, 3, :].reshape(-1),
        ln2_g=r["red2"][2], ln2_b=r["red2"][3])


def _small_payload(reds):
    smalls = [_small_grads(r) for r in reds]
    order = SMALL_REPLICATED + SMALL_SHARDED
    flat = jnp.concatenate([smalls[l][n].reshape(-1) for n in order for l in range(len(reds))])
    return _as_rows(flat)[None]


def _layer_weights(l, ada, conv_a, conv_ffn, p):
    sh1, sc1, gt1, sh2, sc2, gt2 = (ada[l, k][None, :] for k in range(6))
    nb = N_DEV
    fs = p["b_up"].shape[1] // nb
    nj = nb // 2
    pos = jnp.arange(GMLP_BLOCK)
    allowed = (pos[None, :] // CHUNK) <= (pos[:, None] // CHUNK)
    wmask = jnp.where(allowed[None], p["w_spatial"][l], 0.0)
    return dict(
        mod1=jnp.concatenate([1.0 + sc1, sh1]), mod2=jnp.concatenate([1.0 + sc2, sh2]),
        ln1=jnp.concatenate([gt1, p["ln1_g"][l][None], p["ln1_b"][l][None]]),
        ln2=jnp.concatenate([gt2, p["ln2_g"][l][None], p["ln2_b"][l][None]]),
        b_in8=p["b_in"][l].reshape(N_DEV, 1, -1), b_up8=p["b_up"][l].reshape(nb, 1, fs),
        conv_a=conv_a[l], lnv=jnp.stack([p["ln_v_g"][l], p["ln_v_b"][l]]),
        wm=wmask.astype(BF), wmt=jnp.transpose(wmask, (0, 2, 1)).astype(BF),
        bias_full=jnp.repeat(p["b_spatial"][l].T, GMLP_BLOCK, axis=1), mask=allowed.astype(F32),
        pool_scale=p["pool_scale"][l][None],
        cw=jnp.transpose(conv_ffn[l].reshape(3, nj, fs), (1, 0, 2)), cb=p["conv_ffn_b"][l].reshape(nj, 1, fs))


ANY = pl.BlockSpec(memory_space=pl.ANY)


def _place():
    x, y, c = lax.axis_index("x"), lax.axis_index("y"), lax.axis_index("c")
    chips = [(1 - x, y), (x, 1 - y), (1 - x, 1 - y)]
    return x, y, c, chips


def _allgather_vmem(xs, *, name):
    r, cdim = xs.shape

    def body(x_ref, out_ref, send_sems, recv_sems, local_sem):
        x, y, c, chips = _place()
        me, sibling = (x, y, c), (x, y, 1 - c)

        def rows(px, py, pc):
            return out_ref.at[pl.ds((4 * px + 2 * py + pc) * r, r), :]

        def copy(k, block, to, src=None):
            return pltpu.make_async_remote_copy(
                src_ref=rows(*block) if src is None else src, dst_ref=rows(*block),
                send_sem=send_sems.at[k], recv_sem=recv_sems.at[k], device_id=to, device_id_type=MESH)

        mine = pltpu.make_async_copy(x_ref, rows(*me), local_sem)
        mine.start()
        first = [copy(0, me, sibling, src=x_ref)]
        first += [copy(1 + j, me, (*chip, c), src=x_ref) for j, chip in enumerate(chips)]
        for cp in first:
            cp.start()
        passed = [copy(4 + j, (*chip, c), sibling) for j, chip in enumerate(chips)]
        for j, chip in enumerate(chips):
            copy(1 + j, (*chip, c), me).wait_recv()
            passed[j].start()
        copy(0, sibling, me).wait_recv()
        for j, chip in enumerate(chips):
            copy(4 + j, (*chip, 1 - c), me).wait_recv()
        for cp in first + passed:
            cp.wait_send()
        mine.wait()

    return pl.pallas_call(
        body, name=name, out_shape=jax.ShapeDtypeStruct((N_DEV * r, cdim), xs.dtype),
        in_specs=[pl.BlockSpec(memory_space=pltpu.VMEM)], out_specs=pl.BlockSpec(memory_space=pltpu.VMEM),
        scratch_shapes=[pltpu.SemaphoreType.DMA((7,)), pltpu.SemaphoreType.DMA((7,)), pltpu.SemaphoreType.DMA],
        compiler_params=_params(),
    )(xs)


def _gather_weights(shards, *, name):
    n = len(shards)

    def body(*refs):
        ins, outs = refs[:n], refs[n:2 * n]
        send_sems, recv_sems, local_sems = refs[2 * n:]
        x, y, c, chips = _place()
        me, sibling = (x, y, c), (x, y, 1 - c)

        def slot(a, px, py, pc):
            return outs[a].at[:, 4 * px + 2 * py + pc]

        def copy(a, k, block, to, src=None):
            return pltpu.make_async_remote_copy(
                src_ref=slot(a, *block) if src is None else src, dst_ref=slot(a, *block),
                send_sem=send_sems.at[7 * a + k], recv_sem=recv_sems.at[7 * a + k], device_id=to,
                device_id_type=MESH)

        mine = [pltpu.make_async_copy(ins[a], slot(a, *me), local_sems.at[a]) for a in range(n)]
        for cp in mine:
            cp.start()
        first = []
        for j, chip in enumerate(chips):
            first += [copy(a, 1 + j, me, (*chip, c), src=ins[a]) for a in range(n)]
        first += [copy(a, 0, me, sibling, src=ins[a]) for a in range(n)]
        for cp in first:
            cp.start()
        passed = []
        for j, chip in enumerate(chips):
            for a in range(n):
                copy(a, 1 + j, (*chip, c), me).wait_recv()
                fwd = copy(a, 4 + j, (*chip, c), sibling)
                fwd.start()
                passed.append(fwd)
        for a in range(n):
            copy(a, 0, sibling, me).wait_recv()
        for j, chip in enumerate(chips):
            for a in range(n):
                copy(a, 4 + j, (*chip, 1 - c), me).wait_recv()
        for cp in first + passed:
            cp.wait_send()
        for cp in mine:
            cp.wait()

    out_shape = [jax.ShapeDtypeStruct((s.shape[0], N_DEV) + s.shape[1:], s.dtype) for s in shards]
    return pl.pallas_call(
        body, name=name, out_shape=out_shape, in_specs=[ANY] * n, out_specs=[ANY] * n,
        scratch_shapes=[pltpu.SemaphoreType.DMA((7 * n,)), pltpu.SemaphoreType.DMA((7 * n,)),
                        pltpu.SemaphoreType.DMA((n,))],
        compiler_params=_params(),
    )(*shards)


def _pick_tile(r, cap):
    best = None
    for t in range(8, min(r, cap) + 1, 8):
        if r % t == 0:
            best = t
    return best if best is not None else r


def _pair_sum(g, r1, cidx, *, name):
    p, _, r, cdim = g.shape
    tr = _pick_tile(r, 1024)

    def body(c_ref, g_ref, r_ref, o_ref):
        del c_ref
        o_ref[...] = (g_ref[...].astype(F32) + r_ref[...].astype(F32)).astype(BF)

    grid_spec = pltpu.PrefetchScalarGridSpec(
        num_scalar_prefetch=1, grid=(4, r // tr),
        in_specs=[pl.BlockSpec((p, None, tr, cdim), lambda q, i, c: (0, 2 * q + c[0], i, 0)),
                  pl.BlockSpec((None, p, tr, cdim), lambda q, i, c: (q, 0, i, 0))],
        out_specs=pl.BlockSpec((None, p, tr, cdim), lambda q, i, c: (q, 0, i, 0)))
    return pl.pallas_call(
        body, name=name, grid_spec=grid_spec, out_shape=jax.ShapeDtypeStruct((4, p, r, cdim), BF),
        compiler_params=_params(("arbitrary", "arbitrary")),
    )(cidx, g, r1)


def _ada_fwd(c_all, w_ada, *, name):
    depth, d, ns = w_ada.shape
    nb = c_all.shape[0]

    def body(c_ref, w_ref, o_ref):
        cv = c_ref[...]
        act = cv * jax.nn.sigmoid(cv)
        o_ref[...] = jnp.dot(act, w_ref[...], preferred_element_type=F32, precision=lax.Precision.HIGHEST)

    return pl.pallas_call(
        body, name=name, grid=(depth,),
        in_specs=[pl.BlockSpec((nb, d), lambda l: (0, 0)), pl.BlockSpec((None, d, ns), lambda l: (l, 0, 0))],
        out_specs=pl.BlockSpec((None, nb, ns), lambda l: (l, 0, 0)),
        out_shape=jax.ShapeDtypeStruct((depth, nb, ns), F32), compiler_params=_params(("parallel",)),
    )(c_all, w_ada)


def _ada_bwd(ct, dmine, dall, *, name):
    depth, nb, ns = dmine.shape
    d = ct.shape[0]

    def body(ct_ref, dm_ref, da_ref, gw_ref, gb_ref):
        cv = ct_ref[...]
        act = cv * jax.nn.sigmoid(cv)
        gw_ref[...] = jnp.dot(act, dm_ref[...], preferred_element_type=F32, precision=lax.Precision.HIGHEST)
        s = da_ref[0]
        for b in range(1, nb):
            s = s + da_ref[b]
        gb_ref[...] = s

    return pl.pallas_call(
        body, name=name, grid=(depth,),
        in_specs=[pl.BlockSpec((d, nb), lambda l: (0, 0)), pl.BlockSpec((None, nb, ns), lambda l: (l, 0, 0)),
                  pl.BlockSpec(dall.shape, lambda l: (0, 0, 0))],
        out_specs=[pl.BlockSpec((None, d, ns), lambda l: (l, 0, 0)), pl.BlockSpec(dall.shape[1:], lambda l: (0, 0))],
        out_shape=[jax.ShapeDtypeStruct((depth, d, ns), F32), jax.ShapeDtypeStruct(dall.shape[1:], F32)],
        compiler_params=_params(("arbitrary",)),
    )(ct, dmine, dall)


def _sum_parts(parts, *, name):
    p, r, cdim = parts.shape
    tr = _pick_tile(r, 512)

    def body(p_ref, o_ref):
        s = p_ref[0]
        for k in range(1, p):
            s = s + p_ref[k]
        o_ref[...] = s

    return pl.pallas_call(
        body, name=name, grid=(r // tr,),
        in_specs=[pl.BlockSpec((p, tr, cdim), lambda i: (0, i, 0))], out_specs=pl.BlockSpec((tr, cdim), lambda i: (i, 0)),
        out_shape=jax.ShapeDtypeStruct((r, cdim), F32), compiler_params=_params(("parallel",)),
    )(parts)


def _adamw(parts, w, m, v, *, name):
    p, depth, r, cdim = parts.shape
    tr = _pick_tile(r, 256)

    def body(p_ref, w_ref, m_ref, v_ref, g_out, d_out, m_out, v_out):
        g = p_ref[0].astype(F32)
        for k in range(1, p):
            g = g + p_ref[k].astype(F32)
        m2 = ADAM_B1 * m_ref[...] + (1.0 - ADAM_B1) * g
        v2 = ADAM_B2 * v_ref[...] + (1.0 - ADAM_B2) * (g * g)
        m_hat = m2 / (1.0 - ADAM_B1 ** ADAM_STEP)
        v_hat = v2 / (1.0 - ADAM_B2 ** ADAM_STEP)
        g_out[...] = g
        d_out[...] = -ADAM_LR * (m_hat / (jnp.sqrt(v_hat) + ADAM_EPS) + ADAM_WD * w_ref[...])
        m_out[...] = m2
        v_out[...] = v2

    blk = pl.BlockSpec((None, tr, cdim), lambda l, i: (l, i, 0))
    out = jax.ShapeDtypeStruct((depth, r, cdim), F32)
    return pl.pallas_call(
        body, name=name, grid=(depth, r // tr),
        in_specs=[pl.BlockSpec((p, None, tr, cdim), lambda l, i: (0, l, i, 0)), blk, blk, blk],
        out_specs=[blk, blk, blk, blk], out_shape=[out, out, out, out],
        compiler_params=_params(("parallel", "parallel")),
    )(parts, w, m, v)


BIG = ("w_in", "w_a_out", "w_b_out", "w_pool", "w_o", "w_up", "w_down")
SMALL_REPLICATED = ("b_in", "ln_v_g", "ln_v_b", "w_spatial", "b_spatial", "pool_scale", "ln1_g", "ln1_b", "b_up",
                    "conv_ffn_b", "ln2_g", "ln2_b")
SMALL_SHARDED = ("conv_a", "conv_ffn")
WEIGHTS = ("w_ada", "b_ada", "w_in", "b_in", "conv_a", "w_a_out", "ln_v_g", "ln_v_b", "w_spatial", "b_spatial",
           "w_b_out", "w_pool", "pool_scale", "w_o", "ln1_g", "ln1_b", "w_up", "b_up", "conv_ffn", "conv_ffn_b",
           "w_down", "ln2_g", "ln2_b")
LANES = 128


def _as_rows(flat, mult=8):
    n = flat.shape[0]
    pad = (-n) % (LANES * mult)
    if pad:
        flat = jnp.concatenate([flat, jnp.zeros((pad,), flat.dtype)])
    return flat.reshape(-1, LANES)


def _shard3(a):
    return a.reshape((-1,) + a.shape[-2:])


def kernel(x, c, w_ada, b_ada, w_in, b_in, conv_a, w_a_out, ln_v_g, ln_v_b, w_spatial, b_spatial, w_b_out, w_pool, pool_scale, w_o, ln1_g, ln1_b, w_up, b_up, conv_ffn, conv_ffn_b, w_down, ln2_g, ln2_b, loss_target, m_w_ada, m_b_ada, m_w_in, m_b_in, m_conv_a, m_w_a_out, m_ln_v_g, m_ln_v_b, m_w_spatial, m_b_spatial, m_w_b_out, m_w_pool, m_pool_scale, m_w_o, m_ln1_g, m_ln1_b, m_w_up, m_b_up, m_conv_ffn, m_conv_ffn_b, m_w_down, m_ln2_g, m_ln2_b, v_w_ada, v_b_ada, v_w_in, v_b_in, v_conv_a, v_w_a_out, v_ln_v_g, v_ln_v_b, v_w_spatial, v_b_spatial, v_w_b_out, v_w_pool, v_pool_scale, v_w_o, v_ln1_g, v_ln1_b, v_w_up, v_b_up, v_conv_ffn, v_conv_ffn_b, v_w_down, v_ln2_g, v_ln2_b):
    p = dict(locals())
    depth, d = w_in.shape[0], w_in.shape[1]
    alpha = (2 * depth) ** 0.25
    me = 4 * lax.axis_index("x") + 2 * lax.axis_index("y") + lax.axis_index("c")
    cidx = lax.axis_index("c").astype(jnp.int32).reshape(1)

    n_ca, n_cf = conv_a.size, conv_ffn.size
    packed = _as_rows(jnp.concatenate([c.reshape(-1), conv_a.reshape(-1), conv_ffn.reshape(-1)]))
    got = _allgather_vmem(packed, name="gather_cond").reshape(N_DEV, -1)
    c_all = got[:, :d]
    ct = c_all.T
    conv_a_full = jnp.transpose(got[:, d:d + n_ca].reshape((N_DEV,) + conv_a.shape), (1, 2, 0, 3)).reshape(depth, 3, -1)
    conv_ffn_full = jnp.transpose(got[:, d + n_ca:d + n_ca + n_cf].reshape((N_DEV,) + conv_ffn.shape),
                                  (1, 2, 0, 3)).reshape(depth, 3, -1)

    ns_ada = w_ada.shape[2]
    ada_part = _ada_fwd(c_all, w_ada, name="ada_fwd")
    ada_all = _allgather_vmem(_as_rows(ada_part.reshape(-1)), name="gather_ada")
    ada_all = ada_all.reshape(N_DEV, depth, N_DEV, ns_ada)
    ada_mine = lax.dynamic_index_in_dim(ada_all, me, axis=2, keepdims=False)
    ada = jnp.transpose(ada_mine, (1, 0, 2)).reshape(depth, -1) + b_ada
    ada = ada.reshape(depth, 6, d)

    shards = [{n: _shard3(p[n][l].astype(BF)) for n in BIG} for l in range(depth)]
    ws = [_layer_weights(l, ada, conv_a_full, conv_ffn_full, p) for l in range(depth)]
    ws[0]["w_in8"] = _gather_weights([shards[0]["w_in"]], name="gather_w_in0")[0][0]
    plan = _GatherPlan()
    four = ("w_a_out", "w_b_out", "w_pool", "w_o")
    for l in range(depth):
        t, prev, sh = "_l%d" % l, "_l%d" % (l - 1), shards[l]
        if l == 0:
            for n in four:
                plan.add(n + t, sh[n], "in_proj" + t, "mix_fwd" + t)
            plan.add("w_up8" + t, sh["w_up"], "in_proj" + t, "mix_fwd" + t)
            plan.add("wd4" + t, sh["w_down"], "mix_fwd" + t, "up_proj" + t)
        else:
            half = sh["w_in"].shape[1] // 2
            plan.add("w_in8_top" + t, sh["w_in"], "up_proj" + prev, None, rows=(0, half))
            plan.add("w_in8" + t, sh["w_in"], "ffn_fwd" + prev, "down_proj" + prev, rows=(half, half),
                     onto="w_in8_top" + t)
            for n in four:
                plan.add(n + t, sh[n], "down_proj" + prev, "in_proj" + t)
            plan.add("w_up8" + t, sh["w_up"], "in_proj" + t, "mix_fwd" + t)
            plan.add("wd4" + t, sh["w_down"], "in_proj" + t, "mix_fwd" + t)

    rs = _ReduceScatter(depth, cidx)
    loss_blk, grad_x, bigs, reds, d_ada = _local_step(x[0], loss_target[0], ws, alpha, plan, rs)
    loss = lax.psum(loss_blk[0, 0], ("x", "y", "c"))

    dada_all = _allgather_vmem(_as_rows(d_ada.reshape(-1)), name="gather_dada")
    dada_all = dada_all.reshape(N_DEV, -1, LANES)
    dflat = dada_all.reshape(N_DEV, depth, 6 * d)
    dmine = lax.dynamic_slice_in_dim(dflat, me * ns_ada, ns_ada, axis=2)
    gw_ada, gb_rows = _ada_bwd(ct, jnp.transpose(dmine, (1, 0, 2)), dada_all, name="ada_bwd")
    gb_ada = gb_rows.reshape(-1)[:depth * 6 * d].reshape(depth, 6 * d)

    out = {}
    for n in BIG:
        parts = rs.acc[n]
        view = (lambda a: jnp.swapaxes(a, 1, 2)) if n == "w_up" else (lambda a: a)
        shard_shape = view(p[n]).shape
        w3 = view(p[n]).reshape(depth, -1, shard_shape[-1])
        parts4 = parts.reshape((4,) + w3.shape)
        res = _adamw(parts4, w3, view(p["m_" + n]).reshape(w3.shape), view(p["v_" + n]).reshape(w3.shape),
                     name="adamw_" + n)
        out[n] = [view(r.reshape(shard_shape)) for r in res]
    out["w_ada"] = _adamw(gw_ada[None], w_ada, m_w_ada, v_w_ada, name="adamw_w_ada")

    order = SMALL_REPLICATED + SMALL_SHARDED
    n_rep = sum(p[n].size for n in SMALL_REPLICATED)
    n_pay = n_rep + N_DEV * (conv_a.size + conv_ffn.size)
    gsum = _sum_parts(rs.small_gathered.reshape(N_DEV, -1, LANES), name="sum_small").reshape(-1)[:n_pay]
    ga_full = gsum[n_rep:n_rep + depth * 3 * d].reshape(depth, 3, d)
    gf_full = gsum[n_rep + depth * 3 * d:].reshape(depth, 3, -1)
    ca_w, cf_w = conv_a.shape[2], conv_ffn.shape[2]
    g_ca = lax.dynamic_slice_in_dim(ga_full, me * ca_w, ca_w, axis=2)
    g_cf = lax.dynamic_slice_in_dim(gf_full, me * cf_w, cf_w, axis=2)
    names = ("b_ada",) + order
    gflat = _as_rows(jnp.concatenate([gb_ada.reshape(-1), gsum[:n_rep], g_ca.reshape(-1), g_cf.reshape(-1)]))
    pack = lambda pre: _as_rows(jnp.concatenate([p[pre + n].reshape(-1) for n in names]))
    res = _adamw(gflat[None, None], pack("")[None], pack("m_")[None], pack("v_")[None], name="adamw_small")
    off = 0
    for n in names:
        size = p[n].size
        out[n] = [r.reshape(-1)[off:off + size].reshape(p[n].shape) for r in res]
        off += size

    return (loss, grad_x[None]) + tuple(out[n][k] for k in range(4) for n in WEIGHTS)
```

```python
import functools

import jax
import jax.numpy as jnp
from jax import lax
from jax.experimental import pallas as pl
from jax.experimental.pallas import tpu as pltpu

F32 = jnp.float32
BF = jnp.bfloat16
MESH = pl.DeviceIdType.MESH

LN_EPS = 1e-5
POOL_WINDOWS = (2, 4, 8, 16)
GMLP_BLOCK = 128
CHUNK = 64
HALO = 16
ADAM_LR, ADAM_B1, ADAM_B2, ADAM_EPS, ADAM_WD, ADAM_STEP = 0.001, 0.9, 0.999, 1e-08, 0.01, 10
N_DEV = 8
VMEM_LIMIT = 56 * 1024 * 1024

GRAD_DTYPE = BF
ACT_DTYPE = BF

NN = ((1,), (0,))
NT = ((1,), (1,))
TN = ((0,), (0,))


def _params(sem=None, vmem=VMEM_LIMIT, **kw):
    if sem is not None:
        kw["dimension_semantics"] = sem
    return pltpu.CompilerParams(vmem_limit_bytes=vmem, **kw)


class _Phase:
    def __init__(self, ins, out_shapes, aliases, n_remote, n_local, build):
        self.ins, self.out_shapes, self.aliases = list(ins), list(out_shapes), dict(aliases)
        self.n_remote, self.n_local, self.build = n_remote, n_local, build
        self.results = None


def _pcall(body, args, *, name, grid, in_specs, out_specs, out_shape, scratch_shapes=(), sem=None, aliases=None,
           phases=()):
    aliases = dict(aliases or {})
    if not phases:
        return pl.pallas_call(
            body, name=name, grid=grid, in_specs=list(in_specs), out_specs=out_specs, out_shape=out_shape,
            scratch_shapes=list(scratch_shapes), input_output_aliases=aliases, compiler_params=_params(sem),
        )(*args)
    single = not isinstance(out_shape, (list, tuple))
    o_specs = [out_specs] if single else list(out_specs)
    o_shapes = [out_shape] if single else list(out_shape)
    n_in, n_out, n_scr = len(args), len(o_shapes), len(scratch_shapes)
    ex_args, ex_out, sems = [], [], []
    for ph in phases:
        for src, dst in ph.aliases.items():
            aliases[n_in + len(ex_args) + src] = n_out + len(ex_out) + dst
        ex_args += ph.ins
        ex_out += ph.out_shapes
        sems += [pltpu.SemaphoreType.DMA((max(ph.n_remote, 1),)), pltpu.SemaphoreType.DMA((max(ph.n_remote, 1),)),
                 pltpu.SemaphoreType.DMA((max(ph.n_local, 1),))]

    def wrapped(*refs):
        pos = n_in
        ph_in = []
        for ph in phases:
            ph_in.append(refs[pos:pos + len(ph.ins)])
            pos += len(ph.ins)
        base_out = refs[pos:pos + n_out]
        pos += n_out
        ph_out = []
        for ph in phases:
            ph_out.append(refs[pos:pos + len(ph.out_shapes)])
            pos += len(ph.out_shapes)
        base_scr = refs[pos:pos + n_scr]
        ph_sems = refs[pos + n_scr:]
        first = last = None
        for ax, n in enumerate(grid):
            pid = pl.program_id(ax)
            first = (pid == 0) if first is None else first & (pid == 0)
            last = (pid == n - 1) if last is None else last & (pid == n - 1)

        def ops(k):
            return phases[k].build(ph_in[k], ph_out[k], *ph_sems[3 * k:3 * k + 3])

        @pl.when(first)
        def _():
            for k in range(len(phases)):
                for cp in ops(k)["start"]:
                    cp.start()

        body(*refs[:n_in], *base_out, *base_scr)

        @pl.when(last)
        def _():
            for k in range(len(phases)):
                o = ops(k)
                for cp in o["recv"]:
                    cp.wait_recv()
                for cp in o["send"]:
                    cp.wait_send()
                for cp in o["local"]:
                    cp.wait()

    hbm = pl.BlockSpec(memory_space=pl.ANY)
    res = pl.pallas_call(
        wrapped, name=name, grid=grid, in_specs=list(in_specs) + [hbm] * len(ex_args),
        out_specs=o_specs + [hbm] * len(ex_out), out_shape=o_shapes + ex_out,
        scratch_shapes=list(scratch_shapes) + sems, input_output_aliases=aliases,
        compiler_params=_params(("arbitrary",) * len(grid)),
    )(*args, *ex_args)
    pos = n_out
    for ph in phases:
        ph.results = list(res[pos:pos + len(ph.out_shapes)])
        pos += len(ph.out_shapes)
    return res[0] if single else list(res[:n_out])


def _gelu_parts(x):
    k = 0.7978845608028654
    x2 = x * x
    t = jnp.tanh(k * (x + 0.044715 * (x2 * x)))
    cdf = 0.5 * (1.0 + t)
    dcdf = 0.5 * (1.0 - t * t) * (k * (1.0 + 3.0 * 0.044715 * x2))
    return x * cdf, cdf + x * dcdf


def _gelu(x):
    t = jnp.tanh(0.7978845608028654 * (x + 0.044715 * (x * x * x)))
    return x * (0.5 * (1.0 + t))


def _rowsum(v):
    return jnp.sum(v, axis=0, keepdims=True)


def _ln_stats(r):
    mu = jnp.mean(r, axis=-1, keepdims=True)
    xc = r - mu
    var = jnp.mean(xc * xc, axis=-1, keepdims=True)
    rstd = lax.rsqrt(var + LN_EPS)
    return xc * rstd, rstd


def _ln_bwd(dy, xhat, rstd, gain):
    dxh = dy * gain
    m1 = jnp.mean(dxh, axis=-1, keepdims=True)
    m2 = jnp.mean(dxh * xhat, axis=-1, keepdims=True)
    return rstd * (dxh - m1 - xhat * m2)


def _matmul(a, b, *, dn, grid, a_spec, b_spec, o_spec, out_shape, acc_shape, name, phases=(), into=None):
    nk = grid[2]
    direct = out_shape.dtype == F32

    def body(a_ref, b_ref, *rest):
        o_ref, scratch = (rest[1], rest[2:]) if into is not None else (rest[0], rest[1:])
        prod = lax.dot_general(a_ref[...], b_ref[...], (dn, ((), ())), preferred_element_type=F32)
        if nk == 1:
            o_ref[...] = prod.astype(o_ref.dtype)
            return
        acc = o_ref if direct else scratch[0]
        k = pl.program_id(2)

        @pl.when(k == 0)
        def _():
            acc[...] = prod

        @pl.when(k > 0)
        def _():
            acc[...] += prod

        if not direct:
            @pl.when(k == nk - 1)
            def _():
                o_ref[...] = acc[...].astype(o_ref.dtype)

    scratch = [] if (direct or nk == 1) else [pltpu.VMEM(acc_shape, F32)]
    args, in_specs, aliases = (a, b), [a_spec, b_spec], None
    if into is not None:
        args, in_specs, aliases = (a, b, into), in_specs + [pl.BlockSpec(memory_space=pl.ANY)], {2: 0}
    return _pcall(body, args, name=name, grid=grid, in_specs=in_specs, out_specs=o_spec,
                  out_shape=out_shape, scratch_shapes=scratch, sem=("parallel", "parallel", "arbitrary"),
                  aliases=aliases, phases=phases)


def _row_tile(m, want):
    t = min(m, want)
    assert m % t == 0
    return t


def _mm_rows(a, w, *, dn, name, out_dtype=F32, tm=2048):
    m, k = a.shape
    n = w.shape[1] if dn == NN else w.shape[0]
    tm = _row_tile(m, tm)
    return _matmul(
        a, w, dn=dn, grid=(m // tm, 1, 1), name=name,
        a_spec=pl.BlockSpec((tm, k), lambda i, j, kk: (i, 0)),
        b_spec=pl.BlockSpec(w.shape, lambda i, j, kk: (0, 0)),
        o_spec=pl.BlockSpec((tm, n), lambda i, j, kk: (i, 0)),
        out_shape=jax.ShapeDtypeStruct((m, n), out_dtype), acc_shape=(tm, n))


def _mm_tn(a, b, *, name, tk=4096):
    m, ka = a.shape
    n = b.shape[1]
    tk = _row_tile(m, tk)
    return _matmul(
        a, b, dn=TN, grid=(1, 1, m // tk), name=name,
        a_spec=pl.BlockSpec((tk, ka), lambda i, j, kk: (kk, 0)),
        b_spec=pl.BlockSpec((tk, n), lambda i, j, kk: (kk, 0)),
        o_spec=pl.BlockSpec((ka, n), lambda i, j, kk: (0, 0)),
        out_shape=jax.ShapeDtypeStruct((ka, n), GRAD_DTYPE), acc_shape=(ka, n))


def _mod_matmul(x, mod, w8, bias8, *, flat_out, name, tm=2048, phases=()):
    m, k = x.shape
    nb, _, ns = w8.shape
    tm = _row_tile(m, tm)

    def body(x_ref, mod_ref, w_ref, b_ref, o_ref, h_ref, hs):
        @pl.when(pl.program_id(1) == 0)
        def _():
            h = (x_ref[...] * mod_ref[0:1, :] + mod_ref[1:2, :]).astype(BF)
            hs[...] = h
            h_ref[...] = h

        o_ref[...] = (jnp.dot(hs[...], w_ref[...], preferred_element_type=F32) + b_ref[...]).astype(o_ref.dtype)

    if flat_out:
        o_spec = pl.BlockSpec((tm, ns), lambda i, j: (i, j))
        o_shape = jax.ShapeDtypeStruct((m, nb * ns), ACT_DTYPE)
    else:
        o_spec = pl.BlockSpec((None, tm, ns), lambda i, j: (j, i, 0))
        o_shape = jax.ShapeDtypeStruct((nb, m, ns), ACT_DTYPE)
    return _pcall(
        body, (x, mod, w8, bias8), name=name, grid=(m // tm, nb),
        in_specs=[pl.BlockSpec((tm, k), lambda i, j: (i, 0)),
                  pl.BlockSpec((2, k), lambda i, j: (0, 0)),
                  pl.BlockSpec((None, k, ns), lambda i, j: (j, 0, 0)),
                  pl.BlockSpec((None, 1, ns), lambda i, j: (j, 0, 0))],
        out_specs=[o_spec, pl.BlockSpec((tm, k), lambda i, j: (i, 0))],
        out_shape=[o_shape, jax.ShapeDtypeStruct((m, k), BF)],
        scratch_shapes=[pltpu.VMEM((tm, k), BF)], sem=("parallel", "arbitrary"), phases=phases)


def _seg_spec(tm, d, s):
    return pl.BlockSpec((tm, d), lambda i, s=s: (i, s))


def _prev_halo_spec(tm, d, s):
    hb = tm // HALO
    return pl.BlockSpec((HALO, d), lambda i, s=s: (jnp.maximum(i * hb - 1, 0), s))


def _next_halo_spec(tm, d, s, m):
    hb = tm // HALO
    last = m // HALO - 1
    return pl.BlockSpec((HALO, d), lambda i, s=s: (jnp.minimum((i + 1) * hb, last), s))


def _spatial_mix(wm_ref, src, dst, bias_ref, tm, d):
    for n in range(tm // GMLP_BLOCK):
        for g in range(d // GMLP_BLOCK):
            rs = slice(n * GMLP_BLOCK, (n + 1) * GMLP_BLOCK)
            cs = slice(g * GMLP_BLOCK, (g + 1) * GMLP_BLOCK)
            v = jnp.dot(wm_ref[g], src[rs, cs], preferred_element_type=F32)
            if bias_ref is not None:
                v = v + bias_ref[:, cs]
            dst[rs, cs] = v


def _mix_fwd(z, conv_a, lnv, wm, bias_full, *, name, tm=256, phases=()):
    m, d9 = z.shape
    d = d9 // 9
    tm = _row_tile(m, tm)
    grp = d // len(POOL_WINDOWS)

    def body(zb, zc, zx, zu, zv, zp, zc_h, zx_h, zp_h, ca_ref, lnv_ref, wm_ref, bias_ref,
             ua_ref, ub_ref, d_ref, ext, vn_s, mixed_s):
        i = pl.program_id(0)
        first = i == 0
        f32 = lambda r: r[...].astype(F32)
        pa = f32(zc) * f32(zx)
        ext[0:HALO, :] = jnp.where(first, 0.0, f32(zc_h) * f32(zx_h))
        ext[HALO:HALO + tm, :] = pa
        w = ca_ref[...]
        conv = w[0:1, :] * ext[pl.ds(HALO - 2, tm), :] + w[1:2, :] * ext[pl.ds(HALO - 1, tm), :] + w[2:3, :] * pa
        ua_ref[...] = (f32(zb) * conv).astype(BF)
        p = f32(zp)
        ext[0:HALO, :] = jnp.where(first, 0.0, f32(zp_h))
        ext[HALO:HALO + tm, :] = p
        t = (i * tm + lax.broadcasted_iota(jnp.int32, (tm, 1), 0) + 1).astype(F32)
        for k, win in enumerate(POOL_WINDOWS):
            cs = slice(k * grp, (k + 1) * grp)
            s = p[:, cs]
            for j in range(1, win):
                s = s + ext[pl.ds(HALO - j, tm), cs]
            d_ref[:, cs] = (s / jnp.minimum(t, float(win)) - p[:, cs]).astype(BF)
        gv = _gelu(f32(zv))
        vhat, _ = _ln_stats(gv)
        vn_s[...] = (vhat * lnv_ref[0:1, :] + lnv_ref[1:2, :]).astype(BF)
        _spatial_mix(wm_ref, vn_s, mixed_s, bias_ref, tm, d)
        ub_ref[...] = (_gelu(f32(zu)) * mixed_s[...]).astype(BF)

    full = lambda a: pl.BlockSpec(a.shape, lambda i: (0,) * a.ndim)
    out = jax.ShapeDtypeStruct((m, d), BF)
    o_spec = pl.BlockSpec((tm, d), lambda i: (i, 0))
    return _pcall(
        body, (z, z, z, z, z, z, z, z, z, conv_a, lnv, wm, bias_full), name=name, grid=(m // tm,),
        in_specs=[_seg_spec(tm, d, s) for s in range(6)] + [_prev_halo_spec(tm, d, s) for s in (1, 2, 5)]
        + [full(conv_a), full(lnv), full(wm), full(bias_full)],
        out_specs=[o_spec, o_spec, o_spec], out_shape=[out, out, out],
        scratch_shapes=[pltpu.VMEM((HALO + tm, d), F32), pltpu.VMEM((tm, d), BF), pltpu.VMEM((tm, d), F32)],
        sem=("arbitrary",), phases=phases)


def _pool_proj(dd, w_pool, *, dn, name, out_dtype=F32, tm=512):
    m, d = dd.shape
    ng, grp, _ = w_pool.shape
    tm = _row_tile(m, tm)
    return _matmul(
        dd, w_pool, dn=dn, grid=(m // tm, ng, 1), name=name,
        a_spec=pl.BlockSpec((tm, grp), lambda i, j, kk: (i, j)),
        b_spec=pl.BlockSpec((None, grp, grp), lambda i, j, kk: (j, 0, 0)),
        o_spec=pl.BlockSpec((tm, grp), lambda i, j, kk: (i, j)),
        out_shape=jax.ShapeDtypeStruct((m, d), out_dtype), acc_shape=(tm, grp))


def _merge(z, ya, yb, ycp, scale, *, name, tm=512):
    m, d = ya.shape
    tm = _row_tile(m, tm)

    def body(ga, gb, gc, ya_ref, yb_ref, yc_ref, sc_ref, o_ref):
        f32 = lambda r: r[...].astype(F32)
        o_ref[...] = (jax.nn.sigmoid(f32(ga)) * f32(ya_ref) + jax.nn.sigmoid(f32(gb)) * f32(yb_ref)
                      + jax.nn.sigmoid(f32(gc)) * (f32(yc_ref) * sc_ref[...])).astype(BF)

    row = pl.BlockSpec((tm, d), lambda i: (i, 0))
    return pl.pallas_call(
        body, name=name, grid=(m // tm,),
        in_specs=[_seg_spec(tm, d, 6), _seg_spec(tm, d, 7), _seg_spec(tm, d, 8), row, row, row,
                  pl.BlockSpec((1, d), lambda i: (0, 0))],
        out_specs=row, out_shape=jax.ShapeDtypeStruct((m, d), BF),
        compiler_params=_params(("parallel",)),
    )(z, z, z, ya, yb, ycp, scale)


def _resid_ln(xp, ys, vec, alpha, *, name, tm=512):
    m, d = xp.shape
    tm = _row_tile(m, tm)

    def body(xp_ref, ys_ref, v_ref, o_ref):
        xhat, _ = _ln_stats(alpha * xp_ref[...] + v_ref[0:1, :] * ys_ref[...])
        o_ref[...] = xhat * v_ref[1:2, :] + v_ref[2:3, :]

    row = pl.BlockSpec((tm, d), lambda i: (i, 0))
    return pl.pallas_call(
        body, name=name, grid=(m // tm,),
        in_specs=[row, row, pl.BlockSpec(vec.shape, lambda i: (0, 0))],
        out_specs=row, out_shape=jax.ShapeDtypeStruct((m, d), F32),
        compiler_params=_params(("parallel",)),
    )(xp, ys, vec)


def _ffn_fwd(up4, cw, cb, *, name, tm=512, phases=()):
    _, nj, m, fs = up4.shape
    tm = _row_tile(m, tm)
    hb = tm // HALO

    def body(up_ref, ah_ref, cw_ref, cb_ref, f_ref, ext):
        first = pl.program_id(1) == 0
        ext[0:HALO, :] = jnp.where(first, 0.0, ah_ref[...].astype(F32))
        ext[HALO:HALO + tm, :] = up_ref[0].astype(F32)
        w = cw_ref[...]
        w0, w1, w2, bias = w[0:1, :], w[1:2, :], w[2:3, :], cb_ref[...]
        rc = 16

        def step(c, carry):
            r0 = pl.multiple_of(c * rc, rc)
            win = ext[pl.ds(r0 + HALO - 8, rc + 8), :]
            a0, a1, a2 = win[8:8 + rc], pltpu.roll(win, 1, 0)[8:8 + rc], pltpu.roll(win, 2, 0)[8:8 + rc]
            ca = w0 * a2 + w1 * a1 + w2 * a0 + bias
            f_ref[pl.ds(r0, rc), :] = (_gelu(ca) * up_ref[1, pl.ds(r0, rc), :].astype(F32)).astype(BF)
            return carry

        lax.fori_loop(0, tm // rc, step, 0)

    return _pcall(
        body, (up4, up4, cw, cb), name=name, grid=(nj, m // tm),
        in_specs=[pl.BlockSpec((2, None, tm, fs), lambda j, i: (0, j, i, 0)),
                  pl.BlockSpec((None, None, HALO, fs), lambda j, i: (0, j, jnp.maximum(i * hb - 1, 0), 0)),
                  pl.BlockSpec((None, 3, fs), lambda j, i: (j, 0, 0)),
                  pl.BlockSpec((None, 1, fs), lambda j, i: (j, 0, 0))],
        out_specs=pl.BlockSpec((None, tm, fs), lambda j, i: (j, i, 0)),
        out_shape=jax.ShapeDtypeStruct((nj, m, fs), BF),
        scratch_shapes=[pltpu.VMEM((HALO + tm, fs), F32)], sem=("parallel", "arbitrary"), phases=phases)


def _down_proj(f4, wd4, *, name, tm=2048, phases=()):
    nj, m, fs = f4.shape
    d = wd4.shape[2]
    tm = _row_tile(m, tm)
    return _matmul(
        f4, wd4, dn=NN, grid=(m // tm, 1, nj), name=name,
        a_spec=pl.BlockSpec((None, tm, fs), lambda i, j, kk: (kk, i, 0)),
        b_spec=pl.BlockSpec((None, fs, d), lambda i, j, kk: (kk, 0, 0)),
        o_spec=pl.BlockSpec((tm, d), lambda i, j, kk: (i, 0)),
        out_shape=jax.ShapeDtypeStruct((m, d), F32), acc_shape=(tm, d), phases=phases)


def _loss_grad(y, tgt, *, name, tm=512):
    m, d = y.shape
    tm = _row_tile(m, tm)
    ni = m // tm

    def body(y_ref, t_ref, dy_ref, l_ref, acc):
        i = pl.program_id(0)
        e = y_ref[...] - t_ref[...]
        dy_ref[...] = e * (1.0 / d)
        part = jnp.sum((e * e).reshape(tm // 8, 8, d), axis=0)

        @pl.when(i == 0)
        def _():
            acc[...] = part

        @pl.when(i > 0)
        def _():
            acc[...] += part

        @pl.when(i == ni - 1)
        def _():
            l_ref[...] = jnp.full((8, 128), 0.5 / d, F32) * jnp.sum(acc[...])

    row = pl.BlockSpec((tm, d), lambda i: (i, 0))
    return pl.pallas_call(
        body, name=name, grid=(ni,), in_specs=[row, row],
        out_specs=[row, pl.BlockSpec((8, 128), lambda i: (0, 0))],
        out_shape=[jax.ShapeDtypeStruct((m, d), F32), jax.ShapeDtypeStruct((8, 128), F32)],
        scratch_shapes=[pltpu.VMEM((8, d), F32)],
        compiler_params=_params(("arbitrary",)),
    )(y, tgt)


def _resid_ln_bwd(dpart, dh, xmod, mvec, xp, ys, vec, alpha, *, name, tm=256, phases=()):
    m, d = dpart.shape
    tm = _row_tile(m, tm)
    has_dh = dh is not None
    has_ln = xp is not None

    def body(*refs):
        refs = list(refs)
        dpart_ref = refs.pop(0)
        if has_dh:
            dh_ref, xm_ref, mv_ref = refs.pop(0), refs.pop(0), refs.pop(0)
        if has_ln:
            xp_ref, ys_ref, v_ref = refs.pop(0), refs.pop(0), refs.pop(0)
            dys_ref, dxp_ref, red_ref = refs
        else:
            dx_ref, red_ref = refs
        i = pl.program_id(0)
        dtot = dpart_ref[...]
        rows = [jnp.zeros((1, d), F32)] * 5
        if has_dh:
            dhv = dh_ref[...]
            dtot = dtot + dhv * mv_ref[...]
            rows[0] = _rowsum(dhv * xm_ref[...])
            rows[1] = _rowsum(dhv)
        if has_ln:
            ys = ys_ref[...]
            gt = v_ref[0:1, :]
            xhat, rstd = _ln_stats(alpha * xp_ref[...] + gt * ys)
            rows[2] = _rowsum(dtot * xhat)
            rows[3] = _rowsum(dtot)
            dr = _ln_bwd(dtot, xhat, rstd, v_ref[1:2, :])
            rows[4] = _rowsum(dr * ys)
            dys_ref[...] = (dr * gt).astype(BF)
            dxp_ref[...] = alpha * dr
        else:
            dx_ref[...] = dtot
        red = jnp.concatenate(rows + [jnp.zeros((3, d), F32)], axis=0)

        @pl.when(i == 0)
        def _():
            red_ref[...] = red

        @pl.when(i > 0)
        def _():
            red_ref[...] += red

    row = pl.BlockSpec((tm, d), lambda i: (i, 0))
    vrow = lambda a: pl.BlockSpec(a.shape, lambda i: (0, 0))
    args, specs = [dpart], [row]
    if has_dh:
        args += [dh, xmod, mvec]
        specs += [row, row, vrow(mvec)]
    if has_ln:
        args += [xp, ys, vec]
        specs += [row, row, vrow(vec)]
        out_specs = [row, row, pl.BlockSpec((8, d), lambda i: (0, 0))]
        out_shape = [jax.ShapeDtypeStruct((m, d), BF), jax.ShapeDtypeStruct((m, d), F32),
                     jax.ShapeDtypeStruct((8, d), F32)]
    else:
        out_specs = [row, pl.BlockSpec((8, d), lambda i: (0, 0))]
        out_shape = [jax.ShapeDtypeStruct((m, d), F32), jax.ShapeDtypeStruct((8, d), F32)]
    return _pcall(body, args, name=name, grid=(m // tm,), in_specs=specs, out_specs=out_specs, out_shape=out_shape,
                  sem=("arbitrary",), phases=phases)


def _down_bwd(dy, wd4, *, name, tm=2048, phases=()):
    m, d = dy.shape
    nj, fs, _ = wd4.shape
    tm = _row_tile(m, tm)
    return _matmul(
        dy, wd4, dn=NT, grid=(m // tm, nj, 1), name=name,
        a_spec=pl.BlockSpec((tm, d), lambda i, j, kk: (i, 0)),
        b_spec=pl.BlockSpec((None, fs, d), lambda i, j, kk: (j, 0, 0)),
        o_spec=pl.BlockSpec((None, tm, fs), lambda i, j, kk: (j, i, 0)),
        out_shape=jax.ShapeDtypeStruct((nj, m, fs), ACT_DTYPE), acc_shape=(tm, fs), phases=phases)


def _tn_shards_lhs(f4, dy, *, name, tk=4096, phases=()):
    nj, m, fs = f4.shape
    d = dy.shape[1]
    tk = _row_tile(m, tk)
    return _matmul(
        f4, dy, dn=TN, grid=(nj, 1, m // tk), name=name,
        a_spec=pl.BlockSpec((None, tk, fs), lambda i, j, kk: (i, kk, 0)),
        b_spec=pl.BlockSpec((tk, d), lambda i, j, kk: (kk, 0)),
        o_spec=pl.BlockSpec((None, fs, d), lambda i, j, kk: (i, 0, 0)),
        out_shape=jax.ShapeDtypeStruct((nj, fs, d), GRAD_DTYPE), acc_shape=(fs, d), phases=phases)


def _tn_shards_rhs(h, d8, *, name, tk=2048, phases=()):
    m, k = h.shape
    nb, _, ns = d8.shape
    tk = _row_tile(m, tk)
    return _matmul(
        h, d8, dn=TN, grid=(nb, 1, m // tk), name=name,
        a_spec=pl.BlockSpec((tk, k), lambda i, j, kk: (kk, 0)),
        b_spec=pl.BlockSpec((None, tk, ns), lambda i, j, kk: (i, kk, 0)),
        o_spec=pl.BlockSpec((None, k, ns), lambda i, j, kk: (i, 0, 0)),
        out_shape=jax.ShapeDtypeStruct((nb, k, ns), GRAD_DTYPE), acc_shape=(k, ns), phases=phases)


def _tn_cols_rhs(h, dz, nb, *, name, tk=4096, phases=()):
    m, k = h.shape
    ns = dz.shape[1] // nb
    tk = _row_tile(m, tk)
    return _matmul(
        h, dz, dn=TN, grid=(nb, 1, m // tk), name=name,
        a_spec=pl.BlockSpec((tk, k), lambda i, j, kk: (kk, 0)),
        b_spec=pl.BlockSpec((tk, ns), lambda i, j, kk: (kk, i)),
        o_spec=pl.BlockSpec((None, k, ns), lambda i, j, kk: (i, 0, 0)),
        out_shape=jax.ShapeDtypeStruct((nb, k, ns), GRAD_DTYPE), acc_shape=(k, ns), phases=phases)


def _nt_shards(d8, w8, *, name, tm=2048, phases=()):
    nb, m, ns = d8.shape
    k = w8.shape[1]
    tm = _row_tile(m, tm)
    return _matmul(
        d8, w8, dn=NT, grid=(m // tm, 1, nb), name=name,
        a_spec=pl.BlockSpec((None, tm, ns), lambda i, j, kk: (kk, i, 0)),
        b_spec=pl.BlockSpec((None, k, ns), lambda i, j, kk: (kk, 0, 0)),
        o_spec=pl.BlockSpec((tm, k), lambda i, j, kk: (i, 0)),
        out_shape=jax.ShapeDtypeStruct((m, k), F32), acc_shape=(tm, k), phases=phases)


def _nt_cols(dz, w8, *, name, tm=2048, phases=(), tiles=None, into=None):
    m = dz.shape[0]
    nb, k, ns = w8.shape
    tm = _row_tile(m, tm)
    first, count = tiles if tiles is not None else (0, m // tm)
    return _matmul(
        dz, w8, dn=NT, grid=(count, 1, nb), name=name,
        a_spec=pl.BlockSpec((tm, ns), lambda i, j, kk: (i + first, kk)),
        b_spec=pl.BlockSpec((None, k, ns), lambda i, j, kk: (kk, 0, 0)),
        o_spec=pl.BlockSpec((tm, k), lambda i, j, kk: (i + first, 0)),
        out_shape=jax.ShapeDtypeStruct((m, k), F32), acc_shape=(tm, k), phases=phases, into=into)


def _tn_pool(dd, dyc, ng, *, name, tk=2048):
    m, d = dd.shape
    grp = d // ng
    tk = _row_tile(m, tk)
    return _matmul(
        dd, dyc, dn=TN, grid=(ng, 1, m // tk), name=name,
        a_spec=pl.BlockSpec((tk, grp), lambda i, j, kk: (kk, i)),
        b_spec=pl.BlockSpec((tk, grp), lambda i, j, kk: (kk, i)),
        o_spec=pl.BlockSpec((None, grp, grp), lambda i, j, kk: (i, 0, 0)),
        out_shape=jax.ShapeDtypeStruct((ng, grp, grp), GRAD_DTYPE), acc_shape=(grp, grp))


def _ffn_bwd(up4, df4, cw, cb, *, name, tm=512, phases=()):
    _, nj, m, fs = up4.shape
    tm = _row_tile(m, tm)
    hb = tm // HALO
    ni = m // tm
    last_hb = m // HALO - 1
    ext_rows = tm + 8

    rc = 16
    assert tm % rc == 0

    def body(up_ref, ap_ref, un_ref, df_ref, dfn_ref, cw_ref, cb_ref, dup_ref, red_ref, ext, dca_s, racc):
        i = pl.program_id(1)
        ext[0:HALO, :] = jnp.where(i == 0, 0.0, ap_ref[...].astype(F32))
        ext[HALO:HALO + tm, :] = up_ref[0].astype(F32)
        ext[HALO + tm:2 * HALO + tm, :] = un_ref[0].astype(F32)
        racc[...] = jnp.zeros_like(racc)
        w = cw_ref[...]
        w0, w1, w2, bias = w[0:1, :], w[1:2, :], w[2:3, :], cb_ref[...]

        def conv_taps(win, n):
            return (win[8:8 + n], pltpu.roll(win, 1, 0)[8:8 + n], pltpu.roll(win, 2, 0)[8:8 + n])

        def fold(v):
            return v[0:8] + v[8:16]

        def add_red(k, v8):
            racc[8 * k:8 * k + 8, :] += v8

        def first_pass(c, carry):
            r0 = pl.multiple_of(c * rc, rc)
            a0, a1, a2 = conv_taps(ext[pl.ds(r0 + HALO - 8, rc + 8), :], rc)
            act, dact = _gelu_parts(w0 * a2 + w1 * a1 + w2 * a0 + bias)
            dfc = df_ref[pl.ds(r0, rc), :].astype(F32)
            dca = dfc * up_ref[1, pl.ds(r0, rc), :].astype(F32) * dact
            dup_g = dfc * act
            dca_s[pl.ds(r0, rc), :] = dca
            dup_ref[1, pl.ds(r0, rc), :] = dup_g.astype(BF)
            for k, v in enumerate((dca * a2, dca * a1, dca * a0, dca, dup_g)):
                add_red(k if k < 4 else 5, fold(v))
            return carry

        lax.fori_loop(0, tm // rc, first_pass, 0)
        a0, a1, a2 = conv_taps(ext[HALO + tm - 8:HALO + tm + 8, :], 8)
        _, dact = _gelu_parts(w0 * a2 + w1 * a1 + w2 * a0 + bias)
        after = dfn_ref[...].astype(F32)[0:8, :] * un_ref[1].astype(F32)[0:8, :] * dact
        dca_s[tm:tm + 8, :] = jnp.where(i < ni - 1, after, 0.0)
        dca_s[tm + 8:tm + 16, :] = jnp.zeros((8, fs), F32)

        def second_pass(c, carry):
            r0 = pl.multiple_of(c * rc, rc)
            win = dca_s[pl.ds(r0, rc + 8), :]
            up1, up2 = pltpu.roll(win, rc + 7, 0)[0:rc], pltpu.roll(win, rc + 6, 0)[0:rc]
            dup_a = w2 * win[0:rc] + w1 * up1 + w0 * up2
            dup_ref[0, pl.ds(r0, rc), :] = dup_a.astype(BF)
            add_red(4, fold(dup_a))
            return carry

        lax.fori_loop(0, tm // rc, second_pass, 0)
        red = jnp.concatenate([_rowsum(racc[8 * k:8 * k + 8, :]) for k in range(6)] + [jnp.zeros((2, fs), F32)],
                              axis=0)

        @pl.when(i == 0)
        def _():
            red_ref[...] = red

        @pl.when(i > 0)
        def _():
            red_ref[...] += red

    nxt = lambda j, i: jnp.minimum((i + 1) * hb, last_hb)
    return _pcall(
        body, (up4, up4, up4, df4, df4, cw, cb), name=name, grid=(nj, ni), sem=("parallel", "arbitrary"), phases=phases,
        in_specs=[pl.BlockSpec((2, None, tm, fs), lambda j, i: (0, j, i, 0)),
                  pl.BlockSpec((None, None, HALO, fs), lambda j, i: (0, j, jnp.maximum(i * hb - 1, 0), 0)),
                  pl.BlockSpec((2, None, HALO, fs), lambda j, i: (0, j, nxt(j, i), 0)),
                  pl.BlockSpec((None, tm, fs), lambda j, i: (j, i, 0)),
                  pl.BlockSpec((None, HALO, fs), lambda j, i: (j, nxt(j, i), 0)),
                  pl.BlockSpec((None, 3, fs), lambda j, i: (j, 0, 0)),
                  pl.BlockSpec((None, 1, fs), lambda j, i: (j, 0, 0))],
        out_specs=[pl.BlockSpec((2, None, tm, fs), lambda j, i: (0, j, i, 0)),
                   pl.BlockSpec((None, 8, fs), lambda j, i: (j, 0, 0))],
        out_shape=[jax.ShapeDtypeStruct((2, nj, m, fs), BF), jax.ShapeDtypeStruct((nj, 8, fs), F32)],
        scratch_shapes=[pltpu.VMEM((2 * HALO + tm, fs), F32), pltpu.VMEM((tm + 16, fs), F32),
                        pltpu.VMEM((48, fs), F32)])


def _gate_bwd(dm, z, ya, yb, ycp, scale, *, name, tm=512):
    m, d = dm.shape
    tm = _row_tile(m, tm)

    def body(dm_ref, ga, gb, gc, ya_ref, yb_ref, yc_ref, sc_ref, dya_ref, dyb_ref, dyc_ref, dz_ref, red_ref):
        i = pl.program_id(0)
        f32 = lambda r: r[...].astype(F32)
        dmv = f32(dm_ref)
        sa, sb, sc = jax.nn.sigmoid(f32(ga)), jax.nn.sigmoid(f32(gb)), jax.nn.sigmoid(f32(gc))
        scale_v = sc_ref[...]
        ycp_v = f32(yc_ref)
        dya_ref[...] = (dmv * sa).astype(BF)
        dyb_ref[...] = (dmv * sb).astype(BF)
        dyc = dmv * sc
        dyc_ref[...] = (dyc * scale_v).astype(BF)
        dga = dmv * f32(ya_ref) * (sa * (1.0 - sa))
        dgb = dmv * f32(yb_ref) * (sb * (1.0 - sb))
        dgc = dmv * (ycp_v * scale_v) * (sc * (1.0 - sc))
        dz_ref[:, 0:d] = dga.astype(BF)
        dz_ref[:, d:2 * d] = dgb.astype(BF)
        dz_ref[:, 2 * d:3 * d] = dgc.astype(BF)
        red = jnp.concatenate([_rowsum(dyc * ycp_v), _rowsum(dga), _rowsum(dgb), _rowsum(dgc),
                               jnp.zeros((4, d), F32)], axis=0)

        @pl.when(i == 0)
        def _():
            red_ref[...] = red

        @pl.when(i > 0)
        def _():
            red_ref[...] += red

    row = pl.BlockSpec((tm, d), lambda i: (i, 0))
    obf = jax.ShapeDtypeStruct((m, d), BF)
    return pl.pallas_call(
        body, name=name, grid=(m // tm,),
        in_specs=[row, _seg_spec(tm, d, 6), _seg_spec(tm, d, 7), _seg_spec(tm, d, 8), row, row, row,
                  pl.BlockSpec((1, d), lambda i: (0, 0))],
        out_specs=[row, row, row, pl.BlockSpec((tm, 3 * d), lambda i: (i, 2)), pl.BlockSpec((8, d), lambda i: (0, 0))],
        out_shape=[obf, obf, obf, jax.ShapeDtypeStruct((m, 9 * d), BF), jax.ShapeDtypeStruct((8, d), F32)],
        compiler_params=_params(("arbitrary",)),
    )(dm, z, z, z, ya, yb, ycp, scale)


def _mix_bwd(dz, dua, dub, ddd, z, conv_a, lnv, wm, wmt, bias_full, mask, *, name, tm=128, phases=()):
    m, d = dua.shape
    tm = _row_tile(m, tm)
    ni = m // tm
    grp = d // len(POOL_WINDOWS)
    ng = d // GMLP_BLOCK
    ext_rows = tm + 8

    def body(dz_in, dua_ref, dub_ref, dd_ref, zb, zc, zx, zu, zv, zp, zc_h, zx_h, dua_n, zb_n, dd_n,
             ca_ref, lnv_ref, wm_ref, wmt_ref, bias_ref, mask_ref,
             dz_ref, red_ref, dws_ref, dbs_ref, ext, sh_s, vn_s, mixed_s, dmx_s, dvn_s, dbs_acc):
        del dz_in
        i = pl.program_id(0)
        rows = []
        f32 = lambda r: r[...].astype(F32)
        zbv, zcv, zxv = f32(zb), f32(zc), f32(zx)
        pa = zcv * zxv
        ext[0:HALO, :] = jnp.where(i == 0, 0.0, f32(zc_h) * f32(zx_h))
        ext[HALO:HALO + tm, :] = pa
        w = ca_ref[...]
        w0, w1, w2 = w[0:1, :], w[1:2, :], w[2:3, :]
        p1 = ext[pl.ds(HALO - 1, tm), :]
        p2 = ext[pl.ds(HALO - 2, tm), :]
        conv = w0 * p2 + w1 * p1 + w2 * pa
        duav = f32(dua_ref)
        dzb = duav * conv
        dca = duav * zbv
        dca_n = jnp.where(i < ni - 1, f32(dua_n)[0:8, :] * f32(zb_n)[0:8, :], 0.0)
        sh_s[0:tm, :] = dca
        sh_s[tm:tm + 8, :] = dca_n
        dpa = w2 * dca + w1 * sh_s[pl.ds(1, tm), :] + w0 * sh_s[pl.ds(2, tm), :]
        dzc = dpa * zxv
        dzx = dpa * zcv
        dz_ref[:, 0:d] = dzb.astype(BF)
        dz_ref[:, d:2 * d] = dzc.astype(BF)
        dz_ref[:, 2 * d:3 * d] = dzx.astype(BF)
        rows += [_rowsum(dzb), _rowsum(dzc), _rowsum(dzx)]
        dconv = [_rowsum(dca * p2), _rowsum(dca * p1), _rowsum(dca * pa)]
        zuv, zvv = f32(zu), f32(zv)
        gu, dgu_dz = _gelu_parts(zuv)
        gv, dgv_dz = _gelu_parts(zvv)
        vhat, rstd = _ln_stats(gv)
        gain = lnv_ref[0:1, :]
        vn_s[...] = (vhat * gain + lnv_ref[1:2, :]).astype(BF)
        _spatial_mix(wm_ref, vn_s, mixed_s, bias_ref, tm, d)
        dubv = f32(dub_ref)
        dzu = dubv * mixed_s[...] * dgu_dz
        dmixed = dubv * gu
        dmx_s[...] = dmixed.astype(BF)
        _spatial_mix(wmt_ref, dmx_s, dvn_s, None, tm, d)
        dvn = dvn_s[...]
        dzv = _ln_bwd(dvn, vhat, rstd, gain) * dgv_dz
        dz_ref[:, 3 * d:4 * d] = dzu.astype(BF)
        dz_ref[:, 4 * d:5 * d] = dzv.astype(BF)
        rows += [_rowsum(dzu), _rowsum(dzv)]
        dlnv = [_rowsum(dvn * vhat), _rowsum(dvn)]
        dbs_part = dmixed[0:GMLP_BLOCK, :]
        for n in range(1, tm // GMLP_BLOCK):
            dbs_part = dbs_part + dmixed[n * GMLP_BLOCK:(n + 1) * GMLP_BLOCK, :]
        ddv = f32(dd_ref)
        t = (i * tm + lax.broadcasted_iota(jnp.int32, (ext_rows + 8, 1), 0) + 1).astype(F32)
        dde = jnp.concatenate([ddv, jnp.where(i < ni - 1, f32(dd_n), 0.0)], axis=0)
        for k, win in enumerate(POOL_WINDOWS):
            cs = slice(k * grp, (k + 1) * grp)
            ext[0:tm + HALO, cs] = dde[:, cs] / jnp.minimum(t, float(win))
        dzp_parts = []
        for k, win in enumerate(POOL_WINDOWS):
            cs = slice(k * grp, (k + 1) * grp)
            s = ext[0:tm, cs]
            for j in range(1, win):
                s = s + ext[pl.ds(j, tm), cs]
            dzp_parts.append(s - ddv[:, cs])
        dzp = jnp.concatenate(dzp_parts, axis=1)
        dz_ref[:, 5 * d:6 * d] = dzp.astype(BF)
        rows += [_rowsum(dzp)]
        red = jnp.concatenate(rows + dconv + dlnv + [jnp.zeros((5, d), F32)], axis=0)

        @pl.when(i == 0)
        def _():
            red_ref[...] = red
            dbs_acc[...] = dbs_part
            dws_ref[...] = jnp.zeros_like(dws_ref)

        @pl.when(i > 0)
        def _():
            red_ref[...] += red
            dbs_acc[...] += dbs_part

        for n in range(tm // GMLP_BLOCK):
            for g in range(ng):
                rs = slice(n * GMLP_BLOCK, (n + 1) * GMLP_BLOCK)
                cs = slice(g * GMLP_BLOCK, (g + 1) * GMLP_BLOCK)
                dws_ref[g] += mask_ref[...] * lax.dot_general(
                    dmx_s[rs, cs], vn_s[rs, cs], (NT, ((), ())), preferred_element_type=F32)

        @pl.when(i == ni - 1)
        def _():
            lane = lax.broadcasted_iota(jnp.int32, (GMLP_BLOCK, GMLP_BLOCK), 1)
            out = jnp.zeros((GMLP_BLOCK, GMLP_BLOCK), F32)
            for g in range(ng):
                sg = jnp.sum(dbs_acc[:, g * GMLP_BLOCK:(g + 1) * GMLP_BLOCK], axis=1, keepdims=True)
                out = out + jnp.where(lane == g, sg, 0.0)
            dbs_ref[...] = out

    row = pl.BlockSpec((tm, d), lambda i: (i, 0))
    full = lambda a: pl.BlockSpec(a.shape, lambda i: (0,) * a.ndim)
    hb = tm // HALO
    last_hb = m // HALO - 1
    nrow = pl.BlockSpec((HALO, d), lambda i: (jnp.minimum((i + 1) * hb, last_hb), 0))
    return _pcall(
        body, (dz, dua, dub, ddd, z, z, z, z, z, z, z, z, dua, z, ddd, conv_a, lnv, wm, wmt, bias_full, mask),
        name=name, grid=(ni,), sem=("arbitrary",), aliases={0: 0}, phases=phases,
        in_specs=[pl.BlockSpec(memory_space=pl.ANY), row, row, row]
        + [_seg_spec(tm, d, s) for s in range(6)]
        + [_prev_halo_spec(tm, d, 1), _prev_halo_spec(tm, d, 2), nrow, _next_halo_spec(tm, d, 0, m), nrow]
        + [full(conv_a), full(lnv), full(wm), full(wmt), full(bias_full), full(mask)],
        out_specs=[pl.BlockSpec((tm, 6 * d), lambda i: (i, 0)), pl.BlockSpec((16, d), lambda i: (0, 0)),
                   full(wm), pl.BlockSpec((GMLP_BLOCK, GMLP_BLOCK), lambda i: (0, 0))],
        out_shape=[jax.ShapeDtypeStruct(dz.shape, BF), jax.ShapeDtypeStruct((16, d), F32),
                   jax.ShapeDtypeStruct(wm.shape, F32), jax.ShapeDtypeStruct((GMLP_BLOCK, GMLP_BLOCK), F32)],
        scratch_shapes=[pltpu.VMEM((2 * HALO + tm, d), F32), pltpu.VMEM((tm + 8, d), F32),
                        pltpu.VMEM((tm, d), BF), pltpu.VMEM((tm, d), F32), pltpu.VMEM((tm, d), BF),
                        pltpu.VMEM((tm, d), F32), pltpu.VMEM((GMLP_BLOCK, d), F32)])


REST = ("w_a_out", "w_b_out", "w_pool", "w_o", "w_up", "w_down")


def _remote(src, dst, ssem, rsem, k, to):
    return pltpu.make_async_remote_copy(src_ref=src, dst_ref=dst, send_sem=ssem.at[k], recv_sem=rsem.at[k],
                                        device_id=to, device_id_type=MESH)


def _gather_phase1(shards, rows=None, onto=None):
    n = len(shards)
    rows = rows or [None] * n
    onto = onto or [None] * n
    extra = [a for a in range(n) if onto[a] is not None]

    def build(ins, outs, ssem, rsem, lsem):
        x, y, c, chips = _place()
        me = 4 * x + 2 * y + c

        def src(a):
            return ins[a] if rows[a] is None else ins[a].at[:, pl.ds(*rows[a])]

        def dst(a, dev):
            return outs[a].at[:, dev] if rows[a] is None else outs[a].at[:, dev, pl.ds(*rows[a])]

        local = [pltpu.make_async_copy(src(a), dst(a, me), lsem.at[a]) for a in range(n)]
        sends, recvs = [], []
        for j, (cx, cy) in enumerate(chips):
            for a in range(n):
                sends.append(_remote(src(a), dst(a, me), ssem, rsem, 4 * a + 1 + j, (cx, cy, c)))
                recvs.append(_remote(src(a), dst(a, 4 * cx + 2 * cy + c), ssem, rsem, 4 * a + 1 + j, (cx, cy, c)))
        for a in range(n):
            sends.append(_remote(src(a), dst(a, me), ssem, rsem, 4 * a, (x, y, 1 - c)))
            recvs.append(_remote(src(a), dst(a, 4 * x + 2 * y + 1 - c), ssem, rsem, 4 * a, (x, y, 1 - c)))
        return dict(start=local + sends, recv=recvs, send=sends, local=local)

    outs = [jax.ShapeDtypeStruct((s.shape[0], N_DEV) + s.shape[1:], s.dtype) for s in shards]
    return _Phase(list(shards) + [onto[a] for a in extra], outs, {n + k: a for k, a in enumerate(extra)},
                  4 * n, n, build)


def _gather_phase2(fulls):
    n = len(fulls)

    def build(ins, outs, ssem, rsem, lsem):
        x, y, c, chips = _place()
        sends, recvs = [], []
        for j, (cx, cy) in enumerate(chips):
            for a in range(n):
                mine, theirs = 4 * cx + 2 * cy + c, 4 * cx + 2 * cy + 1 - c
                sends.append(_remote(ins[a].at[:, mine], outs[a].at[:, mine], ssem, rsem, 3 * a + j, (x, y, 1 - c)))
                recvs.append(_remote(ins[a].at[:, theirs], outs[a].at[:, theirs], ssem, rsem, 3 * a + j, (x, y, 1 - c)))
        return dict(start=sends, recv=recvs, send=sends, local=[])

    outs = [jax.ShapeDtypeStruct(f.shape, f.dtype) for f in fulls]
    return _Phase(fulls, outs, {a: a for a in range(n)}, 3 * n, 0, build)


def _pair_phase(grads):
    n = len(grads)

    def build(ins, outs, ssem, rsem, lsem):
        x, y, c, _ = _place()
        cps = [_remote(ins[a].at[:, 2 * q + (1 - c)], outs[a].at[q], ssem, rsem, 4 * a + q, (x, y, 1 - c))
               for a in range(n) for q in range(4)]
        return dict(start=cps, recv=cps, send=cps, local=[])

    outs = [jax.ShapeDtypeStruct((4, g.shape[0]) + g.shape[2:], g.dtype) for g in grads]
    return _Phase(grads, outs, {}, 4 * n, 0, build)


def _chip_phase(bufs, accs, l, depth):
    n = len(bufs)
    has = accs is not None

    def build(ins, outs, ssem, rsem, lsem):
        x, y, c, chips = _place()
        myq = 2 * x + y
        local, sends, recvs = [], [], []
        for a in range(n):
            local.append(pltpu.make_async_copy(ins[a].at[myq], outs[a].at[myq, l], lsem.at[a]))
            for j, (cx, cy) in enumerate(chips):
                q = 2 * cx + cy
                sends.append(_remote(ins[a].at[q], outs[a].at[myq, l], ssem, rsem, 3 * a + j, (cx, cy, c)))
                recvs.append(_remote(ins[a].at[q], outs[a].at[q, l], ssem, rsem, 3 * a + j, (cx, cy, c)))
        return dict(start=local + sends, recv=recvs, send=sends, local=local)

    outs = [jax.ShapeDtypeStruct((4, depth) + b.shape[1:], b.dtype) for b in bufs]
    return _Phase(list(bufs) + (list(accs) if has else []), outs, {n + a: a for a in range(n)} if has else {},
                  3 * n, n, build)


def _grad_chunks(n, g):
    if n == "w_pool":
        return g.reshape(g.shape[0], N_DEV, g.shape[1] // N_DEV, g.shape[2])
    if n in ("w_in", "w_up"):
        return g[None]
    return g.reshape(1, N_DEV, -1, g.shape[-1])


class _ReduceScatter:
    def __init__(self, depth, cidx):
        self.depth, self.cidx, self.acc, self.count = depth, cidx, {}, 0
        self.small_gathered = None
        self.presummed = None

    def pair(self, names, grads):
        return _pair_phase([_grad_chunks(n, grads[n]) for n in names])

    def sums(self, names, grads, phase):
        out = []
        for n, r1 in zip(names, phase.results):
            out.append(_pair_sum(_grad_chunks(n, grads[n]), r1, self.cidx, name="rs_sum_%d" % self.count))
            self.count += 1
        return out

    def chip(self, names, bufs, l):
        accs = [self.acc[n] for n in names] if names[0] in self.acc else None
        return _chip_phase(bufs, accs, l, self.depth)

    def done(self, names, phase):
        for n, r in zip(names, phase.results):
            self.acc[n] = r


def _rest_views(fulls, d):
    a_out, b_out, pool, o, up, down = fulls
    grp = d // len(POOL_WINDOWS)
    return dict(w_a_out=a_out.reshape(d, d), w_b_out=b_out.reshape(d, d), w_o=o.reshape(d, d),
                w_pool=pool.reshape(len(POOL_WINDOWS), grp, grp), w_up8=up[0],
                wd4=down.reshape(N_DEV // 2, -1, d))


class _GatherPlan:
    def __init__(self):
        self.jobs, self.part, self.full = [], {}, {}

    def add(self, key, shard, first, second, rows=None, onto=None):
        self.jobs.append((key, shard, first, second, rows, onto))

    def phases(self, name):
        j1 = [j for j in self.jobs if j[2] == name]
        j2 = [j for j in self.jobs if j[3] == name]
        tagged = []
        if j1:
            onto = [self.part[j[5]] if j[5] else None for j in j1]
            tagged.append((self.part, j1, _gather_phase1([j[1] for j in j1], [j[4] for j in j1], onto)))
        if j2:
            tagged.append((self.full, j2, _gather_phase2([self.part[j[0]] for j in j2])))
        return tagged

    @staticmethod
    def collect(tagged):
        for store, jobs, phase in tagged:
            for j, r in zip(jobs, phase.results):
                store[j[0]] = r


def _layer_fwd(x, w, alpha, tag, plan=None):
    d = x.shape[1]
    grp = d // len(POOL_WINDOWS)

    def carried(kernel, *args, name, **kw):
        tagged = plan.phases(name) if plan else []
        out = kernel(*args, name=name, phases=[t[2] for t in tagged], **kw)
        _GatherPlan.collect(tagged)
        return out

    def weight(n, shape):
        return plan.full[n + tag].reshape(shape) if plan else w[n]

    z, h = carried(_mod_matmul, x, w["mod1"], w["w_in8"], w["b_in8"], flat_out=True, name="in_proj" + tag)
    ua, ub, dd = carried(_mix_fwd, z, w["conv_a"], w["lnv"], w["wm"], w["bias_full"], name="mix_fwd" + tag)
    w["w_a_out"], w["w_b_out"], w["w_o"] = (weight(n, (d, d)) for n in ("w_a_out", "w_b_out", "w_o"))
    w["w_pool"] = weight("w_pool", (len(POOL_WINDOWS), grp, grp))
    w["w_up8"] = weight("w_up8", (N_DEV, d, -1))
    ya = _mm_rows(ua, w["w_a_out"], dn=NN, name="a_out" + tag, out_dtype=ACT_DTYPE)
    yb = _mm_rows(ub, w["w_b_out"], dn=NN, name="b_out" + tag, out_dtype=ACT_DTYPE)
    ycp = _pool_proj(dd, w["w_pool"], dn=NN, name="pool_proj" + tag, out_dtype=ACT_DTYPE)
    merged = _merge(z, ya, yb, ycp, w["pool_scale"], name="merge" + tag)
    o = _mm_rows(merged, w["w_o"], dn=NN, name="o_proj" + tag)
    x1 = _resid_ln(x, o, w["ln1"], alpha, name="ln1" + tag)
    up8, h2 = carried(_mod_matmul, x1, w["mod2"], w["w_up8"], w["b_up8"], flat_out=False, name="up_proj" + tag)
    up4 = up8.reshape((2, up8.shape[0] // 2) + up8.shape[1:])
    f4 = carried(_ffn_fwd, up4, w["cw"], w["cb"], name="ffn_fwd" + tag)
    w["wd4"] = weight("wd4", (N_DEV // 2, -1, d))
    y2 = carried(_down_proj, f4, w["wd4"], name="down_proj" + tag)
    x2 = _resid_ln(x1, y2, w["ln2"], alpha, name="ln2" + tag)
    saved = dict(x=x, z=z, h=h, ua=ua, ub=ub, dd=dd, ya=ya, yb=yb, ycp=ycp, merged=merged, o=o, x1=x1,
                 up4=up4, h2=h2, f4=f4, y2=y2)
    return x2, saved


def _layer_bwd(dpart, dh_above, xmod_above, m_above, w, s, alpha, tag, l=0, above=None, rs=None, upper_reds=()):
    first, rest = ("w_in",), REST
    ph = lambda p: [p] if p is not None else ()
    pre = rs.presummed if rs is not None else None
    r1a = rs.pair(first, above) if above and not pre else None
    dy2, dx1p, red2 = _resid_ln_bwd(dpart, dh_above, xmod_above, m_above, s["x1"], s["y2"], w["ln2"], alpha,
                                    name="ln2_bwd" + tag, phases=ph(r1a))
    r1b = rs.pair(rest, above) if above and not pre else None
    df4 = _down_bwd(dy2, w["wd4"], name="down_bwd" + tag, phases=ph(r1b))
    gw_down4 = _tn_shards_lhs(s["f4"], dy2, name="gw_down" + tag)
    r3a = r3b = r3c = None
    if above:
        if pre:
            bufs, rs.presummed = pre, None
        else:
            bufs = dict(zip(first + rest, rs.sums(first, above, r1a) + rs.sums(rest, above, r1b)))
        light = tuple(n for n in rest if n != "w_up")
        r3a = rs.chip(first, [bufs[n] for n in first], l + 1)
    dup4, redf = _ffn_bwd(s["up4"], df4, w["cw"], w["cb"], name="ffn_bwd" + tag, phases=ph(r3a))
    dup8 = dup4.reshape((dup4.shape[0] * dup4.shape[1],) + dup4.shape[2:])
    if above:
        rs.done(first, r3a)
        r3b = rs.chip(("w_up",), [bufs["w_up"]], l + 1)
    gw_up8 = _tn_shards_lhs(dup8, s["h2"], name="gw_up" + tag, phases=ph(r3b))
    if above:
        rs.done(("w_up",), r3b)
        r3c = rs.chip(light, [bufs[n] for n in light], l + 1)
    own = rs is not None and l == 0
    big = dict(w_up=gw_up8, w_down=gw_down4)
    early = ("w_down", "w_up")
    o1 = rs.pair(early, big) if own else None
    dh2 = _nt_shards(dup8, w["w_up8"], name="up_bwd" + tag, phases=list(ph(r3c)) + list(ph(o1)))
    if above:
        rs.done(light, r3c)
    if own:
        sb_o = rs.sums(early, big, o1)
    do, dxp, red1 = _resid_ln_bwd(dx1p, dh2, s["x1"], w["mod2"][0:1], s["x"], s["o"], w["ln1"], alpha,
                                  name="ln1_bwd" + tag)
    dm = _mm_rows(do, w["w_o"], dn=NT, name="o_bwd" + tag, out_dtype=ACT_DTYPE)
    big["w_o"] = _mm_tn(s["merged"], do, name="gw_o" + tag)
    dya, dyb, dyc, dz, redg = _gate_bwd(dm, s["z"], s["ya"], s["yb"], s["ycp"], w["pool_scale"], name="gate_bwd" + tag)
    dua = _mm_rows(dya, w["w_a_out"], dn=NT, name="a_out_bwd" + tag, out_dtype=ACT_DTYPE)
    dub = _mm_rows(dyb, w["w_b_out"], dn=NT, name="b_out_bwd" + tag, out_dtype=ACT_DTYPE)
    ddd = _pool_proj(dyc, w["w_pool"], dn=NT, name="pool_bwd" + tag, out_dtype=ACT_DTYPE)
    big["w_a_out"] = _mm_tn(s["ua"], dya, name="gw_a_out" + tag)
    big["w_b_out"] = _mm_tn(s["ub"], dyb, name="gw_b_out" + tag)
    big["w_pool"] = _tn_pool(s["dd"], dyc, w["w_pool"].shape[0], name="gw_pool" + tag)
    o3 = rs.chip(early, sb_o, l) if own else None
    dz, redm, dws, dbs = _mix_bwd(dz, dua, dub, ddd, s["z"], w["conv_a"], w["lnv"], w["wm"], w["wmt"],
                                  w["bias_full"], w["mask"], name="mix_bwd" + tag, phases=ph(o3))
    reds = dict(red2=red2, redf=redf, red1=red1, redg=redg, redm=redm, dws=dws, dbs=dbs)
    mid = ("w_o", "w_a_out", "w_b_out", "w_pool")
    o1b = sg1 = None
    if own:
        rs.done(early, o3)
        o1b = rs.pair(mid, big)
        sg1 = _gather_phase1([_small_payload([reds] + list(upper_reds))])
    big["w_in"] = _tn_cols_rhs(s["h"], dz, w["w_in8"].shape[0], name="gw_in" + tag,
                               phases=[o1b, sg1] if own else ())
    pending = None
    if own:
        sb_m = rs.sums(mid, big, o1b)
        o1c, o3b, sg2 = rs.pair(first, big), rs.chip(mid, sb_m, l), _gather_phase2(sg1.results)
        split_tm = 1024
        n_tiles = dz.shape[0] // _row_tile(dz.shape[0], split_tm)
        dh = _nt_cols(dz, w["w_in8"], name="in_bwd" + tag + "_a", phases=[o1c, o3b, sg2], tm=split_tm, tiles=(0, 1))
        rs.done(mid, o3b)
        rs.small_gathered = sg2.results[0]
        o3c = rs.chip(first, rs.sums(first, big, o1c), l)
        if n_tiles > 1:
            dh = _nt_cols(dz, w["w_in8"], name="in_bwd" + tag + "_b", phases=[o3c], tm=split_tm,
                          tiles=(1, n_tiles - 1), into=dh)
            rs.done(first, o3c)
        else:
            pending = (first, o3c)
    elif rs is not None:
        pa, pb = rs.pair(first, big), rs.pair(rest, big)
        dh = _nt_cols(dz, w["w_in8"], name="in_bwd" + tag, phases=[pa, pb])
        rs.presummed = dict(zip(first + rest, rs.sums(first, big, pa) + rs.sums(rest, big, pb)))
    else:
        dh = _nt_cols(dz, w["w_in8"], name="in_bwd" + tag)
    return dxp, dh, big, reds, pending


def _local_step(x, tgt, ws, alpha, plan=None, rs=None):
    depth = len(ws)
    saved = []
    y = x
    for l in range(depth):
        if plan and l > 0:
            ws[l]["w_in8"] = plan.full["w_in8_l%d" % l][0]
        y, s = _layer_fwd(y, ws[l], alpha, "_l%d" % l, plan)
        saved.append(s)
    dpart, loss_blk = _loss_grad(y, tgt, name="loss_grad")
    dh = xmod = mvec = above = pending = None
    bigs, reds = [None] * depth, [None] * depth
    for l in reversed(range(depth)):
        dpart, dh, bigs[l], reds[l], pending = _layer_bwd(dpart, dh, xmod, mvec, ws[l], saved[l], alpha, "_l%d" % l,
                                                          l, above if rs else None, rs, reds[l + 1:])
        xmod, mvec, above = saved[l]["x"], ws[l]["mod1"][0:1], bigs[l]
    grad_x, red0 = _resid_ln_bwd(dpart, dh, xmod, mvec, None, None, None, alpha, name="in_bwd_tail",
                                 phases=[pending[1]] if pending else ())
    if pending:
        rs.done(*pending)
    d_ada = []
    for l in range(depth):
        below = red0 if l == 0 else reds[l - 1]["red2"]
        r1, r2 = reds[l]["red1"], reds[l]["red2"]
        d_ada.append(jnp.stack([below[1], below[0], r1[4], r1[1], r1[0], r2[4]]))
    return loss_blk, grad_x, bigs, reds, jnp.stack(d_ada)


def _small_grads(r):
    redm, redg, redf = r["redm"], r["redg"], r["redf"]
    ng = r["dws"].shape[0]
    return dict(
        b_in=jnp.concatenate([redm[0:6], redg[1:4]], axis=0).reshape(-1),
        conv_a=redm[6:9], ln_v_g=redm[9], ln_v_b=redm[10],
        w_spatial=r["dws"], b_spatial=r["dbs"][:, :ng].T,
        pool_scale=redg[0], ln1_g=r["red1"][2], ln1_b=r["red1"][3],
        b_up=jnp.concatenate([redf[:, 4, :].reshape(-1), redf[:, 5, :].reshape(-1)]),
        conv_ffn=jnp.transpose(redf[:, 0:3, :], (1, 0, 2)).reshape(3, -1), conv_ffn_b=redf[:, 3, :].reshape(-1),
        ln2_g=r["red2"][2], ln2_b=r["red2"][3])


def _small_payload(reds):
    smalls = [_small_grads(r) for r in reds]
    order = SMALL_REPLICATED + SMALL_SHARDED
    flat = jnp.concatenate([smalls[l][n].reshape(-1) for n in order for l in range(len(reds))])
    return _as_rows(flat)[None]


def _layer_weights(l, ada, conv_a, conv_ffn, p):
    sh1, sc1, gt1, sh2, sc2, gt2 = (ada[l, k][None, :] for k in range(6))
    nb = N_DEV
    fs = p["b_up"].shape[1] // nb
    nj = nb // 2
    pos = jnp.arange(GMLP_BLOCK)
    allowed = (pos[None, :] // CHUNK) <= (pos[:, None] // CHUNK)
    wmask = jnp.where(allowed[None], p["w_spatial"][l], 0.0)
    return dict(
        mod1=jnp.concatenate([1.0 + sc1, sh1]), mod2=jnp.concatenate([1.0 + sc2, sh2]),
        ln1=jnp.concatenate([gt1, p["ln1_g"][l][None], p["ln1_b"][l][None]]),
        ln2=jnp.concatenate([gt2, p["ln2_g"][l][None], p["ln2_b"][l][None]]),
        b_in8=p["b_in"][l].reshape(N_DEV, 1, -1), b_up8=p["b_up"][l].reshape(nb, 1, fs),
        conv_a=conv_a[l], lnv=jnp.stack([p["ln_v_g"][l], p["ln_v_b"][l]]),
        wm=wmask.astype(BF), wmt=jnp.transpose(wmask, (0, 2, 1)).astype(BF),
        bias_full=jnp.repeat(p["b_spatial"][l].T, GMLP_BLOCK, axis=1), mask=allowed.astype(F32),
        pool_scale=p["pool_scale"][l][None],
        cw=jnp.transpose(conv_ffn[l].reshape(3, nj, fs), (1, 0, 2)), cb=p["conv_ffn_b"][l].reshape(nj, 1, fs))


ANY = pl.BlockSpec(memory_space=pl.ANY)


def _place():
    x, y, c = lax.axis_index("x"), lax.axis_index("y"), lax.axis_index("c")
    chips = [(1 - x, y), (x, 1 - y), (1 - x, 1 - y)]
    return x, y, c, chips


def _allgather_vmem(xs, *, name):
    r, cdim = xs.shape

    def body(x_ref, out_ref, send_sems, recv_sems, local_sem):
        x, y, c, chips = _place()
        me, sibling = (x, y, c), (x, y, 1 - c)

        def rows(px, py, pc):
            return out_ref.at[pl.ds((4 * px + 2 * py + pc) * r, r), :]

        def copy(k, block, to, src=None):
            return pltpu.make_async_remote_copy(
                src_ref=rows(*block) if src is None else src, dst_ref=rows(*block),
                send_sem=send_sems.at[k], recv_sem=recv_sems.at[k], device_id=to, device_id_type=MESH)

        mine = pltpu.make_async_copy(x_ref, rows(*me), local_sem)
        mine.start()
        first = [copy(0, me, sibling, src=x_ref)]
        first += [copy(1 + j, me, (*chip, c), src=x_ref) for j, chip in enumerate(chips)]
        for cp in first:
            cp.start()
        passed = [copy(4 + j, (*chip, c), sibling) for j, chip in enumerate(chips)]
        for j, chip in enumerate(chips):
            copy(1 + j, (*chip, c), me).wait_recv()
            passed[j].start()
        copy(0, sibling, me).wait_recv()
        for j, chip in enumerate(chips):
            copy(4 + j, (*chip, 1 - c), me).wait_recv()
        for cp in first + passed:
            cp.wait_send()
        mine.wait()

    return pl.pallas_call(
        body, name=name, out_shape=jax.ShapeDtypeStruct((N_DEV * r, cdim), xs.dtype),
        in_specs=[pl.BlockSpec(memory_space=pltpu.VMEM)], out_specs=pl.BlockSpec(memory_space=pltpu.VMEM),
        scratch_shapes=[pltpu.SemaphoreType.DMA((7,)), pltpu.SemaphoreType.DMA((7,)), pltpu.SemaphoreType.DMA],
        compiler_params=_params(),
    )(xs)


def _gather_weights(shards, *, name):
    n = len(shards)

    def body(*refs):
        ins, outs = refs[:n], refs[n:2 * n]
        send_sems, recv_sems, local_sems = refs[2 * n:]
        x, y, c, chips = _place()
        me, sibling = (x, y, c), (x, y, 1 - c)

        def slot(a, px, py, pc):
            return outs[a].at[:, 4 * px + 2 * py + pc]

        def copy(a, k, block, to, src=None):
            return pltpu.make_async_remote_copy(
                src_ref=slot(a, *block) if src is None else src, dst_ref=slot(a, *block),
                send_sem=send_sems.at[7 * a + k], recv_sem=recv_sems.at[7 * a + k], device_id=to,
                device_id_type=MESH)

        mine = [pltpu.make_async_copy(ins[a], slot(a, *me), local_sems.at[a]) for a in range(n)]
        for cp in mine:
            cp.start()
        first = []
        for j, chip in enumerate(chips):
            first += [copy(a, 1 + j, me, (*chip, c), src=ins[a]) for a in range(n)]
        first += [copy(a, 0, me, sibling, src=ins[a]) for a in range(n)]
        for cp in first:
            cp.start()
        passed = []
        for j, chip in enumerate(chips):
            for a in range(n):
                copy(a, 1 + j, (*chip, c), me).wait_recv()
                fwd = copy(a, 4 + j, (*chip, c), sibling)
                fwd.start()
                passed.append(fwd)
        for a in range(n):
            copy(a, 0, sibling, me).wait_recv()
        for j, chip in enumerate(chips):
            for a in range(n):
                copy(a, 4 + j, (*chip, 1 - c), me).wait_recv()
        for cp in first + passed:
            cp.wait_send()
        for cp in mine:
            cp.wait()

    out_shape = [jax.ShapeDtypeStruct((s.shape[0], N_DEV) + s.shape[1:], s.dtype) for s in shards]
    return pl.pallas_call(
        body, name=name, out_shape=out_shape, in_specs=[ANY] * n, out_specs=[ANY] * n,
        scratch_shapes=[pltpu.SemaphoreType.DMA((7 * n,)), pltpu.SemaphoreType.DMA((7 * n,)),
                        pltpu.SemaphoreType.DMA((n,))],
        compiler_params=_params(),
    )(*shards)


def _pick_tile(r, cap):
    best = None
    for t in range(8, min(r, cap) + 1, 8):
        if r % t == 0:
            best = t
    return best if best is not None else r


def _pair_sum(g, r1, cidx, *, name):
    p, _, r, cdim = g.shape
    tr = _pick_tile(r, 1024)

    def body(c_ref, g_ref, r_ref, o_ref):
        del c_ref
        o_ref[...] = (g_ref[...].astype(F32) + r_ref[...].astype(F32)).astype(BF)

    grid_spec = pltpu.PrefetchScalarGridSpec(
        num_scalar_prefetch=1, grid=(4, r // tr),
        in_specs=[pl.BlockSpec((p, None, tr, cdim), lambda q, i, c: (0, 2 * q + c[0], i, 0)),
                  pl.BlockSpec((None, p, tr, cdim), lambda q, i, c: (q, 0, i, 0))],
        out_specs=pl.BlockSpec((None, p, tr, cdim), lambda q, i, c: (q, 0, i, 0)))
    return pl.pallas_call(
        body, name=name, grid_spec=grid_spec, out_shape=jax.ShapeDtypeStruct((4, p, r, cdim), BF),
        compiler_params=_params(("arbitrary", "arbitrary")),
    )(cidx, g, r1)


def _ada_fwd(c_all, w_ada, *, name):
    depth, d, ns = w_ada.shape
    nb = c_all.shape[0]

    def body(c_ref, w_ref, o_ref):
        cv = c_ref[...]
        act = cv * jax.nn.sigmoid(cv)
        o_ref[...] = jnp.dot(act, w_ref[...], preferred_element_type=F32, precision=lax.Precision.HIGHEST)

    return pl.pallas_call(
        body, name=name, grid=(depth,),
        in_specs=[pl.BlockSpec((nb, d), lambda l: (0, 0)), pl.BlockSpec((None, d, ns), lambda l: (l, 0, 0))],
        out_specs=pl.BlockSpec((None, nb, ns), lambda l: (l, 0, 0)),
        out_shape=jax.ShapeDtypeStruct((depth, nb, ns), F32), compiler_params=_params(("parallel",)),
    )(c_all, w_ada)


def _ada_bwd(ct, dmine, dall, *, name):
    depth, nb, ns = dmine.shape
    d = ct.shape[0]

    def body(ct_ref, dm_ref, da_ref, gw_ref, gb_ref):
        cv = ct_ref[...]
        act = cv * jax.nn.sigmoid(cv)
        gw_ref[...] = jnp.dot(act, dm_ref[...], preferred_element_type=F32, precision=lax.Precision.HIGHEST)
        s = da_ref[0]
        for b in range(1, nb):
            s = s + da_ref[b]
        gb_ref[...] = s

    return pl.pallas_call(
        body, name=name, grid=(depth,),
        in_specs=[pl.BlockSpec((d, nb), lambda l: (0, 0)), pl.BlockSpec((None, nb, ns), lambda l: (l, 0, 0)),
                  pl.BlockSpec(dall.shape, lambda l: (0, 0, 0))],
        out_specs=[pl.BlockSpec((None, d, ns), lambda l: (l, 0, 0)), pl.BlockSpec(dall.shape[1:], lambda l: (0, 0))],
        out_shape=[jax.ShapeDtypeStruct((depth, d, ns), F32), jax.ShapeDtypeStruct(dall.shape[1:], F32)],
        compiler_params=_params(("arbitrary",)),
    )(ct, dmine, dall)


def _sum_parts(parts, *, name):
    p, r, cdim = parts.shape
    tr = _pick_tile(r, 512)

    def body(p_ref, o_ref):
        s = p_ref[0]
        for k in range(1, p):
            s = s + p_ref[k]
        o_ref[...] = s

    return pl.pallas_call(
        body, name=name, grid=(r // tr,),
        in_specs=[pl.BlockSpec((p, tr, cdim), lambda i: (0, i, 0))], out_specs=pl.BlockSpec((tr, cdim), lambda i: (i, 0)),
        out_shape=jax.ShapeDtypeStruct((r, cdim), F32), compiler_params=_params(("parallel",)),
    )(parts)


def _adamw(parts, w, m, v, *, name):
    p, depth, r, cdim = parts.shape
    tr = _pick_tile(r, 256)

    def body(p_ref, w_ref, m_ref, v_ref, g_out, d_out, m_out, v_out):
        g = p_ref[0].astype(F32)
        for k in range(1, p):
            g = g + p_ref[k].astype(F32)
        m2 = ADAM_B1 * m_ref[...] + (1.0 - ADAM_B1) * g
        v2 = ADAM_B2 * v_ref[...] + (1.0 - ADAM_B2) * (g * g)
        m_hat = m2 / (1.0 - ADAM_B1 ** ADAM_STEP)
        v_hat = v2 / (1.0 - ADAM_B2 ** ADAM_STEP)
        g_out[...] = g
        d_out[...] = -ADAM_LR * (m_hat / (jnp.sqrt(v_hat) + ADAM_EPS) + ADAM_WD * w_ref[...])
        m_out[...] = m2
        v_out[...] = v2

    blk = pl.BlockSpec((None, tr, cdim), lambda l, i: (l, i, 0))
    out = jax.ShapeDtypeStruct((depth, r, cdim), F32)
    return pl.pallas_call(
        body, name=name, grid=(depth, r // tr),
        in_specs=[pl.BlockSpec((p, None, tr, cdim), lambda l, i: (0, l, i, 0)), blk, blk, blk],
        out_specs=[blk, blk, blk, blk], out_shape=[out, out, out, out],
        compiler_params=_params(("parallel", "parallel")),
    )(parts, w, m, v)


BIG = ("w_in", "w_a_out", "w_b_out", "w_pool", "w_o", "w_up", "w_down")
SMALL_REPLICATED = ("b_in", "ln_v_g", "ln_v_b", "w_spatial", "b_spatial", "pool_scale", "ln1_g", "ln1_b", "b_up",
                    "conv_ffn_b", "ln2_g", "ln2_b")
SMALL_SHARDED = ("conv_a", "conv_ffn")
WEIGHTS = ("w_ada", "b_ada", "w_in", "b_in", "conv_a", "w_a_out", "ln_v_g", "ln_v_b", "w_spatial", "b_spatial",
           "w_b_out", "w_pool", "pool_scale", "w_o", "ln1_g", "ln1_b", "w_up", "b_up", "conv_ffn", "conv_ffn_b",
           "w_down", "ln2_g", "ln2_b")
LANES = 128


def _as_rows(flat, mult=8):
    n = flat.shape[0]
    pad = (-n) % (LANES * mult)
    if pad:
        flat = jnp.concatenate([flat, jnp.zeros((pad,), flat.dtype)])
    return flat.reshape(-1, LANES)


def _shard3(a):
    return a.reshape((-1,) + a.shape[-2:])


def kernel(x, c, w_ada, b_ada, w_in, b_in, conv_a, w_a_out, ln_v_g, ln_v_b, w_spatial, b_spatial, w_b_out, w_pool, pool_scale, w_o, ln1_g, ln1_b, w_up, b_up, conv_ffn, conv_ffn_b, w_down, ln2_g, ln2_b, loss_target, m_w_ada, m_b_ada, m_w_in, m_b_in, m_conv_a, m_w_a_out, m_ln_v_g, m_ln_v_b, m_w_spatial, m_b_spatial, m_w_b_out, m_w_pool, m_pool_scale, m_w_o, m_ln1_g, m_ln1_b, m_w_up, m_b_up, m_conv_ffn, m_conv_ffn_b, m_w_down, m_ln2_g, m_ln2_b, v_w_ada, v_b_ada, v_w_in, v_b_in, v_conv_a, v_w_a_out, v_ln_v_g, v_ln_v_b, v_w_spatial, v_b_spatial, v_w_b_out, v_w_pool, v_pool_scale, v_w_o, v_ln1_g, v_ln1_b, v_w_up, v_b_up, v_conv_ffn, v_conv_ffn_b, v_w_down, v_ln2_g, v_ln2_b):
    p = dict(locals())
    depth, d = w_in.shape[0], w_in.shape[1]
    alpha = (2 * depth) ** 0.25
    me = 4 * lax.axis_index("x") + 2 * lax.axis_index("y") + lax.axis_index("c")
    cidx = lax.axis_index("c").astype(jnp.int32).reshape(1)

    n_ca, n_cf = conv_a.size, conv_ffn.size
    packed = _as_rows(jnp.concatenate([c.reshape(-1), conv_a.reshape(-1), conv_ffn.reshape(-1)]))
    got = _allgather_vmem(packed, name="gather_cond").reshape(N_DEV, -1)
    c_all = got[:, :d]
    ct = c_all.T
    conv_a_full = jnp.transpose(got[:, d:d + n_ca].reshape((N_DEV,) + conv_a.shape), (1, 2, 0, 3)).reshape(depth, 3, -1)
    conv_ffn_full = jnp.transpose(got[:, d + n_ca:d + n_ca + n_cf].reshape((N_DEV,) + conv_ffn.shape),
                                  (1, 2, 0, 3)).reshape(depth, 3, -1)

    ns_ada = w_ada.shape[2]
    ada_part = _ada_fwd(c_all, w_ada, name="ada_fwd")
    ada_all = _allgather_vmem(_as_rows(ada_part.reshape(-1)), name="gather_ada")
    ada_all = ada_all.reshape(N_DEV, depth, N_DEV, ns_ada)
    ada_mine = lax.dynamic_index_in_dim(ada_all, me, axis=2, keepdims=False)
    ada = jnp.transpose(ada_mine, (1, 0, 2)).reshape(depth, -1) + b_ada
    ada = ada.reshape(depth, 6, d)

    shards = [{n: _shard3(p[n][l].astype(BF)) for n in BIG} for l in range(depth)]
    ws = [_layer_weights(l, ada, conv_a_full, conv_ffn_full, p) for l in range(depth)]
    ws[0]["w_in8"] = _gather_weights([shards[0]["w_in"]], name="gather_w_in0")[0][0]
    plan = _GatherPlan()
    four = ("w_a_out", "w_b_out", "w_pool", "w_o")
    for l in range(depth):
        t, prev, sh = "_l%d" % l, "_l%d" % (l - 1), shards[l]
        if l == 0:
            for n in four:
                plan.add(n + t, sh[n], "in_proj" + t, "mix_fwd" + t)
            plan.add("w_up8" + t, sh["w_up"], "in_proj" + t, "mix_fwd" + t)
            plan.add("wd4" + t, sh["w_down"], "mix_fwd" + t, "up_proj" + t)
        else:
            half = sh["w_in"].shape[1] // 2
            plan.add("w_in8_top" + t, sh["w_in"], "up_proj" + prev, None, rows=(0, half))
            plan.add("w_in8" + t, sh["w_in"], "ffn_fwd" + prev, "down_proj" + prev, rows=(half, half),
                     onto="w_in8_top" + t)
            for n in four:
                plan.add(n + t, sh[n], "down_proj" + prev, "in_proj" + t)
            plan.add("w_up8" + t, sh["w_up"], "in_proj" + t, "mix_fwd" + t)
            plan.add("wd4" + t, sh["w_down"], "in_proj" + t, "mix_fwd" + t)

    rs = _ReduceScatter(depth, cidx)
    loss_blk, grad_x, bigs, reds, d_ada = _local_step(x[0], loss_target[0], ws, alpha, plan, rs)
    loss = lax.psum(loss_blk[0, 0], ("x", "y", "c"))

    dada_all = _allgather_vmem(_as_rows(d_ada.reshape(-1)), name="gather_dada")
    dada_all = dada_all.reshape(N_DEV, -1, LANES)
    dflat = dada_all.reshape(N_DEV, depth, 6 * d)
    dmine = lax.dynamic_slice_in_dim(dflat, me * ns_ada, ns_ada, axis=2)
    gw_ada, gb_rows = _ada_bwd(ct, jnp.transpose(dmine, (1, 0, 2)), dada_all, name="ada_bwd")
    gb_ada = gb_rows.reshape(-1)[:depth * 6 * d].reshape(depth, 6 * d)

    out = {}
    for n in BIG:
        parts = rs.acc[n]
        view = (lambda a: jnp.swapaxes(a, 1, 2)) if n == "w_up" else (lambda a: a)
        shard_shape = view(p[n]).shape
        w3 = view(p[n]).reshape(depth, -1, shard_shape[-1])
        parts4 = parts.reshape((4,) + w3.shape)
        res = _adamw(parts4, w3, view(p["m_" + n]).reshape(w3.shape), view(p["v_" + n]).reshape(w3.shape),
                     name="adamw_" + n)
        out[n] = [view(r.reshape(shard_shape)) for r in res]
    out["w_ada"] = _adamw(gw_ada[None], w_ada, m_w_ada, v_w_ada, name="adamw_w_ada")

    order = SMALL_REPLICATED + SMALL_SHARDED
    n_rep = sum(p[n].size for n in SMALL_REPLICATED)
    n_pay = n_rep + N_DEV * (conv_a.size + conv_ffn.size)
    gsum = _sum_parts(rs.small_gathered.reshape(N_DEV, -1, LANES), name="sum_small").reshape(-1)[:n_pay]
    ga_full = gsum[n_rep:n_rep + depth * 3 * d].reshape(depth, 3, d)
    gf_full = gsum[n_rep + depth * 3 * d:].reshape(depth, 3, -1)
    ca_w, cf_w = conv_a.shape[2], conv_ffn.shape[2]
    g_ca = lax.dynamic_slice_in_dim(ga_full, me * ca_w, ca_w, axis=2)
    g_cf = lax.dynamic_slice_in_dim(gf_full, me * cf_w, cf_w, axis=2)
    names = ("b_ada",) + order
    gflat = _as_rows(jnp.concatenate([gb_ada.reshape(-1), gsum[:n_rep], g_ca.reshape(-1), g_cf.reshape(-1)]))
    pack = lambda pre: _as_rows(jnp.concatenate([p[pre + n].reshape(-1) for n in names]))
    res = _adamw(gflat[None, None], pack("")[None], pack("m_")[None], pack("v_")[None], name="adamw_small")
    off = 0
    for n in names:
        size = p[n].size
        out[n] = [r.reshape(-1)[off:off + size].reshape(p[n].shape) for r in res]
        off += size

    return (loss, grad_x[None]) + tuple(out[n][k] for k in range(4) for n in WEIGHTS)
```

```python
import functools

import jax
import jax.numpy as jnp
from jax import lax
from jax.experimental import pallas as pl
from jax.experimental.pallas import tpu as pltpu

F32 = jnp.float32
BF = jnp.bfloat16
MESH = pl.DeviceIdType.MESH

LN_EPS = 1e-5
POOL_WINDOWS = (2, 4, 8, 16)
GMLP_BLOCK = 128
CHUNK = 64
HALO = 16
ADAM_LR, ADAM_B1, ADAM_B2, ADAM_EPS, ADAM_WD, ADAM_STEP = 0.001, 0.9, 0.999, 1e-08, 0.01, 10
N_DEV = 8
VMEM_LIMIT = 56 * 1024 * 1024

GRAD_DTYPE = BF
ACT_DTYPE = BF

NN = ((1,), (0,))
NT = ((1,), (1,))
TN = ((0,), (0,))


def _params(sem=None, vmem=VMEM_LIMIT, **kw):
    if sem is not None:
        kw["dimension_semantics"] = sem
    return pltpu.CompilerParams(vmem_limit_bytes=vmem, **kw)


class _Phase:
    def __init__(self, ins, out_shapes, aliases, n_remote, n_local, build):
        self.ins, self.out_shapes, self.aliases = list(ins), list(out_shapes), dict(aliases)
        self.n_remote, self.n_local, self.build = n_remote, n_local, build
        self.results = None


def _pcall(body, args, *, name, grid, in_specs, out_specs, out_shape, scratch_shapes=(), sem=None, aliases=None,
           phases=()):
    aliases = dict(aliases or {})
    if not phases:
        return pl.pallas_call(
            body, name=name, grid=grid, in_specs=list(in_specs), out_specs=out_specs, out_shape=out_shape,
            scratch_shapes=list(scratch_shapes), input_output_aliases=aliases, compiler_params=_params(sem),
        )(*args)
    single = not isinstance(out_shape, (list, tuple))
    o_specs = [out_specs] if single else list(out_specs)
    o_shapes = [out_shape] if single else list(out_shape)
    n_in, n_out, n_scr = len(args), len(o_shapes), len(scratch_shapes)
    ex_args, ex_out, sems = [], [], []
    for ph in phases:
        for src, dst in ph.aliases.items():
            aliases[n_in + len(ex_args) + src] = n_out + len(ex_out) + dst
        ex_args += ph.ins
        ex_out += ph.out_shapes
        sems += [pltpu.SemaphoreType.DMA((max(ph.n_remote, 1),)), pltpu.SemaphoreType.DMA((max(ph.n_remote, 1),)),
                 pltpu.SemaphoreType.DMA((max(ph.n_local, 1),))]

    def wrapped(*refs):
        pos = n_in
        ph_in = []
        for ph in phases:
            ph_in.append(refs[pos:pos + len(ph.ins)])
            pos += len(ph.ins)
        base_out = refs[pos:pos + n_out]
        pos += n_out
        ph_out = []
        for ph in phases:
            ph_out.append(refs[pos:pos + len(ph.out_shapes)])
            pos += len(ph.out_shapes)
        base_scr = refs[pos:pos + n_scr]
        ph_sems = refs[pos + n_scr:]
        first = last = None
        for ax, n in enumerate(grid):
            pid = pl.program_id(ax)
            first = (pid == 0) if first is None else first & (pid == 0)
            last = (pid == n - 1) if last is None else last & (pid == n - 1)

        def ops(k):
            return phases[k].build(ph_in[k], ph_out[k], *ph_sems[3 * k:3 * k + 3])

        @pl.when(first)
        def _():
            for k in range(len(phases)):
                for cp in ops(k)["start"]:
                    cp.start()

        body(*refs[:n_in], *base_out, *base_scr)

        @pl.when(last)
        def _():
            for k in range(len(phases)):
                o = ops(k)
                for cp in o["recv"]:
                    cp.wait_recv()
                for cp in o["send"]:
                    cp.wait_send()
                for cp in o["local"]:
                    cp.wait()

    hbm = pl.BlockSpec(memory_space=pl.ANY)
    res = pl.pallas_call(
        wrapped, name=name, grid=grid, in_specs=list(in_specs) + [hbm] * len(ex_args),
        out_specs=o_specs + [hbm] * len(ex_out), out_shape=o_shapes + ex_out,
        scratch_shapes=list(scratch_shapes) + sems, input_output_aliases=aliases,
        compiler_params=_params(("arbitrary",) * len(grid)),
    )(*args, *ex_args)
    pos = n_out
    for ph in phases:
        ph.results = list(res[pos:pos + len(ph.out_shapes)])
        pos += len(ph.out_shapes)
    return res[0] if single else list(res[:n_out])


def _gelu_parts(x):
    k = 0.7978845608028654
    x2 = x * x
    t = jnp.tanh(k * (x + 0.044715 * (x2 * x)))
    cdf = 0.5 * (1.0 + t)
    dcdf = 0.5 * (1.0 - t * t) * (k * (1.0 + 3.0 * 0.044715 * x2))
    return x * cdf, cdf + x * dcdf


def _gelu(x):
    t = jnp.tanh(0.7978845608028654 * (x + 0.044715 * (x * x * x)))
    return x * (0.5 * (1.0 + t))


def _rowsum(v):
    return jnp.sum(v, axis=0, keepdims=True)


def _ln_stats(r):
    mu = jnp.mean(r, axis=-1, keepdims=True)
    xc = r - mu
    var = jnp.mean(xc * xc, axis=-1, keepdims=True)
    rstd = lax.rsqrt(var + LN_EPS)
    return xc * rstd, rstd


def _ln_bwd(dy, xhat, rstd, gain):
    dxh = dy * gain
    m1 = jnp.mean(dxh, axis=-1, keepdims=True)
    m2 = jnp.mean(dxh * xhat, axis=-1, keepdims=True)
    return rstd * (dxh - m1 - xhat * m2)


def _matmul(a, b, *, dn, grid, a_spec, b_spec, o_spec, out_shape, acc_shape, name, phases=(), into=None):
    nk = grid[2]
    direct = out_shape.dtype == F32

    def body(a_ref, b_ref, *rest):
        o_ref, scratch = (rest[1], rest[2:]) if into is not None else (rest[0], rest[1:])
        prod = lax.dot_general(a_ref[...], b_ref[...], (dn, ((), ())), preferred_element_type=F32)
        if nk == 1:
            o_ref[...] = prod.astype(o_ref.dtype)
            return
        acc = o_ref if direct else scratch[0]
        k = pl.program_id(2)

        @pl.when(k == 0)
        def _():
            acc[...] = prod

        @pl.when(k > 0)
        def _():
            acc[...] += prod

        if not direct:
            @pl.when(k == nk - 1)
            def _():
                o_ref[...] = acc[...].astype(o_ref.dtype)

    scratch = [] if (direct or nk == 1) else [pltpu.VMEM(acc_shape, F32)]
    args, in_specs, aliases = (a, b), [a_spec, b_spec], None
    if into is not None:
        args, in_specs, aliases = (a, b, into), in_specs + [pl.BlockSpec(memory_space=pl.ANY)], {2: 0}
    return _pcall(body, args, name=name, grid=grid, in_specs=in_specs, out_specs=o_spec,
                  out_shape=out_shape, scratch_shapes=scratch, sem=("parallel", "parallel", "arbitrary"),
                  aliases=aliases, phases=phases)


def _row_tile(m, want):
    t = min(m, want)
    assert m % t == 0
    return t


def _mm_rows(a, w, *, dn, name, out_dtype=F32, tm=2048):
    m, k = a.shape
    n = w.shape[1] if dn == NN else w.shape[0]
    tm = _row_tile(m, tm)
    return _matmul(
        a, w, dn=dn, grid=(m // tm, 1, 1), name=name,
        a_spec=pl.BlockSpec((tm, k), lambda i, j, kk: (i, 0)),
        b_spec=pl.BlockSpec(w.shape, lambda i, j, kk: (0, 0)),
        o_spec=pl.BlockSpec((tm, n), lambda i, j, kk: (i, 0)),
        out_shape=jax.ShapeDtypeStruct((m, n), out_dtype), acc_shape=(tm, n))


def _mm_tn(a, b, *, name, tk=4096):
    m, ka = a.shape
    n = b.shape[1]
    tk = _row_tile(m, tk)
    return _matmul(
        a, b, dn=TN, grid=(1, 1, m // tk), name=name,
        a_spec=pl.BlockSpec((tk, ka), lambda i, j, kk: (kk, 0)),
        b_spec=pl.BlockSpec((tk, n), lambda i, j, kk: (kk, 0)),
        o_spec=pl.BlockSpec((ka, n), lambda i, j, kk: (0, 0)),
        out_shape=jax.ShapeDtypeStruct((ka, n), GRAD_DTYPE), acc_shape=(ka, n))


def _mod_matmul(x, mod, w8, bias8, *, flat_out, name, tm=2048, phases=()):
    m, k = x.shape
    nb, _, ns = w8.shape
    tm = _row_tile(m, tm)

    def body(x_ref, mod_ref, w_ref, b_ref, o_ref, h_ref, hs):
        @pl.when(pl.program_id(1) == 0)
        def _():
            h = (x_ref[...] * mod_ref[0:1, :] + mod_ref[1:2, :]).astype(BF)
            hs[...] = h
            h_ref[...] = h

        o_ref[...] = (jnp.dot(hs[...], w_ref[...], preferred_element_type=F32) + b_ref[...]).astype(o_ref.dtype)

    if flat_out:
        o_spec = pl.BlockSpec((tm, ns), lambda i, j: (i, j))
        o_shape = jax.ShapeDtypeStruct((m, nb * ns), ACT_DTYPE)
    else:
        o_spec = pl.BlockSpec((None, tm, ns), lambda i, j: (j, i, 0))
        o_shape = jax.ShapeDtypeStruct((nb, m, ns), ACT_DTYPE)
    return _pcall(
        body, (x, mod, w8, bias8), name=name, grid=(m // tm, nb),
        in_specs=[pl.BlockSpec((tm, k), lambda i, j: (i, 0)),
                  pl.BlockSpec((2, k), lambda i, j: (0, 0)),
                  pl.BlockSpec((None, k, ns), lambda i, j: (j, 0, 0)),
                  pl.BlockSpec((None, 1, ns), lambda i, j: (j, 0, 0))],
        out_specs=[o_spec, pl.BlockSpec((tm, k), lambda i, j: (i, 0))],
        out_shape=[o_shape, jax.ShapeDtypeStruct((m, k), BF)],
        scratch_shapes=[pltpu.VMEM((tm, k), BF)], sem=("parallel", "arbitrary"), phases=phases)


def _seg_spec(tm, d, s):
    return pl.BlockSpec((tm, d), lambda i, s=s: (i, s))


def _prev_halo_spec(tm, d, s):
    hb = tm // HALO
    return pl.BlockSpec((HALO, d), lambda i, s=s: (jnp.maximum(i * hb - 1, 0), s))


def _next_halo_spec(tm, d, s, m):
    hb = tm // HALO
    last = m // HALO - 1
    return pl.BlockSpec((HALO, d), lambda i, s=s: (jnp.minimum((i + 1) * hb, last), s))


def _spatial_mix(wm_ref, src, dst, bias_ref, tm, d):
    for n in range(tm // GMLP_BLOCK):
        for g in range(d // GMLP_BLOCK):
            rs = slice(n * GMLP_BLOCK, (n + 1) * GMLP_BLOCK)
            cs = slice(g * GMLP_BLOCK, (g + 1) * GMLP_BLOCK)
            v = jnp.dot(wm_ref[g], src[rs, cs], preferred_element_type=F32)
            if bias_ref is not None:
                v = v + bias_ref[:, cs]
            dst[rs, cs] = v


def _mix_fwd(z, conv_a, lnv, wm, bias_full, *, name, tm=256, phases=()):
    m, d9 = z.shape
    d = d9 // 9
    tm = _row_tile(m, tm)
    grp = d // len(POOL_WINDOWS)

    def body(zb, zc, zx, zu, zv, zp, zc_h, zx_h, zp_h, ca_ref, lnv_ref, wm_ref, bias_ref,
             ua_ref, ub_ref, d_ref, ext, vn_s, mixed_s):
        i = pl.program_id(0)
        first = i == 0
        f32 = lambda r: r[...].astype(F32)
        pa = f32(zc) * f32(zx)
        ext[0:HALO, :] = jnp.where(first, 0.0, f32(zc_h) * f32(zx_h))
        ext[HALO:HALO + tm, :] = pa
        w = ca_ref[...]
        conv = w[0:1, :] * ext[pl.ds(HALO - 2, tm), :] + w[1:2, :] * ext[pl.ds(HALO - 1, tm), :] + w[2:3, :] * pa
        ua_ref[...] = (f32(zb) * conv).astype(BF)
        p = f32(zp)
        ext[0:HALO, :] = jnp.where(first, 0.0, f32(zp_h))
        ext[HALO:HALO + tm, :] = p
        t = (i * tm + lax.broadcasted_iota(jnp.int32, (tm, 1), 0) + 1).astype(F32)
        for k, win in enumerate(POOL_WINDOWS):
            cs = slice(k * grp, (k + 1) * grp)
            s = p[:, cs]
            for j in range(1, win):
                s = s + ext[pl.ds(HALO - j, tm), cs]
            d_ref[:, cs] = (s / jnp.minimum(t, float(win)) - p[:, cs]).astype(BF)
        gv = _gelu(f32(zv))
        vhat, _ = _ln_stats(gv)
        vn_s[...] = (vhat * lnv_ref[0:1, :] + lnv_ref[1:2, :]).astype(BF)
        _spatial_mix(wm_ref, vn_s, mixed_s, bias_ref, tm, d)
        ub_ref[...] = (_gelu(f32(zu)) * mixed_s[...]).astype(BF)

    full = lambda a: pl.BlockSpec(a.shape, lambda i: (0,) * a.ndim)
    out = jax.ShapeDtypeStruct((m, d), BF)
    o_spec = pl.BlockSpec((tm, d), lambda i: (i, 0))
    return _pcall(
        body, (z, z, z, z, z, z, z, z, z, conv_a, lnv, wm, bias_full), name=name, grid=(m // tm,),
        in_specs=[_seg_spec(tm, d, s) for s in range(6)] + [_prev_halo_spec(tm, d, s) for s in (1, 2, 5)]
        + [full(conv_a), full(lnv), full(wm), full(bias_full)],
        out_specs=[o_spec, o_spec, o_spec], out_shape=[out, out, out],
        scratch_shapes=[pltpu.VMEM((HALO + tm, d), F32), pltpu.VMEM((tm, d), BF), pltpu.VMEM((tm, d), F32)],
        sem=("arbitrary",), phases=phases)


def _pool_proj(dd, w_pool, *, dn, name, out_dtype=F32, tm=512):
    m, d = dd.shape
    ng, grp, _ = w_pool.shape
    tm = _row_tile(m, tm)
    return _matmul(
        dd, w_pool, dn=dn, grid=(m // tm, ng, 1), name=name,
        a_spec=pl.BlockSpec((tm, grp), lambda i, j, kk: (i, j)),
        b_spec=pl.BlockSpec((None, grp, grp), lambda i, j, kk: (j, 0, 0)),
        o_spec=pl.BlockSpec((tm, grp), lambda i, j, kk: (i, j)),
        out_shape=jax.ShapeDtypeStruct((m, d), out_dtype), acc_shape=(tm, grp))


def _merge(z, ya, yb, ycp, scale, *, name, tm=512):
    m, d = ya.shape
    tm = _row_tile(m, tm)

    def body(ga, gb, gc, ya_ref, yb_ref, yc_ref, sc_ref, o_ref):
        f32 = lambda r: r[...].astype(F32)
        o_ref[...] = (jax.nn.sigmoid(f32(ga)) * f32(ya_ref) + jax.nn.sigmoid(f32(gb)) * f32(yb_ref)
                      + jax.nn.sigmoid(f32(gc)) * (f32(yc_ref) * sc_ref[...])).astype(BF)

    row = pl.BlockSpec((tm, d), lambda i: (i, 0))
    return pl.pallas_call(
        body, name=name, grid=(m // tm,),
        in_specs=[_seg_spec(tm, d, 6), _seg_spec(tm, d, 7), _seg_spec(tm, d, 8), row, row, row,
                  pl.BlockSpec((1, d), lambda i: (0, 0))],
        out_specs=row, out_shape=jax.ShapeDtypeStruct((m, d), BF),
        compiler_params=_params(("parallel",)),
    )(z, z, z, ya, yb, ycp, scale)


def _resid_ln(xp, ys, vec, alpha, *, name, tm=512):
    m, d = xp.shape
    tm = _row_tile(m, tm)

    def body(xp_ref, ys_ref, v_ref, o_ref):
        xhat, _ = _ln_stats(alpha * xp_ref[...] + v_ref[0:1, :] * ys_ref[...])
        o_ref[...] = xhat * v_ref[1:2, :] + v_ref[2:3, :]

    row = pl.BlockSpec((tm, d), lambda i: (i, 0))
    return pl.pallas_call(
        body, name=name, grid=(m // tm,),
        in_specs=[row, row, pl.BlockSpec(vec.shape, lambda i: (0, 0))],
        out_specs=row, out_shape=jax.ShapeDtypeStruct((m, d), F32),
        compiler_params=_params(("parallel",)),
    )(xp, ys, vec)


def _ffn_fwd(up4, cw, cb, *, name, tm=512, phases=()):
    _, nj, m, fs = up4.shape
    tm = _row_tile(m, tm)
    hb = tm // HALO

    def body(up_ref, ah_ref, cw_ref, cb_ref, f_ref, ext):
        first = pl.program_id(1) == 0
        ext[0:HALO, :] = jnp.where(first, 0.0, ah_ref[...].astype(F32))
        ext[HALO:HALO + tm, :] = up_ref[0].astype(F32)
        w = cw_ref[...]
        w0, w1, w2, bias = w[0:1, :], w[1:2, :], w[2:3, :], cb_ref[...]
        rc = 16

        def step(c, carry):
            r0 = pl.multiple_of(c * rc, rc)
            win = ext[pl.ds(r0 + HALO - 8, rc + 8), :]
            a0, a1, a2 = win[8:8 + rc], pltpu.roll(win, 1, 0)[8:8 + rc], pltpu.roll(win, 2, 0)[8:8 + rc]
            ca = w0 * a2 + w1 * a1 + w2 * a0 + bias
            f_ref[pl.ds(r0, rc), :] = (_gelu(ca) * up_ref[1, pl.ds(r0, rc), :].astype(F32)).astype(BF)
            return carry

        lax.fori_loop(0, tm // rc, step, 0)

    return _pcall(
        body, (up4, up4, cw, cb), name=name, grid=(nj, m // tm),
        in_specs=[pl.BlockSpec((2, None, tm, fs), lambda j, i: (0, j, i, 0)),
                  pl.BlockSpec((None, None, HALO, fs), lambda j, i: (0, j, jnp.maximum(i * hb - 1, 0), 0)),
                  pl.BlockSpec((None, 3, fs), lambda j, i: (j, 0, 0)),
                  pl.BlockSpec((None, 1, fs), lambda j, i: (j, 0, 0))],
        out_specs=pl.BlockSpec((None, tm, fs), lambda j, i: (j, i, 0)),
        out_shape=jax.ShapeDtypeStruct((nj, m, fs), BF),
        scratch_shapes=[pltpu.VMEM((HALO + tm, fs), F32)], sem=("parallel", "arbitrary"), phases=phases)


def _down_proj(f4, wd4, *, name, tm=2048, phases=()):
    nj, m, fs = f4.shape
    d = wd4.shape[2]
    tm = _row_tile(m, tm)
    return _matmul(
        f4, wd4, dn=NN, grid=(m // tm, 1, nj), name=name,
        a_spec=pl.BlockSpec((None, tm, fs), lambda i, j, kk: (kk, i, 0)),
        b_spec=pl.BlockSpec((None, fs, d), lambda i, j, kk: (kk, 0, 0)),
        o_spec=pl.BlockSpec((tm, d), lambda i, j, kk: (i, 0)),
        out_shape=jax.ShapeDtypeStruct((m, d), F32), acc_shape=(tm, d), phases=phases)


def _loss_grad(y, tgt, *, name, tm=512):
    m, d = y.shape
    tm = _row_tile(m, tm)
    ni = m // tm

    def body(y_ref, t_ref, dy_ref, l_ref, acc):
        i = pl.program_id(0)
        e = y_ref[...] - t_ref[...]
        dy_ref[...] = e * (1.0 / d)
        part = jnp.sum((e * e).reshape(tm // 8, 8, d), axis=0)

        @pl.when(i == 0)
        def _():
            acc[...] = part

        @pl.when(i > 0)
        def _():
            acc[...] += part

        @pl.when(i == ni - 1)
        def _():
            l_ref[...] = jnp.full((8, 128), 0.5 / d, F32) * jnp.sum(acc[...])

    row = pl.BlockSpec((tm, d), lambda i: (i, 0))
    return pl.pallas_call(
        body, name=name, grid=(ni,), in_specs=[row, row],
        out_specs=[row, pl.BlockSpec((8, 128), lambda i: (0, 0))],
        out_shape=[jax.ShapeDtypeStruct((m, d), F32), jax.ShapeDtypeStruct((8, 128), F32)],
        scratch_shapes=[pltpu.VMEM((8, d), F32)],
        compiler_params=_params(("arbitrary",)),
    )(y, tgt)


def _resid_ln_bwd(dpart, dh, xmod, mvec, xp, ys, vec, alpha, *, name, tm=512, phases=()):
    m, d = dpart.shape
    tm = _row_tile(m, tm)
    has_dh = dh is not None
    has_ln = xp is not None

    def body(*refs):
        refs = list(refs)
        dpart_ref = refs.pop(0)
        if has_dh:
            dh_ref, xm_ref, mv_ref = refs.pop(0), refs.pop(0), refs.pop(0)
        if has_ln:
            xp_ref, ys_ref, v_ref = refs.pop(0), refs.pop(0), refs.pop(0)
            dys_ref, dxp_ref, red_ref = refs
        else:
            dx_ref, red_ref = refs
        i = pl.program_id(0)
        dtot = dpart_ref[...]
        rows = [jnp.zeros((1, d), F32)] * 5
        if has_dh:
            dhv = dh_ref[...]
            dtot = dtot + dhv * mv_ref[...]
            rows[0] = _rowsum(dhv * xm_ref[...])
            rows[1] = _rowsum(dhv)
        if has_ln:
            ys = ys_ref[...]
            gt = v_ref[0:1, :]
            xhat, rstd = _ln_stats(alpha * xp_ref[...] + gt * ys)
            rows[2] = _rowsum(dtot * xhat)
            rows[3] = _rowsum(dtot)
            dr = _ln_bwd(dtot, xhat, rstd, v_ref[1:2, :])
            rows[4] = _rowsum(dr * ys)
            dys_ref[...] = (dr * gt).astype(BF)
            dxp_ref[...] = alpha * dr
        else:
            dx_ref[...] = dtot
        red = jnp.concatenate(rows + [jnp.zeros((3, d), F32)], axis=0)

        @pl.when(i == 0)
        def _():
            red_ref[...] = red

        @pl.when(i > 0)
        def _():
            red_ref[...] += red

    row = pl.BlockSpec((tm, d), lambda i: (i, 0))
    vrow = lambda a: pl.BlockSpec(a.shape, lambda i: (0, 0))
    args, specs = [dpart], [row]
    if has_dh:
        args += [dh, xmod, mvec]
        specs += [row, row, vrow(mvec)]
    if has_ln:
        args += [xp, ys, vec]
        specs += [row, row, vrow(vec)]
        out_specs = [row, row, pl.BlockSpec((8, d), lambda i: (0, 0))]
        out_shape = [jax.ShapeDtypeStruct((m, d), BF), jax.ShapeDtypeStruct((m, d), F32),
                     jax.ShapeDtypeStruct((8, d), F32)]
    else:
        out_specs = [row, pl.BlockSpec((8, d), lambda i: (0, 0))]
        out_shape = [jax.ShapeDtypeStruct((m, d), F32), jax.ShapeDtypeStruct((8, d), F32)]
    return _pcall(body, args, name=name, grid=(m // tm,), in_specs=specs, out_specs=out_specs, out_shape=out_shape,
                  sem=("arbitrary",), phases=phases)


def _down_bwd(dy, wd4, *, name, tm=2048, phases=()):
    m, d = dy.shape
    nj, fs, _ = wd4.shape
    tm = _row_tile(m, tm)
    return _matmul(
        dy, wd4, dn=NT, grid=(m // tm, nj, 1), name=name,
        a_spec=pl.BlockSpec((tm, d), lambda i, j, kk: (i, 0)),
        b_spec=pl.BlockSpec((None, fs, d), lambda i, j, kk: (j, 0, 0)),
        o_spec=pl.BlockSpec((None, tm, fs), lambda i, j, kk: (j, i, 0)),
        out_shape=jax.ShapeDtypeStruct((nj, m, fs), ACT_DTYPE), acc_shape=(tm, fs), phases=phases)


def _tn_shards_lhs(f4, dy, *, name, tk=4096, phases=()):
    nj, m, fs = f4.shape
    d = dy.shape[1]
    tk = _row_tile(m, tk)
    return _matmul(
        f4, dy, dn=TN, grid=(nj, 1, m // tk), name=name,
        a_spec=pl.BlockSpec((None, tk, fs), lambda i, j, kk: (i, kk, 0)),
        b_spec=pl.BlockSpec((tk, d), lambda i, j, kk: (kk, 0)),
        o_spec=pl.BlockSpec((None, fs, d), lambda i, j, kk: (i, 0, 0)),
        out_shape=jax.ShapeDtypeStruct((nj, fs, d), GRAD_DTYPE), acc_shape=(fs, d), phases=phases)


def _tn_shards_rhs(h, d8, *, name, tk=2048, phases=()):
    m, k = h.shape
    nb, _, ns = d8.shape
    tk = _row_tile(m, tk)
    return _matmul(
        h, d8, dn=TN, grid=(nb, 1, m // tk), name=name,
        a_spec=pl.BlockSpec((tk, k), lambda i, j, kk: (kk, 0)),
        b_spec=pl.BlockSpec((None, tk, ns), lambda i, j, kk: (i, kk, 0)),
        o_spec=pl.BlockSpec((None, k, ns), lambda i, j, kk: (i, 0, 0)),
        out_shape=jax.ShapeDtypeStruct((nb, k, ns), GRAD_DTYPE), acc_shape=(k, ns), phases=phases)


def _tn_cols_rhs(h, dz, nb, *, name, tk=4096, phases=()):
    m, k = h.shape
    ns = dz.shape[1] // nb
    tk = _row_tile(m, tk)
    return _matmul(
        h, dz, dn=TN, grid=(nb, 1, m // tk), name=name,
        a_spec=pl.BlockSpec((tk, k), lambda i, j, kk: (kk, 0)),
        b_spec=pl.BlockSpec((tk, ns), lambda i, j, kk: (kk, i)),
        o_spec=pl.BlockSpec((None, k, ns), lambda i, j, kk: (i, 0, 0)),
        out_shape=jax.ShapeDtypeStruct((nb, k, ns), GRAD_DTYPE), acc_shape=(k, ns), phases=phases)


def _nt_shards(d8, w8, *, name, tm=2048, phases=()):
    nb, m, ns = d8.shape
    k = w8.shape[1]
    tm = _row_tile(m, tm)
    return _matmul(
        d8, w8, dn=NT, grid=(m // tm, 1, nb), name=name,
        a_spec=pl.BlockSpec((None, tm, ns), lambda i, j, kk: (kk, i, 0)),
        b_spec=pl.BlockSpec((None, k, ns), lambda i, j, kk: (kk, 0, 0)),
        o_spec=pl.BlockSpec((tm, k), lambda i, j, kk: (i, 0)),
        out_shape=jax.ShapeDtypeStruct((m, k), F32), acc_shape=(tm, k), phases=phases)


def _nt_cols(dz, w8, *, name, tm=2048, phases=(), tiles=None, into=None):
    m = dz.shape[0]
    nb, k, ns = w8.shape
    tm = _row_tile(m, tm)
    first, count = tiles if tiles is not None else (0, m // tm)
    return _matmul(
        dz, w8, dn=NT, grid=(count, 1, nb), name=name,
        a_spec=pl.BlockSpec((tm, ns), lambda i, j, kk: (i + first, kk)),
        b_spec=pl.BlockSpec((None, k, ns), lambda i, j, kk: (kk, 0, 0)),
        o_spec=pl.BlockSpec((tm, k), lambda i, j, kk: (i + first, 0)),
        out_shape=jax.ShapeDtypeStruct((m, k), F32), acc_shape=(tm, k), phases=phases, into=into)


def _tn_pool(dd, dyc, ng, *, name, tk=2048):
    m, d = dd.shape
    grp = d // ng
    tk = _row_tile(m, tk)
    return _matmul(
        dd, dyc, dn=TN, grid=(ng, 1, m // tk), name=name,
        a_spec=pl.BlockSpec((tk, grp), lambda i, j, kk: (kk, i)),
        b_spec=pl.BlockSpec((tk, grp), lambda i, j, kk: (kk, i)),
        o_spec=pl.BlockSpec((None, grp, grp), lambda i, j, kk: (i, 0, 0)),
        out_shape=jax.ShapeDtypeStruct((ng, grp, grp), GRAD_DTYPE), acc_shape=(grp, grp))


def _ffn_bwd(up4, df4, cw, cb, *, name, tm=512, phases=()):
    _, nj, m, fs = up4.shape
    tm = _row_tile(m, tm)
    hb = tm // HALO
    ni = m // tm
    last_hb = m // HALO - 1
    ext_rows = tm + 8

    rc = 16
    assert tm % rc == 0

    def body(up_ref, ap_ref, un_ref, df_ref, dfn_ref, cw_ref, cb_ref, dup_ref, red_ref, ext, dca_s, racc):
        i = pl.program_id(1)
        ext[0:HALO, :] = jnp.where(i == 0, 0.0, ap_ref[...].astype(F32))
        ext[HALO:HALO + tm, :] = up_ref[0].astype(F32)
        ext[HALO + tm:2 * HALO + tm, :] = un_ref[0].astype(F32)
        racc[...] = jnp.zeros_like(racc)
        w = cw_ref[...]
        w0, w1, w2, bias = w[0:1, :], w[1:2, :], w[2:3, :], cb_ref[...]

        def conv_taps(win, n):
            return (win[8:8 + n], pltpu.roll(win, 1, 0)[8:8 + n], pltpu.roll(win, 2, 0)[8:8 + n])

        def fold(v):
            return v[0:8] + v[8:16]

        def add_red(k, v8):
            racc[8 * k:8 * k + 8, :] += v8

        def first_pass(c, carry):
            r0 = pl.multiple_of(c * rc, rc)
            a0, a1, a2 = conv_taps(ext[pl.ds(r0 + HALO - 8, rc + 8), :], rc)
            act, dact = _gelu_parts(w0 * a2 + w1 * a1 + w2 * a0 + bias)
            dfc = df_ref[pl.ds(r0, rc), :].astype(F32)
            dca = dfc * up_ref[1, pl.ds(r0, rc), :].astype(F32) * dact
            dup_g = dfc * act
            dca_s[pl.ds(r0, rc), :] = dca
            dup_ref[1, pl.ds(r0, rc), :] = dup_g.astype(BF)
            for k, v in enumerate((dca * a2, dca * a1, dca * a0, dca, dup_g)):
                add_red(k if k < 4 else 5, fold(v))
            return carry

        lax.fori_loop(0, tm // rc, first_pass, 0)
        a0, a1, a2 = conv_taps(ext[HALO + tm - 8:HALO + tm + 8, :], 8)
        _, dact = _gelu_parts(w0 * a2 + w1 * a1 + w2 * a0 + bias)
        after = dfn_ref[...].astype(F32)[0:8, :] * un_ref[1].astype(F32)[0:8, :] * dact
        dca_s[tm:tm + 8, :] = jnp.where(i < ni - 1, after, 0.0)
        dca_s[tm + 8:tm + 16, :] = jnp.zeros((8, fs), F32)

        def second_pass(c, carry):
            r0 = pl.multiple_of(c * rc, rc)
            win = dca_s[pl.ds(r0, rc + 8), :]
            up1, up2 = pltpu.roll(win, rc + 7, 0)[0:rc], pltpu.roll(win, rc + 6, 0)[0:rc]
            dup_a = w2 * win[0:rc] + w1 * up1 + w0 * up2
            dup_ref[0, pl.ds(r0, rc), :] = dup_a.astype(BF)
            add_red(4, fold(dup_a))
            return carry

        lax.fori_loop(0, tm // rc, second_pass, 0)
        red = jnp.concatenate([_rowsum(racc[8 * k:8 * k + 8, :]) for k in range(6)] + [jnp.zeros((2, fs), F32)],
                              axis=0)

        @pl.when(i == 0)
        def _():
            red_ref[...] = red

        @pl.when(i > 0)
        def _():
            red_ref[...] += red

    nxt = lambda j, i: jnp.minimum((i + 1) * hb, last_hb)
    return _pcall(
        body, (up4, up4, up4, df4, df4, cw, cb), name=name, grid=(nj, ni), sem=("parallel", "arbitrary"), phases=phases,
        in_specs=[pl.BlockSpec((2, None, tm, fs), lambda j, i: (0, j, i, 0)),
                  pl.BlockSpec((None, None, HALO, fs), lambda j, i: (0, j, jnp.maximum(i * hb - 1, 0), 0)),
                  pl.BlockSpec((2, None, HALO, fs), lambda j, i: (0, j, nxt(j, i), 0)),
                  pl.BlockSpec((None, tm, fs), lambda j, i: (j, i, 0)),
                  pl.BlockSpec((None, HALO, fs), lambda j, i: (j, nxt(j, i), 0)),
                  pl.BlockSpec((None, 3, fs), lambda j, i: (j, 0, 0)),
                  pl.BlockSpec((None, 1, fs), lambda j, i: (j, 0, 0))],
        out_specs=[pl.BlockSpec((2, None, tm, fs), lambda j, i: (0, j, i, 0)),
                   pl.BlockSpec((None, 8, fs), lambda j, i: (j, 0, 0))],
        out_shape=[jax.ShapeDtypeStruct((2, nj, m, fs), BF), jax.ShapeDtypeStruct((nj, 8, fs), F32)],
        scratch_shapes=[pltpu.VMEM((2 * HALO + tm, fs), F32), pltpu.VMEM((tm + 16, fs), F32),
                        pltpu.VMEM((48, fs), F32)])


def _gate_bwd(dm, z, ya, yb, ycp, scale, *, name, tm=512):
    m, d = dm.shape
    tm = _row_tile(m, tm)

    def body(dm_ref, ga, gb, gc, ya_ref, yb_ref, yc_ref, sc_ref, dya_ref, dyb_ref, dyc_ref, dz_ref, red_ref):
        i = pl.program_id(0)
        f32 = lambda r: r[...].astype(F32)
        dmv = f32(dm_ref)
        sa, sb, sc = jax.nn.sigmoid(f32(ga)), jax.nn.sigmoid(f32(gb)), jax.nn.sigmoid(f32(gc))
        scale_v = sc_ref[...]
        ycp_v = f32(yc_ref)
        dya_ref[...] = (dmv * sa).astype(BF)
        dyb_ref[...] = (dmv * sb).astype(BF)
        dyc = dmv * sc
        dyc_ref[...] = (dyc * scale_v).astype(BF)
        dga = dmv * f32(ya_ref) * (sa * (1.0 - sa))
        dgb = dmv * f32(yb_ref) * (sb * (1.0 - sb))
        dgc = dmv * (ycp_v * scale_v) * (sc * (1.0 - sc))
        dz_ref[:, 0:d] = dga.astype(BF)
        dz_ref[:, d:2 * d] = dgb.astype(BF)
        dz_ref[:, 2 * d:3 * d] = dgc.astype(BF)
        red = jnp.concatenate([_rowsum(dyc * ycp_v), _rowsum(dga), _rowsum(dgb), _rowsum(dgc),
                               jnp.zeros((4, d), F32)], axis=0)

        @pl.when(i == 0)
        def _():
            red_ref[...] = red

        @pl.when(i > 0)
        def _():
            red_ref[...] += red

    row = pl.BlockSpec((tm, d), lambda i: (i, 0))
    obf = jax.ShapeDtypeStruct((m, d), BF)
    return pl.pallas_call(
        body, name=name, grid=(m // tm,),
        in_specs=[row, _seg_spec(tm, d, 6), _seg_spec(tm, d, 7), _seg_spec(tm, d, 8), row, row, row,
                  pl.BlockSpec((1, d), lambda i: (0, 0))],
        out_specs=[row, row, row, pl.BlockSpec((tm, 3 * d), lambda i: (i, 2)), pl.BlockSpec((8, d), lambda i: (0, 0))],
        out_shape=[obf, obf, obf, jax.ShapeDtypeStruct((m, 9 * d), BF), jax.ShapeDtypeStruct((8, d), F32)],
        compiler_params=_params(("arbitrary",)),
    )(dm, z, z, z, ya, yb, ycp, scale)


def _mix_bwd(dz, dua, dub, ddd, z, conv_a, lnv, wm, wmt, bias_full, mask, *, name, tm=128, phases=()):
    m, d = dua.shape
    tm = _row_tile(m, tm)
    ni = m // tm
    grp = d // len(POOL_WINDOWS)
    ng = d // GMLP_BLOCK
    ext_rows = tm + 8

    def body(dz_in, dua_ref, dub_ref, dd_ref, zb, zc, zx, zu, zv, zp, zc_h, zx_h, dua_n, zb_n, dd_n,
             ca_ref, lnv_ref, wm_ref, wmt_ref, bias_ref, mask_ref,
             dz_ref, red_ref, dws_ref, dbs_ref, ext, sh_s, vn_s, mixed_s, dmx_s, dvn_s, dbs_acc):
        del dz_in
        i = pl.program_id(0)
        rows = []
        f32 = lambda r: r[...].astype(F32)
        zbv, zcv, zxv = f32(zb), f32(zc), f32(zx)
        pa = zcv * zxv
        ext[0:HALO, :] = jnp.where(i == 0, 0.0, f32(zc_h) * f32(zx_h))
        ext[HALO:HALO + tm, :] = pa
        w = ca_ref[...]
        w0, w1, w2 = w[0:1, :], w[1:2, :], w[2:3, :]
        p1 = ext[pl.ds(HALO - 1, tm), :]
        p2 = ext[pl.ds(HALO - 2, tm), :]
        conv = w0 * p2 + w1 * p1 + w2 * pa
        duav = f32(dua_ref)
        dzb = duav * conv
        dca = duav * zbv
        dca_n = jnp.where(i < ni - 1, f32(dua_n)[0:8, :] * f32(zb_n)[0:8, :], 0.0)
        sh_s[0:tm, :] = dca
        sh_s[tm:tm + 8, :] = dca_n
        dpa = w2 * dca + w1 * sh_s[pl.ds(1, tm), :] + w0 * sh_s[pl.ds(2, tm), :]
        dzc = dpa * zxv
        dzx = dpa * zcv
        dz_ref[:, 0:d] = dzb.astype(BF)
        dz_ref[:, d:2 * d] = dzc.astype(BF)
        dz_ref[:, 2 * d:3 * d] = dzx.astype(BF)
        rows += [_rowsum(dzb), _rowsum(dzc), _rowsum(dzx)]
        dconv = [_rowsum(dca * p2), _rowsum(dca * p1), _rowsum(dca * pa)]
        zuv, zvv = f32(zu), f32(zv)
        gu, dgu_dz = _gelu_parts(zuv)
        gv, dgv_dz = _gelu_parts(zvv)
        vhat, rstd = _ln_stats(gv)
        gain = lnv_ref[0:1, :]
        vn_s[...] = (vhat * gain + lnv_ref[1:2, :]).astype(BF)
        _spatial_mix(wm_ref, vn_s, mixed_s, bias_ref, tm, d)
        dubv = f32(dub_ref)
        dzu = dubv * mixed_s[...] * dgu_dz
        dmixed = dubv * gu
        dmx_s[...] = dmixed.astype(BF)
        _spatial_mix(wmt_ref, dmx_s, dvn_s, None, tm, d)
        dvn = dvn_s[...]
        dzv = _ln_bwd(dvn, vhat, rstd, gain) * dgv_dz
        dz_ref[:, 3 * d:4 * d] = dzu.astype(BF)
        dz_ref[:, 4 * d:5 * d] = dzv.astype(BF)
        rows += [_rowsum(dzu), _rowsum(dzv)]
        dlnv = [_rowsum(dvn * vhat), _rowsum(dvn)]
        dbs_part = dmixed[0:GMLP_BLOCK, :]
        for n in range(1, tm // GMLP_BLOCK):
            dbs_part = dbs_part + dmixed[n * GMLP_BLOCK:(n + 1) * GMLP_BLOCK, :]
        ddv = f32(dd_ref)
        t = (i * tm + lax.broadcasted_iota(jnp.int32, (ext_rows + 8, 1), 0) + 1).astype(F32)
        dde = jnp.concatenate([ddv, jnp.where(i < ni - 1, f32(dd_n), 0.0)], axis=0)
        for k, win in enumerate(POOL_WINDOWS):
            cs = slice(k * grp, (k + 1) * grp)
            ext[0:tm + HALO, cs] = dde[:, cs] / jnp.minimum(t, float(win))
        dzp_parts = []
        for k, win in enumerate(POOL_WINDOWS):
            cs = slice(k * grp, (k + 1) * grp)
            s = ext[0:tm, cs]
            for j in range(1, win):
                s = s + ext[pl.ds(j, tm), cs]
            dzp_parts.append(s - ddv[:, cs])
        dzp = jnp.concatenate(dzp_parts, axis=1)
        dz_ref[:, 5 * d:6 * d] = dzp.astype(BF)
        rows += [_rowsum(dzp)]
        red = jnp.concatenate(rows + dconv + dlnv + [jnp.zeros((5, d), F32)], axis=0)

        @pl.when(i == 0)
        def _():
            red_ref[...] = red
            dbs_acc[...] = dbs_part
            dws_ref[...] = jnp.zeros_like(dws_ref)

        @pl.when(i > 0)
        def _():
            red_ref[...] += red
            dbs_acc[...] += dbs_part

        for n in range(tm // GMLP_BLOCK):
            for g in range(ng):
                rs = slice(n * GMLP_BLOCK, (n + 1) * GMLP_BLOCK)
                cs = slice(g * GMLP_BLOCK, (g + 1) * GMLP_BLOCK)
                dws_ref[g] += mask_ref[...] * lax.dot_general(
                    dmx_s[rs, cs], vn_s[rs, cs], (NT, ((), ())), preferred_element_type=F32)

        @pl.when(i == ni - 1)
        def _():
            lane = lax.broadcasted_iota(jnp.int32, (GMLP_BLOCK, GMLP_BLOCK), 1)
            out = jnp.zeros((GMLP_BLOCK, GMLP_BLOCK), F32)
            for g in range(ng):
                sg = jnp.sum(dbs_acc[:, g * GMLP_BLOCK:(g + 1) * GMLP_BLOCK], axis=1, keepdims=True)
                out = out + jnp.where(lane == g, sg, 0.0)
            dbs_ref[...] = out

    row = pl.BlockSpec((tm, d), lambda i: (i, 0))
    full = lambda a: pl.BlockSpec(a.shape, lambda i: (0,) * a.ndim)
    hb = tm // HALO
    last_hb = m // HALO - 1
    nrow = pl.BlockSpec((HALO, d), lambda i: (jnp.minimum((i + 1) * hb, last_hb), 0))
    return _pcall(
        body, (dz, dua, dub, ddd, z, z, z, z, z, z, z, z, dua, z, ddd, conv_a, lnv, wm, wmt, bias_full, mask),
        name=name, grid=(ni,), sem=("arbitrary",), aliases={0: 0}, phases=phases,
        in_specs=[pl.BlockSpec(memory_space=pl.ANY), row, row, row]
        + [_seg_spec(tm, d, s) for s in range(6)]
        + [_prev_halo_spec(tm, d, 1), _prev_halo_spec(tm, d, 2), nrow, _next_halo_spec(tm, d, 0, m), nrow]
        + [full(conv_a), full(lnv), full(wm), full(wmt), full(bias_full), full(mask)],
        out_specs=[pl.BlockSpec((tm, 6 * d), lambda i: (i, 0)), pl.BlockSpec((16, d), lambda i: (0, 0)),
                   full(wm), pl.BlockSpec((GMLP_BLOCK, GMLP_BLOCK), lambda i: (0, 0))],
        out_shape=[jax.ShapeDtypeStruct(dz.shape, BF), jax.ShapeDtypeStruct((16, d), F32),
                   jax.ShapeDtypeStruct(wm.shape, F32), jax.ShapeDtypeStruct((GMLP_BLOCK, GMLP_BLOCK), F32)],
        scratch_shapes=[pltpu.VMEM((2 * HALO + tm, d), F32), pltpu.VMEM((tm + 8, d), F32),
                        pltpu.VMEM((tm, d), BF), pltpu.VMEM((tm, d), F32), pltpu.VMEM((tm, d), BF),
                        pltpu.VMEM((tm, d), F32), pltpu.VMEM((GMLP_BLOCK, d), F32)])


REST = ("w_a_out", "w_b_out", "w_pool", "w_o", "w_up", "w_down")


def _remote(src, dst, ssem, rsem, k, to):
    return pltpu.make_async_remote_copy(src_ref=src, dst_ref=dst, send_sem=ssem.at[k], recv_sem=rsem.at[k],
                                        device_id=to, device_id_type=MESH)


def _gather_phase1(shards, rows=None, onto=None):
    n = len(shards)
    rows = rows or [None] * n
    onto = onto or [None] * n
    extra = [a for a in range(n) if onto[a] is not None]

    def build(ins, outs, ssem, rsem, lsem):
        x, y, c, chips = _place()
        me = 4 * x + 2 * y + c

        def src(a):
            return ins[a] if rows[a] is None else ins[a].at[:, pl.ds(*rows[a])]

        def dst(a, dev):
            return outs[a].at[:, dev] if rows[a] is None else outs[a].at[:, dev, pl.ds(*rows[a])]

        local = [pltpu.make_async_copy(src(a), dst(a, me), lsem.at[a]) for a in range(n)]
        sends, recvs = [], []
        for j, (cx, cy) in enumerate(chips):
            for a in range(n):
                sends.append(_remote(src(a), dst(a, me), ssem, rsem, 4 * a + 1 + j, (cx, cy, c)))
                recvs.append(_remote(src(a), dst(a, 4 * cx + 2 * cy + c), ssem, rsem, 4 * a + 1 + j, (cx, cy, c)))
        for a in range(n):
            sends.append(_remote(src(a), dst(a, me), ssem, rsem, 4 * a, (x, y, 1 - c)))
            recvs.append(_remote(src(a), dst(a, 4 * x + 2 * y + 1 - c), ssem, rsem, 4 * a, (x, y, 1 - c)))
        return dict(start=local + sends, recv=recvs, send=sends, local=local)

    outs = [jax.ShapeDtypeStruct((s.shape[0], N_DEV) + s.shape[1:], s.dtype) for s in shards]
    return _Phase(list(shards) + [onto[a] for a in extra], outs, {n + k: a for k, a in enumerate(extra)},
                  4 * n, n, build)


def _gather_phase2(fulls):
    n = len(fulls)

    def build(ins, outs, ssem, rsem, lsem):
        x, y, c, chips = _place()
        sends, recvs = [], []
        for j, (cx, cy) in enumerate(chips):
            for a in range(n):
                mine, theirs = 4 * cx + 2 * cy + c, 4 * cx + 2 * cy + 1 - c
                sends.append(_remote(ins[a].at[:, mine], outs[a].at[:, mine], ssem, rsem, 3 * a + j, (x, y, 1 - c)))
                recvs.append(_remote(ins[a].at[:, theirs], outs[a].at[:, theirs], ssem, rsem, 3 * a + j, (x, y, 1 - c)))
        return dict(start=sends, recv=recvs, send=sends, local=[])

    outs = [jax.ShapeDtypeStruct(f.shape, f.dtype) for f in fulls]
    return _Phase(fulls, outs, {a: a for a in range(n)}, 3 * n, 0, build)


def _pair_phase(grads):
    n = len(grads)

    def build(ins, outs, ssem, rsem, lsem):
        x, y, c, _ = _place()
        cps = [_remote(ins[a].at[:, 2 * q + (1 - c)], outs[a].at[q], ssem, rsem, 4 * a + q, (x, y, 1 - c))
               for a in range(n) for q in range(4)]
        return dict(start=cps, recv=cps, send=cps, local=[])

    outs = [jax.ShapeDtypeStruct((4, g.shape[0]) + g.shape[2:], g.dtype) for g in grads]
    return _Phase(grads, outs, {}, 4 * n, 0, build)


def _chip_phase(bufs, accs, l, depth):
    n = len(bufs)
    has = accs is not None

    def build(ins, outs, ssem, rsem, lsem):
        x, y, c, chips = _place()
        myq = 2 * x + y
        local, sends, recvs = [], [], []
        for a in range(n):
            local.append(pltpu.make_async_copy(ins[a].at[myq], outs[a].at[myq, l], lsem.at[a]))
            for j, (cx, cy) in enumerate(chips):
                q = 2 * cx + cy
                sends.append(_remote(ins[a].at[q], outs[a].at[myq, l], ssem, rsem, 3 * a + j, (cx, cy, c)))
                recvs.append(_remote(ins[a].at[q], outs[a].at[q, l], ssem, rsem, 3 * a + j, (cx, cy, c)))
        return dict(start=local + sends, recv=recvs, send=sends, local=local)

    outs = [jax.ShapeDtypeStruct((4, depth) + b.shape[1:], b.dtype) for b in bufs]
    return _Phase(list(bufs) + (list(accs) if has else []), outs, {n + a: a for a in range(n)} if has else {},
                  3 * n, n, build)


def _grad_chunks(n, g):
    if n == "w_pool":
        return g.reshape(g.shape[0], N_DEV, g.shape[1] // N_DEV, g.shape[2])
    if n in ("w_in", "w_up"):
        return g[None]
    return g.reshape(1, N_DEV, -1, g.shape[-1])


class _ReduceScatter:
    def __init__(self, depth, cidx):
        self.depth, self.cidx, self.acc, self.count = depth, cidx, {}, 0
        self.small_gathered = None
        self.presummed = None

    def pair(self, names, grads):
        return _pair_phase([_grad_chunks(n, grads[n]) for n in names])

    def sums(self, names, grads, phase):
        out = []
        for n, r1 in zip(names, phase.results):
            out.append(_pair_sum(_grad_chunks(n, grads[n]), r1, self.cidx, name="rs_sum_%d" % self.count))
            self.count += 1
        return out

    def chip(self, names, bufs, l):
        accs = [self.acc[n] for n in names] if names[0] in self.acc else None
        return _chip_phase(bufs, accs, l, self.depth)

    def done(self, names, phase):
        for n, r in zip(names, phase.results):
            self.acc[n] = r


def _rest_views(fulls, d):
    a_out, b_out, pool, o, up, down = fulls
    grp = d // len(POOL_WINDOWS)
    return dict(w_a_out=a_out.reshape(d, d), w_b_out=b_out.reshape(d, d), w_o=o.reshape(d, d),
                w_pool=pool.reshape(len(POOL_WINDOWS), grp, grp), w_up8=up[0],
                wd4=down.reshape(N_DEV // 2, -1, d))


class _GatherPlan:
    def __init__(self):
        self.jobs, self.part, self.full = [], {}, {}

    def add(self, key, shard, first, second, rows=None, onto=None):
        self.jobs.append((key, shard, first, second, rows, onto))

    def phases(self, name):
        j1 = [j for j in self.jobs if j[2] == name]
        j2 = [j for j in self.jobs if j[3] == name]
        tagged = []
        if j1:
            onto = [self.part[j[5]] if j[5] else None for j in j1]
            tagged.append((self.part, j1, _gather_phase1([j[1] for j in j1], [j[4] for j in j1], onto)))
        if j2:
            tagged.append((self.full, j2, _gather_phase2([self.part[j[0]] for j in j2])))
        return tagged

    @staticmethod
    def collect(tagged):
        for store, jobs, phase in tagged:
            for j, r in zip(jobs, phase.results):
                store[j[0]] = r


def _layer_fwd(x, w, alpha, tag, plan=None):
    d = x.shape[1]
    grp = d // len(POOL_WINDOWS)

    def carried(kernel, *args, name, **kw):
        tagged = plan.phases(name) if plan else []
        out = kernel(*args, name=name, phases=[t[2] for t in tagged], **kw)
        _GatherPlan.collect(tagged)
        return out

    def weight(n, shape):
        return plan.full[n + tag].reshape(shape) if plan else w[n]

    z, h = carried(_mod_matmul, x, w["mod1"], w["w_in8"], w["b_in8"], flat_out=True, name="in_proj" + tag)
    ua, ub, dd = carried(_mix_fwd, z, w["conv_a"], w["lnv"], w["wm"], w["bias_full"], name="mix_fwd" + tag)
    w["w_a_out"], w["w_b_out"], w["w_o"] = (weight(n, (d, d)) for n in ("w_a_out", "w_b_out", "w_o"))
    w["w_pool"] = weight("w_pool", (len(POOL_WINDOWS), grp, grp))
    w["w_up8"] = weight("w_up8", (N_DEV, d, -1))
    ya = _mm_rows(ua, w["w_a_out"], dn=NN, name="a_out" + tag, out_dtype=ACT_DTYPE)
    yb = _mm_rows(ub, w["w_b_out"], dn=NN, name="b_out" + tag, out_dtype=ACT_DTYPE)
    ycp = _pool_proj(dd, w["w_pool"], dn=NN, name="pool_proj" + tag, out_dtype=ACT_DTYPE)
    merged = _merge(z, ya, yb, ycp, w["pool_scale"], name="merge" + tag)
    o = _mm_rows(merged, w["w_o"], dn=NN, name="o_proj" + tag)
    x1 = _resid_ln(x, o, w["ln1"], alpha, name="ln1" + tag)
    up8, h2 = carried(_mod_matmul, x1, w["mod2"], w["w_up8"], w["b_up8"], flat_out=False, name="up_proj" + tag)
    up4 = up8.reshape((2, up8.shape[0] // 2) + up8.shape[1:])
    f4 = carried(_ffn_fwd, up4, w["cw"], w["cb"], name="ffn_fwd" + tag)
    w["wd4"] = weight("wd4", (N_DEV // 2, -1, d))
    y2 = carried(_down_proj, f4, w["wd4"], name="down_proj" + tag)
    x2 = _resid_ln(x1, y2, w["ln2"], alpha, name="ln2" + tag)
    saved = dict(x=x, z=z, h=h, ua=ua, ub=ub, dd=dd, ya=ya, yb=yb, ycp=ycp, merged=merged, o=o, x1=x1,
                 up4=up4, h2=h2, f4=f4, y2=y2)
    return x2, saved


def _layer_bwd(dpart, dh_above, xmod_above, m_above, w, s, alpha, tag, l=0, above=None, rs=None, upper_reds=()):
    first, rest = ("w_in",), REST
    ph = lambda p: [p] if p is not None else ()
    pre = rs.presummed if rs is not None else None
    r1a = rs.pair(first, above) if above and not pre else None
    dy2, dx1p, red2 = _resid_ln_bwd(dpart, dh_above, xmod_above, m_above, s["x1"], s["y2"], w["ln2"], alpha,
                                    name="ln2_bwd" + tag, phases=ph(r1a))
    r1b = rs.pair(rest, above) if above and not pre else None
    df4 = _down_bwd(dy2, w["wd4"], name="down_bwd" + tag, phases=ph(r1b))
    gw_down4 = _tn_shards_lhs(s["f4"], dy2, name="gw_down" + tag)
    r3a = r3b = r3c = None
    if above:
        if pre:
            bufs, rs.presummed = pre, None
        else:
            bufs = dict(zip(first + rest, rs.sums(first, above, r1a) + rs.sums(rest, above, r1b)))
        light = tuple(n for n in rest if n != "w_up")
        r3a = rs.chip(first, [bufs[n] for n in first], l + 1)
    dup4, redf = _ffn_bwd(s["up4"], df4, w["cw"], w["cb"], name="ffn_bwd" + tag, phases=ph(r3a))
    dup8 = dup4.reshape((dup4.shape[0] * dup4.shape[1],) + dup4.shape[2:])
    if above:
        rs.done(first, r3a)
        r3b = rs.chip(("w_up",), [bufs["w_up"]], l + 1)
    gw_up8 = _tn_shards_lhs(dup8, s["h2"], name="gw_up" + tag, phases=ph(r3b))
    if above:
        rs.done(("w_up",), r3b)
        r3c = rs.chip(light, [bufs[n] for n in light], l + 1)
    own = rs is not None and l == 0
    big = dict(w_up=gw_up8, w_down=gw_down4)
    early = ("w_down", "w_up")
    o1 = rs.pair(early, big) if own else None
    dh2 = _nt_shards(dup8, w["w_up8"], name="up_bwd" + tag, phases=list(ph(r3c)) + list(ph(o1)))
    if above:
        rs.done(light, r3c)
    if own:
        sb_o = rs.sums(early, big, o1)
    do, dxp, red1 = _resid_ln_bwd(dx1p, dh2, s["x1"], w["mod2"][0:1], s["x"], s["o"], w["ln1"], alpha,
                                  name="ln1_bwd" + tag)
    dm = _mm_rows(do, w["w_o"], dn=NT, name="o_bwd" + tag, out_dtype=ACT_DTYPE)
    big["w_o"] = _mm_tn(s["merged"], do, name="gw_o" + tag)
    dya, dyb, dyc, dz, redg = _gate_bwd(dm, s["z"], s["ya"], s["yb"], s["ycp"], w["pool_scale"], name="gate_bwd" + tag)
    dua = _mm_rows(dya, w["w_a_out"], dn=NT, name="a_out_bwd" + tag, out_dtype=ACT_DTYPE)
    dub = _mm_rows(dyb, w["w_b_out"], dn=NT, name="b_out_bwd" + tag, out_dtype=ACT_DTYPE)
    ddd = _pool_proj(dyc, w["w_pool"], dn=NT, name="pool_bwd" + tag, out_dtype=ACT_DTYPE)
    big["w_a_out"] = _mm_tn(s["ua"], dya, name="gw_a_out" + tag)
    big["w_b_out"] = _mm_tn(s["ub"], dyb, name="gw_b_out" + tag)
    big["w_pool"] = _tn_pool(s["dd"], dyc, w["w_pool"].shape[0], name="gw_pool" + tag)
    o3 = rs.chip(early, sb_o, l) if own else None
    dz, redm, dws, dbs = _mix_bwd(dz, dua, dub, ddd, s["z"], w["conv_a"], w["lnv"], w["wm"], w["wmt"],
                                  w["bias_full"], w["mask"], name="mix_bwd" + tag, phases=ph(o3))
    reds = dict(red2=red2, redf=redf, red1=red1, redg=redg, redm=redm, dws=dws, dbs=dbs)
    mid = ("w_o", "w_a_out", "w_b_out", "w_pool")
    o1b = sg1 = None
    if own:
        rs.done(early, o3)
        o1b = rs.pair(mid, big)
        sg1 = _gather_phase1([_small_payload([reds] + list(upper_reds))])
    big["w_in"] = _tn_cols_rhs(s["h"], dz, w["w_in8"].shape[0], name="gw_in" + tag,
                               phases=[o1b, sg1] if own else ())
    pending = None
    if own:
        sb_m = rs.sums(mid, big, o1b)
        o1c, o3b, sg2 = rs.pair(first, big), rs.chip(mid, sb_m, l), _gather_phase2(sg1.results)
        split_tm = 1024
        n_tiles = dz.shape[0] // _row_tile(dz.shape[0], split_tm)
        dh = _nt_cols(dz, w["w_in8"], name="in_bwd" + tag + "_a", phases=[o1c, o3b, sg2], tm=split_tm, tiles=(0, 1))
        rs.done(mid, o3b)
        rs.small_gathered = sg2.results[0]
        o3c = rs.chip(first, rs.sums(first, big, o1c), l)
        if n_tiles > 1:
            dh = _nt_cols(dz, w["w_in8"], name="in_bwd" + tag + "_b", phases=[o3c], tm=split_tm,
                          tiles=(1, n_tiles - 1), into=dh)
            rs.done(first, o3c)
        else:
            pending = (first, o3c)
    elif rs is not None:
        pa, pb = rs.pair(first, big), rs.pair(rest, big)
        dh = _nt_cols(dz, w["w_in8"], name="in_bwd" + tag, phases=[pa, pb])
        rs.presummed = dict(zip(first + rest, rs.sums(first, big, pa) + rs.sums(rest, big, pb)))
    else:
        dh = _nt_cols(dz, w["w_in8"], name="in_bwd" + tag)
    return dxp, dh, big, reds, pending


def _local_step(x, tgt, ws, alpha, plan=None, rs=None):
    depth = len(ws)
    saved = []
    y = x
    for l in range(depth):
        if plan and l > 0:
            ws[l]["w_in8"] = plan.full["w_in8_l%d" % l][0]
        y, s = _layer_fwd(y, ws[l], alpha, "_l%d" % l, plan)
        saved.append(s)
    dpart, loss_blk = _loss_grad(y, tgt, name="loss_grad")
    dh = xmod = mvec = above = pending = None
    bigs, reds = [None] * depth, [None] * depth
    for l in reversed(range(depth)):
        dpart, dh, bigs[l], reds[l], pending = _layer_bwd(dpart, dh, xmod, mvec, ws[l], saved[l], alpha, "_l%d" % l,
                                                          l, above if rs else None, rs, reds[l + 1:])
        xmod, mvec, above = saved[l]["x"], ws[l]["mod1"][0:1], bigs[l]
    grad_x, red0 = _resid_ln_bwd(dpart, dh, xmod, mvec, None, None, None, alpha, name="in_bwd_tail",
                                 phases=[pending[1]] if pending else ())
    if pending:
        rs.done(*pending)
    d_ada = []
    for l in range(depth):
        below = red0 if l == 0 else reds[l - 1]["red2"]
        r1, r2 = reds[l]["red1"], reds[l]["red2"]
        d_ada.append(jnp.stack([below[1], below[0], r1[4], r1[1], r1[0], r2[4]]))
    return loss_blk, grad_x, bigs, reds, jnp.stack(d_ada)


def _small_grads(r):
    redm, redg, redf = r["redm"], r["redg"], r["redf"]
    ng = r["dws"].shape[0]
    return dict(
        b_in=jnp.concatenate([redm[0:6], redg[1:4]], axis=0).reshape(-1),
        conv_a=redm[6:9], ln_v_g=redm[9], ln_v_b=redm[10],
        w_spatial=r["dws"], b_spatial=r["dbs"][:, :ng].T,
        pool_scale=redg[0], ln1_g=r["red1"][2], ln1_b=r["red1"][3],
        b_up=jnp.concatenate([redf[:, 4, :].reshape(-1), redf[:, 5, :].reshape(-1)]),
        conv_ffn=jnp.transpose(redf[:, 0:3, :], (1, 0, 2)).reshape(3, -1), conv_ffn_b=redf[:, 3, :].reshape(-1),
        ln2_g=r["red2"][2], ln2_b=r["red2"][3])


def _small_payload(reds):
    smalls = [_small_grads(r) for r in reds]
    order = SMALL_REPLICATED + SMALL_SHARDED
    flat = jnp.concatenate([smalls[l][n].reshape(-1) for n in order for l in range(len(reds))])
    return _as_rows(flat)[None]


def _layer_weights(l, ada, conv_a, conv_ffn, p):
    sh1, sc1, gt1, sh2, sc2, gt2 = (ada[l, k][None, :] for k in range(6))
    nb = N_DEV
    fs = p["b_up"].shape[1] // nb
    nj = nb // 2
    pos = jnp.arange(GMLP_BLOCK)
    allowed = (pos[None, :] // CHUNK) <= (pos[:, None] // CHUNK)
    wmask = jnp.where(allowed[None], p["w_spatial"][l], 0.0)
    return dict(
        mod1=jnp.concatenate([1.0 + sc1, sh1]), mod2=jnp.concatenate([1.0 + sc2, sh2]),
        ln1=jnp.concatenate([gt1, p["ln1_g"][l][None], p["ln1_b"][l][None]]),
        ln2=jnp.concatenate([gt2, p["ln2_g"][l][None], p["ln2_b"][l][None]]),
        b_in8=p["b_in"][l].reshape(N_DEV, 1, -1), b_up8=p["b_up"][l].reshape(nb, 1, fs),
        conv_a=conv_a[l], lnv=jnp.stack([p["ln_v_g"][l], p["ln_v_b"][l]]),
        wm=wmask.astype(BF), wmt=jnp.transpose(wmask, (0, 2, 1)).astype(BF),
        bias_full=jnp.repeat(p["b_spatial"][l].T, GMLP_BLOCK, axis=1), mask=allowed.astype(F32),
        pool_scale=p["pool_scale"][l][None],
        cw=jnp.transpose(conv_ffn[l].reshape(3, nj, fs), (1, 0, 2)), cb=p["conv_ffn_b"][l].reshape(nj, 1, fs))


ANY = pl.BlockSpec(memory_space=pl.ANY)


def _place():
    x, y, c = lax.axis_index("x"), lax.axis_index("y"), lax.axis_index("c")
    chips = [(1 - x, y), (x, 1 - y), (1 - x, 1 - y)]
    return x, y, c, chips


def _allgather_vmem(xs, *, name):
    r, cdim = xs.shape

    def body(x_ref, out_ref, send_sems, recv_sems, local_sem):
        x, y, c, chips = _place()
        me, sibling = (x, y, c), (x, y, 1 - c)

        def rows(px, py, pc):
            return out_ref.at[pl.ds((4 * px + 2 * py + pc) * r, r), :]

        def copy(k, block, to, src=None):
            return pltpu.make_async_remote_copy(
                src_ref=rows(*block) if src is None else src, dst_ref=rows(*block),
                send_sem=send_sems.at[k], recv_sem=recv_sems.at[k], device_id=to, device_id_type=MESH)

        mine = pltpu.make_async_copy(x_ref, rows(*me), local_sem)
        mine.start()
        first = [copy(0, me, sibling, src=x_ref)]
        first += [copy(1 + j, me, (*chip, c), src=x_ref) for j, chip in enumerate(chips)]
        for cp in first:
            cp.start()
        passed = [copy(4 + j, (*chip, c), sibling) for j, chip in enumerate(chips)]
        for j, chip in enumerate(chips):
            copy(1 + j, (*chip, c), me).wait_recv()
            passed[j].start()
        copy(0, sibling, me).wait_recv()
        for j, chip in enumerate(chips):
            copy(4 + j, (*chip, 1 - c), me).wait_recv()
        for cp in first + passed:
            cp.wait_send()
        mine.wait()

    return pl.pallas_call(
        body, name=name, out_shape=jax.ShapeDtypeStruct((N_DEV * r, cdim), xs.dtype),
        in_specs=[pl.BlockSpec(memory_space=pltpu.VMEM)], out_specs=pl.BlockSpec(memory_space=pltpu.VMEM),
        scratch_shapes=[pltpu.SemaphoreType.DMA((7,)), pltpu.SemaphoreType.DMA((7,)), pltpu.SemaphoreType.DMA],
        compiler_params=_params(),
    )(xs)


def _gather_weights(shards, *, name):
    n = len(shards)

    def body(*refs):
        ins, outs = refs[:n], refs[n:2 * n]
        send_sems, recv_sems, local_sems = refs[2 * n:]
        x, y, c, chips = _place()
        me, sibling = (x, y, c), (x, y, 1 - c)

        def slot(a, px, py, pc):
            return outs[a].at[:, 4 * px + 2 * py + pc]

        def copy(a, k, block, to, src=None):
            return pltpu.make_async_remote_copy(
                src_ref=slot(a, *block) if src is None else src, dst_ref=slot(a, *block),
                send_sem=send_sems.at[7 * a + k], recv_sem=recv_sems.at[7 * a + k], device_id=to,
                device_id_type=MESH)

        mine = [pltpu.make_async_copy(ins[a], slot(a, *me), local_sems.at[a]) for a in range(n)]
        for cp in mine:
            cp.start()
        first = []
        for j, chip in enumerate(chips):
            first += [copy(a, 1 + j, me, (*chip, c), src=ins[a]) for a in range(n)]
        first += [copy(a, 0, me, sibling, src=ins[a]) for a in range(n)]
        for cp in first:
            cp.start()
        passed = []
        for j, chip in enumerate(chips):
            for a in range(n):
                copy(a, 1 + j, (*chip, c), me).wait_recv()
                fwd = copy(a, 4 + j, (*chip, c), sibling)
                fwd.start()
                passed.append(fwd)
        for a in range(n):
            copy(a, 0, sibling, me).wait_recv()
        for j, chip in enumerate(chips):
            for a in range(n):
                copy(a, 4 + j, (*chip, 1 - c), me).wait_recv()
        for cp in first + passed:
            cp.wait_send()
        for cp in mine:
            cp.wait()

    out_shape = [jax.ShapeDtypeStruct((s.shape[0], N_DEV) + s.shape[1:], s.dtype) for s in shards]
    return pl.pallas_call(
        body, name=name, out_shape=out_shape, in_specs=[ANY] * n, out_specs=[ANY] * n,
        scratch_shapes=[pltpu.SemaphoreType.DMA((7 * n,)), pltpu.SemaphoreType.DMA((7 * n,)),
                        pltpu.SemaphoreType.DMA((n,))],
        compiler_params=_params(),
    )(*shards)


def _pick_tile(r, cap):
    best = None
    for t in range(8, min(r, cap) + 1, 8):
        if r % t == 0:
            best = t
    return best if best is not None else r


def _pair_sum(g, r1, cidx, *, name):
    p, _, r, cdim = g.shape
    tr = _pick_tile(r, 1024)

    def body(c_ref, g_ref, r_ref, o_ref):
        del c_ref
        o_ref[...] = (g_ref[...].astype(F32) + r_ref[...].astype(F32)).astype(BF)

    grid_spec = pltpu.PrefetchScalarGridSpec(
        num_scalar_prefetch=1, grid=(4, r // tr),
        in_specs=[pl.BlockSpec((p, None, tr, cdim), lambda q, i, c: (0, 2 * q + c[0], i, 0)),
                  pl.BlockSpec((None, p, tr, cdim), lambda q, i, c: (q, 0, i, 0))],
        out_specs=pl.BlockSpec((None, p, tr, cdim), lambda q, i, c: (q, 0, i, 0)))
    return pl.pallas_call(
        body, name=name, grid_spec=grid_spec, out_shape=jax.ShapeDtypeStruct((4, p, r, cdim), BF),
        compiler_params=_params(("arbitrary", "arbitrary")),
    )(cidx, g, r1)


def _ada_fwd(c_all, w_ada, *, name):
    depth, d, ns = w_ada.shape
    nb = c_all.shape[0]

    def body(c_ref, w_ref, o_ref):
        cv = c_ref[...]
        act = cv * jax.nn.sigmoid(cv)
        o_ref[...] = jnp.dot(act, w_ref[...], preferred_element_type=F32, precision=lax.Precision.HIGHEST)

    return pl.pallas_call(
        body, name=name, grid=(depth,),
        in_specs=[pl.BlockSpec((nb, d), lambda l: (0, 0)), pl.BlockSpec((None, d, ns), lambda l: (l, 0, 0))],
        out_specs=pl.BlockSpec((None, nb, ns), lambda l: (l, 0, 0)),
        out_shape=jax.ShapeDtypeStruct((depth, nb, ns), F32), compiler_params=_params(("parallel",)),
    )(c_all, w_ada)


def _ada_bwd(ct, dmine, dall, *, name):
    depth, nb, ns = dmine.shape
    d = ct.shape[0]

    def body(ct_ref, dm_ref, da_ref, gw_ref, gb_ref):
        cv = ct_ref[...]
        act = cv * jax.nn.sigmoid(cv)
        gw_ref[...] = jnp.dot(act, dm_ref[...], preferred_element_type=F32, precision=lax.Precision.HIGHEST)
        s = da_ref[0]
        for b in range(1, nb):
            s = s + da_ref[b]
        gb_ref[...] = s

    return pl.pallas_call(
        body, name=name, grid=(depth,),
        in_specs=[pl.BlockSpec((d, nb), lambda l: (0, 0)), pl.BlockSpec((None, nb, ns), lambda l: (l, 0, 0)),
                  pl.BlockSpec(dall.shape, lambda l: (0, 0, 0))],
        out_specs=[pl.BlockSpec((None, d, ns), lambda l: (l, 0, 0)), pl.BlockSpec(dall.shape[1:], lambda l: (0, 0))],
        out_shape=[jax.ShapeDtypeStruct((depth, d, ns), F32), jax.ShapeDtypeStruct(dall.shape[1:], F32)],
        compiler_params=_params(("arbitrary",)),
    )(ct, dmine, dall)


def _sum_parts(parts, *, name):
    p, r, cdim = parts.shape
    tr = _pick_tile(r, 512)

    def body(p_ref, o_ref):
        s = p_ref[0]
        for k in range(1, p):
            s = s + p_ref[k]
        o_ref[...] = s

    return pl.pallas_call(
        body, name=name, grid=(r // tr,),
        in_specs=[pl.BlockSpec((p, tr, cdim), lambda i: (0, i, 0))], out_specs=pl.BlockSpec((tr, cdim), lambda i: (i, 0)),
        out_shape=jax.ShapeDtypeStruct((r, cdim), F32), compiler_params=_params(("parallel",)),
    )(parts)


def _adamw(parts, w, m, v, *, name):
    p, depth, r, cdim = parts.shape
    tr = _pick_tile(r, 256)

    def body(p_ref, w_ref, m_ref, v_ref, g_out, d_out, m_out, v_out):
        g = p_ref[0].astype(F32)
        for k in range(1, p):
            g = g + p_ref[k].astype(F32)
        m2 = ADAM_B1 * m_ref[...] + (1.0 - ADAM_B1) * g
        v2 = ADAM_B2 * v_ref[...] + (1.0 - ADAM_B2) * (g * g)
        m_hat = m2 / (1.0 - ADAM_B1 ** ADAM_STEP)
        v_hat = v2 / (1.0 - ADAM_B2 ** ADAM_STEP)
        g_out[...] = g
        d_out[...] = -ADAM_LR * (m_hat / (jnp.sqrt(v_hat) + ADAM_EPS) + ADAM_WD * w_ref[...])
        m_out[...] = m2
        v_out[...] = v2

    blk = pl.BlockSpec((None, tr, cdim), lambda l, i: (l, i, 0))
    out = jax.ShapeDtypeStruct((depth, r, cdim), F32)
    return pl.pallas_call(
        body, name=name, grid=(depth, r // tr),
        in_specs=[pl.BlockSpec((p, None, tr, cdim), lambda l, i: (0, l, i, 0)), blk, blk, blk],
        out_specs=[blk, blk, blk, blk], out_shape=[out, out, out, out],
        compiler_params=_params(("parallel", "parallel")),
    )(parts, w, m, v)


BIG = ("w_in", "w_a_out", "w_b_out", "w_pool", "w_o", "w_up", "w_down")
SMALL_REPLICATED = ("b_in", "ln_v_g", "ln_v_b", "w_spatial", "b_spatial", "pool_scale", "ln1_g", "ln1_b", "b_up",
                    "conv_ffn_b", "ln2_g", "ln2_b")
SMALL_SHARDED = ("conv_a", "conv_ffn")
WEIGHTS = ("w_ada", "b_ada", "w_in", "b_in", "conv_a", "w_a_out", "ln_v_g", "ln_v_b", "w_spatial", "b_spatial",
           "w_b_out", "w_pool", "pool_scale", "w_o", "ln1_g", "ln1_b", "w_up", "b_up", "conv_ffn", "conv_ffn_b",
           "w_down", "ln2_g", "ln2_b")
LANES = 128


def _as_rows(flat, mult=8):
    n = flat.shape[0]
    pad = (-n) % (LANES * mult)
    if pad:
        flat = jnp.concatenate([flat, jnp.zeros((pad,), flat.dtype)])
    return flat.reshape(-1, LANES)


def _shard3(a):
    return a.reshape((-1,) + a.shape[-2:])


def kernel(x, c, w_ada, b_ada, w_in, b_in, conv_a, w_a_out, ln_v_g, ln_v_b, w_spatial, b_spatial, w_b_out, w_pool, pool_scale, w_o, ln1_g, ln1_b, w_up, b_up, conv_ffn, conv_ffn_b, w_down, ln2_g, ln2_b, loss_target, m_w_ada, m_b_ada, m_w_in, m_b_in, m_conv_a, m_w_a_out, m_ln_v_g, m_ln_v_b, m_w_spatial, m_b_spatial, m_w_b_out, m_w_pool, m_pool_scale, m_w_o, m_ln1_g, m_ln1_b, m_w_up, m_b_up, m_conv_ffn, m_conv_ffn_b, m_w_down, m_ln2_g, m_ln2_b, v_w_ada, v_b_ada, v_w_in, v_b_in, v_conv_a, v_w_a_out, v_ln_v_g, v_ln_v_b, v_w_spatial, v_b_spatial, v_w_b_out, v_w_pool, v_pool_scale, v_w_o, v_ln1_g, v_ln1_b, v_w_up, v_b_up, v_conv_ffn, v_conv_ffn_b, v_w_down, v_ln2_g, v_ln2_b):
    p = dict(locals())
    depth, d = w_in.shape[0], w_in.shape[1]
    alpha = (2 * depth) ** 0.25
    me = 4 * lax.axis_index("x") + 2 * lax.axis_index("y") + lax.axis_index("c")
    cidx = lax.axis_index("c").astype(jnp.int32).reshape(1)

    n_ca, n_cf = conv_a.size, conv_ffn.size
    packed = _as_rows(jnp.concatenate([c.reshape(-1), conv_a.reshape(-1), conv_ffn.reshape(-1)]))
    got = _allgather_vmem(packed, name="gather_cond").reshape(N_DEV, -1)
    c_all = got[:, :d]
    ct = c_all.T
    conv_a_full = jnp.transpose(got[:, d:d + n_ca].reshape((N_DEV,) + conv_a.shape), (1, 2, 0, 3)).reshape(depth, 3, -1)
    conv_ffn_full = jnp.transpose(got[:, d + n_ca:d + n_ca + n_cf].reshape((N_DEV,) + conv_ffn.shape),
                                  (1, 2, 0, 3)).reshape(depth, 3, -1)

    ns_ada = w_ada.shape[2]
    ada_part = _ada_fwd(c_all, w_ada, name="ada_fwd")
    ada_all = _allgather_vmem(_as_rows(ada_part.reshape(-1)), name="gather_ada")
    ada_all = ada_all.reshape(N_DEV, depth, N_DEV, ns_ada)
    ada_mine = lax.dynamic_index_in_dim(ada_all, me, axis=2, keepdims=False)
    ada = jnp.transpose(ada_mine, (1, 0, 2)).reshape(depth, -1) + b_ada
    ada = ada.reshape(depth, 6, d)

    shards = [{n: _shard3(p[n][l].astype(BF)) for n in BIG} for l in range(depth)]
    ws = [_layer_weights(l, ada, conv_a_full, conv_ffn_full, p) for l in range(depth)]
    ws[0]["w_in8"] = _gather_weights([shards[0]["w_in"]], name="gather_w_in0")[0][0]
    plan = _GatherPlan()
    four = ("w_a_out", "w_b_out", "w_pool", "w_o")
    for l in range(depth):
        t, prev, sh = "_l%d" % l, "_l%d" % (l - 1), shards[l]
        if l == 0:
            for n in four:
                plan.add(n + t, sh[n], "in_proj" + t, "mix_fwd" + t)
            plan.add("w_up8" + t, sh["w_up"], "in_proj" + t, "mix_fwd" + t)
            plan.add("wd4" + t, sh["w_down"], "mix_fwd" + t, "up_proj" + t)
        else:
            half = sh["w_in"].shape[1] // 2
            plan.add("w_in8_top" + t, sh["w_in"], "up_proj" + prev, None, rows=(0, half))
            plan.add("w_in8" + t, sh["w_in"], "ffn_fwd" + prev, "down_proj" + prev, rows=(half, half),
                     onto="w_in8_top" + t)
            for n in four:
                plan.add(n + t, sh[n], "down_proj" + prev, "in_proj" + t)
            plan.add("w_up8" + t, sh["w_up"], "in_proj" + t, "mix_fwd" + t)
            plan.add("wd4" + t, sh["w_down"], "in_proj" + t, "mix_fwd" + t)

    rs = _ReduceScatter(depth, cidx)
    loss_blk, grad_x, bigs, reds, d_ada = _local_step(x[0], loss_target[0], ws, alpha, plan, rs)
    loss = lax.psum(loss_blk[0, 0], ("x", "y", "c"))

    dada_all = _allgather_vmem(_as_rows(d_ada.reshape(-1)), name="gather_dada")
    dada_all = dada_all.reshape(N_DEV, -1, LANES)
    dflat = dada_all.reshape(N_DEV, depth, 6 * d)
    dmine = lax.dynamic_slice_in_dim(dflat, me * ns_ada, ns_ada, axis=2)
    gw_ada, gb_rows = _ada_bwd(ct, jnp.transpose(dmine, (1, 0, 2)), dada_all, name="ada_bwd")
    gb_ada = gb_rows.reshape(-1)[:depth * 6 * d].reshape(depth, 6 * d)

    out = {}
    for n in BIG:
        parts = rs.acc[n]
        view = (lambda a: jnp.swapaxes(a, 1, 2)) if n == "w_up" else (lambda a: a)
        shard_shape = view(p[n]).shape
        w3 = view(p[n]).reshape(depth, -1, shard_shape[-1])
        parts4 = parts.reshape((4,) + w3.shape)
        res = _adamw(parts4, w3, view(p["m_" + n]).reshape(w3.shape), view(p["v_" + n]).reshape(w3.shape),
                     name="adamw_" + n)
        out[n] = [view(r.reshape(shard_shape)) for r in res]
    out["w_ada"] = _adamw(gw_ada[None], w_ada, m_w_ada, v_w_ada, name="adamw_w_ada")

    order = SMALL_REPLICATED + SMALL_SHARDED
    n_rep = sum(p[n].size for n in SMALL_REPLICATED)
    n_pay = n_rep + N_DEV * (conv_a.size + conv_ffn.size)
    gsum = _sum_parts(rs.small_gathered.reshape(N_DEV, -1, LANES), name="sum_small").reshape(-1)[:n_pay]
    ga_full = gsum[n_rep:n_rep + depth * 3 * d].reshape(depth, 3, d)
    gf_full = gsum[n_rep + depth * 3 * d:].reshape(depth, 3, -1)
    ca_w, cf_w = conv_a.shape[2], conv_ffn.shape[2]
    g_ca = lax.dynamic_slice_in_dim(ga_full, me * ca_w, ca_w, axis=2)
    g_cf = lax.dynamic_slice_in_dim(gf_full, me * cf_w, cf_w, axis=2)
    names = ("b_ada",) + order
    gflat = _as_rows(jnp.concatenate([gb_ada.reshape(-1), gsum[:n_rep], g_ca.reshape(-1), g_cf.reshape(-1)]))
    pack = lambda pre: _as_rows(jnp.concatenate([p[pre + n].reshape(-1) for n in names]))
    res = _adamw(gflat[None, None], pack("")[None], pack("m_")[None], pack("v_")[None], name="adamw_small")
    off = 0
    for n in names:
        size = p[n].size
        out[n] = [r.reshape(-1)[off:off + size].reshape(p[n].shape) for r in res]
        off += size

    return (loss, grad_x[None]) + tuple(out[n][k] for k in range(4) for n in WEIGHTS)
```
